```python
import math
import jax, jax.numpy as jnp
from jax import lax
import numpy as np

D_MODEL = 1024
BATCH = 8
SEQ = 4096
DEPTH = 4

N_MEM = 256
MEM_HEADS = 4
MEM_HEAD_DIM = 64
MEM_WIDTH = MEM_HEADS * MEM_HEAD_DIM
SSM_WIDTH = D_MODEL // 2
MLA_WIDTH = D_MODEL // 2
MIX_WIDTH = SSM_WIDTH + MLA_WIDTH
SSM_GROUP = 16
SSM_GROUPS = SSM_WIDTH // SSM_GROUP
SSM_STATE = 64
MLA_HEADS = 8
QK_NOPE = 64
QK_ROPE = 32
QK_DIM = QK_NOPE + QK_ROPE
V_DIM = MLA_WIDTH // MLA_HEADS
Q_LORA = 256
KV_LORA = 128
ROPE_THETA = 10000.0
Q_BLOCK = 128
D_FF = 4 * D_MODEL
IN_COLS = SSM_WIDTH + Q_LORA + KV_LORA + QK_ROPE
EPS = 1e-6

kernel_name = 'hymba_s5_mla_memory_trunk'


def rms_norm(x, gain):
    xf = x.astype(jnp.float32)
    y = xf * lax.rsqrt(jnp.mean(xf * xf, axis=-1, keepdims=True) + EPS)
    return y.astype(x.dtype) * gain


def rope(x, positions):
    half = QK_ROPE // 2
    inv_freq = ROPE_THETA ** (-jnp.arange(half, dtype=jnp.float32) / half)
    ang = positions.astype(jnp.float32)[..., None] * inv_freq
    ang = ang.reshape(ang.shape[:2] + (1,) * (x.ndim - 3) + (half,))
    cos = jnp.cos(ang).astype(x.dtype)
    sin = jnp.sin(ang).astype(x.dtype)
    x1, x2 = x[..., :half], x[..., half:]
    return jnp.concatenate([x1 * cos - x2 * sin, x2 * cos + x1 * sin], axis=-1)


def s5_mixer(u, lam_re, lam_im, log_step, b_re, b_im, c_re, c_im, d, w_glu, b_glu):
    bsz, seq, _ = u.shape
    f32 = jnp.float32
    uf = u.astype(f32).reshape(bsz, seq, SSM_GROUPS, SSM_GROUP)
    lam = lax.complex(lam_re.astype(f32), lam_im.astype(f32))
    step = jnp.exp(log_step.astype(f32))[:, None]
    a_bar = jnp.exp(lam * step)
    b = lax.complex(b_re.astype(f32), b_im.astype(f32))
    b_bar = ((a_bar - 1.0) / lam)[..., None] * b
    c = lax.complex(c_re.astype(f32), c_im.astype(f32))
    bu = jnp.einsum('gph,bsgh->bsgp', b_bar, uf.astype(jnp.complex64))
    a_seq = jnp.broadcast_to(a_bar, bu.shape)

    def combine(e1, e2):
        a1, s1 = e1
        a2, s2 = e2
        return a2 * a1, a2 * s1 + s2

    _, states = lax.associative_scan(combine, (a_seq, bu), axis=1)
    y = jnp.einsum('ghp,bsgp->bsgh', c, states).real + d.astype(f32).reshape(SSM_GROUPS, SSM_GROUP) * uf
    y = jax.nn.gelu(y.reshape(bsz, seq, SSM_WIDTH)).astype(u.dtype)
    return y * jax.nn.sigmoid(y @ w_glu + b_glu)


def causal_block_attention(q, k, v):
    bsz, seq, heads, dq = q.shape
    n_blocks = seq // Q_BLOCK
    scale = 1.0 / math.sqrt(dq)
    qb = q.reshape(bsz, n_blocks, Q_BLOCK, heads, dq).transpose(1, 0, 3, 2, 4)
    kt = k.transpose(0, 2, 1, 3)
    vt = v.transpose(0, 2, 1, 3)
    k_pos = jnp.arange(seq)

    def block(args):
        q_blk, blk = args
        s = jnp.einsum('bhqd,bhkd->bhqk', q_blk, kt).astype(jnp.float32) * scale
        q_pos = blk * Q_BLOCK + jnp.arange(Q_BLOCK)
        s = jnp.where(k_pos[None, :] <= q_pos[:, None], s, -jnp.inf)
        p = jax.nn.softmax(s, axis=-1).astype(vt.dtype)
        return jnp.einsum('bhqk,bhkd->bhqd', p, vt)

    o = lax.map(block, (qb, jnp.arange(n_blocks)))
    return o.transpose(1, 0, 3, 2, 4).reshape(bsz, seq, heads, v.shape[-1])


def mla_mixer(c_q, c_kv, k_rope, positions, q_norm, w_uq, kv_norm, w_ukv, q_gain, k_gain):
    bsz, seq, _ = c_q.shape
    q = (rms_norm(c_q, q_norm) @ w_uq).reshape(bsz, seq, MLA_HEADS, QK_DIM)
    kv = (rms_norm(c_kv, kv_norm) @ w_ukv).reshape(bsz, seq, MLA_HEADS, QK_NOPE + V_DIM)
    k_nope, v = kv[..., :QK_NOPE], kv[..., QK_NOPE:]
    k_pe = jnp.broadcast_to(k_rope[:, :, None, :], (bsz, seq, MLA_HEADS, QK_ROPE))
    k = jnp.concatenate([k_nope, k_pe], axis=-1)
    q = rms_norm(q, q_gain)
    k = rms_norm(k, k_gain)
    q = jnp.concatenate([q[..., :QK_NOPE], rope(q[..., QK_NOPE:], positions)], axis=-1)
    k = jnp.concatenate([k[..., :QK_NOPE], rope(k[..., QK_NOPE:], positions)], axis=-1)
    out = causal_block_attention(q, k, v)
    return out.reshape(bsz, seq, MLA_WIDTH)


def memory_cross_attention(h, mem_h, w_q, w_kv, q_gain, k_gain, w_o):
    bsz, seq, _ = h.shape
    n_mem = mem_h.shape[1]
    q = (h @ w_q).reshape(bsz, seq, MEM_HEADS, MEM_HEAD_DIM)
    kv = (mem_h @ w_kv).reshape(bsz, n_mem, MEM_HEADS, 2 * MEM_HEAD_DIM)
    k, v = kv[..., :MEM_HEAD_DIM], kv[..., MEM_HEAD_DIM:]
    q = rms_norm(q, q_gain)
    k = rms_norm(k, k_gain)
    s = jnp.einsum('bqhd,bkhd->bhqk', q, k).astype(jnp.float32) / math.sqrt(MEM_HEAD_DIM)
    p = jax.nn.softmax(s, axis=-1).astype(v.dtype)
    o = jnp.einsum('bhqk,bkhd->bqhd', p, v).reshape(bsz, seq, MEM_WIDTH)
    return o @ w_o


def _fwd_setup_inputs(seed: int = 0) -> dict:
    key = jax.random.key(seed)
    ks = jax.random.split(key, 40)
    f32 = jnp.float32

    def nrm(k, shape, scale):
        return jax.random.normal(k, shape, f32) * scale

    def gain(k, shape):
        return 1.0 + 0.01 * jax.random.normal(k, shape, f32)

    L = DEPTH
    G, P, H = SSM_GROUPS, SSM_STATE, SSM_GROUP
    lam_im = jnp.broadcast_to(jnp.pi * jnp.arange(P, dtype=f32), (L, G, P)) + 0.01 * jax.random.normal(ks[4], (L, G, P), f32)
    lam_re = -0.5 + 0.01 * jax.random.normal(ks[3], (L, G, P), f32)
    log_step = jax.random.uniform(ks[5], (L, G), f32, math.log(1e-3), math.log(1e-1))
    return {
        'x': jax.random.normal(ks[0], (BATCH, SEQ, D_MODEL), f32),
        'mem': jax.random.normal(ks[1], (BATCH, N_MEM, D_MODEL), f32),
        'positions': jnp.broadcast_to(jnp.arange(SEQ, dtype=jnp.int32), (BATCH, SEQ)),
        'norm_mix': gain(ks[2], (L, D_MODEL)),
        'w_in': nrm(ks[6], (L, D_MODEL, IN_COLS), D_MODEL ** -0.5),
        'ssm_lambda_re': lam_re,
        'ssm_lambda_im': lam_im,
        'ssm_log_step': log_step,
        'ssm_b_re': nrm(ks[7], (L, G, P, H), (2 * H) ** -0.5),
        'ssm_b_im': nrm(ks[8], (L, G, P, H), (2 * H) ** -0.5),
        'ssm_c_re': nrm(ks[9], (L, G, H, P), (2 * P) ** -0.5),
        'ssm_c_im': nrm(ks[10], (L, G, H, P), (2 * P) ** -0.5),
        'ssm_d': nrm(ks[11], (L, SSM_WIDTH), 1.0),
        'ssm_w_glu': nrm(ks[12], (L, SSM_WIDTH, SSM_WIDTH), SSM_WIDTH ** -0.5),
        'ssm_b_glu': nrm(ks[13], (L, SSM_WIDTH), 0.02),
        'mla_q_norm': gain(ks[14], (L, Q_LORA)),
        'mla_w_uq': nrm(ks[15], (L, Q_LORA, MLA_HEADS * QK_DIM), Q_LORA ** -0.5),
        'mla_kv_norm': gain(ks[16], (L, KV_LORA)),
        'mla_w_ukv': nrm(ks[17], (L, KV_LORA, MLA_HEADS * (QK_NOPE + V_DIM)), KV_LORA ** -0.5),
        'mla_q_gain': gain(ks[18], (L, QK_DIM)),
        'mla_k_gain': gain(ks[19], (L, QK_DIM)),
        'out_norm_ssm': gain(ks[20], (L, SSM_WIDTH)),
        'out_norm_mla': gain(ks[21], (L, MLA_WIDTH)),
        'w_out': nrm(ks[22], (L, MIX_WIDTH, D_MODEL), MIX_WIDTH ** -0.5),
        'norm_mem_q': gain(ks[23], (L, D_MODEL)),
        'norm_mem_kv': gain(ks[24], (L, D_MODEL)),
        'mem_w_q': nrm(ks[25], (L, D_MODEL, MEM_WIDTH), D_MODEL ** -0.5),
        'mem_w_kv': nrm(ks[26], (L, D_MODEL, 2 * MEM_WIDTH), D_MODEL ** -0.5),
        'mem_q_gain': gain(ks[27], (L, MEM_HEAD_DIM)),
        'mem_k_gain': gain(ks[28], (L, MEM_HEAD_DIM)),
        'mem_w_o': nrm(ks[29], (L, MEM_WIDTH, D_MODEL), MEM_WIDTH ** -0.5),
        'norm_mlp': gain(ks[30], (L, D_MODEL)),
        'mlp_w1': nrm(ks[31], (L, D_MODEL, D_FF), D_MODEL ** -0.5),
        'mlp_w2': nrm(ks[32], (L, D_FF, D_MODEL), D_FF ** -0.5),
    }


def _fwd_reference(x, mem, positions, norm_mix, w_in,
              ssm_lambda_re, ssm_lambda_im, ssm_log_step, ssm_b_re, ssm_b_im,
              ssm_c_re, ssm_c_im, ssm_d, ssm_w_glu, ssm_b_glu,
              mla_q_norm, mla_w_uq, mla_kv_norm, mla_w_ukv, mla_q_gain, mla_k_gain,
              out_norm_ssm, out_norm_mla, w_out,
              norm_mem_q, norm_mem_kv, mem_w_q, mem_w_kv, mem_q_gain, mem_k_gain, mem_w_o,
              norm_mlp, mlp_w1, mlp_w2):
    s1 = SSM_WIDTH
    s2 = s1 + Q_LORA
    s3 = s2 + KV_LORA
    for l in range(DEPTH):
        h = rms_norm(x, norm_mix[l])
        proj = h @ w_in[l]
        u, c_q, c_kv, k_rope = proj[..., :s1], proj[..., s1:s2], proj[..., s2:s3], proj[..., s3:]
        y_ssm = s5_mixer(u, ssm_lambda_re[l], ssm_lambda_im[l], ssm_log_step[l],
                         ssm_b_re[l], ssm_b_im[l], ssm_c_re[l], ssm_c_im[l],
                         ssm_d[l], ssm_w_glu[l], ssm_b_glu[l])
        y_mla = mla_mixer(c_q, c_kv, k_rope, positions, mla_q_norm[l], mla_w_uq[l],
                          mla_kv_norm[l], mla_w_ukv[l], mla_q_gain[l], mla_k_gain[l])
        y = jnp.concatenate([rms_norm(y_ssm, out_norm_ssm[l]), rms_norm(y_mla, out_norm_mla[l])], axis=-1)
        x = x + y @ w_out[l]
        x = x + memory_cross_attention(rms_norm(x, norm_mem_q[l]), rms_norm(mem, norm_mem_kv[l]),
                                       mem_w_q[l], mem_w_kv[l], mem_q_gain[l], mem_k_gain[l], mem_w_o[l])
        h = rms_norm(x, norm_mlp[l])
        x = x + jnp.square(jax.nn.relu(h @ mlp_w1[l])) @ mlp_w2[l]
    return x


import jax as _jax
import jax.numpy as _jnp

TWIN_FORMAT = 'train_step'
FWD_PARAMS = ['x', 'mem', 'positions', 'norm_mix', 'w_in', 'ssm_lambda_re', 'ssm_lambda_im', 'ssm_log_step', 'ssm_b_re', 'ssm_b_im', 'ssm_c_re', 'ssm_c_im', 'ssm_d', 'ssm_w_glu', 'ssm_b_glu', 'mla_q_norm', 'mla_w_uq', 'mla_kv_norm', 'mla_w_ukv', 'mla_q_gain', 'mla_k_gain', 'out_norm_ssm', 'out_norm_mla', 'w_out', 'norm_mem_q', 'norm_mem_kv', 'mem_w_q', 'mem_w_kv', 'mem_q_gain', 'mem_k_gain', 'mem_w_o', 'norm_mlp', 'mlp_w1', 'mlp_w2']
TWIN_WEIGHTS = ['norm_mix', 'w_in', 'ssm_lambda_re', 'ssm_lambda_im', 'ssm_log_step', 'ssm_b_re', 'ssm_b_im', 'ssm_c_re', 'ssm_c_im', 'ssm_d', 'ssm_w_glu', 'ssm_b_glu', 'mla_q_norm', 'mla_w_uq', 'mla_kv_norm', 'mla_w_ukv', 'mla_q_gain', 'mla_k_gain', 'out_norm_ssm', 'out_norm_mla', 'w_out', 'norm_mem_q', 'norm_mem_kv', 'mem_w_q', 'mem_w_kv', 'mem_q_gain', 'mem_k_gain', 'mem_w_o', 'norm_mlp', 'mlp_w1', 'mlp_w2']
TWIN_DIFF_INPUT = 'x'
TWIN_INPUTS = ['x', 'mem', 'positions', 'norm_mix', 'w_in', 'ssm_lambda_re', 'ssm_lambda_im', 'ssm_log_step', 'ssm_b_re', 'ssm_b_im', 'ssm_c_re', 'ssm_c_im', 'ssm_d', 'ssm_w_glu', 'ssm_b_glu', 'mla_q_norm', 'mla_w_uq', 'mla_kv_norm', 'mla_w_ukv', 'mla_q_gain', 'mla_k_gain', 'out_norm_ssm', 'out_norm_mla', 'w_out', 'norm_mem_q', 'norm_mem_kv', 'mem_w_q', 'mem_w_kv', 'mem_q_gain', 'mem_k_gain', 'mem_w_o', 'norm_mlp', 'mlp_w1', 'mlp_w2', 'loss_target', 'm_norm_mix', 'm_w_in', 'm_ssm_lambda_re', 'm_ssm_lambda_im', 'm_ssm_log_step', 'm_ssm_b_re', 'm_ssm_b_im', 'm_ssm_c_re', 'm_ssm_c_im', 'm_ssm_d', 'm_ssm_w_glu', 'm_ssm_b_glu', 'm_mla_q_norm', 'm_mla_w_uq', 'm_mla_kv_norm', 'm_mla_w_ukv', 'm_mla_q_gain', 'm_mla_k_gain', 'm_out_norm_ssm', 'm_out_norm_mla', 'm_w_out', 'm_norm_mem_q', 'm_norm_mem_kv', 'm_mem_w_q', 'm_mem_w_kv', 'm_mem_q_gain', 'm_mem_k_gain', 'm_mem_w_o', 'm_norm_mlp', 'm_mlp_w1', 'm_mlp_w2', 'v_norm_mix', 'v_w_in', 'v_ssm_lambda_re', 'v_ssm_lambda_im', 'v_ssm_log_step', 'v_ssm_b_re', 'v_ssm_b_im', 'v_ssm_c_re', 'v_ssm_c_im', 'v_ssm_d', 'v_ssm_w_glu', 'v_ssm_b_glu', 'v_mla_q_norm', 'v_mla_w_uq', 'v_mla_kv_norm', 'v_mla_w_ukv', 'v_mla_q_gain', 'v_mla_k_gain', 'v_out_norm_ssm', 'v_out_norm_mla', 'v_w_out', 'v_norm_mem_q', 'v_norm_mem_kv', 'v_mem_w_q', 'v_mem_w_kv', 'v_mem_q_gain', 'v_mem_k_gain', 'v_mem_w_o', 'v_norm_mlp', 'v_mlp_w1', 'v_mlp_w2']
TWIN_OUTPUTS = ['loss', 'grad_x', 'grad_norm_mix', 'grad_w_in', 'grad_ssm_lambda_re', 'grad_ssm_lambda_im', 'grad_ssm_log_step', 'grad_ssm_b_re', 'grad_ssm_b_im', 'grad_ssm_c_re', 'grad_ssm_c_im', 'grad_ssm_d', 'grad_ssm_w_glu', 'grad_ssm_b_glu', 'grad_mla_q_norm', 'grad_mla_w_uq', 'grad_mla_kv_norm', 'grad_mla_w_ukv', 'grad_mla_q_gain', 'grad_mla_k_gain', 'grad_out_norm_ssm', 'grad_out_norm_mla', 'grad_w_out', 'grad_norm_mem_q', 'grad_norm_mem_kv', 'grad_mem_w_q', 'grad_mem_w_kv', 'grad_mem_q_gain', 'grad_mem_k_gain', 'grad_mem_w_o', 'grad_norm_mlp', 'grad_mlp_w1', 'grad_mlp_w2', 'delta_norm_mix', 'delta_w_in', 'delta_ssm_lambda_re', 'delta_ssm_lambda_im', 'delta_ssm_log_step', 'delta_ssm_b_re', 'delta_ssm_b_im', 'delta_ssm_c_re', 'delta_ssm_c_im', 'delta_ssm_d', 'delta_ssm_w_glu', 'delta_ssm_b_glu', 'delta_mla_q_norm', 'delta_mla_w_uq', 'delta_mla_kv_norm', 'delta_mla_w_ukv', 'delta_mla_q_gain', 'delta_mla_k_gain', 'delta_out_norm_ssm', 'delta_out_norm_mla', 'delta_w_out', 'delta_norm_mem_q', 'delta_norm_mem_kv', 'delta_mem_w_q', 'delta_mem_w_kv', 'delta_mem_q_gain', 'delta_mem_k_gain', 'delta_mem_w_o', 'delta_norm_mlp', 'delta_mlp_w1', 'delta_mlp_w2', 'new_m_norm_mix', 'new_m_w_in', 'new_m_ssm_lambda_re', 'new_m_ssm_lambda_im', 'new_m_ssm_log_step', 'new_m_ssm_b_re', 'new_m_ssm_b_im', 'new_m_ssm_c_re', 'new_m_ssm_c_im', 'new_m_ssm_d', 'new_m_ssm_w_glu', 'new_m_ssm_b_glu', 'new_m_mla_q_norm', 'new_m_mla_w_uq', 'new_m_mla_kv_norm', 'new_m_mla_w_ukv', 'new_m_mla_q_gain', 'new_m_mla_k_gain', 'new_m_out_norm_ssm', 'new_m_out_norm_mla', 'new_m_w_out', 'new_m_norm_mem_q', 'new_m_norm_mem_kv', 'new_m_mem_w_q', 'new_m_mem_w_kv', 'new_m_mem_q_gain', 'new_m_mem_k_gain', 'new_m_mem_w_o', 'new_m_norm_mlp', 'new_m_mlp_w1', 'new_m_mlp_w2', 'new_v_norm_mix', 'new_v_w_in', 'new_v_ssm_lambda_re', 'new_v_ssm_lambda_im', 'new_v_ssm_log_step', 'new_v_ssm_b_re', 'new_v_ssm_b_im', 'new_v_ssm_c_re', 'new_v_ssm_c_im', 'new_v_ssm_d', 'new_v_ssm_w_glu', 'new_v_ssm_b_glu', 'new_v_mla_q_norm', 'new_v_mla_w_uq', 'new_v_mla_kv_norm', 'new_v_mla_w_ukv', 'new_v_mla_q_gain', 'new_v_mla_k_gain', 'new_v_out_norm_ssm', 'new_v_out_norm_mla', 'new_v_w_out', 'new_v_norm_mem_q', 'new_v_norm_mem_kv', 'new_v_mem_w_q', 'new_v_mem_w_kv', 'new_v_mem_q_gain', 'new_v_mem_k_gain', 'new_v_mem_w_o', 'new_v_norm_mlp', 'new_v_mlp_w1', 'new_v_mlp_w2']
TWIN_LEAF_KINDS = {'loss': 'loss', 'grad_x': 'grad_x', 'grad_norm_mix': 'grad_w', 'grad_w_in': 'grad_w', 'grad_ssm_lambda_re': 'grad_w', 'grad_ssm_lambda_im': 'grad_w', 'grad_ssm_log_step': 'grad_w', 'grad_ssm_b_re': 'grad_w', 'grad_ssm_b_im': 'grad_w', 'grad_ssm_c_re': 'grad_w', 'grad_ssm_c_im': 'grad_w', 'grad_ssm_d': 'grad_w', 'grad_ssm_w_glu': 'grad_w', 'grad_ssm_b_glu': 'grad_w', 'grad_mla_q_norm': 'grad_w', 'grad_mla_w_uq': 'grad_w', 'grad_mla_kv_norm': 'grad_w', 'grad_mla_w_ukv': 'grad_w', 'grad_mla_q_gain': 'grad_w', 'grad_mla_k_gain': 'grad_w', 'grad_out_norm_ssm': 'grad_w', 'grad_out_norm_mla': 'grad_w', 'grad_w_out': 'grad_w', 'grad_norm_mem_q': 'grad_w', 'grad_norm_mem_kv': 'grad_w', 'grad_mem_w_q': 'grad_w', 'grad_mem_w_kv': 'grad_w', 'grad_mem_q_gain': 'grad_w', 'grad_mem_k_gain': 'grad_w', 'grad_mem_w_o': 'grad_w', 'grad_norm_mlp': 'grad_w', 'grad_mlp_w1': 'grad_w', 'grad_mlp_w2': 'grad_w', 'delta_norm_mix': 'delta_w', 'delta_w_in': 'delta_w', 'delta_ssm_lambda_re': 'delta_w', 'delta_ssm_lambda_im': 'delta_w', 'delta_ssm_log_step': 'delta_w', 'delta_ssm_b_re': 'delta_w', 'delta_ssm_b_im': 'delta_w', 'delta_ssm_c_re': 'delta_w', 'delta_ssm_c_im': 'delta_w', 'delta_ssm_d': 'delta_w', 'delta_ssm_w_glu': 'delta_w', 'delta_ssm_b_glu': 'delta_w', 'delta_mla_q_norm': 'delta_w', 'delta_mla_w_uq': 'delta_w', 'delta_mla_kv_norm': 'delta_w', 'delta_mla_w_ukv': 'delta_w', 'delta_mla_q_gain': 'delta_w', 'delta_mla_k_gain': 'delta_w', 'delta_out_norm_ssm': 'delta_w', 'delta_out_norm_mla': 'delta_w', 'delta_w_out': 'delta_w', 'delta_norm_mem_q': 'delta_w', 'delta_norm_mem_kv': 'delta_w', 'delta_mem_w_q': 'delta_w', 'delta_mem_w_kv': 'delta_w', 'delta_mem_q_gain': 'delta_w', 'delta_mem_k_gain': 'delta_w', 'delta_mem_w_o': 'delta_w', 'delta_norm_mlp': 'delta_w', 'delta_mlp_w1': 'delta_w', 'delta_mlp_w2': 'delta_w', 'new_m_norm_mix': 'new_m', 'new_m_w_in': 'new_m', 'new_m_ssm_lambda_re': 'new_m', 'new_m_ssm_lambda_im': 'new_m', 'new_m_ssm_log_step': 'new_m', 'new_m_ssm_b_re': 'new_m', 'new_m_ssm_b_im': 'new_m', 'new_m_ssm_c_re': 'new_m', 'new_m_ssm_c_im': 'new_m', 'new_m_ssm_d': 'new_m', 'new_m_ssm_w_glu': 'new_m', 'new_m_ssm_b_glu': 'new_m', 'new_m_mla_q_norm': 'new_m', 'new_m_mla_w_uq': 'new_m', 'new_m_mla_kv_norm': 'new_m', 'new_m_mla_w_ukv': 'new_m', 'new_m_mla_q_gain': 'new_m', 'new_m_mla_k_gain': 'new_m', 'new_m_out_norm_ssm': 'new_m', 'new_m_out_norm_mla': 'new_m', 'new_m_w_out': 'new_m', 'new_m_norm_mem_q': 'new_m', 'new_m_norm_mem_kv': 'new_m', 'new_m_mem_w_q': 'new_m', 'new_m_mem_w_kv': 'new_m', 'new_m_mem_q_gain': 'new_m', 'new_m_mem_k_gain': 'new_m', 'new_m_mem_w_o': 'new_m', 'new_m_norm_mlp': 'new_m', 'new_m_mlp_w1': 'new_m', 'new_m_mlp_w2': 'new_m', 'new_v_norm_mix': 'new_v', 'new_v_w_in': 'new_v', 'new_v_ssm_lambda_re': 'new_v', 'new_v_ssm_lambda_im': 'new_v', 'new_v_ssm_log_step': 'new_v', 'new_v_ssm_b_re': 'new_v', 'new_v_ssm_b_im': 'new_v', 'new_v_ssm_c_re': 'new_v', 'new_v_ssm_c_im': 'new_v', 'new_v_ssm_d': 'new_v', 'new_v_ssm_w_glu': 'new_v', 'new_v_ssm_b_glu': 'new_v', 'new_v_mla_q_norm': 'new_v', 'new_v_mla_w_uq': 'new_v', 'new_v_mla_kv_norm': 'new_v', 'new_v_mla_w_ukv': 'new_v', 'new_v_mla_q_gain': 'new_v', 'new_v_mla_k_gain': 'new_v', 'new_v_out_norm_ssm': 'new_v', 'new_v_out_norm_mla': 'new_v', 'new_v_w_out': 'new_v', 'new_v_norm_mem_q': 'new_v', 'new_v_norm_mem_kv': 'new_v', 'new_v_mem_w_q': 'new_v', 'new_v_mem_w_kv': 'new_v', 'new_v_mem_q_gain': 'new_v', 'new_v_mem_k_gain': 'new_v', 'new_v_mem_w_o': 'new_v', 'new_v_norm_mlp': 'new_v', 'new_v_mlp_w1': 'new_v', 'new_v_mlp_w2': 'new_v'}


def _forward(args):
    return _fwd_reference(*[args[k] for k in FWD_PARAMS])


def _output_shape():
    def fwd():
        inp = _fwd_setup_inputs(0)
        return _fwd_reference(*[inp[k] for k in FWD_PARAMS])
    out = _jax.eval_shape(fwd)
    return out.shape, out.dtype

N_MICROBATCH = 1
ADAM_LR = 0.001
ADAM_B1 = 0.9
ADAM_B2 = 0.999
ADAM_EPS = 1e-08
ADAM_WD = 0.01
ADAM_STEP = 10
PER_EXAMPLE_BATCH_AXIS = {'x': 0, 'mem': 0, 'positions': 0, 'loss_target': 0}
SHARED_INPUTS = []
_WEIGHT_DTYPES = {'norm_mix': _jnp.float32, 'w_in': _jnp.float32, 'ssm_lambda_re': _jnp.float32, 'ssm_lambda_im': _jnp.float32, 'ssm_log_step': _jnp.float32, 'ssm_b_re': _jnp.float32, 'ssm_b_im': _jnp.float32, 'ssm_c_re': _jnp.float32, 'ssm_c_im': _jnp.float32, 'ssm_d': _jnp.float32, 'ssm_w_glu': _jnp.float32, 'ssm_b_glu': _jnp.float32, 'mla_q_norm': _jnp.float32, 'mla_w_uq': _jnp.float32, 'mla_kv_norm': _jnp.float32, 'mla_w_ukv': _jnp.float32, 'mla_q_gain': _jnp.float32, 'mla_k_gain': _jnp.float32, 'out_norm_ssm': _jnp.float32, 'out_norm_mla': _jnp.float32, 'w_out': _jnp.float32, 'norm_mem_q': _jnp.float32, 'norm_mem_kv': _jnp.float32, 'mem_w_q': _jnp.float32, 'mem_w_kv': _jnp.float32, 'mem_q_gain': _jnp.float32, 'mem_k_gain': _jnp.float32, 'mem_w_o': _jnp.float32, 'norm_mlp': _jnp.float32, 'mlp_w1': _jnp.float32, 'mlp_w2': _jnp.float32}
MOMENT_SCALE = {'norm_mix': 4.499405e+01, 'w_in': 4.736034e+01, 'ssm_lambda_re': 1.741475e+00, 'ssm_lambda_im': 1.743986e+00, 'ssm_log_step': 4.435107e+01, 'ssm_b_re': 1.139110e+00, 'ssm_b_im': 1.186222e+00, 'ssm_c_re': 3.074344e+00, 'ssm_c_im': 2.676817e+00, 'ssm_d': 5.839919e+01, 'ssm_w_glu': 8.381601e+00, 'ssm_b_glu': 2.181082e+01, 'mla_q_norm': 6.597267e+00, 'mla_w_uq': 3.680106e+00, 'mla_kv_norm': 1.223600e+02, 'mla_w_ukv': 4.143632e+01, 'mla_q_gain': 1.259172e+01, 'mla_k_gain': 1.334015e+01, 'out_norm_ssm': 9.260335e+01, 'out_norm_mla': 6.752743e+01, 'w_out': 5.534853e+01, 'norm_mem_q': 1.693508e+00, 'norm_mem_kv': 3.816476e+00, 'mem_w_q': 3.388627e+00, 'mem_w_kv': 4.875708e+00, 'mem_q_gain': 8.297688e+00, 'mem_k_gain': 8.299298e+00, 'mem_w_o': 2.926842e+00, 'norm_mlp': 1.072637e+02, 'mlp_w1': 1.851452e+01, 'mlp_w2': 5.415011e+01}


def _to_microbatches(a, axis):
    t = _jnp.moveaxis(a, axis, 0)
    t = t.reshape((N_MICROBATCH, t.shape[0] // N_MICROBATCH) + t.shape[1:])
    return _jnp.moveaxis(t, 1, axis + 1)


def setup_inputs(seed: int = 0) -> dict:
    inp = _fwd_setup_inputs(seed)
    key = _jax.random.fold_in(_jax.random.key(seed), 7919)
    shape, _ = _output_shape()
    out = dict(inp)
    out["loss_target"] = _jax.random.normal(_jax.random.fold_in(key, 0), shape, _jnp.float32)
    for i, name in enumerate(TWIN_WEIGHTS):
        w = inp[name].astype(_jnp.float32)
        if MOMENT_SCALE is None:
            s = _jnp.sqrt(_jnp.mean(_jnp.square(w)) + 1e-30)
        else:
            s = MOMENT_SCALE[name]
        km, kv = _jax.random.split(_jax.random.fold_in(key, i + 1))
        out[name] = w
        out["m_" + name] = s * _jax.random.normal(km, w.shape, _jnp.float32)
        out["v_" + name] = (s * s) * _jax.random.uniform(kv, w.shape, _jnp.float32, 0.5, 1.5)
    if N_MICROBATCH > 1:
        for name, axis in PER_EXAMPLE_BATCH_AXIS.items():
            out[name] = _to_microbatches(out[name], axis)
    return {'x': out['x'], 'mem': out['mem'], 'positions': out['positions'], 'norm_mix': out['norm_mix'], 'w_in': out['w_in'], 'ssm_lambda_re': out['ssm_lambda_re'], 'ssm_lambda_im': out['ssm_lambda_im'], 'ssm_log_step': out['ssm_log_step'], 'ssm_b_re': out['ssm_b_re'], 'ssm_b_im': out['ssm_b_im'], 'ssm_c_re': out['ssm_c_re'], 'ssm_c_im': out['ssm_c_im'], 'ssm_d': out['ssm_d'], 'ssm_w_glu': out['ssm_w_glu'], 'ssm_b_glu': out['ssm_b_glu'], 'mla_q_norm': out['mla_q_norm'], 'mla_w_uq': out['mla_w_uq'], 'mla_kv_norm': out['mla_kv_norm'], 'mla_w_ukv': out['mla_w_ukv'], 'mla_q_gain': out['mla_q_gain'], 'mla_k_gain': out['mla_k_gain'], 'out_norm_ssm': out['out_norm_ssm'], 'out_norm_mla': out['out_norm_mla'], 'w_out': out['w_out'], 'norm_mem_q': out['norm_mem_q'], 'norm_mem_kv': out['norm_mem_kv'], 'mem_w_q': out['mem_w_q'], 'mem_w_kv': out['mem_w_kv'], 'mem_q_gain': out['mem_q_gain'], 'mem_k_gain': out['mem_k_gain'], 'mem_w_o': out['mem_w_o'], 'norm_mlp': out['norm_mlp'], 'mlp_w1': out['mlp_w1'], 'mlp_w2': out['mlp_w2'], 'loss_target': out['loss_target'], 'm_norm_mix': out['m_norm_mix'], 'm_w_in': out['m_w_in'], 'm_ssm_lambda_re': out['m_ssm_lambda_re'], 'm_ssm_lambda_im': out['m_ssm_lambda_im'], 'm_ssm_log_step': out['m_ssm_log_step'], 'm_ssm_b_re': out['m_ssm_b_re'], 'm_ssm_b_im': out['m_ssm_b_im'], 'm_ssm_c_re': out['m_ssm_c_re'], 'm_ssm_c_im': out['m_ssm_c_im'], 'm_ssm_d': out['m_ssm_d'], 'm_ssm_w_glu': out['m_ssm_w_glu'], 'm_ssm_b_glu': out['m_ssm_b_glu'], 'm_mla_q_norm': out['m_mla_q_norm'], 'm_mla_w_uq': out['m_mla_w_uq'], 'm_mla_kv_norm': out['m_mla_kv_norm'], 'm_mla_w_ukv': out['m_mla_w_ukv'], 'm_mla_q_gain': out['m_mla_q_gain'], 'm_mla_k_gain': out['m_mla_k_gain'], 'm_out_norm_ssm': out['m_out_norm_ssm'], 'm_out_norm_mla': out['m_out_norm_mla'], 'm_w_out': out['m_w_out'], 'm_norm_mem_q': out['m_norm_mem_q'], 'm_norm_mem_kv': out['m_norm_mem_kv'], 'm_mem_w_q': out['m_mem_w_q'], 'm_mem_w_kv': out['m_mem_w_kv'], 'm_mem_q_gain': out['m_mem_q_gain'], 'm_mem_k_gain': out['m_mem_k_gain'], 'm_mem_w_o': out['m_mem_w_o'], 'm_norm_mlp': out['m_norm_mlp'], 'm_mlp_w1': out['m_mlp_w1'], 'm_mlp_w2': out['m_mlp_w2'], 'v_norm_mix': out['v_norm_mix'], 'v_w_in': out['v_w_in'], 'v_ssm_lambda_re': out['v_ssm_lambda_re'], 'v_ssm_lambda_im': out['v_ssm_lambda_im'], 'v_ssm_log_step': out['v_ssm_log_step'], 'v_ssm_b_re': out['v_ssm_b_re'], 'v_ssm_b_im': out['v_ssm_b_im'], 'v_ssm_c_re': out['v_ssm_c_re'], 'v_ssm_c_im': out['v_ssm_c_im'], 'v_ssm_d': out['v_ssm_d'], 'v_ssm_w_glu': out['v_ssm_w_glu'], 'v_ssm_b_glu': out['v_ssm_b_glu'], 'v_mla_q_norm': out['v_mla_q_norm'], 'v_mla_w_uq': out['v_mla_w_uq'], 'v_mla_kv_norm': out['v_mla_kv_norm'], 'v_mla_w_ukv': out['v_mla_w_ukv'], 'v_mla_q_gain': out['v_mla_q_gain'], 'v_mla_k_gain': out['v_mla_k_gain'], 'v_out_norm_ssm': out['v_out_norm_ssm'], 'v_out_norm_mla': out['v_out_norm_mla'], 'v_w_out': out['v_w_out'], 'v_norm_mem_q': out['v_norm_mem_q'], 'v_norm_mem_kv': out['v_norm_mem_kv'], 'v_mem_w_q': out['v_mem_w_q'], 'v_mem_w_kv': out['v_mem_w_kv'], 'v_mem_q_gain': out['v_mem_q_gain'], 'v_mem_k_gain': out['v_mem_k_gain'], 'v_mem_w_o': out['v_mem_w_o'], 'v_norm_mlp': out['v_norm_mlp'], 'v_mlp_w1': out['v_mlp_w1'], 'v_mlp_w2': out['v_mlp_w2']}


def _loss(weights, diff, rest, loss_target):
    with _jax.named_scope("forward"):
        args = {**rest, TWIN_DIFF_INPUT: diff, **{k: w.astype(_WEIGHT_DTYPES[k]) for k, w in weights.items()}}
        y = _forward(args)
    with _jax.named_scope("loss_head"):
        err = _jnp.square(y.astype(_jnp.float32) - loss_target)
        return 0.5 * _jnp.sum(_jnp.mean(err, axis=-1)) if err.ndim else 0.5 * err


def _adamw(w, g, m, v):
    m = ADAM_B1 * m + (1.0 - ADAM_B1) * g
    v = ADAM_B2 * v + (1.0 - ADAM_B2) * _jnp.square(g)
    m_hat = m / (1.0 - ADAM_B1 ** ADAM_STEP)
    v_hat = v / (1.0 - ADAM_B2 ** ADAM_STEP)
    delta = -ADAM_LR * (m_hat / (_jnp.sqrt(v_hat) + ADAM_EPS) + ADAM_WD * w)
    return delta, m, v


def reference(x, mem, positions, norm_mix, w_in, ssm_lambda_re, ssm_lambda_im, ssm_log_step, ssm_b_re, ssm_b_im, ssm_c_re, ssm_c_im, ssm_d, ssm_w_glu, ssm_b_glu, mla_q_norm, mla_w_uq, mla_kv_norm, mla_w_ukv, mla_q_gain, mla_k_gain, out_norm_ssm, out_norm_mla, w_out, norm_mem_q, norm_mem_kv, mem_w_q, mem_w_kv, mem_q_gain, mem_k_gain, mem_w_o, norm_mlp, mlp_w1, mlp_w2, loss_target, m_norm_mix, m_w_in, m_ssm_lambda_re, m_ssm_lambda_im, m_ssm_log_step, m_ssm_b_re, m_ssm_b_im, m_ssm_c_re, m_ssm_c_im, m_ssm_d, m_ssm_w_glu, m_ssm_b_glu, m_mla_q_norm, m_mla_w_uq, m_mla_kv_norm, m_mla_w_ukv, m_mla_q_gain, m_mla_k_gain, m_out_norm_ssm, m_out_norm_mla, m_w_out, m_norm_mem_q, m_norm_mem_kv, m_mem_w_q, m_mem_w_kv, m_mem_q_gain, m_mem_k_gain, m_mem_w_o, m_norm_mlp, m_mlp_w1, m_mlp_w2, v_norm_mix, v_w_in, v_ssm_lambda_re, v_ssm_lambda_im, v_ssm_log_step, v_ssm_b_re, v_ssm_b_im, v_ssm_c_re, v_ssm_c_im, v_ssm_d, v_ssm_w_glu, v_ssm_b_glu, v_mla_q_norm, v_mla_w_uq, v_mla_kv_norm, v_mla_w_ukv, v_mla_q_gain, v_mla_k_gain, v_out_norm_ssm, v_out_norm_mla, v_w_out, v_norm_mem_q, v_norm_mem_kv, v_mem_w_q, v_mem_w_kv, v_mem_q_gain, v_mem_k_gain, v_mem_w_o, v_norm_mlp, v_mlp_w1, v_mlp_w2):
    given = dict(x=x, mem=mem, positions=positions, norm_mix=norm_mix, w_in=w_in, ssm_lambda_re=ssm_lambda_re, ssm_lambda_im=ssm_lambda_im, ssm_log_step=ssm_log_step, ssm_b_re=ssm_b_re, ssm_b_im=ssm_b_im, ssm_c_re=ssm_c_re, ssm_c_im=ssm_c_im, ssm_d=ssm_d, ssm_w_glu=ssm_w_glu, ssm_b_glu=ssm_b_glu, mla_q_norm=mla_q_norm, mla_w_uq=mla_w_uq, mla_kv_norm=mla_kv_norm, mla_w_ukv=mla_w_ukv, mla_q_gain=mla_q_gain, mla_k_gain=mla_k_gain, out_norm_ssm=out_norm_ssm, out_norm_mla=out_norm_mla, w_out=w_out, norm_mem_q=norm_mem_q, norm_mem_kv=norm_mem_kv, mem_w_q=mem_w_q, mem_w_kv=mem_w_kv, mem_q_gain=mem_q_gain, mem_k_gain=mem_k_gain, mem_w_o=mem_w_o, norm_mlp=norm_mlp, mlp_w1=mlp_w1, mlp_w2=mlp_w2, loss_target=loss_target, m_norm_mix=m_norm_mix, m_w_in=m_w_in, m_ssm_lambda_re=m_ssm_lambda_re, m_ssm_lambda_im=m_ssm_lambda_im, m_ssm_log_step=m_ssm_log_step, m_ssm_b_re=m_ssm_b_re, m_ssm_b_im=m_ssm_b_im, m_ssm_c_re=m_ssm_c_re, m_ssm_c_im=m_ssm_c_im, m_ssm_d=m_ssm_d, m_ssm_w_glu=m_ssm_w_glu, m_ssm_b_glu=m_ssm_b_glu, m_mla_q_norm=m_mla_q_norm, m_mla_w_uq=m_mla_w_uq, m_mla_kv_norm=m_mla_kv_norm, m_mla_w_ukv=m_mla_w_ukv, m_mla_q_gain=m_mla_q_gain, m_mla_k_gain=m_mla_k_gain, m_out_norm_ssm=m_out_norm_ssm, m_out_norm_mla=m_out_norm_mla, m_w_out=m_w_out, m_norm_mem_q=m_norm_mem_q, m_norm_mem_kv=m_norm_mem_kv, m_mem_w_q=m_mem_w_q, m_mem_w_kv=m_mem_w_kv, m_mem_q_gain=m_mem_q_gain, m_mem_k_gain=m_mem_k_gain, m_mem_w_o=m_mem_w_o, m_norm_mlp=m_norm_mlp, m_mlp_w1=m_mlp_w1, m_mlp_w2=m_mlp_w2, v_norm_mix=v_norm_mix, v_w_in=v_w_in, v_ssm_lambda_re=v_ssm_lambda_re, v_ssm_lambda_im=v_ssm_lambda_im, v_ssm_log_step=v_ssm_log_step, v_ssm_b_re=v_ssm_b_re, v_ssm_b_im=v_ssm_b_im, v_ssm_c_re=v_ssm_c_re, v_ssm_c_im=v_ssm_c_im, v_ssm_d=v_ssm_d, v_ssm_w_glu=v_ssm_w_glu, v_ssm_b_glu=v_ssm_b_glu, v_mla_q_norm=v_mla_q_norm, v_mla_w_uq=v_mla_w_uq, v_mla_kv_norm=v_mla_kv_norm, v_mla_w_ukv=v_mla_w_ukv, v_mla_q_gain=v_mla_q_gain, v_mla_k_gain=v_mla_k_gain, v_out_norm_ssm=v_out_norm_ssm, v_out_norm_mla=v_out_norm_mla, v_w_out=v_w_out, v_norm_mem_q=v_norm_mem_q, v_norm_mem_kv=v_norm_mem_kv, v_mem_w_q=v_mem_w_q, v_mem_w_kv=v_mem_w_kv, v_mem_q_gain=v_mem_q_gain, v_mem_k_gain=v_mem_k_gain, v_mem_w_o=v_mem_w_o, v_norm_mlp=v_norm_mlp, v_mlp_w1=v_mlp_w1, v_mlp_w2=v_mlp_w2)
    weights = {n: given[n] for n in TWIN_WEIGHTS}
    shared = {n: given[n] for n in SHARED_INPUTS}
    per_example = {n: given[n] for n in ['x', 'mem', 'positions']}
    grad_fn = _jax.value_and_grad(_loss, argnums=(0, 1))

    def one_microbatch(ex, loss_target):
        ex = dict(ex)
        diff = ex.pop(TWIN_DIFF_INPUT)
        return grad_fn(weights, diff, {**shared, **ex}, loss_target)

    if N_MICROBATCH == 1:
        loss, (grad_w, grad_x) = one_microbatch(per_example, given["loss_target"])
    else:
        def body(carry, xs):
            loss_sum, grad_sum = carry
            l_k, (gw_k, gx_k) = one_microbatch(xs[0], xs[1])
            with _jax.named_scope("update"):
                return (loss_sum + l_k, _jax.tree.map(_jnp.add, grad_sum, gw_k)), gx_k

        init = (_jnp.zeros((), _jnp.float32), _jax.tree.map(_jnp.zeros_like, weights))
        (loss, grad_w), grad_x = _jax.lax.scan(body, init, (per_example, given["loss_target"]))
    with _jax.named_scope("update"):
        delta_w, new_m, new_v = {}, {}, {}
        for n in TWIN_WEIGHTS:
            delta_w[n], new_m[n], new_v[n] = _adamw(weights[n], grad_w[n], given["m_" + n], given["v_" + n])
    return (loss, grad_x, *[grad_w[n] for n in TWIN_WEIGHTS], *[delta_w[n] for n in TWIN_WEIGHTS],
            *[new_m[n] for n in TWIN_WEIGHTS], *[new_v[n] for n in TWIN_WEIGHTS])
```

```python
import functools
import math

import jax
import jax.numpy as jnp
from jax import lax
from jax.experimental import pallas as pl
from jax.experimental.pallas import tpu as pltpu

F32 = jnp.float32
BF = jnp.bfloat16

D = 1024
DEPTH = 4
N_MEM = 256
MEM_HEADS = 4
MEM_HD = 64
SSM_W = 512
SSM_G = 32
SSM_H = 16
SSM_P = 64
MLA_HEADS = 8
QK_NOPE = 64
QK_ROPE = 32
QK_DIM = 96
V_DIM = 64
Q_LORA = 256
KV_LORA = 128
ROPE_THETA = 10000.0
D_FF = 4096
IN_COLS = 928
EPS = 1e-6
NDEV = 8
LANES = 128
SEGS = 8
S5_LW = 256
S5_NHB = (SSM_G * SSM_P) // S5_LW
ADAM_LR = 0.001
ADAM_B1 = 0.9
ADAM_B2 = 0.999
ADAM_EPS = 1e-08
ADAM_WD = 0.01
ADAM_STEP = 10
VMEM_BIG = 56 * 1024 * 1024

NN = (((1,), (0,)), ((), ()))
NT = (((1,), (1,)), ((), ()))
TN = (((0,), (0,)), ((), ()))

BIG = ('w_in', 'ssm_w_glu', 'mla_w_uq', 'mla_w_ukv', 'w_out', 'mem_w_q', 'mem_w_kv', 'mem_w_o', 'mlp_w1', 'mlp_w2')
BIG_AXIS = {'w_in': 1, 'ssm_w_glu': 1, 'mla_w_uq': 2, 'mla_w_ukv': 2, 'w_out': 1, 'mem_w_q': 1, 'mem_w_kv': 1,
            'mem_w_o': 2, 'mlp_w1': 2, 'mlp_w2': 1}
SMALL = ('norm_mix', 'ssm_lambda_re', 'ssm_lambda_im', 'ssm_log_step', 'ssm_b_re', 'ssm_b_im', 'ssm_c_re', 'ssm_c_im',
         'ssm_d', 'ssm_b_glu', 'mla_q_norm', 'mla_kv_norm', 'mla_q_gain', 'mla_k_gain', 'out_norm_ssm', 'out_norm_mla',
         'norm_mem_q', 'norm_mem_kv', 'mem_q_gain', 'mem_k_gain', 'norm_mlp')
WEIGHTS = ('norm_mix', 'w_in', 'ssm_lambda_re', 'ssm_lambda_im', 'ssm_log_step', 'ssm_b_re', 'ssm_b_im', 'ssm_c_re',
           'ssm_c_im', 'ssm_d', 'ssm_w_glu', 'ssm_b_glu', 'mla_q_norm', 'mla_w_uq', 'mla_kv_norm', 'mla_w_ukv',
           'mla_q_gain', 'mla_k_gain', 'out_norm_ssm', 'out_norm_mla', 'w_out', 'norm_mem_q', 'norm_mem_kv', 'mem_w_q',
           'mem_w_kv', 'mem_q_gain', 'mem_k_gain', 'mem_w_o', 'norm_mlp', 'mlp_w1', 'mlp_w2')


def _call(body, *, name, out_shape, grid=(), in_specs=None, out_specs=None, scratch=(), sem=None, vmem=None):
    params = {}
    if sem is not None:
        params['dimension_semantics'] = sem
    if vmem is not None:
        params['vmem_limit_bytes'] = vmem
    specs = {} if in_specs is None else dict(grid=grid, in_specs=in_specs, out_specs=out_specs)
    return pl.pallas_call(body, name=name, out_shape=out_shape, scratch_shapes=list(scratch),
                          compiler_params=pltpu.CompilerParams(**params), **specs)


def _sds(shape, dtype):
    return jax.ShapeDtypeStruct(shape, dtype)


def _dot(a, b, dims=NN):
    return lax.dot_general(a.astype(BF), b.astype(BF), dims, preferred_element_type=F32)


def _split(a):
    hi = a.astype(BF)
    return hi, (a - hi.astype(F32)).astype(BF)


def _dot3(a, b, dims=NN):
    ah, al = _split(a)
    bh, bl = _split(b)
    d = lambda p, q: lax.dot_general(p, q, dims, preferred_element_type=F32)
    return d(ah, bh) + (d(ah, bl) + d(al, bh))


def _rms(x, n):
    r = lax.rsqrt(jnp.sum(x * x, axis=-1, keepdims=True) * (1.0 / n) + EPS)
    return x * r, r


def _rms_bwd(xhat, r, dxhat, n):
    return r * (dxhat - xhat * (jnp.sum(dxhat * xhat, axis=-1, keepdims=True) * (1.0 / n)))


def _colsum(a):
    return jnp.sum(a, axis=0, keepdims=True)


def _tile(t, want):
    return min(t, want)


def _bidx(nb):
    return (lambda b: b) if nb > 1 else (lambda b: 0)


def mm(a, b, mode, *, name, out_dtype=F32, tm=512, tn=512, tk=1024):
    squeeze = a.ndim == 2 and b.ndim == 2
    a = a[None] if a.ndim == 2 else a
    b = b[None] if b.ndim == 2 else b
    nb = max(a.shape[0], b.shape[0])
    ab, bb = _bidx(a.shape[0]), _bidx(b.shape[0])
    if mode in ('nn', 'nt'):
        m, k = a.shape[1:]
        n = b.shape[2] if mode == 'nn' else b.shape[1]
        tm, tn = _tile(m, tm), _tile(n, tn)
        dims = NN if mode == 'nn' else NT

        def body(a_ref, b_ref, o_ref):
            o_ref[...] = _dot(a_ref[...], b_ref[...], dims).astype(o_ref.dtype)

        bspec = (pl.BlockSpec((None, k, tn), lambda bi, i, j: (bb(bi), 0, j)) if mode == 'nn'
                 else pl.BlockSpec((None, tn, k), lambda bi, i, j: (bb(bi), j, 0)))
        out = _call(body, name=name, grid=(nb, m // tm, n // tn),
                    in_specs=[pl.BlockSpec((None, tm, k), lambda bi, i, j: (ab(bi), i, 0)), bspec],
                    out_specs=pl.BlockSpec((None, tm, tn), lambda bi, i, j: (bi, i, j)),
                    out_shape=_sds((nb, m, n), out_dtype), sem=('parallel', 'parallel', 'parallel'))(a, b)
    else:
        k, m = a.shape[1:]
        n = b.shape[2]
        tm, tn, tk = _tile(m, tm), _tile(n, tn), _tile(k, tk)

        def body(a_ref, b_ref, o_ref):
            @pl.when(pl.program_id(3) == 0)
            def _():
                o_ref[...] = jnp.zeros_like(o_ref)

            o_ref[...] += _dot(a_ref[...], b_ref[...], TN)

        out = _call(body, name=name, grid=(nb, m // tm, n // tn, k // tk),
                    in_specs=[pl.BlockSpec((None, tk, tm), lambda bi, i, j, kk: (ab(bi), kk, i)),
                              pl.BlockSpec((None, tk, tn), lambda bi, i, j, kk: (bb(bi), kk, j))],
                    out_specs=pl.BlockSpec((None, tm, tn), lambda bi, i, j, kk: (bi, i, j)),
                    out_shape=_sds((nb, m, n), F32), sem=('parallel', 'parallel', 'parallel', 'arbitrary'))(a, b)
    return out[0] if squeeze else out


def rmsnorm_fwd(x, g, *, name, tq=512):
    t, d = x.shape
    tq = _tile(t, tq)

    def body(x_ref, g_ref, o_ref):
        xh, _ = _rms(x_ref[...], d)
        o_ref[...] = (xh * g_ref[...]).astype(o_ref.dtype)

    return _call(body, name=name, grid=(t // tq,),
                 in_specs=[pl.BlockSpec((tq, d), lambda i: (i, 0)), pl.BlockSpec((1, d), lambda i: (0, 0))],
                 out_specs=pl.BlockSpec((tq, d), lambda i: (i, 0)), out_shape=_sds((t, d), BF), sem=('parallel',))(x, g)


def rmsnorm_bwd(x, g, dh, dres, *, name, col=0, tq=512):
    t, d = x.shape
    tq = _tile(t, tq)
    has_res = dres is not None

    def body(*refs):
        if has_res:
            x_ref, g_ref, dh_ref, dres_ref, dx_ref, dg_ref = refs
        else:
            x_ref, g_ref, dh_ref, dx_ref, dg_ref = refs
        xh, r = _rms(x_ref[...], d)
        dh_ = dh_ref[...].astype(F32)
        dx = _rms_bwd(xh, r, dh_ * g_ref[...], d)
        if has_res:
            dx = dx + dres_ref[...]
        dx_ref[...] = dx

        @pl.when(pl.program_id(0) == 0)
        def _():
            dg_ref[...] = jnp.zeros_like(dg_ref)

        dg_ref[...] += _colsum(dh_ * xh)

    in_specs = [pl.BlockSpec((tq, d), lambda i: (i, 0)), pl.BlockSpec((1, d), lambda i: (0, 0)),
                pl.BlockSpec((tq, d), lambda i: (i, col))]
    args = [x, g, dh]
    if has_res:
        in_specs.append(pl.BlockSpec((tq, d), lambda i: (i, 0)))
        args.append(dres)
    return _call(body, name=name, grid=(t // tq,), in_specs=in_specs,
                 out_specs=[pl.BlockSpec((tq, d), lambda i: (i, 0)), pl.BlockSpec((1, d), lambda i: (0, 0))],
                 out_shape=[_sds((t, d), F32), _sds((1, d), F32)], sem=('arbitrary',))(*args)


def _cmul(ar, ai, xr, xi):
    return ar * xr - ai * xi, ar * xi + ai * xr


def _seg_carries(er, ei, pr, pi, reverse):
    lw = er.shape[1]
    zero = jnp.zeros((1, lw), F32)
    order = range(SEGS - 1, -1, -1) if reverse else range(SEGS)
    cin_r, cin_i = [None] * SEGS, [None] * SEGS
    tr, ti = zero, zero
    for j in order:
        cin_r[j], cin_i[j] = tr, ti
        mr, mi = _cmul(pr, pi, tr, ti)
        tr, ti = er[j:j + 1, :] + mr, ei[j:j + 1, :] + mi
    return jnp.concatenate(cin_r, axis=0), jnp.concatenate(cin_i, axis=0)


def _s5_chunk(t):
    return _tile(t, 512)


def s5_fwd(u_p, prm, *, name):
    t = u_p.shape[0]
    ch = _s5_chunk(t)
    nch, steps = t // ch, ch // SEGS
    lw = S5_LW

    def body(u_ref, ar_ref, ai_ref, pr_ref, pi_ref, bre_ref, bim_ref, cre_ref, cim_ref, d_ref, y_ref, bur, bui, sr, si):
        hb = pl.program_id(0)
        ar = jnp.broadcast_to(ar_ref[0], (SEGS, lw))
        ai = jnp.broadcast_to(ai_ref[0], (SEGS, lw))

        def load_bu(c):
            u = u_ref[pl.ds(pl.multiple_of(c * ch, ch), ch), :]
            bur[...] = _dot3(u, bre_ref[0])
            bui[...] = _dot3(u, bim_ref[0])

        def scan_chunk(c, carry, store):
            load_bu(c)

            def step(i, s):
                r0 = pl.multiple_of(i * SEGS, SEGS)
                mr, mi = _cmul(ar, ai, s[0], s[1])
                nr, ni = mr + bur[pl.ds(r0, SEGS), :], mi + bui[pl.ds(r0, SEGS), :]
                if store:
                    sr[pl.ds(r0, SEGS), :] = nr
                    si[pl.ds(r0, SEGS), :] = ni
                return nr, ni

            return lax.fori_loop(0, steps, step, carry, unroll=8)

        zero = jnp.zeros((SEGS, lw), F32)
        er, ei = lax.fori_loop(0, nch, lambda c, s: scan_chunk(c, s, False), (zero, zero))
        cin = _seg_carries(er, ei, pr_ref[0], pi_ref[0], False)

        def out_chunk(c, carry):
            carry = scan_chunk(c, carry, True)
            rows = pl.ds(pl.multiple_of(c * ch, ch), ch)
            y = _dot3(sr[...], cre_ref[0]) - _dot3(si[...], cim_ref[0])

            @pl.when(hb % 2 == 0)
            def _():
                y_ref[rows, :] = y + d_ref[...] * u_ref[rows, :]

            @pl.when(hb % 2 == 1)
            def _():
                y_ref[rows, :] += y

            return carry

        lax.fori_loop(0, nch, out_chunk, cin)

    vec = pl.BlockSpec((1, 1, lw), lambda h: (h, 0, 0))
    return _call(
        body, name=name, grid=(S5_NHB,),
        in_specs=[pl.BlockSpec((t, LANES), lambda h: (0, h // 2)), vec, vec, vec, vec,
                  pl.BlockSpec((1, LANES, lw), lambda h: (h, 0, 0)), pl.BlockSpec((1, LANES, lw), lambda h: (h, 0, 0)),
                  pl.BlockSpec((1, lw, LANES), lambda h: (h, 0, 0)), pl.BlockSpec((1, lw, LANES), lambda h: (h, 0, 0)),
                  pl.BlockSpec((1, LANES), lambda h: (0, h // 2))],
        out_specs=pl.BlockSpec((t, LANES), lambda h: (0, h // 2)), out_shape=_sds((t, SSM_W), F32),
        scratch=[pltpu.VMEM((ch, lw), F32)] * 4, sem=('arbitrary',), vmem=VMEM_BIG,
    )(u_p, prm['ar'], prm['ai'], prm['pr'], prm['pi'], prm['bre'], prm['bim'], prm['cre'], prm['cim'], prm['d'])


def s5_bwd(u_p, dy_p, prm, *, name):
    t = u_p.shape[0]
    ch = _s5_chunk(t)
    nch, steps = t // ch, ch // SEGS
    lw = S5_LW

    def body(u_ref, dy_ref, ar_ref, ai_ref, pr_ref, pi_ref, bre_ref, bim_ref, cre_ref, cim_ref, d_ref,
             du_ref, dar_ref, dai_ref, dbre_ref, dbim_ref, dcre_ref, dcim_ref, dd_ref, bur, bui, sr, si):
        hb = pl.program_id(0)
        ar = jnp.broadcast_to(ar_ref[0], (SEGS, lw))
        ai = jnp.broadcast_to(ai_ref[0], (SEGS, lw))
        zero = jnp.zeros((SEGS, lw), F32)

        def rows_of(c):
            return pl.ds(pl.multiple_of(c * ch, ch), ch)

        def fwd_chunk(c, carry, store):
            u = u_ref[rows_of(c), :]
            bur[...] = _dot3(u, bre_ref[0])
            bui[...] = _dot3(u, bim_ref[0])
            base = c * ch + SEGS

            def step(i, s):
                r0 = pl.multiple_of(i * SEGS, SEGS)
                mr, mi = _cmul(ar, ai, s[0], s[1])
                nr, ni = mr + bur[pl.ds(r0, SEGS), :], mi + bui[pl.ds(r0, SEGS), :]
                if store:
                    w0 = pl.multiple_of(base + i * SEGS, SEGS)
                    sr[pl.ds(w0, SEGS), :] = nr
                    si[pl.ds(w0, SEGS), :] = ni
                return nr, ni

            return lax.fori_loop(0, steps, step, carry, unroll=8)

        er, ei = lax.fori_loop(0, nch, lambda c, s: fwd_chunk(c, s, False), (zero, zero))
        cin_r, cin_i = _seg_carries(er, ei, pr_ref[0], pi_ref[0], False)
        sr[pl.ds(0, SEGS), :] = cin_r
        si[pl.ds(0, SEGS), :] = cin_i
        lax.fori_loop(0, nch, lambda c, s: fwd_chunk(c, s, True), (cin_r, cin_i))

        def load_ds(c):
            dy = dy_ref[rows_of(c), :]
            bur[...] = _dot3(dy, cre_ref[0], NT)
            bui[...] = -_dot3(dy, cim_ref[0], NT)

        def rev_local(cc, carry):
            load_ds(nch - 1 - cc)

            def step(ii, lam):
                r0 = pl.multiple_of((steps - 1 - ii) * SEGS, SEGS)
                mr, mi = _cmul(ar, -ai, lam[0], lam[1])
                return mr + bur[pl.ds(r0, SEGS), :], mi + bui[pl.ds(r0, SEGS), :]

            return lax.fori_loop(0, steps, step, carry, unroll=8)

        lr0, li0 = lax.fori_loop(0, nch, rev_local, (zero, zero))
        rin = _seg_carries(lr0, li0, pr_ref[0], -pi_ref[0], True)

        dbre_ref[...] = jnp.zeros_like(dbre_ref)
        dbim_ref[...] = jnp.zeros_like(dbim_ref)
        dcre_ref[...] = jnp.zeros_like(dcre_ref)
        dcim_ref[...] = jnp.zeros_like(dcim_ref)

        def rev_chunk(cc, carry):
            c = nch - 1 - cc
            load_ds(c)
            base = c * ch

            def step(ii, st):
                lam_r, lam_i, acc_r, acc_i = st
                i = steps - 1 - ii
                r0 = pl.multiple_of(i * SEGS, SEGS)
                mr, mi = _cmul(ar, -ai, lam_r, lam_i)
                nr, ni = mr + bur[pl.ds(r0, SEGS), :], mi + bui[pl.ds(r0, SEGS), :]
                bur[pl.ds(r0, SEGS), :] = nr
                bui[pl.ds(r0, SEGS), :] = ni
                p0 = pl.multiple_of(base + i * SEGS, SEGS)
                pr_, pi_ = sr[pl.ds(p0, SEGS), :], si[pl.ds(p0, SEGS), :]
                return nr, ni, acc_r + (nr * pr_ + ni * pi_), acc_i + (ni * pr_ - nr * pi_)

            carry = lax.fori_loop(0, steps, step, carry, unroll=8)
            rows = rows_of(c)
            u = u_ref[rows, :]
            dy = dy_ref[rows, :]
            lam_r, lam_i = bur[...], bui[...]
            du = _dot3(lam_r, bre_ref[0], NT) + _dot3(lam_i, bim_ref[0], NT)

            @pl.when(hb % 2 == 0)
            def _():
                du_ref[rows, :] = du + d_ref[...] * dy

            @pl.when(hb % 2 == 1)
            def _():
                du_ref[rows, :] += du

            dbre_ref[0] += _dot3(u, lam_r, TN)
            dbim_ref[0] += _dot3(u, lam_i, TN)
            srows = pl.ds(pl.multiple_of(base + SEGS, SEGS), ch)
            dcre_ref[0] += _dot3(sr[srows, :], dy, TN)
            dcim_ref[0] -= _dot3(si[srows, :], dy, TN)
            return carry

        _, _, acc_r, acc_i = lax.fori_loop(0, nch, rev_chunk, (rin[0], rin[1], zero, zero))
        dar_ref[0] = _colsum(acc_r)
        dai_ref[0] = _colsum(acc_i)

        @pl.when(hb % 2 == 0)
        def _():
            dd_ref[...] = _colsum(dy_ref[...] * u_ref[...])

    vec = pl.BlockSpec((1, 1, lw), lambda h: (h, 0, 0))
    bsp = pl.BlockSpec((1, LANES, lw), lambda h: (h, 0, 0))
    csp = pl.BlockSpec((1, lw, LANES), lambda h: (h, 0, 0))
    act = pl.BlockSpec((t, LANES), lambda h: (0, h // 2))
    dsp = pl.BlockSpec((1, LANES), lambda h: (0, h // 2))
    return _call(
        body, name=name, grid=(S5_NHB,),
        in_specs=[act, act, vec, vec, vec, vec, bsp, bsp, csp, csp, dsp],
        out_specs=[act, vec, vec, bsp, bsp, csp, csp, dsp],
        out_shape=[_sds((t, SSM_W), F32), _sds((S5_NHB, 1, lw), F32), _sds((S5_NHB, 1, lw), F32),
                   _sds((S5_NHB, LANES, lw), F32), _sds((S5_NHB, LANES, lw), F32),
                   _sds((S5_NHB, lw, LANES), F32), _sds((S5_NHB, lw, LANES), F32), _sds((1, SSM_W), F32)],
        scratch=[pltpu.VMEM((ch, lw), F32), pltpu.VMEM((ch, lw), F32),
                 pltpu.VMEM((t + SEGS, lw), F32), pltpu.VMEM((t + SEGS, lw), F32)],
        sem=('arbitrary',), vmem=VMEM_BIG,
    )(u_p, dy_p, prm['ar'], prm['ai'], prm['pr'], prm['pi'], prm['bre'], prm['bim'], prm['cre'], prm['cim'], prm['d'])


def s5_prep(t, lam_re, lam_im, log_step, b_re, b_im, c_re, c_im):
    step = jnp.exp(log_step)[:, None]
    mag = jnp.exp(lam_re * step)
    ar, ai = mag * jnp.cos(lam_im * step), mag * jnp.sin(lam_im * step)
    den = lam_re * lam_re + lam_im * lam_im
    nr, ni = ar - 1.0, ai
    fr, fi = (nr * lam_re + ni * lam_im) / den, (ni * lam_re - nr * lam_im) / den
    bbr = fr[..., None] * b_re - fi[..., None] * b_im
    bbi = fr[..., None] * b_im + fi[..., None] * b_re
    gl = S5_LW // SSM_P
    eye = jnp.eye(gl, dtype=F32)
    half = (jnp.arange(S5_NHB) % 2)[:, None, None]

    def bmat(bb):
        x = bb.transpose(0, 2, 1).reshape(S5_NHB, gl, SSM_H, SSM_P)
        x = jnp.einsum('bghp,gk->bghkp', x, eye).reshape(S5_NHB, gl * SSM_H, S5_LW)
        z = jnp.zeros_like(x)
        return jnp.where(half == 0, jnp.concatenate([x, z], axis=1), jnp.concatenate([z, x], axis=1))

    def cmat(cc):
        x = cc.transpose(0, 2, 1).reshape(S5_NHB, gl, SSM_P, SSM_H)
        x = jnp.einsum('bgph,gk->bgpkh', x, eye).reshape(S5_NHB, S5_LW, gl * SSM_H)
        z = jnp.zeros_like(x)
        return jnp.where(half == 0, jnp.concatenate([x, z], axis=2), jnp.concatenate([z, x], axis=2))

    vec = lambda a: a.reshape(S5_NHB, 1, S5_LW)
    ni_steps = float(t // SEGS)
    pmag = jnp.exp(lam_re * step * ni_steps)
    pr, pi = pmag * jnp.cos(lam_im * step * ni_steps), pmag * jnp.sin(lam_im * step * ni_steps)
    return dict(ar=vec(ar), ai=vec(ai), bre=bmat(bbr), bim=bmat(bbi), cre=cmat(c_re), cim=cmat(c_im),
                pr=lax.stop_gradient(vec(pr)), pi=lax.stop_gradient(vec(pi)))


def _gelu(x):
    c = math.sqrt(2.0 / math.pi)
    return 0.5 * x * (1.0 + jnp.tanh(c * (x + 0.044715 * (x * x * x))))


def _gelu_grad(x):
    c = math.sqrt(2.0 / math.pi)
    th = jnp.tanh(c * (x + 0.044715 * (x * x * x)))
    return 0.5 * (1.0 + th) + 0.5 * x * (1.0 - th * th) * (c * (1.0 + 3.0 * 0.044715 * (x * x)))


def glu_fwd(ypre, w_glu, b_glu, *, name, tq=512):
    t = ypre.shape[0]
    tq = _tile(t, tq)

    def body(y_ref, w_ref, b_ref, o_ref):
        yg = _gelu(y_ref[...])
        z = _dot(yg, w_ref[...]) + b_ref[...]
        o_ref[...] = yg * jax.nn.sigmoid(z)

    return _call(body, name=name, grid=(t // tq,),
                 in_specs=[pl.BlockSpec((tq, SSM_W), lambda i: (i, 0)), pl.BlockSpec((SSM_W, SSM_W), lambda i: (0, 0)),
                           pl.BlockSpec((1, SSM_W), lambda i: (0, 0))],
                 out_specs=pl.BlockSpec((tq, SSM_W), lambda i: (i, 0)), out_shape=_sds((t, SSM_W), F32),
                 sem=('parallel',))(ypre, w_glu, b_glu)


def glu_bwd(ypre, dy, w_glu, b_glu, *, name, tq=512):
    t = ypre.shape[0]
    tq = _tile(t, tq)

    def body(y_ref, dy_ref, w_ref, b_ref, dyp_ref, yg_ref, dz_ref, db_ref):
        ypre_ = y_ref[...]
        yg = _gelu(ypre_)
        sig = jax.nn.sigmoid(_dot(yg, w_ref[...]) + b_ref[...])
        dy_ = dy_ref[...]
        dz = dy_ * yg * sig * (1.0 - sig)
        dyg = dy_ * sig + _dot(dz, w_ref[...], NT)
        dyp_ref[...] = dyg * _gelu_grad(ypre_)
        yg_ref[...] = yg.astype(BF)
        dz_ref[...] = dz.astype(BF)

        @pl.when(pl.program_id(0) == 0)
        def _():
            db_ref[...] = jnp.zeros_like(db_ref)

        db_ref[...] += _colsum(dz)

    row = pl.BlockSpec((tq, SSM_W), lambda i: (i, 0))
    vec = pl.BlockSpec((1, SSM_W), lambda i: (0, 0))
    return _call(body, name=name, grid=(t // tq,),
                 in_specs=[row, row, pl.BlockSpec((SSM_W, SSM_W), lambda i: (0, 0)), vec],
                 out_specs=[row, row, row, vec],
                 out_shape=[_sds((t, SSM_W), F32), _sds((t, SSM_W), BF), _sds((t, SSM_W), BF), _sds((1, SSM_W), F32)],
                 sem=('arbitrary',))(ypre, dy, w_glu, b_glu)


def _rope(x, cos, sa, sb):
    return x * cos + pltpu.roll(x, 16, 1) * sa + pltpu.roll(x, 112, 1) * sb


def _rope_t(d, cos, sa, sb):
    return d * cos + pltpu.roll(d * sa, 112, 1) + pltpu.roll(d * sb, 16, 1)


def rope_tables(positions):
    half = QK_ROPE // 2
    inv_freq = ROPE_THETA ** (-jnp.arange(half, dtype=F32) / half)
    ang = positions.astype(F32)[:, None] * inv_freq
    cos, sin = jnp.cos(ang), jnp.sin(ang)
    t = positions.shape[0]
    one, zero = jnp.ones((t, QK_NOPE), F32), jnp.zeros((t, QK_NOPE), F32)
    pad1, pad0 = jnp.ones((t, 32), F32), jnp.zeros((t, 32), F32)
    z16 = jnp.zeros((t, half), F32)
    return (jnp.concatenate([one, cos, cos, pad1], axis=1), jnp.concatenate([zero, z16, sin, pad0], axis=1),
            jnp.concatenate([zero, -sin, z16, pad0], axis=1))


def mla_prep_fwd(proj, tabs, w, *, name, tq=256):
    t = proj.shape[0]
    tq = _tile(t, tq)

    def body(cq_ref, ckv_ref, kr_ref, cos_ref, sa_ref, sb_ref, qn_ref, kvn_ref, wq_ref, wk_ref, wv_ref, qg_ref, kg_ref,
             q_ref, k_ref, v_ref):
        cqn = (_rms(cq_ref[...], Q_LORA)[0] * qn_ref[...]).astype(BF)
        ckvn = (_rms(ckv_ref[...], KV_LORA)[0] * kvn_ref[...]).astype(BF)
        cos, sa, sb = cos_ref[...], sa_ref[...], sb_ref[...]
        kr = kr_ref[...]
        for h in range(MLA_HEADS):
            q = _rms(_dot(cqn, wq_ref[h]), QK_DIM)[0] * qg_ref[...]
            q_ref[h] = _rope(q, cos, sa, sb).astype(BF)
            k = _rms(_dot(ckvn, wk_ref[h]) + kr, QK_DIM)[0] * kg_ref[...]
            k_ref[h] = _rope(k, cos, sa, sb).astype(BF)
            v_ref[h] = _dot(ckvn, wv_ref[h]).astype(BF)

    tab = pl.BlockSpec((tq, LANES), lambda i: (i, 0))
    full = lambda shape: pl.BlockSpec(shape, lambda i: (0,) * len(shape))
    hout = pl.BlockSpec((MLA_HEADS, tq, LANES), lambda i: (0, i, 0))
    return _call(
        body, name=name, grid=(t // tq,),
        in_specs=[pl.BlockSpec((tq, Q_LORA), lambda i: (i, 2)), pl.BlockSpec((tq, LANES), lambda i: (i, 6)),
                  pl.BlockSpec((tq, LANES), lambda i: (i, 7)), tab, tab, tab,
                  full((1, Q_LORA)), full((1, KV_LORA)), full((MLA_HEADS, Q_LORA, LANES)),
                  full((MLA_HEADS, KV_LORA, LANES)), full((MLA_HEADS, KV_LORA, LANES)), full((1, LANES)), full((1, LANES))],
        out_specs=[hout, hout, hout], out_shape=[_sds((MLA_HEADS, t, LANES), BF)] * 3, sem=('parallel',),
    )(proj, proj, proj, *tabs, w['q_norm'], w['kv_norm'], w['wq'], w['wk'], w['wv'], w['q_gain'], w['k_gain'])


def mla_prep_bwd(proj, tabs, w, dq, dk, dv, *, name, tq=256):
    t = proj.shape[0]
    tq = _tile(t, tq)

    def body(cq_ref, ckv_ref, kr_ref, cos_ref, sa_ref, sb_ref, qn_ref, kvn_ref, wq_ref, wk_ref, wv_ref, qg_ref, kg_ref,
             dq_ref, dk_ref, dv_ref,
             dpm_ref, cqn_ref, ckvn_ref, dqr_ref, dkraw_ref, dvb_ref, dqn_ref, dkvn_ref, dqg_ref, dkg_ref):
        cq_h, cq_r = _rms(cq_ref[...], Q_LORA)
        ckv_h, ckv_r = _rms(ckv_ref[...], KV_LORA)
        cqn = (cq_h * qn_ref[...]).astype(BF)
        ckvn = (ckv_h * kvn_ref[...]).astype(BF)
        cqn_ref[...] = cqn
        ckvn_ref[...] = ckvn
        cos, sa, sb = cos_ref[...], sa_ref[...], sb_ref[...]
        kr = kr_ref[...]
        dcqn = jnp.zeros((tq, Q_LORA), F32)
        dckvn = jnp.zeros((tq, KV_LORA), F32)
        dkrope = jnp.zeros((tq, LANES), F32)
        dqg = jnp.zeros((1, LANES), F32)
        dkg = jnp.zeros((1, LANES), F32)
        for h in range(MLA_HEADS):
            qh, qr = _rms(_dot(cqn, wq_ref[h]), QK_DIM)
            dqo = _rope_t(dq_ref[h], cos, sa, sb)
            dqg = dqg + _colsum(dqo * qh)
            dqraw = _rms_bwd(qh, qr, dqo * qg_ref[...], QK_DIM).astype(BF)
            dqr_ref[h] = dqraw
            dcqn = dcqn + _dot(dqraw, wq_ref[h], NT)
            kh, krs = _rms(_dot(ckvn, wk_ref[h]) + kr, QK_DIM)
            dko = _rope_t(dk_ref[h], cos, sa, sb)
            dkg = dkg + _colsum(dko * kh)
            dkraw = _rms_bwd(kh, krs, dko * kg_ref[...], QK_DIM)
            dkrope = dkrope + dkraw
            dkraw = dkraw.astype(BF)
            dkraw_ref[h] = dkraw
            dvb = dv_ref[h].astype(BF)
            dvb_ref[h] = dvb
            dckvn = dckvn + _dot(dkraw, wk_ref[h], NT) + _dot(dvb, wv_ref[h], NT)
        dpm_ref[:, 0:Q_LORA] = _rms_bwd(cq_h, cq_r, dcqn * qn_ref[...], Q_LORA)
        dpm_ref[:, Q_LORA:Q_LORA + KV_LORA] = _rms_bwd(ckv_h, ckv_r, dckvn * kvn_ref[...], KV_LORA)
        dpm_ref[:, Q_LORA + KV_LORA:512] = dkrope

        @pl.when(pl.program_id(0) == 0)
        def _():
            dqn_ref[...] = jnp.zeros_like(dqn_ref)
            dkvn_ref[...] = jnp.zeros_like(dkvn_ref)
            dqg_ref[...] = jnp.zeros_like(dqg_ref)
            dkg_ref[...] = jnp.zeros_like(dkg_ref)

        dqn_ref[...] += _colsum(dcqn * cq_h)
        dkvn_ref[...] += _colsum(dckvn * ckv_h)
        dqg_ref[...] += dqg
        dkg_ref[...] += dkg

    tab = pl.BlockSpec((tq, LANES), lambda i: (i, 0))
    full = lambda shape: pl.BlockSpec(shape, lambda i: (0,) * len(shape))
    hblk = pl.BlockSpec((MLA_HEADS, tq, LANES), lambda i: (0, i, 0))
    return _call(
        body, name=name, grid=(t // tq,),
        in_specs=[pl.BlockSpec((tq, Q_LORA), lambda i: (i, 2)), pl.BlockSpec((tq, LANES), lambda i: (i, 6)),
                  pl.BlockSpec((tq, LANES), lambda i: (i, 7)), tab, tab, tab,
                  full((1, Q_LORA)), full((1, KV_LORA)), full((MLA_HEADS, Q_LORA, LANES)),
                  full((MLA_HEADS, KV_LORA, LANES)), full((MLA_HEADS, KV_LORA, LANES)), full((1, LANES)), full((1, LANES)),
                  hblk, hblk, hblk],
        out_specs=[pl.BlockSpec((tq, 512), lambda i: (i, 0)),
                   pl.BlockSpec((tq, Q_LORA), lambda i: (i, 0)), pl.BlockSpec((tq, KV_LORA), lambda i: (i, 0)),
                   hblk, hblk, hblk, full((1, Q_LORA)), full((1, KV_LORA)), full((1, LANES)), full((1, LANES))],
        out_shape=[_sds((t, 512), F32), _sds((t, Q_LORA), BF), _sds((t, KV_LORA), BF),
                   _sds((MLA_HEADS, t, LANES), BF), _sds((MLA_HEADS, t, LANES), BF), _sds((MLA_HEADS, t, LANES), BF),
                   _sds((1, Q_LORA), F32), _sds((1, KV_LORA), F32), _sds((1, LANES), F32), _sds((1, LANES), F32)],
        sem=('arbitrary',),
    )(proj, proj, proj, *tabs, w['q_norm'], w['kv_norm'], w['wq'], w['wk'], w['wv'], w['q_gain'], w['k_gain'], dq, dk, dv)


ATT_BLK = 256
ATT_SCALE = 1.0 / math.sqrt(QK_DIM)


def flash_fwd(q, k, v, *, name):
    t = q.shape[1]
    blk = _tile(t, ATT_BLK)

    def body(q_ref, k_ref, v_ref, o_ref, lse_ref):
        qi = pl.program_id(1)
        row = lax.broadcasted_iota(jnp.int32, (blk, blk), 0)
        col = lax.broadcasted_iota(jnp.int32, (blk, blk), 1)
        o_acc = jnp.zeros((blk, LANES), F32)
        for hh in range(2):
            qv = q_ref[hh]

            def block(j, carry, masked):
                m, l, acc = carry
                rows = pl.ds(pl.multiple_of(j * blk, blk), blk)
                s = _dot(qv, k_ref[hh, rows, :], NT) * ATT_SCALE
                if masked:
                    s = jnp.where(col <= row, s, -jnp.inf)
                m2 = jnp.maximum(m, jnp.max(s, axis=-1, keepdims=True))
                p = jnp.exp(s - m2)
                alpha = jnp.exp(m - m2)
                return m2, alpha * l + jnp.sum(p, axis=-1, keepdims=True), alpha * acc + _dot(p, v_ref[hh, rows, :])

            init = (jnp.full((blk, 1), -jnp.inf, F32), jnp.zeros((blk, 1), F32), jnp.zeros((blk, LANES), F32))
            carry = lax.fori_loop(0, qi, lambda j, c: block(j, c, False), init)
            m, l, acc = block(qi, carry, True)
            o_acc = o_acc + acc / l
            lse_ref[hh] = m + jnp.log(l)
        o_ref[...] = o_acc

    return _call(
        body, name=name, grid=(MLA_HEADS // 2, t // blk),
        in_specs=[pl.BlockSpec((2, blk, LANES), lambda p, i: (p, i, 0)), pl.BlockSpec((2, t, LANES), lambda p, i: (p, 0, 0)),
                  pl.BlockSpec((2, t, LANES), lambda p, i: (p, 0, 0))],
        out_specs=[pl.BlockSpec((blk, LANES), lambda p, i: (i, p)), pl.BlockSpec((2, blk, 1), lambda p, i: (p, i, 0))],
        out_shape=[_sds((t, 512), F32), _sds((MLA_HEADS, t, 1), F32)], sem=('parallel', 'parallel'),
    )(q, k, v)


def flash_bwd(q, k, v, o, do, lse, *, name):
    t = q.shape[1]
    blk = _tile(t, ATT_BLK)
    nb = t // blk

    def body(q_ref, k_ref, v_ref, o_ref, do_ref, lse_ref, dq_ref, dk_ref, dv_ref):
        h, j = pl.program_id(0), pl.program_id(1)
        row = lax.broadcasted_iota(jnp.int32, (blk, blk), 0)
        col = lax.broadcasted_iota(jnp.int32, (blk, blk), 1)
        lane = lax.broadcasted_iota(jnp.int32, (1, LANES), 1)
        mine = (lane // V_DIM) == (h % 2)

        @pl.when(j == 0)
        def _():
            dq_ref[...] = jnp.zeros_like(dq_ref)

        kv, vv = k_ref[...], v_ref[...]

        def block(i, carry, masked):
            dk, dv = carry
            rows = pl.ds(pl.multiple_of(i * blk, blk), blk)
            qv, dov = q_ref[rows, :], do_ref[rows, :]
            delta = jnp.sum(jnp.where(mine, dov * o_ref[rows, :], 0.0), axis=-1, keepdims=True)
            p = jnp.exp(_dot(qv, kv, NT) * ATT_SCALE - lse_ref[rows, :])
            if masked:
                p = jnp.where(col <= row, p, 0.0)
            dob = dov.astype(BF)
            dv = dv + _dot(p, dob, TN)
            ds = p * (_dot(dob, vv, NT) - delta) * ATT_SCALE
            dk = dk + _dot(ds, qv, TN)
            dq_ref[rows, :] += _dot(ds, kv)
            return dk, dv

        zero = jnp.zeros((blk, LANES), F32)
        carry = block(j, (zero, zero), True)
        dk, dv = lax.fori_loop(j + 1, nb, lambda i, c: block(i, c, False), carry)
        dk_ref[...] = dk
        dv_ref[...] = jnp.where(mine, dv, 0.0)

    whole = pl.BlockSpec((None, t, LANES), lambda h, j: (h, 0, 0))
    kvb = pl.BlockSpec((None, blk, LANES), lambda h, j: (h, j, 0))
    pair = pl.BlockSpec((t, LANES), lambda h, j: (0, h // 2))
    return _call(
        body, name=name, grid=(MLA_HEADS, nb),
        in_specs=[whole, kvb, kvb, pair, pair,
                  pl.BlockSpec((None, t, 1), lambda h, j: (h, 0, 0))],
        out_specs=[whole, kvb, kvb], out_shape=[_sds((MLA_HEADS, t, LANES), F32)] * 3,
        sem=('parallel', 'arbitrary'), vmem=VMEM_BIG,
    )(q, k, v, o, do, lse)


def mix_out_fwd(x, y_ssm, o, g_ssm, g_mla, w_out, *, name, tq=512):
    t = x.shape[0]
    tq = _tile(t, tq)

    def body(x_ref, ys_ref, o_ref, gs_ref, gm_ref, w_ref, x1_ref, yn_ref):
        ns = (_rms(ys_ref[...], SSM_W)[0] * gs_ref[...]).astype(BF)
        nm = (_rms(o_ref[...], 512)[0] * gm_ref[...]).astype(BF)
        yn_ref[:, 0:SSM_W] = ns
        yn_ref[:, SSM_W:D] = nm
        x1_ref[...] = x_ref[...] + _dot(ns, w_ref[0:SSM_W, :]) + _dot(nm, w_ref[SSM_W:D, :])

    row = lambda w: pl.BlockSpec((tq, w), lambda i: (i, 0))
    vec = pl.BlockSpec((1, 512), lambda i: (0, 0))
    return _call(body, name=name, grid=(t // tq,),
                 in_specs=[row(D), row(512), row(512), vec, vec, pl.BlockSpec((D, D), lambda i: (0, 0))],
                 out_specs=[row(D), row(D)], out_shape=[_sds((t, D), F32), _sds((t, D), BF)], sem=('parallel',),
                 )(x, y_ssm, o, g_ssm, g_mla, w_out)


MEM_SCALE = 1.0 / math.sqrt(MEM_HD)


def memkv_fwd(mem, g, wk, wv, kg, *, name):
    def body(m_ref, g_ref, wk_ref, wv_ref, kg_ref, mh_ref, k_ref, v_ref):
        mh = (_rms(m_ref[...], D)[0] * g_ref[...]).astype(BF)
        mh_ref[...] = mh
        for h in range(MEM_HEADS):
            cols = slice(h * LANES, (h + 1) * LANES)
            k_ref[h] = (_rms(_dot(mh, wk_ref[:, cols]), MEM_HD)[0] * kg_ref[...]).astype(BF)
            v_ref[h] = _dot(mh, wv_ref[:, cols]).astype(BF)

    return _call(body, name=name,
                 out_shape=[_sds((N_MEM, D), BF), _sds((MEM_HEADS, N_MEM, LANES), BF), _sds((MEM_HEADS, N_MEM, LANES), BF)],
                 )(mem, g, wk, wv, kg)


def memkv_bwd(mem, g, wk, wv, kg, dk, dv, *, name):
    def body(m_ref, g_ref, wk_ref, wv_ref, kg_ref, dk_ref, dv_ref, dwk_ref, dwv_ref, dkg_ref, dg_ref):
        mhat, _ = _rms(m_ref[...], D)
        mh = (mhat * g_ref[...]).astype(BF)
        lane = lax.broadcasted_iota(jnp.int32, (1, LANES), 1)
        dkg = jnp.zeros((1, LANES), F32)
        dmh = jnp.zeros((N_MEM, D), F32)
        for h in range(MEM_HEADS):
            cols = slice(h * LANES, (h + 1) * LANES)
            kh, kr = _rms(_dot(mh, wk_ref[:, cols]), MEM_HD)
            dko = dk_ref[h]
            dkg = dkg + _colsum(dko * kh)
            dkraw = _rms_bwd(kh, kr, dko * kg_ref[...], MEM_HD).astype(BF)
            dvh = jnp.where((lane // MEM_HD) == (h % 2), dv_ref[h], 0.0).astype(BF)
            dwk_ref[:, cols] = _dot(mh, dkraw, TN)
            dwv_ref[:, cols] = _dot(mh, dvh, TN)
            dmh = dmh + _dot(dkraw, wk_ref[:, cols], NT) + _dot(dvh, wv_ref[:, cols], NT)
        dkg_ref[...] = dkg
        dg_ref[...] = _colsum(dmh * mhat)

    return _call(body, name=name,
                 out_shape=[_sds((D, 512), F32), _sds((D, 512), F32), _sds((1, LANES), F32), _sds((1, D), F32)],
                 )(mem, g, wk, wv, kg, dk, dv)


def memattn_fwd(x, g, wq, qg, kh, vh, wo, *, name, tq=256):
    t = x.shape[0]
    tq = _tile(t, tq)

    def body(x_ref, g_ref, wq_ref, qg_ref, k_ref, v_ref, wo_ref, x2_ref, hn_ref):
        xv = x_ref[...]
        hn = (_rms(xv, D)[0] * g_ref[...]).astype(BF)
        hn_ref[...] = hn
        out = xv
        for pb in range(MEM_HEADS // 2):
            o = jnp.zeros((tq, LANES), F32)
            for h in (2 * pb, 2 * pb + 1):
                q = _rms(_dot(hn, wq_ref[:, h * LANES:(h + 1) * LANES]), MEM_HD)[0] * qg_ref[...]
                s = _dot(q, k_ref[h], NT) * MEM_SCALE
                p = jnp.exp(s - jnp.max(s, axis=-1, keepdims=True))
                p = p / jnp.sum(p, axis=-1, keepdims=True)
                o = o + _dot(p, v_ref[h])
            out = out + _dot(o, wo_ref[pb * LANES:(pb + 1) * LANES, :])
        x2_ref[...] = out

    full = lambda shape: pl.BlockSpec(shape, lambda i: (0,) * len(shape))
    row = pl.BlockSpec((tq, D), lambda i: (i, 0))
    return _call(body, name=name, grid=(t // tq,),
                 in_specs=[row, full((1, D)), full((D, 512)), full((1, LANES)), full((MEM_HEADS, N_MEM, LANES)),
                           full((MEM_HEADS, N_MEM, LANES)), full((MEM_HEADS * MEM_HD, D))],
                 out_specs=[row, row], out_shape=[_sds((t, D), F32), _sds((t, D), BF)], sem=('parallel',),
                 )(x, g, wq, qg, kh, vh, wo)


def memattn_bwd(x, dx2, g, wq, qg, kh, vh, wo, *, name, tq=256):
    t = x.shape[0]
    tq = _tile(t, tq)

    def body(x_ref, dx2_ref, g_ref, wq_ref, qg_ref, k_ref, v_ref, wo_ref,
             dx_ref, o_ref, dqr_ref, dk_ref, dv_ref, dqg_ref, dg_ref):
        @pl.when(pl.program_id(0) == 0)
        def _():
            dk_ref[...] = jnp.zeros_like(dk_ref)
            dv_ref[...] = jnp.zeros_like(dv_ref)
            dqg_ref[...] = jnp.zeros_like(dqg_ref)
            dg_ref[...] = jnp.zeros_like(dg_ref)

        xhat, xr = _rms(x_ref[...], D)
        hn = (xhat * g_ref[...]).astype(BF)
        dx2 = dx2_ref[...]
        dx2b = dx2.astype(BF)
        dh = jnp.zeros((tq, D), F32)
        dqg = jnp.zeros((1, LANES), F32)
        for pb in range(MEM_HEADS // 2):
            do = _dot(dx2b, wo_ref[pb * LANES:(pb + 1) * LANES, :], NT).astype(BF)
            o = jnp.zeros((tq, LANES), F32)
            for h in (2 * pb, 2 * pb + 1):
                cols = slice(h * LANES, (h + 1) * LANES)
                qh, qr = _rms(_dot(hn, wq_ref[:, cols]), MEM_HD)
                qb = (qh * qg_ref[...]).astype(BF)
                s = _dot(qb, k_ref[h], NT) * MEM_SCALE
                p = jnp.exp(s - jnp.max(s, axis=-1, keepdims=True))
                p = p / jnp.sum(p, axis=-1, keepdims=True)
                pb16 = p.astype(BF)
                o = o + _dot(pb16, v_ref[h])
                dv_ref[h] += _dot(pb16, do, TN)
                dp = _dot(do, v_ref[h], NT)
                ds = (p * (dp - jnp.sum(dp * p, axis=-1, keepdims=True)) * MEM_SCALE).astype(BF)
                dk_ref[h] += _dot(ds, qb, TN)
                dqo = _dot(ds, k_ref[h])
                dqg = dqg + _colsum(dqo * qh)
                dqraw = _rms_bwd(qh, qr, dqo * qg_ref[...], MEM_HD).astype(BF)
                dqr_ref[:, cols] = dqraw
                dh = dh + _dot(dqraw, wq_ref[:, cols], NT)
            o_ref[:, pb * LANES:(pb + 1) * LANES] = o.astype(BF)
        dx_ref[...] = dx2 + _rms_bwd(xhat, xr, dh * g_ref[...], D)
        dqg_ref[...] += dqg
        dg_ref[...] += _colsum(dh * xhat)

    full = lambda shape: pl.BlockSpec(shape, lambda i: (0,) * len(shape))
    row = lambda w: pl.BlockSpec((tq, w), lambda i: (i, 0))
    return _call(body, name=name, grid=(t // tq,),
                 in_specs=[row(D), row(D), full((1, D)), full((D, 512)), full((1, LANES)), full((MEM_HEADS, N_MEM, LANES)),
                           full((MEM_HEADS, N_MEM, LANES)), full((MEM_HEADS * MEM_HD, D))],
                 out_specs=[row(D), row(256), row(512), full((MEM_HEADS, N_MEM, LANES)), full((MEM_HEADS, N_MEM, LANES)),
                            full((1, LANES)), full((1, D))],
                 out_shape=[_sds((t, D), F32), _sds((t, 256), BF), _sds((t, 512), BF),
                            _sds((MEM_HEADS, N_MEM, LANES), F32), _sds((MEM_HEADS, N_MEM, LANES), F32),
                            _sds((1, LANES), F32), _sds((1, D), F32)],
                 sem=('arbitrary',))(x, dx2, g, wq, qg, kh, vh, wo)


def mlp_fwd(x, h, w1, w2, *, name, tq=1024, tf=512):
    t = x.shape[0]
    tq = _tile(t, tq)

    def body(x_ref, h_ref, w1_ref, w2_ref, o_ref):
        @pl.when(pl.program_id(1) == 0)
        def _():
            o_ref[...] = x_ref[...]

        a = jnp.maximum(_dot(h_ref[...], w1_ref[...]), 0.0)
        o_ref[...] += _dot(a * a, w2_ref[...])

    row = pl.BlockSpec((tq, D), lambda i, f: (i, 0))
    return _call(body, name=name, grid=(t // tq, D_FF // tf),
                 in_specs=[row, row, pl.BlockSpec((D, tf), lambda i, f: (0, f)), pl.BlockSpec((tf, D), lambda i, f: (f, 0))],
                 out_specs=row, out_shape=_sds((t, D), F32), sem=('parallel', 'arbitrary'), vmem=VMEM_BIG)(x, h, w1, w2)


def mlp_bwd(h, dx, w1, w2, *, name, tq=1024, tf=512):
    t = h.shape[0]
    tq = _tile(t, tq)

    def body(h_ref, dx_ref, w1_ref, w2_ref, dh_ref, r_ref, da_ref):
        @pl.when(pl.program_id(1) == 0)
        def _():
            dh_ref[...] = jnp.zeros_like(dh_ref)

        a = jnp.maximum(_dot(h_ref[...], w1_ref[...]), 0.0)
        r_ref[...] = (a * a).astype(BF)
        da = (_dot(dx_ref[...], w2_ref[...], NT) * (2.0 * a)).astype(BF)
        da_ref[...] = da
        dh_ref[...] += _dot(da, w1_ref[...], NT)

    row = pl.BlockSpec((tq, D), lambda i, f: (i, 0))
    act = pl.BlockSpec((tq, tf), lambda i, f: (i, f))
    return _call(body, name=name, grid=(t // tq, D_FF // tf),
                 in_specs=[row, row, pl.BlockSpec((D, tf), lambda i, f: (0, f)), pl.BlockSpec((tf, D), lambda i, f: (f, 0))],
                 out_specs=[row, act, act], out_shape=[_sds((t, D), F32), _sds((t, D_FF), BF), _sds((t, D_FF), BF)],
                 sem=('parallel', 'arbitrary'), vmem=VMEM_BIG)(h, dx, w1, w2)


def loss_fwd_bwd(y, target, *, name, tq=512):
    t = y.shape[0]
    tq = _tile(t, tq)

    def body(y_ref, t_ref, dy_ref, l_ref):
        @pl.when(pl.program_id(0) == 0)
        def _():
            l_ref[...] = jnp.zeros_like(l_ref)

        e = y_ref[...] - t_ref[...]
        dy_ref[...] = e * (1.0 / D)
        l_ref[...] += _colsum(e * e) * (0.5 / D)

    row = pl.BlockSpec((tq, D), lambda i: (i, 0))
    return _call(body, name=name, grid=(t // tq,), in_specs=[row, row],
                 out_specs=[row, pl.BlockSpec((1, D), lambda i: (0, 0))],
                 out_shape=[_sds((t, D), F32), _sds((1, D), F32)], sem=('arbitrary',))(y, target)


def prep_big(w):
    w_in = w['w_in']
    z = lambda r, c: jnp.zeros((r, c), w_in.dtype)
    w_in_pad = jnp.concatenate([w_in[:, :896], z(D, 64), w_in[:, 896:928], z(D, 32)], axis=1)
    wq = w['mla_w_uq'].reshape(Q_LORA, MLA_HEADS, QK_DIM).transpose(1, 0, 2)
    wq = jnp.pad(wq, ((0, 0), (0, 0), (0, LANES - QK_DIM)))
    ukv = w['mla_w_ukv'].reshape(KV_LORA, MLA_HEADS, QK_NOPE + V_DIM).transpose(1, 0, 2)
    wk = jnp.pad(ukv[:, :, :QK_NOPE], ((0, 0), (0, 0), (0, LANES - QK_NOPE)))
    vpart = ukv[:, :, QK_NOPE:]
    zv = jnp.zeros_like(vpart)
    odd = (jnp.arange(MLA_HEADS) % 2)[:, None, None] == 1
    wv = jnp.where(odd, jnp.concatenate([zv, vpart], axis=2), jnp.concatenate([vpart, zv], axis=2))
    mq = jnp.pad(w['mem_w_q'].reshape(D, MEM_HEADS, MEM_HD), ((0, 0), (0, 0), (0, LANES - MEM_HD))).reshape(D, 512)
    mkv = w['mem_w_kv'].reshape(D, MEM_HEADS, 2 * MEM_HD)
    mk = jnp.pad(mkv[:, :, :MEM_HD], ((0, 0), (0, 0), (0, LANES - MEM_HD))).reshape(D, 512)
    mvp = mkv[:, :, MEM_HD:]
    zm = jnp.zeros_like(mvp)
    modd = (jnp.arange(MEM_HEADS) % 2)[None, :, None] == 1
    mv = jnp.where(modd, jnp.concatenate([zm, mvp], axis=2), jnp.concatenate([mvp, zm], axis=2)).reshape(D, 512)
    return dict(w_in=w_in_pad, w_glu=w['ssm_w_glu'], wq=wq, wk=wk, wv=wv, w_out=w['w_out'], mq=mq, mk=mk, mv=mv,
                mo=w['mem_w_o'], w1=w['mlp_w1'], w2=w['mlp_w2'])


def prep_small(t, s):
    row = lambda a: a.reshape(1, -1)
    pad = lambda a: jnp.pad(a, (0, LANES - a.shape[0])).reshape(1, LANES)
    out = s5_prep(t, s['ssm_lambda_re'], s['ssm_lambda_im'], s['ssm_log_step'], s['ssm_b_re'], s['ssm_b_im'],
                  s['ssm_c_re'], s['ssm_c_im'])
    out.update(d=row(s['ssm_d']), norm_mix=row(s['norm_mix']), b_glu=row(s['ssm_b_glu']), q_norm=row(s['mla_q_norm']),
               kv_norm=row(s['mla_kv_norm']), q_gain=pad(s['mla_q_gain']), k_gain=pad(s['mla_k_gain']),
               g_ssm=row(s['out_norm_ssm']), g_mla=row(s['out_norm_mla']), norm_mem_q=row(s['norm_mem_q']),
               norm_mem_kv=row(s['norm_mem_kv']), mem_q_gain=pad(s['mem_q_gain']), mem_k_gain=pad(s['mem_k_gain']),
               norm_mlp=row(s['norm_mlp']))
    return out


def _perm(a):
    t, c = a.shape
    return a.reshape(SEGS, t // SEGS, c).transpose(1, 0, 2).reshape(t, c)


def _unperm(a):
    t, c = a.shape
    return a.reshape(t // SEGS, SEGS, c).transpose(1, 0, 2).reshape(t, c)


def layer_fwd(l, x, mem, tabs, wb, ws):
    n = lambda s: f'l{l}_{s}'
    h1 = rmsnorm_fwd(x, ws['norm_mix'], name=n('norm_mix'))
    proj = mm(h1, wb['w_in'], 'nn', name=n('w_in'))
    ypre_p = s5_fwd(_perm(proj[:, :SSM_W]), ws, name=n('s5'))
    ypre = _unperm(ypre_p)
    y_ssm = glu_fwd(ypre, wb['w_glu'], ws['b_glu'], name=n('glu'))
    mw = dict(q_norm=ws['q_norm'], kv_norm=ws['kv_norm'], wq=wb['wq'], wk=wb['wk'], wv=wb['wv'],
              q_gain=ws['q_gain'], k_gain=ws['k_gain'])
    q, k, v = mla_prep_fwd(proj, tabs, mw, name=n('mla_prep'))
    o, lse = flash_fwd(q, k, v, name=n('flash'))
    x1, yn = mix_out_fwd(x, y_ssm, o, ws['g_ssm'], ws['g_mla'], wb['w_out'], name=n('mix_out'))
    mh, kh, vh = memkv_fwd(mem, ws['norm_mem_kv'], wb['mk'], wb['mv'], ws['mem_k_gain'], name=n('memkv'))
    x2, h2 = memattn_fwd(x1, ws['norm_mem_q'], wb['mq'], ws['mem_q_gain'], kh, vh, wb['mo'], name=n('memattn'))
    h3 = rmsnorm_fwd(x2, ws['norm_mlp'], name=n('norm_mlp'))
    x3 = mlp_fwd(x2, h3, wb['w1'], wb['w2'], name=n('mlp'))
    saved = dict(x=x, h1=h1, proj=proj, ypre=ypre, y_ssm=y_ssm, q=q, k=k, v=v, o=o, lse=lse, x1=x1, yn=yn,
                 kh=kh, vh=vh, x2=x2, h2=h2, h3=h3, mw=mw)
    return x3, saved


def layer_bwd(l, dx3, mem, tabs, wb, ws, sv):
    n = lambda s: f'l{l}_{s}_bwd'
    gb, gs = {}, {}
    dx3b = dx3.astype(BF)
    dh3, r, da = mlp_bwd(sv['h3'], dx3b, wb['w1'], wb['w2'], name=n('mlp'))
    gb['w1'] = mm(sv['h3'], da, 'tn', name=n('w1'))
    gb['w2'] = mm(r, dx3b, 'tn', name=n('w2'))
    dx2, gs['norm_mlp'] = rmsnorm_bwd(sv['x2'], ws['norm_mlp'], dh3, dx3, name=n('norm_mlp'))
    dx1, o_mem, dqr_mem, dkh, dvh, gs['mem_q_gain'], gs['norm_mem_q'] = memattn_bwd(
        sv['x1'], dx2, ws['norm_mem_q'], wb['mq'], ws['mem_q_gain'], sv['kh'], sv['vh'], wb['mo'], name=n('memattn'))
    dx2b = dx2.astype(BF)
    gb['mo'] = mm(o_mem, dx2b, 'tn', name=n('mo'))
    gb['mq'] = mm(sv['h2'], dqr_mem, 'tn', name=n('mq'))
    gb['mk'], gb['mv'], gs['mem_k_gain'], gs['norm_mem_kv'] = memkv_bwd(
        mem, ws['norm_mem_kv'], wb['mk'], wb['mv'], ws['mem_k_gain'], dkh, dvh, name=n('memkv'))
    dx1b = dx1.astype(BF)
    dyn = mm(dx1b, wb['w_out'], 'nt', name=n('w_out_dx'))
    gb['w_out'] = mm(sv['yn'], dx1b, 'tn', name=n('w_out'))
    dy_ssm, gs['g_ssm'] = rmsnorm_bwd(sv['y_ssm'], ws['g_ssm'], dyn, None, name=n('out_norm_ssm'), col=0)
    do, gs['g_mla'] = rmsnorm_bwd(sv['o'], ws['g_mla'], dyn, None, name=n('out_norm_mla'), col=1)
    dq, dk, dv = flash_bwd(sv['q'], sv['k'], sv['v'], sv['o'], do, sv['lse'], name=n('flash'))
    (dproj_m, cqn, ckvn, dqr, dkr, dvb, gs['q_norm'], gs['kv_norm'], gs['q_gain'], gs['k_gain']) = mla_prep_bwd(
        sv['proj'], tabs, sv['mw'], dq, dk, dv, name=n('mla_prep'))
    gb['wq'] = mm(cqn, dqr, 'tn', name=n('wq'))
    gb['wk'] = mm(ckvn, dkr, 'tn', name=n('wk'))
    gb['wv'] = mm(ckvn, dvb, 'tn', name=n('wv'))
    dypre, yg, dz, gs['b_glu'] = glu_bwd(sv['ypre'], dy_ssm, wb['w_glu'], ws['b_glu'], name=n('glu'))
    gb['w_glu'] = mm(yg, dz, 'tn', name=n('w_glu'))
    u_p = _perm(sv['proj'][:, :SSM_W])
    du_p, gs['ar'], gs['ai'], gs['bre'], gs['bim'], gs['cre'], gs['cim'], gs['d'] = s5_bwd(u_p, _perm(dypre), ws, name=n('s5'))
    dproj = jnp.concatenate([_unperm(du_p), dproj_m], axis=1)
    dprojb = dproj.astype(BF)
    dh1 = mm(dprojb, wb['w_in'], 'nt', name=n('w_in_dx'))
    gb['w_in'] = mm(sv['h1'], dprojb, 'tn', name=n('w_in'))
    dx0, gs['norm_mix'] = rmsnorm_bwd(sv['x'], ws['norm_mix'], dh1, dx1, name=n('norm_mix'))
    return dx0, gb, gs


def local_step(x, mem, positions, target, big, small):
    t = x.shape[0]
    tabs = rope_tables(positions)
    big_struct = {k: _sds(big[k].shape[1:], F32) for k in BIG}
    layers = []
    for l in range(DEPTH):
        wb = prep_big({k: big[k][l] for k in BIG})
        ws, small_vjp = jax.vjp(functools.partial(prep_small, t), {k: small[k][l] for k in SMALL})
        x, sv = layer_fwd(l, x, mem, tabs, wb, ws)
        layers.append((wb, ws, small_vjp, sv))
    dx, lcols = loss_fwd_bwd(x, target, name='loss')
    loss = jnp.sum(lcols)
    gbig, gsmall = [None] * DEPTH, [None] * DEPTH
    for l in reversed(range(DEPTH)):
        wb, ws, small_vjp, sv = layers[l]
        dx, gb, gs = layer_bwd(l, dx, mem, tabs, wb, ws, sv)
        gs['pr'], gs['pi'] = jnp.zeros_like(ws['pr']), jnp.zeros_like(ws['pi'])
        gbig[l] = jax.linear_transpose(prep_big, big_struct)(gb)[0]
        gsmall[l] = small_vjp(gs)[0]
    return loss, dx, gbig, gsmall


def _peer(k):
    x, y, c = lax.axis_index('x'), lax.axis_index('y'), lax.axis_index('c')
    px, py, pc = x ^ ((k >> 2) & 1), y ^ ((k >> 1) & 1), c ^ (k & 1)
    return (px, py, pc), 4 * px + 2 * py + pc


def exchange(a, b, *, name):
    ins = [v for v in (a, b) if v is not None]
    n_in = len(ins)
    outs = []
    if a is not None:
        outs.append(_sds(a.shape, a.dtype))
    if b is not None:
        outs.append(_sds((NDEV,) + b.shape, b.dtype))

    def body(*refs):
        in_refs, out_refs = refs[:n_in], refs[n_in:2 * n_in]
        send_sems, recv_sems, loc_sems = refs[2 * n_in:]
        _, me = _peer(0)
        pairs = []
        if a is not None:
            pairs.append((lambda p, r=in_refs[0]: r.at[p], out_refs[0]))
        if b is not None:
            pairs.append((lambda p, r=in_refs[-1]: r, out_refs[-1]))
        local = [pltpu.make_async_copy(src(me), dst.at[me], loc_sems.at[i]) for i, (src, dst) in enumerate(pairs)]
        for cp in local:
            cp.start()
        sends, recvs = [], []
        for k in range(1, NDEV):
            dev, p = _peer(k)
            for i, (src, dst) in enumerate(pairs):
                sends.append(pltpu.make_async_remote_copy(src_ref=src(p), dst_ref=dst.at[me], send_sem=send_sems.at[i, k - 1],
                                                          recv_sem=recv_sems.at[i, k - 1], device_id=dev,
                                                          device_id_type=pl.DeviceIdType.MESH))
                recvs.append(pltpu.make_async_remote_copy(src_ref=src(p), dst_ref=dst.at[p], send_sem=send_sems.at[i, k - 1],
                                                          recv_sem=recv_sems.at[i, k - 1], device_id=dev,
                                                          device_id_type=pl.DeviceIdType.MESH))
        for cp in sends:
            cp.start()
        for cp in sends:
            cp.wait_send()
        for cp in recvs:
            cp.wait_recv()
        for cp in local:
            cp.wait()

    anyspec = pl.BlockSpec(memory_space=pl.ANY)
    res = pl.pallas_call(
        body, name=name, in_specs=[anyspec] * n_in, out_specs=[anyspec] * n_in, out_shape=outs,
        scratch_shapes=[pltpu.SemaphoreType.DMA((n_in, NDEV - 1)), pltpu.SemaphoreType.DMA((n_in, NDEV - 1)),
                        pltpu.SemaphoreType.DMA((n_in,))],
    )(*ins)
    res = list(res)
    ra = res.pop(0) if a is not None else None
    rb = res.pop(0) if b is not None else None
    return ra, rb


def adamw(w, m, v, g8, *, name, tr):
    r = w.shape[0]
    c1 = 1.0 / (1.0 - ADAM_B1 ** ADAM_STEP)
    c2 = 1.0 / (1.0 - ADAM_B2 ** ADAM_STEP)

    def body(w_ref, m_ref, v_ref, g_ref, go_ref, d_ref, mo_ref, vo_ref):
        g = g_ref[0].astype(F32)
        for i in range(1, NDEV):
            g = g + g_ref[i].astype(F32)
        m_new = ADAM_B1 * m_ref[...] + (1.0 - ADAM_B1) * g
        v_new = ADAM_B2 * v_ref[...] + (1.0 - ADAM_B2) * (g * g)
        go_ref[...] = g
        mo_ref[...] = m_new
        vo_ref[...] = v_new
        d_ref[...] = -ADAM_LR * ((m_new * c1) / (jnp.sqrt(v_new * c2) + ADAM_EPS) + ADAM_WD * w_ref[...])

    row = pl.BlockSpec((tr, D), lambda i: (i, 0))
    return _call(body, name=name, grid=(r // tr,),
                 in_specs=[row, row, row, pl.BlockSpec((NDEV, tr, D), lambda i: (0, i, 0))],
                 out_specs=[row] * 4, out_shape=[_sds((r, D), F32)] * 4, sem=('parallel',), vmem=VMEM_BIG)(w, m, v, g8)


def _flat_rows(parts, rows):
    flat = jnp.concatenate([p.reshape(-1) for p in parts])
    return jnp.pad(flat, (0, rows * D - flat.shape[0])).reshape(rows, D)


def _unflat(flat2d, shapes):
    flat = flat2d.reshape(-1)
    out, off = [], 0
    for s in shapes:
        n = math.prod(s)
        out.append(flat[off:off + n].reshape(s))
        off += n
    return out


def _to_slots(g, axis):
    l, r, c = g.shape
    if axis == 1:
        return g.reshape(l, NDEV, r // NDEV, c).transpose(1, 0, 2, 3).reshape(NDEV, -1)
    return g.reshape(l, r, NDEV, c // NDEV).transpose(2, 0, 1, 3).reshape(NDEV, -1)


def _from_slots(s, shard_shape, axis):
    l, r, c = shard_shape
    s = s.reshape(NDEV, l, r, c)
    if axis == 1:
        return s.transpose(1, 0, 2, 3).reshape(l, NDEV * r, c)
    return s.transpose(1, 2, 0, 3).reshape(l, r, NDEV * c)


BIG_ROWS = 6144
SMALL_ROWS = 640


def kernel(x, mem, positions, norm_mix, w_in, ssm_lambda_re, ssm_lambda_im, ssm_log_step, ssm_b_re, ssm_b_im, ssm_c_re, ssm_c_im, ssm_d, ssm_w_glu, ssm_b_glu, mla_q_norm, mla_w_uq, mla_kv_norm, mla_w_ukv, mla_q_gain, mla_k_gain, out_norm_ssm, out_norm_mla, w_out, norm_mem_q, norm_mem_kv, mem_w_q, mem_w_kv, mem_q_gain, mem_k_gain, mem_w_o, norm_mlp, mlp_w1, mlp_w2, loss_target, m_norm_mix, m_w_in, m_ssm_lambda_re, m_ssm_lambda_im, m_ssm_log_step, m_ssm_b_re, m_ssm_b_im, m_ssm_c_re, m_ssm_c_im, m_ssm_d, m_ssm_w_glu, m_ssm_b_glu, m_mla_q_norm, m_mla_w_uq, m_mla_kv_norm, m_mla_w_ukv, m_mla_q_gain, m_mla_k_gain, m_out_norm_ssm, m_out_norm_mla, m_w_out, m_norm_mem_q, m_norm_mem_kv, m_mem_w_q, m_mem_w_kv, m_mem_q_gain, m_mem_k_gain, m_mem_w_o, m_norm_mlp, m_mlp_w1, m_mlp_w2, v_norm_mix, v_w_in, v_ssm_lambda_re, v_ssm_lambda_im, v_ssm_log_step, v_ssm_b_re, v_ssm_b_im, v_ssm_c_re, v_ssm_c_im, v_ssm_d, v_ssm_w_glu, v_ssm_b_glu, v_mla_q_norm, v_mla_w_uq, v_mla_kv_norm, v_mla_w_ukv, v_mla_q_gain, v_mla_k_gain, v_out_norm_ssm, v_out_norm_mla, v_w_out, v_norm_mem_q, v_norm_mem_kv, v_mem_w_q, v_mem_w_kv, v_mem_q_gain, v_mem_k_gain, v_mem_w_o, v_norm_mlp, v_mlp_w1, v_mlp_w2):
    wvals = (norm_mix, w_in, ssm_lambda_re, ssm_lambda_im, ssm_log_step, ssm_b_re, ssm_b_im, ssm_c_re, ssm_c_im, ssm_d, ssm_w_glu, ssm_b_glu, mla_q_norm, mla_w_uq, mla_kv_norm, mla_w_ukv, mla_q_gain, mla_k_gain, out_norm_ssm, out_norm_mla, w_out, norm_mem_q, norm_mem_kv, mem_w_q, mem_w_kv, mem_q_gain, mem_k_gain, mem_w_o, norm_mlp, mlp_w1, mlp_w2)
    mvals = (m_norm_mix, m_w_in, m_ssm_lambda_re, m_ssm_lambda_im, m_ssm_log_step, m_ssm_b_re, m_ssm_b_im, m_ssm_c_re, m_ssm_c_im, m_ssm_d, m_ssm_w_glu, m_ssm_b_glu, m_mla_q_norm, m_mla_w_uq, m_mla_kv_norm, m_mla_w_ukv, m_mla_q_gain, m_mla_k_gain, m_out_norm_ssm, m_out_norm_mla, m_w_out, m_norm_mem_q, m_norm_mem_kv, m_mem_w_q, m_mem_w_kv, m_mem_q_gain, m_mem_k_gain, m_mem_w_o, m_norm_mlp, m_mlp_w1, m_mlp_w2)
    vvals = (v_norm_mix, v_w_in, v_ssm_lambda_re, v_ssm_lambda_im, v_ssm_log_step, v_ssm_b_re, v_ssm_b_im, v_ssm_c_re, v_ssm_c_im, v_ssm_d, v_ssm_w_glu, v_ssm_b_glu, v_mla_q_norm, v_mla_w_uq, v_mla_kv_norm, v_mla_w_ukv, v_mla_q_gain, v_mla_k_gain, v_out_norm_ssm, v_out_norm_mla, v_w_out, v_norm_mem_q, v_norm_mem_kv, v_mem_w_q, v_mem_w_kv, v_mem_q_gain, v_mem_k_gain, v_mem_w_o, v_norm_mlp, v_mlp_w1, v_mlp_w2)
    w = dict(zip(WEIGHTS, wvals))
    m = dict(zip(WEIGHTS, mvals))
    v = dict(zip(WEIGHTS, vvals))

    shard_shapes = {k: w[k].shape for k in BIG}
    mine = _flat_rows([w[k].astype(BF) for k in BIG], BIG_ROWS)
    _, gathered = exchange(None, mine, name='gather_weights')
    gathered = gathered.reshape(NDEV, -1)
    big, off = {}, 0
    for k in BIG:
        n = math.prod(shard_shapes[k])
        big[k] = _from_slots(gathered[:, off:off + n], shard_shapes[k], BIG_AXIS[k])
        off += n
    small = {k: w[k] for k in SMALL}

    loss, grad_x, gbig, gsmall = local_step(x[0], mem[0], positions[0], loss_target[0], big, small)

    gb_full = {k: jnp.stack([gbig[l][k] for l in range(DEPTH)]) for k in BIG}
    slots = jnp.concatenate([_to_slots(gb_full[k], BIG_AXIS[k]) for k in BIG], axis=1)
    slots = jnp.pad(slots, ((0, 0), (0, BIG_ROWS * D - slots.shape[1]))).astype(BF).reshape(NDEV, BIG_ROWS, D)
    gs_full = [jnp.stack([gsmall[l][k] for l in range(DEPTH)]) for k in SMALL]
    small_flat = _flat_rows(gs_full + [loss.reshape(1)], SMALL_ROWS)
    g8_big, g8_small = exchange(slots, small_flat, name='exchange_grads')

    big_shapes = [shard_shapes[k] for k in BIG]
    small_shapes = [w[k].shape for k in SMALL]
    gb, db, mb, vb = adamw(_flat_rows([w[k] for k in BIG], BIG_ROWS), _flat_rows([m[k] for k in BIG], BIG_ROWS),
                           _flat_rows([v[k] for k in BIG], BIG_ROWS), g8_big, name='adamw_big', tr=256)
    gs, ds, ms, vs = adamw(_flat_rows([w[k] for k in SMALL], SMALL_ROWS), _flat_rows([m[k] for k in SMALL], SMALL_ROWS),
                           _flat_rows([v[k] for k in SMALL], SMALL_ROWS), g8_small, name='adamw_small', tr=128)
    n_small = sum(math.prod(s) for s in small_shapes)
    loss_all = gs.reshape(-1)[n_small]

    res = {}
    for tag, fb, fs in (('g', gb, gs), ('d', db, ds), ('m', mb, ms), ('v', vb, vs)):
        res[tag] = dict(zip(BIG, _unflat(fb, big_shapes)))
        res[tag].update(zip(SMALL, _unflat(fs, small_shapes)))
    return (loss_all, grad_x[None], *[res['g'][k] for k in WEIGHTS], *[res['d'][k] for k in WEIGHTS],
            *[res['m'][k] for k in WEIGHTS], *[res['v'][k] for k in WEIGHTS])
```

```python
import functools
import math

import jax
import jax.numpy as jnp
from jax import lax
from jax.experimental import pallas as pl
from jax.experimental.pallas import tpu as pltpu

F32 = jnp.float32
BF = jnp.bfloat16

D = 1024
DEPTH = 4
N_MEM = 256
MEM_HEADS = 4
MEM_HD = 64
SSM_W = 512
SSM_G = 32
SSM_H = 16
SSM_P = 64
MLA_HEADS = 8
QK_NOPE = 64
QK_ROPE = 32
QK_DIM = 96
V_DIM = 64
Q_LORA = 256
KV_LORA = 128
ROPE_THETA = 10000.0
D_FF = 4096
IN_COLS = 928
EPS = 1e-6
NDEV = 8
LANES = 128
SEGS = 8
S5_LW = 256
S5_NHB = (SSM_G * SSM_P) // S5_LW
ADAM_LR = 0.001
ADAM_B1 = 0.9
ADAM_B2 = 0.999
ADAM_EPS = 1e-08
ADAM_WD = 0.01
ADAM_STEP = 10
VMEM_BIG = 56 * 1024 * 1024

NN = (((1,), (0,)), ((), ()))
NT = (((1,), (1,)), ((), ()))
TN = (((0,), (0,)), ((), ()))

BIG = ('w_in', 'ssm_w_glu', 'mla_w_uq', 'mla_w_ukv', 'w_out', 'mem_w_q', 'mem_w_kv', 'mem_w_o', 'mlp_w1', 'mlp_w2')
BIG_AXIS = {'w_in': 1, 'ssm_w_glu': 1, 'mla_w_uq': 2, 'mla_w_ukv': 2, 'w_out': 1, 'mem_w_q': 1, 'mem_w_kv': 1,
            'mem_w_o': 2, 'mlp_w1': 2, 'mlp_w2': 1}
SMALL = ('norm_mix', 'ssm_lambda_re', 'ssm_lambda_im', 'ssm_log_step', 'ssm_b_re', 'ssm_b_im', 'ssm_c_re', 'ssm_c_im',
         'ssm_d', 'ssm_b_glu', 'mla_q_norm', 'mla_kv_norm', 'mla_q_gain', 'mla_k_gain', 'out_norm_ssm', 'out_norm_mla',
         'norm_mem_q', 'norm_mem_kv', 'mem_q_gain', 'mem_k_gain', 'norm_mlp')
WEIGHTS = ('norm_mix', 'w_in', 'ssm_lambda_re', 'ssm_lambda_im', 'ssm_log_step', 'ssm_b_re', 'ssm_b_im', 'ssm_c_re',
           'ssm_c_im', 'ssm_d', 'ssm_w_glu', 'ssm_b_glu', 'mla_q_norm', 'mla_w_uq', 'mla_kv_norm', 'mla_w_ukv',
           'mla_q_gain', 'mla_k_gain', 'out_norm_ssm', 'out_norm_mla', 'w_out', 'norm_mem_q', 'norm_mem_kv', 'mem_w_q',
           'mem_w_kv', 'mem_q_gain', 'mem_k_gain', 'mem_w_o', 'norm_mlp', 'mlp_w1', 'mlp_w2')


def _call(body, *, name, out_shape, grid=(), in_specs=None, out_specs=None, scratch=(), sem=None, vmem=None):
    params = {}
    if sem is not None:
        params['dimension_semantics'] = sem
    if vmem is not None:
        params['vmem_limit_bytes'] = vmem
    specs = {} if in_specs is None else dict(grid=grid, in_specs=in_specs, out_specs=out_specs)
    return pl.pallas_call(body, name=name, out_shape=out_shape, scratch_shapes=list(scratch),
                          compiler_params=pltpu.CompilerParams(**params), **specs)


def _sds(shape, dtype):
    return jax.ShapeDtypeStruct(shape, dtype)


def _dot(a, b, dims=NN):
    return lax.dot_general(a.astype(BF), b.astype(BF), dims, preferred_element_type=F32)


def _split(a):
    hi = a.astype(BF)
    return hi, (a - hi.astype(F32)).astype(BF)


def _dot3(a, b, dims=NN):
    ah, al = _split(a)
    bh, bl = _split(b)
    d = lambda p, q: lax.dot_general(p, q, dims, preferred_element_type=F32)
    return d(ah, bh) + (d(ah, bl) + d(al, bh))


_sdot = _dot


def _rms(x, n):
    r = lax.rsqrt(jnp.sum(x * x, axis=-1, keepdims=True) * (1.0 / n) + EPS)
    return x * r, r


def _rms_bwd(xhat, r, dxhat, n):
    return r * (dxhat - xhat * (jnp.sum(dxhat * xhat, axis=-1, keepdims=True) * (1.0 / n)))


def _colsum(a):
    return jnp.sum(a, axis=0, keepdims=True)


def _tile(t, want):
    return min(t, want)


def _bidx(nb):
    return (lambda b: b) if nb > 1 else (lambda b: 0)


def mm(a, b, mode, *, name, out_dtype=F32, tm=512, tn=512, tk=1024):
    squeeze = a.ndim == 2 and b.ndim == 2
    a = a[None] if a.ndim == 2 else a
    b = b[None] if b.ndim == 2 else b
    nb = max(a.shape[0], b.shape[0])
    ab, bb = _bidx(a.shape[0]), _bidx(b.shape[0])
    if mode in ('nn', 'nt'):
        m, k = a.shape[1:]
        n = b.shape[2] if mode == 'nn' else b.shape[1]
        tm, tn = _tile(m, tm), _tile(n, tn)
        dims = NN if mode == 'nn' else NT

        def body(a_ref, b_ref, o_ref):
            o_ref[...] = _dot(a_ref[...], b_ref[...], dims).astype(o_ref.dtype)

        bspec = (pl.BlockSpec((None, k, tn), lambda bi, i, j: (bb(bi), 0, j)) if mode == 'nn'
                 else pl.BlockSpec((None, tn, k), lambda bi, i, j: (bb(bi), j, 0)))
        out = _call(body, name=name, grid=(nb, m // tm, n // tn),
                    in_specs=[pl.BlockSpec((None, tm, k), lambda bi, i, j: (ab(bi), i, 0)), bspec],
                    out_specs=pl.BlockSpec((None, tm, tn), lambda bi, i, j: (bi, i, j)),
                    out_shape=_sds((nb, m, n), out_dtype), sem=('parallel', 'parallel', 'parallel'))(a, b)
    else:
        k, m = a.shape[1:]
        n = b.shape[2]
        tm, tn, tk = _tile(m, 1024), _tile(n, 1024), _tile(k, 512)

        def body(a_ref, b_ref, o_ref):
            @pl.when(pl.program_id(3) == 0)
            def _():
                o_ref[...] = jnp.zeros_like(o_ref)

            o_ref[...] += _dot(a_ref[...], b_ref[...], TN)

        out = _call(body, name=name, grid=(nb, m // tm, n // tn, k // tk),
                    in_specs=[pl.BlockSpec((None, tk, tm), lambda bi, i, j, kk: (ab(bi), kk, i)),
                              pl.BlockSpec((None, tk, tn), lambda bi, i, j, kk: (bb(bi), kk, j))],
                    out_specs=pl.BlockSpec((None, tm, tn), lambda bi, i, j, kk: (bi, i, j)),
                    out_shape=_sds((nb, m, n), F32), sem=('parallel', 'parallel', 'parallel', 'arbitrary'))(a, b)
    return out[0] if squeeze else out


def rmsnorm_fwd(x, g, *, name, tq=512):
    t, d = x.shape
    tq = _tile(t, tq)

    def body(x_ref, g_ref, o_ref):
        xh, _ = _rms(x_ref[...], d)
        o_ref[...] = (xh * g_ref[...]).astype(o_ref.dtype)

    return _call(body, name=name, grid=(t // tq,),
                 in_specs=[pl.BlockSpec((tq, d), lambda i: (i, 0)), pl.BlockSpec((1, d), lambda i: (0, 0))],
                 out_specs=pl.BlockSpec((tq, d), lambda i: (i, 0)), out_shape=_sds((t, d), BF), sem=('parallel',))(x, g)


def rmsnorm_bwd(x, g, dh, dres, *, name, col=0, tq=512):
    t, d = x.shape
    tq = _tile(t, tq)
    has_res = dres is not None

    def body(*refs):
        if has_res:
            x_ref, g_ref, dh_ref, dres_ref, dx_ref, dg_ref = refs
        else:
            x_ref, g_ref, dh_ref, dx_ref, dg_ref = refs
        xh, r = _rms(x_ref[...], d)
        dh_ = dh_ref[...].astype(F32)
        dx = _rms_bwd(xh, r, dh_ * g_ref[...], d)
        if has_res:
            dx = dx + dres_ref[...]
        dx_ref[...] = dx

        @pl.when(pl.program_id(0) == 0)
        def _():
            dg_ref[...] = jnp.zeros_like(dg_ref)

        dg_ref[...] += _colsum(dh_ * xh)

    in_specs = [pl.BlockSpec((tq, d), lambda i: (i, 0)), pl.BlockSpec((1, d), lambda i: (0, 0)),
                pl.BlockSpec((tq, d), lambda i: (i, col))]
    args = [x, g, dh]
    if has_res:
        in_specs.append(pl.BlockSpec((tq, d), lambda i: (i, 0)))
        args.append(dres)
    return _call(body, name=name, grid=(t // tq,), in_specs=in_specs,
                 out_specs=[pl.BlockSpec((tq, d), lambda i: (i, 0)), pl.BlockSpec((1, d), lambda i: (0, 0))],
                 out_shape=[_sds((t, d), F32), _sds((1, d), F32)], sem=('arbitrary',))(*args)


def _cmul(ar, ai, xr, xi):
    return ar * xr - ai * xi, ar * xi + ai * xr


def _seg_carries(er, ei, pr, pi, reverse):
    lw = er.shape[1]
    zero = jnp.zeros((1, lw), F32)
    order = range(SEGS - 1, -1, -1) if reverse else range(SEGS)
    cin_r, cin_i = [None] * SEGS, [None] * SEGS
    tr, ti = zero, zero
    for j in order:
        cin_r[j], cin_i[j] = tr, ti
        mr, mi = _cmul(pr, pi, tr, ti)
        tr, ti = er[j:j + 1, :] + mr, ei[j:j + 1, :] + mi
    return jnp.concatenate(cin_r, axis=0), jnp.concatenate(cin_i, axis=0)


def _s5_chunk(t):
    return _tile(t, 512)


def s5_fwd(u_p, prm, *, name):
    t = u_p.shape[0]
    ch = _s5_chunk(t)
    nch, steps = t // ch, ch // SEGS
    lw = S5_LW

    def body(u_ref, ar_ref, ai_ref, pr_ref, pi_ref, bre_ref, bim_ref, cre_ref, cim_ref, d_ref, y_ref, bur, bui):
        hb = pl.program_id(0)
        ar = jnp.broadcast_to(ar_ref[0], (SEGS, lw))
        ai = jnp.broadcast_to(ai_ref[0], (SEGS, lw))

        def rows_of(c):
            return pl.ds(pl.multiple_of(c * ch, ch), ch)

        @pl.loop(0, nch)
        def _(c):
            u = u_ref[rows_of(c), :]
            bur[rows_of(c), :] = _sdot(u, bre_ref[0])
            bui[rows_of(c), :] = _sdot(u, bim_ref[0])

        def scan(carry, store):
            def step(i, s):
                r0 = pl.multiple_of(i * SEGS, SEGS)
                mr, mi = _cmul(ar, ai, s[0], s[1])
                nr, ni = mr + bur[pl.ds(r0, SEGS), :], mi + bui[pl.ds(r0, SEGS), :]
                if store:
                    bur[pl.ds(r0, SEGS), :] = nr
                    bui[pl.ds(r0, SEGS), :] = ni
                return nr, ni

            return lax.fori_loop(0, t // SEGS, step, carry, unroll=8)

        zero = jnp.zeros((SEGS, lw), F32)
        er, ei = scan((zero, zero), False)
        scan(_seg_carries(er, ei, pr_ref[0], pi_ref[0], False), True)

        @pl.loop(0, nch)
        def _(c):
            rows = rows_of(c)
            y = _sdot(bur[rows, :], cre_ref[0]) - _sdot(bui[rows, :], cim_ref[0])

            @pl.when(hb % 2 == 0)
            def _():
                y_ref[rows, :] = y + d_ref[...] * u_ref[rows, :]

            @pl.when(hb % 2 == 1)
            def _():
                y_ref[rows, :] += y

    vec = pl.BlockSpec((1, 1, lw), lambda h: (h, 0, 0))
    return _call(
        body, name=name, grid=(S5_NHB,),
        in_specs=[pl.BlockSpec((t, LANES), lambda h: (0, h // 2)), vec, vec, vec, vec,
                  pl.BlockSpec((1, LANES, lw), lambda h: (h, 0, 0)), pl.BlockSpec((1, LANES, lw), lambda h: (h, 0, 0)),
                  pl.BlockSpec((1, lw, LANES), lambda h: (h, 0, 0)), pl.BlockSpec((1, lw, LANES), lambda h: (h, 0, 0)),
                  pl.BlockSpec((1, LANES), lambda h: (0, h // 2))],
        out_specs=pl.BlockSpec((t, LANES), lambda h: (0, h // 2)), out_shape=_sds((t, SSM_W), F32),
        scratch=[pltpu.VMEM((t, lw), F32)] * 2, sem=('arbitrary',), vmem=VMEM_BIG,
    )(u_p, prm['ar'], prm['ai'], prm['pr'], prm['pi'], prm['bre'], prm['bim'], prm['cre'], prm['cim'], prm['d'])


def s5_bwd(u_p, dy_p, prm, *, name):
    t = u_p.shape[0]
    ch = _s5_chunk(t)
    nch, steps = t // ch, ch // SEGS
    lw = S5_LW

    def body(u_ref, dy_ref, ar_ref, ai_ref, pr_ref, pi_ref, bre_ref, bim_ref, cre_ref, cim_ref, d_ref,
             du_ref, dar_ref, dai_ref, dbre_ref, dbim_ref, dcre_ref, dcim_ref, dd_ref, bur, bui, sr, si):
        hb = pl.program_id(0)
        ar = jnp.broadcast_to(ar_ref[0], (SEGS, lw))
        ai = jnp.broadcast_to(ai_ref[0], (SEGS, lw))
        zero = jnp.zeros((SEGS, lw), F32)

        def rows_of(c):
            return pl.ds(pl.multiple_of(c * ch, ch), ch)

        nsteps = t // SEGS

        @pl.loop(0, nch)
        def _(c):
            u = u_ref[rows_of(c), :]
            bur[rows_of(c), :] = _sdot(u, bre_ref[0])
            bui[rows_of(c), :] = _sdot(u, bim_ref[0])

        def fwd_scan(carry, store):
            def step(i, s):
                r0 = pl.multiple_of(i * SEGS, SEGS)
                mr, mi = _cmul(ar, ai, s[0], s[1])
                nr, ni = mr + bur[pl.ds(r0, SEGS), :], mi + bui[pl.ds(r0, SEGS), :]
                if store:
                    w0 = pl.multiple_of(i * SEGS + SEGS, SEGS)
                    sr[pl.ds(w0, SEGS), :] = nr
                    si[pl.ds(w0, SEGS), :] = ni
                return nr, ni

            return lax.fori_loop(0, nsteps, step, carry, unroll=8)

        er, ei = fwd_scan((zero, zero), False)
        cin_r, cin_i = _seg_carries(er, ei, pr_ref[0], pi_ref[0], False)
        sr[pl.ds(0, SEGS), :] = cin_r
        si[pl.ds(0, SEGS), :] = cin_i
        fwd_scan((cin_r, cin_i), True)

        @pl.loop(0, nch)
        def _(c):
            dy = dy_ref[rows_of(c), :]
            bur[rows_of(c), :] = _sdot(dy, cre_ref[0], NT)
            bui[rows_of(c), :] = -_sdot(dy, cim_ref[0], NT)

        def rev_local(ii, lam):
            r0 = pl.multiple_of((nsteps - 1 - ii) * SEGS, SEGS)
            mr, mi = _cmul(ar, -ai, lam[0], lam[1])
            return mr + bur[pl.ds(r0, SEGS), :], mi + bui[pl.ds(r0, SEGS), :]

        lr0, li0 = lax.fori_loop(0, nsteps, rev_local, (zero, zero), unroll=8)
        rin = _seg_carries(lr0, li0, pr_ref[0], -pi_ref[0], True)

        def rev_step(ii, st):
            lam_r, lam_i, acc_r, acc_i = st
            r0 = pl.multiple_of((nsteps - 1 - ii) * SEGS, SEGS)
            mr, mi = _cmul(ar, -ai, lam_r, lam_i)
            nr, ni = mr + bur[pl.ds(r0, SEGS), :], mi + bui[pl.ds(r0, SEGS), :]
            bur[pl.ds(r0, SEGS), :] = nr
            bui[pl.ds(r0, SEGS), :] = ni
            pr_, pi_ = sr[pl.ds(r0, SEGS), :], si[pl.ds(r0, SEGS), :]
            return nr, ni, acc_r + (nr * pr_ + ni * pi_), acc_i + (ni * pr_ - nr * pi_)

        _, _, acc_r, acc_i = lax.fori_loop(0, nsteps, rev_step, (rin[0], rin[1], zero, zero), unroll=8)
        dar_ref[0] = _colsum(acc_r)
        dai_ref[0] = _colsum(acc_i)

        dbre_ref[...] = jnp.zeros_like(dbre_ref)
        dbim_ref[...] = jnp.zeros_like(dbim_ref)
        dcre_ref[...] = jnp.zeros_like(dcre_ref)
        dcim_ref[...] = jnp.zeros_like(dcim_ref)

        @pl.loop(0, nch)
        def _(c):
            rows = rows_of(c)
            u = u_ref[rows, :]
            dy = dy_ref[rows, :]
            lam_r, lam_i = bur[rows, :], bui[rows, :]
            du = _sdot(lam_r, bre_ref[0], NT) + _sdot(lam_i, bim_ref[0], NT)

            @pl.when(hb % 2 == 0)
            def _():
                du_ref[rows, :] = du + d_ref[...] * dy

            @pl.when(hb % 2 == 1)
            def _():
                du_ref[rows, :] += du

            dbre_ref[0] += _sdot(u, lam_r, TN)
            dbim_ref[0] += _sdot(u, lam_i, TN)
            srows = pl.ds(pl.multiple_of(c * ch + SEGS, SEGS), ch)
            dcre_ref[0] += _sdot(sr[srows, :], dy, TN)
            dcim_ref[0] -= _sdot(si[srows, :], dy, TN)

        @pl.when(hb % 2 == 0)
        def _():
            dd_ref[...] = _colsum(dy_ref[...] * u_ref[...])

    vec = pl.BlockSpec((1, 1, lw), lambda h: (h, 0, 0))
    bsp = pl.BlockSpec((1, LANES, lw), lambda h: (h, 0, 0))
    csp = pl.BlockSpec((1, lw, LANES), lambda h: (h, 0, 0))
    act = pl.BlockSpec((t, LANES), lambda h: (0, h // 2))
    dsp = pl.BlockSpec((1, LANES), lambda h: (0, h // 2))
    return _call(
        body, name=name, grid=(S5_NHB,),
        in_specs=[act, act, vec, vec, vec, vec, bsp, bsp, csp, csp, dsp],
        out_specs=[act, vec, vec, bsp, bsp, csp, csp, dsp],
        out_shape=[_sds((t, SSM_W), F32), _sds((S5_NHB, 1, lw), F32), _sds((S5_NHB, 1, lw), F32),
                   _sds((S5_NHB, LANES, lw), F32), _sds((S5_NHB, LANES, lw), F32),
                   _sds((S5_NHB, lw, LANES), F32), _sds((S5_NHB, lw, LANES), F32), _sds((1, SSM_W), F32)],
        scratch=[pltpu.VMEM((t, lw), F32), pltpu.VMEM((t, lw), F32),
                 pltpu.VMEM((t + SEGS, lw), F32), pltpu.VMEM((t + SEGS, lw), F32)],
        sem=('arbitrary',), vmem=VMEM_BIG,
    )(u_p, dy_p, prm['ar'], prm['ai'], prm['pr'], prm['pi'], prm['bre'], prm['bim'], prm['cre'], prm['cim'], prm['d'])


def s5_prep(t, lam_re, lam_im, log_step, b_re, b_im, c_re, c_im):
    step = jnp.exp(log_step)[:, None]
    mag = jnp.exp(lam_re * step)
    ar, ai = mag * jnp.cos(lam_im * step), mag * jnp.sin(lam_im * step)
    den = lam_re * lam_re + lam_im * lam_im
    nr, ni = ar - 1.0, ai
    fr, fi = (nr * lam_re + ni * lam_im) / den, (ni * lam_re - nr * lam_im) / den
    bbr = fr[..., None] * b_re - fi[..., None] * b_im
    bbi = fr[..., None] * b_im + fi[..., None] * b_re
    gl = S5_LW // SSM_P
    eye = jnp.eye(gl, dtype=F32)
    half = (jnp.arange(S5_NHB) % 2)[:, None, None]

    def bmat(bb):
        x = bb.transpose(0, 2, 1).reshape(S5_NHB, gl, SSM_H, SSM_P)
        x = jnp.einsum('bghp,gk->bghkp', x, eye).reshape(S5_NHB, gl * SSM_H, S5_LW)
        z = jnp.zeros_like(x)
        return jnp.where(half == 0, jnp.concatenate([x, z], axis=1), jnp.concatenate([z, x], axis=1))

    def cmat(cc):
        x = cc.transpose(0, 2, 1).reshape(S5_NHB, gl, SSM_P, SSM_H)
        x = jnp.einsum('bgph,gk->bgpkh', x, eye).reshape(S5_NHB, S5_LW, gl * SSM_H)
        z = jnp.zeros_like(x)
        return jnp.where(half == 0, jnp.concatenate([x, z], axis=2), jnp.concatenate([z, x], axis=2))

    vec = lambda a: a.reshape(S5_NHB, 1, S5_LW)
    ni_steps = float(t // SEGS)
    pmag = jnp.exp(lam_re * step * ni_steps)
    pr, pi = pmag * jnp.cos(lam_im * step * ni_steps), pmag * jnp.sin(lam_im * step * ni_steps)
    return dict(ar=vec(ar), ai=vec(ai), bre=bmat(bbr), bim=bmat(bbi), cre=cmat(c_re), cim=cmat(c_im),
                pr=lax.stop_gradient(vec(pr)), pi=lax.stop_gradient(vec(pi)))


def _gelu(x):
    c = math.sqrt(2.0 / math.pi)
    return 0.5 * x * (1.0 + jnp.tanh(c * (x + 0.044715 * (x * x * x))))


def _gelu_grad(x):
    c = math.sqrt(2.0 / math.pi)
    th = jnp.tanh(c * (x + 0.044715 * (x * x * x)))
    return 0.5 * (1.0 + th) + 0.5 * x * (1.0 - th * th) * (c * (1.0 + 3.0 * 0.044715 * (x * x)))


def glu_fwd(ypre, w_glu, b_glu, *, name, tq=512):
    t = ypre.shape[0]
    tq = _tile(t, tq)

    def body(y_ref, w_ref, b_ref, o_ref):
        yg = _gelu(y_ref[...])
        z = _dot(yg, w_ref[...]) + b_ref[...]
        o_ref[...] = yg * jax.nn.sigmoid(z)

    return _call(body, name=name, grid=(t // tq,),
                 in_specs=[pl.BlockSpec((tq, SSM_W), lambda i: (i, 0)), pl.BlockSpec((SSM_W, SSM_W), lambda i: (0, 0)),
                           pl.BlockSpec((1, SSM_W), lambda i: (0, 0))],
                 out_specs=pl.BlockSpec((tq, SSM_W), lambda i: (i, 0)), out_shape=_sds((t, SSM_W), F32),
                 sem=('parallel',))(ypre, w_glu, b_glu)


def glu_bwd(ypre, dy, w_glu, b_glu, *, name, tq=512):
    t = ypre.shape[0]
    tq = _tile(t, tq)

    def body(y_ref, dy_ref, w_ref, b_ref, dyp_ref, yg_ref, dz_ref, db_ref):
        ypre_ = y_ref[...]
        yg = _gelu(ypre_)
        sig = jax.nn.sigmoid(_dot(yg, w_ref[...]) + b_ref[...])
        dy_ = dy_ref[...]
        dz = dy_ * yg * sig * (1.0 - sig)
        dyg = dy_ * sig + _dot(dz, w_ref[...], NT)
        dyp_ref[...] = dyg * _gelu_grad(ypre_)
        yg_ref[...] = yg.astype(BF)
        dz_ref[...] = dz.astype(BF)

        @pl.when(pl.program_id(0) == 0)
        def _():
            db_ref[...] = jnp.zeros_like(db_ref)

        db_ref[...] += _colsum(dz)

    row = pl.BlockSpec((tq, SSM_W), lambda i: (i, 0))
    vec = pl.BlockSpec((1, SSM_W), lambda i: (0, 0))
    return _call(body, name=name, grid=(t // tq,),
                 in_specs=[row, row, pl.BlockSpec((SSM_W, SSM_W), lambda i: (0, 0)), vec],
                 out_specs=[row, row, row, vec],
                 out_shape=[_sds((t, SSM_W), F32), _sds((t, SSM_W), BF), _sds((t, SSM_W), BF), _sds((1, SSM_W), F32)],
                 sem=('arbitrary',))(ypre, dy, w_glu, b_glu)


def _rope(x, cos, sa, sb):
    return x * cos + pltpu.roll(x, 16, 1) * sa + pltpu.roll(x, 112, 1) * sb


def _rope_t(d, cos, sa, sb):
    return d * cos + pltpu.roll(d * sa, 112, 1) + pltpu.roll(d * sb, 16, 1)


def rope_tables(positions):
    half = QK_ROPE // 2
    inv_freq = ROPE_THETA ** (-jnp.arange(half, dtype=F32) / half)
    ang = positions.astype(F32)[:, None] * inv_freq
    cos, sin = jnp.cos(ang), jnp.sin(ang)
    t = positions.shape[0]
    one, zero = jnp.ones((t, QK_NOPE), F32), jnp.zeros((t, QK_NOPE), F32)
    pad1, pad0 = jnp.ones((t, 32), F32), jnp.zeros((t, 32), F32)
    z16 = jnp.zeros((t, half), F32)
    return (jnp.concatenate([one, cos, cos, pad1], axis=1), jnp.concatenate([zero, z16, sin, pad0], axis=1),
            jnp.concatenate([zero, -sin, z16, pad0], axis=1))


def mla_prep_fwd(proj, tabs, w, *, name):
    t = proj.shape[0]
    tq = _tile(t, ATT_BLK)

    def body(cq_ref, ckv_ref, kr_ref, cos_ref, sa_ref, sb_ref, qn_ref, kvn_ref, wq_ref, wk_ref, wv_ref, qg_ref, kg_ref,
             q_ref, qt_ref, k_ref, kt_ref, v_ref):
        cqn = (_rms(cq_ref[...], Q_LORA)[0] * qn_ref[...]).astype(BF)
        ckvn = (_rms(ckv_ref[...], KV_LORA)[0] * kvn_ref[...]).astype(BF)
        cos, sa, sb = cos_ref[...], sa_ref[...], sb_ref[...]
        kr = kr_ref[...]
        for h in range(MLA_HEADS):
            q = _rms(_dot(cqn, wq_ref[h]), QK_DIM)[0] * qg_ref[...]
            q = _rope(q, cos, sa, sb) * ATT_SCALE
            q_ref[h] = q.astype(BF)
            qt_ref[h, 0] = q.T.astype(BF)
            k = _rms(_dot(ckvn, wk_ref[h]) + kr, QK_DIM)[0] * kg_ref[...]
            k = _rope(k, cos, sa, sb)
            k_ref[h] = k.astype(BF)
            kt_ref[h, 0] = k.T.astype(BF)
            v_ref[h] = _dot(ckvn, wv_ref[h]).astype(BF)

    tab = pl.BlockSpec((tq, LANES), lambda i: (i, 0))
    full = lambda shape: pl.BlockSpec(shape, lambda i: (0,) * len(shape))
    hout = pl.BlockSpec((MLA_HEADS, tq, LANES), lambda i: (0, i, 0))
    tout = pl.BlockSpec((MLA_HEADS, 1, LANES, tq), lambda i: (0, i, 0, 0))
    hshape = _sds((MLA_HEADS, t, LANES), BF)
    tshape = _sds((MLA_HEADS, t // tq, LANES, tq), BF)
    return _call(
        body, name=name, grid=(t // tq,),
        in_specs=[pl.BlockSpec((tq, Q_LORA), lambda i: (i, 2)), pl.BlockSpec((tq, LANES), lambda i: (i, 6)),
                  pl.BlockSpec((tq, LANES), lambda i: (i, 7)), tab, tab, tab,
                  full((1, Q_LORA)), full((1, KV_LORA)), full((MLA_HEADS, Q_LORA, LANES)),
                  full((MLA_HEADS, KV_LORA, LANES)), full((MLA_HEADS, KV_LORA, LANES)), full((1, LANES)), full((1, LANES))],
        out_specs=[hout, tout, hout, tout, hout], out_shape=[hshape, tshape, hshape, tshape, hshape], sem=('parallel',),
    )(proj, proj, proj, *tabs, w['q_norm'], w['kv_norm'], w['wq'], w['wk'], w['wv'], w['q_gain'], w['k_gain'])


def mla_prep_bwd(proj, tabs, w, dq, dk, dv, *, name):
    t = proj.shape[0]
    tq = _tile(t, ATT_BLK)

    def body(cq_ref, ckv_ref, kr_ref, cos_ref, sa_ref, sb_ref, qn_ref, kvn_ref, wq_ref, wk_ref, wv_ref, qg_ref, kg_ref,
             dq_ref, dk_ref, dv_ref,
             dpm_ref, cqn_ref, ckvn_ref, dqr_ref, dkraw_ref, dvb_ref, dqn_ref, dkvn_ref, dqg_ref, dkg_ref):
        cq_h, cq_r = _rms(cq_ref[...], Q_LORA)
        ckv_h, ckv_r = _rms(ckv_ref[...], KV_LORA)
        cqn = (cq_h * qn_ref[...]).astype(BF)
        ckvn = (ckv_h * kvn_ref[...]).astype(BF)
        cqn_ref[...] = cqn
        ckvn_ref[...] = ckvn
        cos, sa, sb = cos_ref[...], sa_ref[...], sb_ref[...]
        kr = kr_ref[...]
        dcqn = jnp.zeros((tq, Q_LORA), F32)
        dckvn = jnp.zeros((tq, KV_LORA), F32)
        dkrope = jnp.zeros((tq, LANES), F32)
        dqg = jnp.zeros((1, LANES), F32)
        dkg = jnp.zeros((1, LANES), F32)
        for h in range(MLA_HEADS):
            qh, qr = _rms(_dot(cqn, wq_ref[h]), QK_DIM)
            dqo = _rope_t(dq_ref[h, 0].T * ATT_SCALE, cos, sa, sb)
            dqg = dqg + _colsum(dqo * qh)
            dqraw = _rms_bwd(qh, qr, dqo * qg_ref[...], QK_DIM).astype(BF)
            dqr_ref[h] = dqraw
            dcqn = dcqn + _dot(dqraw, wq_ref[h], NT)
            kh, krs = _rms(_dot(ckvn, wk_ref[h]) + kr, QK_DIM)
            dko = _rope_t(dk_ref[h], cos, sa, sb)
            dkg = dkg + _colsum(dko * kh)
            dkraw = _rms_bwd(kh, krs, dko * kg_ref[...], QK_DIM)
            dkrope = dkrope + dkraw
            dkraw = dkraw.astype(BF)
            dkraw_ref[h] = dkraw
            dvb = dv_ref[h].astype(BF)
            dvb_ref[h] = dvb
            dckvn = dckvn + _dot(dkraw, wk_ref[h], NT) + _dot(dvb, wv_ref[h], NT)
        dpm_ref[:, 0:Q_LORA] = _rms_bwd(cq_h, cq_r, dcqn * qn_ref[...], Q_LORA)
        dpm_ref[:, Q_LORA:Q_LORA + KV_LORA] = _rms_bwd(ckv_h, ckv_r, dckvn * kvn_ref[...], KV_LORA)
        dpm_ref[:, Q_LORA + KV_LORA:512] = dkrope

        @pl.when(pl.program_id(0) == 0)
        def _():
            dqn_ref[...] = jnp.zeros_like(dqn_ref)
            dkvn_ref[...] = jnp.zeros_like(dkvn_ref)
            dqg_ref[...] = jnp.zeros_like(dqg_ref)
            dkg_ref[...] = jnp.zeros_like(dkg_ref)

        dqn_ref[...] += _colsum(dcqn * cq_h)
        dkvn_ref[...] += _colsum(dckvn * ckv_h)
        dqg_ref[...] += dqg
        dkg_ref[...] += dkg

    tab = pl.BlockSpec((tq, LANES), lambda i: (i, 0))
    full = lambda shape: pl.BlockSpec(shape, lambda i: (0,) * len(shape))
    hblk = pl.BlockSpec((MLA_HEADS, tq, LANES), lambda i: (0, i, 0))
    return _call(
        body, name=name, grid=(t // tq,),
        in_specs=[pl.BlockSpec((tq, Q_LORA), lambda i: (i, 2)), pl.BlockSpec((tq, LANES), lambda i: (i, 6)),
                  pl.BlockSpec((tq, LANES), lambda i: (i, 7)), tab, tab, tab,
                  full((1, Q_LORA)), full((1, KV_LORA)), full((MLA_HEADS, Q_LORA, LANES)),
                  full((MLA_HEADS, KV_LORA, LANES)), full((MLA_HEADS, KV_LORA, LANES)), full((1, LANES)), full((1, LANES)),
                  pl.BlockSpec((MLA_HEADS, 1, LANES, tq), lambda i: (0, i, 0, 0)), hblk, hblk],
        out_specs=[pl.BlockSpec((tq, 512), lambda i: (i, 0)),
                   pl.BlockSpec((tq, Q_LORA), lambda i: (i, 0)), pl.BlockSpec((tq, KV_LORA), lambda i: (i, 0)),
                   hblk, hblk, hblk, full((1, Q_LORA)), full((1, KV_LORA)), full((1, LANES)), full((1, LANES))],
        out_shape=[_sds((t, 512), F32), _sds((t, Q_LORA), BF), _sds((t, KV_LORA), BF),
                   _sds((MLA_HEADS, t, LANES), BF), _sds((MLA_HEADS, t, LANES), BF), _sds((MLA_HEADS, t, LANES), BF),
                   _sds((1, Q_LORA), F32), _sds((1, KV_LORA), F32), _sds((1, LANES), F32), _sds((1, LANES), F32)],
        sem=('arbitrary',),
    )(proj, proj, proj, *tabs, w['q_norm'], w['kv_norm'], w['wq'], w['wk'], w['wv'], w['q_gain'], w['k_gain'], dq, dk, dv)


ATT_BLK = 256
ATT_SCALE = 1.0 / math.sqrt(QK_DIM)


def flash_fwd(q, kt, v, *, name):
    t = q.shape[1]
    blk = _tile(t, ATT_BLK)

    def body(q_ref, kt_ref, v_ref, o_ref, lse_ref):
        qi = pl.program_id(1)
        row = lax.broadcasted_iota(jnp.int32, (blk, blk), 0)
        col = lax.broadcasted_iota(jnp.int32, (blk, blk), 1)

        def block(j, carry, masked):
            out = []
            for hh in range(2):
                m, l, acc = carry[hh]
                s = _dot(q_ref[hh], kt_ref[hh, j])
                if masked:
                    s = jnp.where(col <= row, s, -jnp.inf)
                m2 = jnp.maximum(m, jnp.max(s, axis=-1, keepdims=True))
                p = jnp.exp(s - m2)
                alpha = jnp.exp(m - m2)
                rows = pl.ds(pl.multiple_of(j * blk, blk), blk)
                out.append((m2, alpha * l + jnp.sum(p, axis=-1, keepdims=True), alpha * acc + _dot(p, v_ref[hh, rows, :])))
            return tuple(out)

        init = (jnp.full((blk, 1), -jnp.inf, F32), jnp.zeros((blk, 1), F32), jnp.zeros((blk, LANES), F32))
        carry = lax.fori_loop(0, qi, lambda j, c: block(j, c, False), (init, init))
        carry = block(qi, carry, True)
        o_acc = jnp.zeros((blk, LANES), F32)
        for hh in range(2):
            m, l, acc = carry[hh]
            o_acc = o_acc + acc / l
            lse_ref[hh, 0] = jnp.broadcast_to(m + jnp.log(l), (blk, LANES)).T[0:1, :]
        o_ref[...] = o_acc

    return _call(
        body, name=name, grid=(MLA_HEADS // 2, t // blk),
        in_specs=[pl.BlockSpec((2, blk, LANES), lambda p, i: (p, i, 0)),
                  pl.BlockSpec((2, t // blk, LANES, blk), lambda p, i: (p, 0, 0, 0)),
                  pl.BlockSpec((2, t, LANES), lambda p, i: (p, 0, 0))],
        out_specs=[pl.BlockSpec((blk, LANES), lambda p, i: (i, p)), pl.BlockSpec((2, 1, 1, blk), lambda p, i: (p, i, 0, 0))],
        out_shape=[_sds((t, 512), F32), _sds((MLA_HEADS, t // blk, 1, blk), F32)], sem=('parallel', 'parallel'),
    )(q, kt, v)


def mla_out_bwd(o, dyn, g, *, name):
    t = o.shape[0]
    blk = _tile(t, ATT_BLK)

    def body(o_ref, dh_ref, g_ref, do_ref, dot_ref, delta_ref, dg_ref):
        ov = o_ref[...]
        oh, r = _rms(ov, 512)
        dh = dh_ref[...]
        do = _rms_bwd(oh, r, dh * g_ref[...], 512)
        do_ref[...] = do.astype(BF)
        dd = do * ov
        for pb in range(MLA_HEADS // 2):
            cols = slice(pb * LANES, (pb + 1) * LANES)
            dot_ref[pb, 0] = do[:, cols].T.astype(BF)
            ddt = dd[:, cols].T
            delta_ref[2 * pb, 0] = jnp.sum(ddt[0:V_DIM, :], axis=0, keepdims=True)
            delta_ref[2 * pb + 1, 0] = jnp.sum(ddt[V_DIM:LANES, :], axis=0, keepdims=True)

        @pl.when(pl.program_id(0) == 0)
        def _():
            dg_ref[...] = jnp.zeros_like(dg_ref)

        dg_ref[...] += _colsum(dh * oh)

    return _call(
        body, name=name, grid=(t // blk,),
        in_specs=[pl.BlockSpec((blk, 512), lambda i: (i, 0)), pl.BlockSpec((blk, 512), lambda i: (i, 1)),
                  pl.BlockSpec((1, 512), lambda i: (0, 0))],
        out_specs=[pl.BlockSpec((blk, 512), lambda i: (i, 0)), pl.BlockSpec((MLA_HEADS // 2, 1, LANES, blk), lambda i: (0, i, 0, 0)),
                   pl.BlockSpec((MLA_HEADS, 1, 1, blk), lambda i: (0, i, 0, 0)), pl.BlockSpec((1, 512), lambda i: (0, 0))],
        out_shape=[_sds((t, 512), BF), _sds((MLA_HEADS // 2, t // blk, LANES, blk), BF),
                   _sds((MLA_HEADS, t // blk, 1, blk), F32), _sds((1, 512), F32)],
        sem=('arbitrary',),
    )(o, dyn, g)


def flash_bwd(q, qt, k, kt, v, do, dot, lse, delta, *, name):
    t = q.shape[1]
    blk = _tile(t, ATT_BLK)
    nb = t // blk

    def body(q_ref, qt_ref, k_ref, kt_ref, v_ref, do_ref, dot_ref, lse_ref, delta_ref, dqt_ref, dk_ref, dv_ref):
        h, j = pl.program_id(0), pl.program_id(1)
        row = lax.broadcasted_iota(jnp.int32, (blk, blk), 0)
        col = lax.broadcasted_iota(jnp.int32, (blk, blk), 1)
        lane = lax.broadcasted_iota(jnp.int32, (1, LANES), 1)
        mine = (lane // V_DIM) == (h % 2)

        @pl.when(j == 0)
        def _():
            dqt_ref[...] = jnp.zeros_like(dqt_ref)

        kv, ktv, vv = k_ref[...], kt_ref[...], v_ref[...]

        def block(i, carry, masked):
            dk, dv = carry
            rows = pl.ds(pl.multiple_of(i * blk, blk), blk)
            pt = jnp.exp(_dot(kv, qt_ref[i]) - lse_ref[i])
            if masked:
                pt = jnp.where(col >= row, pt, 0.0)
            dv = dv + _dot(pt, do_ref[rows, :])
            dst = (pt * (_dot(vv, dot_ref[i]) - delta_ref[i])).astype(BF)
            dk = dk + _dot(dst, q_ref[rows, :])
            dqt_ref[i] += _dot(ktv, dst)
            return dk, dv

        zero = jnp.zeros((blk, LANES), F32)
        carry = block(j, (zero, zero), True)
        dk, dv = lax.fori_loop(j + 1, nb, lambda i, c: block(i, c, False), carry)
        dk_ref[...] = dk
        dv_ref[...] = jnp.where(mine, dv, 0.0)

    whole = pl.BlockSpec((None, t, LANES), lambda h, j: (h, 0, 0))
    wholet = pl.BlockSpec((None, nb, LANES, blk), lambda h, j: (h, 0, 0, 0))
    kvb = pl.BlockSpec((None, blk, LANES), lambda h, j: (h, j, 0))
    rowv = pl.BlockSpec((None, nb, 1, blk), lambda h, j: (h, 0, 0, 0))
    return _call(
        body, name=name, grid=(MLA_HEADS, nb),
        in_specs=[whole, wholet, kvb, pl.BlockSpec((None, None, LANES, blk), lambda h, j: (h, j, 0, 0)), kvb,
                  pl.BlockSpec((t, LANES), lambda h, j: (0, h // 2)),
                  pl.BlockSpec((None, nb, LANES, blk), lambda h, j: (h // 2, 0, 0, 0)), rowv, rowv],
        out_specs=[wholet, kvb, kvb],
        out_shape=[_sds((MLA_HEADS, nb, LANES, blk), F32), _sds((MLA_HEADS, t, LANES), F32), _sds((MLA_HEADS, t, LANES), F32)],
        sem=('parallel', 'arbitrary'), vmem=VMEM_BIG,
    )(q, qt, k, kt, v, do, dot, lse, delta)


def mix_out_fwd(x, y_ssm, o, g_ssm, g_mla, w_out, *, name, tq=512):
    t = x.shape[0]
    tq = _tile(t, tq)

    def body(x_ref, ys_ref, o_ref, gs_ref, gm_ref, w_ref, x1_ref, yn_ref):
        ns = (_rms(ys_ref[...], SSM_W)[0] * gs_ref[...]).astype(BF)
        nm = (_rms(o_ref[...], 512)[0] * gm_ref[...]).astype(BF)
        yn_ref[:, 0:SSM_W] = ns
        yn_ref[:, SSM_W:D] = nm
        x1_ref[...] = x_ref[...] + _dot(ns, w_ref[0:SSM_W, :]) + _dot(nm, w_ref[SSM_W:D, :])

    row = lambda w: pl.BlockSpec((tq, w), lambda i: (i, 0))
    vec = pl.BlockSpec((1, 512), lambda i: (0, 0))
    return _call(body, name=name, grid=(t // tq,),
                 in_specs=[row(D), row(512), row(512), vec, vec, pl.BlockSpec((D, D), lambda i: (0, 0))],
                 out_specs=[row(D), row(D)], out_shape=[_sds((t, D), F32), _sds((t, D), BF)], sem=('parallel',),
                 )(x, y_ssm, o, g_ssm, g_mla, w_out)


MEM_SCALE = 1.0 / math.sqrt(MEM_HD)


def memkv_fwd(mem, g, wk, wv, kg, *, name):
    def body(m_ref, g_ref, wk_ref, wv_ref, kg_ref, mh_ref, k_ref, v_ref):
        mh = (_rms(m_ref[...], D)[0] * g_ref[...]).astype(BF)
        mh_ref[...] = mh
        for h in range(MEM_HEADS):
            cols = slice(h * LANES, (h + 1) * LANES)
            k_ref[h] = (_rms(_dot(mh, wk_ref[:, cols]), MEM_HD)[0] * kg_ref[...]).astype(BF)
            v_ref[h] = _dot(mh, wv_ref[:, cols]).astype(BF)

    return _call(body, name=name,
                 out_shape=[_sds((N_MEM, D), BF), _sds((MEM_HEADS, N_MEM, LANES), BF), _sds((MEM_HEADS, N_MEM, LANES), BF)],
                 )(mem, g, wk, wv, kg)


def memkv_bwd(mem, g, wk, wv, kg, dk, dv, *, name):
    def body(m_ref, g_ref, wk_ref, wv_ref, kg_ref, dk_ref, dv_ref, dwk_ref, dwv_ref, dkg_ref, dg_ref):
        mhat, _ = _rms(m_ref[...], D)
        mh = (mhat * g_ref[...]).astype(BF)
        lane = lax.broadcasted_iota(jnp.int32, (1, LANES), 1)
        dkg = jnp.zeros((1, LANES), F32)
        dmh = jnp.zeros((N_MEM, D), F32)
        for h in range(MEM_HEADS):
            cols = slice(h * LANES, (h + 1) * LANES)
            kh, kr = _rms(_dot(mh, wk_ref[:, cols]), MEM_HD)
            dko = dk_ref[h]
            dkg = dkg + _colsum(dko * kh)
            dkraw = _rms_bwd(kh, kr, dko * kg_ref[...], MEM_HD).astype(BF)
            dvh = jnp.where((lane // MEM_HD) == (h % 2), dv_ref[h], 0.0).astype(BF)
            dwk_ref[:, cols] = _dot(mh, dkraw, TN)
            dwv_ref[:, cols] = _dot(mh, dvh, TN)
            dmh = dmh + _dot(dkraw, wk_ref[:, cols], NT) + _dot(dvh, wv_ref[:, cols], NT)
        dkg_ref[...] = dkg
        dg_ref[...] = _colsum(dmh * mhat)

    return _call(body, name=name,
                 out_shape=[_sds((D, 512), F32), _sds((D, 512), F32), _sds((1, LANES), F32), _sds((1, D), F32)],
                 )(mem, g, wk, wv, kg, dk, dv)


def memattn_fwd(x, g, wq, qg, kh, vh, wo, *, name, tq=256):
    t = x.shape[0]
    tq = _tile(t, tq)

    def body(x_ref, g_ref, wq_ref, qg_ref, k_ref, v_ref, wo_ref, x2_ref, hn_ref):
        xv = x_ref[...]
        hn = (_rms(xv, D)[0] * g_ref[...]).astype(BF)
        hn_ref[...] = hn
        out = xv
        for pb in range(MEM_HEADS // 2):
            o = jnp.zeros((tq, LANES), F32)
            for h in (2 * pb, 2 * pb + 1):
                q = _rms(_dot(hn, wq_ref[:, h * LANES:(h + 1) * LANES]), MEM_HD)[0] * qg_ref[...]
                s = _dot(q, k_ref[h], NT) * MEM_SCALE
                p = jnp.exp(s - jnp.max(s, axis=-1, keepdims=True))
                p = p / jnp.sum(p, axis=-1, keepdims=True)
                o = o + _dot(p, v_ref[h])
            out = out + _dot(o, wo_ref[pb * LANES:(pb + 1) * LANES, :])
        x2_ref[...] = out

    full = lambda shape: pl.BlockSpec(shape, lambda i: (0,) * len(shape))
    row = pl.BlockSpec((tq, D), lambda i: (i, 0))
    return _call(body, name=name, grid=(t // tq,),
                 in_specs=[row, full((1, D)), full((D, 512)), full((1, LANES)), full((MEM_HEADS, N_MEM, LANES)),
                           full((MEM_HEADS, N_MEM, LANES)), full((MEM_HEADS * MEM_HD, D))],
                 out_specs=[row, row], out_shape=[_sds((t, D), F32), _sds((t, D), BF)], sem=('parallel',),
                 )(x, g, wq, qg, kh, vh, wo)


def memattn_bwd(x, dx2, g, wq, qg, kh, vh, wo, *, name, tq=256):
    t = x.shape[0]
    tq = _tile(t, tq)

    def body(x_ref, dx2_ref, g_ref, wq_ref, qg_ref, k_ref, v_ref, wo_ref,
             dx_ref, o_ref, dqr_ref, dk_ref, dv_ref, dqg_ref, dg_ref):
        @pl.when(pl.program_id(0) == 0)
        def _():
            dk_ref[...] = jnp.zeros_like(dk_ref)
            dv_ref[...] = jnp.zeros_like(dv_ref)
            dqg_ref[...] = jnp.zeros_like(dqg_ref)
            dg_ref[...] = jnp.zeros_like(dg_ref)

        xhat, xr = _rms(x_ref[...], D)
        hn = (xhat * g_ref[...]).astype(BF)
        dx2 = dx2_ref[...]
        dx2b = dx2.astype(BF)
        dh = jnp.zeros((tq, D), F32)
        dqg = jnp.zeros((1, LANES), F32)
        for pb in range(MEM_HEADS // 2):
            do = _dot(dx2b, wo_ref[pb * LANES:(pb + 1) * LANES, :], NT).astype(BF)
            o = jnp.zeros((tq, LANES), F32)
            for h in (2 * pb, 2 * pb + 1):
                cols = slice(h * LANES, (h + 1) * LANES)
                qh, qr = _rms(_dot(hn, wq_ref[:, cols]), MEM_HD)
                qb = (qh * qg_ref[...]).astype(BF)
                s = _dot(qb, k_ref[h], NT) * MEM_SCALE
                p = jnp.exp(s - jnp.max(s, axis=-1, keepdims=True))
                p = p / jnp.sum(p, axis=-1, keepdims=True)
                pb16 = p.astype(BF)
                o = o + _dot(pb16, v_ref[h])
                dv_ref[h] += _dot(pb16, do, TN)
                dp = _dot(do, v_ref[h], NT)
                ds = (p * (dp - jnp.sum(dp * p, axis=-1, keepdims=True)) * MEM_SCALE).astype(BF)
                dk_ref[h] += _dot(ds, qb, TN)
                dqo = _dot(ds, k_ref[h])
                dqg = dqg + _colsum(dqo * qh)
                dqraw = _rms_bwd(qh, qr, dqo * qg_ref[...], MEM_HD).astype(BF)
                dqr_ref[:, cols] = dqraw
                dh = dh + _dot(dqraw, wq_ref[:, cols], NT)
            o_ref[:, pb * LANES:(pb + 1) * LANES] = o.astype(BF)
        dx_ref[...] = dx2 + _rms_bwd(xhat, xr, dh * g_ref[...], D)
        dqg_ref[...] += dqg
        dg_ref[...] += _colsum(dh * xhat)

    full = lambda shape: pl.BlockSpec(shape, lambda i: (0,) * len(shape))
    row = lambda w: pl.BlockSpec((tq, w), lambda i: (i, 0))
    return _call(body, name=name, grid=(t // tq,),
                 in_specs=[row(D), row(D), full((1, D)), full((D, 512)), full((1, LANES)), full((MEM_HEADS, N_MEM, LANES)),
                           full((MEM_HEADS, N_MEM, LANES)), full((MEM_HEADS * MEM_HD, D))],
                 out_specs=[row(D), row(256), row(512), full((MEM_HEADS, N_MEM, LANES)), full((MEM_HEADS, N_MEM, LANES)),
                            full((1, LANES)), full((1, D))],
                 out_shape=[_sds((t, D), F32), _sds((t, 256), BF), _sds((t, 512), BF),
                            _sds((MEM_HEADS, N_MEM, LANES), F32), _sds((MEM_HEADS, N_MEM, LANES), F32),
                            _sds((1, LANES), F32), _sds((1, D), F32)],
                 sem=('arbitrary',))(x, dx2, g, wq, qg, kh, vh, wo)


def mlp_fwd(x, h, w1, w2, *, name, tq=1024, tf=512):
    t = x.shape[0]
    tq = _tile(t, tq)

    def body(x_ref, h_ref, w1_ref, w2_ref, o_ref):
        @pl.when(pl.program_id(1) == 0)
        def _():
            o_ref[...] = x_ref[...]

        a = jnp.maximum(_dot(h_ref[...], w1_ref[...]), 0.0)
        o_ref[...] += _dot(a * a, w2_ref[...])

    row = pl.BlockSpec((tq, D), lambda i, f: (i, 0))
    return _call(body, name=name, grid=(t // tq, D_FF // tf),
                 in_specs=[row, row, pl.BlockSpec((D, tf), lambda i, f: (0, f)), pl.BlockSpec((tf, D), lambda i, f: (f, 0))],
                 out_specs=row, out_shape=_sds((t, D), F32), sem=('parallel', 'arbitrary'), vmem=VMEM_BIG)(x, h, w1, w2)


def mlp_bwd(h, dx, w1, w2, *, name, tq=1024, tf=512):
    t = h.shape[0]
    tq = _tile(t, tq)

    def body(h_ref, dx_ref, w1_ref, w2_ref, dh_ref, r_ref, da_ref):
        @pl.when(pl.program_id(1) == 0)
        def _():
            dh_ref[...] = jnp.zeros_like(dh_ref)

        a = jnp.maximum(_dot(h_ref[...], w1_ref[...]), 0.0)
        r_ref[...] = (a * a).astype(BF)
        da = (_dot(dx_ref[...], w2_ref[...], NT) * (2.0 * a)).astype(BF)
        da_ref[...] = da
        dh_ref[...] += _dot(da, w1_ref[...], NT)

    row = pl.BlockSpec((tq, D), lambda i, f: (i, 0))
    act = pl.BlockSpec((tq, tf), lambda i, f: (i, f))
    return _call(body, name=name, grid=(t // tq, D_FF // tf),
                 in_specs=[row, row, pl.BlockSpec((D, tf), lambda i, f: (0, f)), pl.BlockSpec((tf, D), lambda i, f: (f, 0))],
                 out_specs=[row, act, act], out_shape=[_sds((t, D), F32), _sds((t, D_FF), BF), _sds((t, D_FF), BF)],
                 sem=('parallel', 'arbitrary'), vmem=VMEM_BIG)(h, dx, w1, w2)


def loss_fwd_bwd(y, target, *, name, tq=512):
    t = y.shape[0]
    tq = _tile(t, tq)

    def body(y_ref, t_ref, dy_ref, l_ref):
        @pl.when(pl.program_id(0) == 0)
        def _():
            l_ref[...] = jnp.zeros_like(l_ref)

        e = y_ref[...] - t_ref[...]
        dy_ref[...] = e * (1.0 / D)
        l_ref[...] += _colsum(e * e) * (0.5 / D)

    row = pl.BlockSpec((tq, D), lambda i: (i, 0))
    return _call(body, name=name, grid=(t // tq,), in_specs=[row, row],
                 out_specs=[row, pl.BlockSpec((1, D), lambda i: (0, 0))],
                 out_shape=[_sds((t, D), F32), _sds((1, D), F32)], sem=('arbitrary',))(y, target)


def prep_big(w):
    w_in = w['w_in']
    z = lambda r, c: jnp.zeros((r, c), w_in.dtype)
    w_in_pad = jnp.concatenate([w_in[:, :896], z(D, 64), w_in[:, 896:928], z(D, 32)], axis=1)
    wq = w['mla_w_uq'].reshape(Q_LORA, MLA_HEADS, QK_DIM).transpose(1, 0, 2)
    wq = jnp.pad(wq, ((0, 0), (0, 0), (0, LANES - QK_DIM)))
    ukv = w['mla_w_ukv'].reshape(KV_LORA, MLA_HEADS, QK_NOPE + V_DIM).transpose(1, 0, 2)
    wk = jnp.pad(ukv[:, :, :QK_NOPE], ((0, 0), (0, 0), (0, LANES - QK_NOPE)))
    vpart = ukv[:, :, QK_NOPE:]
    zv = jnp.zeros_like(vpart)
    odd = (jnp.arange(MLA_HEADS) % 2)[:, None, None] == 1
    wv = jnp.where(odd, jnp.concatenate([zv, vpart], axis=2), jnp.concatenate([vpart, zv], axis=2))
    mq = jnp.pad(w['mem_w_q'].reshape(D, MEM_HEADS, MEM_HD), ((0, 0), (0, 0), (0, LANES - MEM_HD))).reshape(D, 512)
    mkv = w['mem_w_kv'].reshape(D, MEM_HEADS, 2 * MEM_HD)
    mk = jnp.pad(mkv[:, :, :MEM_HD], ((0, 0), (0, 0), (0, LANES - MEM_HD))).reshape(D, 512)
    mvp = mkv[:, :, MEM_HD:]
    zm = jnp.zeros_like(mvp)
    modd = (jnp.arange(MEM_HEADS) % 2)[None, :, None] == 1
    mv = jnp.where(modd, jnp.concatenate([zm, mvp], axis=2), jnp.concatenate([mvp, zm], axis=2)).reshape(D, 512)
    return dict(w_in=w_in_pad, w_glu=w['ssm_w_glu'], wq=wq, wk=wk, wv=wv, w_out=w['w_out'], mq=mq, mk=mk, mv=mv,
                mo=w['mem_w_o'], w1=w['mlp_w1'], w2=w['mlp_w2'])


def prep_small(t, s):
    row = lambda a: a.reshape(1, -1)
    pad = lambda a: jnp.pad(a, (0, LANES - a.shape[0])).reshape(1, LANES)
    out = s5_prep(t, s['ssm_lambda_re'], s['ssm_lambda_im'], s['ssm_log_step'], s['ssm_b_re'], s['ssm_b_im'],
                  s['ssm_c_re'], s['ssm_c_im'])
    out.update(d=row(s['ssm_d']), norm_mix=row(s['norm_mix']), b_glu=row(s['ssm_b_glu']), q_norm=row(s['mla_q_norm']),
               kv_norm=row(s['mla_kv_norm']), q_gain=pad(s['mla_q_gain']), k_gain=pad(s['mla_k_gain']),
               g_ssm=row(s['out_norm_ssm']), g_mla=row(s['out_norm_mla']), norm_mem_q=row(s['norm_mem_q']),
               norm_mem_kv=row(s['norm_mem_kv']), mem_q_gain=pad(s['mem_q_gain']), mem_k_gain=pad(s['mem_k_gain']),
               norm_mlp=row(s['norm_mlp']))
    return out


def _perm(a):
    t, c = a.shape
    return a.reshape(SEGS, t // SEGS, c).transpose(1, 0, 2).reshape(t, c)


def _unperm(a):
    t, c = a.shape
    return a.reshape(t // SEGS, SEGS, c).transpose(1, 0, 2).reshape(t, c)


def layer_fwd(l, x, mem, tabs, wb, ws):
    n = lambda s: f'l{l}_{s}'
    h1 = rmsnorm_fwd(x, ws['norm_mix'], name=n('norm_mix'))
    proj = mm(h1, wb['w_in'], 'nn', name=n('w_in'))
    ypre_p = s5_fwd(_perm(proj[:, :SSM_W]), ws, name=n('s5'))
    ypre = _unperm(ypre_p)
    y_ssm = glu_fwd(ypre, wb['w_glu'], ws['b_glu'], name=n('glu'))
    mw = dict(q_norm=ws['q_norm'], kv_norm=ws['kv_norm'], wq=wb['wq'], wk=wb['wk'], wv=wb['wv'],
              q_gain=ws['q_gain'], k_gain=ws['k_gain'])
    q, qt, k, kt, v = mla_prep_fwd(proj, tabs, mw, name=n('mla_prep'))
    o, lse = flash_fwd(q, kt, v, name=n('flash'))
    x1, yn = mix_out_fwd(x, y_ssm, o, ws['g_ssm'], ws['g_mla'], wb['w_out'], name=n('mix_out'))
    mh, kh, vh = memkv_fwd(mem, ws['norm_mem_kv'], wb['mk'], wb['mv'], ws['mem_k_gain'], name=n('memkv'))
    x2, h2 = memattn_fwd(x1, ws['norm_mem_q'], wb['mq'], ws['mem_q_gain'], kh, vh, wb['mo'], name=n('memattn'))
    h3 = rmsnorm_fwd(x2, ws['norm_mlp'], name=n('norm_mlp'))
    x3 = mlp_fwd(x2, h3, wb['w1'], wb['w2'], name=n('mlp'))
    saved = dict(x=x, h1=h1, proj=proj, ypre=ypre, y_ssm=y_ssm, q=q, qt=qt, k=k, kt=kt, v=v, o=o, lse=lse, x1=x1, yn=yn,
                 kh=kh, vh=vh, x2=x2, h2=h2, h3=h3, mw=mw)
    return x3, saved


def layer_bwd(l, dx3, mem, tabs, wb, ws, sv):
    n = lambda s: f'l{l}_{s}_bwd'
    gb, gs = {}, {}
    dx3b = dx3.astype(BF)
    dh3, r, da = mlp_bwd(sv['h3'], dx3b, wb['w1'], wb['w2'], name=n('mlp'))
    gb['w1'] = mm(sv['h3'], da, 'tn', name=n('w1'))
    gb['w2'] = mm(r, dx3b, 'tn', name=n('w2'))
    dx2, gs['norm_mlp'] = rmsnorm_bwd(sv['x2'], ws['norm_mlp'], dh3, dx3, name=n('norm_mlp'))
    dx1, o_mem, dqr_mem, dkh, dvh, gs['mem_q_gain'], gs['norm_mem_q'] = memattn_bwd(
        sv['x1'], dx2, ws['norm_mem_q'], wb['mq'], ws['mem_q_gain'], sv['kh'], sv['vh'], wb['mo'], name=n('memattn'))
    dx2b = dx2.astype(BF)
    gb['mo'] = mm(o_mem, dx2b, 'tn', name=n('mo'))
    gb['mq'] = mm(sv['h2'], dqr_mem, 'tn', name=n('mq'))
    gb['mk'], gb['mv'], gs['mem_k_gain'], gs['norm_mem_kv'] = memkv_bwd(
        mem, ws['norm_mem_kv'], wb['mk'], wb['mv'], ws['mem_k_gain'], dkh, dvh, name=n('memkv'))
    dx1b = dx1.astype(BF)
    dyn = mm(dx1b, wb['w_out'], 'nt', name=n('w_out_dx'))
    gb['w_out'] = mm(sv['yn'], dx1b, 'tn', name=n('w_out'))
    dy_ssm, gs['g_ssm'] = rmsnorm_bwd(sv['y_ssm'], ws['g_ssm'], dyn, None, name=n('out_norm_ssm'), col=0)
    do, dot, delta, gs['g_mla'] = mla_out_bwd(sv['o'], dyn, ws['g_mla'], name=n('out_norm_mla'))
    dq, dk, dv = flash_bwd(sv['q'], sv['qt'], sv['k'], sv['kt'], sv['v'], do, dot, sv['lse'], delta, name=n('flash'))
    (dproj_m, cqn, ckvn, dqr, dkr, dvb, gs['q_norm'], gs['kv_norm'], gs['q_gain'], gs['k_gain']) = mla_prep_bwd(
        sv['proj'], tabs, sv['mw'], dq, dk, dv, name=n('mla_prep'))
    gb['wq'] = mm(cqn, dqr, 'tn', name=n('wq'))
    gb['wk'] = mm(ckvn, dkr, 'tn', name=n('wk'))
    gb['wv'] = mm(ckvn, dvb, 'tn', name=n('wv'))
    dypre, yg, dz, gs['b_glu'] = glu_bwd(sv['ypre'], dy_ssm, wb['w_glu'], ws['b_glu'], name=n('glu'))
    gb['w_glu'] = mm(yg, dz, 'tn', name=n('w_glu'))
    u_p = _perm(sv['proj'][:, :SSM_W])
    du_p, gs['ar'], gs['ai'], gs['bre'], gs['bim'], gs['cre'], gs['cim'], gs['d'] = s5_bwd(u_p, _perm(dypre), ws, name=n('s5'))
    dproj = jnp.concatenate([_unperm(du_p), dproj_m], axis=1)
    dprojb = dproj.astype(BF)
    dh1 = mm(dprojb, wb['w_in'], 'nt', name=n('w_in_dx'))
    gb['w_in'] = mm(sv['h1'], dprojb, 'tn', name=n('w_in'))
    dx0, gs['norm_mix'] = rmsnorm_bwd(sv['x'], ws['norm_mix'], dh1, dx1, name=n('norm_mix'))
    return dx0, gb, gs


def local_step(x, mem, positions, target, big, small):
    t = x.shape[0]
    tabs = rope_tables(positions)
    big_struct = {k: _sds(big[k].shape[1:], F32) for k in BIG}
    layers = []
    for l in range(DEPTH):
        wb = prep_big({k: big[k][l] for k in BIG})
        ws, small_vjp = jax.vjp(functools.partial(prep_small, t), {k: small[k][l] for k in SMALL})
        x, sv = layer_fwd(l, x, mem, tabs, wb, ws)
        layers.append((wb, ws, small_vjp, sv))
    dx, lcols = loss_fwd_bwd(x, target, name='loss')
    loss = jnp.sum(lcols)
    gbig, gsmall = [None] * DEPTH, [None] * DEPTH
    for l in reversed(range(DEPTH)):
        wb, ws, small_vjp, sv = layers[l]
        dx, gb, gs = layer_bwd(l, dx, mem, tabs, wb, ws, sv)
        gs['pr'], gs['pi'] = jnp.zeros_like(ws['pr']), jnp.zeros_like(ws['pi'])
        gbig[l] = jax.linear_transpose(prep_big, big_struct)(gb)[0]
        gsmall[l] = small_vjp(gs)[0]
    return loss, dx, gbig, gsmall


def _peer(k):
    x, y, c = lax.axis_index('x'), lax.axis_index('y'), lax.axis_index('c')
    px, py, pc = x ^ ((k >> 2) & 1), y ^ ((k >> 1) & 1), c ^ (k & 1)
    return (px, py, pc), 4 * px + 2 * py + pc


def exchange(a, b, *, name):
    ins = [v for v in (a, b) if v is not None]
    n_in = len(ins)
    outs = []
    if a is not None:
        outs.append(_sds(a.shape, a.dtype))
    if b is not None:
        outs.append(_sds((NDEV,) + b.shape, b.dtype))

    def body(*refs):
        in_refs, out_refs = refs[:n_in], refs[n_in:2 * n_in]
        send_sems, recv_sems, loc_sems = refs[2 * n_in:]
        _, me = _peer(0)
        pairs = []
        if a is not None:
            pairs.append((lambda p, r=in_refs[0]: r.at[p], out_refs[0]))
        if b is not None:
            pairs.append((lambda p, r=in_refs[-1]: r, out_refs[-1]))
        local = [pltpu.make_async_copy(src(me), dst.at[me], loc_sems.at[i]) for i, (src, dst) in enumerate(pairs)]
        for cp in local:
            cp.start()
        sends, recvs = [], []
        for k in range(1, NDEV):
            dev, p = _peer(k)
            for i, (src, dst) in enumerate(pairs):
                sends.append(pltpu.make_async_remote_copy(src_ref=src(p), dst_ref=dst.at[me], send_sem=send_sems.at[i, k - 1],
                                                          recv_sem=recv_sems.at[i, k - 1], device_id=dev,
                                                          device_id_type=pl.DeviceIdType.MESH))
                recvs.append(pltpu.make_async_remote_copy(src_ref=src(p), dst_ref=dst.at[p], send_sem=send_sems.at[i, k - 1],
                                                          recv_sem=recv_sems.at[i, k - 1], device_id=dev,
                                                          device_id_type=pl.DeviceIdType.MESH))
        for cp in sends:
            cp.start()
        for cp in sends:
            cp.wait_send()
        for cp in recvs:
            cp.wait_recv()
        for cp in local:
            cp.wait()

    anyspec = pl.BlockSpec(memory_space=pl.ANY)
    res = pl.pallas_call(
        body, name=name, in_specs=[anyspec] * n_in, out_specs=[anyspec] * n_in, out_shape=outs,
        scratch_shapes=[pltpu.SemaphoreType.DMA((n_in, NDEV - 1)), pltpu.SemaphoreType.DMA((n_in, NDEV - 1)),
                        pltpu.SemaphoreType.DMA((n_in,))],
    )(*ins)
    res = list(res)
    ra = res.pop(0) if a is not None else None
    rb = res.pop(0) if b is not None else None
    return ra, rb


def adamw(w, m, v, g8, *, name, tr):
    r = w.shape[0]
    c1 = 1.0 / (1.0 - ADAM_B1 ** ADAM_STEP)
    c2 = 1.0 / (1.0 - ADAM_B2 ** ADAM_STEP)

    def body(w_ref, m_ref, v_ref, g_ref, go_ref, d_ref, mo_ref, vo_ref):
        g = g_ref[0].astype(F32)
        for i in range(1, NDEV):
            g = g + g_ref[i].astype(F32)
        m_new = ADAM_B1 * m_ref[...] + (1.0 - ADAM_B1) * g
        v_new = ADAM_B2 * v_ref[...] + (1.0 - ADAM_B2) * (g * g)
        go_ref[...] = g
        mo_ref[...] = m_new
        vo_ref[...] = v_new
        d_ref[...] = -ADAM_LR * ((m_new * c1) / (jnp.sqrt(v_new * c2) + ADAM_EPS) + ADAM_WD * w_ref[...])

    row = pl.BlockSpec((tr, D), lambda i: (i, 0))
    return _call(body, name=name, grid=(r // tr,),
                 in_specs=[row, row, row, pl.BlockSpec((NDEV, tr, D), lambda i: (0, i, 0))],
                 out_specs=[row] * 4, out_shape=[_sds((r, D), F32)] * 4, sem=('parallel',), vmem=VMEM_BIG)(w, m, v, g8)


def _flat_rows(parts, rows):
    flat = jnp.concatenate([p.reshape(-1) for p in parts])
    return jnp.pad(flat, (0, rows * D - flat.shape[0])).reshape(rows, D)


def _unflat(flat2d, shapes):
    flat = flat2d.reshape(-1)
    out, off = [], 0
    for s in shapes:
        n = math.prod(s)
        out.append(flat[off:off + n].reshape(s))
        off += n
    return out


def _to_slots(g, axis):
    l, r, c = g.shape
    if axis == 1:
        return g.reshape(l, NDEV, r // NDEV, c).transpose(1, 0, 2, 3).reshape(NDEV, -1)
    return g.reshape(l, r, NDEV, c // NDEV).transpose(2, 0, 1, 3).reshape(NDEV, -1)


def _from_slots(s, shard_shape, axis):
    l, r, c = shard_shape
    s = s.reshape(NDEV, l, r, c)
    if axis == 1:
        return s.transpose(1, 0, 2, 3).reshape(l, NDEV * r, c)
    return s.transpose(1, 2, 0, 3).reshape(l, r, NDEV * c)


BIG_ROWS = 6144
SMALL_ROWS = 640


def kernel(x, mem, positions, norm_mix, w_in, ssm_lambda_re, ssm_lambda_im, ssm_log_step, ssm_b_re, ssm_b_im, ssm_c_re, ssm_c_im, ssm_d, ssm_w_glu, ssm_b_glu, mla_q_norm, mla_w_uq, mla_kv_norm, mla_w_ukv, mla_q_gain, mla_k_gain, out_norm_ssm, out_norm_mla, w_out, norm_mem_q, norm_mem_kv, mem_w_q, mem_w_kv, mem_q_gain, mem_k_gain, mem_w_o, norm_mlp, mlp_w1, mlp_w2, loss_target, m_norm_mix, m_w_in, m_ssm_lambda_re, m_ssm_lambda_im, m_ssm_log_step, m_ssm_b_re, m_ssm_b_im, m_ssm_c_re, m_ssm_c_im, m_ssm_d, m_ssm_w_glu, m_ssm_b_glu, m_mla_q_norm, m_mla_w_uq, m_mla_kv_norm, m_mla_w_ukv, m_mla_q_gain, m_mla_k_gain, m_out_norm_ssm, m_out_norm_mla, m_w_out, m_norm_mem_q, m_norm_mem_kv, m_mem_w_q, m_mem_w_kv, m_mem_q_gain, m_mem_k_gain, m_mem_w_o, m_norm_mlp, m_mlp_w1, m_mlp_w2, v_norm_mix, v_w_in, v_ssm_lambda_re, v_ssm_lambda_im, v_ssm_log_step, v_ssm_b_re, v_ssm_b_im, v_ssm_c_re, v_ssm_c_im, v_ssm_d, v_ssm_w_glu, v_ssm_b_glu, v_mla_q_norm, v_mla_w_uq, v_mla_kv_norm, v_mla_w_ukv, v_mla_q_gain, v_mla_k_gain, v_out_norm_ssm, v_out_norm_mla, v_w_out, v_norm_mem_q, v_norm_mem_kv, v_mem_w_q, v_mem_w_kv, v_mem_q_gain, v_mem_k_gain, v_mem_w_o, v_norm_mlp, v_mlp_w1, v_mlp_w2):
    wvals = (norm_mix, w_in, ssm_lambda_re, ssm_lambda_im, ssm_log_step, ssm_b_re, ssm_b_im, ssm_c_re, ssm_c_im, ssm_d, ssm_w_glu, ssm_b_glu, mla_q_norm, mla_w_uq, mla_kv_norm, mla_w_ukv, mla_q_gain, mla_k_gain, out_norm_ssm, out_norm_mla, w_out, norm_mem_q, norm_mem_kv, mem_w_q, mem_w_kv, mem_q_gain, mem_k_gain, mem_w_o, norm_mlp, mlp_w1, mlp_w2)
    mvals = (m_norm_mix, m_w_in, m_ssm_lambda_re, m_ssm_lambda_im, m_ssm_log_step, m_ssm_b_re, m_ssm_b_im, m_ssm_c_re, m_ssm_c_im, m_ssm_d, m_ssm_w_glu, m_ssm_b_glu, m_mla_q_norm, m_mla_w_uq, m_mla_kv_norm, m_mla_w_ukv, m_mla_q_gain, m_mla_k_gain, m_out_norm_ssm, m_out_norm_mla, m_w_out, m_norm_mem_q, m_norm_mem_kv, m_mem_w_q, m_mem_w_kv, m_mem_q_gain, m_mem_k_gain, m_mem_w_o, m_norm_mlp, m_mlp_w1, m_mlp_w2)
    vvals = (v_norm_mix, v_w_in, v_ssm_lambda_re, v_ssm_lambda_im, v_ssm_log_step, v_ssm_b_re, v_ssm_b_im, v_ssm_c_re, v_ssm_c_im, v_ssm_d, v_ssm_w_glu, v_ssm_b_glu, v_mla_q_norm, v_mla_w_uq, v_mla_kv_norm, v_mla_w_ukv, v_mla_q_gain, v_mla_k_gain, v_out_norm_ssm, v_out_norm_mla, v_w_out, v_norm_mem_q, v_norm_mem_kv, v_mem_w_q, v_mem_w_kv, v_mem_q_gain, v_mem_k_gain, v_mem_w_o, v_norm_mlp, v_mlp_w1, v_mlp_w2)
    w = dict(zip(WEIGHTS, wvals))
    m = dict(zip(WEIGHTS, mvals))
    v = dict(zip(WEIGHTS, vvals))

    shard_shapes = {k: w[k].shape for k in BIG}
    mine = _flat_rows([w[k].astype(BF) for k in BIG], BIG_ROWS)
    _, gathered = exchange(None, mine, name='gather_weights')
    gathered = gathered.reshape(NDEV, -1)
    big, off = {}, 0
    for k in BIG:
        n = math.prod(shard_shapes[k])
        big[k] = _from_slots(gathered[:, off:off + n], shard_shapes[k], BIG_AXIS[k])
        off += n
    small = {k: w[k] for k in SMALL}

    loss, grad_x, gbig, gsmall = local_step(x[0], mem[0], positions[0], loss_target[0], big, small)

    gb_full = {k: jnp.stack([gbig[l][k] for l in range(DEPTH)]) for k in BIG}
    slots = jnp.concatenate([_to_slots(gb_full[k], BIG_AXIS[k]) for k in BIG], axis=1)
    slots = jnp.pad(slots, ((0, 0), (0, BIG_ROWS * D - slots.shape[1]))).astype(BF).reshape(NDEV, BIG_ROWS, D)
    gs_full = [jnp.stack([gsmall[l][k] for l in range(DEPTH)]) for k in SMALL]
    small_flat = _flat_rows(gs_full + [loss.reshape(1)], SMALL_ROWS)
    g8_big, g8_small = exchange(slots, small_flat, name='exchange_grads')

    big_shapes = [shard_shapes[k] for k in BIG]
    small_shapes = [w[k].shape for k in SMALL]
    gb, db, mb, vb = adamw(_flat_rows([w[k] for k in BIG], BIG_ROWS), _flat_rows([m[k] for k in BIG], BIG_ROWS),
                           _flat_rows([v[k] for k in BIG], BIG_ROWS), g8_big, name='adamw_big', tr=256)
    gs, ds, ms, vs = adamw(_flat_rows([w[k] for k in SMALL], SMALL_ROWS), _flat_rows([m[k] for k in SMALL], SMALL_ROWS),
                           _flat_rows([v[k] for k in SMALL], SMALL_ROWS), g8_small, name='adamw_small', tr=128)
    n_small = sum(math.prod(s) for s in small_shapes)
    loss_all = gs.reshape(-1)[n_small]

    res = {}
    for tag, fb, fs in (('g', gb, gs), ('d', db, ds), ('m', mb, ms), ('v', vb, vs)):
        res[tag] = dict(zip(BIG, _unflat(fb, big_shapes)))
        res[tag].update(zip(SMALL, _unflat(fs, small_shapes)))
    return (loss_all, grad_x[None], *[res['g'][k] for k in WEIGHTS], *[res['d'][k] for k in WEIGHTS],
            *[res['m'][k] for k in WEIGHTS], *[res['v'][k] for k in WEIGHTS])
```

```python
import functools
import math

import jax
import jax.numpy as jnp
from jax import lax
from jax.experimental import pallas as pl
from jax.experimental.pallas import tpu as pltpu

F32 = jnp.float32
BF = jnp.bfloat16

D = 1024
DEPTH = 4
N_MEM = 256
MEM_HEADS = 4
MEM_HD = 64
SSM_W = 512
SSM_G = 32
SSM_H = 16
SSM_P = 64
MLA_HEADS = 8
QK_NOPE = 64
QK_ROPE = 32
QK_DIM = 96
V_DIM = 64
Q_LORA = 256
KV_LORA = 128
ROPE_THETA = 10000.0
D_FF = 4096
IN_COLS = 928
EPS = 1e-6
NDEV = 8
LANES = 128
SEGS = 8
S5_LW = 256
S5_NHB = (SSM_G * SSM_P) // S5_LW
ADAM_LR = 0.001
ADAM_B1 = 0.9
ADAM_B2 = 0.999
ADAM_EPS = 1e-08
ADAM_WD = 0.01
ADAM_STEP = 10
VMEM_BIG = 56 * 1024 * 1024

NN = (((1,), (0,)), ((), ()))
NT = (((1,), (1,)), ((), ()))
TN = (((0,), (0,)), ((), ()))

BIG = ('w_in', 'ssm_w_glu', 'mla_w_uq', 'mla_w_ukv', 'w_out', 'mem_w_q', 'mem_w_kv', 'mem_w_o', 'mlp_w1', 'mlp_w2')
BIG_AXIS = {'w_in': 1, 'ssm_w_glu': 1, 'mla_w_uq': 2, 'mla_w_ukv': 2, 'w_out': 1, 'mem_w_q': 1, 'mem_w_kv': 1,
            'mem_w_o': 2, 'mlp_w1': 2, 'mlp_w2': 1}
SMALL = ('norm_mix', 'ssm_lambda_re', 'ssm_lambda_im', 'ssm_log_step', 'ssm_b_re', 'ssm_b_im', 'ssm_c_re', 'ssm_c_im',
         'ssm_d', 'ssm_b_glu', 'mla_q_norm', 'mla_kv_norm', 'mla_q_gain', 'mla_k_gain', 'out_norm_ssm', 'out_norm_mla',
         'norm_mem_q', 'norm_mem_kv', 'mem_q_gain', 'mem_k_gain', 'norm_mlp')
WEIGHTS = ('norm_mix', 'w_in', 'ssm_lambda_re', 'ssm_lambda_im', 'ssm_log_step', 'ssm_b_re', 'ssm_b_im', 'ssm_c_re',
           'ssm_c_im', 'ssm_d', 'ssm_w_glu', 'ssm_b_glu', 'mla_q_norm', 'mla_w_uq', 'mla_kv_norm', 'mla_w_ukv',
           'mla_q_gain', 'mla_k_gain', 'out_norm_ssm', 'out_norm_mla', 'w_out', 'norm_mem_q', 'norm_mem_kv', 'mem_w_q',
           'mem_w_kv', 'mem_q_gain', 'mem_k_gain', 'mem_w_o', 'norm_mlp', 'mlp_w1', 'mlp_w2')


def _call(body, *, name, out_shape, grid=(), in_specs=None, out_specs=None, scratch=(), sem=None, vmem=None):
    params = {}
    if sem is not None:
        params['dimension_semantics'] = sem
    if vmem is not None:
        params['vmem_limit_bytes'] = vmem
    specs = {} if in_specs is None else dict(grid=grid, in_specs=in_specs, out_specs=out_specs)
    return pl.pallas_call(body, name=name, out_shape=out_shape, scratch_shapes=list(scratch),
                          compiler_params=pltpu.CompilerParams(**params), **specs)


def _sds(shape, dtype):
    return jax.ShapeDtypeStruct(shape, dtype)


def _dot(a, b, dims=NN):
    return lax.dot_general(a.astype(BF), b.astype(BF), dims, preferred_element_type=F32)


def _split(a):
    hi = a.astype(BF)
    return hi, (a - hi.astype(F32)).astype(BF)


def _dot3(a, b, dims=NN):
    ah, al = _split(a)
    bh, bl = _split(b)
    d = lambda p, q: lax.dot_general(p, q, dims, preferred_element_type=F32)
    return d(ah, bh) + (d(ah, bl) + d(al, bh))


_sdot = _dot


def _rms(x, n):
    r = lax.rsqrt(jnp.sum(x * x, axis=-1, keepdims=True) * (1.0 / n) + EPS)
    return x * r, r


def _rms_bwd(xhat, r, dxhat, n):
    return r * (dxhat - xhat * (jnp.sum(dxhat * xhat, axis=-1, keepdims=True) * (1.0 / n)))


def _colsum(a):
    return jnp.sum(a, axis=0, keepdims=True)


def _tile(t, want):
    return min(t, want)


def _bidx(nb):
    return (lambda b: b) if nb > 1 else (lambda b: 0)


def mm(a, b, mode, *, name, out_dtype=F32, tm=512, tn=512, tk=1024):
    squeeze = a.ndim == 2 and b.ndim == 2
    a = a[None] if a.ndim == 2 else a
    b = b[None] if b.ndim == 2 else b
    nb = max(a.shape[0], b.shape[0])
    ab, bb = _bidx(a.shape[0]), _bidx(b.shape[0])
    if mode in ('nn', 'nt'):
        m, k = a.shape[1:]
        n = b.shape[2] if mode == 'nn' else b.shape[1]
        tm, tn = _tile(m, tm), _tile(n, tn)
        dims = NN if mode == 'nn' else NT

        def body(a_ref, b_ref, o_ref):
            o_ref[...] = _dot(a_ref[...], b_ref[...], dims).astype(o_ref.dtype)

        bspec = (pl.BlockSpec((None, k, tn), lambda bi, i, j: (bb(bi), 0, j)) if mode == 'nn'
                 else pl.BlockSpec((None, tn, k), lambda bi, i, j: (bb(bi), j, 0)))
        out = _call(body, name=name, grid=(nb, m // tm, n // tn),
                    in_specs=[pl.BlockSpec((None, tm, k), lambda bi, i, j: (ab(bi), i, 0)), bspec],
                    out_specs=pl.BlockSpec((None, tm, tn), lambda bi, i, j: (bi, i, j)),
                    out_shape=_sds((nb, m, n), out_dtype), sem=('parallel', 'parallel', 'parallel'))(a, b)
    else:
        k, m = a.shape[1:]
        n = b.shape[2]
        tm, tn, tk = _tile(m, 1024), _tile(n, 1024), _tile(k, 512)

        def body(a_ref, b_ref, o_ref):
            @pl.when(pl.program_id(3) == 0)
            def _():
                o_ref[...] = jnp.zeros_like(o_ref)

            o_ref[...] += _dot(a_ref[...], b_ref[...], TN)

        out = _call(body, name=name, grid=(nb, m // tm, n // tn, k // tk),
                    in_specs=[pl.BlockSpec((None, tk, tm), lambda bi, i, j, kk: (ab(bi), kk, i)),
                              pl.BlockSpec((None, tk, tn), lambda bi, i, j, kk: (bb(bi), kk, j))],
                    out_specs=pl.BlockSpec((None, tm, tn), lambda bi, i, j, kk: (bi, i, j)),
                    out_shape=_sds((nb, m, n), F32), sem=('parallel', 'parallel', 'parallel', 'arbitrary'))(a, b)
    return out[0] if squeeze else out


def rmsnorm_fwd(x, g, *, name, tq=512):
    t, d = x.shape
    tq = _tile(t, tq)

    def body(x_ref, g_ref, o_ref):
        xh, _ = _rms(x_ref[...], d)
        o_ref[...] = (xh * g_ref[...]).astype(o_ref.dtype)

    return _call(body, name=name, grid=(t // tq,),
                 in_specs=[pl.BlockSpec((tq, d), lambda i: (i, 0)), pl.BlockSpec((1, d), lambda i: (0, 0))],
                 out_specs=pl.BlockSpec((tq, d), lambda i: (i, 0)), out_shape=_sds((t, d), BF), sem=('parallel',))(x, g)


def rmsnorm_bwd(x, g, dh, dres, *, name, col=0, tq=512):
    t, d = x.shape
    tq = _tile(t, tq)
    has_res = dres is not None

    def body(*refs):
        if has_res:
            x_ref, g_ref, dh_ref, dres_ref, dx_ref, dg_ref = refs
        else:
            x_ref, g_ref, dh_ref, dx_ref, dg_ref = refs
        xh, r = _rms(x_ref[...], d)
        dh_ = dh_ref[...].astype(F32)
        dx = _rms_bwd(xh, r, dh_ * g_ref[...], d)
        if has_res:
            dx = dx + dres_ref[...]
        dx_ref[...] = dx

        @pl.when(pl.program_id(0) == 0)
        def _():
            dg_ref[...] = jnp.zeros_like(dg_ref)

        dg_ref[...] += _colsum(dh_ * xh)

    in_specs = [pl.BlockSpec((tq, d), lambda i: (i, 0)), pl.BlockSpec((1, d), lambda i: (0, 0)),
                pl.BlockSpec((tq, d), lambda i: (i, col))]
    args = [x, g, dh]
    if has_res:
        in_specs.append(pl.BlockSpec((tq, d), lambda i: (i, 0)))
        args.append(dres)
    return _call(body, name=name, grid=(t // tq,), in_specs=in_specs,
                 out_specs=[pl.BlockSpec((tq, d), lambda i: (i, 0)), pl.BlockSpec((1, d), lambda i: (0, 0))],
                 out_shape=[_sds((t, d), F32), _sds((1, d), F32)], sem=('arbitrary',))(*args)


def _cmul(ar, ai, xr, xi):
    return ar * xr - ai * xi, ar * xi + ai * xr


def _seg_carries(er, ei, pr, pi, reverse):
    lw = er.shape[1]
    zero = jnp.zeros((1, lw), F32)
    order = range(SEGS - 1, -1, -1) if reverse else range(SEGS)
    cin_r, cin_i = [None] * SEGS, [None] * SEGS
    tr, ti = zero, zero
    for j in order:
        cin_r[j], cin_i[j] = tr, ti
        mr, mi = _cmul(pr, pi, tr, ti)
        tr, ti = er[j:j + 1, :] + mr, ei[j:j + 1, :] + mi
    return jnp.concatenate(cin_r, axis=0), jnp.concatenate(cin_i, axis=0)


def _s5_chunk(t):
    return _tile(t, 512)


def s5_fwd(u_p, prm, *, name):
    t = u_p.shape[0]
    ch = _s5_chunk(t)
    nch, steps = t // ch, ch // SEGS
    lw = S5_LW

    def body(u_ref, ar_ref, ai_ref, pr_ref, pi_ref, bre_ref, bim_ref, cre_ref, cim_ref, d_ref, y_ref, bur, bui):
        hb = pl.program_id(0)
        ar = jnp.broadcast_to(ar_ref[0], (SEGS, lw))
        ai = jnp.broadcast_to(ai_ref[0], (SEGS, lw))

        def rows_of(c):
            return pl.ds(pl.multiple_of(c * ch, ch), ch)

        @pl.loop(0, nch)
        def _(c):
            u = u_ref[rows_of(c), :]
            bur[rows_of(c), :] = _sdot(u, bre_ref[0])
            bui[rows_of(c), :] = _sdot(u, bim_ref[0])

        def scan(carry, store):
            def step(i, s):
                r0 = pl.multiple_of(i * SEGS, SEGS)
                mr, mi = _cmul(ar, ai, s[0], s[1])
                nr, ni = mr + bur[pl.ds(r0, SEGS), :], mi + bui[pl.ds(r0, SEGS), :]
                if store:
                    bur[pl.ds(r0, SEGS), :] = nr
                    bui[pl.ds(r0, SEGS), :] = ni
                return nr, ni

            return lax.fori_loop(0, t // SEGS, step, carry, unroll=8)

        zero = jnp.zeros((SEGS, lw), F32)
        er, ei = scan((zero, zero), False)
        scan(_seg_carries(er, ei, pr_ref[0], pi_ref[0], False), True)

        @pl.loop(0, nch)
        def _(c):
            rows = rows_of(c)
            y = _sdot(bur[rows, :], cre_ref[0]) - _sdot(bui[rows, :], cim_ref[0])

            @pl.when(hb % 2 == 0)
            def _():
                y_ref[rows, :] = y + d_ref[...] * u_ref[rows, :]

            @pl.when(hb % 2 == 1)
            def _():
                y_ref[rows, :] += y

    vec = pl.BlockSpec((1, 1, lw), lambda h: (h, 0, 0))
    return _call(
        body, name=name, grid=(S5_NHB,),
        in_specs=[pl.BlockSpec((t, LANES), lambda h: (0, h // 2)), vec, vec, vec, vec,
                  pl.BlockSpec((1, LANES, lw), lambda h: (h, 0, 0)), pl.BlockSpec((1, LANES, lw), lambda h: (h, 0, 0)),
                  pl.BlockSpec((1, lw, LANES), lambda h: (h, 0, 0)), pl.BlockSpec((1, lw, LANES), lambda h: (h, 0, 0)),
                  pl.BlockSpec((1, LANES), lambda h: (0, h // 2))],
        out_specs=pl.BlockSpec((t, LANES), lambda h: (0, h // 2)), out_shape=_sds((t, SSM_W), F32),
        scratch=[pltpu.VMEM((t, lw), F32)] * 2, sem=('arbitrary',), vmem=VMEM_BIG,
    )(u_p, prm['ar'], prm['ai'], prm['pr'], prm['pi'], prm['bre'], prm['bim'], prm['cre'], prm['cim'], prm['d'])


def s5_bwd(u_p, dy_p, prm, *, name):
    t = u_p.shape[0]
    ch = _s5_chunk(t)
    nch, steps = t // ch, ch // SEGS
    lw = S5_LW

    def body(u_ref, dy_ref, ar_ref, ai_ref, pr_ref, pi_ref, bre_ref, bim_ref, cre_ref, cim_ref, d_ref,
             du_ref, dar_ref, dai_ref, dbre_ref, dbim_ref, dcre_ref, dcim_ref, dd_ref, bur, bui, sr, si):
        hb = pl.program_id(0)
        ar = jnp.broadcast_to(ar_ref[0], (SEGS, lw))
        ai = jnp.broadcast_to(ai_ref[0], (SEGS, lw))
        zero = jnp.zeros((SEGS, lw), F32)

        def rows_of(c):
            return pl.ds(pl.multiple_of(c * ch, ch), ch)

        nsteps = t // SEGS

        @pl.loop(0, nch)
        def _(c):
            u = u_ref[rows_of(c), :]
            bur[rows_of(c), :] = _sdot(u, bre_ref[0])
            bui[rows_of(c), :] = _sdot(u, bim_ref[0])

        def fwd_scan(carry, store):
            def step(i, s):
                r0 = pl.multiple_of(i * SEGS, SEGS)
                mr, mi = _cmul(ar, ai, s[0], s[1])
                nr, ni = mr + bur[pl.ds(r0, SEGS), :], mi + bui[pl.ds(r0, SEGS), :]
                if store:
                    w0 = pl.multiple_of(i * SEGS + SEGS, SEGS)
                    sr[pl.ds(w0, SEGS), :] = nr
                    si[pl.ds(w0, SEGS), :] = ni
                return nr, ni

            return lax.fori_loop(0, nsteps, step, carry, unroll=8)

        er, ei = fwd_scan((zero, zero), False)
        cin_r, cin_i = _seg_carries(er, ei, pr_ref[0], pi_ref[0], False)
        sr[pl.ds(0, SEGS), :] = cin_r
        si[pl.ds(0, SEGS), :] = cin_i
        fwd_scan((cin_r, cin_i), True)

        @pl.loop(0, nch)
        def _(c):
            dy = dy_ref[rows_of(c), :]
            bur[rows_of(c), :] = _sdot(dy, cre_ref[0], NT)
            bui[rows_of(c), :] = -_sdot(dy, cim_ref[0], NT)

        def rev_local(ii, lam):
            r0 = pl.multiple_of((nsteps - 1 - ii) * SEGS, SEGS)
            mr, mi = _cmul(ar, -ai, lam[0], lam[1])
            return mr + bur[pl.ds(r0, SEGS), :], mi + bui[pl.ds(r0, SEGS), :]

        lr0, li0 = lax.fori_loop(0, nsteps, rev_local, (zero, zero), unroll=8)
        rin = _seg_carries(lr0, li0, pr_ref[0], -pi_ref[0], True)

        def rev_step(ii, st):
            lam_r, lam_i, acc_r, acc_i = st
            r0 = pl.multiple_of((nsteps - 1 - ii) * SEGS, SEGS)
            mr, mi = _cmul(ar, -ai, lam_r, lam_i)
            nr, ni = mr + bur[pl.ds(r0, SEGS), :], mi + bui[pl.ds(r0, SEGS), :]
            bur[pl.ds(r0, SEGS), :] = nr
            bui[pl.ds(r0, SEGS), :] = ni
            pr_, pi_ = sr[pl.ds(r0, SEGS), :], si[pl.ds(r0, SEGS), :]
            return nr, ni, acc_r + (nr * pr_ + ni * pi_), acc_i + (ni * pr_ - nr * pi_)

        _, _, acc_r, acc_i = lax.fori_loop(0, nsteps, rev_step, (rin[0], rin[1], zero, zero), unroll=8)
        dar_ref[0] = _colsum(acc_r)
        dai_ref[0] = _colsum(acc_i)

        dbre_ref[...] = jnp.zeros_like(dbre_ref)
        dbim_ref[...] = jnp.zeros_like(dbim_ref)
        dcre_ref[...] = jnp.zeros_like(dcre_ref)
        dcim_ref[...] = jnp.zeros_like(dcim_ref)

        @pl.loop(0, nch)
        def _(c):
            rows = rows_of(c)
            u = u_ref[rows, :]
            dy = dy_ref[rows, :]
            lam_r, lam_i = bur[rows, :], bui[rows, :]
            du = _sdot(lam_r, bre_ref[0], NT) + _sdot(lam_i, bim_ref[0], NT)

            @pl.when(hb % 2 == 0)
            def _():
                du_ref[rows, :] = du + d_ref[...] * dy

            @pl.when(hb % 2 == 1)
            def _():
                du_ref[rows, :] += du

            dbre_ref[0] += _sdot(u, lam_r, TN)
            dbim_ref[0] += _sdot(u, lam_i, TN)
            srows = pl.ds(pl.multiple_of(c * ch + SEGS, SEGS), ch)
            dcre_ref[0] += _sdot(sr[srows, :], dy, TN)
            dcim_ref[0] -= _sdot(si[srows, :], dy, TN)

        @pl.when(hb % 2 == 0)
        def _():
            dd_ref[...] = _colsum(dy_ref[...] * u_ref[...])

    vec = pl.BlockSpec((1, 1, lw), lambda h: (h, 0, 0))
    bsp = pl.BlockSpec((1, LANES, lw), lambda h: (h, 0, 0))
    csp = pl.BlockSpec((1, lw, LANES), lambda h: (h, 0, 0))
    act = pl.BlockSpec((t, LANES), lambda h: (0, h // 2))
    dsp = pl.BlockSpec((1, LANES), lambda h: (0, h // 2))
    return _call(
        body, name=name, grid=(S5_NHB,),
        in_specs=[act, act, vec, vec, vec, vec, bsp, bsp, csp, csp, dsp],
        out_specs=[act, vec, vec, bsp, bsp, csp, csp, dsp],
        out_shape=[_sds((t, SSM_W), F32), _sds((S5_NHB, 1, lw), F32), _sds((S5_NHB, 1, lw), F32),
                   _sds((S5_NHB, LANES, lw), F32), _sds((S5_NHB, LANES, lw), F32),
                   _sds((S5_NHB, lw, LANES), F32), _sds((S5_NHB, lw, LANES), F32), _sds((1, SSM_W), F32)],
        scratch=[pltpu.VMEM((t, lw), F32), pltpu.VMEM((t, lw), F32),
                 pltpu.VMEM((t + SEGS, lw), F32), pltpu.VMEM((t + SEGS, lw), F32)],
        sem=('arbitrary',), vmem=VMEM_BIG,
    )(u_p, dy_p, prm['ar'], prm['ai'], prm['pr'], prm['pi'], prm['bre'], prm['bim'], prm['cre'], prm['cim'], prm['d'])


def s5_prep(t, lam_re, lam_im, log_step, b_re, b_im, c_re, c_im):
    step = jnp.exp(log_step)[:, None]
    mag = jnp.exp(lam_re * step)
    ar, ai = mag * jnp.cos(lam_im * step), mag * jnp.sin(lam_im * step)
    den = lam_re * lam_re + lam_im * lam_im
    nr, ni = ar - 1.0, ai
    fr, fi = (nr * lam_re + ni * lam_im) / den, (ni * lam_re - nr * lam_im) / den
    bbr = fr[..., None] * b_re - fi[..., None] * b_im
    bbi = fr[..., None] * b_im + fi[..., None] * b_re
    gl = S5_LW // SSM_P
    eye = jnp.eye(gl, dtype=F32)
    half = (jnp.arange(S5_NHB) % 2)[:, None, None]

    def bmat(bb):
        x = bb.transpose(0, 2, 1).reshape(S5_NHB, gl, SSM_H, SSM_P)
        x = jnp.einsum('bghp,gk->bghkp', x, eye).reshape(S5_NHB, gl * SSM_H, S5_LW)
        z = jnp.zeros_like(x)
        return jnp.where(half == 0, jnp.concatenate([x, z], axis=1), jnp.concatenate([z, x], axis=1))

    def cmat(cc):
        x = cc.transpose(0, 2, 1).reshape(S5_NHB, gl, SSM_P, SSM_H)
        x = jnp.einsum('bgph,gk->bgpkh', x, eye).reshape(S5_NHB, S5_LW, gl * SSM_H)
        z = jnp.zeros_like(x)
        return jnp.where(half == 0, jnp.concatenate([x, z], axis=2), jnp.concatenate([z, x], axis=2))

    vec = lambda a: a.reshape(S5_NHB, 1, S5_LW)
    ni_steps = float(t // SEGS)
    pmag = jnp.exp(lam_re * step * ni_steps)
    pr, pi = pmag * jnp.cos(lam_im * step * ni_steps), pmag * jnp.sin(lam_im * step * ni_steps)
    return dict(ar=vec(ar), ai=vec(ai), bre=bmat(bbr), bim=bmat(bbi), cre=cmat(c_re), cim=cmat(c_im),
                pr=lax.stop_gradient(vec(pr)), pi=lax.stop_gradient(vec(pi)))


def _gelu(x):
    c = math.sqrt(2.0 / math.pi)
    return 0.5 * x * (1.0 + jnp.tanh(c * (x + 0.044715 * (x * x * x))))


def _gelu_grad(x):
    c = math.sqrt(2.0 / math.pi)
    th = jnp.tanh(c * (x + 0.044715 * (x * x * x)))
    return 0.5 * (1.0 + th) + 0.5 * x * (1.0 - th * th) * (c * (1.0 + 3.0 * 0.044715 * (x * x)))


def glu_fwd(ypre, w_glu, b_glu, *, name, tq=512):
    t = ypre.shape[0]
    tq = _tile(t, tq)

    def body(y_ref, w_ref, b_ref, o_ref):
        yg = _gelu(y_ref[...])
        z = _dot(yg, w_ref[...]) + b_ref[...]
        o_ref[...] = yg * jax.nn.sigmoid(z)

    return _call(body, name=name, grid=(t // tq,),
                 in_specs=[pl.BlockSpec((tq, SSM_W), lambda i: (i, 0)), pl.BlockSpec((SSM_W, SSM_W), lambda i: (0, 0)),
                           pl.BlockSpec((1, SSM_W), lambda i: (0, 0))],
                 out_specs=pl.BlockSpec((tq, SSM_W), lambda i: (i, 0)), out_shape=_sds((t, SSM_W), F32),
                 sem=('parallel',))(ypre, w_glu, b_glu)


def glu_bwd(ypre, dy, w_glu, b_glu, *, name, tq=512):
    t = ypre.shape[0]
    tq = _tile(t, tq)

    def body(y_ref, dy_ref, w_ref, b_ref, dyp_ref, yg_ref, dz_ref, db_ref):
        ypre_ = y_ref[...]
        yg = _gelu(ypre_)
        sig = jax.nn.sigmoid(_dot(yg, w_ref[...]) + b_ref[...])
        dy_ = dy_ref[...]
        dz = dy_ * yg * sig * (1.0 - sig)
        dyg = dy_ * sig + _dot(dz, w_ref[...], NT)
        dyp_ref[...] = dyg * _gelu_grad(ypre_)
        yg_ref[...] = yg.astype(BF)
        dz_ref[...] = dz.astype(BF)

        @pl.when(pl.program_id(0) == 0)
        def _():
            db_ref[...] = jnp.zeros_like(db_ref)

        db_ref[...] += _colsum(dz)

    row = pl.BlockSpec((tq, SSM_W), lambda i: (i, 0))
    vec = pl.BlockSpec((1, SSM_W), lambda i: (0, 0))
    return _call(body, name=name, grid=(t // tq,),
                 in_specs=[row, row, pl.BlockSpec((SSM_W, SSM_W), lambda i: (0, 0)), vec],
                 out_specs=[row, row, row, vec],
                 out_shape=[_sds((t, SSM_W), F32), _sds((t, SSM_W), BF), _sds((t, SSM_W), BF), _sds((1, SSM_W), F32)],
                 sem=('arbitrary',))(ypre, dy, w_glu, b_glu)


def _rope(x, cos, sa, sb):
    return x * cos + pltpu.roll(x, 16, 1) * sa + pltpu.roll(x, 112, 1) * sb


def _rope_t(d, cos, sa, sb):
    return d * cos + pltpu.roll(d * sa, 112, 1) + pltpu.roll(d * sb, 16, 1)


def rope_tables(positions):
    half = QK_ROPE // 2
    inv_freq = ROPE_THETA ** (-jnp.arange(half, dtype=F32) / half)
    ang = positions.astype(F32)[:, None] * inv_freq
    cos, sin = jnp.cos(ang), jnp.sin(ang)
    t = positions.shape[0]
    one, zero = jnp.ones((t, QK_NOPE), F32), jnp.zeros((t, QK_NOPE), F32)
    pad1, pad0 = jnp.ones((t, 32), F32), jnp.zeros((t, 32), F32)
    z16 = jnp.zeros((t, half), F32)
    return (jnp.concatenate([one, cos, cos, pad1], axis=1), jnp.concatenate([zero, z16, sin, pad0], axis=1),
            jnp.concatenate([zero, -sin, z16, pad0], axis=1))


def mla_prep_fwd(proj, tabs, w, *, name):
    t = proj.shape[0]
    tq = _tile(t, ATT_BLK)

    def body(cq_ref, ckv_ref, kr_ref, cos_ref, sa_ref, sb_ref, qn_ref, kvn_ref, wq_ref, wk_ref, wv_ref, qg_ref, kg_ref,
             q_ref, qt_ref, k_ref, kt_ref, v_ref):
        cqn = (_rms(cq_ref[...], Q_LORA)[0] * qn_ref[...]).astype(BF)
        ckvn = (_rms(ckv_ref[...], KV_LORA)[0] * kvn_ref[...]).astype(BF)
        cos, sa, sb = cos_ref[...], sa_ref[...], sb_ref[...]
        kr = kr_ref[...]
        for h in range(MLA_HEADS):
            q = _rms(_dot(cqn, wq_ref[h]), QK_DIM)[0] * qg_ref[...]
            q = _rope(q, cos, sa, sb) * ATT_SCALE
            q_ref[h] = q.astype(BF)
            qt_ref[h, 0] = q.T.astype(BF)
            k = _rms(_dot(ckvn, wk_ref[h]) + kr, QK_DIM)[0] * kg_ref[...]
            k = _rope(k, cos, sa, sb)
            k_ref[h] = k.astype(BF)
            kt_ref[h, 0] = k.T.astype(BF)
            v_ref[h] = _dot(ckvn, wv_ref[h]).astype(BF)

    tab = pl.BlockSpec((tq, LANES), lambda i: (i, 0))
    full = lambda shape: pl.BlockSpec(shape, lambda i: (0,) * len(shape))
    hout = pl.BlockSpec((MLA_HEADS, tq, LANES), lambda i: (0, i, 0))
    tout = pl.BlockSpec((MLA_HEADS, 1, LANES, tq), lambda i: (0, i, 0, 0))
    hshape = _sds((MLA_HEADS, t, LANES), BF)
    tshape = _sds((MLA_HEADS, t // tq, LANES, tq), BF)
    return _call(
        body, name=name, grid=(t // tq,),
        in_specs=[pl.BlockSpec((tq, Q_LORA), lambda i: (i, 2)), pl.BlockSpec((tq, LANES), lambda i: (i, 6)),
                  pl.BlockSpec((tq, LANES), lambda i: (i, 7)), tab, tab, tab,
                  full((1, Q_LORA)), full((1, KV_LORA)), full((MLA_HEADS, Q_LORA, LANES)),
                  full((MLA_HEADS, KV_LORA, LANES)), full((MLA_HEADS, KV_LORA, LANES)), full((1, LANES)), full((1, LANES))],
        out_specs=[hout, tout, hout, tout, hout], out_shape=[hshape, tshape, hshape, tshape, hshape], sem=('parallel',),
    )(proj, proj, proj, *tabs, w['q_norm'], w['kv_norm'], w['wq'], w['wk'], w['wv'], w['q_gain'], w['k_gain'])


def mla_prep_bwd(proj, tabs, w, dq, dk, dv, *, name):
    t = proj.shape[0]
    tq = _tile(t, ATT_BLK)

    def body(cq_ref, ckv_ref, kr_ref, cos_ref, sa_ref, sb_ref, qn_ref, kvn_ref, wq_ref, wk_ref, wv_ref, qg_ref, kg_ref,
             dq_ref, dk_ref, dv_ref,
             dpm_ref, cqn_ref, ckvn_ref, dqr_ref, dkraw_ref, dvb_ref, dqn_ref, dkvn_ref, dqg_ref, dkg_ref):
        cq_h, cq_r = _rms(cq_ref[...], Q_LORA)
        ckv_h, ckv_r = _rms(ckv_ref[...], KV_LORA)
        cqn = (cq_h * qn_ref[...]).astype(BF)
        ckvn = (ckv_h * kvn_ref[...]).astype(BF)
        cqn_ref[...] = cqn
        ckvn_ref[...] = ckvn
        cos, sa, sb = cos_ref[...], sa_ref[...], sb_ref[...]
        kr = kr_ref[...]
        dcqn = jnp.zeros((tq, Q_LORA), F32)
        dckvn = jnp.zeros((tq, KV_LORA), F32)
        dkrope = jnp.zeros((tq, LANES), F32)
        dqg = jnp.zeros((1, LANES), F32)
        dkg = jnp.zeros((1, LANES), F32)
        for h in range(MLA_HEADS):
            qh, qr = _rms(_dot(cqn, wq_ref[h]), QK_DIM)
            dqo = _rope_t(dq_ref[h, 0].T * ATT_SCALE, cos, sa, sb)
            dqg = dqg + _colsum(dqo * qh)
            dqraw = _rms_bwd(qh, qr, dqo * qg_ref[...], QK_DIM).astype(BF)
            dqr_ref[:, h * LANES:(h + 1) * LANES] = dqraw
            dcqn = dcqn + _dot(dqraw, wq_ref[h], NT)
            kh, krs = _rms(_dot(ckvn, wk_ref[h]) + kr, QK_DIM)
            dko = _rope_t(dk_ref[h], cos, sa, sb)
            dkg = dkg + _colsum(dko * kh)
            dkraw = _rms_bwd(kh, krs, dko * kg_ref[...], QK_DIM)
            dkrope = dkrope + dkraw
            dkraw = dkraw.astype(BF)
            dkraw_ref[:, h * LANES:(h + 1) * LANES] = dkraw
            dvb = dv_ref[h].astype(BF)
            dvb_ref[:, h * LANES:(h + 1) * LANES] = dvb
            dckvn = dckvn + _dot(dkraw, wk_ref[h], NT) + _dot(dvb, wv_ref[h], NT)
        dpm_ref[:, 0:Q_LORA] = _rms_bwd(cq_h, cq_r, dcqn * qn_ref[...], Q_LORA)
        dpm_ref[:, Q_LORA:Q_LORA + KV_LORA] = _rms_bwd(ckv_h, ckv_r, dckvn * kvn_ref[...], KV_LORA)
        dpm_ref[:, Q_LORA + KV_LORA:512] = dkrope

        @pl.when(pl.program_id(0) == 0)
        def _():
            dqn_ref[...] = jnp.zeros_like(dqn_ref)
            dkvn_ref[...] = jnp.zeros_like(dkvn_ref)
            dqg_ref[...] = jnp.zeros_like(dqg_ref)
            dkg_ref[...] = jnp.zeros_like(dkg_ref)

        dqn_ref[...] += _colsum(dcqn * cq_h)
        dkvn_ref[...] += _colsum(dckvn * ckv_h)
        dqg_ref[...] += dqg
        dkg_ref[...] += dkg

    tab = pl.BlockSpec((tq, LANES), lambda i: (i, 0))
    full = lambda shape: pl.BlockSpec(shape, lambda i: (0,) * len(shape))
    hblk = pl.BlockSpec((MLA_HEADS, tq, LANES), lambda i: (0, i, 0))
    wide = pl.BlockSpec((tq, MLA_HEADS * LANES), lambda i: (i, 0))
    return _call(
        body, name=name, grid=(t // tq,),
        in_specs=[pl.BlockSpec((tq, Q_LORA), lambda i: (i, 2)), pl.BlockSpec((tq, LANES), lambda i: (i, 6)),
                  pl.BlockSpec((tq, LANES), lambda i: (i, 7)), tab, tab, tab,
                  full((1, Q_LORA)), full((1, KV_LORA)), full((MLA_HEADS, Q_LORA, LANES)),
                  full((MLA_HEADS, KV_LORA, LANES)), full((MLA_HEADS, KV_LORA, LANES)), full((1, LANES)), full((1, LANES)),
                  pl.BlockSpec((MLA_HEADS, 1, LANES, tq), lambda i: (0, i, 0, 0)), hblk, hblk],
        out_specs=[pl.BlockSpec((tq, 512), lambda i: (i, 0)),
                   pl.BlockSpec((tq, Q_LORA), lambda i: (i, 0)), pl.BlockSpec((tq, KV_LORA), lambda i: (i, 0)),
                   wide, wide, wide, full((1, Q_LORA)), full((1, KV_LORA)), full((1, LANES)), full((1, LANES))],
        out_shape=[_sds((t, 512), F32), _sds((t, Q_LORA), BF), _sds((t, KV_LORA), BF),
                   _sds((t, MLA_HEADS * LANES), BF), _sds((t, MLA_HEADS * LANES), BF), _sds((t, MLA_HEADS * LANES), BF),
                   _sds((1, Q_LORA), F32), _sds((1, KV_LORA), F32), _sds((1, LANES), F32), _sds((1, LANES), F32)],
        sem=('arbitrary',),
    )(proj, proj, proj, *tabs, w['q_norm'], w['kv_norm'], w['wq'], w['wk'], w['wv'], w['q_gain'], w['k_gain'], dq, dk, dv)


ATT_BLK = 256
ATT_SCALE = 1.0 / math.sqrt(QK_DIM)


def _overlapped(grid, make_copies):
    ids = [pl.program_id(a) for a in range(len(grid))]
    first = functools.reduce(jnp.logical_and, [i == 0 for i in ids])
    last = functools.reduce(jnp.logical_and, [i == n - 1 for i, n in zip(ids, grid)])

    @pl.when(first)
    def _():
        _start_copies(make_copies())

    @pl.when(last)
    def _():
        _wait_copies(make_copies())


def flash_fwd(q, kt, v, *, name, gather=None):
    t = q.shape[1]
    blk = _tile(t, ATT_BLK)
    grid = (MLA_HEADS // 2, t // blk)

    def body(q_ref, kt_ref, v_ref, *rest):
        if gather is None:
            o_ref, lse_ref = rest
        else:
            src_ref, o_ref, lse_ref, dst_ref, *sems = rest
            _overlapped(grid, lambda: _copies('gather', src_ref, dst_ref, *sems))
        qi = pl.program_id(1)
        row = lax.broadcasted_iota(jnp.int32, (blk, blk), 0)
        col = lax.broadcasted_iota(jnp.int32, (blk, blk), 1)

        def block(j, carry, masked):
            out = []
            for hh in range(2):
                m, l, acc = carry[hh]
                s = _dot(q_ref[hh], kt_ref[hh, j])
                if masked:
                    s = jnp.where(col <= row, s, -jnp.inf)
                m2 = jnp.maximum(m, jnp.max(s, axis=-1, keepdims=True))
                p = jnp.exp(s - m2)
                alpha = jnp.exp(m - m2)
                rows = pl.ds(pl.multiple_of(j * blk, blk), blk)
                out.append((m2, alpha * l + jnp.sum(p, axis=-1, keepdims=True), alpha * acc + _dot(p, v_ref[hh, rows, :])))
            return tuple(out)

        init = (jnp.full((blk, 1), -jnp.inf, F32), jnp.zeros((blk, 1), F32), jnp.zeros((blk, LANES), F32))
        carry = lax.fori_loop(0, qi, lambda j, c: block(j, c, False), (init, init))
        carry = block(qi, carry, True)
        o_acc = jnp.zeros((blk, LANES), F32)
        for hh in range(2):
            m, l, acc = carry[hh]
            o_acc = o_acc + acc / l
            lse_ref[hh, 0] = jnp.broadcast_to(m + jnp.log(l), (blk, LANES)).T[0:1, :]
        o_ref[...] = o_acc

    in_specs = [pl.BlockSpec((2, blk, LANES), lambda p, i: (p, i, 0)),
                pl.BlockSpec((2, t // blk, LANES, blk), lambda p, i: (p, 0, 0, 0)),
                pl.BlockSpec((2, t, LANES), lambda p, i: (p, 0, 0))]
    out_specs = [pl.BlockSpec((blk, LANES), lambda p, i: (i, p)), pl.BlockSpec((2, 1, 1, blk), lambda p, i: (p, i, 0, 0))]
    out_shape = [_sds((t, 512), F32), _sds((MLA_HEADS, t // blk, 1, blk), F32)]
    if gather is None:
        return _call(body, name=name, grid=grid, in_specs=in_specs, out_specs=out_specs, out_shape=out_shape,
                     sem=('parallel', 'parallel'))(q, kt, v)
    return _call(body, name=name, grid=grid, in_specs=in_specs + [_ANY], out_specs=out_specs + [_ANY],
                 out_shape=out_shape + [_sds((NDEV,) + gather.shape, gather.dtype)], scratch=_COMM_SCRATCH,
                 sem=('arbitrary', 'arbitrary'))(q, kt, v, gather)


def mla_out_bwd(o, dyn, g, *, name):
    t = o.shape[0]
    blk = _tile(t, ATT_BLK)

    def body(o_ref, dh_ref, g_ref, do_ref, dot_ref, delta_ref, dg_ref):
        ov = o_ref[...]
        oh, r = _rms(ov, 512)
        dh = dh_ref[...]
        do = _rms_bwd(oh, r, dh * g_ref[...], 512)
        do_ref[...] = do.astype(BF)
        dd = do * ov
        for pb in range(MLA_HEADS // 2):
            cols = slice(pb * LANES, (pb + 1) * LANES)
            dot_ref[pb, 0] = do[:, cols].T.astype(BF)
            ddt = dd[:, cols].T
            delta_ref[2 * pb, 0] = jnp.sum(ddt[0:V_DIM, :], axis=0, keepdims=True)
            delta_ref[2 * pb + 1, 0] = jnp.sum(ddt[V_DIM:LANES, :], axis=0, keepdims=True)

        @pl.when(pl.program_id(0) == 0)
        def _():
            dg_ref[...] = jnp.zeros_like(dg_ref)

        dg_ref[...] += _colsum(dh * oh)

    return _call(
        body, name=name, grid=(t // blk,),
        in_specs=[pl.BlockSpec((blk, 512), lambda i: (i, 0)), pl.BlockSpec((blk, 512), lambda i: (i, 1)),
                  pl.BlockSpec((1, 512), lambda i: (0, 0))],
        out_specs=[pl.BlockSpec((blk, 512), lambda i: (i, 0)), pl.BlockSpec((MLA_HEADS // 2, 1, LANES, blk), lambda i: (0, i, 0, 0)),
                   pl.BlockSpec((MLA_HEADS, 1, 1, blk), lambda i: (0, i, 0, 0)), pl.BlockSpec((1, 512), lambda i: (0, 0))],
        out_shape=[_sds((t, 512), BF), _sds((MLA_HEADS // 2, t // blk, LANES, blk), BF),
                   _sds((MLA_HEADS, t // blk, 1, blk), F32), _sds((1, 512), F32)],
        sem=('arbitrary',),
    )(o, dyn, g)


def flash_bwd(q, qt, k, kt, v, do, dot, lse, delta, *, name, scatter=None):
    t = q.shape[1]
    blk = _tile(t, ATT_BLK)
    nb = t // blk
    grid = (MLA_HEADS, nb)

    def body(q_ref, qt_ref, k_ref, kt_ref, v_ref, do_ref, dot_ref, lse_ref, delta_ref, *rest):
        if scatter is None:
            dqt_ref, dk_ref, dv_ref = rest
        else:
            src_ref, dqt_ref, dk_ref, dv_ref, dst_ref, *sems = rest
            _overlapped(grid, lambda: _copies('scatter', src_ref, dst_ref, *sems))
        h, j = pl.program_id(0), pl.program_id(1)
        row = lax.broadcasted_iota(jnp.int32, (blk, blk), 0)
        col = lax.broadcasted_iota(jnp.int32, (blk, blk), 1)
        lane = lax.broadcasted_iota(jnp.int32, (1, LANES), 1)
        mine = (lane // V_DIM) == (h % 2)

        @pl.when(j == 0)
        def _():
            dqt_ref[...] = jnp.zeros_like(dqt_ref)

        kv, ktv, vv = k_ref[...], kt_ref[...], v_ref[...]

        def block(i, carry, masked):
            dk, dv = carry
            rows = pl.ds(pl.multiple_of(i * blk, blk), blk)
            pt = jnp.exp(_dot(kv, qt_ref[i]) - lse_ref[i])
            if masked:
                pt = jnp.where(col >= row, pt, 0.0)
            dv = dv + _dot(pt, do_ref[rows, :])
            dst = (pt * (_dot(vv, dot_ref[i]) - delta_ref[i])).astype(BF)
            dk = dk + _dot(dst, q_ref[rows, :])
            dqt_ref[i] += _dot(ktv, dst)
            return dk, dv

        zero = jnp.zeros((blk, LANES), F32)
        carry = block(j, (zero, zero), True)
        dk, dv = lax.fori_loop(j + 1, nb, lambda i, c: block(i, c, False), carry)
        dk_ref[...] = dk
        dv_ref[...] = jnp.where(mine, dv, 0.0)

    whole = pl.BlockSpec((None, t, LANES), lambda h, j: (h, 0, 0))
    wholet = pl.BlockSpec((None, nb, LANES, blk), lambda h, j: (h, 0, 0, 0))
    kvb = pl.BlockSpec((None, blk, LANES), lambda h, j: (h, j, 0))
    rowv = pl.BlockSpec((None, nb, 1, blk), lambda h, j: (h, 0, 0, 0))
    in_specs = [whole, wholet, kvb, pl.BlockSpec((None, None, LANES, blk), lambda h, j: (h, j, 0, 0)), kvb,
                pl.BlockSpec((t, LANES), lambda h, j: (0, h // 2)),
                pl.BlockSpec((None, nb, LANES, blk), lambda h, j: (h // 2, 0, 0, 0)), rowv, rowv]
    out_specs = [wholet, kvb, kvb]
    out_shape = [_sds((MLA_HEADS, nb, LANES, blk), F32), _sds((MLA_HEADS, t, LANES), F32), _sds((MLA_HEADS, t, LANES), F32)]
    args = (q, qt, k, kt, v, do, dot, lse, delta)
    if scatter is None:
        return _call(body, name=name, grid=grid, in_specs=in_specs, out_specs=out_specs, out_shape=out_shape,
                     sem=('parallel', 'arbitrary'), vmem=VMEM_BIG)(*args)
    return _call(body, name=name, grid=grid, in_specs=in_specs + [_ANY], out_specs=out_specs + [_ANY],
                 out_shape=out_shape + [_sds(scatter.shape, scatter.dtype)], scratch=_COMM_SCRATCH,
                 sem=('arbitrary', 'arbitrary'), vmem=VMEM_BIG)(*args, scatter)


def mix_out_fwd(x, y_ssm, o, g_ssm, g_mla, w_out, *, name, tq=512):
    t = x.shape[0]
    tq = _tile(t, tq)

    def body(x_ref, ys_ref, o_ref, gs_ref, gm_ref, w_ref, x1_ref, yn_ref):
        ns = (_rms(ys_ref[...], SSM_W)[0] * gs_ref[...]).astype(BF)
        nm = (_rms(o_ref[...], 512)[0] * gm_ref[...]).astype(BF)
        yn_ref[:, 0:SSM_W] = ns
        yn_ref[:, SSM_W:D] = nm
        x1_ref[...] = x_ref[...] + _dot(ns, w_ref[0:SSM_W, :]) + _dot(nm, w_ref[SSM_W:D, :])

    row = lambda w: pl.BlockSpec((tq, w), lambda i: (i, 0))
    vec = pl.BlockSpec((1, 512), lambda i: (0, 0))
    return _call(body, name=name, grid=(t // tq,),
                 in_specs=[row(D), row(512), row(512), vec, vec, pl.BlockSpec((D, D), lambda i: (0, 0))],
                 out_specs=[row(D), row(D)], out_shape=[_sds((t, D), F32), _sds((t, D), BF)], sem=('parallel',),
                 )(x, y_ssm, o, g_ssm, g_mla, w_out)


MEM_SCALE = 1.0 / math.sqrt(MEM_HD)


def memkv_fwd(mem, g, wk, wv, kg, *, name):
    def body(m_ref, g_ref, wk_ref, wv_ref, kg_ref, mh_ref, k_ref, v_ref):
        mh = (_rms(m_ref[...], D)[0] * g_ref[...]).astype(BF)
        mh_ref[...] = mh
        for h in range(MEM_HEADS):
            cols = slice(h * LANES, (h + 1) * LANES)
            k_ref[h] = (_rms(_dot(mh, wk_ref[:, cols]), MEM_HD)[0] * kg_ref[...]).astype(BF)
            v_ref[h] = _dot(mh, wv_ref[:, cols]).astype(BF)

    return _call(body, name=name,
                 out_shape=[_sds((N_MEM, D), BF), _sds((MEM_HEADS, N_MEM, LANES), BF), _sds((MEM_HEADS, N_MEM, LANES), BF)],
                 )(mem, g, wk, wv, kg)


def memkv_bwd(mem, g, wk, wv, kg, dk, dv, *, name):
    def body(m_ref, g_ref, wk_ref, wv_ref, kg_ref, dk_ref, dv_ref, dwk_ref, dwv_ref, dkg_ref, dg_ref):
        mhat, _ = _rms(m_ref[...], D)
        mh = (mhat * g_ref[...]).astype(BF)
        lane = lax.broadcasted_iota(jnp.int32, (1, LANES), 1)
        dkg = jnp.zeros((1, LANES), F32)
        dmh = jnp.zeros((N_MEM, D), F32)
        for h in range(MEM_HEADS):
            cols = slice(h * LANES, (h + 1) * LANES)
            kh, kr = _rms(_dot(mh, wk_ref[:, cols]), MEM_HD)
            dko = dk_ref[h]
            dkg = dkg + _colsum(dko * kh)
            dkraw = _rms_bwd(kh, kr, dko * kg_ref[...], MEM_HD).astype(BF)
            dvh = jnp.where((lane // MEM_HD) == (h % 2), dv_ref[h], 0.0).astype(BF)
            dwk_ref[:, cols] = _dot(mh, dkraw, TN)
            dwv_ref[:, cols] = _dot(mh, dvh, TN)
            dmh = dmh + _dot(dkraw, wk_ref[:, cols], NT) + _dot(dvh, wv_ref[:, cols], NT)
        dkg_ref[...] = dkg
        dg_ref[...] = _colsum(dmh * mhat)

    return _call(body, name=name,
                 out_shape=[_sds((D, 512), F32), _sds((D, 512), F32), _sds((1, LANES), F32), _sds((1, D), F32)],
                 )(mem, g, wk, wv, kg, dk, dv)


def memattn_fwd(x, g, wq, qg, kh, vh, wo, *, name, tq=256):
    t = x.shape[0]
    tq = _tile(t, tq)

    def body(x_ref, g_ref, wq_ref, qg_ref, k_ref, v_ref, wo_ref, x2_ref, hn_ref):
        xv = x_ref[...]
        hn = (_rms(xv, D)[0] * g_ref[...]).astype(BF)
        hn_ref[...] = hn
        out = xv
        for pb in range(MEM_HEADS // 2):
            o = jnp.zeros((tq, LANES), F32)
            for h in (2 * pb, 2 * pb + 1):
                q = _rms(_dot(hn, wq_ref[:, h * LANES:(h + 1) * LANES]), MEM_HD)[0] * qg_ref[...]
                s = _dot(q, k_ref[h], NT) * MEM_SCALE
                p = jnp.exp(s - jnp.max(s, axis=-1, keepdims=True))
                p = p / jnp.sum(p, axis=-1, keepdims=True)
                o = o + _dot(p, v_ref[h])
            out = out + _dot(o, wo_ref[pb * LANES:(pb + 1) * LANES, :])
        x2_ref[...] = out

    full = lambda shape: pl.BlockSpec(shape, lambda i: (0,) * len(shape))
    row = pl.BlockSpec((tq, D), lambda i: (i, 0))
    return _call(body, name=name, grid=(t // tq,),
                 in_specs=[row, full((1, D)), full((D, 512)), full((1, LANES)), full((MEM_HEADS, N_MEM, LANES)),
                           full((MEM_HEADS, N_MEM, LANES)), full((MEM_HEADS * MEM_HD, D))],
                 out_specs=[row, row], out_shape=[_sds((t, D), F32), _sds((t, D), BF)], sem=('parallel',),
                 )(x, g, wq, qg, kh, vh, wo)


def memattn_bwd(x, dx2, g, wq, qg, kh, vh, wo, *, name, tq=256):
    t = x.shape[0]
    tq = _tile(t, tq)

    def body(x_ref, dx2_ref, g_ref, wq_ref, qg_ref, k_ref, v_ref, wo_ref,
             dx_ref, o_ref, dqr_ref, dk_ref, dv_ref, dqg_ref, dg_ref):
        @pl.when(pl.program_id(0) == 0)
        def _():
            dk_ref[...] = jnp.zeros_like(dk_ref)
            dv_ref[...] = jnp.zeros_like(dv_ref)
            dqg_ref[...] = jnp.zeros_like(dqg_ref)
            dg_ref[...] = jnp.zeros_like(dg_ref)

        xhat, xr = _rms(x_ref[...], D)
        hn = (xhat * g_ref[...]).astype(BF)
        dx2 = dx2_ref[...]
        dx2b = dx2.astype(BF)
        dh = jnp.zeros((tq, D), F32)
        dqg = jnp.zeros((1, LANES), F32)
        for pb in range(MEM_HEADS // 2):
            do = _dot(dx2b, wo_ref[pb * LANES:(pb + 1) * LANES, :], NT).astype(BF)
            o = jnp.zeros((tq, LANES), F32)
            for h in (2 * pb, 2 * pb + 1):
                cols = slice(h * LANES, (h + 1) * LANES)
                qh, qr = _rms(_dot(hn, wq_ref[:, cols]), MEM_HD)
                qb = (qh * qg_ref[...]).astype(BF)
                s = _dot(qb, k_ref[h], NT) * MEM_SCALE
                p = jnp.exp(s - jnp.max(s, axis=-1, keepdims=True))
                p = p / jnp.sum(p, axis=-1, keepdims=True)
                pb16 = p.astype(BF)
                o = o + _dot(pb16, v_ref[h])
                dv_ref[h] += _dot(pb16, do, TN)
                dp = _dot(do, v_ref[h], NT)
                ds = (p * (dp - jnp.sum(dp * p, axis=-1, keepdims=True)) * MEM_SCALE).astype(BF)
                dk_ref[h] += _dot(ds, qb, TN)
                dqo = _dot(ds, k_ref[h])
                dqg = dqg + _colsum(dqo * qh)
                dqraw = _rms_bwd(qh, qr, dqo * qg_ref[...], MEM_HD).astype(BF)
                dqr_ref[:, cols] = dqraw
                dh = dh + _dot(dqraw, wq_ref[:, cols], NT)
            o_ref[:, pb * LANES:(pb + 1) * LANES] = o.astype(BF)
        dx_ref[...] = dx2 + _rms_bwd(xhat, xr, dh * g_ref[...], D)
        dqg_ref[...] += dqg
        dg_ref[...] += _colsum(dh * xhat)

    full = lambda shape: pl.BlockSpec(shape, lambda i: (0,) * len(shape))
    row = lambda w: pl.BlockSpec((tq, w), lambda i: (i, 0))
    return _call(body, name=name, grid=(t // tq,),
                 in_specs=[row(D), row(D), full((1, D)), full((D, 512)), full((1, LANES)), full((MEM_HEADS, N_MEM, LANES)),
                           full((MEM_HEADS, N_MEM, LANES)), full((MEM_HEADS * MEM_HD, D))],
                 out_specs=[row(D), row(256), row(512), full((MEM_HEADS, N_MEM, LANES)), full((MEM_HEADS, N_MEM, LANES)),
                            full((1, LANES)), full((1, D))],
                 out_shape=[_sds((t, D), F32), _sds((t, 256), BF), _sds((t, 512), BF),
                            _sds((MEM_HEADS, N_MEM, LANES), F32), _sds((MEM_HEADS, N_MEM, LANES), F32),
                            _sds((1, LANES), F32), _sds((1, D), F32)],
                 sem=('arbitrary',))(x, dx2, g, wq, qg, kh, vh, wo)


def mlp_fwd(x, h, w1, w2, *, name, tq=1024, tf=512):
    t = x.shape[0]
    tq = _tile(t, tq)

    def body(x_ref, h_ref, w1_ref, w2_ref, o_ref):
        @pl.when(pl.program_id(1) == 0)
        def _():
            o_ref[...] = x_ref[...]

        a = jnp.maximum(_dot(h_ref[...], w1_ref[...]), 0.0)
        o_ref[...] += _dot(a * a, w2_ref[...])

    row = pl.BlockSpec((tq, D), lambda i, f: (i, 0))
    return _call(body, name=name, grid=(t // tq, D_FF // tf),
                 in_specs=[row, row, pl.BlockSpec((D, tf), lambda i, f: (0, f)), pl.BlockSpec((tf, D), lambda i, f: (f, 0))],
                 out_specs=row, out_shape=_sds((t, D), F32), sem=('parallel', 'arbitrary'), vmem=VMEM_BIG)(x, h, w1, w2)


def mlp_bwd(h, dx, w1, w2, *, name, tq=1024, tf=512):
    t = h.shape[0]
    tq = _tile(t, tq)

    def body(h_ref, dx_ref, w1_ref, w2_ref, dh_ref, r_ref, da_ref):
        @pl.when(pl.program_id(1) == 0)
        def _():
            dh_ref[...] = jnp.zeros_like(dh_ref)

        a = jnp.maximum(_dot(h_ref[...], w1_ref[...]), 0.0)
        r_ref[...] = (a * a).astype(BF)
        da = (_dot(dx_ref[...], w2_ref[...], NT) * (2.0 * a)).astype(BF)
        da_ref[...] = da
        dh_ref[...] += _dot(da, w1_ref[...], NT)

    row = pl.BlockSpec((tq, D), lambda i, f: (i, 0))
    act = pl.BlockSpec((tq, tf), lambda i, f: (i, f))
    return _call(body, name=name, grid=(t // tq, D_FF // tf),
                 in_specs=[row, row, pl.BlockSpec((D, tf), lambda i, f: (0, f)), pl.BlockSpec((tf, D), lambda i, f: (f, 0))],
                 out_specs=[row, act, act], out_shape=[_sds((t, D), F32), _sds((t, D_FF), BF), _sds((t, D_FF), BF)],
                 sem=('parallel', 'arbitrary'), vmem=VMEM_BIG)(h, dx, w1, w2)


def loss_fwd_bwd(y, target, *, name, tq=512):
    t = y.shape[0]
    tq = _tile(t, tq)

    def body(y_ref, t_ref, dy_ref, l_ref):
        @pl.when(pl.program_id(0) == 0)
        def _():
            l_ref[...] = jnp.zeros_like(l_ref)

        e = y_ref[...] - t_ref[...]
        dy_ref[...] = e * (1.0 / D)
        l_ref[...] += _colsum(e * e) * (0.5 / D)

    row = pl.BlockSpec((tq, D), lambda i: (i, 0))
    return _call(body, name=name, grid=(t // tq,), in_specs=[row, row],
                 out_specs=[row, pl.BlockSpec((1, D), lambda i: (0, 0))],
                 out_shape=[_sds((t, D), F32), _sds((1, D), F32)], sem=('arbitrary',))(y, target)


def prep_big(w):
    w_in = w['w_in']
    z = lambda r, c: jnp.zeros((r, c), w_in.dtype)
    w_in_pad = jnp.concatenate([w_in[:, :896], z(D, 64), w_in[:, 896:928], z(D, 32)], axis=1)
    wq = w['mla_w_uq'].reshape(Q_LORA, MLA_HEADS, QK_DIM).transpose(1, 0, 2)
    wq = jnp.pad(wq, ((0, 0), (0, 0), (0, LANES - QK_DIM)))
    ukv = w['mla_w_ukv'].reshape(KV_LORA, MLA_HEADS, QK_NOPE + V_DIM).transpose(1, 0, 2)
    wk = jnp.pad(ukv[:, :, :QK_NOPE], ((0, 0), (0, 0), (0, LANES - QK_NOPE)))
    vpart = ukv[:, :, QK_NOPE:]
    zv = jnp.zeros_like(vpart)
    odd = (jnp.arange(MLA_HEADS) % 2)[:, None, None] == 1
    wv = jnp.where(odd, jnp.concatenate([zv, vpart], axis=2), jnp.concatenate([vpart, zv], axis=2))
    mq = jnp.pad(w['mem_w_q'].reshape(D, MEM_HEADS, MEM_HD), ((0, 0), (0, 0), (0, LANES - MEM_HD))).reshape(D, 512)
    mkv = w['mem_w_kv'].reshape(D, MEM_HEADS, 2 * MEM_HD)
    mk = jnp.pad(mkv[:, :, :MEM_HD], ((0, 0), (0, 0), (0, LANES - MEM_HD))).reshape(D, 512)
    mvp = mkv[:, :, MEM_HD:]
    zm = jnp.zeros_like(mvp)
    modd = (jnp.arange(MEM_HEADS) % 2)[None, :, None] == 1
    mv = jnp.where(modd, jnp.concatenate([zm, mvp], axis=2), jnp.concatenate([mvp, zm], axis=2)).reshape(D, 512)
    return dict(w_in=w_in_pad, w_glu=w['ssm_w_glu'], wq=wq, wk=wk, wv=wv, w_out=w['w_out'], mq=mq, mk=mk, mv=mv,
                mo=w['mem_w_o'], w1=w['mlp_w1'], w2=w['mlp_w2'])


def prep_small(t, s):
    row = lambda a: a.reshape(1, -1)
    pad = lambda a: jnp.pad(a, (0, LANES - a.shape[0])).reshape(1, LANES)
    out = s5_prep(t, s['ssm_lambda_re'], s['ssm_lambda_im'], s['ssm_log_step'], s['ssm_b_re'], s['ssm_b_im'],
                  s['ssm_c_re'], s['ssm_c_im'])
    out.update(d=row(s['ssm_d']), norm_mix=row(s['norm_mix']), b_glu=row(s['ssm_b_glu']), q_norm=row(s['mla_q_norm']),
               kv_norm=row(s['mla_kv_norm']), q_gain=pad(s['mla_q_gain']), k_gain=pad(s['mla_k_gain']),
               g_ssm=row(s['out_norm_ssm']), g_mla=row(s['out_norm_mla']), norm_mem_q=row(s['norm_mem_q']),
               norm_mem_kv=row(s['norm_mem_kv']), mem_q_gain=pad(s['mem_q_gain']), mem_k_gain=pad(s['mem_k_gain']),
               norm_mlp=row(s['norm_mlp']))
    return out


def _perm(a):
    t, c = a.shape
    return a.reshape(SEGS, t // SEGS, c).transpose(1, 0, 2).reshape(t, c)


def _unperm(a):
    t, c = a.shape
    return a.reshape(t // SEGS, SEGS, c).transpose(1, 0, 2).reshape(t, c)


def layer_fwd(l, x, mem, tabs, wb, ws, gather=None):
    n = lambda s: f'l{l}_{s}'
    h1 = rmsnorm_fwd(x, ws['norm_mix'], name=n('norm_mix'))
    proj = mm(h1, wb['w_in'], 'nn', name=n('w_in'))
    ypre_p = s5_fwd(_perm(proj[:, :SSM_W]), ws, name=n('s5'))
    ypre = _unperm(ypre_p)
    y_ssm = glu_fwd(ypre, wb['w_glu'], ws['b_glu'], name=n('glu'))
    mw = dict(q_norm=ws['q_norm'], kv_norm=ws['kv_norm'], wq=wb['wq'], wk=wb['wk'], wv=wb['wv'],
              q_gain=ws['q_gain'], k_gain=ws['k_gain'])
    q, qt, k, kt, v = mla_prep_fwd(proj, tabs, mw, name=n('mla_prep'))
    o, lse, *gathered = flash_fwd(q, kt, v, name=n('flash'), gather=gather)
    x1, yn = mix_out_fwd(x, y_ssm, o, ws['g_ssm'], ws['g_mla'], wb['w_out'], name=n('mix_out'))
    mh, kh, vh = memkv_fwd(mem, ws['norm_mem_kv'], wb['mk'], wb['mv'], ws['mem_k_gain'], name=n('memkv'))
    x2, h2 = memattn_fwd(x1, ws['norm_mem_q'], wb['mq'], ws['mem_q_gain'], kh, vh, wb['mo'], name=n('memattn'))
    h3 = rmsnorm_fwd(x2, ws['norm_mlp'], name=n('norm_mlp'))
    x3 = mlp_fwd(x2, h3, wb['w1'], wb['w2'], name=n('mlp'))
    saved = dict(x=x, h1=h1, proj=proj, ypre=ypre, y_ssm=y_ssm, q=q, qt=qt, k=k, kt=kt, v=v, o=o, lse=lse, x1=x1, yn=yn,
                 kh=kh, vh=vh, x2=x2, h2=h2, h3=h3, mw=mw)
    return x3, saved, (gathered[0] if gathered else None)


def layer_bwd(l, dx3, mem, tabs, wb, ws, sv, scatter=None):
    n = lambda s: f'l{l}_{s}_bwd'
    gb, gs = {}, {}
    dx3b = dx3.astype(BF)
    dh3, r, da = mlp_bwd(sv['h3'], dx3b, wb['w1'], wb['w2'], name=n('mlp'))
    gb['w1'] = mm(sv['h3'], da, 'tn', name=n('w1'))
    gb['w2'] = mm(r, dx3b, 'tn', name=n('w2'))
    dx2, gs['norm_mlp'] = rmsnorm_bwd(sv['x2'], ws['norm_mlp'], dh3, dx3, name=n('norm_mlp'))
    dx1, o_mem, dqr_mem, dkh, dvh, gs['mem_q_gain'], gs['norm_mem_q'] = memattn_bwd(
        sv['x1'], dx2, ws['norm_mem_q'], wb['mq'], ws['mem_q_gain'], sv['kh'], sv['vh'], wb['mo'], name=n('memattn'))
    dx2b = dx2.astype(BF)
    gb['mo'] = mm(o_mem, dx2b, 'tn', name=n('mo'))
    gb['mq'] = mm(sv['h2'], dqr_mem, 'tn', name=n('mq'))
    gb['mk'], gb['mv'], gs['mem_k_gain'], gs['norm_mem_kv'] = memkv_bwd(
        mem, ws['norm_mem_kv'], wb['mk'], wb['mv'], ws['mem_k_gain'], dkh, dvh, name=n('memkv'))
    dx1b = dx1.astype(BF)
    dyn = mm(dx1b, wb['w_out'], 'nt', name=n('w_out_dx'))
    gb['w_out'] = mm(sv['yn'], dx1b, 'tn', name=n('w_out'))
    dy_ssm, gs['g_ssm'] = rmsnorm_bwd(sv['y_ssm'], ws['g_ssm'], dyn, None, name=n('out_norm_ssm'), col=0)
    do, dot, delta, gs['g_mla'] = mla_out_bwd(sv['o'], dyn, ws['g_mla'], name=n('out_norm_mla'))
    dq, dk, dv, *received = flash_bwd(sv['q'], sv['qt'], sv['k'], sv['kt'], sv['v'], do, dot, sv['lse'], delta,
                                      name=n('flash'), scatter=scatter)
    (dproj_m, cqn, ckvn, dqr, dkr, dvb, gs['q_norm'], gs['kv_norm'], gs['q_gain'], gs['k_gain']) = mla_prep_bwd(
        sv['proj'], tabs, sv['mw'], dq, dk, dv, name=n('mla_prep'))
    by_head = lambda g: g.reshape(g.shape[0], MLA_HEADS, LANES).transpose(1, 0, 2)
    gb['wq'] = by_head(mm(cqn, dqr, 'tn', name=n('wq')))
    gb['wk'] = by_head(mm(ckvn, dkr, 'tn', name=n('wk')))
    gb['wv'] = by_head(mm(ckvn, dvb, 'tn', name=n('wv')))
    dypre, yg, dz, gs['b_glu'] = glu_bwd(sv['ypre'], dy_ssm, wb['w_glu'], ws['b_glu'], name=n('glu'))
    gb['w_glu'] = mm(yg, dz, 'tn', name=n('w_glu'))
    u_p = _perm(sv['proj'][:, :SSM_W])
    du_p, gs['ar'], gs['ai'], gs['bre'], gs['bim'], gs['cre'], gs['cim'], gs['d'] = s5_bwd(u_p, _perm(dypre), ws, name=n('s5'))
    dproj = jnp.concatenate([_unperm(du_p), dproj_m], axis=1)
    dprojb = dproj.astype(BF)
    dh1 = mm(dprojb, wb['w_in'], 'nt', name=n('w_in_dx'))
    gb['w_in'] = mm(sv['h1'], dprojb, 'tn', name=n('w_in'))
    dx0, gs['norm_mix'] = rmsnorm_bwd(sv['x'], ws['norm_mix'], dh1, dx1, name=n('norm_mix'))
    return dx0, gb, gs, (received[0] if received else None)


def local_step(x, mem, positions, target, small, big_of, gather_src=None, pack=None):
    t = x.shape[0]
    tabs = rope_tables(positions)
    layers, gathered, big_struct = [], None, None
    for l in range(DEPTH):
        big_l = big_of(l, gathered)
        big_struct = {k: _sds(big_l[k].shape, F32) for k in BIG}
        wb = prep_big(big_l)
        ws, small_vjp = jax.vjp(functools.partial(prep_small, t), {k: small[k][l] for k in SMALL})
        nxt = gather_src[l + 1] if gather_src is not None and l + 1 < DEPTH else None
        x, sv, gathered = layer_fwd(l, x, mem, tabs, wb, ws, gather=nxt)
        layers.append((wb, ws, small_vjp, sv))
    dx, lcols = loss_fwd_bwd(x, target, name='loss')
    loss = jnp.sum(lcols)
    gbig, gsmall, received, pending = [None] * DEPTH, [None] * DEPTH, [None] * DEPTH, None
    for l in reversed(range(DEPTH)):
        wb, ws, small_vjp, sv = layers[l]
        dx, gb, gs, rec = layer_bwd(l, dx, mem, tabs, wb, ws, sv, scatter=pending)
        if pending is not None:
            received[l + 1] = rec
        gs['pr'], gs['pi'] = jnp.zeros_like(ws['pr']), jnp.zeros_like(ws['pi'])
        gbig[l] = jax.linear_transpose(prep_big, big_struct)(gb)[0]
        gsmall[l] = small_vjp(gs)[0]
        pending = pack(gbig[l]) if pack is not None else None
    return loss, dx, gbig, gsmall, received, pending


def _peer(k):
    x, y, c = lax.axis_index('x'), lax.axis_index('y'), lax.axis_index('c')
    px, py, pc = x ^ ((k >> 2) & 1), y ^ ((k >> 1) & 1), c ^ (k & 1)
    return (px, py, pc), 4 * px + 2 * py + pc


def _copies(kind, src_ref, dst_ref, send_sems, recv_sems, loc_sem):
    _, me = _peer(0)
    src = (lambda p: src_ref.at[p]) if kind == 'scatter' else (lambda p: src_ref)
    local = pltpu.make_async_copy(src(me), dst_ref.at[me], loc_sem)
    sends, recvs = [], []
    for k in range(1, NDEV):
        dev, p = _peer(k)
        for slot, lst in ((me, sends), (p, recvs)):
            lst.append(pltpu.make_async_remote_copy(src_ref=src(p), dst_ref=dst_ref.at[slot], send_sem=send_sems.at[k - 1],
                                                    recv_sem=recv_sems.at[k - 1], device_id=dev,
                                                    device_id_type=pl.DeviceIdType.MESH))
    return local, sends, recvs


def _start_copies(cs):
    local, sends, _ = cs
    local.start()
    for cp in sends:
        cp.start()


def _wait_copies(cs):
    local, sends, recvs = cs
    for cp in sends:
        cp.wait_send()
    for cp in recvs:
        cp.wait_recv()
    local.wait()


_COMM_SCRATCH = (pltpu.SemaphoreType.DMA((NDEV - 1,)), pltpu.SemaphoreType.DMA((NDEV - 1,)), pltpu.SemaphoreType.DMA(()))
_ANY = pl.BlockSpec(memory_space=pl.ANY)


def exchange(a, b, *, name):
    ins = [v for v in (a, b) if v is not None]
    n_in = len(ins)
    outs = []
    if a is not None:
        outs.append(_sds(a.shape, a.dtype))
    if b is not None:
        outs.append(_sds((NDEV,) + b.shape, b.dtype))

    def body(*refs):
        in_refs, out_refs = refs[:n_in], refs[n_in:2 * n_in]
        send_sems, recv_sems, loc_sems = refs[2 * n_in:]
        kinds = (['scatter'] if a is not None else []) + (['gather'] if b is not None else [])
        sets = [_copies(kind, in_refs[i], out_refs[i], send_sems.at[i], recv_sems.at[i], loc_sems.at[i])
                for i, kind in enumerate(kinds)]
        for cs in sets:
            _start_copies(cs)
        for cs in sets:
            _wait_copies(cs)

    anyspec = pl.BlockSpec(memory_space=pl.ANY)
    res = pl.pallas_call(
        body, name=name, in_specs=[anyspec] * n_in, out_specs=[anyspec] * n_in, out_shape=outs,
        scratch_shapes=[pltpu.SemaphoreType.DMA((n_in, NDEV - 1)), pltpu.SemaphoreType.DMA((n_in, NDEV - 1)),
                        pltpu.SemaphoreType.DMA((n_in,))],
    )(*ins)
    res = list(res)
    ra = res.pop(0) if a is not None else None
    rb = res.pop(0) if b is not None else None
    return ra, rb


def adamw(w, m, v, g8, *, name, tr):
    r = w.shape[0]
    c1 = 1.0 / (1.0 - ADAM_B1 ** ADAM_STEP)
    c2 = 1.0 / (1.0 - ADAM_B2 ** ADAM_STEP)

    def body(w_ref, m_ref, v_ref, g_ref, go_ref, d_ref, mo_ref, vo_ref):
        g = g_ref[0].astype(F32)
        for i in range(1, NDEV):
            g = g + g_ref[i].astype(F32)
        m_new = ADAM_B1 * m_ref[...] + (1.0 - ADAM_B1) * g
        v_new = ADAM_B2 * v_ref[...] + (1.0 - ADAM_B2) * (g * g)
        go_ref[...] = g
        mo_ref[...] = m_new
        vo_ref[...] = v_new
        d_ref[...] = -ADAM_LR * ((m_new * c1) / (jnp.sqrt(v_new * c2) + ADAM_EPS) + ADAM_WD * w_ref[...])

    row = pl.BlockSpec((tr, D), lambda i: (i, 0))
    return _call(body, name=name, grid=(r // tr,),
                 in_specs=[row, row, row, pl.BlockSpec((NDEV, tr, D), lambda i: (0, i, 0))],
                 out_specs=[row] * 4, out_shape=[_sds((r, D), F32)] * 4, sem=('parallel',), vmem=VMEM_BIG)(w, m, v, g8)


def _flat_rows(parts, rows):
    flat = jnp.concatenate([p.reshape(-1) for p in parts])
    return jnp.pad(flat, (0, rows * D - flat.shape[0])).reshape(rows, D)


def _unflat(flat2d, shapes):
    flat = flat2d.reshape(-1)
    out, off = [], 0
    for s in shapes:
        n = math.prod(s)
        out.append(flat[off:off + n].reshape(s))
        off += n
    return out


def _to_slots(g, axis):
    l, r, c = g.shape
    if axis == 1:
        return g.reshape(l, NDEV, r // NDEV, c).transpose(1, 0, 2, 3).reshape(NDEV, -1)
    return g.reshape(l, r, NDEV, c // NDEV).transpose(2, 0, 1, 3).reshape(NDEV, -1)


def _from_slots(s, shard_shape, axis):
    l, r, c = shard_shape
    s = s.reshape(NDEV, l, r, c)
    if axis == 1:
        return s.transpose(1, 0, 2, 3).reshape(l, NDEV * r, c)
    return s.transpose(1, 2, 0, 3).reshape(l, r, NDEV * c)


LAYER_ROWS = 1536
BIG_ROWS = DEPTH * LAYER_ROWS
SMALL_ROWS = 640


def kernel(x, mem, positions, norm_mix, w_in, ssm_lambda_re, ssm_lambda_im, ssm_log_step, ssm_b_re, ssm_b_im, ssm_c_re, ssm_c_im, ssm_d, ssm_w_glu, ssm_b_glu, mla_q_norm, mla_w_uq, mla_kv_norm, mla_w_ukv, mla_q_gain, mla_k_gain, out_norm_ssm, out_norm_mla, w_out, norm_mem_q, norm_mem_kv, mem_w_q, mem_w_kv, mem_q_gain, mem_k_gain, mem_w_o, norm_mlp, mlp_w1, mlp_w2, loss_target, m_norm_mix, m_w_in, m_ssm_lambda_re, m_ssm_lambda_im, m_ssm_log_step, m_ssm_b_re, m_ssm_b_im, m_ssm_c_re, m_ssm_c_im, m_ssm_d, m_ssm_w_glu, m_ssm_b_glu, m_mla_q_norm, m_mla_w_uq, m_mla_kv_norm, m_mla_w_ukv, m_mla_q_gain, m_mla_k_gain, m_out_norm_ssm, m_out_norm_mla, m_w_out, m_norm_mem_q, m_norm_mem_kv, m_mem_w_q, m_mem_w_kv, m_mem_q_gain, m_mem_k_gain, m_mem_w_o, m_norm_mlp, m_mlp_w1, m_mlp_w2, v_norm_mix, v_w_in, v_ssm_lambda_re, v_ssm_lambda_im, v_ssm_log_step, v_ssm_b_re, v_ssm_b_im, v_ssm_c_re, v_ssm_c_im, v_ssm_d, v_ssm_w_glu, v_ssm_b_glu, v_mla_q_norm, v_mla_w_uq, v_mla_kv_norm, v_mla_w_ukv, v_mla_q_gain, v_mla_k_gain, v_out_norm_ssm, v_out_norm_mla, v_w_out, v_norm_mem_q, v_norm_mem_kv, v_mem_w_q, v_mem_w_kv, v_mem_q_gain, v_mem_k_gain, v_mem_w_o, v_norm_mlp, v_mlp_w1, v_mlp_w2):
    wvals = (norm_mix, w_in, ssm_lambda_re, ssm_lambda_im, ssm_log_step, ssm_b_re, ssm_b_im, ssm_c_re, ssm_c_im, ssm_d, ssm_w_glu, ssm_b_glu, mla_q_norm, mla_w_uq, mla_kv_norm, mla_w_ukv, mla_q_gain, mla_k_gain, out_norm_ssm, out_norm_mla, w_out, norm_mem_q, norm_mem_kv, mem_w_q, mem_w_kv, mem_q_gain, mem_k_gain, mem_w_o, norm_mlp, mlp_w1, mlp_w2)
    mvals = (m_norm_mix, m_w_in, m_ssm_lambda_re, m_ssm_lambda_im, m_ssm_log_step, m_ssm_b_re, m_ssm_b_im, m_ssm_c_re, m_ssm_c_im, m_ssm_d, m_ssm_w_glu, m_ssm_b_glu, m_mla_q_norm, m_mla_w_uq, m_mla_kv_norm, m_mla_w_ukv, m_mla_q_gain, m_mla_k_gain, m_out_norm_ssm, m_out_norm_mla, m_w_out, m_norm_mem_q, m_norm_mem_kv, m_mem_w_q, m_mem_w_kv, m_mem_q_gain, m_mem_k_gain, m_mem_w_o, m_norm_mlp, m_mlp_w1, m_mlp_w2)
    vvals = (v_norm_mix, v_w_in, v_ssm_lambda_re, v_ssm_lambda_im, v_ssm_log_step, v_ssm_b_re, v_ssm_b_im, v_ssm_c_re, v_ssm_c_im, v_ssm_d, v_ssm_w_glu, v_ssm_b_glu, v_mla_q_norm, v_mla_w_uq, v_mla_kv_norm, v_mla_w_ukv, v_mla_q_gain, v_mla_k_gain, v_out_norm_ssm, v_out_norm_mla, v_w_out, v_norm_mem_q, v_norm_mem_kv, v_mem_w_q, v_mem_w_kv, v_mem_q_gain, v_mem_k_gain, v_mem_w_o, v_norm_mlp, v_mlp_w1, v_mlp_w2)
    w = dict(zip(WEIGHTS, wvals))
    m = dict(zip(WEIGHTS, mvals))
    v = dict(zip(WEIGHTS, vvals))

    shard_shapes = {k: w[k].shape for k in BIG}
    layer_shapes = [shard_shapes[k][1:] for k in BIG]

    def layer_flat(parts):
        flat = jnp.concatenate([p.reshape(DEPTH, -1) for p in parts], axis=1)
        return jnp.pad(flat, ((0, 0), (0, LAYER_ROWS * D - flat.shape[1]))).reshape(DEPTH, LAYER_ROWS, D)

    def big_of(l, gathered):
        g = (first if l == 0 else gathered).reshape(NDEV, -1)
        out, off = {}, 0
        for k, shp in zip(BIG, layer_shapes):
            n = math.prod(shp)
            out[k] = _from_slots(g[:, off:off + n], (1,) + shp, BIG_AXIS[k])[0]
            off += n
        return out

    def pack(g):
        slots = jnp.concatenate([_to_slots(g[k][None], BIG_AXIS[k]) for k in BIG], axis=1)
        return jnp.pad(slots, ((0, 0), (0, LAYER_ROWS * D - slots.shape[1]))).astype(BF).reshape(NDEV, LAYER_ROWS, D)

    mine = layer_flat([w[k].astype(BF) for k in BIG])
    _, first = exchange(None, mine[0], name='gather_layer0')
    small = {k: w[k] for k in SMALL}
    loss, grad_x, _, gsmall, received, last = local_step(x[0], mem[0], positions[0], loss_target[0], small, big_of,
                                                         gather_src=mine, pack=pack)
    gs_full = [jnp.stack([gsmall[l][k] for l in range(DEPTH)]) for k in SMALL]
    small_flat = _flat_rows(gs_full + [loss.reshape(1)], SMALL_ROWS)
    received[0], g8_small = exchange(last, small_flat, name='exchange_last')
    g8_big = jnp.concatenate(received, axis=1)

    small_shapes = [w[k].shape for k in SMALL]
    flat_big = lambda d: layer_flat([d[k] for k in BIG]).reshape(BIG_ROWS, D)
    gb, db, mb, vb = adamw(flat_big(w), flat_big(m), flat_big(v), g8_big, name='adamw_big', tr=256)
    gs, ds, ms, vs = adamw(_flat_rows([w[k] for k in SMALL], SMALL_ROWS), _flat_rows([m[k] for k in SMALL], SMALL_ROWS),
                           _flat_rows([v[k] for k in SMALL], SMALL_ROWS), g8_small, name='adamw_small', tr=128)
    n_small = sum(math.prod(s) for s in small_shapes)
    loss_all = gs.reshape(-1)[n_small]

    def unflat_big(fb):
        fb, out, off = fb.reshape(DEPTH, LAYER_ROWS * D), [], 0
        for k, shp in zip(BIG, layer_shapes):
            n = math.prod(shp)
            out.append(fb[:, off:off + n].reshape(shard_shapes[k]))
            off += n
        return out

    res = {}
    for tag, fb, fs in (('g', gb, gs), ('d', db, ds), ('m', mb, ms), ('v', vb, vs)):
        res[tag] = dict(zip(BIG, unflat_big(fb)))
        res[tag].update(zip(SMALL, _unflat(fs, small_shapes)))
    return (loss_all, grad_x[None], *[res['g'][k] for k in WEIGHTS], *[res['d'][k] for k in WEIGHTS],
            *[res['m'][k] for k in WEIGHTS], *[res['v'][k] for k in WEIGHTS])
```

```python
import functools
import math

import jax
import jax.numpy as jnp
from jax import lax
from jax.experimental import pallas as pl
from jax.experimental.pallas import tpu as pltpu

F32 = jnp.float32
BF = jnp.bfloat16

D = 1024
DEPTH = 4
N_MEM = 256
MEM_HEADS = 4
MEM_HD = 64
SSM_W = 512
SSM_G = 32
SSM_H = 16
SSM_P = 64
MLA_HEADS = 8
QK_NOPE = 64
QK_ROPE = 32
QK_DIM = 96
V_DIM = 64
Q_LORA = 256
KV_LORA = 128
ROPE_THETA = 10000.0
D_FF = 4096
IN_COLS = 928
EPS = 1e-6
NDEV = 8
LANES = 128
SEGS = 8
S5_LW = 256
S5_NHB = (SSM_G * SSM_P) // S5_LW
ADAM_LR = 0.001
ADAM_B1 = 0.9
ADAM_B2 = 0.999
ADAM_EPS = 1e-08
ADAM_WD = 0.01
ADAM_STEP = 10
VMEM_BIG = 56 * 1024 * 1024

NN = (((1,), (0,)), ((), ()))
NT = (((1,), (1,)), ((), ()))
TN = (((0,), (0,)), ((), ()))

BIG_LATE = ('w_out', 'mem_w_q', 'mem_w_kv', 'mem_w_o', 'mlp_w1', 'mlp_w2')
BIG_EARLY = ('w_in', 'ssm_w_glu', 'mla_w_uq', 'mla_w_ukv')
BIG = BIG_LATE + BIG_EARLY
BIG_AXIS = {'w_in': 1, 'ssm_w_glu': 1, 'mla_w_uq': 2, 'mla_w_ukv': 2, 'w_out': 1, 'mem_w_q': 1, 'mem_w_kv': 1,
            'mem_w_o': 2, 'mlp_w1': 2, 'mlp_w2': 1}
SMALL = ('norm_mix', 'ssm_lambda_re', 'ssm_lambda_im', 'ssm_log_step', 'ssm_b_re', 'ssm_b_im', 'ssm_c_re', 'ssm_c_im',
         'ssm_d', 'ssm_b_glu', 'mla_q_norm', 'mla_kv_norm', 'mla_q_gain', 'mla_k_gain', 'out_norm_ssm', 'out_norm_mla',
         'norm_mem_q', 'norm_mem_kv', 'mem_q_gain', 'mem_k_gain', 'norm_mlp')
WEIGHTS = ('norm_mix', 'w_in', 'ssm_lambda_re', 'ssm_lambda_im', 'ssm_log_step', 'ssm_b_re', 'ssm_b_im', 'ssm_c_re',
           'ssm_c_im', 'ssm_d', 'ssm_w_glu', 'ssm_b_glu', 'mla_q_norm', 'mla_w_uq', 'mla_kv_norm', 'mla_w_ukv',
           'mla_q_gain', 'mla_k_gain', 'out_norm_ssm', 'out_norm_mla', 'w_out', 'norm_mem_q', 'norm_mem_kv', 'mem_w_q',
           'mem_w_kv', 'mem_q_gain', 'mem_k_gain', 'mem_w_o', 'norm_mlp', 'mlp_w1', 'mlp_w2')


def _call(body, *, name, out_shape, grid=(), in_specs=None, out_specs=None, scratch=(), sem=None, vmem=None):
    params = {}
    if sem is not None:
        params['dimension_semantics'] = sem
    if vmem is not None:
        params['vmem_limit_bytes'] = vmem
    specs = {} if in_specs is None else dict(grid=grid, in_specs=in_specs, out_specs=out_specs)
    return pl.pallas_call(body, name=name, out_shape=out_shape, scratch_shapes=list(scratch),
                          compiler_params=pltpu.CompilerParams(**params), **specs)


def _sds(shape, dtype):
    return jax.ShapeDtypeStruct(shape, dtype)


def _dot(a, b, dims=NN):
    return lax.dot_general(a.astype(BF), b.astype(BF), dims, preferred_element_type=F32)


def _split(a):
    hi = a.astype(BF)
    return hi, (a - hi.astype(F32)).astype(BF)


def _dot3(a, b, dims=NN):
    ah, al = _split(a)
    bh, bl = _split(b)
    d = lambda p, q: lax.dot_general(p, q, dims, preferred_element_type=F32)
    return d(ah, bh) + (d(ah, bl) + d(al, bh))


_sdot = _dot


def _rms(x, n):
    r = lax.rsqrt(jnp.sum(x * x, axis=-1, keepdims=True) * (1.0 / n) + EPS)
    return x * r, r


def _rms_bwd(xhat, r, dxhat, n):
    return r * (dxhat - xhat * (jnp.sum(dxhat * xhat, axis=-1, keepdims=True) * (1.0 / n)))


def _colsum(a):
    return jnp.sum(a, axis=0, keepdims=True)


def _tile(t, want):
    return min(t, want)


def _bidx(nb):
    return (lambda b: b) if nb > 1 else (lambda b: 0)


def mm(a, b, mode, *, name, out_dtype=F32, tm=512, tn=512, tk=1024):
    squeeze = a.ndim == 2 and b.ndim == 2
    a = a[None] if a.ndim == 2 else a
    b = b[None] if b.ndim == 2 else b
    nb = max(a.shape[0], b.shape[0])
    ab, bb = _bidx(a.shape[0]), _bidx(b.shape[0])
    if mode in ('nn', 'nt'):
        m, k = a.shape[1:]
        n = b.shape[2] if mode == 'nn' else b.shape[1]
        tm, tn = _tile(m, tm), _tile(n, tn)
        dims = NN if mode == 'nn' else NT

        def body(a_ref, b_ref, o_ref):
            o_ref[...] = _dot(a_ref[...], b_ref[...], dims).astype(o_ref.dtype)

        bspec = (pl.BlockSpec((None, k, tn), lambda bi, i, j: (bb(bi), 0, j)) if mode == 'nn'
                 else pl.BlockSpec((None, tn, k), lambda bi, i, j: (bb(bi), j, 0)))
        out = _call(body, name=name, grid=(nb, m // tm, n // tn),
                    in_specs=[pl.BlockSpec((None, tm, k), lambda bi, i, j: (ab(bi), i, 0)), bspec],
                    out_specs=pl.BlockSpec((None, tm, tn), lambda bi, i, j: (bi, i, j)),
                    out_shape=_sds((nb, m, n), out_dtype), sem=('parallel', 'parallel', 'parallel'))(a, b)
    else:
        k, m = a.shape[1:]
        n = b.shape[2]
        tm, tn, tk = _tile(m, 1024), _tile(n, 1024), _tile(k, 512)

        def body(a_ref, b_ref, o_ref):
            @pl.when(pl.program_id(3) == 0)
            def _():
                o_ref[...] = jnp.zeros_like(o_ref)

            o_ref[...] += _dot(a_ref[...], b_ref[...], TN)

        out = _call(body, name=name, grid=(nb, m // tm, n // tn, k // tk),
                    in_specs=[pl.BlockSpec((None, tk, tm), lambda bi, i, j, kk: (ab(bi), kk, i)),
                              pl.BlockSpec((None, tk, tn), lambda bi, i, j, kk: (bb(bi), kk, j))],
                    out_specs=pl.BlockSpec((None, tm, tn), lambda bi, i, j, kk: (bi, i, j)),
                    out_shape=_sds((nb, m, n), F32), sem=('parallel', 'parallel', 'parallel', 'arbitrary'))(a, b)
    return out[0] if squeeze else out


def rmsnorm_fwd(x, g, *, name, tq=512):
    t, d = x.shape
    tq = _tile(t, tq)

    def body(x_ref, g_ref, o_ref):
        xh, _ = _rms(x_ref[...], d)
        o_ref[...] = (xh * g_ref[...]).astype(o_ref.dtype)

    return _call(body, name=name, grid=(t // tq,),
                 in_specs=[pl.BlockSpec((tq, d), lambda i: (i, 0)), pl.BlockSpec((1, d), lambda i: (0, 0))],
                 out_specs=pl.BlockSpec((tq, d), lambda i: (i, 0)), out_shape=_sds((t, d), BF), sem=('parallel',))(x, g)


def rmsnorm_bwd(x, g, dh, dres, *, name, col=0, tq=512):
    t, d = x.shape
    tq = _tile(t, tq)
    has_res = dres is not None

    def body(*refs):
        if has_res:
            x_ref, g_ref, dh_ref, dres_ref, dx_ref, dg_ref = refs
        else:
            x_ref, g_ref, dh_ref, dx_ref, dg_ref = refs
        xh, r = _rms(x_ref[...], d)
        dh_ = dh_ref[...].astype(F32)
        dx = _rms_bwd(xh, r, dh_ * g_ref[...], d)
        if has_res:
            dx = dx + dres_ref[...]
        dx_ref[...] = dx

        @pl.when(pl.program_id(0) == 0)
        def _():
            dg_ref[...] = jnp.zeros_like(dg_ref)

        dg_ref[...] += _colsum(dh_ * xh)

    in_specs = [pl.BlockSpec((tq, d), lambda i: (i, 0)), pl.BlockSpec((1, d), lambda i: (0, 0)),
                pl.BlockSpec((tq, d), lambda i: (i, col))]
    args = [x, g, dh]
    if has_res:
        in_specs.append(pl.BlockSpec((tq, d), lambda i: (i, 0)))
        args.append(dres)
    return _call(body, name=name, grid=(t // tq,), in_specs=in_specs,
                 out_specs=[pl.BlockSpec((tq, d), lambda i: (i, 0)), pl.BlockSpec((1, d), lambda i: (0, 0))],
                 out_shape=[_sds((t, d), F32), _sds((1, d), F32)], sem=('arbitrary',))(*args)


def _cmul(ar, ai, xr, xi):
    return ar * xr - ai * xi, ar * xi + ai * xr


def _seg_carries(er, ei, pr, pi, reverse):
    lw = er.shape[1]
    zero = jnp.zeros((1, lw), F32)
    order = range(SEGS - 1, -1, -1) if reverse else range(SEGS)
    cin_r, cin_i = [None] * SEGS, [None] * SEGS
    tr, ti = zero, zero
    for j in order:
        cin_r[j], cin_i[j] = tr, ti
        mr, mi = _cmul(pr, pi, tr, ti)
        tr, ti = er[j:j + 1, :] + mr, ei[j:j + 1, :] + mi
    return jnp.concatenate(cin_r, axis=0), jnp.concatenate(cin_i, axis=0)


def _s5_chunk(t):
    return _tile(t, 512)


def s5_fwd(u_p, prm, *, name):
    t = u_p.shape[0]
    ch = _s5_chunk(t)
    nch, steps = t // ch, ch // SEGS
    lw = S5_LW

    def body(u_ref, ar_ref, ai_ref, pr_ref, pi_ref, bre_ref, bim_ref, cre_ref, cim_ref, d_ref, y_ref, bur, bui):
        hb = pl.program_id(0)
        ar = jnp.broadcast_to(ar_ref[0], (SEGS, lw))
        ai = jnp.broadcast_to(ai_ref[0], (SEGS, lw))

        def rows_of(c):
            return pl.ds(pl.multiple_of(c * ch, ch), ch)

        @pl.loop(0, nch)
        def _(c):
            u = u_ref[rows_of(c), :]
            bur[rows_of(c), :] = _sdot(u, bre_ref[0])
            bui[rows_of(c), :] = _sdot(u, bim_ref[0])

        def scan(carry, store):
            def step(i, s):
                r0 = pl.multiple_of(i * SEGS, SEGS)
                mr, mi = _cmul(ar, ai, s[0], s[1])
                nr, ni = mr + bur[pl.ds(r0, SEGS), :], mi + bui[pl.ds(r0, SEGS), :]
                if store:
                    bur[pl.ds(r0, SEGS), :] = nr
                    bui[pl.ds(r0, SEGS), :] = ni
                return nr, ni

            return lax.fori_loop(0, t // SEGS, step, carry, unroll=8)

        zero = jnp.zeros((SEGS, lw), F32)
        er, ei = scan((zero, zero), False)
        scan(_seg_carries(er, ei, pr_ref[0], pi_ref[0], False), True)

        @pl.loop(0, nch)
        def _(c):
            rows = rows_of(c)
            y = _sdot(bur[rows, :], cre_ref[0]) - _sdot(bui[rows, :], cim_ref[0])

            @pl.when(hb % 2 == 0)
            def _():
                y_ref[rows, :] = y + d_ref[...] * u_ref[rows, :]

            @pl.when(hb % 2 == 1)
            def _():
                y_ref[rows, :] += y

    vec = pl.BlockSpec((1, 1, lw), lambda h: (h, 0, 0))
    return _call(
        body, name=name, grid=(S5_NHB,),
        in_specs=[pl.BlockSpec((t, LANES), lambda h: (0, h // 2)), vec, vec, vec, vec,
                  pl.BlockSpec((1, LANES, lw), lambda h: (h, 0, 0)), pl.BlockSpec((1, LANES, lw), lambda h: (h, 0, 0)),
                  pl.BlockSpec((1, lw, LANES), lambda h: (h, 0, 0)), pl.BlockSpec((1, lw, LANES), lambda h: (h, 0, 0)),
                  pl.BlockSpec((1, LANES), lambda h: (0, h // 2))],
        out_specs=pl.BlockSpec((t, LANES), lambda h: (0, h // 2)), out_shape=_sds((t, SSM_W), F32),
        scratch=[pltpu.VMEM((t, lw), F32)] * 2, sem=('arbitrary',), vmem=VMEM_BIG,
    )(u_p, prm['ar'], prm['ai'], prm['pr'], prm['pi'], prm['bre'], prm['bim'], prm['cre'], prm['cim'], prm['d'])


def s5_bwd(u_p, dy_p, prm, *, name):
    t = u_p.shape[0]
    ch = _s5_chunk(t)
    nch, steps = t // ch, ch // SEGS
    lw = S5_LW

    def body(u_ref, dy_ref, ar_ref, ai_ref, pr_ref, pi_ref, bre_ref, bim_ref, cre_ref, cim_ref, d_ref,
             du_ref, dar_ref, dai_ref, dbre_ref, dbim_ref, dcre_ref, dcim_ref, dd_ref, bur, bui, sr, si):
        hb = pl.program_id(0)
        ar = jnp.broadcast_to(ar_ref[0], (SEGS, lw))
        ai = jnp.broadcast_to(ai_ref[0], (SEGS, lw))
        zero = jnp.zeros((SEGS, lw), F32)

        def rows_of(c):
            return pl.ds(pl.multiple_of(c * ch, ch), ch)

        nsteps = t // SEGS

        @pl.loop(0, nch)
        def _(c):
            u = u_ref[rows_of(c), :]
            bur[rows_of(c), :] = _sdot(u, bre_ref[0])
            bui[rows_of(c), :] = _sdot(u, bim_ref[0])

        def fwd_scan(carry, store):
            def step(i, s):
                r0 = pl.multiple_of(i * SEGS, SEGS)
                mr, mi = _cmul(ar, ai, s[0], s[1])
                nr, ni = mr + bur[pl.ds(r0, SEGS), :], mi + bui[pl.ds(r0, SEGS), :]
                if store:
                    w0 = pl.multiple_of(i * SEGS + SEGS, SEGS)
                    sr[pl.ds(w0, SEGS), :] = nr
                    si[pl.ds(w0, SEGS), :] = ni
                return nr, ni

            return lax.fori_loop(0, nsteps, step, carry, unroll=8)

        er, ei = fwd_scan((zero, zero), False)
        cin_r, cin_i = _seg_carries(er, ei, pr_ref[0], pi_ref[0], False)
        sr[pl.ds(0, SEGS), :] = cin_r
        si[pl.ds(0, SEGS), :] = cin_i
        fwd_scan((cin_r, cin_i), True)

        @pl.loop(0, nch)
        def _(c):
            dy = dy_ref[rows_of(c), :]
            bur[rows_of(c), :] = _sdot(dy, cre_ref[0], NT)
            bui[rows_of(c), :] = -_sdot(dy, cim_ref[0], NT)

        def rev_local(ii, lam):
            r0 = pl.multiple_of((nsteps - 1 - ii) * SEGS, SEGS)
            mr, mi = _cmul(ar, -ai, lam[0], lam[1])
            return mr + bur[pl.ds(r0, SEGS), :], mi + bui[pl.ds(r0, SEGS), :]

        lr0, li0 = lax.fori_loop(0, nsteps, rev_local, (zero, zero), unroll=8)
        rin = _seg_carries(lr0, li0, pr_ref[0], -pi_ref[0], True)

        def rev_step(ii, st):
            lam_r, lam_i, acc_r, acc_i = st
            r0 = pl.multiple_of((nsteps - 1 - ii) * SEGS, SEGS)
            mr, mi = _cmul(ar, -ai, lam_r, lam_i)
            nr, ni = mr + bur[pl.ds(r0, SEGS), :], mi + bui[pl.ds(r0, SEGS), :]
            bur[pl.ds(r0, SEGS), :] = nr
            bui[pl.ds(r0, SEGS), :] = ni
            pr_, pi_ = sr[pl.ds(r0, SEGS), :], si[pl.ds(r0, SEGS), :]
            return nr, ni, acc_r + (nr * pr_ + ni * pi_), acc_i + (ni * pr_ - nr * pi_)

        _, _, acc_r, acc_i = lax.fori_loop(0, nsteps, rev_step, (rin[0], rin[1], zero, zero), unroll=8)
        dar_ref[0] = _colsum(acc_r)
        dai_ref[0] = _colsum(acc_i)

        dbre_ref[...] = jnp.zeros_like(dbre_ref)
        dbim_ref[...] = jnp.zeros_like(dbim_ref)
        dcre_ref[...] = jnp.zeros_like(dcre_ref)
        dcim_ref[...] = jnp.zeros_like(dcim_ref)

        @pl.loop(0, nch)
        def _(c):
            rows = rows_of(c)
            u = u_ref[rows, :]
            dy = dy_ref[rows, :]
            lam_r, lam_i = bur[rows, :], bui[rows, :]
            du = _sdot(lam_r, bre_ref[0], NT) + _sdot(lam_i, bim_ref[0], NT)

            @pl.when(hb % 2 == 0)
            def _():
                du_ref[rows, :] = du + d_ref[...] * dy

            @pl.when(hb % 2 == 1)
            def _():
                du_ref[rows, :] += du

            dbre_ref[0] += _sdot(u, lam_r, TN)
            dbim_ref[0] += _sdot(u, lam_i, TN)
            srows = pl.ds(pl.multiple_of(c * ch + SEGS, SEGS), ch)
            dcre_ref[0] += _sdot(sr[srows, :], dy, TN)
            dcim_ref[0] -= _sdot(si[srows, :], dy, TN)

        @pl.when(hb % 2 == 0)
        def _():
            dd_ref[...] = _colsum(dy_ref[...] * u_ref[...])

    vec = pl.BlockSpec((1, 1, lw), lambda h: (h, 0, 0))
    bsp = pl.BlockSpec((1, LANES, lw), lambda h: (h, 0, 0))
    csp = pl.BlockSpec((1, lw, LANES), lambda h: (h, 0, 0))
    act = pl.BlockSpec((t, LANES), lambda h: (0, h // 2))
    dsp = pl.BlockSpec((1, LANES), lambda h: (0, h // 2))
    return _call(
        body, name=name, grid=(S5_NHB,),
        in_specs=[act, act, vec, vec, vec, vec, bsp, bsp, csp, csp, dsp],
        out_specs=[act, vec, vec, bsp, bsp, csp, csp, dsp],
        out_shape=[_sds((t, SSM_W), F32), _sds((S5_NHB, 1, lw), F32), _sds((S5_NHB, 1, lw), F32),
                   _sds((S5_NHB, LANES, lw), F32), _sds((S5_NHB, LANES, lw), F32),
                   _sds((S5_NHB, lw, LANES), F32), _sds((S5_NHB, lw, LANES), F32), _sds((1, SSM_W), F32)],
        scratch=[pltpu.VMEM((t, lw), F32), pltpu.VMEM((t, lw), F32),
                 pltpu.VMEM((t + SEGS, lw), F32), pltpu.VMEM((t + SEGS, lw), F32)],
        sem=('arbitrary',), vmem=VMEM_BIG,
    )(u_p, dy_p, prm['ar'], prm['ai'], prm['pr'], prm['pi'], prm['bre'], prm['bim'], prm['cre'], prm['cim'], prm['d'])


def s5_prep(t, lam_re, lam_im, log_step, b_re, b_im, c_re, c_im):
    step = jnp.exp(log_step)[:, None]
    mag = jnp.exp(lam_re * step)
    ar, ai = mag * jnp.cos(lam_im * step), mag * jnp.sin(lam_im * step)
    den = lam_re * lam_re + lam_im * lam_im
    nr, ni = ar - 1.0, ai
    fr, fi = (nr * lam_re + ni * lam_im) / den, (ni * lam_re - nr * lam_im) / den
    bbr = fr[..., None] * b_re - fi[..., None] * b_im
    bbi = fr[..., None] * b_im + fi[..., None] * b_re
    gl = S5_LW // SSM_P
    eye = jnp.eye(gl, dtype=F32)
    half = (jnp.arange(S5_NHB) % 2)[:, None, None]

    def bmat(bb):
        x = bb.transpose(0, 2, 1).reshape(S5_NHB, gl, SSM_H, SSM_P)
        x = jnp.einsum('bghp,gk->bghkp', x, eye).reshape(S5_NHB, gl * SSM_H, S5_LW)
        z = jnp.zeros_like(x)
        return jnp.where(half == 0, jnp.concatenate([x, z], axis=1), jnp.concatenate([z, x], axis=1))

    def cmat(cc):
        x = cc.transpose(0, 2, 1).reshape(S5_NHB, gl, SSM_P, SSM_H)
        x = jnp.einsum('bgph,gk->bgpkh', x, eye).reshape(S5_NHB, S5_LW, gl * SSM_H)
        z = jnp.zeros_like(x)
        return jnp.where(half == 0, jnp.concatenate([x, z], axis=2), jnp.concatenate([z, x], axis=2))

    vec = lambda a: a.reshape(S5_NHB, 1, S5_LW)
    ni_steps = float(t // SEGS)
    pmag = jnp.exp(lam_re * step * ni_steps)
    pr, pi = pmag * jnp.cos(lam_im * step * ni_steps), pmag * jnp.sin(lam_im * step * ni_steps)
    return dict(ar=vec(ar), ai=vec(ai), bre=bmat(bbr), bim=bmat(bbi), cre=cmat(c_re), cim=cmat(c_im),
                pr=lax.stop_gradient(vec(pr)), pi=lax.stop_gradient(vec(pi)))


def _gelu(x):
    c = math.sqrt(2.0 / math.pi)
    return 0.5 * x * (1.0 + jnp.tanh(c * (x + 0.044715 * (x * x * x))))


def _gelu_grad(x):
    c = math.sqrt(2.0 / math.pi)
    th = jnp.tanh(c * (x + 0.044715 * (x * x * x)))
    return 0.5 * (1.0 + th) + 0.5 * x * (1.0 - th * th) * (c * (1.0 + 3.0 * 0.044715 * (x * x)))


def glu_fwd(ypre, w_glu, b_glu, *, name, tq=512):
    t = ypre.shape[0]
    tq = _tile(t, tq)

    def body(y_ref, w_ref, b_ref, o_ref):
        yg = _gelu(y_ref[...])
        z = _dot(yg, w_ref[...]) + b_ref[...]
        o_ref[...] = yg * jax.nn.sigmoid(z)

    return _call(body, name=name, grid=(t // tq,),
                 in_specs=[pl.BlockSpec((tq, SSM_W), lambda i: (i, 0)), pl.BlockSpec((SSM_W, SSM_W), lambda i: (0, 0)),
                           pl.BlockSpec((1, SSM_W), lambda i: (0, 0))],
                 out_specs=pl.BlockSpec((tq, SSM_W), lambda i: (i, 0)), out_shape=_sds((t, SSM_W), F32),
                 sem=('parallel',))(ypre, w_glu, b_glu)


def glu_bwd(ypre, dy, w_glu, b_glu, *, name, tq=512):
    t = ypre.shape[0]
    tq = _tile(t, tq)

    def body(y_ref, dy_ref, w_ref, b_ref, dyp_ref, yg_ref, dz_ref, db_ref):
        ypre_ = y_ref[...]
        yg = _gelu(ypre_)
        sig = jax.nn.sigmoid(_dot(yg, w_ref[...]) + b_ref[...])
        dy_ = dy_ref[...]
        dz = dy_ * yg * sig * (1.0 - sig)
        dyg = dy_ * sig + _dot(dz, w_ref[...], NT)
        dyp_ref[...] = dyg * _gelu_grad(ypre_)
        yg_ref[...] = yg.astype(BF)
        dz_ref[...] = dz.astype(BF)

        @pl.when(pl.program_id(0) == 0)
        def _():
            db_ref[...] = jnp.zeros_like(db_ref)

        db_ref[...] += _colsum(dz)

    row = pl.BlockSpec((tq, SSM_W), lambda i: (i, 0))
    vec = pl.BlockSpec((1, SSM_W), lambda i: (0, 0))
    return _call(body, name=name, grid=(t // tq,),
                 in_specs=[row, row, pl.BlockSpec((SSM_W, SSM_W), lambda i: (0, 0)), vec],
                 out_specs=[row, row, row, vec],
                 out_shape=[_sds((t, SSM_W), F32), _sds((t, SSM_W), BF), _sds((t, SSM_W), BF), _sds((1, SSM_W), F32)],
                 sem=('arbitrary',))(ypre, dy, w_glu, b_glu)


def _rope(x, cos, sa, sb):
    return x * cos + pltpu.roll(x, 16, 1) * sa + pltpu.roll(x, 112, 1) * sb


def _rope_t(d, cos, sa, sb):
    return d * cos + pltpu.roll(d * sa, 112, 1) + pltpu.roll(d * sb, 16, 1)


def rope_tables(positions):
    half = QK_ROPE // 2
    inv_freq = ROPE_THETA ** (-jnp.arange(half, dtype=F32) / half)
    ang = positions.astype(F32)[:, None] * inv_freq
    cos, sin = jnp.cos(ang), jnp.sin(ang)
    t = positions.shape[0]
    one, zero = jnp.ones((t, QK_NOPE), F32), jnp.zeros((t, QK_NOPE), F32)
    pad1, pad0 = jnp.ones((t, 32), F32), jnp.zeros((t, 32), F32)
    z16 = jnp.zeros((t, half), F32)
    return (jnp.concatenate([one, cos, cos, pad1], axis=1), jnp.concatenate([zero, z16, sin, pad0], axis=1),
            jnp.concatenate([zero, -sin, z16, pad0], axis=1))


def mla_prep_fwd(proj, tabs, w, *, name):
    t = proj.shape[0]
    tq = _tile(t, ATT_BLK)

    def body(cq_ref, ckv_ref, kr_ref, cos_ref, sa_ref, sb_ref, qn_ref, kvn_ref, wq_ref, wk_ref, wv_ref, qg_ref, kg_ref,
             q_ref, qt_ref, k_ref, kt_ref, v_ref):
        cqn = (_rms(cq_ref[...], Q_LORA)[0] * qn_ref[...]).astype(BF)
        ckvn = (_rms(ckv_ref[...], KV_LORA)[0] * kvn_ref[...]).astype(BF)
        cos, sa, sb = cos_ref[...], sa_ref[...], sb_ref[...]
        kr = kr_ref[...]
        for h in range(MLA_HEADS):
            q = _rms(_dot(cqn, wq_ref[h]), QK_DIM)[0] * qg_ref[...]
            q = _rope(q, cos, sa, sb) * ATT_SCALE
            q_ref[h] = q.astype(BF)
            qt_ref[h, 0] = q.T.astype(BF)
            k = _rms(_dot(ckvn, wk_ref[h]) + kr, QK_DIM)[0] * kg_ref[...]
            k = _rope(k, cos, sa, sb)
            k_ref[h] = k.astype(BF)
            kt_ref[h, 0] = k.T.astype(BF)
            v_ref[h] = _dot(ckvn, wv_ref[h]).astype(BF)

    tab = pl.BlockSpec((tq, LANES), lambda i: (i, 0))
    full = lambda shape: pl.BlockSpec(shape, lambda i: (0,) * len(shape))
    hout = pl.BlockSpec((MLA_HEADS, tq, LANES), lambda i: (0, i, 0))
    tout = pl.BlockSpec((MLA_HEADS, 1, LANES, tq), lambda i: (0, i, 0, 0))
    hshape = _sds((MLA_HEADS, t, LANES), BF)
    tshape = _sds((MLA_HEADS, t // tq, LANES, tq), BF)
    return _call(
        body, name=name, grid=(t // tq,),
        in_specs=[pl.BlockSpec((tq, Q_LORA), lambda i: (i, 2)), pl.BlockSpec((tq, LANES), lambda i: (i, 6)),
                  pl.BlockSpec((tq, LANES), lambda i: (i, 7)), tab, tab, tab,
                  full((1, Q_LORA)), full((1, KV_LORA)), full((MLA_HEADS, Q_LORA, LANES)),
                  full((MLA_HEADS, KV_LORA, LANES)), full((MLA_HEADS, KV_LORA, LANES)), full((1, LANES)), full((1, LANES))],
        out_specs=[hout, tout, hout, tout, hout], out_shape=[hshape, tshape, hshape, tshape, hshape], sem=('parallel',),
    )(proj, proj, proj, *tabs, w['q_norm'], w['kv_norm'], w['wq'], w['wk'], w['wv'], w['q_gain'], w['k_gain'])


def mla_prep_bwd(proj, tabs, w, dq, dk, dv, *, name):
    t = proj.shape[0]
    tq = _tile(t, ATT_BLK)

    def body(cq_ref, ckv_ref, kr_ref, cos_ref, sa_ref, sb_ref, qn_ref, kvn_ref, wq_ref, wk_ref, wv_ref, qg_ref, kg_ref,
             dq_ref, dk_ref, dv_ref,
             dpm_ref, cqn_ref, ckvn_ref, dqr_ref, dkraw_ref, dvb_ref, dqn_ref, dkvn_ref, dqg_ref, dkg_ref):
        cq_h, cq_r = _rms(cq_ref[...], Q_LORA)
        ckv_h, ckv_r = _rms(ckv_ref[...], KV_LORA)
        cqn = (cq_h * qn_ref[...]).astype(BF)
        ckvn = (ckv_h * kvn_ref[...]).astype(BF)
        cqn_ref[...] = cqn
        ckvn_ref[...] = ckvn
        cos, sa, sb = cos_ref[...], sa_ref[...], sb_ref[...]
        kr = kr_ref[...]
        dcqn = jnp.zeros((tq, Q_LORA), F32)
        dckvn = jnp.zeros((tq, KV_LORA), F32)
        dkrope = jnp.zeros((tq, LANES), F32)
        dqg = jnp.zeros((1, LANES), F32)
        dkg = jnp.zeros((1, LANES), F32)
        for h in range(MLA_HEADS):
            qh, qr = _rms(_dot(cqn, wq_ref[h]), QK_DIM)
            dqo = _rope_t(dq_ref[h, 0].T * ATT_SCALE, cos, sa, sb)
            dqg = dqg + _colsum(dqo * qh)
            dqraw = _rms_bwd(qh, qr, dqo * qg_ref[...], QK_DIM).astype(BF)
            dqr_ref[:, h * LANES:(h + 1) * LANES] = dqraw
            dcqn = dcqn + _dot(dqraw, wq_ref[h], NT)
            kh, krs = _rms(_dot(ckvn, wk_ref[h]) + kr, QK_DIM)
            dko = _rope_t(dk_ref[h], cos, sa, sb)
            dkg = dkg + _colsum(dko * kh)
            dkraw = _rms_bwd(kh, krs, dko * kg_ref[...], QK_DIM)
            dkrope = dkrope + dkraw
            dkraw = dkraw.astype(BF)
            dkraw_ref[:, h * LANES:(h + 1) * LANES] = dkraw
            dvb = dv_ref[h].astype(BF)
            dvb_ref[:, h * LANES:(h + 1) * LANES] = dvb
            dckvn = dckvn + _dot(dkraw, wk_ref[h], NT) + _dot(dvb, wv_ref[h], NT)
        dpm_ref[:, 0:Q_LORA] = _rms_bwd(cq_h, cq_r, dcqn * qn_ref[...], Q_LORA)
        dpm_ref[:, Q_LORA:Q_LORA + KV_LORA] = _rms_bwd(ckv_h, ckv_r, dckvn * kvn_ref[...], KV_LORA)
        dpm_ref[:, Q_LORA + KV_LORA:512] = dkrope

        @pl.when(pl.program_id(0) == 0)
        def _():
            dqn_ref[...] = jnp.zeros_like(dqn_ref)
            dkvn_ref[...] = jnp.zeros_like(dkvn_ref)
            dqg_ref[...] = jnp.zeros_like(dqg_ref)
            dkg_ref[...] = jnp.zeros_like(dkg_ref)

        dqn_ref[...] += _colsum(dcqn * cq_h)
        dkvn_ref[...] += _colsum(dckvn * ckv_h)
        dqg_ref[...] += dqg
        dkg_ref[...] += dkg

    tab = pl.BlockSpec((tq, LANES), lambda i: (i, 0))
    full = lambda shape: pl.BlockSpec(shape, lambda i: (0,) * len(shape))
    hblk = pl.BlockSpec((MLA_HEADS, tq, LANES), lambda i: (0, i, 0))
    wide = pl.BlockSpec((tq, MLA_HEADS * LANES), lambda i: (i, 0))
    return _call(
        body, name=name, grid=(t // tq,),
        in_specs=[pl.BlockSpec((tq, Q_LORA), lambda i: (i, 2)), pl.BlockSpec((tq, LANES), lambda i: (i, 6)),
                  pl.BlockSpec((tq, LANES), lambda i: (i, 7)), tab, tab, tab,
                  full((1, Q_LORA)), full((1, KV_LORA)), full((MLA_HEADS, Q_LORA, LANES)),
                  full((MLA_HEADS, KV_LORA, LANES)), full((MLA_HEADS, KV_LORA, LANES)), full((1, LANES)), full((1, LANES)),
                  pl.BlockSpec((MLA_HEADS, 1, LANES, tq), lambda i: (0, i, 0, 0)), hblk, hblk],
        out_specs=[pl.BlockSpec((tq, 512), lambda i: (i, 0)),
                   pl.BlockSpec((tq, Q_LORA), lambda i: (i, 0)), pl.BlockSpec((tq, KV_LORA), lambda i: (i, 0)),
                   wide, wide, wide, full((1, Q_LORA)), full((1, KV_LORA)), full((1, LANES)), full((1, LANES))],
        out_shape=[_sds((t, 512), F32), _sds((t, Q_LORA), BF), _sds((t, KV_LORA), BF),
                   _sds((t, MLA_HEADS * LANES), BF), _sds((t, MLA_HEADS * LANES), BF), _sds((t, MLA_HEADS * LANES), BF),
                   _sds((1, Q_LORA), F32), _sds((1, KV_LORA), F32), _sds((1, LANES), F32), _sds((1, LANES), F32)],
        sem=('arbitrary',),
    )(proj, proj, proj, *tabs, w['q_norm'], w['kv_norm'], w['wq'], w['wk'], w['wv'], w['q_gain'], w['k_gain'], dq, dk, dv)


ATT_BLK = 256
ATT_SCALE = 1.0 / math.sqrt(QK_DIM)


def _overlapped(grid, make_copies):
    ids = [pl.program_id(a) for a in range(len(grid))]
    first = functools.reduce(jnp.logical_and, [i == 0 for i in ids])
    last = functools.reduce(jnp.logical_and, [i == n - 1 for i, n in zip(ids, grid)])

    @pl.when(first)
    def _():
        for cs in make_copies():
            _start_copies(cs)

    @pl.when(last)
    def _():
        for cs in make_copies():
            _wait_copies(cs)


def flash_fwd(q, kt, v, *, name, gather=()):
    t = q.shape[1]
    blk = _tile(t, ATT_BLK)
    grid = (MLA_HEADS // 2, t // blk)

    def body(q_ref, kt_ref, v_ref, *rest):
        nc = len(gather)
        srcs, (o_ref, lse_ref), dsts, sems = rest[:nc], rest[nc:nc + 2], rest[nc + 2:2 * nc + 2], rest[2 * nc + 2:]
        if nc:
            _overlapped(grid, lambda: [_copies('gather', srcs[i], dsts[i], *sems[3 * i:3 * i + 3]) for i in range(nc)])
        qi = pl.program_id(1)
        row = lax.broadcasted_iota(jnp.int32, (blk, blk), 0)
        col = lax.broadcasted_iota(jnp.int32, (blk, blk), 1)

        def block(j, carry, masked):
            out = []
            for hh in range(2):
                m, l, acc = carry[hh]
                s = _dot(q_ref[hh], kt_ref[hh, j])
                if masked:
                    s = jnp.where(col <= row, s, -jnp.inf)
                m2 = jnp.maximum(m, jnp.max(s, axis=-1, keepdims=True))
                p = jnp.exp(s - m2)
                alpha = jnp.exp(m - m2)
                rows = pl.ds(pl.multiple_of(j * blk, blk), blk)
                out.append((m2, alpha * l + jnp.sum(p, axis=-1, keepdims=True), alpha * acc + _dot(p, v_ref[hh, rows, :])))
            return tuple(out)

        init = (jnp.full((blk, 1), -jnp.inf, F32), jnp.zeros((blk, 1), F32), jnp.zeros((blk, LANES), F32))
        carry = lax.fori_loop(0, qi, lambda j, c: block(j, c, False), (init, init))
        carry = block(qi, carry, True)
        o_acc = jnp.zeros((blk, LANES), F32)
        for hh in range(2):
            m, l, acc = carry[hh]
            o_acc = o_acc + acc / l
            lse_ref[hh, 0] = jnp.broadcast_to(m + jnp.log(l), (blk, LANES)).T[0:1, :]
        o_ref[...] = o_acc

    in_specs = [pl.BlockSpec((2, blk, LANES), lambda p, i: (p, i, 0)),
                pl.BlockSpec((2, t // blk, LANES, blk), lambda p, i: (p, 0, 0, 0)),
                pl.BlockSpec((2, t, LANES), lambda p, i: (p, 0, 0))]
    out_specs = [pl.BlockSpec((blk, LANES), lambda p, i: (i, p)), pl.BlockSpec((2, 1, 1, blk), lambda p, i: (p, i, 0, 0))]
    out_shape = [_sds((t, 512), F32), _sds((MLA_HEADS, t // blk, 1, blk), F32)]
    nc = len(gather)
    return _call(body, name=name, grid=grid, in_specs=in_specs + [_ANY] * nc, out_specs=out_specs + [_ANY] * nc,
                 out_shape=out_shape + [_sds((NDEV,) + g.shape, g.dtype) for g in gather], scratch=_COMM_SCRATCH * nc,
                 sem=('arbitrary', 'arbitrary') if nc else ('parallel', 'parallel'))(q, kt, v, *gather)


def mla_out_bwd(o, dyn, g, *, name):
    t = o.shape[0]
    blk = _tile(t, ATT_BLK)

    def body(o_ref, dh_ref, g_ref, do_ref, dot_ref, delta_ref, dg_ref):
        ov = o_ref[...]
        oh, r = _rms(ov, 512)
        dh = dh_ref[...]
        do = _rms_bwd(oh, r, dh * g_ref[...], 512)
        do_ref[...] = do.astype(BF)
        dd = do * ov
        for pb in range(MLA_HEADS // 2):
            cols = slice(pb * LANES, (pb + 1) * LANES)
            dot_ref[pb, 0] = do[:, cols].T.astype(BF)
            ddt = dd[:, cols].T
            delta_ref[2 * pb, 0] = jnp.sum(ddt[0:V_DIM, :], axis=0, keepdims=True)
            delta_ref[2 * pb + 1, 0] = jnp.sum(ddt[V_DIM:LANES, :], axis=0, keepdims=True)

        @pl.when(pl.program_id(0) == 0)
        def _():
            dg_ref[...] = jnp.zeros_like(dg_ref)

        dg_ref[...] += _colsum(dh * oh)

    return _call(
        body, name=name, grid=(t // blk,),
        in_specs=[pl.BlockSpec((blk, 512), lambda i: (i, 0)), pl.BlockSpec((blk, 512), lambda i: (i, 1)),
                  pl.BlockSpec((1, 512), lambda i: (0, 0))],
        out_specs=[pl.BlockSpec((blk, 512), lambda i: (i, 0)), pl.BlockSpec((MLA_HEADS // 2, 1, LANES, blk), lambda i: (0, i, 0, 0)),
                   pl.BlockSpec((MLA_HEADS, 1, 1, blk), lambda i: (0, i, 0, 0)), pl.BlockSpec((1, 512), lambda i: (0, 0))],
        out_shape=[_sds((t, 512), BF), _sds((MLA_HEADS // 2, t // blk, LANES, blk), BF),
                   _sds((MLA_HEADS, t // blk, 1, blk), F32), _sds((1, 512), F32)],
        sem=('arbitrary',),
    )(o, dyn, g)


def flash_bwd(q, qt, k, kt, v, do, dot, lse, delta, *, name, scatter=()):
    t = q.shape[1]
    blk = _tile(t, ATT_BLK)
    nb = t // blk
    grid = (MLA_HEADS, nb)

    def body(q_ref, qt_ref, k_ref, kt_ref, v_ref, do_ref, dot_ref, lse_ref, delta_ref, *rest):
        nc = len(scatter)
        srcs, (dqt_ref, dk_ref, dv_ref), dsts, sems = rest[:nc], rest[nc:nc + 3], rest[nc + 3:2 * nc + 3], rest[2 * nc + 3:]
        if nc:
            _overlapped(grid, lambda: [_copies('scatter', srcs[i], dsts[i], *sems[3 * i:3 * i + 3]) for i in range(nc)])
        h, j = pl.program_id(0), pl.program_id(1)
        row = lax.broadcasted_iota(jnp.int32, (blk, blk), 0)
        col = lax.broadcasted_iota(jnp.int32, (blk, blk), 1)
        lane = lax.broadcasted_iota(jnp.int32, (1, LANES), 1)
        mine = (lane // V_DIM) == (h % 2)

        @pl.when(j == 0)
        def _():
            dqt_ref[...] = jnp.zeros_like(dqt_ref)

        kv, ktv, vv = k_ref[...], kt_ref[...], v_ref[...]

        def block(i, carry, masked):
            dk, dv = carry
            rows = pl.ds(pl.multiple_of(i * blk, blk), blk)
            pt = jnp.exp(_dot(kv, qt_ref[i]) - lse_ref[i])
            if masked:
                pt = jnp.where(col >= row, pt, 0.0)
            dv = dv + _dot(pt, do_ref[rows, :])
            dst = (pt * (_dot(vv, dot_ref[i]) - delta_ref[i])).astype(BF)
            dk = dk + _dot(dst, q_ref[rows, :])
            dqt_ref[i] += _dot(ktv, dst)
            return dk, dv

        zero = jnp.zeros((blk, LANES), F32)
        carry = block(j, (zero, zero), True)
        npairs = (nb - 1 - j) // 2
        carry = lax.fori_loop(0, npairs, lambda p, c: block(j + 2 + 2 * p, block(j + 1 + 2 * p, c, False), False), carry)
        dk, dv = lax.fori_loop(j + 1 + 2 * npairs, nb, lambda i, c: block(i, c, False), carry)
        dk_ref[...] = dk
        dv_ref[...] = jnp.where(mine, dv, 0.0)

    whole = pl.BlockSpec((None, t, LANES), lambda h, j: (h, 0, 0))
    wholet = pl.BlockSpec((None, nb, LANES, blk), lambda h, j: (h, 0, 0, 0))
    kvb = pl.BlockSpec((None, blk, LANES), lambda h, j: (h, j, 0))
    rowv = pl.BlockSpec((None, nb, 1, blk), lambda h, j: (h, 0, 0, 0))
    in_specs = [whole, wholet, kvb, pl.BlockSpec((None, None, LANES, blk), lambda h, j: (h, j, 0, 0)), kvb,
                pl.BlockSpec((t, LANES), lambda h, j: (0, h // 2)),
                pl.BlockSpec((None, nb, LANES, blk), lambda h, j: (h // 2, 0, 0, 0)), rowv, rowv]
    out_specs = [wholet, kvb, kvb]
    out_shape = [_sds((MLA_HEADS, nb, LANES, blk), F32), _sds((MLA_HEADS, t, LANES), F32), _sds((MLA_HEADS, t, LANES), F32)]
    args = (q, qt, k, kt, v, do, dot, lse, delta)
    nc = len(scatter)
    return _call(body, name=name, grid=grid, in_specs=in_specs + [_ANY] * nc, out_specs=out_specs + [_ANY] * nc,
                 out_shape=out_shape + [_sds(s.shape, s.dtype) for s in scatter], scratch=_COMM_SCRATCH * nc,
                 sem=('arbitrary', 'arbitrary') if nc else ('parallel', 'arbitrary'), vmem=VMEM_BIG)(*args, *scatter)


def mix_out_fwd(x, y_ssm, o, g_ssm, g_mla, w_out, *, name, tq=512):
    t = x.shape[0]
    tq = _tile(t, tq)

    def body(x_ref, ys_ref, o_ref, gs_ref, gm_ref, w_ref, x1_ref, yn_ref):
        ns = (_rms(ys_ref[...], SSM_W)[0] * gs_ref[...]).astype(BF)
        nm = (_rms(o_ref[...], 512)[0] * gm_ref[...]).astype(BF)
        yn_ref[:, 0:SSM_W] = ns
        yn_ref[:, SSM_W:D] = nm
        x1_ref[...] = x_ref[...] + _dot(ns, w_ref[0:SSM_W, :]) + _dot(nm, w_ref[SSM_W:D, :])

    row = lambda w: pl.BlockSpec((tq, w), lambda i: (i, 0))
    vec = pl.BlockSpec((1, 512), lambda i: (0, 0))
    return _call(body, name=name, grid=(t // tq,),
                 in_specs=[row(D), row(512), row(512), vec, vec, pl.BlockSpec((D, D), lambda i: (0, 0))],
                 out_specs=[row(D), row(D)], out_shape=[_sds((t, D), F32), _sds((t, D), BF)], sem=('parallel',),
                 )(x, y_ssm, o, g_ssm, g_mla, w_out)


MEM_SCALE = 1.0 / math.sqrt(MEM_HD)


def memkv_fwd(mem, g, wk, wv, kg, *, name):
    def body(m_ref, g_ref, wk_ref, wv_ref, kg_ref, mh_ref, k_ref, v_ref):
        mh = (_rms(m_ref[...], D)[0] * g_ref[...]).astype(BF)
        mh_ref[...] = mh
        for h in range(MEM_HEADS):
            cols = slice(h * LANES, (h + 1) * LANES)
            k_ref[h] = (_rms(_dot(mh, wk_ref[:, cols]), MEM_HD)[0] * kg_ref[...]).astype(BF)
            v_ref[h] = _dot(mh, wv_ref[:, cols]).astype(BF)

    return _call(body, name=name,
                 out_shape=[_sds((N_MEM, D), BF), _sds((MEM_HEADS, N_MEM, LANES), BF), _sds((MEM_HEADS, N_MEM, LANES), BF)],
                 )(mem, g, wk, wv, kg)


def memkv_bwd(mem, g, wk, wv, kg, dk, dv, *, name):
    def body(m_ref, g_ref, wk_ref, wv_ref, kg_ref, dk_ref, dv_ref, dwk_ref, dwv_ref, dkg_ref, dg_ref):
        mhat, _ = _rms(m_ref[...], D)
        mh = (mhat * g_ref[...]).astype(BF)
        lane = lax.broadcasted_iota(jnp.int32, (1, LANES), 1)
        dkg = jnp.zeros((1, LANES), F32)
        dmh = jnp.zeros((N_MEM, D), F32)
        for h in range(MEM_HEADS):
            cols = slice(h * LANES, (h + 1) * LANES)
            kh, kr = _rms(_dot(mh, wk_ref[:, cols]), MEM_HD)
            dko = dk_ref[h]
            dkg = dkg + _colsum(dko * kh)
            dkraw = _rms_bwd(kh, kr, dko * kg_ref[...], MEM_HD).astype(BF)
            dvh = jnp.where((lane // MEM_HD) == (h % 2), dv_ref[h], 0.0).astype(BF)
            dwk_ref[:, cols] = _dot(mh, dkraw, TN)
            dwv_ref[:, cols] = _dot(mh, dvh, TN)
            dmh = dmh + _dot(dkraw, wk_ref[:, cols], NT) + _dot(dvh, wv_ref[:, cols], NT)
        dkg_ref[...] = dkg
        dg_ref[...] = _colsum(dmh * mhat)

    return _call(body, name=name,
                 out_shape=[_sds((D, 512), F32), _sds((D, 512), F32), _sds((1, LANES), F32), _sds((1, D), F32)],
                 )(mem, g, wk, wv, kg, dk, dv)


def memattn_fwd(x, g, wq, qg, kh, vh, wo, *, name, tq=256):
    t = x.shape[0]
    tq = _tile(t, tq)

    def body(x_ref, g_ref, wq_ref, qg_ref, k_ref, v_ref, wo_ref, x2_ref, hn_ref):
        xv = x_ref[...]
        hn = (_rms(xv, D)[0] * g_ref[...]).astype(BF)
        hn_ref[...] = hn
        out = xv
        for pb in range(MEM_HEADS // 2):
            o = jnp.zeros((tq, LANES), F32)
            for h in (2 * pb, 2 * pb + 1):
                q = _rms(_dot(hn, wq_ref[:, h * LANES:(h + 1) * LANES]), MEM_HD)[0] * qg_ref[...]
                s = _dot(q, k_ref[h], NT) * MEM_SCALE
                p = jnp.exp(s - jnp.max(s, axis=-1, keepdims=True))
                p = p / jnp.sum(p, axis=-1, keepdims=True)
                o = o + _dot(p, v_ref[h])
            out = out + _dot(o, wo_ref[pb * LANES:(pb + 1) * LANES, :])
        x2_ref[...] = out

    full = lambda shape: pl.BlockSpec(shape, lambda i: (0,) * len(shape))
    row = pl.BlockSpec((tq, D), lambda i: (i, 0))
    return _call(body, name=name, grid=(t // tq,),
                 in_specs=[row, full((1, D)), full((D, 512)), full((1, LANES)), full((MEM_HEADS, N_MEM, LANES)),
                           full((MEM_HEADS, N_MEM, LANES)), full((MEM_HEADS * MEM_HD, D))],
                 out_specs=[row, row], out_shape=[_sds((t, D), F32), _sds((t, D), BF)], sem=('parallel',),
                 )(x, g, wq, qg, kh, vh, wo)


def memattn_bwd(x, dx2, g, wq, qg, kh, vh, wo, *, name, tq=256):
    t = x.shape[0]
    tq = _tile(t, tq)

    def body(x_ref, dx2_ref, g_ref, wq_ref, qg_ref, k_ref, v_ref, wo_ref,
             dx_ref, o_ref, dqr_ref, dk_ref, dv_ref, dqg_ref, dg_ref):
        @pl.when(pl.program_id(0) == 0)
        def _():
            dk_ref[...] = jnp.zeros_like(dk_ref)
            dv_ref[...] = jnp.zeros_like(dv_ref)
            dqg_ref[...] = jnp.zeros_like(dqg_ref)
            dg_ref[...] = jnp.zeros_like(dg_ref)

        xhat, xr = _rms(x_ref[...], D)
        hn = (xhat * g_ref[...]).astype(BF)
        dx2 = dx2_ref[...]
        dx2b = dx2.astype(BF)
        dh = jnp.zeros((tq, D), F32)
        dqg = jnp.zeros((1, LANES), F32)
        for pb in range(MEM_HEADS // 2):
            do = _dot(dx2b, wo_ref[pb * LANES:(pb + 1) * LANES, :], NT).astype(BF)
            o = jnp.zeros((tq, LANES), F32)
            for h in (2 * pb, 2 * pb + 1):
                cols = slice(h * LANES, (h + 1) * LANES)
                qh, qr = _rms(_dot(hn, wq_ref[:, cols]), MEM_HD)
                qb = (qh * qg_ref[...]).astype(BF)
                s = _dot(qb, k_ref[h], NT) * MEM_SCALE
                p = jnp.exp(s - jnp.max(s, axis=-1, keepdims=True))
                p = p / jnp.sum(p, axis=-1, keepdims=True)
                pb16 = p.astype(BF)
                o = o + _dot(pb16, v_ref[h])
                dv_ref[h] += _dot(pb16, do, TN)
                dp = _dot(do, v_ref[h], NT)
                ds = (p * (dp - jnp.sum(dp * p, axis=-1, keepdims=True)) * MEM_SCALE).astype(BF)
                dk_ref[h] += _dot(ds, qb, TN)
                dqo = _dot(ds, k_ref[h])
                dqg = dqg + _colsum(dqo * qh)
                dqraw = _rms_bwd(qh, qr, dqo * qg_ref[...], MEM_HD).astype(BF)
                dqr_ref[:, cols] = dqraw
                dh = dh + _dot(dqraw, wq_ref[:, cols], NT)
            o_ref[:, pb * LANES:(pb + 1) * LANES] = o.astype(BF)
        dx_ref[...] = dx2 + _rms_bwd(xhat, xr, dh * g_ref[...], D)
        dqg_ref[...] += dqg
        dg_ref[...] += _colsum(dh * xhat)

    full = lambda shape: pl.BlockSpec(shape, lambda i: (0,) * len(shape))
    row = lambda w: pl.BlockSpec((tq, w), lambda i: (i, 0))
    return _call(body, name=name, grid=(t // tq,),
                 in_specs=[row(D), row(D), full((1, D)), full((D, 512)), full((1, LANES)), full((MEM_HEADS, N_MEM, LANES)),
                           full((MEM_HEADS, N_MEM, LANES)), full((MEM_HEADS * MEM_HD, D))],
                 out_specs=[row(D), row(256), row(512), full((MEM_HEADS, N_MEM, LANES)), full((MEM_HEADS, N_MEM, LANES)),
                            full((1, LANES)), full((1, D))],
                 out_shape=[_sds((t, D), F32), _sds((t, 256), BF), _sds((t, 512), BF),
                            _sds((MEM_HEADS, N_MEM, LANES), F32), _sds((MEM_HEADS, N_MEM, LANES), F32),
                            _sds((1, LANES), F32), _sds((1, D), F32)],
                 sem=('arbitrary',))(x, dx2, g, wq, qg, kh, vh, wo)


def mlp_fwd(x, h, w1, w2, *, name, tq=1024, tf=512):
    t = x.shape[0]
    tq = _tile(t, tq)

    def body(x_ref, h_ref, w1_ref, w2_ref, o_ref):
        @pl.when(pl.program_id(1) == 0)
        def _():
            o_ref[...] = x_ref[...]

        a = jnp.maximum(_dot(h_ref[...], w1_ref[...]), 0.0)
        o_ref[...] += _dot(a * a, w2_ref[...])

    row = pl.BlockSpec((tq, D), lambda i, f: (i, 0))
    return _call(body, name=name, grid=(t // tq, D_FF // tf),
                 in_specs=[row, row, pl.BlockSpec((D, tf), lambda i, f: (0, f)), pl.BlockSpec((tf, D), lambda i, f: (f, 0))],
                 out_specs=row, out_shape=_sds((t, D), F32), sem=('parallel', 'arbitrary'), vmem=VMEM_BIG)(x, h, w1, w2)


def mlp_bwd(h, dx, w1, w2, *, name, tq=1024, tf=512):
    t = h.shape[0]
    tq = _tile(t, tq)

    def body(h_ref, dx_ref, w1_ref, w2_ref, dh_ref, r_ref, da_ref):
        @pl.when(pl.program_id(1) == 0)
        def _():
            dh_ref[...] = jnp.zeros_like(dh_ref)

        a = jnp.maximum(_dot(h_ref[...], w1_ref[...]), 0.0)
        r_ref[...] = (a * a).astype(BF)
        da = (_dot(dx_ref[...], w2_ref[...], NT) * (2.0 * a)).astype(BF)
        da_ref[...] = da
        dh_ref[...] += _dot(da, w1_ref[...], NT)

    row = pl.BlockSpec((tq, D), lambda i, f: (i, 0))
    act = pl.BlockSpec((tq, tf), lambda i, f: (i, f))
    return _call(body, name=name, grid=(t // tq, D_FF // tf),
                 in_specs=[row, row, pl.BlockSpec((D, tf), lambda i, f: (0, f)), pl.BlockSpec((tf, D), lambda i, f: (f, 0))],
                 out_specs=[row, act, act], out_shape=[_sds((t, D), F32), _sds((t, D_FF), BF), _sds((t, D_FF), BF)],
                 sem=('parallel', 'arbitrary'), vmem=VMEM_BIG)(h, dx, w1, w2)


def loss_fwd_bwd(y, target, *, name, tq=512):
    t = y.shape[0]
    tq = _tile(t, tq)

    def body(y_ref, t_ref, dy_ref, l_ref):
        @pl.when(pl.program_id(0) == 0)
        def _():
            l_ref[...] = jnp.zeros_like(l_ref)

        e = y_ref[...] - t_ref[...]
        dy_ref[...] = e * (1.0 / D)
        l_ref[...] += _colsum(e * e) * (0.5 / D)

    row = pl.BlockSpec((tq, D), lambda i: (i, 0))
    return _call(body, name=name, grid=(t // tq,), in_specs=[row, row],
                 out_specs=[row, pl.BlockSpec((1, D), lambda i: (0, 0))],
                 out_shape=[_sds((t, D), F32), _sds((1, D), F32)], sem=('arbitrary',))(y, target)


def prep_early(w):
    w_in = w['w_in']
    z = lambda r, c: jnp.zeros((r, c), w_in.dtype)
    w_in_pad = jnp.concatenate([w_in[:, :896], z(D, 64), w_in[:, 896:928], z(D, 32)], axis=1)
    wq = w['mla_w_uq'].reshape(Q_LORA, MLA_HEADS, QK_DIM).transpose(1, 0, 2)
    wq = jnp.pad(wq, ((0, 0), (0, 0), (0, LANES - QK_DIM)))
    ukv = w['mla_w_ukv'].reshape(KV_LORA, MLA_HEADS, QK_NOPE + V_DIM).transpose(1, 0, 2)
    wk = jnp.pad(ukv[:, :, :QK_NOPE], ((0, 0), (0, 0), (0, LANES - QK_NOPE)))
    vpart = ukv[:, :, QK_NOPE:]
    zv = jnp.zeros_like(vpart)
    odd = (jnp.arange(MLA_HEADS) % 2)[:, None, None] == 1
    wv = jnp.where(odd, jnp.concatenate([zv, vpart], axis=2), jnp.concatenate([vpart, zv], axis=2))
    return dict(w_in=w_in_pad, w_glu=w['ssm_w_glu'], wq=wq, wk=wk, wv=wv)


def prep_late(w):
    mq = jnp.pad(w['mem_w_q'].reshape(D, MEM_HEADS, MEM_HD), ((0, 0), (0, 0), (0, LANES - MEM_HD))).reshape(D, 512)
    mkv = w['mem_w_kv'].reshape(D, MEM_HEADS, 2 * MEM_HD)
    mk = jnp.pad(mkv[:, :, :MEM_HD], ((0, 0), (0, 0), (0, LANES - MEM_HD))).reshape(D, 512)
    mvp = mkv[:, :, MEM_HD:]
    zm = jnp.zeros_like(mvp)
    modd = (jnp.arange(MEM_HEADS) % 2)[None, :, None] == 1
    mv = jnp.where(modd, jnp.concatenate([zm, mvp], axis=2), jnp.concatenate([mvp, zm], axis=2)).reshape(D, 512)
    return dict(w_out=w['w_out'], mq=mq, mk=mk, mv=mv, mo=w['mem_w_o'], w1=w['mlp_w1'], w2=w['mlp_w2'])


def prep_small(t, s):
    row = lambda a: a.reshape(1, -1)
    pad = lambda a: jnp.pad(a, (0, LANES - a.shape[0])).reshape(1, LANES)
    out = s5_prep(t, s['ssm_lambda_re'], s['ssm_lambda_im'], s['ssm_log_step'], s['ssm_b_re'], s['ssm_b_im'],
                  s['ssm_c_re'], s['ssm_c_im'])
    out.update(d=row(s['ssm_d']), norm_mix=row(s['norm_mix']), b_glu=row(s['ssm_b_glu']), q_norm=row(s['mla_q_norm']),
               kv_norm=row(s['mla_kv_norm']), q_gain=pad(s['mla_q_gain']), k_gain=pad(s['mla_k_gain']),
               g_ssm=row(s['out_norm_ssm']), g_mla=row(s['out_norm_mla']), norm_mem_q=row(s['norm_mem_q']),
               norm_mem_kv=row(s['norm_mem_kv']), mem_q_gain=pad(s['mem_q_gain']), mem_k_gain=pad(s['mem_k_gain']),
               norm_mlp=row(s['norm_mlp']))
    return out


def _perm(a):
    t, c = a.shape
    return a.reshape(SEGS, t // SEGS, c).transpose(1, 0, 2).reshape(t, c)


def _unperm(a):
    t, c = a.shape
    return a.reshape(t // SEGS, SEGS, c).transpose(1, 0, 2).reshape(t, c)


def layer_fwd(l, x, mem, tabs, plan, ws):
    n = lambda s: f'l{l}_{s}'
    wb = prep_early(plan.early(l))
    h1 = rmsnorm_fwd(x, ws['norm_mix'], name=n('norm_mix'))
    proj = mm(h1, wb['w_in'], 'nn', name=n('w_in'))
    ypre_p = s5_fwd(_perm(proj[:, :SSM_W]), ws, name=n('s5'))
    ypre = _unperm(ypre_p)
    y_ssm = glu_fwd(ypre, wb['w_glu'], ws['b_glu'], name=n('glu'))
    mw = dict(q_norm=ws['q_norm'], kv_norm=ws['kv_norm'], wq=wb['wq'], wk=wb['wk'], wv=wb['wv'],
              q_gain=ws['q_gain'], k_gain=ws['k_gain'])
    q, qt, k, kt, v = mla_prep_fwd(proj, tabs, mw, name=n('mla_prep'))
    o, lse, *gathered = flash_fwd(q, kt, v, name=n('flash'), gather=plan.gather_src(l))
    plan.gathered(l, gathered)
    wb.update(prep_late(plan.late(l)))
    x1, yn = mix_out_fwd(x, y_ssm, o, ws['g_ssm'], ws['g_mla'], wb['w_out'], name=n('mix_out'))
    mh, kh, vh = memkv_fwd(mem, ws['norm_mem_kv'], wb['mk'], wb['mv'], ws['mem_k_gain'], name=n('memkv'))
    x2, h2 = memattn_fwd(x1, ws['norm_mem_q'], wb['mq'], ws['mem_q_gain'], kh, vh, wb['mo'], name=n('memattn'))
    h3 = rmsnorm_fwd(x2, ws['norm_mlp'], name=n('norm_mlp'))
    x3 = mlp_fwd(x2, h3, wb['w1'], wb['w2'], name=n('mlp'))
    saved = dict(x=x, h1=h1, proj=proj, ypre=ypre, y_ssm=y_ssm, q=q, qt=qt, k=k, kt=kt, v=v, o=o, lse=lse, x1=x1, yn=yn,
                 kh=kh, vh=vh, x2=x2, h2=h2, h3=h3, mw=mw)
    return x3, wb, saved


def layer_bwd(l, dx3, mem, tabs, plan, wb, ws, sv):
    n = lambda s: f'l{l}_{s}_bwd'
    gb, gs = {}, {}
    structs = lambda names: {k: _sds(plan.shapes[k], F32) for k in names}
    dx3b = dx3.astype(BF)
    dh3, r, da = mlp_bwd(sv['h3'], dx3b, wb['w1'], wb['w2'], name=n('mlp'))
    gb['w1'] = mm(sv['h3'], da, 'tn', name=n('w1'))
    gb['w2'] = mm(r, dx3b, 'tn', name=n('w2'))
    dx2, gs['norm_mlp'] = rmsnorm_bwd(sv['x2'], ws['norm_mlp'], dh3, dx3, name=n('norm_mlp'))
    dx1, o_mem, dqr_mem, dkh, dvh, gs['mem_q_gain'], gs['norm_mem_q'] = memattn_bwd(
        sv['x1'], dx2, ws['norm_mem_q'], wb['mq'], ws['mem_q_gain'], sv['kh'], sv['vh'], wb['mo'], name=n('memattn'))
    dx2b = dx2.astype(BF)
    gb['mo'] = mm(o_mem, dx2b, 'tn', name=n('mo'))
    gb['mq'] = mm(sv['h2'], dqr_mem, 'tn', name=n('mq'))
    gb['mk'], gb['mv'], gs['mem_k_gain'], gs['norm_mem_kv'] = memkv_bwd(
        mem, ws['norm_mem_kv'], wb['mk'], wb['mv'], ws['mem_k_gain'], dkh, dvh, name=n('memkv'))
    dx1b = dx1.astype(BF)
    dyn = mm(dx1b, wb['w_out'], 'nt', name=n('w_out_dx'))
    gb['w_out'] = mm(sv['yn'], dx1b, 'tn', name=n('w_out'))
    dy_ssm, gs['g_ssm'] = rmsnorm_bwd(sv['y_ssm'], ws['g_ssm'], dyn, None, name=n('out_norm_ssm'), col=0)
    do, dot, delta, gs['g_mla'] = mla_out_bwd(sv['o'], dyn, ws['g_mla'], name=n('out_norm_mla'))
    late = {k: gb.pop(k) for k in ('w_out', 'mq', 'mk', 'mv', 'mo', 'w1', 'w2')}
    plan.late_grads(l, jax.linear_transpose(prep_late, structs(BIG_LATE))(late)[0])
    dq, dk, dv, *received = flash_bwd(sv['q'], sv['qt'], sv['k'], sv['kt'], sv['v'], do, dot, sv['lse'], delta,
                                      name=n('flash'), scatter=plan.scatter_src(l))
    plan.scattered(l, received)
    (dproj_m, cqn, ckvn, dqr, dkr, dvb, gs['q_norm'], gs['kv_norm'], gs['q_gain'], gs['k_gain']) = mla_prep_bwd(
        sv['proj'], tabs, sv['mw'], dq, dk, dv, name=n('mla_prep'))
    by_head = lambda g: g.reshape(g.shape[0], MLA_HEADS, LANES).transpose(1, 0, 2)
    gb['wq'] = by_head(mm(cqn, dqr, 'tn', name=n('wq')))
    gb['wk'] = by_head(mm(ckvn, dkr, 'tn', name=n('wk')))
    gb['wv'] = by_head(mm(ckvn, dvb, 'tn', name=n('wv')))
    dypre, yg, dz, gs['b_glu'] = glu_bwd(sv['ypre'], dy_ssm, wb['w_glu'], ws['b_glu'], name=n('glu'))
    gb['w_glu'] = mm(yg, dz, 'tn', name=n('w_glu'))
    u_p = _perm(sv['proj'][:, :SSM_W])
    du_p, gs['ar'], gs['ai'], gs['bre'], gs['bim'], gs['cre'], gs['cim'], gs['d'] = s5_bwd(u_p, _perm(dypre), ws, name=n('s5'))
    dproj = jnp.concatenate([_unperm(du_p), dproj_m], axis=1)
    dprojb = dproj.astype(BF)
    dh1 = mm(dprojb, wb['w_in'], 'nt', name=n('w_in_dx'))
    gb['w_in'] = mm(sv['h1'], dprojb, 'tn', name=n('w_in'))
    dx0, gs['norm_mix'] = rmsnorm_bwd(sv['x'], ws['norm_mix'], dh1, dx1, name=n('norm_mix'))
    plan.early_grads(l, jax.linear_transpose(prep_early, structs(BIG_EARLY))(gb)[0])
    return dx0, gs


def local_step(x, mem, positions, target, small, plan):
    t = x.shape[0]
    tabs = rope_tables(positions)
    layers = []
    for l in range(DEPTH):
        ws, small_vjp = jax.vjp(functools.partial(prep_small, t), {k: small[k][l] for k in SMALL})
        x, wb, sv = layer_fwd(l, x, mem, tabs, plan, ws)
        layers.append((wb, ws, small_vjp, sv))
    dx, lcols = loss_fwd_bwd(x, target, name='loss')
    loss = jnp.sum(lcols)
    gsmall = [None] * DEPTH
    for l in reversed(range(DEPTH)):
        wb, ws, small_vjp, sv = layers[l]
        dx, gs = layer_bwd(l, dx, mem, tabs, plan, wb, ws, sv)
        gs['pr'], gs['pi'] = jnp.zeros_like(ws['pr']), jnp.zeros_like(ws['pi'])
        gsmall[l] = small_vjp(gs)[0]
    return loss, dx, gsmall


class ExchangePlan:
    def __init__(self, shard_shapes, mine, first_early):
        self.shapes = {k: (s[1] * (NDEV if BIG_AXIS[k] == 1 else 1), s[2] * (NDEV if BIG_AXIS[k] == 2 else 1))
                       for k, s in shard_shapes.items()}
        self.shard = {k: s[1:] for k, s in shard_shapes.items()}
        self.mine = mine
        self.w_early = {0: first_early}
        self.w_late = {}
        self.g_late, self.g_early = {}, {}
        self.r_late, self.r_early = {}, {}

    def _unpack(self, g, names):
        g, out, off = g.reshape(NDEV, -1), {}, 0
        for k in names:
            n = math.prod(self.shard[k])
            out[k] = _from_slots(g[:, off:off + n], (1,) + self.shard[k], BIG_AXIS[k])[0]
            off += n
        return out

    def _pack(self, g, names, rows):
        slots = jnp.concatenate([_to_slots(g[k][None], BIG_AXIS[k]) for k in names], axis=1)
        return jnp.pad(slots, ((0, 0), (0, rows * D - slots.shape[1]))).astype(BF).reshape(NDEV, rows, D)

    def early(self, l):
        return self._unpack(self.w_early.pop(l), BIG_EARLY)

    def late(self, l):
        return self._unpack(self.w_late.pop(l), BIG_LATE)

    def gather_src(self, l):
        src = [self.mine[l, :LATE_ROWS]]
        if l + 1 < DEPTH:
            src.append(self.mine[l + 1, LATE_ROWS:])
        return tuple(src)

    def gathered(self, l, res):
        self.w_late[l] = res[0]
        if l + 1 < DEPTH:
            self.w_early[l + 1] = res[1]

    def late_grads(self, l, g):
        self.g_late[l] = self._pack(g, BIG_LATE, LATE_ROWS)

    def early_grads(self, l, g):
        self.g_early[l] = self._pack(g, BIG_EARLY, LAYER_ROWS - LATE_ROWS)

    def scatter_src(self, l):
        src = [self.g_late.pop(l)]
        if l + 1 < DEPTH:
            src.append(self.g_early.pop(l + 1))
        return tuple(src)

    def scattered(self, l, res):
        self.r_late[l] = res[0]
        if l + 1 < DEPTH:
            self.r_early[l + 1] = res[1]


def _peer(k):
    x, y, c = lax.axis_index('x'), lax.axis_index('y'), lax.axis_index('c')
    px, py, pc = x ^ ((k >> 2) & 1), y ^ ((k >> 1) & 1), c ^ (k & 1)
    return (px, py, pc), 4 * px + 2 * py + pc


def _copies(kind, src_ref, dst_ref, send_sems, recv_sems, loc_sem):
    _, me = _peer(0)
    src = (lambda p: src_ref.at[p]) if kind == 'scatter' else (lambda p: src_ref)
    local = pltpu.make_async_copy(src(me), dst_ref.at[me], loc_sem)
    sends, recvs = [], []
    for k in range(1, NDEV):
        dev, p = _peer(k)
        for slot, lst in ((me, sends), (p, recvs)):
            lst.append(pltpu.make_async_remote_copy(src_ref=src(p), dst_ref=dst_ref.at[slot], send_sem=send_sems.at[k - 1],
                                                    recv_sem=recv_sems.at[k - 1], device_id=dev,
                                                    device_id_type=pl.DeviceIdType.MESH))
    return local, sends, recvs


def _start_copies(cs):
    local, sends, _ = cs
    local.start()
    for cp in sends:
        cp.start()


def _wait_copies(cs):
    local, sends, recvs = cs
    for cp in sends:
        cp.wait_send()
    for cp in recvs:
        cp.wait_recv()
    local.wait()


_COMM_SCRATCH = (pltpu.SemaphoreType.DMA((NDEV - 1,)), pltpu.SemaphoreType.DMA((NDEV - 1,)), pltpu.SemaphoreType.DMA(()))
_ANY = pl.BlockSpec(memory_space=pl.ANY)


def exchange(a, b, *, name):
    ins = [v for v in (a, b) if v is not None]
    n_in = len(ins)
    outs = []
    if a is not None:
        outs.append(_sds(a.shape, a.dtype))
    if b is not None:
        outs.append(_sds((NDEV,) + b.shape, b.dtype))

    def body(*refs):
        in_refs, out_refs = refs[:n_in], refs[n_in:2 * n_in]
        send_sems, recv_sems, loc_sems = refs[2 * n_in:]
        kinds = (['scatter'] if a is not None else []) + (['gather'] if b is not None else [])
        sets = [_copies(kind, in_refs[i], out_refs[i], send_sems.at[i], recv_sems.at[i], loc_sems.at[i])
                for i, kind in enumerate(kinds)]
        for cs in sets:
            _start_copies(cs)
        for cs in sets:
            _wait_copies(cs)

    anyspec = pl.BlockSpec(memory_space=pl.ANY)
    res = pl.pallas_call(
        body, name=name, in_specs=[anyspec] * n_in, out_specs=[anyspec] * n_in, out_shape=outs,
        scratch_shapes=[pltpu.SemaphoreType.DMA((n_in, NDEV - 1)), pltpu.SemaphoreType.DMA((n_in, NDEV - 1)),
                        pltpu.SemaphoreType.DMA((n_in,))],
    )(*ins)
    res = list(res)
    ra = res.pop(0) if a is not None else None
    rb = res.pop(0) if b is not None else None
    return ra, rb


def adamw(w, m, v, g8, *, name, tr):
    r = w.shape[0]
    c1 = 1.0 / (1.0 - ADAM_B1 ** ADAM_STEP)
    c2 = 1.0 / (1.0 - ADAM_B2 ** ADAM_STEP)

    def body(w_ref, m_ref, v_ref, g_ref, go_ref, d_ref, mo_ref, vo_ref):
        g = g_ref[0].astype(F32)
        for i in range(1, NDEV):
            g = g + g_ref[i].astype(F32)
        m_new = ADAM_B1 * m_ref[...] + (1.0 - ADAM_B1) * g
        v_new = ADAM_B2 * v_ref[...] + (1.0 - ADAM_B2) * (g * g)
        go_ref[...] = g
        mo_ref[...] = m_new
        vo_ref[...] = v_new
        d_ref[...] = -ADAM_LR * ((m_new * c1) / (jnp.sqrt(v_new * c2) + ADAM_EPS) + ADAM_WD * w_ref[...])

    row = pl.BlockSpec((tr, D), lambda i: (i, 0))
    return _call(body, name=name, grid=(r // tr,),
                 in_specs=[row, row, row, pl.BlockSpec((NDEV, tr, D), lambda i: (0, i, 0))],
                 out_specs=[row] * 4, out_shape=[_sds((r, D), F32)] * 4, sem=('parallel',), vmem=VMEM_BIG)(w, m, v, g8)


def _flat_rows(parts, rows):
    flat = jnp.concatenate([p.reshape(-1) for p in parts])
    return jnp.pad(flat, (0, rows * D - flat.shape[0])).reshape(rows, D)


def _unflat(flat2d, shapes):
    flat = flat2d.reshape(-1)
    out, off = [], 0
    for s in shapes:
        n = math.prod(s)
        out.append(flat[off:off + n].reshape(s))
        off += n
    return out


def _to_slots(g, axis):
    l, r, c = g.shape
    if axis == 1:
        return g.reshape(l, NDEV, r // NDEV, c).transpose(1, 0, 2, 3).reshape(NDEV, -1)
    return g.reshape(l, r, NDEV, c // NDEV).transpose(2, 0, 1, 3).reshape(NDEV, -1)


def _from_slots(s, shard_shape, axis):
    l, r, c = shard_shape
    s = s.reshape(NDEV, l, r, c)
    if axis == 1:
        return s.transpose(1, 0, 2, 3).reshape(l, NDEV * r, c)
    return s.transpose(1, 2, 0, 3).reshape(l, r, NDEV * c)


LATE_ROWS = 1280
LAYER_ROWS = 1536
BIG_ROWS = DEPTH * LAYER_ROWS
SMALL_ROWS = 640


def kernel(x, mem, positions, norm_mix, w_in, ssm_lambda_re, ssm_lambda_im, ssm_log_step, ssm_b_re, ssm_b_im, ssm_c_re, ssm_c_im, ssm_d, ssm_w_glu, ssm_b_glu, mla_q_norm, mla_w_uq, mla_kv_norm, mla_w_ukv, mla_q_gain, mla_k_gain, out_norm_ssm, out_norm_mla, w_out, norm_mem_q, norm_mem_kv, mem_w_q, mem_w_kv, mem_q_gain, mem_k_gain, mem_w_o, norm_mlp, mlp_w1, mlp_w2, loss_target, m_norm_mix, m_w_in, m_ssm_lambda_re, m_ssm_lambda_im, m_ssm_log_step, m_ssm_b_re, m_ssm_b_im, m_ssm_c_re, m_ssm_c_im, m_ssm_d, m_ssm_w_glu, m_ssm_b_glu, m_mla_q_norm, m_mla_w_uq, m_mla_kv_norm, m_mla_w_ukv, m_mla_q_gain, m_mla_k_gain, m_out_norm_ssm, m_out_norm_mla, m_w_out, m_norm_mem_q, m_norm_mem_kv, m_mem_w_q, m_mem_w_kv, m_mem_q_gain, m_mem_k_gain, m_mem_w_o, m_norm_mlp, m_mlp_w1, m_mlp_w2, v_norm_mix, v_w_in, v_ssm_lambda_re, v_ssm_lambda_im, v_ssm_log_step, v_ssm_b_re, v_ssm_b_im, v_ssm_c_re, v_ssm_c_im, v_ssm_d, v_ssm_w_glu, v_ssm_b_glu, v_mla_q_norm, v_mla_w_uq, v_mla_kv_norm, v_mla_w_ukv, v_mla_q_gain, v_mla_k_gain, v_out_norm_ssm, v_out_norm_mla, v_w_out, v_norm_mem_q, v_norm_mem_kv, v_mem_w_q, v_mem_w_kv, v_mem_q_gain, v_mem_k_gain, v_mem_w_o, v_norm_mlp, v_mlp_w1, v_mlp_w2):
    wvals = (norm_mix, w_in, ssm_lambda_re, ssm_lambda_im, ssm_log_step, ssm_b_re, ssm_b_im, ssm_c_re, ssm_c_im, ssm_d, ssm_w_glu, ssm_b_glu, mla_q_norm, mla_w_uq, mla_kv_norm, mla_w_ukv, mla_q_gain, mla_k_gain, out_norm_ssm, out_norm_mla, w_out, norm_mem_q, norm_mem_kv, mem_w_q, mem_w_kv, mem_q_gain, mem_k_gain, mem_w_o, norm_mlp, mlp_w1, mlp_w2)
    mvals = (m_norm_mix, m_w_in, m_ssm_lambda_re, m_ssm_lambda_im, m_ssm_log_step, m_ssm_b_re, m_ssm_b_im, m_ssm_c_re, m_ssm_c_im, m_ssm_d, m_ssm_w_glu, m_ssm_b_glu, m_mla_q_norm, m_mla_w_uq, m_mla_kv_norm, m_mla_w_ukv, m_mla_q_gain, m_mla_k_gain, m_out_norm_ssm, m_out_norm_mla, m_w_out, m_norm_mem_q, m_norm_mem_kv, m_mem_w_q, m_mem_w_kv, m_mem_q_gain, m_mem_k_gain, m_mem_w_o, m_norm_mlp, m_mlp_w1, m_mlp_w2)
    vvals = (v_norm_mix, v_w_in, v_ssm_lambda_re, v_ssm_lambda_im, v_ssm_log_step, v_ssm_b_re, v_ssm_b_im, v_ssm_c_re, v_ssm_c_im, v_ssm_d, v_ssm_w_glu, v_ssm_b_glu, v_mla_q_norm, v_mla_w_uq, v_mla_kv_norm, v_mla_w_ukv, v_mla_q_gain, v_mla_k_gain, v_out_norm_ssm, v_out_norm_mla, v_w_out, v_norm_mem_q, v_norm_mem_kv, v_mem_w_q, v_mem_w_kv, v_mem_q_gain, v_mem_k_gain, v_mem_w_o, v_norm_mlp, v_mlp_w1, v_mlp_w2)
    w = dict(zip(WEIGHTS, wvals))
    m = dict(zip(WEIGHTS, mvals))
    v = dict(zip(WEIGHTS, vvals))

    shard_shapes = {k: w[k].shape for k in BIG}
    layer_shapes = [shard_shapes[k][1:] for k in BIG]

    def layer_flat(parts):
        flat = jnp.concatenate([p.reshape(DEPTH, -1) for p in parts], axis=1)
        return jnp.pad(flat, ((0, 0), (0, LAYER_ROWS * D - flat.shape[1]))).reshape(DEPTH, LAYER_ROWS, D)

    mine = layer_flat([w[k].astype(BF) for k in BIG])
    _, first = exchange(None, mine[0, LATE_ROWS:], name='gather_early0')
    plan = ExchangePlan(shard_shapes, mine, first)
    small = {k: w[k] for k in SMALL}
    loss, grad_x, gsmall = local_step(x[0], mem[0], positions[0], loss_target[0], small, plan)
    gs_full = [jnp.stack([gsmall[l][k] for l in range(DEPTH)]) for k in SMALL]
    small_flat = _flat_rows(gs_full + [loss.reshape(1)], SMALL_ROWS)
    plan.r_early[0], g8_small = exchange(plan.g_early.pop(0), small_flat, name='exchange_last')
    g8_big = jnp.concatenate([r[l] for l in range(DEPTH) for r in (plan.r_late, plan.r_early)], axis=1)

    small_shapes = [w[k].shape for k in SMALL]
    flat_big = lambda d: layer_flat([d[k] for k in BIG]).reshape(BIG_ROWS, D)
    gb, db, mb, vb = adamw(flat_big(w), flat_big(m), flat_big(v), g8_big, name='adamw_big', tr=256)
    gs, ds, ms, vs = adamw(_flat_rows([w[k] for k in SMALL], SMALL_ROWS), _flat_rows([m[k] for k in SMALL], SMALL_ROWS),
                           _flat_rows([v[k] for k in SMALL], SMALL_ROWS), g8_small, name='adamw_small', tr=128)
    n_small = sum(math.prod(s) for s in small_shapes)
    loss_all = gs.reshape(-1)[n_small]

    def unflat_big(fb):
        fb, out, off = fb.reshape(DEPTH, LAYER_ROWS * D), [], 0
        for k, shp in zip(BIG, layer_shapes):
            n = math.prod(shp)
            out.append(fb[:, off:off + n].reshape(shard_shapes[k]))
            off += n
        return out

    res = {}
    for tag, fb, fs in (('g', gb, gs), ('d', db, ds), ('m', mb, ms), ('v', vb, vs)):
        res[tag] = dict(zip(BIG, unflat_big(fb)))
        res[tag].update(zip(SMALL, _unflat(fs, small_shapes)))
    return (loss_all, grad_x[None], *[res['g'][k] for k in WEIGHTS], *[res['d'][k] for k in WEIGHTS],
            *[res['m'][k] for k in WEIGHTS], *[res['v'][k] for k in WEIGHTS])
```

```python
import functools
import math

import jax
import jax.numpy as jnp
from jax import lax
from jax.experimental import pallas as pl
from jax.experimental.pallas import tpu as pltpu

F32 = jnp.float32
BF = jnp.bfloat16

D = 1024
DEPTH = 4
N_MEM = 256
MEM_HEADS = 4
MEM_HD = 64
SSM_W = 512
SSM_G = 32
SSM_H = 16
SSM_P = 64
MLA_HEADS = 8
QK_NOPE = 64
QK_ROPE = 32
QK_DIM = 96
V_DIM = 64
Q_LORA = 256
KV_LORA = 128
ROPE_THETA = 10000.0
D_FF = 4096
IN_COLS = 928
EPS = 1e-6
NDEV = 8
LANES = 128
SEGS = 8
S5_LW = 256
S5_NHB = (SSM_G * SSM_P) // S5_LW
ADAM_LR = 0.001
ADAM_B1 = 0.9
ADAM_B2 = 0.999
ADAM_EPS = 1e-08
ADAM_WD = 0.01
ADAM_STEP = 10
VMEM_BIG = 56 * 1024 * 1024

NN = (((1,), (0,)), ((), ()))
NT = (((1,), (1,)), ((), ()))
TN = (((0,), (0,)), ((), ()))

BIG_LATE = ('w_out', 'mem_w_q', 'mem_w_kv', 'mem_w_o', 'mlp_w1', 'mlp_w2')
BIG_EARLY = ('w_in', 'ssm_w_glu', 'mla_w_uq', 'mla_w_ukv')
BIG = BIG_LATE + BIG_EARLY
BIG_AXIS = {'w_in': 1, 'ssm_w_glu': 1, 'mla_w_uq': 2, 'mla_w_ukv': 2, 'w_out': 1, 'mem_w_q': 1, 'mem_w_kv': 1,
            'mem_w_o': 2, 'mlp_w1': 2, 'mlp_w2': 1}
SMALL = ('norm_mix', 'ssm_lambda_re', 'ssm_lambda_im', 'ssm_log_step', 'ssm_b_re', 'ssm_b_im', 'ssm_c_re', 'ssm_c_im',
         'ssm_d', 'ssm_b_glu', 'mla_q_norm', 'mla_kv_norm', 'mla_q_gain', 'mla_k_gain', 'out_norm_ssm', 'out_norm_mla',
         'norm_mem_q', 'norm_mem_kv', 'mem_q_gain', 'mem_k_gain', 'norm_mlp')
WEIGHTS = ('norm_mix', 'w_in', 'ssm_lambda_re', 'ssm_lambda_im', 'ssm_log_step', 'ssm_b_re', 'ssm_b_im', 'ssm_c_re',
           'ssm_c_im', 'ssm_d', 'ssm_w_glu', 'ssm_b_glu', 'mla_q_norm', 'mla_w_uq', 'mla_kv_norm', 'mla_w_ukv',
           'mla_q_gain', 'mla_k_gain', 'out_norm_ssm', 'out_norm_mla', 'w_out', 'norm_mem_q', 'norm_mem_kv', 'mem_w_q',
           'mem_w_kv', 'mem_q_gain', 'mem_k_gain', 'mem_w_o', 'norm_mlp', 'mlp_w1', 'mlp_w2')


def _call(body, *, name, out_shape, grid=(), in_specs=None, out_specs=None, scratch=(), sem=None, vmem=None):
    params = {}
    if sem is not None:
        params['dimension_semantics'] = sem
    if vmem is not None:
        params['vmem_limit_bytes'] = vmem
    specs = {} if in_specs is None else dict(grid=grid, in_specs=in_specs, out_specs=out_specs)
    return pl.pallas_call(body, name=name, out_shape=out_shape, scratch_shapes=list(scratch),
                          compiler_params=pltpu.CompilerParams(**params), **specs)


def _sds(shape, dtype):
    return jax.ShapeDtypeStruct(shape, dtype)


def _dot(a, b, dims=NN):
    return lax.dot_general(a.astype(BF), b.astype(BF), dims, preferred_element_type=F32)


def _split(a):
    hi = a.astype(BF)
    return hi, (a - hi.astype(F32)).astype(BF)


def _dot3(a, b, dims=NN):
    ah, al = _split(a)
    bh, bl = _split(b)
    d = lambda p, q: lax.dot_general(p, q, dims, preferred_element_type=F32)
    return d(ah, bh) + (d(ah, bl) + d(al, bh))


_sdot = _dot


def _rms(x, n):
    r = lax.rsqrt(jnp.sum(x * x, axis=-1, keepdims=True) * (1.0 / n) + EPS)
    return x * r, r


def _rms_bwd(xhat, r, dxhat, n):
    return r * (dxhat - xhat * (jnp.sum(dxhat * xhat, axis=-1, keepdims=True) * (1.0 / n)))


def _colsum(a):
    return jnp.sum(a, axis=0, keepdims=True)


def _tile(t, want):
    return min(t, want)


def _bidx(nb):
    return (lambda b: b) if nb > 1 else (lambda b: 0)


def mm(a, b, mode, *, name, out_dtype=F32, tm=512, tn=512, slots=0):
    squeeze = a.ndim == 2 and b.ndim == 2
    a = a[None] if a.ndim == 2 else a
    b = b[None] if b.ndim == 2 else b
    nb = max(a.shape[0], b.shape[0])
    ab, bb = _bidx(a.shape[0]), _bidx(b.shape[0])
    if mode in ('nn', 'nt'):
        m, k = a.shape[1:]
        n = b.shape[2] if mode == 'nn' else b.shape[1]
        tm, tn = _tile(m, tm), _tile(n, tn)
        dims = NN if mode == 'nn' else NT

        def body(a_ref, b_ref, o_ref):
            o_ref[...] = _dot(a_ref[...], b_ref[...], dims).astype(o_ref.dtype)

        bspec = (pl.BlockSpec((None, k, tn), lambda bi, i, j: (bb(bi), 0, j)) if mode == 'nn'
                 else pl.BlockSpec((None, tn, k), lambda bi, i, j: (bb(bi), j, 0)))
        out = _call(body, name=name, grid=(nb, m // tm, n // tn),
                    in_specs=[pl.BlockSpec((None, tm, k), lambda bi, i, j: (ab(bi), i, 0)), bspec],
                    out_specs=pl.BlockSpec((None, tm, tn), lambda bi, i, j: (bi, i, j)),
                    out_shape=_sds((nb, m, n), out_dtype), sem=('parallel', 'parallel', 'parallel'))(a, b)
    else:
        k, m = a.shape[1:]
        n = b.shape[2]
        tm, tn, tk = _tile(m, 1024), _tile(n, 1024), _tile(k, 512)
        if slots:
            tn = n // slots
            out_spec, out_shape = pl.BlockSpec((None, tm, tn), lambda bi, i, j, kk: (j, i, 0)), _sds((slots, m, tn), F32)
        else:
            out_spec, out_shape = pl.BlockSpec((None, tm, tn), lambda bi, i, j, kk: (bi, i, j)), _sds((nb, m, n), F32)

        def body(a_ref, b_ref, o_ref):
            @pl.when(pl.program_id(3) == 0)
            def _():
                o_ref[...] = jnp.zeros_like(o_ref)

            o_ref[...] += _dot(a_ref[...], b_ref[...], TN)

        out = _call(body, name=name, grid=(nb, m // tm, n // tn, k // tk),
                    in_specs=[pl.BlockSpec((None, tk, tm), lambda bi, i, j, kk: (ab(bi), kk, i)),
                              pl.BlockSpec((None, tk, tn), lambda bi, i, j, kk: (bb(bi), kk, j))],
                    out_specs=out_spec, out_shape=out_shape,
                    sem=('parallel', 'parallel', 'parallel', 'arbitrary'))(a, b)
    return out[0] if squeeze and not slots else out


def rmsnorm_fwd(x, g, *, name, tq=512):
    t, d = x.shape
    tq = _tile(t, tq)

    def body(x_ref, g_ref, o_ref):
        xh, _ = _rms(x_ref[...], d)
        o_ref[...] = (xh * g_ref[...]).astype(o_ref.dtype)

    return _call(body, name=name, grid=(t // tq,),
                 in_specs=[pl.BlockSpec((tq, d), lambda i: (i, 0)), pl.BlockSpec((1, d), lambda i: (0, 0))],
                 out_specs=pl.BlockSpec((tq, d), lambda i: (i, 0)), out_shape=_sds((t, d), BF), sem=('parallel',))(x, g)


def rmsnorm_bwd(x, g, dh, dres, *, name, col=0, tq=512):
    t, d = x.shape
    tq = _tile(t, tq)
    has_res = dres is not None

    def body(*refs):
        if has_res:
            x_ref, g_ref, dh_ref, dres_ref, dx_ref, dxb_ref, dg_ref = refs
        else:
            x_ref, g_ref, dh_ref, dx_ref, dxb_ref, dg_ref = refs
        xh, r = _rms(x_ref[...], d)
        dh_ = dh_ref[...].astype(F32)
        dx = _rms_bwd(xh, r, dh_ * g_ref[...], d)
        if has_res:
            dx = dx + dres_ref[...]
        dx_ref[...] = dx
        dxb_ref[...] = dx.astype(BF)

        @pl.when(pl.program_id(0) == 0)
        def _():
            dg_ref[...] = jnp.zeros_like(dg_ref)

        dg_ref[...] += _colsum(dh_ * xh)

    in_specs = [pl.BlockSpec((tq, d), lambda i: (i, 0)), pl.BlockSpec((1, d), lambda i: (0, 0)),
                pl.BlockSpec((tq, d), lambda i: (i, col))]
    args = [x, g, dh]
    if has_res:
        in_specs.append(pl.BlockSpec((tq, d), lambda i: (i, 0)))
        args.append(dres)
    row = pl.BlockSpec((tq, d), lambda i: (i, 0))
    return _call(body, name=name, grid=(t // tq,), in_specs=in_specs,
                 out_specs=[row, row, pl.BlockSpec((1, d), lambda i: (0, 0))],
                 out_shape=[_sds((t, d), F32), _sds((t, d), BF), _sds((1, d), F32)], sem=('arbitrary',))(*args)


def _cmul(ar, ai, xr, xi):
    return ar * xr - ai * xi, ar * xi + ai * xr


def _seg_carries(er, ei, pr, pi, reverse):
    lw = er.shape[1]
    zero = jnp.zeros((1, lw), F32)
    order = range(SEGS - 1, -1, -1) if reverse else range(SEGS)
    cin_r, cin_i = [None] * SEGS, [None] * SEGS
    tr, ti = zero, zero
    for j in order:
        cin_r[j], cin_i[j] = tr, ti
        mr, mi = _cmul(pr, pi, tr, ti)
        tr, ti = er[j:j + 1, :] + mr, ei[j:j + 1, :] + mi
    return jnp.concatenate(cin_r, axis=0), jnp.concatenate(cin_i, axis=0)


def _s5_chunk(t):
    return _tile(t, 512)


def s5_fwd(u_p, prm, *, name):
    t = u_p.shape[0]
    ch = _s5_chunk(t)
    nch, steps = t // ch, ch // SEGS
    lw = S5_LW

    def body(u_ref, ar_ref, ai_ref, pr_ref, pi_ref, bre_ref, bim_ref, cre_ref, cim_ref, d_ref, y_ref, bur, bui):
        hb = pl.program_id(0)
        ar = jnp.broadcast_to(ar_ref[0], (SEGS, lw))
        ai = jnp.broadcast_to(ai_ref[0], (SEGS, lw))

        def rows_of(c):
            return pl.ds(pl.multiple_of(c * ch, ch), ch)

        @pl.loop(0, nch)
        def _(c):
            u = u_ref[rows_of(c), :]
            bur[rows_of(c), :] = _sdot(u, bre_ref[0])
            bui[rows_of(c), :] = _sdot(u, bim_ref[0])

        def scan(carry, store):
            def step(i, s):
                r0 = pl.multiple_of(i * SEGS, SEGS)
                mr, mi = _cmul(ar, ai, s[0], s[1])
                nr, ni = mr + bur[pl.ds(r0, SEGS), :], mi + bui[pl.ds(r0, SEGS), :]
                if store:
                    bur[pl.ds(r0, SEGS), :] = nr
                    bui[pl.ds(r0, SEGS), :] = ni
                return nr, ni

            return lax.fori_loop(0, t // SEGS, step, carry, unroll=8)

        zero = jnp.zeros((SEGS, lw), F32)
        er, ei = scan((zero, zero), False)
        scan(_seg_carries(er, ei, pr_ref[0], pi_ref[0], False), True)

        @pl.loop(0, nch)
        def _(c):
            rows = rows_of(c)
            y = _sdot(bur[rows, :], cre_ref[0]) - _sdot(bui[rows, :], cim_ref[0])

            @pl.when(hb % 2 == 0)
            def _():
                y_ref[rows, :] = y + d_ref[...] * u_ref[rows, :]

            @pl.when(hb % 2 == 1)
            def _():
                y_ref[rows, :] += y

    vec = pl.BlockSpec((1, 1, lw), lambda h: (h, 0, 0))
    return _call(
        body, name=name, grid=(S5_NHB,),
        in_specs=[pl.BlockSpec((t, LANES), lambda h: (0, h // 2)), vec, vec, vec, vec,
                  pl.BlockSpec((1, LANES, lw), lambda h: (h, 0, 0)), pl.BlockSpec((1, LANES, lw), lambda h: (h, 0, 0)),
                  pl.BlockSpec((1, lw, LANES), lambda h: (h, 0, 0)), pl.BlockSpec((1, lw, LANES), lambda h: (h, 0, 0)),
                  pl.BlockSpec((1, LANES), lambda h: (0, h // 2))],
        out_specs=pl.BlockSpec((t, LANES), lambda h: (0, h // 2)), out_shape=_sds((t, SSM_W), F32),
        scratch=[pltpu.VMEM((t, lw), F32)] * 2, sem=('arbitrary',), vmem=VMEM_BIG,
    )(u_p, prm['ar'], prm['ai'], prm['pr'], prm['pi'], prm['bre'], prm['bim'], prm['cre'], prm['cim'], prm['d'])


def s5_bwd(u_p, dy_p, prm, *, name):
    t = u_p.shape[0]
    ch = _s5_chunk(t)
    nch, steps = t // ch, ch // SEGS
    lw = S5_LW

    def body(u_ref, dy_ref, ar_ref, ai_ref, pr_ref, pi_ref, bre_ref, bim_ref, cre_ref, cim_ref, d_ref,
             du_ref, dar_ref, dai_ref, dbre_ref, dbim_ref, dcre_ref, dcim_ref, dd_ref, bur, bui, sr, si, du_acc):
        hb = pl.program_id(0)
        ar = jnp.broadcast_to(ar_ref[0], (SEGS, lw))
        ai = jnp.broadcast_to(ai_ref[0], (SEGS, lw))
        zero = jnp.zeros((SEGS, lw), F32)

        def rows_of(c):
            return pl.ds(pl.multiple_of(c * ch, ch), ch)

        nsteps = t // SEGS

        @pl.loop(0, nch)
        def _(c):
            u = u_ref[rows_of(c), :]
            bur[rows_of(c), :] = _sdot(u, bre_ref[0])
            bui[rows_of(c), :] = _sdot(u, bim_ref[0])

        def fwd_scan(carry, store):
            def step(i, s):
                r0 = pl.multiple_of(i * SEGS, SEGS)
                mr, mi = _cmul(ar, ai, s[0], s[1])
                nr, ni = mr + bur[pl.ds(r0, SEGS), :], mi + bui[pl.ds(r0, SEGS), :]
                if store:
                    w0 = pl.multiple_of(i * SEGS + SEGS, SEGS)
                    sr[pl.ds(w0, SEGS), :] = nr
                    si[pl.ds(w0, SEGS), :] = ni
                return nr, ni

            return lax.fori_loop(0, nsteps, step, carry, unroll=8)

        er, ei = fwd_scan((zero, zero), False)
        cin_r, cin_i = _seg_carries(er, ei, pr_ref[0], pi_ref[0], False)
        sr[pl.ds(0, SEGS), :] = cin_r
        si[pl.ds(0, SEGS), :] = cin_i
        fwd_scan((cin_r, cin_i), True)

        @pl.loop(0, nch)
        def _(c):
            dy = dy_ref[rows_of(c), :]
            bur[rows_of(c), :] = _sdot(dy, cre_ref[0], NT)
            bui[rows_of(c), :] = -_sdot(dy, cim_ref[0], NT)

        def rev_local(ii, lam):
            r0 = pl.multiple_of((nsteps - 1 - ii) * SEGS, SEGS)
            mr, mi = _cmul(ar, -ai, lam[0], lam[1])
            return mr + bur[pl.ds(r0, SEGS), :], mi + bui[pl.ds(r0, SEGS), :]

        lr0, li0 = lax.fori_loop(0, nsteps, rev_local, (zero, zero), unroll=8)
        rin = _seg_carries(lr0, li0, pr_ref[0], -pi_ref[0], True)

        def rev_step(ii, st):
            lam_r, lam_i, acc_r, acc_i = st
            r0 = pl.multiple_of((nsteps - 1 - ii) * SEGS, SEGS)
            mr, mi = _cmul(ar, -ai, lam_r, lam_i)
            nr, ni = mr + bur[pl.ds(r0, SEGS), :], mi + bui[pl.ds(r0, SEGS), :]
            bur[pl.ds(r0, SEGS), :] = nr
            bui[pl.ds(r0, SEGS), :] = ni
            pr_, pi_ = sr[pl.ds(r0, SEGS), :], si[pl.ds(r0, SEGS), :]
            return nr, ni, acc_r + (nr * pr_ + ni * pi_), acc_i + (ni * pr_ - nr * pi_)

        _, _, acc_r, acc_i = lax.fori_loop(0, nsteps, rev_step, (rin[0], rin[1], zero, zero), unroll=8)
        dar_ref[0] = _colsum(acc_r)
        dai_ref[0] = _colsum(acc_i)

        dbre_ref[...] = jnp.zeros_like(dbre_ref)
        dbim_ref[...] = jnp.zeros_like(dbim_ref)
        dcre_ref[...] = jnp.zeros_like(dcre_ref)
        dcim_ref[...] = jnp.zeros_like(dcim_ref)

        @pl.loop(0, nch)
        def _(c):
            rows = rows_of(c)
            u = u_ref[rows, :]
            dy = dy_ref[rows, :]
            lam_r, lam_i = bur[rows, :], bui[rows, :]
            du = _sdot(lam_r, bre_ref[0], NT) + _sdot(lam_i, bim_ref[0], NT)

            @pl.when(hb % 2 == 0)
            def _():
                du_acc[rows, :] = du + d_ref[...] * dy

            @pl.when(hb % 2 == 1)
            def _():
                du_ref[rows, :] = (du_acc[rows, :] + du).astype(BF)

            dbre_ref[0] += _sdot(u, lam_r, TN)
            dbim_ref[0] += _sdot(u, lam_i, TN)
            srows = pl.ds(pl.multiple_of(c * ch + SEGS, SEGS), ch)
            dcre_ref[0] += _sdot(sr[srows, :], dy, TN)
            dcim_ref[0] -= _sdot(si[srows, :], dy, TN)

        @pl.when(hb % 2 == 0)
        def _():
            dd_ref[...] = _colsum(dy_ref[...] * u_ref[...])

    vec = pl.BlockSpec((1, 1, lw), lambda h: (h, 0, 0))
    bsp = pl.BlockSpec((1, LANES, lw), lambda h: (h, 0, 0))
    csp = pl.BlockSpec((1, lw, LANES), lambda h: (h, 0, 0))
    act = pl.BlockSpec((t, LANES), lambda h: (0, h // 2))
    dsp = pl.BlockSpec((1, LANES), lambda h: (0, h // 2))
    return _call(
        body, name=name, grid=(S5_NHB,),
        in_specs=[act, act, vec, vec, vec, vec, bsp, bsp, csp, csp, dsp],
        out_specs=[act, vec, vec, bsp, bsp, csp, csp, dsp],
        out_shape=[_sds((t, SSM_W), BF), _sds((S5_NHB, 1, lw), F32), _sds((S5_NHB, 1, lw), F32),
                   _sds((S5_NHB, LANES, lw), F32), _sds((S5_NHB, LANES, lw), F32),
                   _sds((S5_NHB, lw, LANES), F32), _sds((S5_NHB, lw, LANES), F32), _sds((1, SSM_W), F32)],
        scratch=[pltpu.VMEM((t, lw), F32), pltpu.VMEM((t, lw), F32),
                 pltpu.VMEM((t + SEGS, lw), F32), pltpu.VMEM((t + SEGS, lw), F32), pltpu.VMEM((t, LANES), F32)],
        sem=('arbitrary',), vmem=VMEM_BIG,
    )(u_p, dy_p, prm['ar'], prm['ai'], prm['pr'], prm['pi'], prm['bre'], prm['bim'], prm['cre'], prm['cim'], prm['d'])


def s5_prep(t, lam_re, lam_im, log_step, b_re, b_im, c_re, c_im):
    step = jnp.exp(log_step)[:, None]
    mag = jnp.exp(lam_re * step)
    ar, ai = mag * jnp.cos(lam_im * step), mag * jnp.sin(lam_im * step)
    den = lam_re * lam_re + lam_im * lam_im
    nr, ni = ar - 1.0, ai
    fr, fi = (nr * lam_re + ni * lam_im) / den, (ni * lam_re - nr * lam_im) / den
    bbr = fr[..., None] * b_re - fi[..., None] * b_im
    bbi = fr[..., None] * b_im + fi[..., None] * b_re
    gl = S5_LW // SSM_P
    eye = jnp.eye(gl, dtype=F32)
    half = (jnp.arange(S5_NHB) % 2)[:, None, None]

    def bmat(bb):
        x = bb.transpose(0, 2, 1).reshape(S5_NHB, gl, SSM_H, SSM_P)
        x = jnp.einsum('bghp,gk->bghkp', x, eye).reshape(S5_NHB, gl * SSM_H, S5_LW)
        z = jnp.zeros_like(x)
        return jnp.where(half == 0, jnp.concatenate([x, z], axis=1), jnp.concatenate([z, x], axis=1))

    def cmat(cc):
        x = cc.transpose(0, 2, 1).reshape(S5_NHB, gl, SSM_P, SSM_H)
        x = jnp.einsum('bgph,gk->bgpkh', x, eye).reshape(S5_NHB, S5_LW, gl * SSM_H)
        z = jnp.zeros_like(x)
        return jnp.where(half == 0, jnp.concatenate([x, z], axis=2), jnp.concatenate([z, x], axis=2))

    vec = lambda a: a.reshape(S5_NHB, 1, S5_LW)
    ni_steps = float(t // SEGS)
    pmag = jnp.exp(lam_re * step * ni_steps)
    pr, pi = pmag * jnp.cos(lam_im * step * ni_steps), pmag * jnp.sin(lam_im * step * ni_steps)
    return dict(ar=vec(ar), ai=vec(ai), bre=bmat(bbr), bim=bmat(bbi), cre=cmat(c_re), cim=cmat(c_im),
                pr=lax.stop_gradient(vec(pr)), pi=lax.stop_gradient(vec(pi)))


def _gelu(x):
    c = math.sqrt(2.0 / math.pi)
    return 0.5 * x * (1.0 + jnp.tanh(c * (x + 0.044715 * (x * x * x))))


def _gelu_grad(x):
    c = math.sqrt(2.0 / math.pi)
    th = jnp.tanh(c * (x + 0.044715 * (x * x * x)))
    return 0.5 * (1.0 + th) + 0.5 * x * (1.0 - th * th) * (c * (1.0 + 3.0 * 0.044715 * (x * x)))


def glu_fwd(ypre, w_glu, b_glu, *, name, tq=512):
    t = ypre.shape[0]
    tq = _tile(t, tq)

    def body(y_ref, w_ref, b_ref, o_ref):
        yg = _gelu(y_ref[...])
        z = _dot(yg, w_ref[...]) + b_ref[...]
        o_ref[...] = yg * jax.nn.sigmoid(z)

    return _call(body, name=name, grid=(t // tq,),
                 in_specs=[pl.BlockSpec((tq, SSM_W), lambda i: (i, 0)), pl.BlockSpec((SSM_W, SSM_W), lambda i: (0, 0)),
                           pl.BlockSpec((1, SSM_W), lambda i: (0, 0))],
                 out_specs=pl.BlockSpec((tq, SSM_W), lambda i: (i, 0)), out_shape=_sds((t, SSM_W), F32),
                 sem=('parallel',))(ypre, w_glu, b_glu)


def glu_bwd(ypre, dy, w_glu, b_glu, *, name, tq=512):
    t = ypre.shape[0]
    tq = _tile(t, tq)

    def body(y_ref, dy_ref, w_ref, b_ref, dyp_ref, yg_ref, dz_ref, db_ref):
        ypre_ = y_ref[...]
        yg = _gelu(ypre_)
        sig = jax.nn.sigmoid(_dot(yg, w_ref[...]) + b_ref[...])
        dy_ = dy_ref[...]
        dz = dy_ * yg * sig * (1.0 - sig)
        dyg = dy_ * sig + _dot(dz, w_ref[...], NT)
        dyp_ref[...] = dyg * _gelu_grad(ypre_)
        yg_ref[...] = yg.astype(BF)
        dz_ref[...] = dz.astype(BF)

        @pl.when(pl.program_id(0) == 0)
        def _():
            db_ref[...] = jnp.zeros_like(db_ref)

        db_ref[...] += _colsum(dz)

    row = pl.BlockSpec((tq, SSM_W), lambda i: (i, 0))
    vec = pl.BlockSpec((1, SSM_W), lambda i: (0, 0))
    return _call(body, name=name, grid=(t // tq,),
                 in_specs=[row, row, pl.BlockSpec((SSM_W, SSM_W), lambda i: (0, 0)), vec],
                 out_specs=[row, row, row, vec],
                 out_shape=[_sds((t, SSM_W), F32), _sds((t, SSM_W), BF), _sds((t, SSM_W), BF), _sds((1, SSM_W), F32)],
                 sem=('arbitrary',))(ypre, dy, w_glu, b_glu)


def _rope(x, cos, sa, sb):
    return x * cos + pltpu.roll(x, 16, 1) * sa + pltpu.roll(x, 112, 1) * sb


def _rope_t(d, cos, sa, sb):
    return d * cos + pltpu.roll(d * sa, 112, 1) + pltpu.roll(d * sb, 16, 1)


def rope_tables(positions):
    half = QK_ROPE // 2
    inv_freq = ROPE_THETA ** (-jnp.arange(half, dtype=F32) / half)
    ang = positions.astype(F32)[:, None] * inv_freq
    cos, sin = jnp.cos(ang), jnp.sin(ang)
    t = positions.shape[0]
    one, zero = jnp.ones((t, QK_NOPE), F32), jnp.zeros((t, QK_NOPE), F32)
    pad1, pad0 = jnp.ones((t, 32), F32), jnp.zeros((t, 32), F32)
    z16 = jnp.zeros((t, half), F32)
    return (jnp.concatenate([one, cos, cos, pad1], axis=1), jnp.concatenate([zero, z16, sin, pad0], axis=1),
            jnp.concatenate([zero, -sin, z16, pad0], axis=1))


def mla_prep_fwd(proj, tabs, w, *, name):
    t = proj.shape[0]
    tq = _tile(t, ATT_BLK)

    def body(cq_ref, ckv_ref, kr_ref, cos_ref, sa_ref, sb_ref, qn_ref, kvn_ref, wq_ref, wk_ref, wv_ref, qg_ref, kg_ref,
             q_ref, qt_ref, k_ref, kt_ref, v_ref):
        cqn = (_rms(cq_ref[...], Q_LORA)[0] * qn_ref[...]).astype(BF)
        ckvn = (_rms(ckv_ref[...], KV_LORA)[0] * kvn_ref[...]).astype(BF)
        cos, sa, sb = cos_ref[...], sa_ref[...], sb_ref[...]
        kr = kr_ref[...]
        for h in range(MLA_HEADS):
            q = _rms(_dot(cqn, wq_ref[h]), QK_DIM)[0] * qg_ref[...]
            q = _rope(q, cos, sa, sb) * ATT_SCALE
            q_ref[h] = q.astype(BF)
            qt_ref[h, 0] = q.T.astype(BF)
            k = _rms(_dot(ckvn, wk_ref[h]) + kr, QK_DIM)[0] * kg_ref[...]
            k = _rope(k, cos, sa, sb)
            k_ref[h] = k.astype(BF)
            kt_ref[h, 0] = k.T.astype(BF)
            v_ref[h] = _dot(ckvn, wv_ref[h]).astype(BF)

    tab = pl.BlockSpec((tq, LANES), lambda i: (i, 0))
    full = lambda shape: pl.BlockSpec(shape, lambda i: (0,) * len(shape))
    hout = pl.BlockSpec((MLA_HEADS, tq, LANES), lambda i: (0, i, 0))
    tout = pl.BlockSpec((MLA_HEADS, 1, LANES, tq), lambda i: (0, i, 0, 0))
    hshape = _sds((MLA_HEADS, t, LANES), BF)
    tshape = _sds((MLA_HEADS, t // tq, LANES, tq), BF)
    return _call(
        body, name=name, grid=(t // tq,),
        in_specs=[pl.BlockSpec((tq, Q_LORA), lambda i: (i, 2)), pl.BlockSpec((tq, LANES), lambda i: (i, 6)),
                  pl.BlockSpec((tq, LANES), lambda i: (i, 7)), tab, tab, tab,
                  full((1, Q_LORA)), full((1, KV_LORA)), full((MLA_HEADS, Q_LORA, LANES)),
                  full((MLA_HEADS, KV_LORA, LANES)), full((MLA_HEADS, KV_LORA, LANES)), full((1, LANES)), full((1, LANES))],
        out_specs=[hout, tout, hout, tout, hout], out_shape=[hshape, tshape, hshape, tshape, hshape], sem=('parallel',),
    )(proj, proj, proj, *tabs, w['q_norm'], w['kv_norm'], w['wq'], w['wk'], w['wv'], w['q_gain'], w['k_gain'])


def mla_prep_bwd(proj, tabs, w, dq, dk, dv, *, name):
    t = proj.shape[0]
    tq = _tile(t, ATT_BLK)

    def body(cq_ref, ckv_ref, kr_ref, cos_ref, sa_ref, sb_ref, qn_ref, kvn_ref, wq_ref, wk_ref, wv_ref, qg_ref, kg_ref,
             dq_ref, dk_ref, dv_ref,
             dpm_ref, cqn_ref, ckvn_ref, dqr_ref, dkraw_ref, dvb_ref, dqn_ref, dkvn_ref, dqg_ref, dkg_ref):
        cq_h, cq_r = _rms(cq_ref[...], Q_LORA)
        ckv_h, ckv_r = _rms(ckv_ref[...], KV_LORA)
        cqn = (cq_h * qn_ref[...]).astype(BF)
        ckvn = (ckv_h * kvn_ref[...]).astype(BF)
        cqn_ref[...] = cqn
        ckvn_ref[...] = ckvn
        cos, sa, sb = cos_ref[...], sa_ref[...], sb_ref[...]
        kr = kr_ref[...]
        dcqn = jnp.zeros((tq, Q_LORA), F32)
        dckvn = jnp.zeros((tq, KV_LORA), F32)
        dkrope = jnp.zeros((tq, LANES), F32)
        dqg = jnp.zeros((1, LANES), F32)
        dkg = jnp.zeros((1, LANES), F32)
        for h in range(MLA_HEADS):
            qh, qr = _rms(_dot(cqn, wq_ref[h]), QK_DIM)
            dqo = _rope_t(dq_ref[h, 0].T * ATT_SCALE, cos, sa, sb)
            dqg = dqg + _colsum(dqo * qh)
            dqraw = _rms_bwd(qh, qr, dqo * qg_ref[...], QK_DIM).astype(BF)
            dqr_ref[:, h * LANES:(h + 1) * LANES] = dqraw
            dcqn = dcqn + _dot(dqraw, wq_ref[h], NT)
            kh, krs = _rms(_dot(ckvn, wk_ref[h]) + kr, QK_DIM)
            dko = _rope_t(dk_ref[h], cos, sa, sb)
            dkg = dkg + _colsum(dko * kh)
            dkraw = _rms_bwd(kh, krs, dko * kg_ref[...], QK_DIM)
            dkrope = dkrope + dkraw
            dkraw = dkraw.astype(BF)
            dkraw_ref[:, h * LANES:(h + 1) * LANES] = dkraw
            dvb = dv_ref[h].astype(BF)
            dvb_ref[:, h * LANES:(h + 1) * LANES] = dvb
            dckvn = dckvn + _dot(dkraw, wk_ref[h], NT) + _dot(dvb, wv_ref[h], NT)
        dpm_ref[:, 0:Q_LORA] = _rms_bwd(cq_h, cq_r, dcqn * qn_ref[...], Q_LORA).astype(BF)
        dpm_ref[:, Q_LORA:Q_LORA + KV_LORA] = _rms_bwd(ckv_h, ckv_r, dckvn * kvn_ref[...], KV_LORA).astype(BF)
        dpm_ref[:, Q_LORA + KV_LORA:512] = dkrope.astype(BF)

        @pl.when(pl.program_id(0) == 0)
        def _():
            dqn_ref[...] = jnp.zeros_like(dqn_ref)
            dkvn_ref[...] = jnp.zeros_like(dkvn_ref)
            dqg_ref[...] = jnp.zeros_like(dqg_ref)
            dkg_ref[...] = jnp.zeros_like(dkg_ref)

        dqn_ref[...] += _colsum(dcqn * cq_h)
        dkvn_ref[...] += _colsum(dckvn * ckv_h)
        dqg_ref[...] += dqg
        dkg_ref[...] += dkg

    tab = pl.BlockSpec((tq, LANES), lambda i: (i, 0))
    full = lambda shape: pl.BlockSpec(shape, lambda i: (0,) * len(shape))
    hblk = pl.BlockSpec((MLA_HEADS, tq, LANES), lambda i: (0, i, 0))
    wide = pl.BlockSpec((tq, MLA_HEADS * LANES), lambda i: (i, 0))
    return _call(
        body, name=name, grid=(t // tq,),
        in_specs=[pl.BlockSpec((tq, Q_LORA), lambda i: (i, 2)), pl.BlockSpec((tq, LANES), lambda i: (i, 6)),
                  pl.BlockSpec((tq, LANES), lambda i: (i, 7)), tab, tab, tab,
                  full((1, Q_LORA)), full((1, KV_LORA)), full((MLA_HEADS, Q_LORA, LANES)),
                  full((MLA_HEADS, KV_LORA, LANES)), full((MLA_HEADS, KV_LORA, LANES)), full((1, LANES)), full((1, LANES)),
                  pl.BlockSpec((MLA_HEADS, 1, LANES, tq), lambda i: (0, i, 0, 0)), hblk, hblk],
        out_specs=[pl.BlockSpec((tq, 512), lambda i: (i, 0)),
                   pl.BlockSpec((tq, Q_LORA), lambda i: (i, 0)), pl.BlockSpec((tq, KV_LORA), lambda i: (i, 0)),
                   wide, wide, wide, full((1, Q_LORA)), full((1, KV_LORA)), full((1, LANES)), full((1, LANES))],
        out_shape=[_sds((t, 512), BF), _sds((t, Q_LORA), BF), _sds((t, KV_LORA), BF),
                   _sds((t, MLA_HEADS * LANES), BF), _sds((t, MLA_HEADS * LANES), BF), _sds((t, MLA_HEADS * LANES), BF),
                   _sds((1, Q_LORA), F32), _sds((1, KV_LORA), F32), _sds((1, LANES), F32), _sds((1, LANES), F32)],
        sem=('arbitrary',),
    )(proj, proj, proj, *tabs, w['q_norm'], w['kv_norm'], w['wq'], w['wk'], w['wv'], w['q_gain'], w['k_gain'], dq, dk, dv)


ATT_BLK = 256
ATT_SCALE = 1.0 / math.sqrt(QK_DIM)


def _overlapped(grid, make_copies):
    ids = [pl.program_id(a) for a in range(len(grid))]
    first = functools.reduce(jnp.logical_and, [i == 0 for i in ids])
    last = functools.reduce(jnp.logical_and, [i == n - 1 for i, n in zip(ids, grid)])

    @pl.when(first)
    def _():
        for cs in make_copies():
            _start_copies(cs)

    @pl.when(last)
    def _():
        for cs in make_copies():
            _wait_copies(cs)


def flash_fwd(q, kt, v, *, name, gather=()):
    t = q.shape[1]
    blk = _tile(t, ATT_BLK)
    grid = (MLA_HEADS // 2, t // blk)

    def body(q_ref, kt_ref, v_ref, *rest):
        nc = len(gather)
        srcs, (o_ref, lse_ref), dsts, sems = rest[:nc], rest[nc:nc + 2], rest[nc + 2:2 * nc + 2], rest[2 * nc + 2:]
        if nc:
            _overlapped(grid, lambda: [_copies('gather', srcs[i], dsts[i], *sems[3 * i:3 * i + 3]) for i in range(nc)])
        qi = pl.program_id(1)
        row = lax.broadcasted_iota(jnp.int32, (blk, blk), 0)
        col = lax.broadcasted_iota(jnp.int32, (blk, blk), 1)

        def block(j, carry, masked):
            out = []
            for hh in range(2):
                m, l, acc = carry[hh]
                s = _dot(q_ref[hh], kt_ref[hh, j])
                if masked:
                    s = jnp.where(col <= row, s, -jnp.inf)
                m2 = jnp.maximum(m, jnp.max(s, axis=-1, keepdims=True))
                p = jnp.exp(s - m2)
                alpha = jnp.exp(m - m2)
                rows = pl.ds(pl.multiple_of(j * blk, blk), blk)
                out.append((m2, alpha * l + jnp.sum(p, axis=-1, keepdims=True), alpha * acc + _dot(p, v_ref[hh, rows, :])))
            return tuple(out)

        init = (jnp.full((blk, 1), -jnp.inf, F32), jnp.zeros((blk, 1), F32), jnp.zeros((blk, LANES), F32))
        carry = lax.fori_loop(0, qi, lambda j, c: block(j, c, False), (init, init))
        carry = block(qi, carry, True)
        o_acc = jnp.zeros((blk, LANES), F32)
        for hh in range(2):
            m, l, acc = carry[hh]
            o_acc = o_acc + acc / l
            lse_ref[hh, 0] = jnp.broadcast_to(m + jnp.log(l), (blk, LANES)).T[0:1, :]
        o_ref[...] = o_acc

    in_specs = [pl.BlockSpec((2, blk, LANES), lambda p, i: (p, i, 0)),
                pl.BlockSpec((2, t // blk, LANES, blk), lambda p, i: (p, 0, 0, 0)),
                pl.BlockSpec((2, t, LANES), lambda p, i: (p, 0, 0))]
    out_specs = [pl.BlockSpec((blk, LANES), lambda p, i: (i, p)), pl.BlockSpec((2, 1, 1, blk), lambda p, i: (p, i, 0, 0))]
    out_shape = [_sds((t, 512), F32), _sds((MLA_HEADS, t // blk, 1, blk), F32)]
    nc = len(gather)
    return _call(body, name=name, grid=grid, in_specs=in_specs + [_ANY] * nc, out_specs=out_specs + [_ANY] * nc,
                 out_shape=out_shape + [_sds((NDEV,) + g.shape, g.dtype) for g in gather], scratch=_COMM_SCRATCH * nc,
                 sem=('arbitrary', 'arbitrary') if nc else ('parallel', 'parallel'))(q, kt, v, *gather)


def mla_out_bwd(o, dyn, g, *, name):
    t = o.shape[0]
    blk = _tile(t, ATT_BLK)

    def body(o_ref, dh_ref, g_ref, do_ref, dot_ref, delta_ref, dg_ref):
        ov = o_ref[...]
        oh, r = _rms(ov, 512)
        dh = dh_ref[...]
        do = _rms_bwd(oh, r, dh * g_ref[...], 512)
        do_ref[...] = do.astype(BF)
        dd = do * ov
        for pb in range(MLA_HEADS // 2):
            cols = slice(pb * LANES, (pb + 1) * LANES)
            dot_ref[pb, 0] = do[:, cols].T.astype(BF)
            ddt = dd[:, cols].T
            delta_ref[2 * pb, 0] = jnp.sum(ddt[0:V_DIM, :], axis=0, keepdims=True)
            delta_ref[2 * pb + 1, 0] = jnp.sum(ddt[V_DIM:LANES, :], axis=0, keepdims=True)

        @pl.when(pl.program_id(0) == 0)
        def _():
            dg_ref[...] = jnp.zeros_like(dg_ref)

        dg_ref[...] += _colsum(dh * oh)

    return _call(
        body, name=name, grid=(t // blk,),
        in_specs=[pl.BlockSpec((blk, 512), lambda i: (i, 0)), pl.BlockSpec((blk, 512), lambda i: (i, 1)),
                  pl.BlockSpec((1, 512), lambda i: (0, 0))],
        out_specs=[pl.BlockSpec((blk, 512), lambda i: (i, 0)), pl.BlockSpec((MLA_HEADS // 2, 1, LANES, blk), lambda i: (0, i, 0, 0)),
                   pl.BlockSpec((MLA_HEADS, 1, 1, blk), lambda i: (0, i, 0, 0)), pl.BlockSpec((1, 512), lambda i: (0, 0))],
        out_shape=[_sds((t, 512), BF), _sds((MLA_HEADS // 2, t // blk, LANES, blk), BF),
                   _sds((MLA_HEADS, t // blk, 1, blk), F32), _sds((1, 512), F32)],
        sem=('arbitrary',),
    )(o, dyn, g)


def flash_bwd(q, qt, k, kt, v, do, dot, lse, delta, *, name, scatter=()):
    t = q.shape[1]
    blk = _tile(t, ATT_BLK)
    nb = t // blk
    grid = (MLA_HEADS, nb)

    def body(q_ref, qt_ref, k_ref, kt_ref, v_ref, do_ref, dot_ref, lse_ref, delta_ref, *rest):
        nc = len(scatter)
        srcs, (dqt_ref, dk_ref, dv_ref), dsts, sems = rest[:nc], rest[nc:nc + 3], rest[nc + 3:2 * nc + 3], rest[2 * nc + 3:]
        if nc:
            _overlapped(grid, lambda: [_copies('scatter', srcs[i], dsts[i], *sems[3 * i:3 * i + 3]) for i in range(nc)])
        h, j = pl.program_id(0), pl.program_id(1)
        row = lax.broadcasted_iota(jnp.int32, (blk, blk), 0)
        col = lax.broadcasted_iota(jnp.int32, (blk, blk), 1)
        lane = lax.broadcasted_iota(jnp.int32, (1, LANES), 1)
        mine = (lane // V_DIM) == (h % 2)

        @pl.when(j == 0)
        def _():
            dqt_ref[...] = jnp.zeros_like(dqt_ref)

        kv, ktv, vv = k_ref[...], kt_ref[...], v_ref[...]

        def block(i, carry, masked):
            dk, dv = carry
            rows = pl.ds(pl.multiple_of(i * blk, blk), blk)
            pt = jnp.exp(_dot(kv, qt_ref[i]) - lse_ref[i])
            if masked:
                pt = jnp.where(col >= row, pt, 0.0)
            dv = dv + _dot(pt, do_ref[rows, :])
            dst = (pt * (_dot(vv, dot_ref[i]) - delta_ref[i])).astype(BF)
            dk = dk + _dot(dst, q_ref[rows, :])
            dqt_ref[i] += _dot(ktv, dst)
            return dk, dv

        zero = jnp.zeros((blk, LANES), F32)
        carry = block(j, (zero, zero), True)
        npairs = (nb - 1 - j) // 2
        carry = lax.fori_loop(0, npairs, lambda p, c: block(j + 2 + 2 * p, block(j + 1 + 2 * p, c, False), False), carry)
        dk, dv = lax.fori_loop(j + 1 + 2 * npairs, nb, lambda i, c: block(i, c, False), carry)
        dk_ref[...] = dk
        dv_ref[...] = jnp.where(mine, dv, 0.0)

    whole = pl.BlockSpec((None, t, LANES), lambda h, j: (h, 0, 0))
    wholet = pl.BlockSpec((None, nb, LANES, blk), lambda h, j: (h, 0, 0, 0))
    kvb = pl.BlockSpec((None, blk, LANES), lambda h, j: (h, j, 0))
    rowv = pl.BlockSpec((None, nb, 1, blk), lambda h, j: (h, 0, 0, 0))
    in_specs = [whole, wholet, kvb, pl.BlockSpec((None, None, LANES, blk), lambda h, j: (h, j, 0, 0)), kvb,
                pl.BlockSpec((t, LANES), lambda h, j: (0, h // 2)),
                pl.BlockSpec((None, nb, LANES, blk), lambda h, j: (h // 2, 0, 0, 0)), rowv, rowv]
    out_specs = [wholet, kvb, kvb]
    out_shape = [_sds((MLA_HEADS, nb, LANES, blk), F32), _sds((MLA_HEADS, t, LANES), F32), _sds((MLA_HEADS, t, LANES), F32)]
    args = (q, qt, k, kt, v, do, dot, lse, delta)
    nc = len(scatter)
    return _call(body, name=name, grid=grid, in_specs=in_specs + [_ANY] * nc, out_specs=out_specs + [_ANY] * nc,
                 out_shape=out_shape + [_sds(s.shape, s.dtype) for s in scatter], scratch=_COMM_SCRATCH * nc,
                 sem=('arbitrary', 'arbitrary') if nc else ('parallel', 'arbitrary'), vmem=VMEM_BIG)(*args, *scatter)


def mix_out_fwd(x, y_ssm, o, g_ssm, g_mla, w_out, *, name, tq=512):
    t = x.shape[0]
    tq = _tile(t, tq)

    def body(x_ref, ys_ref, o_ref, gs_ref, gm_ref, w_ref, x1_ref, yn_ref):
        ns = (_rms(ys_ref[...], SSM_W)[0] * gs_ref[...]).astype(BF)
        nm = (_rms(o_ref[...], 512)[0] * gm_ref[...]).astype(BF)
        yn_ref[:, 0:SSM_W] = ns
        yn_ref[:, SSM_W:D] = nm
        x1_ref[...] = x_ref[...] + _dot(ns, w_ref[0:SSM_W, :]) + _dot(nm, w_ref[SSM_W:D, :])

    row = lambda w: pl.BlockSpec((tq, w), lambda i: (i, 0))
    vec = pl.BlockSpec((1, 512), lambda i: (0, 0))
    return _call(body, name=name, grid=(t // tq,),
                 in_specs=[row(D), row(512), row(512), vec, vec, pl.BlockSpec((D, D), lambda i: (0, 0))],
                 out_specs=[row(D), row(D)], out_shape=[_sds((t, D), F32), _sds((t, D), BF)], sem=('parallel',),
                 )(x, y_ssm, o, g_ssm, g_mla, w_out)


MEM_SCALE = 1.0 / math.sqrt(MEM_HD)


def memkv_fwd(mem, g, wk, wv, kg, *, name):
    def body(m_ref, g_ref, wk_ref, wv_ref, kg_ref, mh_ref, k_ref, v_ref):
        mh = (_rms(m_ref[...], D)[0] * g_ref[...]).astype(BF)
        mh_ref[...] = mh
        for h in range(MEM_HEADS):
            cols = slice(h * LANES, (h + 1) * LANES)
            k_ref[h] = (_rms(_dot(mh, wk_ref[:, cols]), MEM_HD)[0] * kg_ref[...]).astype(BF)
            v_ref[h] = _dot(mh, wv_ref[:, cols]).astype(BF)

    return _call(body, name=name,
                 out_shape=[_sds((N_MEM, D), BF), _sds((MEM_HEADS, N_MEM, LANES), BF), _sds((MEM_HEADS, N_MEM, LANES), BF)],
                 )(mem, g, wk, wv, kg)


def memkv_bwd(mem, g, wk, wv, kg, dk, dv, *, name):
    def body(m_ref, g_ref, wk_ref, wv_ref, kg_ref, dk_ref, dv_ref, dwk_ref, dwv_ref, dkg_ref, dg_ref):
        mhat, _ = _rms(m_ref[...], D)
        mh = (mhat * g_ref[...]).astype(BF)
        lane = lax.broadcasted_iota(jnp.int32, (1, LANES), 1)
        dkg = jnp.zeros((1, LANES), F32)
        dmh = jnp.zeros((N_MEM, D), F32)
        for h in range(MEM_HEADS):
            cols = slice(h * LANES, (h + 1) * LANES)
            kh, kr = _rms(_dot(mh, wk_ref[:, cols]), MEM_HD)
            dko = dk_ref[h]
            dkg = dkg + _colsum(dko * kh)
            dkraw = _rms_bwd(kh, kr, dko * kg_ref[...], MEM_HD).astype(BF)
            dvh = jnp.where((lane // MEM_HD) == (h % 2), dv_ref[h], 0.0).astype(BF)
            dwk_ref[:, cols] = _dot(mh, dkraw, TN)
            dwv_ref[:, cols] = _dot(mh, dvh, TN)
            dmh = dmh + _dot(dkraw, wk_ref[:, cols], NT) + _dot(dvh, wv_ref[:, cols], NT)
        dkg_ref[...] = dkg
        dg_ref[...] = _colsum(dmh * mhat)

    return _call(body, name=name,
                 out_shape=[_sds((D, 512), F32), _sds((D, 512), F32), _sds((1, LANES), F32), _sds((1, D), F32)],
                 )(mem, g, wk, wv, kg, dk, dv)


def memattn_fwd(x, g, wq, qg, kh, vh, wo, *, name, tq=256):
    t = x.shape[0]
    tq = _tile(t, tq)

    def body(x_ref, g_ref, wq_ref, qg_ref, k_ref, v_ref, wo_ref, x2_ref, hn_ref):
        xv = x_ref[...]
        hn = (_rms(xv, D)[0] * g_ref[...]).astype(BF)
        hn_ref[...] = hn
        out = xv
        for pb in range(MEM_HEADS // 2):
            o = jnp.zeros((tq, LANES), F32)
            for h in (2 * pb, 2 * pb + 1):
                q = _rms(_dot(hn, wq_ref[:, h * LANES:(h + 1) * LANES]), MEM_HD)[0] * qg_ref[...]
                s = _dot(q, k_ref[h], NT) * MEM_SCALE
                p = jnp.exp(s - jnp.max(s, axis=-1, keepdims=True))
                p = p / jnp.sum(p, axis=-1, keepdims=True)
                o = o + _dot(p, v_ref[h])
            out = out + _dot(o, wo_ref[pb * LANES:(pb + 1) * LANES, :])
        x2_ref[...] = out

    full = lambda shape: pl.BlockSpec(shape, lambda i: (0,) * len(shape))
    row = pl.BlockSpec((tq, D), lambda i: (i, 0))
    return _call(body, name=name, grid=(t // tq,),
                 in_specs=[row, full((1, D)), full((D, 512)), full((1, LANES)), full((MEM_HEADS, N_MEM, LANES)),
                           full((MEM_HEADS, N_MEM, LANES)), full((MEM_HEADS * MEM_HD, D))],
                 out_specs=[row, row], out_shape=[_sds((t, D), F32), _sds((t, D), BF)], sem=('parallel',),
                 )(x, g, wq, qg, kh, vh, wo)


def memattn_bwd(x, dx2, g, wq, qg, kh, vh, wo, *, name, tq=256):
    t = x.shape[0]
    tq = _tile(t, tq)

    def body(x_ref, dx2_ref, g_ref, wq_ref, qg_ref, k_ref, v_ref, wo_ref,
             dx_ref, dxb_ref, o_ref, dqr_ref, dk_ref, dv_ref, dqg_ref, dg_ref):
        @pl.when(pl.program_id(0) == 0)
        def _():
            dk_ref[...] = jnp.zeros_like(dk_ref)
            dv_ref[...] = jnp.zeros_like(dv_ref)
            dqg_ref[...] = jnp.zeros_like(dqg_ref)
            dg_ref[...] = jnp.zeros_like(dg_ref)

        xhat, xr = _rms(x_ref[...], D)
        hn = (xhat * g_ref[...]).astype(BF)
        dx2 = dx2_ref[...]
        dx2b = dx2.astype(BF)
        dh = jnp.zeros((tq, D), F32)
        dqg = jnp.zeros((1, LANES), F32)
        for pb in range(MEM_HEADS // 2):
            do = _dot(dx2b, wo_ref[pb * LANES:(pb + 1) * LANES, :], NT).astype(BF)
            o = jnp.zeros((tq, LANES), F32)
            for h in (2 * pb, 2 * pb + 1):
                cols = slice(h * LANES, (h + 1) * LANES)
                qh, qr = _rms(_dot(hn, wq_ref[:, cols]), MEM_HD)
                qb = (qh * qg_ref[...]).astype(BF)
                s = _dot(qb, k_ref[h], NT) * MEM_SCALE
                p = jnp.exp(s - jnp.max(s, axis=-1, keepdims=True))
                p = p / jnp.sum(p, axis=-1, keepdims=True)
                pb16 = p.astype(BF)
                o = o + _dot(pb16, v_ref[h])
                dv_ref[h] += _dot(pb16, do, TN)
                dp = _dot(do, v_ref[h], NT)
                ds = (p * (dp - jnp.sum(dp * p, axis=-1, keepdims=True)) * MEM_SCALE).astype(BF)
                dk_ref[h] += _dot(ds, qb, TN)
                dqo = _dot(ds, k_ref[h])
                dqg = dqg + _colsum(dqo * qh)
                dqraw = _rms_bwd(qh, qr, dqo * qg_ref[...], MEM_HD).astype(BF)
                dqr_ref[:, cols] = dqraw
                dh = dh + _dot(dqraw, wq_ref[:, cols], NT)
            o_ref[:, pb * LANES:(pb + 1) * LANES] = o.astype(BF)
        dx = dx2 + _rms_bwd(xhat, xr, dh * g_ref[...], D)
        dx_ref[...] = dx
        dxb_ref[...] = dx.astype(BF)
        dqg_ref[...] += dqg
        dg_ref[...] += _colsum(dh * xhat)

    full = lambda shape: pl.BlockSpec(shape, lambda i: (0,) * len(shape))
    row = lambda w: pl.BlockSpec((tq, w), lambda i: (i, 0))
    return _call(body, name=name, grid=(t // tq,),
                 in_specs=[row(D), row(D), full((1, D)), full((D, 512)), full((1, LANES)), full((MEM_HEADS, N_MEM, LANES)),
                           full((MEM_HEADS, N_MEM, LANES)), full((MEM_HEADS * MEM_HD, D))],
                 out_specs=[row(D), row(D), row(256), row(512), full((MEM_HEADS, N_MEM, LANES)), full((MEM_HEADS, N_MEM, LANES)),
                            full((1, LANES)), full((1, D))],
                 out_shape=[_sds((t, D), F32), _sds((t, D), BF), _sds((t, 256), BF), _sds((t, 512), BF),
                            _sds((MEM_HEADS, N_MEM, LANES), F32), _sds((MEM_HEADS, N_MEM, LANES), F32),
                            _sds((1, LANES), F32), _sds((1, D), F32)],
                 sem=('arbitrary',))(x, dx2, g, wq, qg, kh, vh, wo)


def mlp_fwd(x, h, w1, w2, *, name, tq=1024, tf=512):
    t = x.shape[0]
    tq = _tile(t, tq)

    def body(x_ref, h_ref, w1_ref, w2_ref, o_ref):
        @pl.when(pl.program_id(1) == 0)
        def _():
            o_ref[...] = x_ref[...]

        a = jnp.maximum(_dot(h_ref[...], w1_ref[...]), 0.0)
        o_ref[...] += _dot(a * a, w2_ref[...])

    row = pl.BlockSpec((tq, D), lambda i, f: (i, 0))
    return _call(body, name=name, grid=(t // tq, D_FF // tf),
                 in_specs=[row, row, pl.BlockSpec((None, D, tf), lambda i, f: (f, 0, 0)), pl.BlockSpec((tf, D), lambda i, f: (f, 0))],
                 out_specs=row, out_shape=_sds((t, D), F32), sem=('parallel', 'arbitrary'), vmem=VMEM_BIG)(x, h, w1, w2)


def mlp_bwd(h, dx, w1, w2, *, name, tq=1024, tf=512):
    t = h.shape[0]
    tq = _tile(t, tq)

    def body(h_ref, dx_ref, w1_ref, w2_ref, dh_ref, r_ref, da_ref):
        @pl.when(pl.program_id(1) == 0)
        def _():
            dh_ref[...] = jnp.zeros_like(dh_ref)

        a = jnp.maximum(_dot(h_ref[...], w1_ref[...]), 0.0)
        r_ref[...] = (a * a).astype(BF)
        da = (_dot(dx_ref[...], w2_ref[...], NT) * (2.0 * a)).astype(BF)
        da_ref[...] = da
        dh_ref[...] += _dot(da, w1_ref[...], NT)

    row = pl.BlockSpec((tq, D), lambda i, f: (i, 0))
    act = pl.BlockSpec((tq, tf), lambda i, f: (i, f))
    return _call(body, name=name, grid=(t // tq, D_FF // tf),
                 in_specs=[row, row, pl.BlockSpec((None, D, tf), lambda i, f: (f, 0, 0)), pl.BlockSpec((tf, D), lambda i, f: (f, 0))],
                 out_specs=[row, act, act], out_shape=[_sds((t, D), F32), _sds((t, D_FF), BF), _sds((t, D_FF), BF)],
                 sem=('parallel', 'arbitrary'), vmem=VMEM_BIG)(h, dx, w1, w2)


def loss_fwd_bwd(y, target, *, name, tq=512):
    t = y.shape[0]
    tq = _tile(t, tq)

    def body(y_ref, t_ref, dy_ref, dyb_ref, l_ref):
        @pl.when(pl.program_id(0) == 0)
        def _():
            l_ref[...] = jnp.zeros_like(l_ref)

        e = y_ref[...] - t_ref[...]
        dy = e * (1.0 / D)
        dy_ref[...] = dy
        dyb_ref[...] = dy.astype(BF)
        l_ref[...] += _colsum(e * e) * (0.5 / D)

    row = pl.BlockSpec((tq, D), lambda i: (i, 0))
    return _call(body, name=name, grid=(t // tq,), in_specs=[row, row],
                 out_specs=[row, row, pl.BlockSpec((1, D), lambda i: (0, 0))],
                 out_shape=[_sds((t, D), F32), _sds((t, D), BF), _sds((1, D), F32)], sem=('arbitrary',))(y, target)


def prep_early(w):
    w_in = w['w_in']
    z = lambda r, c: jnp.zeros((r, c), w_in.dtype)
    w_in_pad = jnp.concatenate([w_in[:, :896], z(D, 64), w_in[:, 896:928], z(D, 32)], axis=1)
    wq = w['mla_w_uq'].reshape(Q_LORA, MLA_HEADS, QK_DIM).transpose(1, 0, 2)
    wq = jnp.pad(wq, ((0, 0), (0, 0), (0, LANES - QK_DIM)))
    ukv = w['mla_w_ukv'].reshape(KV_LORA, MLA_HEADS, QK_NOPE + V_DIM).transpose(1, 0, 2)
    wk = jnp.pad(ukv[:, :, :QK_NOPE], ((0, 0), (0, 0), (0, LANES - QK_NOPE)))
    vpart = ukv[:, :, QK_NOPE:]
    zv = jnp.zeros_like(vpart)
    odd = (jnp.arange(MLA_HEADS) % 2)[:, None, None] == 1
    wv = jnp.where(odd, jnp.concatenate([zv, vpart], axis=2), jnp.concatenate([vpart, zv], axis=2))
    return dict(w_in=w_in_pad, w_glu=w['ssm_w_glu'], wq=wq, wk=wk, wv=wv)


def prep_late(w):
    mq = jnp.pad(w['mem_w_q'].reshape(D, MEM_HEADS, MEM_HD), ((0, 0), (0, 0), (0, LANES - MEM_HD))).reshape(D, 512)
    mkv = w['mem_w_kv'].reshape(D, MEM_HEADS, 2 * MEM_HD)
    mk = jnp.pad(mkv[:, :, :MEM_HD], ((0, 0), (0, 0), (0, LANES - MEM_HD))).reshape(D, 512)
    mvp = mkv[:, :, MEM_HD:]
    zm = jnp.zeros_like(mvp)
    modd = (jnp.arange(MEM_HEADS) % 2)[None, :, None] == 1
    mv = jnp.where(modd, jnp.concatenate([zm, mvp], axis=2), jnp.concatenate([mvp, zm], axis=2)).reshape(D, 512)
    return dict(w_out=w['w_out'], mq=mq, mk=mk, mv=mv, mo=w['mem_w_o'], w1=w['mlp_w1'], w2=w['mlp_w2'])


def prep_small(t, s):
    row = lambda a: a.reshape(1, -1)
    pad = lambda a: jnp.pad(a, (0, LANES - a.shape[0])).reshape(1, LANES)
    out = s5_prep(t, s['ssm_lambda_re'], s['ssm_lambda_im'], s['ssm_log_step'], s['ssm_b_re'], s['ssm_b_im'],
                  s['ssm_c_re'], s['ssm_c_im'])
    out.update(d=row(s['ssm_d']), norm_mix=row(s['norm_mix']), b_glu=row(s['ssm_b_glu']), q_norm=row(s['mla_q_norm']),
               kv_norm=row(s['mla_kv_norm']), q_gain=pad(s['mla_q_gain']), k_gain=pad(s['mla_k_gain']),
               g_ssm=row(s['out_norm_ssm']), g_mla=row(s['out_norm_mla']), norm_mem_q=row(s['norm_mem_q']),
               norm_mem_kv=row(s['norm_mem_kv']), mem_q_gain=pad(s['mem_q_gain']), mem_k_gain=pad(s['mem_k_gain']),
               norm_mlp=row(s['norm_mlp']))
    return out


def _perm(a):
    t, c = a.shape
    return a.reshape(SEGS, t // SEGS, c).transpose(1, 0, 2).reshape(t, c)


def _unperm(a):
    t, c = a.shape
    return a.reshape(t // SEGS, SEGS, c).transpose(1, 0, 2).reshape(t, c)


def layer_fwd(l, x, mem, tabs, plan, ws):
    n = lambda s: f'l{l}_{s}'
    wb = prep_early(plan.early(l))
    h1 = rmsnorm_fwd(x, ws['norm_mix'], name=n('norm_mix'))
    proj = mm(h1, wb['w_in'], 'nn', name=n('w_in'))
    ypre_p = s5_fwd(_perm(proj[:, :SSM_W]), ws, name=n('s5'))
    ypre = _unperm(ypre_p)
    y_ssm = glu_fwd(ypre, wb['w_glu'], ws['b_glu'], name=n('glu'))
    mw = dict(q_norm=ws['q_norm'], kv_norm=ws['kv_norm'], wq=wb['wq'], wk=wb['wk'], wv=wb['wv'],
              q_gain=ws['q_gain'], k_gain=ws['k_gain'])
    q, qt, k, kt, v = mla_prep_fwd(proj, tabs, mw, name=n('mla_prep'))
    o, lse, *gathered = flash_fwd(q, kt, v, name=n('flash'), gather=plan.gather_src(l))
    plan.gathered(l, gathered)
    wb.update(prep_late(plan.late(l)))
    x1, yn = mix_out_fwd(x, y_ssm, o, ws['g_ssm'], ws['g_mla'], wb['w_out'], name=n('mix_out'))
    mh, kh, vh = memkv_fwd(mem, ws['norm_mem_kv'], wb['mk'], wb['mv'], ws['mem_k_gain'], name=n('memkv'))
    x2, h2 = memattn_fwd(x1, ws['norm_mem_q'], wb['mq'], ws['mem_q_gain'], kh, vh, wb['mo'], name=n('memattn'))
    h3 = rmsnorm_fwd(x2, ws['norm_mlp'], name=n('norm_mlp'))
    x3 = mlp_fwd(x2, h3, wb['w1'], wb['w2'], name=n('mlp'))
    saved = dict(x=x, h1=h1, proj=proj, ypre=ypre, y_ssm=y_ssm, q=q, qt=qt, k=k, kt=kt, v=v, o=o, lse=lse, x1=x1, yn=yn,
                 kh=kh, vh=vh, x2=x2, h2=h2, h3=h3, mw=mw)
    return x3, wb, saved


def layer_bwd(l, dx3, dx3b, mem, tabs, plan, wb, ws, sv):
    n = lambda s: f'l{l}_{s}_bwd'
    gb, gs = {}, {}
    structs = lambda names: {k: _sds(plan.shapes[k], F32) for k in names}
    dh3, r, da = mlp_bwd(sv['h3'], dx3b, wb['w1'], wb['w2'], name=n('mlp'))
    gb['w1'] = mm(sv['h3'], da, 'tn', name=n('w1'), slots=NDEV)
    gb['w2'] = mm(r, dx3b, 'tn', name=n('w2'))
    dx2, dx2b, gs['norm_mlp'] = rmsnorm_bwd(sv['x2'], ws['norm_mlp'], dh3, dx3, name=n('norm_mlp'))
    dx1, dx1b, o_mem, dqr_mem, dkh, dvh, gs['mem_q_gain'], gs['norm_mem_q'] = memattn_bwd(
        sv['x1'], dx2, ws['norm_mem_q'], wb['mq'], ws['mem_q_gain'], sv['kh'], sv['vh'], wb['mo'], name=n('memattn'))
    gb['mo'] = mm(o_mem, dx2b, 'tn', name=n('mo'))
    gb['mq'] = mm(sv['h2'], dqr_mem, 'tn', name=n('mq'))
    gb['mk'], gb['mv'], gs['mem_k_gain'], gs['norm_mem_kv'] = memkv_bwd(
        mem, ws['norm_mem_kv'], wb['mk'], wb['mv'], ws['mem_k_gain'], dkh, dvh, name=n('memkv'))
    dyn = mm(dx1b, wb['w_out'], 'nt', name=n('w_out_dx'))
    gb['w_out'] = mm(sv['yn'], dx1b, 'tn', name=n('w_out'))
    dy_ssm, _, gs['g_ssm'] = rmsnorm_bwd(sv['y_ssm'], ws['g_ssm'], dyn, None, name=n('out_norm_ssm'), col=0)
    do, dot, delta, gs['g_mla'] = mla_out_bwd(sv['o'], dyn, ws['g_mla'], name=n('out_norm_mla'))
    late = {k: gb.pop(k) for k in ('w_out', 'mq', 'mk', 'mv', 'mo', 'w1', 'w2')}
    plan.late_grads(l, jax.linear_transpose(prep_late, structs(BIG_LATE))(late)[0])
    dq, dk, dv, *received = flash_bwd(sv['q'], sv['qt'], sv['k'], sv['kt'], sv['v'], do, dot, sv['lse'], delta,
                                      name=n('flash'), scatter=plan.scatter_src(l))
    plan.scattered(l, received)
    (dproj_m, cqn, ckvn, dqr, dkr, dvb, gs['q_norm'], gs['kv_norm'], gs['q_gain'], gs['k_gain']) = mla_prep_bwd(
        sv['proj'], tabs, sv['mw'], dq, dk, dv, name=n('mla_prep'))
    by_head = lambda g: g.reshape(g.shape[0], MLA_HEADS, LANES).transpose(1, 0, 2)
    gb['wq'] = by_head(mm(cqn, dqr, 'tn', name=n('wq')))
    gb['wk'] = by_head(mm(ckvn, dkr, 'tn', name=n('wk')))
    gb['wv'] = by_head(mm(ckvn, dvb, 'tn', name=n('wv')))
    dypre, yg, dz, gs['b_glu'] = glu_bwd(sv['ypre'], dy_ssm, wb['w_glu'], ws['b_glu'], name=n('glu'))
    gb['w_glu'] = mm(yg, dz, 'tn', name=n('w_glu'))
    u_p = _perm(sv['proj'][:, :SSM_W])
    du_p, gs['ar'], gs['ai'], gs['bre'], gs['bim'], gs['cre'], gs['cim'], gs['d'] = s5_bwd(u_p, _perm(dypre), ws, name=n('s5'))
    dprojb = jnp.concatenate([_unperm(du_p), dproj_m], axis=1)
    dh1 = mm(dprojb, wb['w_in'], 'nt', name=n('w_in_dx'))
    gb['w_in'] = mm(sv['h1'], dprojb, 'tn', name=n('w_in'))
    dx0, dx0b, gs['norm_mix'] = rmsnorm_bwd(sv['x'], ws['norm_mix'], dh1, dx1, name=n('norm_mix'))
    plan.early_grads(l, jax.linear_transpose(prep_early, structs(BIG_EARLY))(gb)[0])
    return dx0, dx0b, gs


def local_step(x, mem, positions, target, small, plan):
    t = x.shape[0]
    tabs = rope_tables(positions)
    layers = []
    for l in range(DEPTH):
        ws, small_vjp = jax.vjp(functools.partial(prep_small, t), {k: small[k][l] for k in SMALL})
        x, wb, sv = layer_fwd(l, x, mem, tabs, plan, ws)
        layers.append((wb, ws, small_vjp, sv))
    dx, dxb, lcols = loss_fwd_bwd(x, target, name='loss')
    loss = jnp.sum(lcols)
    gsmall = [None] * DEPTH
    for l in reversed(range(DEPTH)):
        wb, ws, small_vjp, sv = layers[l]
        dx, dxb, gs = layer_bwd(l, dx, dxb, mem, tabs, plan, wb, ws, sv)
        gs['pr'], gs['pi'] = jnp.zeros_like(ws['pr']), jnp.zeros_like(ws['pi'])
        gsmall[l] = small_vjp(gs)[0]
    return loss, dx, gsmall


class ExchangePlan:
    def __init__(self, shard_shapes, mine, first_early):
        self.shapes = {k: (s[1] * (NDEV if BIG_AXIS[k] == 1 else 1), s[2] * (NDEV if BIG_AXIS[k] == 2 else 1))
                       for k, s in shard_shapes.items()}
        self.shard = {k: s[1:] for k, s in shard_shapes.items()}
        self.shapes['mlp_w1'] = (NDEV,) + self.shard['mlp_w1']
        self.mine = mine
        self.w_early = {0: first_early}
        self.w_late = {}
        self.g_late, self.g_early = {}, {}
        self.r_late, self.r_early = {}, {}

    def _unpack(self, g, names):
        out, r0 = {}, 0
        for k in names:
            nr = math.prod(self.shard[k]) // D
            s = g[:, r0:r0 + nr]
            out[k] = (s.reshape(self.shapes[k]) if k == 'mlp_w1'
                      else _from_slots(s.reshape(NDEV, -1), (1,) + self.shard[k], BIG_AXIS[k])[0])
            r0 += nr
        return out

    def _pack(self, g, names, rows):
        slots = jnp.concatenate([g[k].reshape(NDEV, -1) if k == 'mlp_w1' else _to_slots(g[k][None], BIG_AXIS[k])
                                 for k in names], axis=1)
        return jnp.pad(slots, ((0, 0), (0, rows * D - slots.shape[1]))).astype(BF).reshape(NDEV, rows, D)

    def early(self, l):
        return self._unpack(self.w_early.pop(l), BIG_EARLY)

    def late(self, l):
        return self._unpack(self.w_late.pop(l), BIG_LATE)

    def gather_src(self, l):
        src = [self.mine[l, :LATE_ROWS]]
        if l + 1 < DEPTH:
            src.append(self.mine[l + 1, LATE_ROWS:])
        return tuple(src)

    def gathered(self, l, res):
        self.w_late[l] = res[0]
        if l + 1 < DEPTH:
            self.w_early[l + 1] = res[1]

    def late_grads(self, l, g):
        self.g_late[l] = self._pack(g, BIG_LATE, LATE_ROWS)

    def early_grads(self, l, g):
        self.g_early[l] = self._pack(g, BIG_EARLY, LAYER_ROWS - LATE_ROWS)

    def scatter_src(self, l):
        src = [self.g_late.pop(l)]
        if l + 1 < DEPTH:
            src.append(self.g_early.pop(l + 1))
        return tuple(src)

    def scattered(self, l, res):
        self.r_late[l] = res[0]
        if l + 1 < DEPTH:
            self.r_early[l + 1] = res[1]


def _peer(k):
    x, y, c = lax.axis_index('x'), lax.axis_index('y'), lax.axis_index('c')
    px, py, pc = x ^ ((k >> 2) & 1), y ^ ((k >> 1) & 1), c ^ (k & 1)
    return (px, py, pc), 4 * px + 2 * py + pc


def _copies(kind, src_ref, dst_ref, send_sems, recv_sems, loc_sem):
    _, me = _peer(0)
    src = (lambda p: src_ref.at[p]) if kind == 'scatter' else (lambda p: src_ref)
    local = pltpu.make_async_copy(src(me), dst_ref.at[me], loc_sem)
    sends, recvs = [], []
    for k in range(1, NDEV):
        dev, p = _peer(k)
        for slot, lst in ((me, sends), (p, recvs)):
            lst.append(pltpu.make_async_remote_copy(src_ref=src(p), dst_ref=dst_ref.at[slot], send_sem=send_sems.at[k - 1],
                                                    recv_sem=recv_sems.at[k - 1], device_id=dev,
                                                    device_id_type=pl.DeviceIdType.MESH))
    return local, sends, recvs


def _start_copies(cs):
    local, sends, _ = cs
    local.start()
    for cp in sends:
        cp.start()


def _wait_copies(cs):
    local, sends, recvs = cs
    for cp in sends:
        cp.wait_send()
    for cp in recvs:
        cp.wait_recv()
    local.wait()


_COMM_SCRATCH = (pltpu.SemaphoreType.DMA((NDEV - 1,)), pltpu.SemaphoreType.DMA((NDEV - 1,)), pltpu.SemaphoreType.DMA(()))
_ANY = pl.BlockSpec(memory_space=pl.ANY)


def exchange(scatters, gathers, *, name):
    ins = list(scatters) + list(gathers)
    kinds = ['scatter'] * len(scatters) + ['gather'] * len(gathers)
    n_in = len(ins)
    outs = [_sds(a.shape, a.dtype) for a in scatters] + [_sds((NDEV,) + b.shape, b.dtype) for b in gathers]

    def body(*refs):
        in_refs, out_refs, sems = refs[:n_in], refs[n_in:2 * n_in], refs[2 * n_in:]
        sets = [_copies(kind, in_refs[i], out_refs[i], *sems[3 * i:3 * i + 3]) for i, kind in enumerate(kinds)]
        for cs in sets:
            _start_copies(cs)
        for cs in sets:
            _wait_copies(cs)

    return pl.pallas_call(body, name=name, in_specs=[_ANY] * n_in, out_specs=[_ANY] * n_in, out_shape=outs,
                          scratch_shapes=list(_COMM_SCRATCH * n_in))(*ins)


def adamw(w, m, v, g8, *, name, tr):
    r = w.shape[0]
    c1 = 1.0 / (1.0 - ADAM_B1 ** ADAM_STEP)
    c2 = 1.0 / (1.0 - ADAM_B2 ** ADAM_STEP)

    def body(w_ref, m_ref, v_ref, g_ref, go_ref, d_ref, mo_ref, vo_ref):
        g = g_ref[0].astype(F32)
        for i in range(1, NDEV):
            g = g + g_ref[i].astype(F32)
        m_new = ADAM_B1 * m_ref[...] + (1.0 - ADAM_B1) * g
        v_new = ADAM_B2 * v_ref[...] + (1.0 - ADAM_B2) * (g * g)
        go_ref[...] = g
        mo_ref[...] = m_new
        vo_ref[...] = v_new
        d_ref[...] = -ADAM_LR * ((m_new * c1) / (jnp.sqrt(v_new * c2) + ADAM_EPS) + ADAM_WD * w_ref[...])

    row = pl.BlockSpec((tr, D), lambda i: (i, 0))
    return _call(body, name=name, grid=(r // tr,),
                 in_specs=[row, row, row, pl.BlockSpec((NDEV, tr, D), lambda i: (0, i, 0))],
                 out_specs=[row] * 4, out_shape=[_sds((r, D), F32)] * 4, sem=('parallel',), vmem=VMEM_BIG)(w, m, v, g8)


def _flat_rows(parts, rows):
    flat = jnp.concatenate([p.reshape(-1) for p in parts])
    return jnp.pad(flat, (0, rows * D - flat.shape[0])).reshape(rows, D)


def _unflat(flat2d, shapes):
    flat = flat2d.reshape(-1)
    out, off = [], 0
    for s in shapes:
        n = math.prod(s)
        out.append(flat[off:off + n].reshape(s))
        off += n
    return out


def _to_slots(g, axis):
    l, r, c = g.shape
    if axis == 1:
        return g.reshape(l, NDEV, r // NDEV, c).transpose(1, 0, 2, 3).reshape(NDEV, -1)
    return g.reshape(l, r, NDEV, c // NDEV).transpose(2, 0, 1, 3).reshape(NDEV, -1)


def _from_slots(s, shard_shape, axis):
    l, r, c = shard_shape
    s = s.reshape(NDEV, l, r, c)
    if axis == 1:
        return s.transpose(1, 0, 2, 3).reshape(l, NDEV * r, c)
    return s.transpose(1, 2, 0, 3).reshape(l, r, NDEV * c)


LATE_ROWS = 1280
LAYER_ROWS = 1536
BIG_ROWS = DEPTH * LAYER_ROWS
SMALL_ROWS = 640


def kernel(x, mem, positions, norm_mix, w_in, ssm_lambda_re, ssm_lambda_im, ssm_log_step, ssm_b_re, ssm_b_im, ssm_c_re, ssm_c_im, ssm_d, ssm_w_glu, ssm_b_glu, mla_q_norm, mla_w_uq, mla_kv_norm, mla_w_ukv, mla_q_gain, mla_k_gain, out_norm_ssm, out_norm_mla, w_out, norm_mem_q, norm_mem_kv, mem_w_q, mem_w_kv, mem_q_gain, mem_k_gain, mem_w_o, norm_mlp, mlp_w1, mlp_w2, loss_target, m_norm_mix, m_w_in, m_ssm_lambda_re, m_ssm_lambda_im, m_ssm_log_step, m_ssm_b_re, m_ssm_b_im, m_ssm_c_re, m_ssm_c_im, m_ssm_d, m_ssm_w_glu, m_ssm_b_glu, m_mla_q_norm, m_mla_w_uq, m_mla_kv_norm, m_mla_w_ukv, m_mla_q_gain, m_mla_k_gain, m_out_norm_ssm, m_out_norm_mla, m_w_out, m_norm_mem_q, m_norm_mem_kv, m_mem_w_q, m_mem_w_kv, m_mem_q_gain, m_mem_k_gain, m_mem_w_o, m_norm_mlp, m_mlp_w1, m_mlp_w2, v_norm_mix, v_w_in, v_ssm_lambda_re, v_ssm_lambda_im, v_ssm_log_step, v_ssm_b_re, v_ssm_b_im, v_ssm_c_re, v_ssm_c_im, v_ssm_d, v_ssm_w_glu, v_ssm_b_glu, v_mla_q_norm, v_mla_w_uq, v_mla_kv_norm, v_mla_w_ukv, v_mla_q_gain, v_mla_k_gain, v_out_norm_ssm, v_out_norm_mla, v_w_out, v_norm_mem_q, v_norm_mem_kv, v_mem_w_q, v_mem_w_kv, v_mem_q_gain, v_mem_k_gain, v_mem_w_o, v_norm_mlp, v_mlp_w1, v_mlp_w2):
    wvals = (norm_mix, w_in, ssm_lambda_re, ssm_lambda_im, ssm_log_step, ssm_b_re, ssm_b_im, ssm_c_re, ssm_c_im, ssm_d, ssm_w_glu, ssm_b_glu, mla_q_norm, mla_w_uq, mla_kv_norm, mla_w_ukv, mla_q_gain, mla_k_gain, out_norm_ssm, out_norm_mla, w_out, norm_mem_q, norm_mem_kv, mem_w_q, mem_w_kv, mem_q_gain, mem_k_gain, mem_w_o, norm_mlp, mlp_w1, mlp_w2)
    mvals = (m_norm_mix, m_w_in, m_ssm_lambda_re, m_ssm_lambda_im, m_ssm_log_step, m_ssm_b_re, m_ssm_b_im, m_ssm_c_re, m_ssm_c_im, m_ssm_d, m_ssm_w_glu, m_ssm_b_glu, m_mla_q_norm, m_mla_w_uq, m_mla_kv_norm, m_mla_w_ukv, m_mla_q_gain, m_mla_k_gain, m_out_norm_ssm, m_out_norm_mla, m_w_out, m_norm_mem_q, m_norm_mem_kv, m_mem_w_q, m_mem_w_kv, m_mem_q_gain, m_mem_k_gain, m_mem_w_o, m_norm_mlp, m_mlp_w1, m_mlp_w2)
    vvals = (v_norm_mix, v_w_in, v_ssm_lambda_re, v_ssm_lambda_im, v_ssm_log_step, v_ssm_b_re, v_ssm_b_im, v_ssm_c_re, v_ssm_c_im, v_ssm_d, v_ssm_w_glu, v_ssm_b_glu, v_mla_q_norm, v_mla_w_uq, v_mla_kv_norm, v_mla_w_ukv, v_mla_q_gain, v_mla_k_gain, v_out_norm_ssm, v_out_norm_mla, v_w_out, v_norm_mem_q, v_norm_mem_kv, v_mem_w_q, v_mem_w_kv, v_mem_q_gain, v_mem_k_gain, v_mem_w_o, v_norm_mlp, v_mlp_w1, v_mlp_w2)
    w = dict(zip(WEIGHTS, wvals))
    m = dict(zip(WEIGHTS, mvals))
    v = dict(zip(WEIGHTS, vvals))

    shard_shapes = {k: w[k].shape for k in BIG}
    layer_shapes = [shard_shapes[k][1:] for k in BIG]

    def layer_flat(parts):
        flat = jnp.concatenate([p.reshape(DEPTH, -1) for p in parts], axis=1)
        return jnp.pad(flat, ((0, 0), (0, LAYER_ROWS * D - flat.shape[1]))).reshape(DEPTH, LAYER_ROWS, D)

    mine = layer_flat([w[k].astype(BF) for k in BIG])
    first, = exchange([], [mine[0, LATE_ROWS:]], name='gather_early0')
    plan = ExchangePlan(shard_shapes, mine, first)
    small = {k: w[k] for k in SMALL}
    loss, grad_x, gsmall = local_step(x[0], mem[0], positions[0], loss_target[0], small, plan)
    gs_full = [jnp.stack([gsmall[l][k] for l in range(DEPTH)]) for k in SMALL]
    small_flat = _flat_rows(gs_full, SMALL_ROWS).astype(BF)
    plan.r_early[0], g8_small, losses = exchange([plan.g_early.pop(0)], [small_flat, jnp.full((8, LANES), loss, F32)],
                                                 name='exchange_last')
    loss_all = jnp.sum(losses[:, 0, 0])
    g8_big = jnp.concatenate([r[l] for l in range(DEPTH) for r in (plan.r_late, plan.r_early)], axis=1)

    small_shapes = [w[k].shape for k in SMALL]
    flat_big = lambda d: layer_flat([d[k] for k in BIG]).reshape(BIG_ROWS, D)
    gb, db, mb, vb = adamw(flat_big(w), flat_big(m), flat_big(v), g8_big, name='adamw_big', tr=256)
    gs, ds, ms, vs = adamw(_flat_rows([w[k] for k in SMALL], SMALL_ROWS), _flat_rows([m[k] for k in SMALL], SMALL_ROWS),
                           _flat_rows([v[k] for k in SMALL], SMALL_ROWS), g8_small, name='adamw_small', tr=128)

    def unflat_big(fb):
        fb, out, r0 = fb.reshape(DEPTH, LAYER_ROWS, D), [], 0
        for k, shp in zip(BIG, layer_shapes):
            nr = math.prod(shp) // D
            out.append(fb[:, r0:r0 + nr].reshape(shard_shapes[k]))
            r0 += nr
        return out

    res = {}
    for tag, fb, fs in (('g', gb, gs), ('d', db, ds), ('m', mb, ms), ('v', vb, vs)):
        res[tag] = dict(zip(BIG, unflat_big(fb)))
        res[tag].update(zip(SMALL, _unflat(fs, small_shapes)))
    return (loss_all, grad_x[None], *[res['g'][k] for k in WEIGHTS], *[res['d'][k] for k in WEIGHTS],
            *[res['m'][k] for k in WEIGHTS], *[res['v'][k] for k in WEIGHTS])
```

```python
import functools
import math

import jax
import jax.numpy as jnp
from jax import lax
from jax.experimental import pallas as pl
from jax.experimental.pallas import tpu as pltpu

F32 = jnp.float32
BF = jnp.bfloat16

D = 1024
DEPTH = 4
N_MEM = 256
MEM_HEADS = 4
MEM_HD = 64
SSM_W = 512
SSM_G = 32
SSM_H = 16
SSM_P = 64
MLA_HEADS = 8
QK_NOPE = 64
QK_ROPE = 32
QK_DIM = 96
V_DIM = 64
Q_LORA = 256
KV_LORA = 128
ROPE_THETA = 10000.0
D_FF = 4096
IN_COLS = 928
EPS = 1e-6
NDEV = 8
LANES = 128
SEGS = 8
S5_LW = 256
S5_NHB = (SSM_G * SSM_P) // S5_LW
ADAM_LR = 0.001
ADAM_B1 = 0.9
ADAM_B2 = 0.999
ADAM_EPS = 1e-08
ADAM_WD = 0.01
ADAM_STEP = 10
VMEM_BIG = 56 * 1024 * 1024

NN = (((1,), (0,)), ((), ()))
NT = (((1,), (1,)), ((), ()))
TN = (((0,), (0,)), ((), ()))

BIG_LATE = ('w_out', 'mem_w_q', 'mem_w_kv', 'mem_w_o', 'mlp_w1', 'mlp_w2')
BIG_EARLY = ('w_in', 'ssm_w_glu', 'mla_w_uq', 'mla_w_ukv')
BIG = BIG_LATE + BIG_EARLY
BIG_AXIS = {'w_in': 1, 'ssm_w_glu': 1, 'mla_w_uq': 2, 'mla_w_ukv': 2, 'w_out': 1, 'mem_w_q': 1, 'mem_w_kv': 1,
            'mem_w_o': 2, 'mlp_w1': 2, 'mlp_w2': 1}
SMALL = ('norm_mix', 'ssm_lambda_re', 'ssm_lambda_im', 'ssm_log_step', 'ssm_b_re', 'ssm_b_im', 'ssm_c_re', 'ssm_c_im',
         'ssm_d', 'ssm_b_glu', 'mla_q_norm', 'mla_kv_norm', 'mla_q_gain', 'mla_k_gain', 'out_norm_ssm', 'out_norm_mla',
         'norm_mem_q', 'norm_mem_kv', 'mem_q_gain', 'mem_k_gain', 'norm_mlp')
WEIGHTS = ('norm_mix', 'w_in', 'ssm_lambda_re', 'ssm_lambda_im', 'ssm_log_step', 'ssm_b_re', 'ssm_b_im', 'ssm_c_re',
           'ssm_c_im', 'ssm_d', 'ssm_w_glu', 'ssm_b_glu', 'mla_q_norm', 'mla_w_uq', 'mla_kv_norm', 'mla_w_ukv',
           'mla_q_gain', 'mla_k_gain', 'out_norm_ssm', 'out_norm_mla', 'w_out', 'norm_mem_q', 'norm_mem_kv', 'mem_w_q',
           'mem_w_kv', 'mem_q_gain', 'mem_k_gain', 'mem_w_o', 'norm_mlp', 'mlp_w1', 'mlp_w2')


def _call(body, *, name, out_shape, grid=(), in_specs=None, out_specs=None, scratch=(), sem=None, vmem=None):
    params = {}
    if sem is not None:
        params['dimension_semantics'] = sem
    if vmem is not None:
        params['vmem_limit_bytes'] = vmem
    specs = {} if in_specs is None else dict(grid=grid, in_specs=in_specs, out_specs=out_specs)
    return pl.pallas_call(body, name=name, out_shape=out_shape, scratch_shapes=list(scratch),
                          compiler_params=pltpu.CompilerParams(**params), **specs)


def _sds(shape, dtype):
    return jax.ShapeDtypeStruct(shape, dtype)


def _dot(a, b, dims=NN):
    return lax.dot_general(a.astype(BF), b.astype(BF), dims, preferred_element_type=F32)


def _split(a):
    hi = a.astype(BF)
    return hi, (a - hi.astype(F32)).astype(BF)


def _dot3(a, b, dims=NN):
    ah, al = _split(a)
    bh, bl = _split(b)
    d = lambda p, q: lax.dot_general(p, q, dims, preferred_element_type=F32)
    return d(ah, bh) + (d(ah, bl) + d(al, bh))


_sdot = _dot


def _rms(x, n):
    r = lax.rsqrt(jnp.sum(x * x, axis=-1, keepdims=True) * (1.0 / n) + EPS)
    return x * r, r


def _rms_bwd(xhat, r, dxhat, n):
    return r * (dxhat - xhat * (jnp.sum(dxhat * xhat, axis=-1, keepdims=True) * (1.0 / n)))


def _colsum(a):
    return jnp.sum(a, axis=0, keepdims=True)


def _tile(t, want):
    return min(t, want)


def _bidx(nb):
    return (lambda b: b) if nb > 1 else (lambda b: 0)


def mm(a, b, mode, *, name, out_dtype=F32, tm=1024, tn=1024, slots=0):
    squeeze = a.ndim == 2 and b.ndim == 2
    a = a[None] if a.ndim == 2 else a
    b = b[None] if b.ndim == 2 else b
    nb = max(a.shape[0], b.shape[0])
    ab, bb = _bidx(a.shape[0]), _bidx(b.shape[0])
    if mode in ('nn', 'nt'):
        m, k = a.shape[1:]
        n = b.shape[2] if mode == 'nn' else b.shape[1]
        tm, tn = _tile(m, tm), _tile(n, tn)
        dims = NN if mode == 'nn' else NT

        def body(a_ref, b_ref, o_ref):
            o_ref[...] = _dot(a_ref[...], b_ref[...], dims).astype(o_ref.dtype)

        bspec = (pl.BlockSpec((None, k, tn), lambda bi, i, j: (bb(bi), 0, j)) if mode == 'nn'
                 else pl.BlockSpec((None, tn, k), lambda bi, i, j: (bb(bi), j, 0)))
        out = _call(body, name=name, grid=(nb, m // tm, n // tn),
                    in_specs=[pl.BlockSpec((None, tm, k), lambda bi, i, j: (ab(bi), i, 0)), bspec],
                    out_specs=pl.BlockSpec((None, tm, tn), lambda bi, i, j: (bi, i, j)),
                    out_shape=_sds((nb, m, n), out_dtype), sem=('parallel', 'parallel', 'parallel'), vmem=VMEM_BIG)(a, b)
    else:
        k, m = a.shape[1:]
        n = b.shape[2]
        tm, tn, tk = _tile(m, 1024), _tile(n, 1024), _tile(k, 512)
        per = 1
        if slots:
            ts = n // slots
            per = tn // ts
            out_spec, out_shape = pl.BlockSpec((per, tm, ts), lambda bi, i, j, kk: (j, i, 0)), _sds((slots, m, ts), F32)
        else:
            out_spec, out_shape = pl.BlockSpec((None, tm, tn), lambda bi, i, j, kk: (bi, i, j)), _sds((nb, m, n), F32)

        def body(a_ref, b_ref, o_ref):
            @pl.when(pl.program_id(3) == 0)
            def _():
                o_ref[...] = jnp.zeros_like(o_ref)

            res = _dot(a_ref[...], b_ref[...], TN)
            if slots:
                for s in range(per):
                    o_ref[s] += res[:, s * ts:(s + 1) * ts]
            else:
                o_ref[...] += res

        out = _call(body, name=name, grid=(nb, m // tm, n // tn, k // tk),
                    in_specs=[pl.BlockSpec((None, tk, tm), lambda bi, i, j, kk: (ab(bi), kk, i)),
                              pl.BlockSpec((None, tk, tn), lambda bi, i, j, kk: (bb(bi), kk, j))],
                    out_specs=out_spec, out_shape=out_shape,
                    sem=('parallel', 'parallel', 'parallel', 'arbitrary'), vmem=VMEM_BIG)(a, b)
    return out[0] if squeeze and not slots else out


def rmsnorm_fwd(x, g, *, name, tq=512):
    t, d = x.shape
    tq = _tile(t, tq)

    def body(x_ref, g_ref, o_ref):
        xh, _ = _rms(x_ref[...], d)
        o_ref[...] = (xh * g_ref[...]).astype(o_ref.dtype)

    return _call(body, name=name, grid=(t // tq,),
                 in_specs=[pl.BlockSpec((tq, d), lambda i: (i, 0)), pl.BlockSpec((1, d), lambda i: (0, 0))],
                 out_specs=pl.BlockSpec((tq, d), lambda i: (i, 0)), out_shape=_sds((t, d), BF), sem=('parallel',))(x, g)


def rmsnorm_bwd(x, g, dh, dres, *, name, col=0, tq=512):
    t, d = x.shape
    tq = _tile(t, tq)
    has_res = dres is not None

    def body(*refs):
        if has_res:
            x_ref, g_ref, dh_ref, dres_ref, dx_ref, dxb_ref, dg_ref = refs
        else:
            x_ref, g_ref, dh_ref, dx_ref, dxb_ref, dg_ref = refs
        xh, r = _rms(x_ref[...], d)
        dh_ = dh_ref[...].astype(F32)
        dx = _rms_bwd(xh, r, dh_ * g_ref[...], d)
        if has_res:
            dx = dx + dres_ref[...]
        dx_ref[...] = dx
        dxb_ref[...] = dx.astype(BF)

        @pl.when(pl.program_id(0) == 0)
        def _():
            dg_ref[...] = jnp.zeros_like(dg_ref)

        dg_ref[...] += _colsum(dh_ * xh)

    in_specs = [pl.BlockSpec((tq, d), lambda i: (i, 0)), pl.BlockSpec((1, d), lambda i: (0, 0)),
                pl.BlockSpec((tq, d), lambda i: (i, col))]
    args = [x, g, dh]
    if has_res:
        in_specs.append(pl.BlockSpec((tq, d), lambda i: (i, 0)))
        args.append(dres)
    row = pl.BlockSpec((tq, d), lambda i: (i, 0))
    return _call(body, name=name, grid=(t // tq,), in_specs=in_specs,
                 out_specs=[row, row, pl.BlockSpec((1, d), lambda i: (0, 0))],
                 out_shape=[_sds((t, d), F32), _sds((t, d), BF), _sds((1, d), F32)], sem=('arbitrary',))(*args)


def _cmul(ar, ai, xr, xi):
    return ar * xr - ai * xi, ar * xi + ai * xr


def _seg_carries(er, ei, pr, pi, reverse):
    lw = er.shape[1]
    zero = jnp.zeros((1, lw), F32)
    order = range(SEGS - 1, -1, -1) if reverse else range(SEGS)
    cin_r, cin_i = [None] * SEGS, [None] * SEGS
    tr, ti = zero, zero
    for j in order:
        cin_r[j], cin_i[j] = tr, ti
        mr, mi = _cmul(pr, pi, tr, ti)
        tr, ti = er[j:j + 1, :] + mr, ei[j:j + 1, :] + mi
    return jnp.concatenate(cin_r, axis=0), jnp.concatenate(cin_i, axis=0)


def _s5_chunk(t):
    return _tile(t, 512)


def s5_fwd(u_p, prm, *, name):
    t = u_p.shape[0]
    ch = _s5_chunk(t)
    nch, steps = t // ch, ch // SEGS
    lw = S5_LW

    def body(u_ref, ar_ref, ai_ref, pr_ref, pi_ref, bre_ref, bim_ref, cre_ref, cim_ref, d_ref, y_ref, bur, bui):
        hb = pl.program_id(0)
        ar = jnp.broadcast_to(ar_ref[0], (SEGS, lw))
        ai = jnp.broadcast_to(ai_ref[0], (SEGS, lw))

        def rows_of(c):
            return pl.ds(pl.multiple_of(c * ch, ch), ch)

        @pl.loop(0, nch)
        def _(c):
            u = u_ref[rows_of(c), :]
            bur[rows_of(c), :] = _sdot(u, bre_ref[0])
            bui[rows_of(c), :] = _sdot(u, bim_ref[0])

        def scan(carry, store):
            def step(i, s):
                r0 = pl.multiple_of(i * SEGS, SEGS)
                mr, mi = _cmul(ar, ai, s[0], s[1])
                nr, ni = mr + bur[pl.ds(r0, SEGS), :], mi + bui[pl.ds(r0, SEGS), :]
                if store:
                    bur[pl.ds(r0, SEGS), :] = nr
                    bui[pl.ds(r0, SEGS), :] = ni
                return nr, ni

            return lax.fori_loop(0, t // SEGS, step, carry, unroll=8)

        zero = jnp.zeros((SEGS, lw), F32)
        er, ei = scan((zero, zero), False)
        scan(_seg_carries(er, ei, pr_ref[0], pi_ref[0], False), True)

        @pl.loop(0, nch)
        def _(c):
            rows = rows_of(c)
            y = _sdot(bur[rows, :], cre_ref[0]) - _sdot(bui[rows, :], cim_ref[0])

            @pl.when(hb % 2 == 0)
            def _():
                y_ref[rows, :] = y + d_ref[...] * u_ref[rows, :]

            @pl.when(hb % 2 == 1)
            def _():
                y_ref[rows, :] += y

    vec = pl.BlockSpec((1, 1, lw), lambda h: (h, 0, 0))
    return _call(
        body, name=name, grid=(S5_NHB,),
        in_specs=[pl.BlockSpec((t, LANES), lambda h: (0, h // 2)), vec, vec, vec, vec,
                  pl.BlockSpec((1, LANES, lw), lambda h: (h, 0, 0)), pl.BlockSpec((1, LANES, lw), lambda h: (h, 0, 0)),
                  pl.BlockSpec((1, lw, LANES), lambda h: (h, 0, 0)), pl.BlockSpec((1, lw, LANES), lambda h: (h, 0, 0)),
                  pl.BlockSpec((1, LANES), lambda h: (0, h // 2))],
        out_specs=pl.BlockSpec((t, LANES), lambda h: (0, h // 2)), out_shape=_sds((t, SSM_W), F32),
        scratch=[pltpu.VMEM((t, lw), F32)] * 2, sem=('arbitrary',), vmem=VMEM_BIG,
    )(u_p, prm['ar'], prm['ai'], prm['pr'], prm['pi'], prm['bre'], prm['bim'], prm['cre'], prm['cim'], prm['d'])


def s5_bwd(u_p, dy_p, prm, *, name):
    t = u_p.shape[0]
    ch = _s5_chunk(t)
    nch, steps = t // ch, ch // SEGS
    lw = S5_LW

    def body(u_ref, dy_ref, ar_ref, ai_ref, pr_ref, pi_ref, bre_ref, bim_ref, cre_ref, cim_ref, d_ref,
             du_ref, dar_ref, dai_ref, dbre_ref, dbim_ref, dcre_ref, dcim_ref, dd_ref, bur, bui, sr, si, du_acc):
        hb = pl.program_id(0)
        ar = jnp.broadcast_to(ar_ref[0], (SEGS, lw))
        ai = jnp.broadcast_to(ai_ref[0], (SEGS, lw))
        zero = jnp.zeros((SEGS, lw), F32)

        def rows_of(c):
            return pl.ds(pl.multiple_of(c * ch, ch), ch)

        nsteps = t // SEGS

        @pl.loop(0, nch)
        def _(c):
            u = u_ref[rows_of(c), :]
            bur[rows_of(c), :] = _sdot(u, bre_ref[0])
            bui[rows_of(c), :] = _sdot(u, bim_ref[0])

        def fwd_scan(carry, store):
            def step(i, s):
                r0 = pl.multiple_of(i * SEGS, SEGS)
                mr, mi = _cmul(ar, ai, s[0], s[1])
                nr, ni = mr + bur[pl.ds(r0, SEGS), :], mi + bui[pl.ds(r0, SEGS), :]
                if store:
                    w0 = pl.multiple_of(i * SEGS + SEGS, SEGS)
                    sr[pl.ds(w0, SEGS), :] = nr
                    si[pl.ds(w0, SEGS), :] = ni
                return nr, ni

            return lax.fori_loop(0, nsteps, step, carry, unroll=8)

        er, ei = fwd_scan((zero, zero), False)
        cin_r, cin_i = _seg_carries(er, ei, pr_ref[0], pi_ref[0], False)
        sr[pl.ds(0, SEGS), :] = cin_r
        si[pl.ds(0, SEGS), :] = cin_i
        fwd_scan((cin_r, cin_i), True)

        @pl.loop(0, nch)
        def _(c):
            dy = dy_ref[rows_of(c), :]
            bur[rows_of(c), :] = _sdot(dy, cre_ref[0], NT)
            bui[rows_of(c), :] = -_sdot(dy, cim_ref[0], NT)

        def rev_local(ii, lam):
            r0 = pl.multiple_of((nsteps - 1 - ii) * SEGS, SEGS)
            mr, mi = _cmul(ar, -ai, lam[0], lam[1])
            return mr + bur[pl.ds(r0, SEGS), :], mi + bui[pl.ds(r0, SEGS), :]

        lr0, li0 = lax.fori_loop(0, nsteps, rev_local, (zero, zero), unroll=8)
        rin = _seg_carries(lr0, li0, pr_ref[0], -pi_ref[0], True)

        def rev_step(ii, st):
            lam_r, lam_i, acc_r, acc_i = st
            r0 = pl.multiple_of((nsteps - 1 - ii) * SEGS, SEGS)
            mr, mi = _cmul(ar, -ai, lam_r, lam_i)
            nr, ni = mr + bur[pl.ds(r0, SEGS), :], mi + bui[pl.ds(r0, SEGS), :]
            bur[pl.ds(r0, SEGS), :] = nr
            bui[pl.ds(r0, SEGS), :] = ni
            pr_, pi_ = sr[pl.ds(r0, SEGS), :], si[pl.ds(r0, SEGS), :]
            return nr, ni, acc_r + (nr * pr_ + ni * pi_), acc_i + (ni * pr_ - nr * pi_)

        _, _, acc_r, acc_i = lax.fori_loop(0, nsteps, rev_step, (rin[0], rin[1], zero, zero), unroll=8)
        dar_ref[0] = _colsum(acc_r)
        dai_ref[0] = _colsum(acc_i)

        dbre_ref[...] = jnp.zeros_like(dbre_ref)
        dbim_ref[...] = jnp.zeros_like(dbim_ref)
        dcre_ref[...] = jnp.zeros_like(dcre_ref)
        dcim_ref[...] = jnp.zeros_like(dcim_ref)

        @pl.loop(0, nch)
        def _(c):
            rows = rows_of(c)
            u = u_ref[rows, :]
            dy = dy_ref[rows, :]
            lam_r, lam_i = bur[rows, :], bui[rows, :]
            du = _sdot(lam_r, bre_ref[0], NT) + _sdot(lam_i, bim_ref[0], NT)

            @pl.when(hb % 2 == 0)
            def _():
                du_acc[rows, :] = du + d_ref[...] * dy

            @pl.when(hb % 2 == 1)
            def _():
                du_ref[rows, :] = (du_acc[rows, :] + du).astype(BF)

            dbre_ref[0] += _sdot(u, lam_r, TN)
            dbim_ref[0] += _sdot(u, lam_i, TN)
            srows = pl.ds(pl.multiple_of(c * ch + SEGS, SEGS), ch)
            dcre_ref[0] += _sdot(sr[srows, :], dy, TN)
            dcim_ref[0] -= _sdot(si[srows, :], dy, TN)

        @pl.when(hb % 2 == 0)
        def _():
            dd_ref[...] = _colsum(dy_ref[...] * u_ref[...])

    vec = pl.BlockSpec((1, 1, lw), lambda h: (h, 0, 0))
    bsp = pl.BlockSpec((1, LANES, lw), lambda h: (h, 0, 0))
    csp = pl.BlockSpec((1, lw, LANES), lambda h: (h, 0, 0))
    act = pl.BlockSpec((t, LANES), lambda h: (0, h // 2))
    dsp = pl.BlockSpec((1, LANES), lambda h: (0, h // 2))
    return _call(
        body, name=name, grid=(S5_NHB,),
        in_specs=[act, act, vec, vec, vec, vec, bsp, bsp, csp, csp, dsp],
        out_specs=[act, vec, vec, bsp, bsp, csp, csp, dsp],
        out_shape=[_sds((t, SSM_W), BF), _sds((S5_NHB, 1, lw), F32), _sds((S5_NHB, 1, lw), F32),
                   _sds((S5_NHB, LANES, lw), F32), _sds((S5_NHB, LANES, lw), F32),
                   _sds((S5_NHB, lw, LANES), F32), _sds((S5_NHB, lw, LANES), F32), _sds((1, SSM_W), F32)],
        scratch=[pltpu.VMEM((t, lw), F32), pltpu.VMEM((t, lw), F32),
                 pltpu.VMEM((t + SEGS, lw), F32), pltpu.VMEM((t + SEGS, lw), F32), pltpu.VMEM((t, LANES), F32)],
        sem=('arbitrary',), vmem=VMEM_BIG,
    )(u_p, dy_p, prm['ar'], prm['ai'], prm['pr'], prm['pi'], prm['bre'], prm['bim'], prm['cre'], prm['cim'], prm['d'])


def s5_prep(t, lam_re, lam_im, log_step, b_re, b_im, c_re, c_im):
    step = jnp.exp(log_step)[:, None]
    mag = jnp.exp(lam_re * step)
    ar, ai = mag * jnp.cos(lam_im * step), mag * jnp.sin(lam_im * step)
    den = lam_re * lam_re + lam_im * lam_im
    nr, ni = ar - 1.0, ai
    fr, fi = (nr * lam_re + ni * lam_im) / den, (ni * lam_re - nr * lam_im) / den
    bbr = fr[..., None] * b_re - fi[..., None] * b_im
    bbi = fr[..., None] * b_im + fi[..., None] * b_re
    gl = S5_LW // SSM_P
    eye = jnp.eye(gl, dtype=F32)
    half = (jnp.arange(S5_NHB) % 2)[:, None, None]

    def bmat(bb):
        x = bb.transpose(0, 2, 1).reshape(S5_NHB, gl, SSM_H, SSM_P)
        x = jnp.einsum('bghp,gk->bghkp', x, eye).reshape(S5_NHB, gl * SSM_H, S5_LW)
        z = jnp.zeros_like(x)
        return jnp.where(half == 0, jnp.concatenate([x, z], axis=1), jnp.concatenate([z, x], axis=1))

    def cmat(cc):
        x = cc.transpose(0, 2, 1).reshape(S5_NHB, gl, SSM_P, SSM_H)
        x = jnp.einsum('bgph,gk->bgpkh', x, eye).reshape(S5_NHB, S5_LW, gl * SSM_H)
        z = jnp.zeros_like(x)
        return jnp.where(half == 0, jnp.concatenate([x, z], axis=2), jnp.concatenate([z, x], axis=2))

    vec = lambda a: a.reshape(S5_NHB, 1, S5_LW)
    ni_steps = float(t // SEGS)
    pmag = jnp.exp(lam_re * step * ni_steps)
    pr, pi = pmag * jnp.cos(lam_im * step * ni_steps), pmag * jnp.sin(lam_im * step * ni_steps)
    return dict(ar=vec(ar), ai=vec(ai), bre=bmat(bbr), bim=bmat(bbi), cre=cmat(c_re), cim=cmat(c_im),
                pr=lax.stop_gradient(vec(pr)), pi=lax.stop_gradient(vec(pi)))


def _gelu(x):
    c = math.sqrt(2.0 / math.pi)
    return 0.5 * x * (1.0 + jnp.tanh(c * (x + 0.044715 * (x * x * x))))


def _gelu_grad(x):
    c = math.sqrt(2.0 / math.pi)
    th = jnp.tanh(c * (x + 0.044715 * (x * x * x)))
    return 0.5 * (1.0 + th) + 0.5 * x * (1.0 - th * th) * (c * (1.0 + 3.0 * 0.044715 * (x * x)))


def glu_fwd(ypre, w_glu, b_glu, *, name, tq=512):
    t = ypre.shape[0]
    tq = _tile(t, tq)

    def body(y_ref, w_ref, b_ref, o_ref):
        yg = _gelu(y_ref[...])
        z = _dot(yg, w_ref[...]) + b_ref[...]
        o_ref[...] = yg * jax.nn.sigmoid(z)

    return _call(body, name=name, grid=(t // tq,),
                 in_specs=[pl.BlockSpec((tq, SSM_W), lambda i: (i, 0)), pl.BlockSpec((SSM_W, SSM_W), lambda i: (0, 0)),
                           pl.BlockSpec((1, SSM_W), lambda i: (0, 0))],
                 out_specs=pl.BlockSpec((tq, SSM_W), lambda i: (i, 0)), out_shape=_sds((t, SSM_W), F32),
                 sem=('parallel',))(ypre, w_glu, b_glu)


def glu_bwd(ypre, dy, w_glu, b_glu, *, name, tq=512):
    t = ypre.shape[0]
    tq = _tile(t, tq)

    def body(y_ref, dy_ref, w_ref, b_ref, dyp_ref, yg_ref, dz_ref, db_ref):
        ypre_ = y_ref[...]
        yg = _gelu(ypre_)
        sig = jax.nn.sigmoid(_dot(yg, w_ref[...]) + b_ref[...])
        dy_ = dy_ref[...]
        dz = dy_ * yg * sig * (1.0 - sig)
        dyg = dy_ * sig + _dot(dz, w_ref[...], NT)
        dyp_ref[...] = dyg * _gelu_grad(ypre_)
        yg_ref[...] = yg.astype(BF)
        dz_ref[...] = dz.astype(BF)

        @pl.when(pl.program_id(0) == 0)
        def _():
            db_ref[...] = jnp.zeros_like(db_ref)

        db_ref[...] += _colsum(dz)

    row = pl.BlockSpec((tq, SSM_W), lambda i: (i, 0))
    vec = pl.BlockSpec((1, SSM_W), lambda i: (0, 0))
    return _call(body, name=name, grid=(t // tq,),
                 in_specs=[row, row, pl.BlockSpec((SSM_W, SSM_W), lambda i: (0, 0)), vec],
                 out_specs=[row, row, row, vec],
                 out_shape=[_sds((t, SSM_W), F32), _sds((t, SSM_W), BF), _sds((t, SSM_W), BF), _sds((1, SSM_W), F32)],
                 sem=('arbitrary',))(ypre, dy, w_glu, b_glu)


def _rope(x, cos, sa, sb):
    return x * cos + pltpu.roll(x, 16, 1) * sa + pltpu.roll(x, 112, 1) * sb


def _rope_t(d, cos, sa, sb):
    return d * cos + pltpu.roll(d * sa, 112, 1) + pltpu.roll(d * sb, 16, 1)


def rope_tables(positions):
    half = QK_ROPE // 2
    inv_freq = ROPE_THETA ** (-jnp.arange(half, dtype=F32) / half)
    ang = positions.astype(F32)[:, None] * inv_freq
    cos, sin = jnp.cos(ang), jnp.sin(ang)
    t = positions.shape[0]
    one, zero = jnp.ones((t, QK_NOPE), F32), jnp.zeros((t, QK_NOPE), F32)
    pad1, pad0 = jnp.ones((t, 32), F32), jnp.zeros((t, 32), F32)
    z16 = jnp.zeros((t, half), F32)
    return (jnp.concatenate([one, cos, cos, pad1], axis=1), jnp.concatenate([zero, z16, sin, pad0], axis=1),
            jnp.concatenate([zero, -sin, z16, pad0], axis=1))


def mla_prep_fwd(proj, tabs, w, *, name):
    t = proj.shape[0]
    tq = _tile(t, ATT_BLK)

    def body(cq_ref, ckv_ref, kr_ref, cos_ref, sa_ref, sb_ref, qn_ref, kvn_ref, wq_ref, wk_ref, wv_ref, qg_ref, kg_ref,
             q_ref, qt_ref, k_ref, kt_ref, v_ref):
        cqn = (_rms(cq_ref[...], Q_LORA)[0] * qn_ref[...]).astype(BF)
        ckvn = (_rms(ckv_ref[...], KV_LORA)[0] * kvn_ref[...]).astype(BF)
        cos, sa, sb = cos_ref[...], sa_ref[...], sb_ref[...]
        kr = kr_ref[...]
        for h in range(MLA_HEADS):
            q = _rms(_dot(cqn, wq_ref[h]), QK_DIM)[0] * qg_ref[...]
            q = _rope(q, cos, sa, sb) * ATT_SCALE
            q_ref[h] = q.astype(BF)
            qt_ref[h, 0] = q.T.astype(BF)
            k = _rms(_dot(ckvn, wk_ref[h]) + kr, QK_DIM)[0] * kg_ref[...]
            k = _rope(k, cos, sa, sb)
            k_ref[h] = k.astype(BF)
            kt_ref[h, 0] = k.T.astype(BF)
            v_ref[h] = _dot(ckvn, wv_ref[h]).astype(BF)

    tab = pl.BlockSpec((tq, LANES), lambda i: (i, 0))
    full = lambda shape: pl.BlockSpec(shape, lambda i: (0,) * len(shape))
    hout = pl.BlockSpec((MLA_HEADS, tq, LANES), lambda i: (0, i, 0))
    tout = pl.BlockSpec((MLA_HEADS, 1, LANES, tq), lambda i: (0, i, 0, 0))
    hshape = _sds((MLA_HEADS, t, LANES), BF)
    tshape = _sds((MLA_HEADS, t // tq, LANES, tq), BF)
    return _call(
        body, name=name, grid=(t // tq,),
        in_specs=[pl.BlockSpec((tq, Q_LORA), lambda i: (i, 2)), pl.BlockSpec((tq, LANES), lambda i: (i, 6)),
                  pl.BlockSpec((tq, LANES), lambda i: (i, 7)), tab, tab, tab,
                  full((1, Q_LORA)), full((1, KV_LORA)), full((MLA_HEADS, Q_LORA, LANES)),
                  full((MLA_HEADS, KV_LORA, LANES)), full((MLA_HEADS, KV_LORA, LANES)), full((1, LANES)), full((1, LANES))],
        out_specs=[hout, tout, hout, tout, hout], out_shape=[hshape, tshape, hshape, tshape, hshape], sem=('parallel',),
    )(proj, proj, proj, *tabs, w['q_norm'], w['kv_norm'], w['wq'], w['wk'], w['wv'], w['q_gain'], w['k_gain'])


def mla_prep_bwd(proj, tabs, w, dq, dk, dv, *, name):
    t = proj.shape[0]
    tq = _tile(t, ATT_BLK)

    def body(cq_ref, ckv_ref, kr_ref, cos_ref, sa_ref, sb_ref, qn_ref, kvn_ref, wq_ref, wk_ref, wv_ref, qg_ref, kg_ref,
             dq_ref, dk_ref, dv_ref,
             dpm_ref, cqn_ref, ckvn_ref, dqr_ref, dkraw_ref, dvb_ref, dqn_ref, dkvn_ref, dqg_ref, dkg_ref):
        cq_h, cq_r = _rms(cq_ref[...], Q_LORA)
        ckv_h, ckv_r = _rms(ckv_ref[...], KV_LORA)
        cqn = (cq_h * qn_ref[...]).astype(BF)
        ckvn = (ckv_h * kvn_ref[...]).astype(BF)
        cqn_ref[...] = cqn
        ckvn_ref[...] = ckvn
        cos, sa, sb = cos_ref[...], sa_ref[...], sb_ref[...]
        kr = kr_ref[...]
        dcqn = jnp.zeros((tq, Q_LORA), F32)
        dckvn = jnp.zeros((tq, KV_LORA), F32)
        dkrope = jnp.zeros((tq, LANES), F32)
        dqg = jnp.zeros((1, LANES), F32)
        dkg = jnp.zeros((1, LANES), F32)
        for h in range(MLA_HEADS):
            qh, qr = _rms(_dot(cqn, wq_ref[h]), QK_DIM)
            dqo = _rope_t(dq_ref[h, 0].T * ATT_SCALE, cos, sa, sb)
            dqg = dqg + _colsum(dqo * qh)
            dqraw = _rms_bwd(qh, qr, dqo * qg_ref[...], QK_DIM).astype(BF)
            dqr_ref[:, h * LANES:(h + 1) * LANES] = dqraw
            dcqn = dcqn + _dot(dqraw, wq_ref[h], NT)
            kh, krs = _rms(_dot(ckvn, wk_ref[h]) + kr, QK_DIM)
            dko = _rope_t(dk_ref[h], cos, sa, sb)
            dkg = dkg + _colsum(dko * kh)
            dkraw = _rms_bwd(kh, krs, dko * kg_ref[...], QK_DIM)
            dkrope = dkrope + dkraw
            dkraw = dkraw.astype(BF)
            dkraw_ref[:, h * LANES:(h + 1) * LANES] = dkraw
            dvb = dv_ref[h].astype(BF)
            dvb_ref[:, h * LANES:(h + 1) * LANES] = dvb
            dckvn = dckvn + _dot(dkraw, wk_ref[h], NT) + _dot(dvb, wv_ref[h], NT)
        dpm_ref[:, 0:Q_LORA] = _rms_bwd(cq_h, cq_r, dcqn * qn_ref[...], Q_LORA).astype(BF)
        dpm_ref[:, Q_LORA:Q_LORA + KV_LORA] = _rms_bwd(ckv_h, ckv_r, dckvn * kvn_ref[...], KV_LORA).astype(BF)
        dpm_ref[:, Q_LORA + KV_LORA:512] = dkrope.astype(BF)

        @pl.when(pl.program_id(0) == 0)
        def _():
            dqn_ref[...] = jnp.zeros_like(dqn_ref)
            dkvn_ref[...] = jnp.zeros_like(dkvn_ref)
            dqg_ref[...] = jnp.zeros_like(dqg_ref)
            dkg_ref[...] = jnp.zeros_like(dkg_ref)

        dqn_ref[...] += _colsum(dcqn * cq_h)
        dkvn_ref[...] += _colsum(dckvn * ckv_h)
        dqg_ref[...] += dqg
        dkg_ref[...] += dkg

    tab = pl.BlockSpec((tq, LANES), lambda i: (i, 0))
    full = lambda shape: pl.BlockSpec(shape, lambda i: (0,) * len(shape))
    hblk = pl.BlockSpec((MLA_HEADS, tq, LANES), lambda i: (0, i, 0))
    wide = pl.BlockSpec((tq, MLA_HEADS * LANES), lambda i: (i, 0))
    return _call(
        body, name=name, grid=(t // tq,),
        in_specs=[pl.BlockSpec((tq, Q_LORA), lambda i: (i, 2)), pl.BlockSpec((tq, LANES), lambda i: (i, 6)),
                  pl.BlockSpec((tq, LANES), lambda i: (i, 7)), tab, tab, tab,
                  full((1, Q_LORA)), full((1, KV_LORA)), full((MLA_HEADS, Q_LORA, LANES)),
                  full((MLA_HEADS, KV_LORA, LANES)), full((MLA_HEADS, KV_LORA, LANES)), full((1, LANES)), full((1, LANES)),
                  pl.BlockSpec((MLA_HEADS, 1, LANES, tq), lambda i: (0, i, 0, 0)), hblk, hblk],
        out_specs=[pl.BlockSpec((tq, 512), lambda i: (i, 0)),
                   pl.BlockSpec((tq, Q_LORA), lambda i: (i, 0)), pl.BlockSpec((tq, KV_LORA), lambda i: (i, 0)),
                   wide, wide, wide, full((1, Q_LORA)), full((1, KV_LORA)), full((1, LANES)), full((1, LANES))],
        out_shape=[_sds((t, 512), BF), _sds((t, Q_LORA), BF), _sds((t, KV_LORA), BF),
                   _sds((t, MLA_HEADS * LANES), BF), _sds((t, MLA_HEADS * LANES), BF), _sds((t, MLA_HEADS * LANES), BF),
                   _sds((1, Q_LORA), F32), _sds((1, KV_LORA), F32), _sds((1, LANES), F32), _sds((1, LANES), F32)],
        sem=('arbitrary',),
    )(proj, proj, proj, *tabs, w['q_norm'], w['kv_norm'], w['wq'], w['wk'], w['wv'], w['q_gain'], w['k_gain'], dq, dk, dv)


ATT_BLK = 256
ATT_SCALE = 1.0 / math.sqrt(QK_DIM)


def _overlapped(grid, make_copies):
    ids = [pl.program_id(a) for a in range(len(grid))]
    first = functools.reduce(jnp.logical_and, [i == 0 for i in ids])
    last = functools.reduce(jnp.logical_and, [i == n - 1 for i, n in zip(ids, grid)])

    @pl.when(first)
    def _():
        for cs in make_copies():
            _start_copies(cs)

    @pl.when(last)
    def _():
        for cs in make_copies():
            _wait_copies(cs)


def flash_fwd(q, kt, v, *, name, gather=()):
    t = q.shape[1]
    blk = _tile(t, ATT_BLK)
    grid = (MLA_HEADS // 2, t // blk)

    def body(q_ref, kt_ref, v_ref, *rest):
        nc = len(gather)
        srcs, (o_ref, lse_ref), dsts, sems = rest[:nc], rest[nc:nc + 2], rest[nc + 2:2 * nc + 2], rest[2 * nc + 2:]
        if nc:
            _overlapped(grid, lambda: [_copies('gather', srcs[i], dsts[i], *sems[3 * i:3 * i + 3]) for i in range(nc)])
        qi = pl.program_id(1)
        row = lax.broadcasted_iota(jnp.int32, (blk, blk), 0)
        col = lax.broadcasted_iota(jnp.int32, (blk, blk), 1)

        def block(j, carry, masked):
            out = []
            for hh in range(2):
                m, l, acc = carry[hh]
                s = _dot(q_ref[hh], kt_ref[hh, j])
                if masked:
                    s = jnp.where(col <= row, s, -jnp.inf)
                m2 = jnp.maximum(m, jnp.max(s, axis=-1, keepdims=True))
                p = jnp.exp(s - m2)
                alpha = jnp.exp(m - m2)
                rows = pl.ds(pl.multiple_of(j * blk, blk), blk)
                out.append((m2, alpha * l + jnp.sum(p, axis=-1, keepdims=True), alpha * acc + _dot(p, v_ref[hh, rows, :])))
            return tuple(out)

        init = (jnp.full((blk, 1), -jnp.inf, F32), jnp.zeros((blk, 1), F32), jnp.zeros((blk, LANES), F32))
        carry = lax.fori_loop(0, qi, lambda j, c: block(j, c, False), (init, init))
        carry = block(qi, carry, True)
        o_acc = jnp.zeros((blk, LANES), F32)
        for hh in range(2):
            m, l, acc = carry[hh]
            o_acc = o_acc + acc / l
            lse_ref[hh, 0] = jnp.broadcast_to(m + jnp.log(l), (blk, LANES)).T[0:1, :]
        o_ref[...] = o_acc

    in_specs = [pl.BlockSpec((2, blk, LANES), lambda p, i: (p, i, 0)),
                pl.BlockSpec((2, t // blk, LANES, blk), lambda p, i: (p, 0, 0, 0)),
                pl.BlockSpec((2, t, LANES), lambda p, i: (p, 0, 0))]
    out_specs = [pl.BlockSpec((blk, LANES), lambda p, i: (i, p)), pl.BlockSpec((2, 1, 1, blk), lambda p, i: (p, i, 0, 0))]
    out_shape = [_sds((t, 512), F32), _sds((MLA_HEADS, t // blk, 1, blk), F32)]
    nc = len(gather)
    return _call(body, name=name, grid=grid, in_specs=in_specs + [_ANY] * nc, out_specs=out_specs + [_ANY] * nc,
                 out_shape=out_shape + [_sds((NDEV,) + g.shape, g.dtype) for g in gather], scratch=_COMM_SCRATCH * nc,
                 sem=('arbitrary', 'arbitrary') if nc else ('parallel', 'parallel'))(q, kt, v, *gather)


def mla_out_bwd(o, dyn, g, *, name):
    t = o.shape[0]
    blk = _tile(t, ATT_BLK)

    def body(o_ref, dh_ref, g_ref, do_ref, dot_ref, delta_ref, dg_ref):
        ov = o_ref[...]
        oh, r = _rms(ov, 512)
        dh = dh_ref[...]
        do = _rms_bwd(oh, r, dh * g_ref[...], 512)
        do_ref[...] = do.astype(BF)
        dd = do * ov
        for pb in range(MLA_HEADS // 2):
            cols = slice(pb * LANES, (pb + 1) * LANES)
            dot_ref[pb, 0] = do[:, cols].T.astype(BF)
            ddt = dd[:, cols].T
            delta_ref[2 * pb, 0] = jnp.sum(ddt[0:V_DIM, :], axis=0, keepdims=True)
            delta_ref[2 * pb + 1, 0] = jnp.sum(ddt[V_DIM:LANES, :], axis=0, keepdims=True)

        @pl.when(pl.program_id(0) == 0)
        def _():
            dg_ref[...] = jnp.zeros_like(dg_ref)

        dg_ref[...] += _colsum(dh * oh)

    return _call(
        body, name=name, grid=(t // blk,),
        in_specs=[pl.BlockSpec((blk, 512), lambda i: (i, 0)), pl.BlockSpec((blk, 512), lambda i: (i, 1)),
                  pl.BlockSpec((1, 512), lambda i: (0, 0))],
        out_specs=[pl.BlockSpec((blk, 512), lambda i: (i, 0)), pl.BlockSpec((MLA_HEADS // 2, 1, LANES, blk), lambda i: (0, i, 0, 0)),
                   pl.BlockSpec((MLA_HEADS, 1, 1, blk), lambda i: (0, i, 0, 0)), pl.BlockSpec((1, 512), lambda i: (0, 0))],
        out_shape=[_sds((t, 512), BF), _sds((MLA_HEADS // 2, t // blk, LANES, blk), BF),
                   _sds((MLA_HEADS, t // blk, 1, blk), F32), _sds((1, 512), F32)],
        sem=('arbitrary',),
    )(o, dyn, g)


def flash_bwd(q, qt, k, kt, v, do, dot, lse, delta, *, name, scatter=()):
    t = q.shape[1]
    blk = _tile(t, ATT_BLK)
    nb = t // blk
    grid = (MLA_HEADS, nb)

    def body(q_ref, qt_ref, k_ref, kt_ref, v_ref, do_ref, dot_ref, lse_ref, delta_ref, *rest):
        nc = len(scatter)
        srcs, (dqt_ref, dk_ref, dv_ref), dsts, sems = rest[:nc], rest[nc:nc + 3], rest[nc + 3:2 * nc + 3], rest[2 * nc + 3:]
        if nc:
            _overlapped(grid, lambda: [_copies('scatter', srcs[i], dsts[i], *sems[3 * i:3 * i + 3]) for i in range(nc)])
        h, j = pl.program_id(0), pl.program_id(1)
        row = lax.broadcasted_iota(jnp.int32, (blk, blk), 0)
        col = lax.broadcasted_iota(jnp.int32, (blk, blk), 1)
        lane = lax.broadcasted_iota(jnp.int32, (1, LANES), 1)
        mine = (lane // V_DIM) == (h % 2)

        @pl.when(j == 0)
        def _():
            dqt_ref[...] = jnp.zeros_like(dqt_ref)

        kv, ktv, vv = k_ref[...], kt_ref[...], v_ref[...]

        def block(i, carry, masked):
            dk, dv = carry
            rows = pl.ds(pl.multiple_of(i * blk, blk), blk)
            pt = jnp.exp(_dot(kv, qt_ref[i]) - lse_ref[i])
            if masked:
                pt = jnp.where(col >= row, pt, 0.0)
            dv = dv + _dot(pt, do_ref[rows, :])
            dst = (pt * (_dot(vv, dot_ref[i]) - delta_ref[i])).astype(BF)
            dk = dk + _dot(dst, q_ref[rows, :])
            dqt_ref[i] += _dot(ktv, dst)
            return dk, dv

        zero = jnp.zeros((blk, LANES), F32)
        carry = block(j, (zero, zero), True)
        npairs = (nb - 1 - j) // 2
        carry = lax.fori_loop(0, npairs, lambda p, c: block(j + 2 + 2 * p, block(j + 1 + 2 * p, c, False), False), carry)
        dk, dv = lax.fori_loop(j + 1 + 2 * npairs, nb, lambda i, c: block(i, c, False), carry)
        dk_ref[...] = dk
        dv_ref[...] = jnp.where(mine, dv, 0.0)

    whole = pl.BlockSpec((None, t, LANES), lambda h, j: (h, 0, 0))
    wholet = pl.BlockSpec((None, nb, LANES, blk), lambda h, j: (h, 0, 0, 0))
    kvb = pl.BlockSpec((None, blk, LANES), lambda h, j: (h, j, 0))
    rowv = pl.BlockSpec((None, nb, 1, blk), lambda h, j: (h, 0, 0, 0))
    in_specs = [whole, wholet, kvb, pl.BlockSpec((None, None, LANES, blk), lambda h, j: (h, j, 0, 0)), kvb,
                pl.BlockSpec((t, LANES), lambda h, j: (0, h // 2)),
                pl.BlockSpec((None, nb, LANES, blk), lambda h, j: (h // 2, 0, 0, 0)), rowv, rowv]
    out_specs = [wholet, kvb, kvb]
    out_shape = [_sds((MLA_HEADS, nb, LANES, blk), F32), _sds((MLA_HEADS, t, LANES), F32), _sds((MLA_HEADS, t, LANES), F32)]
    args = (q, qt, k, kt, v, do, dot, lse, delta)
    nc = len(scatter)
    return _call(body, name=name, grid=grid, in_specs=in_specs + [_ANY] * nc, out_specs=out_specs + [_ANY] * nc,
                 out_shape=out_shape + [_sds(s.shape, s.dtype) for s in scatter], scratch=_COMM_SCRATCH * nc,
                 sem=('arbitrary', 'arbitrary') if nc else ('parallel', 'arbitrary'), vmem=VMEM_BIG)(*args, *scatter)


def mix_out_fwd(x, y_ssm, o, g_ssm, g_mla, w_out, *, name, tq=512):
    t = x.shape[0]
    tq = _tile(t, tq)

    def body(x_ref, ys_ref, o_ref, gs_ref, gm_ref, w_ref, x1_ref, yn_ref):
        ns = (_rms(ys_ref[...], SSM_W)[0] * gs_ref[...]).astype(BF)
        nm = (_rms(o_ref[...], 512)[0] * gm_ref[...]).astype(BF)
        yn_ref[:, 0:SSM_W] = ns
        yn_ref[:, SSM_W:D] = nm
        x1_ref[...] = x_ref[...] + _dot(ns, w_ref[0:SSM_W, :]) + _dot(nm, w_ref[SSM_W:D, :])

    row = lambda w: pl.BlockSpec((tq, w), lambda i: (i, 0))
    vec = pl.BlockSpec((1, 512), lambda i: (0, 0))
    return _call(body, name=name, grid=(t // tq,),
                 in_specs=[row(D), row(512), row(512), vec, vec, pl.BlockSpec((D, D), lambda i: (0, 0))],
                 out_specs=[row(D), row(D)], out_shape=[_sds((t, D), F32), _sds((t, D), BF)], sem=('parallel',),
                 )(x, y_ssm, o, g_ssm, g_mla, w_out)


MEM_SCALE = 1.0 / math.sqrt(MEM_HD)


def memkv_fwd(mem, g, wk, wv, kg, *, name):
    def body(m_ref, g_ref, wk_ref, wv_ref, kg_ref, mh_ref, k_ref, v_ref):
        mh = (_rms(m_ref[...], D)[0] * g_ref[...]).astype(BF)
        mh_ref[...] = mh
        for h in range(MEM_HEADS):
            cols = slice(h * LANES, (h + 1) * LANES)
            k_ref[h] = (_rms(_dot(mh, wk_ref[:, cols]), MEM_HD)[0] * kg_ref[...]).astype(BF)
            v_ref[h] = _dot(mh, wv_ref[:, cols]).astype(BF)

    return _call(body, name=name,
                 out_shape=[_sds((N_MEM, D), BF), _sds((MEM_HEADS, N_MEM, LANES), BF), _sds((MEM_HEADS, N_MEM, LANES), BF)],
                 )(mem, g, wk, wv, kg)


def memkv_bwd(mem, g, wk, wv, kg, dk, dv, *, name):
    def body(m_ref, g_ref, wk_ref, wv_ref, kg_ref, dk_ref, dv_ref, dwk_ref, dwv_ref, dkg_ref, dg_ref):
        mhat, _ = _rms(m_ref[...], D)
        mh = (mhat * g_ref[...]).astype(BF)
        lane = lax.broadcasted_iota(jnp.int32, (1, LANES), 1)
        dkg = jnp.zeros((1, LANES), F32)
        dmh = jnp.zeros((N_MEM, D), F32)
        for h in range(MEM_HEADS):
            cols = slice(h * LANES, (h + 1) * LANES)
            kh, kr = _rms(_dot(mh, wk_ref[:, cols]), MEM_HD)
            dko = dk_ref[h]
            dkg = dkg + _colsum(dko * kh)
            dkraw = _rms_bwd(kh, kr, dko * kg_ref[...], MEM_HD).astype(BF)
            dvh = jnp.where((lane // MEM_HD) == (h % 2), dv_ref[h], 0.0).astype(BF)
            dwk_ref[:, cols] = _dot(mh, dkraw, TN)
            dwv_ref[:, cols] = _dot(mh, dvh, TN)
            dmh = dmh + _dot(dkraw, wk_ref[:, cols], NT) + _dot(dvh, wv_ref[:, cols], NT)
        dkg_ref[...] = dkg
        dg_ref[...] = _colsum(dmh * mhat)

    return _call(body, name=name,
                 out_shape=[_sds((D, 512), F32), _sds((D, 512), F32), _sds((1, LANES), F32), _sds((1, D), F32)],
                 )(mem, g, wk, wv, kg, dk, dv)


def memattn_fwd(x, g, wq, qg, kh, vh, wo, *, name, tq=512):
    t = x.shape[0]
    tq = _tile(t, tq)

    def body(x_ref, g_ref, wq_ref, qg_ref, k_ref, v_ref, wo_ref, x2_ref, hn_ref):
        xv = x_ref[...]
        hn = (_rms(xv, D)[0] * g_ref[...]).astype(BF)
        hn_ref[...] = hn
        out = xv
        for pb in range(MEM_HEADS // 2):
            o = jnp.zeros((tq, LANES), F32)
            for h in (2 * pb, 2 * pb + 1):
                q = _rms(_dot(hn, wq_ref[:, h * LANES:(h + 1) * LANES]), MEM_HD)[0] * qg_ref[...]
                s = _dot(q, k_ref[h], NT) * MEM_SCALE
                p = jnp.exp(s - jnp.max(s, axis=-1, keepdims=True))
                p = p / jnp.sum(p, axis=-1, keepdims=True)
                o = o + _dot(p, v_ref[h])
            out = out + _dot(o, wo_ref[pb * LANES:(pb + 1) * LANES, :])
        x2_ref[...] = out

    full = lambda shape: pl.BlockSpec(shape, lambda i: (0,) * len(shape))
    row = pl.BlockSpec((tq, D), lambda i: (i, 0))
    return _call(body, name=name, grid=(t // tq,),
                 in_specs=[row, full((1, D)), full((D, 512)), full((1, LANES)), full((MEM_HEADS, N_MEM, LANES)),
                           full((MEM_HEADS, N_MEM, LANES)), full((MEM_HEADS * MEM_HD, D))],
                 out_specs=[row, row], out_shape=[_sds((t, D), F32), _sds((t, D), BF)], sem=('parallel',),
                 )(x, g, wq, qg, kh, vh, wo)


def memattn_bwd(x, dx2, g, wq, qg, kh, vh, wo, *, name, tq=512):
    t = x.shape[0]
    tq = _tile(t, tq)

    def body(x_ref, dx2_ref, g_ref, wq_ref, qg_ref, k_ref, v_ref, wo_ref,
             dx_ref, dxb_ref, o_ref, dqr_ref, dk_ref, dv_ref, dqg_ref, dg_ref):
        @pl.when(pl.program_id(0) == 0)
        def _():
            dk_ref[...] = jnp.zeros_like(dk_ref)
            dv_ref[...] = jnp.zeros_like(dv_ref)
            dqg_ref[...] = jnp.zeros_like(dqg_ref)
            dg_ref[...] = jnp.zeros_like(dg_ref)

        xhat, xr = _rms(x_ref[...], D)
        hn = (xhat * g_ref[...]).astype(BF)
        dx2 = dx2_ref[...]
        dx2b = dx2.astype(BF)
        dh = jnp.zeros((tq, D), F32)
        dqg = jnp.zeros((1, LANES), F32)
        for pb in range(MEM_HEADS // 2):
            do = _dot(dx2b, wo_ref[pb * LANES:(pb + 1) * LANES, :], NT).astype(BF)
            o = jnp.zeros((tq, LANES), F32)
            for h in (2 * pb, 2 * pb + 1):
                cols = slice(h * LANES, (h + 1) * LANES)
                qh, qr = _rms(_dot(hn, wq_ref[:, cols]), MEM_HD)
                qb = (qh * qg_ref[...]).astype(BF)
                s = _dot(qb, k_ref[h], NT) * MEM_SCALE
                p = jnp.exp(s - jnp.max(s, axis=-1, keepdims=True))
                p = p / jnp.sum(p, axis=-1, keepdims=True)
                pb16 = p.astype(BF)
                o = o + _dot(pb16, v_ref[h])
                dv_ref[h] += _dot(pb16, do, TN)
                dp = _dot(do, v_ref[h], NT)
                ds = (p * (dp - jnp.sum(dp * p, axis=-1, keepdims=True)) * MEM_SCALE).astype(BF)
                dk_ref[h] += _dot(ds, qb, TN)
                dqo = _dot(ds, k_ref[h])
                dqg = dqg + _colsum(dqo * qh)
                dqraw = _rms_bwd(qh, qr, dqo * qg_ref[...], MEM_HD).astype(BF)
                dqr_ref[:, cols] = dqraw
                dh = dh + _dot(dqraw, wq_ref[:, cols], NT)
            o_ref[:, pb * LANES:(pb + 1) * LANES] = o.astype(BF)
        dx = dx2 + _rms_bwd(xhat, xr, dh * g_ref[...], D)
        dx_ref[...] = dx
        dxb_ref[...] = dx.astype(BF)
        dqg_ref[...] += dqg
        dg_ref[...] += _colsum(dh * xhat)

    full = lambda shape: pl.BlockSpec(shape, lambda i: (0,) * len(shape))
    row = lambda w: pl.BlockSpec((tq, w), lambda i: (i, 0))
    return _call(body, name=name, grid=(t // tq,),
                 in_specs=[row(D), row(D), full((1, D)), full((D, 512)), full((1, LANES)), full((MEM_HEADS, N_MEM, LANES)),
                           full((MEM_HEADS, N_MEM, LANES)), full((MEM_HEADS * MEM_HD, D))],
                 out_specs=[row(D), row(D), row(256), row(512), full((MEM_HEADS, N_MEM, LANES)), full((MEM_HEADS, N_MEM, LANES)),
                            full((1, LANES)), full((1, D))],
                 out_shape=[_sds((t, D), F32), _sds((t, D), BF), _sds((t, 256), BF), _sds((t, 512), BF),
                            _sds((MEM_HEADS, N_MEM, LANES), F32), _sds((MEM_HEADS, N_MEM, LANES), F32),
                            _sds((1, LANES), F32), _sds((1, D), F32)],
                 sem=('arbitrary',))(x, dx2, g, wq, qg, kh, vh, wo)


def mlp_fwd(x, h, w1, w2, *, name, tq=1024, tf=512):
    t = x.shape[0]
    tq = _tile(t, tq)

    def body(x_ref, h_ref, w1_ref, w2_ref, o_ref):
        @pl.when(pl.program_id(1) == 0)
        def _():
            o_ref[...] = x_ref[...]

        a = jnp.maximum(_dot(h_ref[...], w1_ref[...]), 0.0)
        o_ref[...] += _dot(a * a, w2_ref[...])

    row = pl.BlockSpec((tq, D), lambda i, f: (i, 0))
    return _call(body, name=name, grid=(t // tq, D_FF // tf),
                 in_specs=[row, row, pl.BlockSpec((None, D, tf), lambda i, f: (f, 0, 0)), pl.BlockSpec((tf, D), lambda i, f: (f, 0))],
                 out_specs=row, out_shape=_sds((t, D), F32), sem=('parallel', 'arbitrary'), vmem=VMEM_BIG)(x, h, w1, w2)


def mlp_bwd(h, dx, w1, w2, *, name, tq=1024, tf=512):
    t = h.shape[0]
    tq = _tile(t, tq)

    def body(h_ref, dx_ref, w1_ref, w2_ref, dh_ref, r_ref, da_ref):
        @pl.when(pl.program_id(1) == 0)
        def _():
            dh_ref[...] = jnp.zeros_like(dh_ref)

        a = jnp.maximum(_dot(h_ref[...], w1_ref[...]), 0.0)
        r_ref[...] = (a * a).astype(BF)
        da = (_dot(dx_ref[...], w2_ref[...], NT) * (2.0 * a)).astype(BF)
        da_ref[...] = da
        dh_ref[...] += _dot(da, w1_ref[...], NT)

    row = pl.BlockSpec((tq, D), lambda i, f: (i, 0))
    act = pl.BlockSpec((tq, tf), lambda i, f: (i, f))
    return _call(body, name=name, grid=(t // tq, D_FF // tf),
                 in_specs=[row, row, pl.BlockSpec((None, D, tf), lambda i, f: (f, 0, 0)), pl.BlockSpec((tf, D), lambda i, f: (f, 0))],
                 out_specs=[row, act, act], out_shape=[_sds((t, D), F32), _sds((t, D_FF), BF), _sds((t, D_FF), BF)],
                 sem=('parallel', 'arbitrary'), vmem=VMEM_BIG)(h, dx, w1, w2)


def loss_fwd_bwd(y, target, *, name, tq=512):
    t = y.shape[0]
    tq = _tile(t, tq)

    def body(y_ref, t_ref, dy_ref, dyb_ref, l_ref):
        @pl.when(pl.program_id(0) == 0)
        def _():
            l_ref[...] = jnp.zeros_like(l_ref)

        e = y_ref[...] - t_ref[...]
        dy = e * (1.0 / D)
        dy_ref[...] = dy
        dyb_ref[...] = dy.astype(BF)
        l_ref[...] += _colsum(e * e) * (0.5 / D)

    row = pl.BlockSpec((tq, D), lambda i: (i, 0))
    return _call(body, name=name, grid=(t // tq,), in_specs=[row, row],
                 out_specs=[row, row, pl.BlockSpec((1, D), lambda i: (0, 0))],
                 out_shape=[_sds((t, D), F32), _sds((t, D), BF), _sds((1, D), F32)], sem=('arbitrary',))(y, target)


def prep_early(w):
    w_in = w['w_in']
    z = lambda r, c: jnp.zeros((r, c), w_in.dtype)
    w_in_pad = jnp.concatenate([w_in[:, :896], z(D, 64), w_in[:, 896:928], z(D, 32)], axis=1)
    wq = w['mla_w_uq'].reshape(Q_LORA, MLA_HEADS, QK_DIM).transpose(1, 0, 2)
    wq = jnp.pad(wq, ((0, 0), (0, 0), (0, LANES - QK_DIM)))
    ukv = w['mla_w_ukv'].reshape(KV_LORA, MLA_HEADS, QK_NOPE + V_DIM).transpose(1, 0, 2)
    wk = jnp.pad(ukv[:, :, :QK_NOPE], ((0, 0), (0, 0), (0, LANES - QK_NOPE)))
    vpart = ukv[:, :, QK_NOPE:]
    zv = jnp.zeros_like(vpart)
    odd = (jnp.arange(MLA_HEADS) % 2)[:, None, None] == 1
    wv = jnp.where(odd, jnp.concatenate([zv, vpart], axis=2), jnp.concatenate([vpart, zv], axis=2))
    return dict(w_in=w_in_pad, w_glu=w['ssm_w_glu'], wq=wq, wk=wk, wv=wv)


def prep_late(w):
    mq = jnp.pad(w['mem_w_q'].reshape(D, MEM_HEADS, MEM_HD), ((0, 0), (0, 0), (0, LANES - MEM_HD))).reshape(D, 512)
    mkv = w['mem_w_kv'].reshape(D, MEM_HEADS, 2 * MEM_HD)
    mk = jnp.pad(mkv[:, :, :MEM_HD], ((0, 0), (0, 0), (0, LANES - MEM_HD))).reshape(D, 512)
    mvp = mkv[:, :, MEM_HD:]
    zm = jnp.zeros_like(mvp)
    modd = (jnp.arange(MEM_HEADS) % 2)[None, :, None] == 1
    mv = jnp.where(modd, jnp.concatenate([zm, mvp], axis=2), jnp.concatenate([mvp, zm], axis=2)).reshape(D, 512)
    return dict(w_out=w['w_out'], mq=mq, mk=mk, mv=mv, mo=w['mem_w_o'], w1=w['mlp_w1'], w2=w['mlp_w2'])


def prep_small(t, s):
    row = lambda a: a.reshape(1, -1)
    pad = lambda a: jnp.pad(a, (0, LANES - a.shape[0])).reshape(1, LANES)
    out = s5_prep(t, s['ssm_lambda_re'], s['ssm_lambda_im'], s['ssm_log_step'], s['ssm_b_re'], s['ssm_b_im'],
                  s['ssm_c_re'], s['ssm_c_im'])
    out.update(d=row(s['ssm_d']), norm_mix=row(s['norm_mix']), b_glu=row(s['ssm_b_glu']), q_norm=row(s['mla_q_norm']),
               kv_norm=row(s['mla_kv_norm']), q_gain=pad(s['mla_q_gain']), k_gain=pad(s['mla_k_gain']),
               g_ssm=row(s['out_norm_ssm']), g_mla=row(s['out_norm_mla']), norm_mem_q=row(s['norm_mem_q']),
               norm_mem_kv=row(s['norm_mem_kv']), mem_q_gain=pad(s['mem_q_gain']), mem_k_gain=pad(s['mem_k_gain']),
               norm_mlp=row(s['norm_mlp']))
    return out


def _perm(a):
    t, c = a.shape
    return a.reshape(SEGS, t // SEGS, c).transpose(1, 0, 2).reshape(t, c)


def _unperm(a):
    t, c = a.shape
    return a.reshape(t // SEGS, SEGS, c).transpose(1, 0, 2).reshape(t, c)


def layer_fwd(l, x, mem, tabs, plan, ws):
    n = lambda s: f'l{l}_{s}'
    wb = prep_early(plan.early(l))
    h1 = rmsnorm_fwd(x, ws['norm_mix'], name=n('norm_mix'))
    proj = mm(h1, wb['w_in'], 'nn', name=n('w_in'))
    ypre_p = s5_fwd(_perm(proj[:, :SSM_W]), ws, name=n('s5'))
    ypre = _unperm(ypre_p)
    y_ssm = glu_fwd(ypre, wb['w_glu'], ws['b_glu'], name=n('glu'))
    mw = dict(q_norm=ws['q_norm'], kv_norm=ws['kv_norm'], wq=wb['wq'], wk=wb['wk'], wv=wb['wv'],
              q_gain=ws['q_gain'], k_gain=ws['k_gain'])
    q, qt, k, kt, v = mla_prep_fwd(proj, tabs, mw, name=n('mla_prep'))
    o, lse, *gathered = flash_fwd(q, kt, v, name=n('flash'), gather=plan.gather_src(l))
    plan.gathered(l, gathered)
    wb.update(prep_late(plan.late(l)))
    x1, yn = mix_out_fwd(x, y_ssm, o, ws['g_ssm'], ws['g_mla'], wb['w_out'], name=n('mix_out'))
    mh, kh, vh = memkv_fwd(mem, ws['norm_mem_kv'], wb['mk'], wb['mv'], ws['mem_k_gain'], name=n('memkv'))
    x2, h2 = memattn_fwd(x1, ws['norm_mem_q'], wb['mq'], ws['mem_q_gain'], kh, vh, wb['mo'], name=n('memattn'))
    h3 = rmsnorm_fwd(x2, ws['norm_mlp'], name=n('norm_mlp'))
    x3 = mlp_fwd(x2, h3, wb['w1'], wb['w2'], name=n('mlp'))
    saved = dict(x=x, h1=h1, proj=proj, ypre=ypre, y_ssm=y_ssm, q=q, qt=qt, k=k, kt=kt, v=v, o=o, lse=lse, x1=x1, yn=yn,
                 kh=kh, vh=vh, x2=x2, h2=h2, h3=h3, mw=mw)
    return x3, wb, saved


def layer_bwd(l, dx3, dx3b, mem, tabs, plan, wb, ws, sv):
    n = lambda s: f'l{l}_{s}_bwd'
    gb, gs = {}, {}
    structs = lambda names: {k: _sds(plan.shapes[k], F32) for k in names}
    dh3, r, da = mlp_bwd(sv['h3'], dx3b, wb['w1'], wb['w2'], name=n('mlp'))
    gb['w1'] = mm(sv['h3'], da, 'tn', name=n('w1'), slots=NDEV)
    gb['w2'] = mm(r, dx3b, 'tn', name=n('w2'))
    dx2, dx2b, gs['norm_mlp'] = rmsnorm_bwd(sv['x2'], ws['norm_mlp'], dh3, dx3, name=n('norm_mlp'))
    dx1, dx1b, o_mem, dqr_mem, dkh, dvh, gs['mem_q_gain'], gs['norm_mem_q'] = memattn_bwd(
        sv['x1'], dx2, ws['norm_mem_q'], wb['mq'], ws['mem_q_gain'], sv['kh'], sv['vh'], wb['mo'], name=n('memattn'))
    gb['mo'] = mm(o_mem, dx2b, 'tn', name=n('mo'))
    gb['mq'] = mm(sv['h2'], dqr_mem, 'tn', name=n('mq'))
    gb['mk'], gb['mv'], gs['mem_k_gain'], gs['norm_mem_kv'] = memkv_bwd(
        mem, ws['norm_mem_kv'], wb['mk'], wb['mv'], ws['mem_k_gain'], dkh, dvh, name=n('memkv'))
    dyn = mm(dx1b, wb['w_out'], 'nt', name=n('w_out_dx'))
    gb['w_out'] = mm(sv['yn'], dx1b, 'tn', name=n('w_out'))
    dy_ssm, _, gs['g_ssm'] = rmsnorm_bwd(sv['y_ssm'], ws['g_ssm'], dyn, None, name=n('out_norm_ssm'), col=0)
    do, dot, delta, gs['g_mla'] = mla_out_bwd(sv['o'], dyn, ws['g_mla'], name=n('out_norm_mla'))
    late = {k: gb.pop(k) for k in ('w_out', 'mq', 'mk', 'mv', 'mo', 'w1', 'w2')}
    plan.late_grads(l, jax.linear_transpose(prep_late, structs(BIG_LATE))(late)[0])
    dq, dk, dv, *received = flash_bwd(sv['q'], sv['qt'], sv['k'], sv['kt'], sv['v'], do, dot, sv['lse'], delta,
                                      name=n('flash'), scatter=plan.scatter_src(l))
    plan.scattered(l, received)
    (dproj_m, cqn, ckvn, dqr, dkr, dvb, gs['q_norm'], gs['kv_norm'], gs['q_gain'], gs['k_gain']) = mla_prep_bwd(
        sv['proj'], tabs, sv['mw'], dq, dk, dv, name=n('mla_prep'))
    by_head = lambda g: g.reshape(g.shape[0], MLA_HEADS, LANES).transpose(1, 0, 2)
    gb['wq'] = by_head(mm(cqn, dqr, 'tn', name=n('wq')))
    gb['wk'] = by_head(mm(ckvn, dkr, 'tn', name=n('wk')))
    gb['wv'] = by_head(mm(ckvn, dvb, 'tn', name=n('wv')))
    dypre, yg, dz, gs['b_glu'] = glu_bwd(sv['ypre'], dy_ssm, wb['w_glu'], ws['b_glu'], name=n('glu'))
    gb['w_glu'] = mm(yg, dz, 'tn', name=n('w_glu'))
    u_p = _perm(sv['proj'][:, :SSM_W])
    du_p, gs['ar'], gs['ai'], gs['bre'], gs['bim'], gs['cre'], gs['cim'], gs['d'] = s5_bwd(u_p, _perm(dypre), ws, name=n('s5'))
    dprojb = jnp.concatenate([_unperm(du_p), dproj_m], axis=1)
    dh1 = mm(dprojb, wb['w_in'], 'nt', name=n('w_in_dx'))
    gb['w_in'] = mm(sv['h1'], dprojb, 'tn', name=n('w_in'))
    dx0, dx0b, gs['norm_mix'] = rmsnorm_bwd(sv['x'], ws['norm_mix'], dh1, dx1, name=n('norm_mix'))
    plan.early_grads(l, jax.linear_transpose(prep_early, structs(BIG_EARLY))(gb)[0])
    return dx0, dx0b, gs


def local_step(x, mem, positions, target, small, plan):
    t = x.shape[0]
    tabs = rope_tables(positions)
    layers = []
    for l in range(DEPTH):
        ws, small_vjp = jax.vjp(functools.partial(prep_small, t), {k: small[k][l] for k in SMALL})
        x, wb, sv = layer_fwd(l, x, mem, tabs, plan, ws)
        layers.append((wb, ws, small_vjp, sv))
    dx, dxb, lcols = loss_fwd_bwd(x, target, name='loss')
    loss = jnp.sum(lcols)
    gsmall = [None] * DEPTH
    for l in reversed(range(DEPTH)):
        wb, ws, small_vjp, sv = layers[l]
        dx, dxb, gs = layer_bwd(l, dx, dxb, mem, tabs, plan, wb, ws, sv)
        gs['pr'], gs['pi'] = jnp.zeros_like(ws['pr']), jnp.zeros_like(ws['pi'])
        gsmall[l] = small_vjp(gs)[0]
    return loss, dx, gsmall


class ExchangePlan:
    def __init__(self, shard_shapes, mine, first_early):
        self.shapes = {k: (s[1] * (NDEV if BIG_AXIS[k] == 1 else 1), s[2] * (NDEV if BIG_AXIS[k] == 2 else 1))
                       for k, s in shard_shapes.items()}
        self.shard = {k: s[1:] for k, s in shard_shapes.items()}
        self.shapes['mlp_w1'] = (NDEV,) + self.shard['mlp_w1']
        self.mine = mine
        self.w_early = {0: first_early}
        self.w_late = {}
        self.g_late, self.g_early = {}, {}
        self.r_late, self.r_early = {}, {}

    def _unpack(self, g, names):
        out, r0 = {}, 0
        for k in names:
            nr = math.prod(self.shard[k]) // D
            s = g[:, r0:r0 + nr]
            out[k] = (s.reshape(self.shapes[k]) if k == 'mlp_w1'
                      else _from_slots(s.reshape(NDEV, -1), (1,) + self.shard[k], BIG_AXIS[k])[0])
            r0 += nr
        return out

    def _pack(self, g, names, rows):
        slots = jnp.concatenate([g[k].reshape(NDEV, -1) if k == 'mlp_w1' else _to_slots(g[k][None], BIG_AXIS[k])
                                 for k in names], axis=1)
        return jnp.pad(slots, ((0, 0), (0, rows * D - slots.shape[1]))).astype(BF).reshape(NDEV, rows, D)

    def early(self, l):
        return self._unpack(self.w_early.pop(l), BIG_EARLY)

    def late(self, l):
        return self._unpack(self.w_late.pop(l), BIG_LATE)

    def gather_src(self, l):
        src = [self.mine[l, :LATE_ROWS]]
        if l + 1 < DEPTH:
            src.append(self.mine[l + 1, LATE_ROWS:])
        return tuple(src)

    def gathered(self, l, res):
        self.w_late[l] = res[0]
        if l + 1 < DEPTH:
            self.w_early[l + 1] = res[1]

    def late_grads(self, l, g):
        self.g_late[l] = self._pack(g, BIG_LATE, LATE_ROWS)

    def early_grads(self, l, g):
        self.g_early[l] = self._pack(g, BIG_EARLY, LAYER_ROWS - LATE_ROWS)

    def scatter_src(self, l):
        src = [self.g_late.pop(l)]
        if l + 1 < DEPTH:
            src.append(self.g_early.pop(l + 1))
        return tuple(src)

    def scattered(self, l, res):
        self.r_late[l] = res[0]
        if l + 1 < DEPTH:
            self.r_early[l + 1] = res[1]


def _peer(k):
    x, y, c = lax.axis_index('x'), lax.axis_index('y'), lax.axis_index('c')
    px, py, pc = x ^ ((k >> 2) & 1), y ^ ((k >> 1) & 1), c ^ (k & 1)
    return (px, py, pc), 4 * px + 2 * py + pc


def _copies(kind, src_ref, dst_ref, send_sems, recv_sems, loc_sem):
    _, me = _peer(0)
    src = (lambda p: src_ref.at[p]) if kind == 'scatter' else (lambda p: src_ref)
    local = pltpu.make_async_copy(src(me), dst_ref.at[me], loc_sem)
    sends, recvs = [], []
    for k in range(1, NDEV):
        dev, p = _peer(k)
        for slot, lst in ((me, sends), (p, recvs)):
            lst.append(pltpu.make_async_remote_copy(src_ref=src(p), dst_ref=dst_ref.at[slot], send_sem=send_sems.at[k - 1],
                                                    recv_sem=recv_sems.at[k - 1], device_id=dev,
                                                    device_id_type=pl.DeviceIdType.MESH))
    return local, sends, recvs


def _start_copies(cs):
    local, sends, _ = cs
    local.start()
    for cp in sends:
        cp.start()


def _wait_copies(cs):
    local, sends, recvs = cs
    for cp in sends:
        cp.wait_send()
    for cp in recvs:
        cp.wait_recv()
    local.wait()


_COMM_SCRATCH = (pltpu.SemaphoreType.DMA((NDEV - 1,)), pltpu.SemaphoreType.DMA((NDEV - 1,)), pltpu.SemaphoreType.DMA(()))
_ANY = pl.BlockSpec(memory_space=pl.ANY)


def exchange(scatters, gathers, *, name):
    ins = list(scatters) + list(gathers)
    kinds = ['scatter'] * len(scatters) + ['gather'] * len(gathers)
    n_in = len(ins)
    outs = [_sds(a.shape, a.dtype) for a in scatters] + [_sds((NDEV,) + b.shape, b.dtype) for b in gathers]

    def body(*refs):
        in_refs, out_refs, sems = refs[:n_in], refs[n_in:2 * n_in], refs[2 * n_in:]
        sets = [_copies(kind, in_refs[i], out_refs[i], *sems[3 * i:3 * i + 3]) for i, kind in enumerate(kinds)]
        for cs in sets:
            _start_copies(cs)
        for cs in sets:
            _wait_copies(cs)

    return pl.pallas_call(body, name=name, in_specs=[_ANY] * n_in, out_specs=[_ANY] * n_in, out_shape=outs,
                          scratch_shapes=list(_COMM_SCRATCH * n_in))(*ins)


def adamw(w, m, v, g8, *, name, tr):
    r = w.shape[0]
    c1 = 1.0 / (1.0 - ADAM_B1 ** ADAM_STEP)
    c2 = 1.0 / (1.0 - ADAM_B2 ** ADAM_STEP)

    def body(w_ref, m_ref, v_ref, g_ref, go_ref, d_ref, mo_ref, vo_ref):
        g = g_ref[0].astype(F32)
        for i in range(1, NDEV):
            g = g + g_ref[i].astype(F32)
        m_new = ADAM_B1 * m_ref[...] + (1.0 - ADAM_B1) * g
        v_new = ADAM_B2 * v_ref[...] + (1.0 - ADAM_B2) * (g * g)
        go_ref[...] = g
        mo_ref[...] = m_new
        vo_ref[...] = v_new
        d_ref[...] = -ADAM_LR * ((m_new * c1) / (jnp.sqrt(v_new * c2) + ADAM_EPS) + ADAM_WD * w_ref[...])

    row = pl.BlockSpec((tr, D), lambda i: (i, 0))
    return _call(body, name=name, grid=(r // tr,),
                 in_specs=[row, row, row, pl.BlockSpec((NDEV, tr, D), lambda i: (0, i, 0))],
                 out_specs=[row] * 4, out_shape=[_sds((r, D), F32)] * 4, sem=('parallel',), vmem=VMEM_BIG)(w, m, v, g8)


def _flat_rows(parts, rows):
    flat = jnp.concatenate([p.reshape(-1) for p in parts])
    return jnp.pad(flat, (0, rows * D - flat.shape[0])).reshape(rows, D)


def _unflat(flat2d, shapes):
    flat = flat2d.reshape(-1)
    out, off = [], 0
    for s in shapes:
        n = math.prod(s)
        out.append(flat[off:off + n].reshape(s))
        off += n
    return out


def _to_slots(g, axis):
    l, r, c = g.shape
    if axis == 1:
        return g.reshape(l, NDEV, r // NDEV, c).transpose(1, 0, 2, 3).reshape(NDEV, -1)
    return g.reshape(l, r, NDEV, c // NDEV).transpose(2, 0, 1, 3).reshape(NDEV, -1)


def _from_slots(s, shard_shape, axis):
    l, r, c = shard_shape
    s = s.reshape(NDEV, l, r, c)
    if axis == 1:
        return s.transpose(1, 0, 2, 3).reshape(l, NDEV * r, c)
    return s.transpose(1, 2, 0, 3).reshape(l, r, NDEV * c)


LATE_ROWS = 1280
LAYER_ROWS = 1536
BIG_ROWS = DEPTH * LAYER_ROWS
SMALL_ROWS = 640


def kernel(x, mem, positions, norm_mix, w_in, ssm_lambda_re, ssm_lambda_im, ssm_log_step, ssm_b_re, ssm_b_im, ssm_c_re, ssm_c_im, ssm_d, ssm_w_glu, ssm_b_glu, mla_q_norm, mla_w_uq, mla_kv_norm, mla_w_ukv, mla_q_gain, mla_k_gain, out_norm_ssm, out_norm_mla, w_out, norm_mem_q, norm_mem_kv, mem_w_q, mem_w_kv, mem_q_gain, mem_k_gain, mem_w_o, norm_mlp, mlp_w1, mlp_w2, loss_target, m_norm_mix, m_w_in, m_ssm_lambda_re, m_ssm_lambda_im, m_ssm_log_step, m_ssm_b_re, m_ssm_b_im, m_ssm_c_re, m_ssm_c_im, m_ssm_d, m_ssm_w_glu, m_ssm_b_glu, m_mla_q_norm, m_mla_w_uq, m_mla_kv_norm, m_mla_w_ukv, m_mla_q_gain, m_mla_k_gain, m_out_norm_ssm, m_out_norm_mla, m_w_out, m_norm_mem_q, m_norm_mem_kv, m_mem_w_q, m_mem_w_kv, m_mem_q_gain, m_mem_k_gain, m_mem_w_o, m_norm_mlp, m_mlp_w1, m_mlp_w2, v_norm_mix, v_w_in, v_ssm_lambda_re, v_ssm_lambda_im, v_ssm_log_step, v_ssm_b_re, v_ssm_b_im, v_ssm_c_re, v_ssm_c_im, v_ssm_d, v_ssm_w_glu, v_ssm_b_glu, v_mla_q_norm, v_mla_w_uq, v_mla_kv_norm, v_mla_w_ukv, v_mla_q_gain, v_mla_k_gain, v_out_norm_ssm, v_out_norm_mla, v_w_out, v_norm_mem_q, v_norm_mem_kv, v_mem_w_q, v_mem_w_kv, v_mem_q_gain, v_mem_k_gain, v_mem_w_o, v_norm_mlp, v_mlp_w1, v_mlp_w2):
    wvals = (norm_mix, w_in, ssm_lambda_re, ssm_lambda_im, ssm_log_step, ssm_b_re, ssm_b_im, ssm_c_re, ssm_c_im, ssm_d, ssm_w_glu, ssm_b_glu, mla_q_norm, mla_w_uq, mla_kv_norm, mla_w_ukv, mla_q_gain, mla_k_gain, out_norm_ssm, out_norm_mla, w_out, norm_mem_q, norm_mem_kv, mem_w_q, mem_w_kv, mem_q_gain, mem_k_gain, mem_w_o, norm_mlp, mlp_w1, mlp_w2)
    mvals = (m_norm_mix, m_w_in, m_ssm_lambda_re, m_ssm_lambda_im, m_ssm_log_step, m_ssm_b_re, m_ssm_b_im, m_ssm_c_re, m_ssm_c_im, m_ssm_d, m_ssm_w_glu, m_ssm_b_glu, m_mla_q_norm, m_mla_w_uq, m_mla_kv_norm, m_mla_w_ukv, m_mla_q_gain, m_mla_k_gain, m_out_norm_ssm, m_out_norm_mla, m_w_out, m_norm_mem_q, m_norm_mem_kv, m_mem_w_q, m_mem_w_kv, m_mem_q_gain, m_mem_k_gain, m_mem_w_o, m_norm_mlp, m_mlp_w1, m_mlp_w2)
    vvals = (v_norm_mix, v_w_in, v_ssm_lambda_re, v_ssm_lambda_im, v_ssm_log_step, v_ssm_b_re, v_ssm_b_im, v_ssm_c_re, v_ssm_c_im, v_ssm_d, v_ssm_w_glu, v_ssm_b_glu, v_mla_q_norm, v_mla_w_uq, v_mla_kv_norm, v_mla_w_ukv, v_mla_q_gain, v_mla_k_gain, v_out_norm_ssm, v_out_norm_mla, v_w_out, v_norm_mem_q, v_norm_mem_kv, v_mem_w_q, v_mem_w_kv, v_mem_q_gain, v_mem_k_gain, v_mem_w_o, v_norm_mlp, v_mlp_w1, v_mlp_w2)
    w = dict(zip(WEIGHTS, wvals))
    m = dict(zip(WEIGHTS, mvals))
    v = dict(zip(WEIGHTS, vvals))

    shard_shapes = {k: w[k].shape for k in BIG}
    layer_shapes = [shard_shapes[k][1:] for k in BIG]

    def layer_flat(parts):
        flat = jnp.concatenate([p.reshape(DEPTH, -1) for p in parts], axis=1)
        return jnp.pad(flat, ((0, 0), (0, LAYER_ROWS * D - flat.shape[1]))).reshape(DEPTH, LAYER_ROWS, D)

    mine = layer_flat([w[k].astype(BF) for k in BIG])
    first, = exchange([], [mine[0, LATE_ROWS:]], name='gather_early0')
    plan = ExchangePlan(shard_shapes, mine, first)
    small = {k: w[k] for k in SMALL}
    loss, grad_x, gsmall = local_step(x[0], mem[0], positions[0], loss_target[0], small, plan)
    gs_full = [jnp.stack([gsmall[l][k] for l in range(DEPTH)]) for k in SMALL]
    small_flat = _flat_rows(gs_full, SMALL_ROWS).astype(BF)
    plan.r_early[0], g8_small, losses = exchange([plan.g_early.pop(0)], [small_flat, jnp.full((8, LANES), loss, F32)],
                                                 name='exchange_last')
    loss_all = jnp.sum(losses[:, 0, 0])
    g8_big = jnp.concatenate([r[l] for l in range(DEPTH) for r in (plan.r_late, plan.r_early)], axis=1)

    small_shapes = [w[k].shape for k in SMALL]
    flat_big = lambda d: layer_flat([d[k] for k in BIG]).reshape(BIG_ROWS, D)
    gb, db, mb, vb = adamw(flat_big(w), flat_big(m), flat_big(v), g8_big, name='adamw_big', tr=256)
    gs, ds, ms, vs = adamw(_flat_rows([w[k] for k in SMALL], SMALL_ROWS), _flat_rows([m[k] for k in SMALL], SMALL_ROWS),
                           _flat_rows([v[k] for k in SMALL], SMALL_ROWS), g8_small, name='adamw_small', tr=128)

    def unflat_big(fb):
        fb, out, r0 = fb.reshape(DEPTH, LAYER_ROWS, D), [], 0
        for k, shp in zip(BIG, layer_shapes):
            nr = math.prod(shp) // D
            out.append(fb[:, r0:r0 + nr].reshape(shard_shapes[k]))
            r0 += nr
        return out

    res = {}
    for tag, fb, fs in (('g', gb, gs), ('d', db, ds), ('m', mb, ms), ('v', vb, vs)):
        res[tag] = dict(zip(BIG, unflat_big(fb)))
        res[tag].update(zip(SMALL, _unflat(fs, small_shapes)))
    return (loss_all, grad_x[None], *[res['g'][k] for k in WEIGHTS], *[res['d'][k] for k in WEIGHTS],
            *[res['m'][k] for k in WEIGHTS], *[res['v'][k] for k in WEIGHTS])
```

```python
import functools
import math

import jax
import jax.numpy as jnp
from jax import lax
from jax.experimental import pallas as pl
from jax.experimental.pallas import tpu as pltpu

F32 = jnp.float32
BF = jnp.bfloat16

D = 1024
DEPTH = 4
N_MEM = 256
MEM_HEADS = 4
MEM_HD = 64
SSM_W = 512
SSM_G = 32
SSM_H = 16
SSM_P = 64
MLA_HEADS = 8
QK_NOPE = 64
QK_ROPE = 32
QK_DIM = 96
V_DIM = 64
Q_LORA = 256
KV_LORA = 128
ROPE_THETA = 10000.0
D_FF = 4096
IN_COLS = 928
EPS = 1e-6
NDEV = 8
LANES = 128
SEGS = 16
S5_LW = 256
S5_NHB = (SSM_G * SSM_P) // S5_LW
ADAM_LR = 0.001
ADAM_B1 = 0.9
ADAM_B2 = 0.999
ADAM_EPS = 1e-08
ADAM_WD = 0.01
ADAM_STEP = 10
VMEM_BIG = 56 * 1024 * 1024

NN = (((1,), (0,)), ((), ()))
NT = (((1,), (1,)), ((), ()))
TN = (((0,), (0,)), ((), ()))

BIG_LATE = ('w_out', 'mem_w_q', 'mem_w_kv', 'mem_w_o', 'mlp_w1', 'mlp_w2')
BIG_EARLY = ('w_in', 'ssm_w_glu', 'mla_w_uq', 'mla_w_ukv')
BIG = BIG_LATE + BIG_EARLY
BIG_AXIS = {'w_in': 1, 'ssm_w_glu': 1, 'mla_w_uq': 2, 'mla_w_ukv': 2, 'w_out': 1, 'mem_w_q': 1, 'mem_w_kv': 1,
            'mem_w_o': 2, 'mlp_w1': 2, 'mlp_w2': 1}
SMALL = ('norm_mix', 'ssm_lambda_re', 'ssm_lambda_im', 'ssm_log_step', 'ssm_b_re', 'ssm_b_im', 'ssm_c_re', 'ssm_c_im',
         'ssm_d', 'ssm_b_glu', 'mla_q_norm', 'mla_kv_norm', 'mla_q_gain', 'mla_k_gain', 'out_norm_ssm', 'out_norm_mla',
         'norm_mem_q', 'norm_mem_kv', 'mem_q_gain', 'mem_k_gain', 'norm_mlp')
WEIGHTS = ('norm_mix', 'w_in', 'ssm_lambda_re', 'ssm_lambda_im', 'ssm_log_step', 'ssm_b_re', 'ssm_b_im', 'ssm_c_re',
           'ssm_c_im', 'ssm_d', 'ssm_w_glu', 'ssm_b_glu', 'mla_q_norm', 'mla_w_uq', 'mla_kv_norm', 'mla_w_ukv',
           'mla_q_gain', 'mla_k_gain', 'out_norm_ssm', 'out_norm_mla', 'w_out', 'norm_mem_q', 'norm_mem_kv', 'mem_w_q',
           'mem_w_kv', 'mem_q_gain', 'mem_k_gain', 'mem_w_o', 'norm_mlp', 'mlp_w1', 'mlp_w2')


def _call(body, *, name, out_shape, grid=(), in_specs=None, out_specs=None, scratch=(), sem=None, vmem=None):
    params = {}
    if sem is not None:
        params['dimension_semantics'] = sem
    if vmem is not None:
        params['vmem_limit_bytes'] = vmem
    specs = {} if in_specs is None else dict(grid=grid, in_specs=in_specs, out_specs=out_specs)
    return pl.pallas_call(body, name=name, out_shape=out_shape, scratch_shapes=list(scratch),
                          compiler_params=pltpu.CompilerParams(**params), **specs)


def _sds(shape, dtype):
    return jax.ShapeDtypeStruct(shape, dtype)


def _dot(a, b, dims=NN):
    return lax.dot_general(a.astype(BF), b.astype(BF), dims, preferred_element_type=F32)


def _split(a):
    hi = a.astype(BF)
    return hi, (a - hi.astype(F32)).astype(BF)


def _dot3(a, b, dims=NN):
    ah, al = _split(a)
    bh, bl = _split(b)
    d = lambda p, q: lax.dot_general(p, q, dims, preferred_element_type=F32)
    return d(ah, bh) + (d(ah, bl) + d(al, bh))


_sdot = _dot


def _rms(x, n):
    r = lax.rsqrt(jnp.sum(x * x, axis=-1, keepdims=True) * (1.0 / n) + EPS)
    return x * r, r


def _rms_bwd(xhat, r, dxhat, n):
    return r * (dxhat - xhat * (jnp.sum(dxhat * xhat, axis=-1, keepdims=True) * (1.0 / n)))


def _colsum(a):
    return jnp.sum(a, axis=0, keepdims=True)


def _tile(t, want):
    return min(t, want)


def _bidx(nb):
    return (lambda b: b) if nb > 1 else (lambda b: 0)


def mm(a, b, mode, *, name, out_dtype=F32, tm=1024, tn=1024, slots=0):
    squeeze = a.ndim == 2 and b.ndim == 2
    a = a[None] if a.ndim == 2 else a
    b = b[None] if b.ndim == 2 else b
    nb = max(a.shape[0], b.shape[0])
    ab, bb = _bidx(a.shape[0]), _bidx(b.shape[0])
    if mode in ('nn', 'nt'):
        m, k = a.shape[1:]
        n = b.shape[2] if mode == 'nn' else b.shape[1]
        tm, tn = _tile(m, tm), _tile(n, tn)
        dims = NN if mode == 'nn' else NT

        def body(a_ref, b_ref, o_ref):
            o_ref[...] = _dot(a_ref[...], b_ref[...], dims).astype(o_ref.dtype)

        bspec = (pl.BlockSpec((None, k, tn), lambda bi, i, j: (bb(bi), 0, j)) if mode == 'nn'
                 else pl.BlockSpec((None, tn, k), lambda bi, i, j: (bb(bi), j, 0)))
        out = _call(body, name=name, grid=(nb, m // tm, n // tn),
                    in_specs=[pl.BlockSpec((None, tm, k), lambda bi, i, j: (ab(bi), i, 0)), bspec],
                    out_specs=pl.BlockSpec((None, tm, tn), lambda bi, i, j: (bi, i, j)),
                    out_shape=_sds((nb, m, n), out_dtype), sem=('parallel', 'parallel', 'parallel'), vmem=VMEM_BIG)(a, b)
    else:
        k, m = a.shape[1:]
        n = b.shape[2]
        tm, tn, tk = _tile(m, 1024), _tile(n, 1024), _tile(k, 512)
        per = 1
        if slots:
            ts = n // slots
            per = tn // ts
            out_spec, out_shape = pl.BlockSpec((per, tm, ts), lambda bi, i, j, kk: (j, i, 0)), _sds((slots, m, ts), F32)
        else:
            out_spec, out_shape = pl.BlockSpec((None, tm, tn), lambda bi, i, j, kk: (bi, i, j)), _sds((nb, m, n), F32)

        def body(a_ref, b_ref, o_ref):
            @pl.when(pl.program_id(3) == 0)
            def _():
                o_ref[...] = jnp.zeros_like(o_ref)

            res = _dot(a_ref[...], b_ref[...], TN)
            if slots:
                for s in range(per):
                    o_ref[s] += res[:, s * ts:(s + 1) * ts]
            else:
                o_ref[...] += res

        out = _call(body, name=name, grid=(nb, m // tm, n // tn, k // tk),
                    in_specs=[pl.BlockSpec((None, tk, tm), lambda bi, i, j, kk: (ab(bi), kk, i)),
                              pl.BlockSpec((None, tk, tn), lambda bi, i, j, kk: (bb(bi), kk, j))],
                    out_specs=out_spec, out_shape=out_shape,
                    sem=('parallel', 'parallel', 'parallel', 'arbitrary'), vmem=VMEM_BIG)(a, b)
    return out[0] if squeeze and not slots else out


def rmsnorm_fwd(x, g, *, name, tq=512):
    t, d = x.shape
    tq = _tile(t, tq)

    def body(x_ref, g_ref, o_ref):
        xh, _ = _rms(x_ref[...], d)
        o_ref[...] = (xh * g_ref[...]).astype(o_ref.dtype)

    return _call(body, name=name, grid=(t // tq,),
                 in_specs=[pl.BlockSpec((tq, d), lambda i: (i, 0)), pl.BlockSpec((1, d), lambda i: (0, 0))],
                 out_specs=pl.BlockSpec((tq, d), lambda i: (i, 0)), out_shape=_sds((t, d), BF), sem=('parallel',))(x, g)


def rmsnorm_bwd(x, g, dh, dres, *, name, col=0, tq=512):
    t, d = x.shape
    tq = _tile(t, tq)
    has_res = dres is not None

    def body(*refs):
        if has_res:
            x_ref, g_ref, dh_ref, dres_ref, dx_ref, dxb_ref, dg_ref = refs
        else:
            x_ref, g_ref, dh_ref, dx_ref, dxb_ref, dg_ref = refs
        xh, r = _rms(x_ref[...], d)
        dh_ = dh_ref[...].astype(F32)
        dx = _rms_bwd(xh, r, dh_ * g_ref[...], d)
        if has_res:
            dx = dx + dres_ref[...]
        dx_ref[...] = dx
        dxb_ref[...] = dx.astype(BF)

        @pl.when(pl.program_id(0) == 0)
        def _():
            dg_ref[...] = jnp.zeros_like(dg_ref)

        dg_ref[...] += _colsum(dh_ * xh)

    in_specs = [pl.BlockSpec((tq, d), lambda i: (i, 0)), pl.BlockSpec((1, d), lambda i: (0, 0)),
                pl.BlockSpec((tq, d), lambda i: (i, col))]
    args = [x, g, dh]
    if has_res:
        in_specs.append(pl.BlockSpec((tq, d), lambda i: (i, 0)))
        args.append(dres)
    row = pl.BlockSpec((tq, d), lambda i: (i, 0))
    return _call(body, name=name, grid=(t // tq,), in_specs=in_specs,
                 out_specs=[row, row, pl.BlockSpec((1, d), lambda i: (0, 0))],
                 out_shape=[_sds((t, d), F32), _sds((t, d), BF), _sds((1, d), F32)], sem=('arbitrary',))(*args)


def _cmul(ar, ai, xr, xi):
    return ar * xr - ai * xi, ar * xi + ai * xr


def _seg_carries(er, ei, pr, pi, reverse):
    lw = er.shape[1]
    zero = jnp.zeros((1, lw), F32)
    order = range(SEGS - 1, -1, -1) if reverse else range(SEGS)
    cin_r, cin_i = [None] * SEGS, [None] * SEGS
    tr, ti = zero, zero
    for j in order:
        cin_r[j], cin_i[j] = tr, ti
        mr, mi = _cmul(pr, pi, tr, ti)
        tr, ti = er[j:j + 1, :] + mr, ei[j:j + 1, :] + mi
    return jnp.concatenate(cin_r, axis=0), jnp.concatenate(cin_i, axis=0)


def _s5_chunk(t):
    return _tile(t, 512)


def s5_fwd(u_p, prm, *, name):
    t = u_p.shape[0]
    ch = _s5_chunk(t)
    nch, steps = t // ch, ch // SEGS
    lw = S5_LW

    def body(u_ref, ar_ref, ai_ref, pr_ref, pi_ref, bre_ref, bim_ref, cre_ref, cim_ref, d_ref, y_ref, bur, bui):
        hb = pl.program_id(0)
        ar = jnp.broadcast_to(ar_ref[0], (SEGS, lw))
        ai = jnp.broadcast_to(ai_ref[0], (SEGS, lw))

        def rows_of(c):
            return pl.ds(pl.multiple_of(c * ch, ch), ch)

        @pl.loop(0, nch)
        def _(c):
            u = u_ref[rows_of(c), :]
            bur[rows_of(c), :] = _sdot(u, bre_ref[0])
            bui[rows_of(c), :] = _sdot(u, bim_ref[0])

        def scan(carry, store):
            def step(i, s):
                r0 = pl.multiple_of(i * SEGS, SEGS)
                mr, mi = _cmul(ar, ai, s[0], s[1])
                nr, ni = mr + bur[pl.ds(r0, SEGS), :], mi + bui[pl.ds(r0, SEGS), :]
                if store:
                    bur[pl.ds(r0, SEGS), :] = nr
                    bui[pl.ds(r0, SEGS), :] = ni
                return nr, ni

            return lax.fori_loop(0, t // SEGS, step, carry, unroll=8)

        zero = jnp.zeros((SEGS, lw), F32)
        er, ei = scan((zero, zero), False)
        scan(_seg_carries(er, ei, pr_ref[0], pi_ref[0], False), True)

        @pl.loop(0, nch)
        def _(c):
            rows = rows_of(c)
            y = _sdot(bur[rows, :], cre_ref[0]) - _sdot(bui[rows, :], cim_ref[0])

            @pl.when(hb % 2 == 0)
            def _():
                y_ref[rows, :] = y + d_ref[...] * u_ref[rows, :]

            @pl.when(hb % 2 == 1)
            def _():
                y_ref[rows, :] += y

    vec = pl.BlockSpec((1, 1, lw), lambda h: (h, 0, 0))
    return _call(
        body, name=name, grid=(S5_NHB,),
        in_specs=[pl.BlockSpec((t, LANES), lambda h: (0, h // 2)), vec, vec, vec, vec,
                  pl.BlockSpec((1, LANES, lw), lambda h: (h, 0, 0)), pl.BlockSpec((1, LANES, lw), lambda h: (h, 0, 0)),
                  pl.BlockSpec((1, lw, LANES), lambda h: (h, 0, 0)), pl.BlockSpec((1, lw, LANES), lambda h: (h, 0, 0)),
                  pl.BlockSpec((1, LANES), lambda h: (0, h // 2))],
        out_specs=pl.BlockSpec((t, LANES), lambda h: (0, h // 2)), out_shape=_sds((t, SSM_W), F32),
        scratch=[pltpu.VMEM((t, lw), F32)] * 2, sem=('arbitrary',), vmem=VMEM_BIG,
    )(u_p, prm['ar'], prm['ai'], prm['pr'], prm['pi'], prm['bre'], prm['bim'], prm['cre'], prm['cim'], prm['d'])


def s5_bwd(u_p, dy_p, prm, *, name):
    t = u_p.shape[0]
    ch = _s5_chunk(t)
    nch, steps = t // ch, ch // SEGS
    lw = S5_LW

    def body(u_ref, dy_ref, ar_ref, ai_ref, pr_ref, pi_ref, bre_ref, bim_ref, cre_ref, cim_ref, d_ref,
             du_ref, dar_ref, dai_ref, dbre_ref, dbim_ref, dcre_ref, dcim_ref, dd_ref, bur, bui, sr, si, du_acc):
        hb = pl.program_id(0)
        ar = jnp.broadcast_to(ar_ref[0], (SEGS, lw))
        ai = jnp.broadcast_to(ai_ref[0], (SEGS, lw))
        zero = jnp.zeros((SEGS, lw), F32)

        def rows_of(c):
            return pl.ds(pl.multiple_of(c * ch, ch), ch)

        nsteps = t // SEGS

        @pl.loop(0, nch)
        def _(c):
            u = u_ref[rows_of(c), :]
            bur[rows_of(c), :] = _sdot(u, bre_ref[0])
            bui[rows_of(c), :] = _sdot(u, bim_ref[0])

        def fwd_scan(carry, store):
            def step(i, s):
                r0 = pl.multiple_of(i * SEGS, SEGS)
                mr, mi = _cmul(ar, ai, s[0], s[1])
                nr, ni = mr + bur[pl.ds(r0, SEGS), :], mi + bui[pl.ds(r0, SEGS), :]
                if store:
                    w0 = pl.multiple_of(i * SEGS + SEGS, SEGS)
                    sr[pl.ds(w0, SEGS), :] = nr
                    si[pl.ds(w0, SEGS), :] = ni
                return nr, ni

            return lax.fori_loop(0, nsteps, step, carry, unroll=8)

        er, ei = fwd_scan((zero, zero), False)
        cin_r, cin_i = _seg_carries(er, ei, pr_ref[0], pi_ref[0], False)
        sr[pl.ds(0, SEGS), :] = cin_r
        si[pl.ds(0, SEGS), :] = cin_i
        fwd_scan((cin_r, cin_i), True)

        @pl.loop(0, nch)
        def _(c):
            dy = dy_ref[rows_of(c), :]
            bur[rows_of(c), :] = _sdot(dy, cre_ref[0], NT)
            bui[rows_of(c), :] = -_sdot(dy, cim_ref[0], NT)

        def rev_local(ii, lam):
            r0 = pl.multiple_of((nsteps - 1 - ii) * SEGS, SEGS)
            mr, mi = _cmul(ar, -ai, lam[0], lam[1])
            return mr + bur[pl.ds(r0, SEGS), :], mi + bui[pl.ds(r0, SEGS), :]

        lr0, li0 = lax.fori_loop(0, nsteps, rev_local, (zero, zero), unroll=8)
        rin = _seg_carries(lr0, li0, pr_ref[0], -pi_ref[0], True)

        def rev_step(ii, st):
            lam_r, lam_i, acc_r, acc_i = st
            r0 = pl.multiple_of((nsteps - 1 - ii) * SEGS, SEGS)
            mr, mi = _cmul(ar, -ai, lam_r, lam_i)
            nr, ni = mr + bur[pl.ds(r0, SEGS), :], mi + bui[pl.ds(r0, SEGS), :]
            bur[pl.ds(r0, SEGS), :] = nr
            bui[pl.ds(r0, SEGS), :] = ni
            pr_, pi_ = sr[pl.ds(r0, SEGS), :], si[pl.ds(r0, SEGS), :]
            return nr, ni, acc_r + (nr * pr_ + ni * pi_), acc_i + (ni * pr_ - nr * pi_)

        _, _, acc_r, acc_i = lax.fori_loop(0, nsteps, rev_step, (rin[0], rin[1], zero, zero), unroll=8)
        dar_ref[0] = _colsum(acc_r)
        dai_ref[0] = _colsum(acc_i)

        dbre_ref[...] = jnp.zeros_like(dbre_ref)
        dbim_ref[...] = jnp.zeros_like(dbim_ref)
        dcre_ref[...] = jnp.zeros_like(dcre_ref)
        dcim_ref[...] = jnp.zeros_like(dcim_ref)

        @pl.loop(0, nch)
        def _(c):
            rows = rows_of(c)
            u = u_ref[rows, :]
            dy = dy_ref[rows, :]
            lam_r, lam_i = bur[rows, :], bui[rows, :]
            du = _sdot(lam_r, bre_ref[0], NT) + _sdot(lam_i, bim_ref[0], NT)

            @pl.when(hb % 2 == 0)
            def _():
                du_acc[rows, :] = du + d_ref[...] * dy

            @pl.when(hb % 2 == 1)
            def _():
                du_ref[rows, :] = (du_acc[rows, :] + du).astype(BF)

            dbre_ref[0] += _sdot(u, lam_r, TN)
            dbim_ref[0] += _sdot(u, lam_i, TN)
            srows = pl.ds(pl.multiple_of(c * ch + SEGS, SEGS), ch)
            dcre_ref[0] += _sdot(sr[srows, :], dy, TN)
            dcim_ref[0] -= _sdot(si[srows, :], dy, TN)

        @pl.when(hb % 2 == 0)
        def _():
            dd_ref[...] = _colsum(dy_ref[...] * u_ref[...])

    vec = pl.BlockSpec((1, 1, lw), lambda h: (h, 0, 0))
    bsp = pl.BlockSpec((1, LANES, lw), lambda h: (h, 0, 0))
    csp = pl.BlockSpec((1, lw, LANES), lambda h: (h, 0, 0))
    act = pl.BlockSpec((t, LANES), lambda h: (0, h // 2))
    dsp = pl.BlockSpec((1, LANES), lambda h: (0, h // 2))
    return _call(
        body, name=name, grid=(S5_NHB,),
        in_specs=[act, act, vec, vec, vec, vec, bsp, bsp, csp, csp, dsp],
        out_specs=[act, vec, vec, bsp, bsp, csp, csp, dsp],
        out_shape=[_sds((t, SSM_W), BF), _sds((S5_NHB, 1, lw), F32), _sds((S5_NHB, 1, lw), F32),
                   _sds((S5_NHB, LANES, lw), F32), _sds((S5_NHB, LANES, lw), F32),
                   _sds((S5_NHB, lw, LANES), F32), _sds((S5_NHB, lw, LANES), F32), _sds((1, SSM_W), F32)],
        scratch=[pltpu.VMEM((t, lw), F32), pltpu.VMEM((t, lw), F32),
                 pltpu.VMEM((t + SEGS, lw), F32), pltpu.VMEM((t + SEGS, lw), F32), pltpu.VMEM((t, LANES), F32)],
        sem=('arbitrary',), vmem=VMEM_BIG,
    )(u_p, dy_p, prm['ar'], prm['ai'], prm['pr'], prm['pi'], prm['bre'], prm['bim'], prm['cre'], prm['cim'], prm['d'])


def s5_prep(t, lam_re, lam_im, log_step, b_re, b_im, c_re, c_im):
    step = jnp.exp(log_step)[:, None]
    mag = jnp.exp(lam_re * step)
    ar, ai = mag * jnp.cos(lam_im * step), mag * jnp.sin(lam_im * step)
    den = lam_re * lam_re + lam_im * lam_im
    nr, ni = ar - 1.0, ai
    fr, fi = (nr * lam_re + ni * lam_im) / den, (ni * lam_re - nr * lam_im) / den
    bbr = fr[..., None] * b_re - fi[..., None] * b_im
    bbi = fr[..., None] * b_im + fi[..., None] * b_re
    gl = S5_LW // SSM_P
    eye = jnp.eye(gl, dtype=F32)
    half = (jnp.arange(S5_NHB) % 2)[:, None, None]

    def bmat(bb):
        x = bb.transpose(0, 2, 1).reshape(S5_NHB, gl, SSM_H, SSM_P)
        x = jnp.einsum('bghp,gk->bghkp', x, eye).reshape(S5_NHB, gl * SSM_H, S5_LW)
        z = jnp.zeros_like(x)
        return jnp.where(half == 0, jnp.concatenate([x, z], axis=1), jnp.concatenate([z, x], axis=1))

    def cmat(cc):
        x = cc.transpose(0, 2, 1).reshape(S5_NHB, gl, SSM_P, SSM_H)
        x = jnp.einsum('bgph,gk->bgpkh', x, eye).reshape(S5_NHB, S5_LW, gl * SSM_H)
        z = jnp.zeros_like(x)
        return jnp.where(half == 0, jnp.concatenate([x, z], axis=2), jnp.concatenate([z, x], axis=2))

    vec = lambda a: a.reshape(S5_NHB, 1, S5_LW)
    ni_steps = float(t // SEGS)
    pmag = jnp.exp(lam_re * step * ni_steps)
    pr, pi = pmag * jnp.cos(lam_im * step * ni_steps), pmag * jnp.sin(lam_im * step * ni_steps)
    return dict(ar=vec(ar), ai=vec(ai), bre=bmat(bbr), bim=bmat(bbi), cre=cmat(c_re), cim=cmat(c_im),
                pr=lax.stop_gradient(vec(pr)), pi=lax.stop_gradient(vec(pi)))


def _gelu(x):
    c = math.sqrt(2.0 / math.pi)
    return 0.5 * x * (1.0 + jnp.tanh(c * (x + 0.044715 * (x * x * x))))


def _gelu_grad(x):
    c = math.sqrt(2.0 / math.pi)
    th = jnp.tanh(c * (x + 0.044715 * (x * x * x)))
    return 0.5 * (1.0 + th) + 0.5 * x * (1.0 - th * th) * (c * (1.0 + 3.0 * 0.044715 * (x * x)))


def glu_fwd(ypre, w_glu, b_glu, *, name, tq=512):
    t = ypre.shape[0]
    tq = _tile(t, tq)

    def body(y_ref, w_ref, b_ref, o_ref):
        yg = _gelu(y_ref[...])
        z = _dot(yg, w_ref[...]) + b_ref[...]
        o_ref[...] = yg * jax.nn.sigmoid(z)

    return _call(body, name=name, grid=(t // tq,),
                 in_specs=[pl.BlockSpec((tq, SSM_W), lambda i: (i, 0)), pl.BlockSpec((SSM_W, SSM_W), lambda i: (0, 0)),
                           pl.BlockSpec((1, SSM_W), lambda i: (0, 0))],
                 out_specs=pl.BlockSpec((tq, SSM_W), lambda i: (i, 0)), out_shape=_sds((t, SSM_W), F32),
                 sem=('parallel',))(ypre, w_glu, b_glu)


def glu_bwd(ypre, dy, w_glu, b_glu, *, name, tq=512):
    t = ypre.shape[0]
    tq = _tile(t, tq)

    def body(y_ref, dy_ref, w_ref, b_ref, dyp_ref, yg_ref, dz_ref, db_ref):
        ypre_ = y_ref[...]
        yg = _gelu(ypre_)
        sig = jax.nn.sigmoid(_dot(yg, w_ref[...]) + b_ref[...])
        dy_ = dy_ref[...]
        dz = dy_ * yg * sig * (1.0 - sig)
        dyg = dy_ * sig + _dot(dz, w_ref[...], NT)
        dyp_ref[...] = dyg * _gelu_grad(ypre_)
        yg_ref[...] = yg.astype(BF)
        dz_ref[...] = dz.astype(BF)

        @pl.when(pl.program_id(0) == 0)
        def _():
            db_ref[...] = jnp.zeros_like(db_ref)

        db_ref[...] += _colsum(dz)

    row = pl.BlockSpec((tq, SSM_W), lambda i: (i, 0))
    vec = pl.BlockSpec((1, SSM_W), lambda i: (0, 0))
    return _call(body, name=name, grid=(t // tq,),
                 in_specs=[row, row, pl.BlockSpec((SSM_W, SSM_W), lambda i: (0, 0)), vec],
                 out_specs=[row, row, row, vec],
                 out_shape=[_sds((t, SSM_W), F32), _sds((t, SSM_W), BF), _sds((t, SSM_W), BF), _sds((1, SSM_W), F32)],
                 sem=('arbitrary',))(ypre, dy, w_glu, b_glu)


def _rope(x, cos, sa, sb):
    return x * cos + pltpu.roll(x, 16, 1) * sa + pltpu.roll(x, 112, 1) * sb


def _rope_t(d, cos, sa, sb):
    return d * cos + pltpu.roll(d * sa, 112, 1) + pltpu.roll(d * sb, 16, 1)


def rope_tables(positions):
    half = QK_ROPE // 2
    inv_freq = ROPE_THETA ** (-jnp.arange(half, dtype=F32) / half)
    ang = positions.astype(F32)[:, None] * inv_freq
    cos, sin = jnp.cos(ang), jnp.sin(ang)
    t = positions.shape[0]
    one, zero = jnp.ones((t, QK_NOPE), F32), jnp.zeros((t, QK_NOPE), F32)
    pad1, pad0 = jnp.ones((t, 32), F32), jnp.zeros((t, 32), F32)
    z16 = jnp.zeros((t, half), F32)
    return (jnp.concatenate([one, cos, cos, pad1], axis=1), jnp.concatenate([zero, z16, sin, pad0], axis=1),
            jnp.concatenate([zero, -sin, z16, pad0], axis=1))


def mla_prep_fwd(proj, tabs, w, *, name):
    t = proj.shape[0]
    tq = _tile(t, ATT_BLK)

    def body(cq_ref, ckv_ref, kr_ref, cos_ref, sa_ref, sb_ref, qn_ref, kvn_ref, wq_ref, wk_ref, wv_ref, qg_ref, kg_ref,
             q_ref, qt_ref, k_ref, kt_ref, v_ref):
        cqn = (_rms(cq_ref[...], Q_LORA)[0] * qn_ref[...]).astype(BF)
        ckvn = (_rms(ckv_ref[...], KV_LORA)[0] * kvn_ref[...]).astype(BF)
        cos, sa, sb = cos_ref[...], sa_ref[...], sb_ref[...]
        kr = kr_ref[...]
        for h in range(MLA_HEADS):
            q = _rms(_dot(cqn, wq_ref[h]), QK_DIM)[0] * qg_ref[...]
            q = _rope(q, cos, sa, sb) * ATT_SCALE
            q_ref[h] = q.astype(BF)
            qt_ref[h, 0] = q.T.astype(BF)
            k = _rms(_dot(ckvn, wk_ref[h]) + kr, QK_DIM)[0] * kg_ref[...]
            k = _rope(k, cos, sa, sb)
            k_ref[h] = k.astype(BF)
            kt_ref[h, 0] = k.T.astype(BF)
            v_ref[h] = _dot(ckvn, wv_ref[h]).astype(BF)

    tab = pl.BlockSpec((tq, LANES), lambda i: (i, 0))
    full = lambda shape: pl.BlockSpec(shape, lambda i: (0,) * len(shape))
    hout = pl.BlockSpec((MLA_HEADS, tq, LANES), lambda i: (0, i, 0))
    tout = pl.BlockSpec((MLA_HEADS, 1, LANES, tq), lambda i: (0, i, 0, 0))
    hshape = _sds((MLA_HEADS, t, LANES), BF)
    tshape = _sds((MLA_HEADS, t // tq, LANES, tq), BF)
    return _call(
        body, name=name, grid=(t // tq,),
        in_specs=[pl.BlockSpec((tq, Q_LORA), lambda i: (i, 2)), pl.BlockSpec((tq, LANES), lambda i: (i, 6)),
                  pl.BlockSpec((tq, LANES), lambda i: (i, 7)), tab, tab, tab,
                  full((1, Q_LORA)), full((1, KV_LORA)), full((MLA_HEADS, Q_LORA, LANES)),
                  full((MLA_HEADS, KV_LORA, LANES)), full((MLA_HEADS, KV_LORA, LANES)), full((1, LANES)), full((1, LANES))],
        out_specs=[hout, tout, hout, tout, hout], out_shape=[hshape, tshape, hshape, tshape, hshape], sem=('parallel',),
    )(proj, proj, proj, *tabs, w['q_norm'], w['kv_norm'], w['wq'], w['wk'], w['wv'], w['q_gain'], w['k_gain'])


def mla_prep_bwd(proj, tabs, w, dq, dk, dv, *, name):
    t = proj.shape[0]
    tq = _tile(t, ATT_BLK)

    def body(cq_ref, ckv_ref, kr_ref, cos_ref, sa_ref, sb_ref, qn_ref, kvn_ref, wq_ref, wk_ref, wv_ref, qg_ref, kg_ref,
             dq_ref, dk_ref, dv_ref,
             dpm_ref, cqn_ref, ckvn_ref, dqr_ref, dkraw_ref, dvb_ref, dqn_ref, dkvn_ref, dqg_ref, dkg_ref):
        cq_h, cq_r = _rms(cq_ref[...], Q_LORA)
        ckv_h, ckv_r = _rms(ckv_ref[...], KV_LORA)
        cqn = (cq_h * qn_ref[...]).astype(BF)
        ckvn = (ckv_h * kvn_ref[...]).astype(BF)
        cqn_ref[...] = cqn
        ckvn_ref[...] = ckvn
        cos, sa, sb = cos_ref[...], sa_ref[...], sb_ref[...]
        kr = kr_ref[...]
        dcqn = jnp.zeros((tq, Q_LORA), F32)
        dckvn = jnp.zeros((tq, KV_LORA), F32)
        dkrope = jnp.zeros((tq, LANES), F32)
        dqg = jnp.zeros((1, LANES), F32)
        dkg = jnp.zeros((1, LANES), F32)
        for h in range(MLA_HEADS):
            qh, qr = _rms(_dot(cqn, wq_ref[h]), QK_DIM)
            dqo = _rope_t(dq_ref[h, 0].T * ATT_SCALE, cos, sa, sb)
            dqg = dqg + _colsum(dqo * qh)
            dqraw = _rms_bwd(qh, qr, dqo * qg_ref[...], QK_DIM).astype(BF)
            dqr_ref[:, h * LANES:(h + 1) * LANES] = dqraw
            dcqn = dcqn + _dot(dqraw, wq_ref[h], NT)
            kh, krs = _rms(_dot(ckvn, wk_ref[h]) + kr, QK_DIM)
            dko = _rope_t(dk_ref[h], cos, sa, sb)
            dkg = dkg + _colsum(dko * kh)
            dkraw = _rms_bwd(kh, krs, dko * kg_ref[...], QK_DIM)
            dkrope = dkrope + dkraw
            dkraw = dkraw.astype(BF)
            dkraw_ref[:, h * LANES:(h + 1) * LANES] = dkraw
            dvb = dv_ref[h].astype(BF)
            dvb_ref[:, h * LANES:(h + 1) * LANES] = dvb
            dckvn = dckvn + _dot(dkraw, wk_ref[h], NT) + _dot(dvb, wv_ref[h], NT)
        dpm_ref[:, 0:Q_LORA] = _rms_bwd(cq_h, cq_r, dcqn * qn_ref[...], Q_LORA).astype(BF)
        dpm_ref[:, Q_LORA:Q_LORA + KV_LORA] = _rms_bwd(ckv_h, ckv_r, dckvn * kvn_ref[...], KV_LORA).astype(BF)
        dpm_ref[:, Q_LORA + KV_LORA:512] = dkrope.astype(BF)

        @pl.when(pl.program_id(0) == 0)
        def _():
            dqn_ref[...] = jnp.zeros_like(dqn_ref)
            dkvn_ref[...] = jnp.zeros_like(dkvn_ref)
            dqg_ref[...] = jnp.zeros_like(dqg_ref)
            dkg_ref[...] = jnp.zeros_like(dkg_ref)

        dqn_ref[...] += _colsum(dcqn * cq_h)
        dkvn_ref[...] += _colsum(dckvn * ckv_h)
        dqg_ref[...] += dqg
        dkg_ref[...] += dkg

    tab = pl.BlockSpec((tq, LANES), lambda i: (i, 0))
    full = lambda shape: pl.BlockSpec(shape, lambda i: (0,) * len(shape))
    hblk = pl.BlockSpec((MLA_HEADS, tq, LANES), lambda i: (0, i, 0))
    wide = pl.BlockSpec((tq, MLA_HEADS * LANES), lambda i: (i, 0))
    return _call(
        body, name=name, grid=(t // tq,),
        in_specs=[pl.BlockSpec((tq, Q_LORA), lambda i: (i, 2)), pl.BlockSpec((tq, LANES), lambda i: (i, 6)),
                  pl.BlockSpec((tq, LANES), lambda i: (i, 7)), tab, tab, tab,
                  full((1, Q_LORA)), full((1, KV_LORA)), full((MLA_HEADS, Q_LORA, LANES)),
                  full((MLA_HEADS, KV_LORA, LANES)), full((MLA_HEADS, KV_LORA, LANES)), full((1, LANES)), full((1, LANES)),
                  pl.BlockSpec((MLA_HEADS, 1, LANES, tq), lambda i: (0, i, 0, 0)), hblk, hblk],
        out_specs=[pl.BlockSpec((tq, 512), lambda i: (i, 0)),
                   pl.BlockSpec((tq, Q_LORA), lambda i: (i, 0)), pl.BlockSpec((tq, KV_LORA), lambda i: (i, 0)),
                   wide, wide, wide, full((1, Q_LORA)), full((1, KV_LORA)), full((1, LANES)), full((1, LANES))],
        out_shape=[_sds((t, 512), BF), _sds((t, Q_LORA), BF), _sds((t, KV_LORA), BF),
                   _sds((t, MLA_HEADS * LANES), BF), _sds((t, MLA_HEADS * LANES), BF), _sds((t, MLA_HEADS * LANES), BF),
                   _sds((1, Q_LORA), F32), _sds((1, KV_LORA), F32), _sds((1, LANES), F32), _sds((1, LANES), F32)],
        sem=('arbitrary',),
    )(proj, proj, proj, *tabs, w['q_norm'], w['kv_norm'], w['wq'], w['wk'], w['wv'], w['q_gain'], w['k_gain'], dq, dk, dv)


ATT_BLK = 256
ATT_SCALE = 1.0 / math.sqrt(QK_DIM)


def _overlapped(grid, make_copies):
    ids = [pl.program_id(a) for a in range(len(grid))]
    first = functools.reduce(jnp.logical_and, [i == 0 for i in ids])
    last = functools.reduce(jnp.logical_and, [i == n - 1 for i, n in zip(ids, grid)])

    @pl.when(first)
    def _():
        for cs in make_copies():
            _start_copies(cs)

    @pl.when(last)
    def _():
        for cs in make_copies():
            _wait_copies(cs)


def flash_fwd(q, kt, v, *, name, gather=()):
    t = q.shape[1]
    blk = _tile(t, ATT_BLK)
    grid = (MLA_HEADS // 2, t // blk)

    def body(q_ref, kt_ref, v_ref, *rest):
        nc = len(gather)
        srcs, (o_ref, lse_ref), dsts, sems = rest[:nc], rest[nc:nc + 2], rest[nc + 2:2 * nc + 2], rest[2 * nc + 2:]
        if nc:
            _overlapped(grid, lambda: [_copies('gather', srcs[i], dsts[i], *sems[3 * i:3 * i + 3]) for i in range(nc)])
        qi = pl.program_id(1)
        row = lax.broadcasted_iota(jnp.int32, (blk, blk), 0)
        col = lax.broadcasted_iota(jnp.int32, (blk, blk), 1)

        def block(j, carry, masked):
            out = []
            for hh in range(2):
                m, l, acc = carry[hh]
                s = _dot(q_ref[hh], kt_ref[hh, j])
                if masked:
                    s = jnp.where(col <= row, s, -jnp.inf)
                m2 = jnp.maximum(m, jnp.max(s, axis=-1, keepdims=True))
                p = jnp.exp(s - m2)
                alpha = jnp.exp(m - m2)
                rows = pl.ds(pl.multiple_of(j * blk, blk), blk)
                out.append((m2, alpha * l + jnp.sum(p, axis=-1, keepdims=True), alpha * acc + _dot(p, v_ref[hh, rows, :])))
            return tuple(out)

        init = (jnp.full((blk, 1), -jnp.inf, F32), jnp.zeros((blk, 1), F32), jnp.zeros((blk, LANES), F32))
        carry = lax.fori_loop(0, qi, lambda j, c: block(j, c, False), (init, init))
        carry = block(qi, carry, True)
        o_acc = jnp.zeros((blk, LANES), F32)
        for hh in range(2):
            m, l, acc = carry[hh]
            o_acc = o_acc + acc / l
            lse_ref[hh, 0] = jnp.broadcast_to(m + jnp.log(l), (blk, LANES)).T[0:1, :]
        o_ref[...] = o_acc

    in_specs = [pl.BlockSpec((2, blk, LANES), lambda p, i: (p, i, 0)),
                pl.BlockSpec((2, t // blk, LANES, blk), lambda p, i: (p, 0, 0, 0)),
                pl.BlockSpec((2, t, LANES), lambda p, i: (p, 0, 0))]
    out_specs = [pl.BlockSpec((blk, LANES), lambda p, i: (i, p)), pl.BlockSpec((2, 1, 1, blk), lambda p, i: (p, i, 0, 0))]
    out_shape = [_sds((t, 512), F32), _sds((MLA_HEADS, t // blk, 1, blk), F32)]
    nc = len(gather)
    return _call(body, name=name, grid=grid, in_specs=in_specs + [_ANY] * nc, out_specs=out_specs + [_ANY] * nc,
                 out_shape=out_shape + [_sds((NDEV,) + g.shape, g.dtype) for g in gather], scratch=_COMM_SCRATCH * nc,
                 sem=('arbitrary', 'arbitrary') if nc else ('parallel', 'parallel'))(q, kt, v, *gather)


def mla_out_bwd(o, dyn, g, *, name):
    t = o.shape[0]
    blk = _tile(t, ATT_BLK)

    def body(o_ref, dh_ref, g_ref, do_ref, dot_ref, delta_ref, dg_ref):
        ov = o_ref[...]
        oh, r = _rms(ov, 512)
        dh = dh_ref[...]
        do = _rms_bwd(oh, r, dh * g_ref[...], 512)
        do_ref[...] = do.astype(BF)
        dd = do * ov
        for pb in range(MLA_HEADS // 2):
            cols = slice(pb * LANES, (pb + 1) * LANES)
            dot_ref[pb, 0] = do[:, cols].T.astype(BF)
            ddt = dd[:, cols].T
            delta_ref[2 * pb, 0] = jnp.sum(ddt[0:V_DIM, :], axis=0, keepdims=True)
            delta_ref[2 * pb + 1, 0] = jnp.sum(ddt[V_DIM:LANES, :], axis=0, keepdims=True)

        @pl.when(pl.program_id(0) == 0)
        def _():
            dg_ref[...] = jnp.zeros_like(dg_ref)

        dg_ref[...] += _colsum(dh * oh)

    return _call(
        body, name=name, grid=(t // blk,),
        in_specs=[pl.BlockSpec((blk, 512), lambda i: (i, 0)), pl.BlockSpec((blk, 512), lambda i: (i, 1)),
                  pl.BlockSpec((1, 512), lambda i: (0, 0))],
        out_specs=[pl.BlockSpec((blk, 512), lambda i: (i, 0)), pl.BlockSpec((MLA_HEADS // 2, 1, LANES, blk), lambda i: (0, i, 0, 0)),
                   pl.BlockSpec((MLA_HEADS, 1, 1, blk), lambda i: (0, i, 0, 0)), pl.BlockSpec((1, 512), lambda i: (0, 0))],
        out_shape=[_sds((t, 512), BF), _sds((MLA_HEADS // 2, t // blk, LANES, blk), BF),
                   _sds((MLA_HEADS, t // blk, 1, blk), F32), _sds((1, 512), F32)],
        sem=('arbitrary',),
    )(o, dyn, g)


def flash_bwd(q, qt, k, kt, v, do, dot, lse, delta, *, name, scatter=()):
    t = q.shape[1]
    blk = _tile(t, ATT_BLK)
    nb = t // blk
    grid = (MLA_HEADS, nb)

    def body(q_ref, qt_ref, k_ref, kt_ref, v_ref, do_ref, dot_ref, lse_ref, delta_ref, *rest):
        nc = len(scatter)
        srcs, (dqt_ref, dk_ref, dv_ref), dsts, sems = rest[:nc], rest[nc:nc + 3], rest[nc + 3:2 * nc + 3], rest[2 * nc + 3:]
        if nc:
            _overlapped(grid, lambda: [_copies('scatter', srcs[i], dsts[i], *sems[3 * i:3 * i + 3]) for i in range(nc)])
        h, j = pl.program_id(0), pl.program_id(1)
        row = lax.broadcasted_iota(jnp.int32, (blk, blk), 0)
        col = lax.broadcasted_iota(jnp.int32, (blk, blk), 1)
        lane = lax.broadcasted_iota(jnp.int32, (1, LANES), 1)
        mine = (lane // V_DIM) == (h % 2)

        @pl.when(j == 0)
        def _():
            dqt_ref[...] = jnp.zeros_like(dqt_ref)

        kv, ktv, vv = k_ref[...], kt_ref[...], v_ref[...]

        def block(i, carry, masked):
            dk, dv = carry
            rows = pl.ds(pl.multiple_of(i * blk, blk), blk)
            pt = jnp.exp(_dot(kv, qt_ref[i]) - lse_ref[i])
            if masked:
                pt = jnp.where(col >= row, pt, 0.0)
            dv = dv + _dot(pt, do_ref[rows, :])
            dst = (pt * (_dot(vv, dot_ref[i]) - delta_ref[i])).astype(BF)
            dk = dk + _dot(dst, q_ref[rows, :])
            dqt_ref[i] += _dot(ktv, dst)
            return dk, dv

        zero = jnp.zeros((blk, LANES), F32)
        carry = block(j, (zero, zero), True)
        npairs = (nb - 1 - j) // 2
        carry = lax.fori_loop(0, npairs, lambda p, c: block(j + 2 + 2 * p, block(j + 1 + 2 * p, c, False), False), carry)
        dk, dv = lax.fori_loop(j + 1 + 2 * npairs, nb, lambda i, c: block(i, c, False), carry)
        dk_ref[...] = dk
        dv_ref[...] = jnp.where(mine, dv, 0.0)

    whole = pl.BlockSpec((None, t, LANES), lambda h, j: (h, 0, 0))
    wholet = pl.BlockSpec((None, nb, LANES, blk), lambda h, j: (h, 0, 0, 0))
    kvb = pl.BlockSpec((None, blk, LANES), lambda h, j: (h, j, 0))
    rowv = pl.BlockSpec((None, nb, 1, blk), lambda h, j: (h, 0, 0, 0))
    in_specs = [whole, wholet, kvb, pl.BlockSpec((None, None, LANES, blk), lambda h, j: (h, j, 0, 0)), kvb,
                pl.BlockSpec((t, LANES), lambda h, j: (0, h // 2)),
                pl.BlockSpec((None, nb, LANES, blk), lambda h, j: (h // 2, 0, 0, 0)), rowv, rowv]
    out_specs = [wholet, kvb, kvb]
    out_shape = [_sds((MLA_HEADS, nb, LANES, blk), F32), _sds((MLA_HEADS, t, LANES), F32), _sds((MLA_HEADS, t, LANES), F32)]
    args = (q, qt, k, kt, v, do, dot, lse, delta)
    nc = len(scatter)
    return _call(body, name=name, grid=grid, in_specs=in_specs + [_ANY] * nc, out_specs=out_specs + [_ANY] * nc,
                 out_shape=out_shape + [_sds(s.shape, s.dtype) for s in scatter], scratch=_COMM_SCRATCH * nc,
                 sem=('arbitrary', 'arbitrary') if nc else ('parallel', 'arbitrary'), vmem=VMEM_BIG)(*args, *scatter)


def mix_out_fwd(x, y_ssm, o, g_ssm, g_mla, w_out, *, name, tq=512):
    t = x.shape[0]
    tq = _tile(t, tq)

    def body(x_ref, ys_ref, o_ref, gs_ref, gm_ref, w_ref, x1_ref, yn_ref):
        ns = (_rms(ys_ref[...], SSM_W)[0] * gs_ref[...]).astype(BF)
        nm = (_rms(o_ref[...], 512)[0] * gm_ref[...]).astype(BF)
        yn_ref[:, 0:SSM_W] = ns
        yn_ref[:, SSM_W:D] = nm
        x1_ref[...] = x_ref[...] + _dot(ns, w_ref[0:SSM_W, :]) + _dot(nm, w_ref[SSM_W:D, :])

    row = lambda w: pl.BlockSpec((tq, w), lambda i: (i, 0))
    vec = pl.BlockSpec((1, 512), lambda i: (0, 0))
    return _call(body, name=name, grid=(t // tq,),
                 in_specs=[row(D), row(512), row(512), vec, vec, pl.BlockSpec((D, D), lambda i: (0, 0))],
                 out_specs=[row(D), row(D)], out_shape=[_sds((t, D), F32), _sds((t, D), BF)], sem=('parallel',),
                 )(x, y_ssm, o, g_ssm, g_mla, w_out)


MEM_SCALE = 1.0 / math.sqrt(MEM_HD)


def memkv_fwd(mem, g, wk, wv, kg, *, name):
    def body(m_ref, g_ref, wk_ref, wv_ref, kg_ref, mh_ref, k_ref, v_ref):
        mh = (_rms(m_ref[...], D)[0] * g_ref[...]).astype(BF)
        mh_ref[...] = mh
        for h in range(MEM_HEADS):
            cols = slice(h * LANES, (h + 1) * LANES)
            k_ref[h] = (_rms(_dot(mh, wk_ref[:, cols]), MEM_HD)[0] * kg_ref[...]).astype(BF)
            v_ref[h] = _dot(mh, wv_ref[:, cols]).astype(BF)

    return _call(body, name=name,
                 out_shape=[_sds((N_MEM, D), BF), _sds((MEM_HEADS, N_MEM, LANES), BF), _sds((MEM_HEADS, N_MEM, LANES), BF)],
                 )(mem, g, wk, wv, kg)


def memkv_bwd(mem, g, wk, wv, kg, dk, dv, *, name):
    def body(m_ref, g_ref, wk_ref, wv_ref, kg_ref, dk_ref, dv_ref, dwk_ref, dwv_ref, dkg_ref, dg_ref):
        mhat, _ = _rms(m_ref[...], D)
        mh = (mhat * g_ref[...]).astype(BF)
        lane = lax.broadcasted_iota(jnp.int32, (1, LANES), 1)
        dkg = jnp.zeros((1, LANES), F32)
        dmh = jnp.zeros((N_MEM, D), F32)
        for h in range(MEM_HEADS):
            cols = slice(h * LANES, (h + 1) * LANES)
            kh, kr = _rms(_dot(mh, wk_ref[:, cols]), MEM_HD)
            dko = dk_ref[h]
            dkg = dkg + _colsum(dko * kh)
            dkraw = _rms_bwd(kh, kr, dko * kg_ref[...], MEM_HD).astype(BF)
            dvh = jnp.where((lane // MEM_HD) == (h % 2), dv_ref[h], 0.0).astype(BF)
            dwk_ref[:, cols] = _dot(mh, dkraw, TN)
            dwv_ref[:, cols] = _dot(mh, dvh, TN)
            dmh = dmh + _dot(dkraw, wk_ref[:, cols], NT) + _dot(dvh, wv_ref[:, cols], NT)
        dkg_ref[...] = dkg
        dg_ref[...] = _colsum(dmh * mhat)

    return _call(body, name=name,
                 out_shape=[_sds((D, 512), F32), _sds((D, 512), F32), _sds((1, LANES), F32), _sds((1, D), F32)],
                 )(mem, g, wk, wv, kg, dk, dv)


def memattn_fwd(x, g, wq, qg, kh, vh, wo, *, name, tq=512):
    t = x.shape[0]
    tq = _tile(t, tq)

    def body(x_ref, g_ref, wq_ref, qg_ref, k_ref, v_ref, wo_ref, x2_ref, hn_ref):
        xv = x_ref[...]
        hn = (_rms(xv, D)[0] * g_ref[...]).astype(BF)
        hn_ref[...] = hn
        out = xv
        for pb in range(MEM_HEADS // 2):
            o = jnp.zeros((tq, LANES), F32)
            for h in (2 * pb, 2 * pb + 1):
                q = _rms(_dot(hn, wq_ref[:, h * LANES:(h + 1) * LANES]), MEM_HD)[0] * qg_ref[...]
                s = _dot(q, k_ref[h], NT) * MEM_SCALE
                p = jnp.exp(s - jnp.max(s, axis=-1, keepdims=True))
                p = p / jnp.sum(p, axis=-1, keepdims=True)
                o = o + _dot(p, v_ref[h])
            out = out + _dot(o, wo_ref[pb * LANES:(pb + 1) * LANES, :])
        x2_ref[...] = out

    full = lambda shape: pl.BlockSpec(shape, lambda i: (0,) * len(shape))
    row = pl.BlockSpec((tq, D), lambda i: (i, 0))
    return _call(body, name=name, grid=(t // tq,),
                 in_specs=[row, full((1, D)), full((D, 512)), full((1, LANES)), full((MEM_HEADS, N_MEM, LANES)),
                           full((MEM_HEADS, N_MEM, LANES)), full((MEM_HEADS * MEM_HD, D))],
                 out_specs=[row, row], out_shape=[_sds((t, D), F32), _sds((t, D), BF)], sem=('parallel',),
                 )(x, g, wq, qg, kh, vh, wo)


def memattn_bwd(x, dx2, g, wq, qg, kh, vh, wo, *, name, tq=512):
    t = x.shape[0]
    tq = _tile(t, tq)

    def body(x_ref, dx2_ref, g_ref, wq_ref, qg_ref, k_ref, v_ref, wo_ref,
             dx_ref, dxb_ref, o_ref, dqr_ref, dk_ref, dv_ref, dqg_ref, dg_ref):
        @pl.when(pl.program_id(0) == 0)
        def _():
            dk_ref[...] = jnp.zeros_like(dk_ref)
            dv_ref[...] = jnp.zeros_like(dv_ref)
            dqg_ref[...] = jnp.zeros_like(dqg_ref)
            dg_ref[...] = jnp.zeros_like(dg_ref)

        xhat, xr = _rms(x_ref[...], D)
        hn = (xhat * g_ref[...]).astype(BF)
        dx2 = dx2_ref[...]
        dx2b = dx2.astype(BF)
        dh = jnp.zeros((tq, D), F32)
        dqg = jnp.zeros((1, LANES), F32)
        for pb in range(MEM_HEADS // 2):
            do = _dot(dx2b, wo_ref[pb * LANES:(pb + 1) * LANES, :], NT).astype(BF)
            o = jnp.zeros((tq, LANES), F32)
            for h in (2 * pb, 2 * pb + 1):
                cols = slice(h * LANES, (h + 1) * LANES)
                qh, qr = _rms(_dot(hn, wq_ref[:, cols]), MEM_HD)
                qb = (qh * qg_ref[...]).astype(BF)
                s = _dot(qb, k_ref[h], NT) * MEM_SCALE
                p = jnp.exp(s - jnp.max(s, axis=-1, keepdims=True))
                p = p / jnp.sum(p, axis=-1, keepdims=True)
                pb16 = p.astype(BF)
                o = o + _dot(pb16, v_ref[h])
                dv_ref[h] += _dot(pb16, do, TN)
                dp = _dot(do, v_ref[h], NT)
                ds = (p * (dp - jnp.sum(dp * p, axis=-1, keepdims=True)) * MEM_SCALE).astype(BF)
                dk_ref[h] += _dot(ds, qb, TN)
                dqo = _dot(ds, k_ref[h])
                dqg = dqg + _colsum(dqo * qh)
                dqraw = _rms_bwd(qh, qr, dqo * qg_ref[...], MEM_HD).astype(BF)
                dqr_ref[:, cols] = dqraw
                dh = dh + _dot(dqraw, wq_ref[:, cols], NT)
            o_ref[:, pb * LANES:(pb + 1) * LANES] = o.astype(BF)
        dx = dx2 + _rms_bwd(xhat, xr, dh * g_ref[...], D)
        dx_ref[...] = dx
        dxb_ref[...] = dx.astype(BF)
        dqg_ref[...] += dqg
        dg_ref[...] += _colsum(dh * xhat)

    full = lambda shape: pl.BlockSpec(shape, lambda i: (0,) * len(shape))
    row = lambda w: pl.BlockSpec((tq, w), lambda i: (i, 0))
    return _call(body, name=name, grid=(t // tq,),
                 in_specs=[row(D), row(D), full((1, D)), full((D, 512)), full((1, LANES)), full((MEM_HEADS, N_MEM, LANES)),
                           full((MEM_HEADS, N_MEM, LANES)), full((MEM_HEADS * MEM_HD, D))],
                 out_specs=[row(D), row(D), row(256), row(512), full((MEM_HEADS, N_MEM, LANES)), full((MEM_HEADS, N_MEM, LANES)),
                            full((1, LANES)), full((1, D))],
                 out_shape=[_sds((t, D), F32), _sds((t, D), BF), _sds((t, 256), BF), _sds((t, 512), BF),
                            _sds((MEM_HEADS, N_MEM, LANES), F32), _sds((MEM_HEADS, N_MEM, LANES), F32),
                            _sds((1, LANES), F32), _sds((1, D), F32)],
                 sem=('arbitrary',))(x, dx2, g, wq, qg, kh, vh, wo)


def mlp_fwd(x, h, w1, w2, *, name, tq=1024, tf=512):
    t = x.shape[0]
    tq = _tile(t, tq)

    def body(x_ref, h_ref, w1_ref, w2_ref, o_ref):
        @pl.when(pl.program_id(1) == 0)
        def _():
            o_ref[...] = x_ref[...]

        a = jnp.maximum(_dot(h_ref[...], w1_ref[...]), 0.0)
        o_ref[...] += _dot(a * a, w2_ref[...])

    row = pl.BlockSpec((tq, D), lambda i, f: (i, 0))
    return _call(body, name=name, grid=(t // tq, D_FF // tf),
                 in_specs=[row, row, pl.BlockSpec((None, D, tf), lambda i, f: (f, 0, 0)), pl.BlockSpec((tf, D), lambda i, f: (f, 0))],
                 out_specs=row, out_shape=_sds((t, D), F32), sem=('parallel', 'arbitrary'), vmem=VMEM_BIG)(x, h, w1, w2)


def mlp_bwd(h, dx, w1, w2, *, name, tq=1024, tf=512):
    t = h.shape[0]
    tq = _tile(t, tq)

    def body(h_ref, dx_ref, w1_ref, w2_ref, dh_ref, r_ref, da_ref):
        @pl.when(pl.program_id(1) == 0)
        def _():
            dh_ref[...] = jnp.zeros_like(dh_ref)

        a = jnp.maximum(_dot(h_ref[...], w1_ref[...]), 0.0)
        r_ref[...] = (a * a).astype(BF)
        da = (_dot(dx_ref[...], w2_ref[...], NT) * (2.0 * a)).astype(BF)
        da_ref[...] = da
        dh_ref[...] += _dot(da, w1_ref[...], NT)

    row = pl.BlockSpec((tq, D), lambda i, f: (i, 0))
    act = pl.BlockSpec((tq, tf), lambda i, f: (i, f))
    return _call(body, name=name, grid=(t // tq, D_FF // tf),
                 in_specs=[row, row, pl.BlockSpec((None, D, tf), lambda i, f: (f, 0, 0)), pl.BlockSpec((tf, D), lambda i, f: (f, 0))],
                 out_specs=[row, act, act], out_shape=[_sds((t, D), F32), _sds((t, D_FF), BF), _sds((t, D_FF), BF)],
                 sem=('parallel', 'arbitrary'), vmem=VMEM_BIG)(h, dx, w1, w2)


def loss_fwd_bwd(y, target, *, name, tq=512):
    t = y.shape[0]
    tq = _tile(t, tq)

    def body(y_ref, t_ref, dy_ref, dyb_ref, l_ref):
        @pl.when(pl.program_id(0) == 0)
        def _():
            l_ref[...] = jnp.zeros_like(l_ref)

        e = y_ref[...] - t_ref[...]
        dy = e * (1.0 / D)
        dy_ref[...] = dy
        dyb_ref[...] = dy.astype(BF)
        l_ref[...] += _colsum(e * e) * (0.5 / D)

    row = pl.BlockSpec((tq, D), lambda i: (i, 0))
    return _call(body, name=name, grid=(t // tq,), in_specs=[row, row],
                 out_specs=[row, row, pl.BlockSpec((1, D), lambda i: (0, 0))],
                 out_shape=[_sds((t, D), F32), _sds((t, D), BF), _sds((1, D), F32)], sem=('arbitrary',))(y, target)


def prep_early(w):
    w_in = w['w_in']
    z = lambda r, c: jnp.zeros((r, c), w_in.dtype)
    w_in_pad = jnp.concatenate([w_in[:, :896], z(D, 64), w_in[:, 896:928], z(D, 32)], axis=1)
    wq = w['mla_w_uq'].reshape(Q_LORA, MLA_HEADS, QK_DIM).transpose(1, 0, 2)
    wq = jnp.pad(wq, ((0, 0), (0, 0), (0, LANES - QK_DIM)))
    ukv = w['mla_w_ukv'].reshape(KV_LORA, MLA_HEADS, QK_NOPE + V_DIM).transpose(1, 0, 2)
    wk = jnp.pad(ukv[:, :, :QK_NOPE], ((0, 0), (0, 0), (0, LANES - QK_NOPE)))
    vpart = ukv[:, :, QK_NOPE:]
    zv = jnp.zeros_like(vpart)
    odd = (jnp.arange(MLA_HEADS) % 2)[:, None, None] == 1
    wv = jnp.where(odd, jnp.concatenate([zv, vpart], axis=2), jnp.concatenate([vpart, zv], axis=2))
    return dict(w_in=w_in_pad, w_glu=w['ssm_w_glu'], wq=wq, wk=wk, wv=wv)


def prep_late(w):
    mq = jnp.pad(w['mem_w_q'].reshape(D, MEM_HEADS, MEM_HD), ((0, 0), (0, 0), (0, LANES - MEM_HD))).reshape(D, 512)
    mkv = w['mem_w_kv'].reshape(D, MEM_HEADS, 2 * MEM_HD)
    mk = jnp.pad(mkv[:, :, :MEM_HD], ((0, 0), (0, 0), (0, LANES - MEM_HD))).reshape(D, 512)
    mvp = mkv[:, :, MEM_HD:]
    zm = jnp.zeros_like(mvp)
    modd = (jnp.arange(MEM_HEADS) % 2)[None, :, None] == 1
    mv = jnp.where(modd, jnp.concatenate([zm, mvp], axis=2), jnp.concatenate([mvp, zm], axis=2)).reshape(D, 512)
    return dict(w_out=w['w_out'], mq=mq, mk=mk, mv=mv, mo=w['mem_w_o'], w1=w['mlp_w1'], w2=w['mlp_w2'])


def prep_small(t, s):
    row = lambda a: a.reshape(1, -1)
    pad = lambda a: jnp.pad(a, (0, LANES - a.shape[0])).reshape(1, LANES)
    out = s5_prep(t, s['ssm_lambda_re'], s['ssm_lambda_im'], s['ssm_log_step'], s['ssm_b_re'], s['ssm_b_im'],
                  s['ssm_c_re'], s['ssm_c_im'])
    out.update(d=row(s['ssm_d']), norm_mix=row(s['norm_mix']), b_glu=row(s['ssm_b_glu']), q_norm=row(s['mla_q_norm']),
               kv_norm=row(s['mla_kv_norm']), q_gain=pad(s['mla_q_gain']), k_gain=pad(s['mla_k_gain']),
               g_ssm=row(s['out_norm_ssm']), g_mla=row(s['out_norm_mla']), norm_mem_q=row(s['norm_mem_q']),
               norm_mem_kv=row(s['norm_mem_kv']), mem_q_gain=pad(s['mem_q_gain']), mem_k_gain=pad(s['mem_k_gain']),
               norm_mlp=row(s['norm_mlp']))
    return out


def _perm(a):
    t, c = a.shape
    return a.reshape(SEGS, t // SEGS, c).transpose(1, 0, 2).reshape(t, c)


def _unperm(a):
    t, c = a.shape
    return a.reshape(t // SEGS, SEGS, c).transpose(1, 0, 2).reshape(t, c)


def layer_fwd(l, x, mem, tabs, plan, ws):
    n = lambda s: f'l{l}_{s}'
    wb = prep_early(plan.early(l))
    h1 = rmsnorm_fwd(x, ws['norm_mix'], name=n('norm_mix'))
    proj = mm(h1, wb['w_in'], 'nn', name=n('w_in'))
    u_p = _perm(proj[:, :SSM_W])
    ypre_p = s5_fwd(u_p, ws, name=n('s5'))
    ypre = _unperm(ypre_p)
    y_ssm = glu_fwd(ypre, wb['w_glu'], ws['b_glu'], name=n('glu'))
    mw = dict(q_norm=ws['q_norm'], kv_norm=ws['kv_norm'], wq=wb['wq'], wk=wb['wk'], wv=wb['wv'],
              q_gain=ws['q_gain'], k_gain=ws['k_gain'])
    q, qt, k, kt, v = mla_prep_fwd(proj, tabs, mw, name=n('mla_prep'))
    o, lse, *gathered = flash_fwd(q, kt, v, name=n('flash'), gather=plan.gather_src(l))
    plan.gathered(l, gathered)
    wb.update(prep_late(plan.late(l)))
    x1, yn = mix_out_fwd(x, y_ssm, o, ws['g_ssm'], ws['g_mla'], wb['w_out'], name=n('mix_out'))
    mh, kh, vh = memkv_fwd(mem, ws['norm_mem_kv'], wb['mk'], wb['mv'], ws['mem_k_gain'], name=n('memkv'))
    x2, h2 = memattn_fwd(x1, ws['norm_mem_q'], wb['mq'], ws['mem_q_gain'], kh, vh, wb['mo'], name=n('memattn'))
    h3 = rmsnorm_fwd(x2, ws['norm_mlp'], name=n('norm_mlp'))
    x3 = mlp_fwd(x2, h3, wb['w1'], wb['w2'], name=n('mlp'))
    saved = dict(x=x, h1=h1, proj=proj, u_p=u_p, ypre=ypre, y_ssm=y_ssm, q=q, qt=qt, k=k, kt=kt, v=v, o=o, lse=lse, x1=x1, yn=yn,
                 kh=kh, vh=vh, x2=x2, h2=h2, h3=h3, mw=mw)
    return x3, wb, saved


def layer_bwd(l, dx3, dx3b, mem, tabs, plan, wb, ws, sv):
    n = lambda s: f'l{l}_{s}_bwd'
    gb, gs = {}, {}
    structs = lambda names: {k: _sds(plan.shapes[k], F32) for k in names}
    dh3, r, da = mlp_bwd(sv['h3'], dx3b, wb['w1'], wb['w2'], name=n('mlp'))
    gb['w1'] = mm(sv['h3'], da, 'tn', name=n('w1'), slots=NDEV)
    gb['w2'] = mm(r, dx3b, 'tn', name=n('w2'))
    dx2, dx2b, gs['norm_mlp'] = rmsnorm_bwd(sv['x2'], ws['norm_mlp'], dh3, dx3, name=n('norm_mlp'))
    dx1, dx1b, o_mem, dqr_mem, dkh, dvh, gs['mem_q_gain'], gs['norm_mem_q'] = memattn_bwd(
        sv['x1'], dx2, ws['norm_mem_q'], wb['mq'], ws['mem_q_gain'], sv['kh'], sv['vh'], wb['mo'], name=n('memattn'))
    gb['mo'] = mm(o_mem, dx2b, 'tn', name=n('mo'))
    gb['mq'] = mm(sv['h2'], dqr_mem, 'tn', name=n('mq'))
    gb['mk'], gb['mv'], gs['mem_k_gain'], gs['norm_mem_kv'] = memkv_bwd(
        mem, ws['norm_mem_kv'], wb['mk'], wb['mv'], ws['mem_k_gain'], dkh, dvh, name=n('memkv'))
    dyn = mm(dx1b, wb['w_out'], 'nt', name=n('w_out_dx'))
    gb['w_out'] = mm(sv['yn'], dx1b, 'tn', name=n('w_out'))
    dy_ssm, _, gs['g_ssm'] = rmsnorm_bwd(sv['y_ssm'], ws['g_ssm'], dyn, None, name=n('out_norm_ssm'), col=0)
    do, dot, delta, gs['g_mla'] = mla_out_bwd(sv['o'], dyn, ws['g_mla'], name=n('out_norm_mla'))
    late = {k: gb.pop(k) for k in ('w_out', 'mq', 'mk', 'mv', 'mo', 'w1', 'w2')}
    plan.late_grads(l, jax.linear_transpose(prep_late, structs(BIG_LATE))(late)[0])
    dq, dk, dv, *received = flash_bwd(sv['q'], sv['qt'], sv['k'], sv['kt'], sv['v'], do, dot, sv['lse'], delta,
                                      name=n('flash'), scatter=plan.scatter_src(l))
    plan.scattered(l, received)
    (dproj_m, cqn, ckvn, dqr, dkr, dvb, gs['q_norm'], gs['kv_norm'], gs['q_gain'], gs['k_gain']) = mla_prep_bwd(
        sv['proj'], tabs, sv['mw'], dq, dk, dv, name=n('mla_prep'))
    by_head = lambda g: g.reshape(g.shape[0], MLA_HEADS, LANES).transpose(1, 0, 2)
    gb['wq'] = by_head(mm(cqn, dqr, 'tn', name=n('wq')))
    gb['wk'] = by_head(mm(ckvn, dkr, 'tn', name=n('wk')))
    gb['wv'] = by_head(mm(ckvn, dvb, 'tn', name=n('wv')))
    dypre, yg, dz, gs['b_glu'] = glu_bwd(sv['ypre'], dy_ssm, wb['w_glu'], ws['b_glu'], name=n('glu'))
    gb['w_glu'] = mm(yg, dz, 'tn', name=n('w_glu'))
    du_p, gs['ar'], gs['ai'], gs['bre'], gs['bim'], gs['cre'], gs['cim'], gs['d'] = s5_bwd(sv['u_p'], _perm(dypre), ws, name=n('s5'))
    dprojb = jnp.concatenate([_unperm(du_p), dproj_m], axis=1)
    dh1 = mm(dprojb, wb['w_in'], 'nt', name=n('w_in_dx'))
    gb['w_in'] = mm(sv['h1'], dprojb, 'tn', name=n('w_in'))
    dx0, dx0b, gs['norm_mix'] = rmsnorm_bwd(sv['x'], ws['norm_mix'], dh1, dx1, name=n('norm_mix'))
    plan.early_grads(l, jax.linear_transpose(prep_early, structs(BIG_EARLY))(gb)[0])
    return dx0, dx0b, gs


def local_step(x, mem, positions, target, small, plan):
    t = x.shape[0]
    tabs = rope_tables(positions)
    layers = []
    for l in range(DEPTH):
        ws, small_vjp = jax.vjp(functools.partial(prep_small, t), {k: small[k][l] for k in SMALL})
        x, wb, sv = layer_fwd(l, x, mem, tabs, plan, ws)
        layers.append((wb, ws, small_vjp, sv))
    dx, dxb, lcols = loss_fwd_bwd(x, target, name='loss')
    loss = jnp.sum(lcols)
    gsmall = [None] * DEPTH
    for l in reversed(range(DEPTH)):
        wb, ws, small_vjp, sv = layers[l]
        dx, dxb, gs = layer_bwd(l, dx, dxb, mem, tabs, plan, wb, ws, sv)
        gs['pr'], gs['pi'] = jnp.zeros_like(ws['pr']), jnp.zeros_like(ws['pi'])
        gsmall[l] = small_vjp(gs)[0]
    return loss, dx, gsmall


class ExchangePlan:
    def __init__(self, shard_shapes, mine, first_early):
        self.shapes = {k: (s[1] * (NDEV if BIG_AXIS[k] == 1 else 1), s[2] * (NDEV if BIG_AXIS[k] == 2 else 1))
                       for k, s in shard_shapes.items()}
        self.shard = {k: s[1:] for k, s in shard_shapes.items()}
        self.shapes['mlp_w1'] = (NDEV,) + self.shard['mlp_w1']
        self.mine = mine
        self.w_early = {0: first_early}
        self.w_late = {}
        self.g_late, self.g_early = {}, {}
        self.r_late, self.r_early = {}, {}

    def _unpack(self, g, names):
        out, r0 = {}, 0
        for k in names:
            nr = math.prod(self.shard[k]) // D
            s = g[:, r0:r0 + nr]
            out[k] = (s.reshape(self.shapes[k]) if k == 'mlp_w1'
                      else _from_slots(s.reshape(NDEV, -1), (1,) + self.shard[k], BIG_AXIS[k])[0])
            r0 += nr
        return out

    def _pack(self, g, names, rows):
        slots = jnp.concatenate([g[k].reshape(NDEV, -1) if k == 'mlp_w1' else _to_slots(g[k][None], BIG_AXIS[k])
                                 for k in names], axis=1)
        return jnp.pad(slots, ((0, 0), (0, rows * D - slots.shape[1]))).astype(BF).reshape(NDEV, rows, D)

    def early(self, l):
        return self._unpack(self.w_early.pop(l), BIG_EARLY)

    def late(self, l):
        return self._unpack(self.w_late.pop(l), BIG_LATE)

    def gather_src(self, l):
        src = [self.mine[l, :LATE_ROWS]]
        if l + 1 < DEPTH:
            src.append(self.mine[l + 1, LATE_ROWS:])
        return tuple(src)

    def gathered(self, l, res):
        self.w_late[l] = res[0]
        if l + 1 < DEPTH:
            self.w_early[l + 1] = res[1]

    def late_grads(self, l, g):
        self.g_late[l] = self._pack(g, BIG_LATE, LATE_ROWS)

    def early_grads(self, l, g):
        self.g_early[l] = self._pack(g, BIG_EARLY, LAYER_ROWS - LATE_ROWS)

    def scatter_src(self, l):
        src = [self.g_late.pop(l)]
        if l + 1 < DEPTH:
            src.append(self.g_early.pop(l + 1))
        return tuple(src)

    def scattered(self, l, res):
        self.r_late[l] = res[0]
        if l + 1 < DEPTH:
            self.r_early[l + 1] = res[1]


def _peer(k):
    x, y, c = lax.axis_index('x'), lax.axis_index('y'), lax.axis_index('c')
    px, py, pc = x ^ ((k >> 2) & 1), y ^ ((k >> 1) & 1), c ^ (k & 1)
    return (px, py, pc), 4 * px + 2 * py + pc


def _copies(kind, src_ref, dst_ref, send_sems, recv_sems, loc_sem):
    _, me = _peer(0)
    src = (lambda p: src_ref.at[p]) if kind == 'scatter' else (lambda p: src_ref)
    local = pltpu.make_async_copy(src(me), dst_ref.at[me], loc_sem)
    sends, recvs = [], []
    for k in range(1, NDEV):
        dev, p = _peer(k)
        for slot, lst in ((me, sends), (p, recvs)):
            lst.append(pltpu.make_async_remote_copy(src_ref=src(p), dst_ref=dst_ref.at[slot], send_sem=send_sems.at[k - 1],
                                                    recv_sem=recv_sems.at[k - 1], device_id=dev,
                                                    device_id_type=pl.DeviceIdType.MESH))
    return local, sends, recvs


def _start_copies(cs):
    local, sends, _ = cs
    local.start()
    for cp in sends:
        cp.start()


def _wait_copies(cs):
    local, sends, recvs = cs
    for cp in sends:
        cp.wait_send()
    for cp in recvs:
        cp.wait_recv()
    local.wait()


_COMM_SCRATCH = (pltpu.SemaphoreType.DMA((NDEV - 1,)), pltpu.SemaphoreType.DMA((NDEV - 1,)), pltpu.SemaphoreType.DMA(()))
_ANY = pl.BlockSpec(memory_space=pl.ANY)


def exchange(scatters, gathers, *, name):
    ins = list(scatters) + list(gathers)
    kinds = ['scatter'] * len(scatters) + ['gather'] * len(gathers)
    n_in = len(ins)
    outs = [_sds(a.shape, a.dtype) for a in scatters] + [_sds((NDEV,) + b.shape, b.dtype) for b in gathers]

    def body(*refs):
        in_refs, out_refs, sems = refs[:n_in], refs[n_in:2 * n_in], refs[2 * n_in:]
        sets = [_copies(kind, in_refs[i], out_refs[i], *sems[3 * i:3 * i + 3]) for i, kind in enumerate(kinds)]
        for cs in sets:
            _start_copies(cs)
        for cs in sets:
            _wait_copies(cs)

    return pl.pallas_call(body, name=name, in_specs=[_ANY] * n_in, out_specs=[_ANY] * n_in, out_shape=outs,
                          scratch_shapes=list(_COMM_SCRATCH * n_in))(*ins)


def adamw(w, m, v, g8, *, name, tr):
    r = w.shape[0]
    c1 = 1.0 / (1.0 - ADAM_B1 ** ADAM_STEP)
    c2 = 1.0 / (1.0 - ADAM_B2 ** ADAM_STEP)

    def body(w_ref, m_ref, v_ref, g_ref, go_ref, d_ref, mo_ref, vo_ref):
        g = g_ref[0].astype(F32)
        for i in range(1, NDEV):
            g = g + g_ref[i].astype(F32)
        m_new = ADAM_B1 * m_ref[...] + (1.0 - ADAM_B1) * g
        v_new = ADAM_B2 * v_ref[...] + (1.0 - ADAM_B2) * (g * g)
        go_ref[...] = g
        mo_ref[...] = m_new
        vo_ref[...] = v_new
        d_ref[...] = -ADAM_LR * ((m_new * c1) / (jnp.sqrt(v_new * c2) + ADAM_EPS) + ADAM_WD * w_ref[...])

    row = pl.BlockSpec((tr, D), lambda i: (i, 0))
    return _call(body, name=name, grid=(r // tr,),
                 in_specs=[row, row, row, pl.BlockSpec((NDEV, tr, D), lambda i: (0, i, 0))],
                 out_specs=[row] * 4, out_shape=[_sds((r, D), F32)] * 4, sem=('parallel',), vmem=VMEM_BIG)(w, m, v, g8)


def _flat_rows(parts, rows):
    flat = jnp.concatenate([p.reshape(-1) for p in parts])
    return jnp.pad(flat, (0, rows * D - flat.shape[0])).reshape(rows, D)


def _unflat(flat2d, shapes):
    flat = flat2d.reshape(-1)
    out, off = [], 0
    for s in shapes:
        n = math.prod(s)
        out.append(flat[off:off + n].reshape(s))
        off += n
    return out


def _to_slots(g, axis):
    l, r, c = g.shape
    if axis == 1:
        return g.reshape(l, NDEV, r // NDEV, c).transpose(1, 0, 2, 3).reshape(NDEV, -1)
    return g.reshape(l, r, NDEV, c // NDEV).transpose(2, 0, 1, 3).reshape(NDEV, -1)


def _from_slots(s, shard_shape, axis):
    l, r, c = shard_shape
    s = s.reshape(NDEV, l, r, c)
    if axis == 1:
        return s.transpose(1, 0, 2, 3).reshape(l, NDEV * r, c)
    return s.transpose(1, 2, 0, 3).reshape(l, r, NDEV * c)


LATE_ROWS = 1280
LAYER_ROWS = 1536
BIG_ROWS = DEPTH * LAYER_ROWS
SMALL_ROWS = 640


def kernel(x, mem, positions, norm_mix, w_in, ssm_lambda_re, ssm_lambda_im, ssm_log_step, ssm_b_re, ssm_b_im, ssm_c_re, ssm_c_im, ssm_d, ssm_w_glu, ssm_b_glu, mla_q_norm, mla_w_uq, mla_kv_norm, mla_w_ukv, mla_q_gain, mla_k_gain, out_norm_ssm, out_norm_mla, w_out, norm_mem_q, norm_mem_kv, mem_w_q, mem_w_kv, mem_q_gain, mem_k_gain, mem_w_o, norm_mlp, mlp_w1, mlp_w2, loss_target, m_norm_mix, m_w_in, m_ssm_lambda_re, m_ssm_lambda_im, m_ssm_log_step, m_ssm_b_re, m_ssm_b_im, m_ssm_c_re, m_ssm_c_im, m_ssm_d, m_ssm_w_glu, m_ssm_b_glu, m_mla_q_norm, m_mla_w_uq, m_mla_kv_norm, m_mla_w_ukv, m_mla_q_gain, m_mla_k_gain, m_out_norm_ssm, m_out_norm_mla, m_w_out, m_norm_mem_q, m_norm_mem_kv, m_mem_w_q, m_mem_w_kv, m_mem_q_gain, m_mem_k_gain, m_mem_w_o, m_norm_mlp, m_mlp_w1, m_mlp_w2, v_norm_mix, v_w_in, v_ssm_lambda_re, v_ssm_lambda_im, v_ssm_log_step, v_ssm_b_re, v_ssm_b_im, v_ssm_c_re, v_ssm_c_im, v_ssm_d, v_ssm_w_glu, v_ssm_b_glu, v_mla_q_norm, v_mla_w_uq, v_mla_kv_norm, v_mla_w_ukv, v_mla_q_gain, v_mla_k_gain, v_out_norm_ssm, v_out_norm_mla, v_w_out, v_norm_mem_q, v_norm_mem_kv, v_mem_w_q, v_mem_w_kv, v_mem_q_gain, v_mem_k_gain, v_mem_w_o, v_norm_mlp, v_mlp_w1, v_mlp_w2):
    wvals = (norm_mix, w_in, ssm_lambda_re, ssm_lambda_im, ssm_log_step, ssm_b_re, ssm_b_im, ssm_c_re, ssm_c_im, ssm_d, ssm_w_glu, ssm_b_glu, mla_q_norm, mla_w_uq, mla_kv_norm, mla_w_ukv, mla_q_gain, mla_k_gain, out_norm_ssm, out_norm_mla, w_out, norm_mem_q, norm_mem_kv, mem_w_q, mem_w_kv, mem_q_gain, mem_k_gain, mem_w_o, norm_mlp, mlp_w1, mlp_w2)
    mvals = (m_norm_mix, m_w_in, m_ssm_lambda_re, m_ssm_lambda_im, m_ssm_log_step, m_ssm_b_re, m_ssm_b_im, m_ssm_c_re, m_ssm_c_im, m_ssm_d, m_ssm_w_glu, m_ssm_b_glu, m_mla_q_norm, m_mla_w_uq, m_mla_kv_norm, m_mla_w_ukv, m_mla_q_gain, m_mla_k_gain, m_out_norm_ssm, m_out_norm_mla, m_w_out, m_norm_mem_q, m_norm_mem_kv, m_mem_w_q, m_mem_w_kv, m_mem_q_gain, m_mem_k_gain, m_mem_w_o, m_norm_mlp, m_mlp_w1, m_mlp_w2)
    vvals = (v_norm_mix, v_w_in, v_ssm_lambda_re, v_ssm_lambda_im, v_ssm_log_step, v_ssm_b_re, v_ssm_b_im, v_ssm_c_re, v_ssm_c_im, v_ssm_d, v_ssm_w_glu, v_ssm_b_glu, v_mla_q_norm, v_mla_w_uq, v_mla_kv_norm, v_mla_w_ukv, v_mla_q_gain, v_mla_k_gain, v_out_norm_ssm, v_out_norm_mla, v_w_out, v_norm_mem_q, v_norm_mem_kv, v_mem_w_q, v_mem_w_kv, v_mem_q_gain, v_mem_k_gain, v_mem_w_o, v_norm_mlp, v_mlp_w1, v_mlp_w2)
    w = dict(zip(WEIGHTS, wvals))
    m = dict(zip(WEIGHTS, mvals))
    v = dict(zip(WEIGHTS, vvals))

    shard_shapes = {k: w[k].shape for k in BIG}
    layer_shapes = [shard_shapes[k][1:] for k in BIG]

    def layer_flat(parts):
        flat = jnp.concatenate([p.reshape(DEPTH, -1) for p in parts], axis=1)
        return jnp.pad(flat, ((0, 0), (0, LAYER_ROWS * D - flat.shape[1]))).reshape(DEPTH, LAYER_ROWS, D)

    mine = layer_flat([w[k].astype(BF) for k in BIG])
    first, = exchange([], [mine[0, LATE_ROWS:]], name='gather_early0')
    plan = ExchangePlan(shard_shapes, mine, first)
    small = {k: w[k] for k in SMALL}
    loss, grad_x, gsmall = local_step(x[0], mem[0], positions[0], loss_target[0], small, plan)
    gs_full = [jnp.stack([gsmall[l][k] for l in range(DEPTH)]) for k in SMALL]
    small_flat = _flat_rows(gs_full, SMALL_ROWS).astype(BF)
    plan.r_early[0], g8_small, losses = exchange([plan.g_early.pop(0)], [small_flat, jnp.full((8, LANES), loss, F32)],
                                                 name='exchange_last')
    loss_all = jnp.sum(losses[:, 0, 0])
    g8_big = jnp.concatenate([r[l] for l in range(DEPTH) for r in (plan.r_late, plan.r_early)], axis=1)

    small_shapes = [w[k].shape for k in SMALL]
    flat_big = lambda d: layer_flat([d[k] for k in BIG]).reshape(BIG_ROWS, D)
    gb, db, mb, vb = adamw(flat_big(w), flat_big(m), flat_big(v), g8_big, name='adamw_big', tr=256)
    gs, ds, ms, vs = adamw(_flat_rows([w[k] for k in SMALL], SMALL_ROWS), _flat_rows([m[k] for k in SMALL], SMALL_ROWS),
                           _flat_rows([v[k] for k in SMALL], SMALL_ROWS), g8_small, name='adamw_small', tr=128)

    def unflat_big(fb):
        fb, out, r0 = fb.reshape(DEPTH, LAYER_ROWS, D), [], 0
        for k, shp in zip(BIG, layer_shapes):
            nr = math.prod(shp) // D
            out.append(fb[:, r0:r0 + nr].reshape(shard_shapes[k]))
            r0 += nr
        return out

    res = {}
    for tag, fb, fs in (('g', gb, gs), ('d', db, ds), ('m', mb, ms), ('v', vb, vs)):
        res[tag] = dict(zip(BIG, unflat_big(fb)))
        res[tag].update(zip(SMALL, _unflat(fs, small_shapes)))
    return (loss_all, grad_x[None], *[res['g'][k] for k in WEIGHTS], *[res['d'][k] for k in WEIGHTS],
            *[res['m'][k] for k in WEIGHTS], *[res['v'][k] for k in WEIGHTS])
```

```python
import functools
import math

import jax
import jax.numpy as jnp
from jax import lax
from jax.experimental import pallas as pl
from jax.experimental.pallas import tpu as pltpu

F32 = jnp.float32
BF = jnp.bfloat16

D = 1024
DEPTH = 4
N_MEM = 256
MEM_HEADS = 4
MEM_HD = 64
SSM_W = 512
SSM_G = 32
SSM_H = 16
SSM_P = 64
MLA_HEADS = 8
QK_NOPE = 64
QK_ROPE = 32
QK_DIM = 96
V_DIM = 64
Q_LORA = 256
KV_LORA = 128
ROPE_THETA = 10000.0
D_FF = 4096
IN_COLS = 928
EPS = 1e-6
NDEV = 8
LANES = 128
SEGS = 32
S5_LW = 256
S5_NHB = (SSM_G * SSM_P) // S5_LW
ADAM_LR = 0.001
ADAM_B1 = 0.9
ADAM_B2 = 0.999
ADAM_EPS = 1e-08
ADAM_WD = 0.01
ADAM_STEP = 10
VMEM_BIG = 56 * 1024 * 1024

NN = (((1,), (0,)), ((), ()))
NT = (((1,), (1,)), ((), ()))
TN = (((0,), (0,)), ((), ()))

BIG_LATE = ('w_out', 'mem_w_q', 'mem_w_kv', 'mem_w_o', 'mlp_w1', 'mlp_w2')
BIG_EARLY = ('w_in', 'ssm_w_glu', 'mla_w_uq', 'mla_w_ukv')
BIG = BIG_LATE + BIG_EARLY
BIG_AXIS = {'w_in': 1, 'ssm_w_glu': 1, 'mla_w_uq': 2, 'mla_w_ukv': 2, 'w_out': 1, 'mem_w_q': 1, 'mem_w_kv': 1,
            'mem_w_o': 2, 'mlp_w1': 2, 'mlp_w2': 1}
SMALL = ('norm_mix', 'ssm_lambda_re', 'ssm_lambda_im', 'ssm_log_step', 'ssm_b_re', 'ssm_b_im', 'ssm_c_re', 'ssm_c_im',
         'ssm_d', 'ssm_b_glu', 'mla_q_norm', 'mla_kv_norm', 'mla_q_gain', 'mla_k_gain', 'out_norm_ssm', 'out_norm_mla',
         'norm_mem_q', 'norm_mem_kv', 'mem_q_gain', 'mem_k_gain', 'norm_mlp')
WEIGHTS = ('norm_mix', 'w_in', 'ssm_lambda_re', 'ssm_lambda_im', 'ssm_log_step', 'ssm_b_re', 'ssm_b_im', 'ssm_c_re',
           'ssm_c_im', 'ssm_d', 'ssm_w_glu', 'ssm_b_glu', 'mla_q_norm', 'mla_w_uq', 'mla_kv_norm', 'mla_w_ukv',
           'mla_q_gain', 'mla_k_gain', 'out_norm_ssm', 'out_norm_mla', 'w_out', 'norm_mem_q', 'norm_mem_kv', 'mem_w_q',
           'mem_w_kv', 'mem_q_gain', 'mem_k_gain', 'mem_w_o', 'norm_mlp', 'mlp_w1', 'mlp_w2')


def _call(body, *, name, out_shape, grid=(), in_specs=None, out_specs=None, scratch=(), sem=None, vmem=None):
    params = {}
    if sem is not None:
        params['dimension_semantics'] = sem
    if vmem is not None:
        params['vmem_limit_bytes'] = vmem
    specs = {} if in_specs is None else dict(grid=grid, in_specs=in_specs, out_specs=out_specs)
    return pl.pallas_call(body, name=name, out_shape=out_shape, scratch_shapes=list(scratch),
                          compiler_params=pltpu.CompilerParams(**params), **specs)


def _sds(shape, dtype):
    return jax.ShapeDtypeStruct(shape, dtype)


def _dot(a, b, dims=NN):
    return lax.dot_general(a.astype(BF), b.astype(BF), dims, preferred_element_type=F32)


def _split(a):
    hi = a.astype(BF)
    return hi, (a - hi.astype(F32)).astype(BF)


def _dot3(a, b, dims=NN):
    ah, al = _split(a)
    bh, bl = _split(b)
    d = lambda p, q: lax.dot_general(p, q, dims, preferred_element_type=F32)
    return d(ah, bh) + (d(ah, bl) + d(al, bh))


_sdot = _dot


def _rms(x, n):
    r = lax.rsqrt(jnp.sum(x * x, axis=-1, keepdims=True) * (1.0 / n) + EPS)
    return x * r, r


def _rms_bwd(xhat, r, dxhat, n):
    return r * (dxhat - xhat * (jnp.sum(dxhat * xhat, axis=-1, keepdims=True) * (1.0 / n)))


def _colsum(a):
    return jnp.sum(a, axis=0, keepdims=True)


def _tile(t, want):
    return min(t, want)


def _bidx(nb):
    return (lambda b: b) if nb > 1 else (lambda b: 0)


def mm(a, b, mode, *, name, out_dtype=F32, tm=1024, tn=1024, slots=0):
    squeeze = a.ndim == 2 and b.ndim == 2
    a = a[None] if a.ndim == 2 else a
    b = b[None] if b.ndim == 2 else b
    nb = max(a.shape[0], b.shape[0])
    ab, bb = _bidx(a.shape[0]), _bidx(b.shape[0])
    if mode in ('nn', 'nt'):
        m, k = a.shape[1:]
        n = b.shape[2] if mode == 'nn' else b.shape[1]
        tm, tn = _tile(m, tm), _tile(n, tn)
        dims = NN if mode == 'nn' else NT

        def body(a_ref, b_ref, o_ref):
            o_ref[...] = _dot(a_ref[...], b_ref[...], dims).astype(o_ref.dtype)

        bspec = (pl.BlockSpec((None, k, tn), lambda bi, i, j: (bb(bi), 0, j)) if mode == 'nn'
                 else pl.BlockSpec((None, tn, k), lambda bi, i, j: (bb(bi), j, 0)))
        out = _call(body, name=name, grid=(nb, m // tm, n // tn),
                    in_specs=[pl.BlockSpec((None, tm, k), lambda bi, i, j: (ab(bi), i, 0)), bspec],
                    out_specs=pl.BlockSpec((None, tm, tn), lambda bi, i, j: (bi, i, j)),
                    out_shape=_sds((nb, m, n), out_dtype), sem=('parallel', 'parallel', 'parallel'), vmem=VMEM_BIG)(a, b)
    else:
        k, m = a.shape[1:]
        n = b.shape[2]
        tm, tn, tk = _tile(m, 1024), _tile(n, 1024), _tile(k, 512)
        per = 1
        if slots:
            ts = n // slots
            per = tn // ts
            out_spec, out_shape = pl.BlockSpec((per, tm, ts), lambda bi, i, j, kk: (j, i, 0)), _sds((slots, m, ts), F32)
        else:
            out_spec, out_shape = pl.BlockSpec((None, tm, tn), lambda bi, i, j, kk: (bi, i, j)), _sds((nb, m, n), F32)

        def body(a_ref, b_ref, o_ref):
            @pl.when(pl.program_id(3) == 0)
            def _():
                o_ref[...] = jnp.zeros_like(o_ref)

            res = _dot(a_ref[...], b_ref[...], TN)
            if slots:
                for s in range(per):
                    o_ref[s] += res[:, s * ts:(s + 1) * ts]
            else:
                o_ref[...] += res

        out = _call(body, name=name, grid=(nb, m // tm, n // tn, k // tk),
                    in_specs=[pl.BlockSpec((None, tk, tm), lambda bi, i, j, kk: (ab(bi), kk, i)),
                              pl.BlockSpec((None, tk, tn), lambda bi, i, j, kk: (bb(bi), kk, j))],
                    out_specs=out_spec, out_shape=out_shape,
                    sem=('parallel', 'parallel', 'parallel', 'arbitrary'), vmem=VMEM_BIG)(a, b)
    return out[0] if squeeze and not slots else out


def rmsnorm_fwd(x, g, *, name, tq=512):
    t, d = x.shape
    tq = _tile(t, tq)

    def body(x_ref, g_ref, o_ref):
        xh, _ = _rms(x_ref[...], d)
        o_ref[...] = (xh * g_ref[...]).astype(o_ref.dtype)

    return _call(body, name=name, grid=(t // tq,),
                 in_specs=[pl.BlockSpec((tq, d), lambda i: (i, 0)), pl.BlockSpec((1, d), lambda i: (0, 0))],
                 out_specs=pl.BlockSpec((tq, d), lambda i: (i, 0)), out_shape=_sds((t, d), BF), sem=('parallel',))(x, g)


def rmsnorm_bwd(x, g, dh, dres, *, name, col=0, tq=512):
    t, d = x.shape
    tq = _tile(t, tq)
    has_res = dres is not None

    def body(*refs):
        if has_res:
            x_ref, g_ref, dh_ref, dres_ref, dx_ref, dxb_ref, dg_ref = refs
        else:
            x_ref, g_ref, dh_ref, dx_ref, dxb_ref, dg_ref = refs
        xh, r = _rms(x_ref[...], d)
        dh_ = dh_ref[...].astype(F32)
        dx = _rms_bwd(xh, r, dh_ * g_ref[...], d)
        if has_res:
            dx = dx + dres_ref[...]
        dx_ref[...] = dx
        dxb_ref[...] = dx.astype(BF)

        @pl.when(pl.program_id(0) == 0)
        def _():
            dg_ref[...] = jnp.zeros_like(dg_ref)

        dg_ref[...] += _colsum(dh_ * xh)

    in_specs = [pl.BlockSpec((tq, d), lambda i: (i, 0)), pl.BlockSpec((1, d), lambda i: (0, 0)),
                pl.BlockSpec((tq, d), lambda i: (i, col))]
    args = [x, g, dh]
    if has_res:
        in_specs.append(pl.BlockSpec((tq, d), lambda i: (i, 0)))
        args.append(dres)
    row = pl.BlockSpec((tq, d), lambda i: (i, 0))
    return _call(body, name=name, grid=(t // tq,), in_specs=in_specs,
                 out_specs=[row, row, pl.BlockSpec((1, d), lambda i: (0, 0))],
                 out_shape=[_sds((t, d), F32), _sds((t, d), BF), _sds((1, d), F32)], sem=('arbitrary',))(*args)


def _cmul(ar, ai, xr, xi):
    return ar * xr - ai * xi, ar * xi + ai * xr


def _seg_carries(er, ei, pr, pi, reverse):
    lw = er.shape[1]
    zero = jnp.zeros((1, lw), F32)
    order = range(SEGS - 1, -1, -1) if reverse else range(SEGS)
    cin_r, cin_i = [None] * SEGS, [None] * SEGS
    tr, ti = zero, zero
    for j in order:
        cin_r[j], cin_i[j] = tr, ti
        mr, mi = _cmul(pr, pi, tr, ti)
        tr, ti = er[j:j + 1, :] + mr, ei[j:j + 1, :] + mi
    return jnp.concatenate(cin_r, axis=0), jnp.concatenate(cin_i, axis=0)


def _s5_chunk(t):
    return _tile(t, 512)


def s5_fwd(u_p, prm, *, name):
    t = u_p.shape[0]
    ch = _s5_chunk(t)
    nch, steps = t // ch, ch // SEGS
    lw = S5_LW

    def body(u_ref, ar_ref, ai_ref, pr_ref, pi_ref, bre_ref, bim_ref, cre_ref, cim_ref, d_ref, y_ref, bur, bui):
        hb = pl.program_id(0)
        ar = jnp.broadcast_to(ar_ref[0], (SEGS, lw))
        ai = jnp.broadcast_to(ai_ref[0], (SEGS, lw))

        def rows_of(c):
            return pl.ds(pl.multiple_of(c * ch, ch), ch)

        @pl.loop(0, nch)
        def _(c):
            u = u_ref[rows_of(c), :]
            bur[rows_of(c), :] = _sdot(u, bre_ref[0])
            bui[rows_of(c), :] = _sdot(u, bim_ref[0])

        def scan(carry, store):
            def step(i, s):
                r0 = pl.multiple_of(i * SEGS, SEGS)
                mr, mi = _cmul(ar, ai, s[0], s[1])
                nr, ni = mr + bur[pl.ds(r0, SEGS), :], mi + bui[pl.ds(r0, SEGS), :]
                if store:
                    bur[pl.ds(r0, SEGS), :] = nr
                    bui[pl.ds(r0, SEGS), :] = ni
                return nr, ni

            return lax.fori_loop(0, t // SEGS, step, carry, unroll=8)

        zero = jnp.zeros((SEGS, lw), F32)
        er, ei = scan((zero, zero), False)
        scan(_seg_carries(er, ei, pr_ref[0], pi_ref[0], False), True)

        @pl.loop(0, nch)
        def _(c):
            rows = rows_of(c)
            y = _sdot(bur[rows, :], cre_ref[0]) - _sdot(bui[rows, :], cim_ref[0])

            @pl.when(hb % 2 == 0)
            def _():
                y_ref[rows, :] = y + d_ref[...] * u_ref[rows, :]

            @pl.when(hb % 2 == 1)
            def _():
                y_ref[rows, :] += y

    vec = pl.BlockSpec((1, 1, lw), lambda h: (h, 0, 0))
    return _call(
        body, name=name, grid=(S5_NHB,),
        in_specs=[pl.BlockSpec((t, LANES), lambda h: (0, h // 2)), vec, vec, vec, vec,
                  pl.BlockSpec((1, LANES, lw), lambda h: (h, 0, 0)), pl.BlockSpec((1, LANES, lw), lambda h: (h, 0, 0)),
                  pl.BlockSpec((1, lw, LANES), lambda h: (h, 0, 0)), pl.BlockSpec((1, lw, LANES), lambda h: (h, 0, 0)),
                  pl.BlockSpec((1, LANES), lambda h: (0, h // 2))],
        out_specs=pl.BlockSpec((t, LANES), lambda h: (0, h // 2)), out_shape=_sds((t, SSM_W), F32),
        scratch=[pltpu.VMEM((t, lw), F32)] * 2, sem=('arbitrary',), vmem=VMEM_BIG,
    )(u_p, prm['ar'], prm['ai'], prm['pr'], prm['pi'], prm['bre'], prm['bim'], prm['cre'], prm['cim'], prm['d'])


def s5_bwd(u_p, dy_p, prm, *, name):
    t = u_p.shape[0]
    ch = _s5_chunk(t)
    nch, steps = t // ch, ch // SEGS
    lw = S5_LW

    def body(u_ref, dy_ref, ar_ref, ai_ref, pr_ref, pi_ref, bre_ref, bim_ref, cre_ref, cim_ref, d_ref,
             du_ref, dar_ref, dai_ref, dbre_ref, dbim_ref, dcre_ref, dcim_ref, dd_ref, bur, bui, sr, si, du_acc):
        hb = pl.program_id(0)
        ar = jnp.broadcast_to(ar_ref[0], (SEGS, lw))
        ai = jnp.broadcast_to(ai_ref[0], (SEGS, lw))
        zero = jnp.zeros((SEGS, lw), F32)

        def rows_of(c):
            return pl.ds(pl.multiple_of(c * ch, ch), ch)

        nsteps = t // SEGS

        @pl.loop(0, nch)
        def _(c):
            u = u_ref[rows_of(c), :]
            bur[rows_of(c), :] = _sdot(u, bre_ref[0])
            bui[rows_of(c), :] = _sdot(u, bim_ref[0])

        def fwd_scan(carry, store):
            def step(i, s):
                r0 = pl.multiple_of(i * SEGS, SEGS)
                mr, mi = _cmul(ar, ai, s[0], s[1])
                nr, ni = mr + bur[pl.ds(r0, SEGS), :], mi + bui[pl.ds(r0, SEGS), :]
                if store:
                    w0 = pl.multiple_of(i * SEGS + SEGS, SEGS)
                    sr[pl.ds(w0, SEGS), :] = nr
                    si[pl.ds(w0, SEGS), :] = ni
                return nr, ni

            return lax.fori_loop(0, nsteps, step, carry, unroll=8)

        er, ei = fwd_scan((zero, zero), False)
        cin_r, cin_i = _seg_carries(er, ei, pr_ref[0], pi_ref[0], False)
        sr[pl.ds(0, SEGS), :] = cin_r
        si[pl.ds(0, SEGS), :] = cin_i
        fwd_scan((cin_r, cin_i), True)

        @pl.loop(0, nch)
        def _(c):
            dy = dy_ref[rows_of(c), :]
            bur[rows_of(c), :] = _sdot(dy, cre_ref[0], NT)
            bui[rows_of(c), :] = -_sdot(dy, cim_ref[0], NT)

        def rev_local(ii, lam):
            r0 = pl.multiple_of((nsteps - 1 - ii) * SEGS, SEGS)
            mr, mi = _cmul(ar, -ai, lam[0], lam[1])
            return mr + bur[pl.ds(r0, SEGS), :], mi + bui[pl.ds(r0, SEGS), :]

        lr0, li0 = lax.fori_loop(0, nsteps, rev_local, (zero, zero), unroll=8)
        rin = _seg_carries(lr0, li0, pr_ref[0], -pi_ref[0], True)

        def rev_step(ii, st):
            lam_r, lam_i, acc_r, acc_i = st
            r0 = pl.multiple_of((nsteps - 1 - ii) * SEGS, SEGS)
            mr, mi = _cmul(ar, -ai, lam_r, lam_i)
            nr, ni = mr + bur[pl.ds(r0, SEGS), :], mi + bui[pl.ds(r0, SEGS), :]
            bur[pl.ds(r0, SEGS), :] = nr
            bui[pl.ds(r0, SEGS), :] = ni
            pr_, pi_ = sr[pl.ds(r0, SEGS), :], si[pl.ds(r0, SEGS), :]
            return nr, ni, acc_r + (nr * pr_ + ni * pi_), acc_i + (ni * pr_ - nr * pi_)

        _, _, acc_r, acc_i = lax.fori_loop(0, nsteps, rev_step, (rin[0], rin[1], zero, zero), unroll=8)
        dar_ref[0] = _colsum(acc_r)
        dai_ref[0] = _colsum(acc_i)

        dbre_ref[...] = jnp.zeros_like(dbre_ref)
        dbim_ref[...] = jnp.zeros_like(dbim_ref)
        dcre_ref[...] = jnp.zeros_like(dcre_ref)
        dcim_ref[...] = jnp.zeros_like(dcim_ref)

        @pl.loop(0, nch)
        def _(c):
            rows = rows_of(c)
            u = u_ref[rows, :]
            dy = dy_ref[rows, :]
            lam_r, lam_i = bur[rows, :], bui[rows, :]
            du = _sdot(lam_r, bre_ref[0], NT) + _sdot(lam_i, bim_ref[0], NT)

            @pl.when(hb % 2 == 0)
            def _():
                du_acc[rows, :] = du + d_ref[...] * dy

            @pl.when(hb % 2 == 1)
            def _():
                du_ref[rows, :] = (du_acc[rows, :] + du).astype(BF)

            dbre_ref[0] += _sdot(u, lam_r, TN)
            dbim_ref[0] += _sdot(u, lam_i, TN)
            srows = pl.ds(pl.multiple_of(c * ch + SEGS, SEGS), ch)
            dcre_ref[0] += _sdot(sr[srows, :], dy, TN)
            dcim_ref[0] -= _sdot(si[srows, :], dy, TN)

        @pl.when(hb % 2 == 0)
        def _():
            dd_ref[...] = _colsum(dy_ref[...] * u_ref[...])

    vec = pl.BlockSpec((1, 1, lw), lambda h: (h, 0, 0))
    bsp = pl.BlockSpec((1, LANES, lw), lambda h: (h, 0, 0))
    csp = pl.BlockSpec((1, lw, LANES), lambda h: (h, 0, 0))
    act = pl.BlockSpec((t, LANES), lambda h: (0, h // 2))
    dsp = pl.BlockSpec((1, LANES), lambda h: (0, h // 2))
    return _call(
        body, name=name, grid=(S5_NHB,),
        in_specs=[act, act, vec, vec, vec, vec, bsp, bsp, csp, csp, dsp],
        out_specs=[act, vec, vec, bsp, bsp, csp, csp, dsp],
        out_shape=[_sds((t, SSM_W), BF), _sds((S5_NHB, 1, lw), F32), _sds((S5_NHB, 1, lw), F32),
                   _sds((S5_NHB, LANES, lw), F32), _sds((S5_NHB, LANES, lw), F32),
                   _sds((S5_NHB, lw, LANES), F32), _sds((S5_NHB, lw, LANES), F32), _sds((1, SSM_W), F32)],
        scratch=[pltpu.VMEM((t, lw), F32), pltpu.VMEM((t, lw), F32),
                 pltpu.VMEM((t + SEGS, lw), F32), pltpu.VMEM((t + SEGS, lw), F32), pltpu.VMEM((t, LANES), F32)],
        sem=('arbitrary',), vmem=VMEM_BIG,
    )(u_p, dy_p, prm['ar'], prm['ai'], prm['pr'], prm['pi'], prm['bre'], prm['bim'], prm['cre'], prm['cim'], prm['d'])


def s5_prep(t, lam_re, lam_im, log_step, b_re, b_im, c_re, c_im):
    step = jnp.exp(log_step)[:, None]
    mag = jnp.exp(lam_re * step)
    ar, ai = mag * jnp.cos(lam_im * step), mag * jnp.sin(lam_im * step)
    den = lam_re * lam_re + lam_im * lam_im
    nr, ni = ar - 1.0, ai
    fr, fi = (nr * lam_re + ni * lam_im) / den, (ni * lam_re - nr * lam_im) / den
    bbr = fr[..., None] * b_re - fi[..., None] * b_im
    bbi = fr[..., None] * b_im + fi[..., None] * b_re
    gl = S5_LW // SSM_P
    eye = jnp.eye(gl, dtype=F32)
    half = (jnp.arange(S5_NHB) % 2)[:, None, None]

    def bmat(bb):
        x = bb.transpose(0, 2, 1).reshape(S5_NHB, gl, SSM_H, SSM_P)
        x = jnp.einsum('bghp,gk->bghkp', x, eye).reshape(S5_NHB, gl * SSM_H, S5_LW)
        z = jnp.zeros_like(x)
        return jnp.where(half == 0, jnp.concatenate([x, z], axis=1), jnp.concatenate([z, x], axis=1))

    def cmat(cc):
        x = cc.transpose(0, 2, 1).reshape(S5_NHB, gl, SSM_P, SSM_H)
        x = jnp.einsum('bgph,gk->bgpkh', x, eye).reshape(S5_NHB, S5_LW, gl * SSM_H)
        z = jnp.zeros_like(x)
        return jnp.where(half == 0, jnp.concatenate([x, z], axis=2), jnp.concatenate([z, x], axis=2))

    vec = lambda a: a.reshape(S5_NHB, 1, S5_LW)
    ni_steps = float(t // SEGS)
    pmag = jnp.exp(lam_re * step * ni_steps)
    pr, pi = pmag * jnp.cos(lam_im * step * ni_steps), pmag * jnp.sin(lam_im * step * ni_steps)
    return dict(ar=vec(ar), ai=vec(ai), bre=bmat(bbr), bim=bmat(bbi), cre=cmat(c_re), cim=cmat(c_im),
                pr=lax.stop_gradient(vec(pr)), pi=lax.stop_gradient(vec(pi)))


def _gelu(x):
    c = math.sqrt(2.0 / math.pi)
    return 0.5 * x * (1.0 + jnp.tanh(c * (x + 0.044715 * (x * x * x))))


def _gelu_grad(x):
    c = math.sqrt(2.0 / math.pi)
    th = jnp.tanh(c * (x + 0.044715 * (x * x * x)))
    return 0.5 * (1.0 + th) + 0.5 * x * (1.0 - th * th) * (c * (1.0 + 3.0 * 0.044715 * (x * x)))


def glu_fwd(ypre, w_glu, b_glu, *, name, tq=512):
    t = ypre.shape[0]
    tq = _tile(t, tq)

    def body(y_ref, w_ref, b_ref, o_ref):
        yg = _gelu(y_ref[...])
        z = _dot(yg, w_ref[...]) + b_ref[...]
        o_ref[...] = yg * jax.nn.sigmoid(z)

    return _call(body, name=name, grid=(t // tq,),
                 in_specs=[pl.BlockSpec((tq, SSM_W), lambda i: (i, 0)), pl.BlockSpec((SSM_W, SSM_W), lambda i: (0, 0)),
                           pl.BlockSpec((1, SSM_W), lambda i: (0, 0))],
                 out_specs=pl.BlockSpec((tq, SSM_W), lambda i: (i, 0)), out_shape=_sds((t, SSM_W), F32),
                 sem=('parallel',))(ypre, w_glu, b_glu)


def glu_bwd(ypre, dy, w_glu, b_glu, *, name, tq=512):
    t = ypre.shape[0]
    tq = _tile(t, tq)

    def body(y_ref, dy_ref, w_ref, b_ref, dyp_ref, yg_ref, dz_ref, db_ref):
        ypre_ = y_ref[...]
        yg = _gelu(ypre_)
        sig = jax.nn.sigmoid(_dot(yg, w_ref[...]) + b_ref[...])
        dy_ = dy_ref[...]
        dz = dy_ * yg * sig * (1.0 - sig)
        dyg = dy_ * sig + _dot(dz, w_ref[...], NT)
        dyp_ref[...] = dyg * _gelu_grad(ypre_)
        yg_ref[...] = yg.astype(BF)
        dz_ref[...] = dz.astype(BF)

        @pl.when(pl.program_id(0) == 0)
        def _():
            db_ref[...] = jnp.zeros_like(db_ref)

        db_ref[...] += _colsum(dz)

    row = pl.BlockSpec((tq, SSM_W), lambda i: (i, 0))
    vec = pl.BlockSpec((1, SSM_W), lambda i: (0, 0))
    return _call(body, name=name, grid=(t // tq,),
                 in_specs=[row, row, pl.BlockSpec((SSM_W, SSM_W), lambda i: (0, 0)), vec],
                 out_specs=[row, row, row, vec],
                 out_shape=[_sds((t, SSM_W), F32), _sds((t, SSM_W), BF), _sds((t, SSM_W), BF), _sds((1, SSM_W), F32)],
                 sem=('arbitrary',))(ypre, dy, w_glu, b_glu)


def _rope(x, cos, sa, sb):
    return x * cos + pltpu.roll(x, 16, 1) * sa + pltpu.roll(x, 112, 1) * sb


def _rope_t(d, cos, sa, sb):
    return d * cos + pltpu.roll(d * sa, 112, 1) + pltpu.roll(d * sb, 16, 1)


def rope_tables(positions):
    half = QK_ROPE // 2
    inv_freq = ROPE_THETA ** (-jnp.arange(half, dtype=F32) / half)
    ang = positions.astype(F32)[:, None] * inv_freq
    cos, sin = jnp.cos(ang), jnp.sin(ang)
    t = positions.shape[0]
    one, zero = jnp.ones((t, QK_NOPE), F32), jnp.zeros((t, QK_NOPE), F32)
    pad1, pad0 = jnp.ones((t, 32), F32), jnp.zeros((t, 32), F32)
    z16 = jnp.zeros((t, half), F32)
    return (jnp.concatenate([one, cos, cos, pad1], axis=1), jnp.concatenate([zero, z16, sin, pad0], axis=1),
            jnp.concatenate([zero, -sin, z16, pad0], axis=1))


def mla_prep_fwd(proj, tabs, w, *, name):
    t = proj.shape[0]
    tq = _tile(t, ATT_BLK)

    def body(cq_ref, ckv_ref, kr_ref, cos_ref, sa_ref, sb_ref, qn_ref, kvn_ref, wq_ref, wk_ref, wv_ref, qg_ref, kg_ref,
             q_ref, qt_ref, k_ref, kt_ref, v_ref):
        cqn = (_rms(cq_ref[...], Q_LORA)[0] * qn_ref[...]).astype(BF)
        ckvn = (_rms(ckv_ref[...], KV_LORA)[0] * kvn_ref[...]).astype(BF)
        cos, sa, sb = cos_ref[...], sa_ref[...], sb_ref[...]
        kr = kr_ref[...]
        for h in range(MLA_HEADS):
            q = _rms(_dot(cqn, wq_ref[h]), QK_DIM)[0] * qg_ref[...]
            q = _rope(q, cos, sa, sb) * ATT_SCALE
            q_ref[h] = q.astype(BF)
            qt_ref[h, 0] = q.T.astype(BF)
            k = _rms(_dot(ckvn, wk_ref[h]) + kr, QK_DIM)[0] * kg_ref[...]
            k = _rope(k, cos, sa, sb)
            k_ref[h] = k.astype(BF)
            kt_ref[h, 0] = k.T.astype(BF)
            v_ref[h] = _dot(ckvn, wv_ref[h]).astype(BF)

    tab = pl.BlockSpec((tq, LANES), lambda i: (i, 0))
    full = lambda shape: pl.BlockSpec(shape, lambda i: (0,) * len(shape))
    hout = pl.BlockSpec((MLA_HEADS, tq, LANES), lambda i: (0, i, 0))
    tout = pl.BlockSpec((MLA_HEADS, 1, LANES, tq), lambda i: (0, i, 0, 0))
    hshape = _sds((MLA_HEADS, t, LANES), BF)
    tshape = _sds((MLA_HEADS, t // tq, LANES, tq), BF)
    return _call(
        body, name=name, grid=(t // tq,),
        in_specs=[pl.BlockSpec((tq, Q_LORA), lambda i: (i, 2)), pl.BlockSpec((tq, LANES), lambda i: (i, 6)),
                  pl.BlockSpec((tq, LANES), lambda i: (i, 7)), tab, tab, tab,
                  full((1, Q_LORA)), full((1, KV_LORA)), full((MLA_HEADS, Q_LORA, LANES)),
                  full((MLA_HEADS, KV_LORA, LANES)), full((MLA_HEADS, KV_LORA, LANES)), full((1, LANES)), full((1, LANES))],
        out_specs=[hout, tout, hout, tout, hout], out_shape=[hshape, tshape, hshape, tshape, hshape], sem=('parallel',),
    )(proj, proj, proj, *tabs, w['q_norm'], w['kv_norm'], w['wq'], w['wk'], w['wv'], w['q_gain'], w['k_gain'])


def mla_prep_bwd(proj, tabs, w, dq, dk, dv, *, name):
    t = proj.shape[0]
    tq = _tile(t, ATT_BLK)

    def body(cq_ref, ckv_ref, kr_ref, cos_ref, sa_ref, sb_ref, qn_ref, kvn_ref, wq_ref, wk_ref, wv_ref, qg_ref, kg_ref,
             dq_ref, dk_ref, dv_ref,
             dpm_ref, cqn_ref, ckvn_ref, dqr_ref, dkraw_ref, dvb_ref, dqn_ref, dkvn_ref, dqg_ref, dkg_ref):
        cq_h, cq_r = _rms(cq_ref[...], Q_LORA)
        ckv_h, ckv_r = _rms(ckv_ref[...], KV_LORA)
        cqn = (cq_h * qn_ref[...]).astype(BF)
        ckvn = (ckv_h * kvn_ref[...]).astype(BF)
        cqn_ref[...] = cqn
        ckvn_ref[...] = ckvn
        cos, sa, sb = cos_ref[...], sa_ref[...], sb_ref[...]
        kr = kr_ref[...]
        dcqn = jnp.zeros((tq, Q_LORA), F32)
        dckvn = jnp.zeros((tq, KV_LORA), F32)
        dkrope = jnp.zeros((tq, LANES), F32)
        dqg = jnp.zeros((1, LANES), F32)
        dkg = jnp.zeros((1, LANES), F32)
        for h in range(MLA_HEADS):
            qh, qr = _rms(_dot(cqn, wq_ref[h]), QK_DIM)
            dqo = _rope_t(dq_ref[h, 0].T * ATT_SCALE, cos, sa, sb)
            dqg = dqg + _colsum(dqo * qh)
            dqraw = _rms_bwd(qh, qr, dqo * qg_ref[...], QK_DIM).astype(BF)
            dqr_ref[:, h * LANES:(h + 1) * LANES] = dqraw
            dcqn = dcqn + _dot(dqraw, wq_ref[h], NT)
            kh, krs = _rms(_dot(ckvn, wk_ref[h]) + kr, QK_DIM)
            dko = _rope_t(dk_ref[h], cos, sa, sb)
            dkg = dkg + _colsum(dko * kh)
            dkraw = _rms_bwd(kh, krs, dko * kg_ref[...], QK_DIM)
            dkrope = dkrope + dkraw
            dkraw = dkraw.astype(BF)
            dkraw_ref[:, h * LANES:(h + 1) * LANES] = dkraw
            dvb = dv_ref[h].astype(BF)
            dvb_ref[:, h * LANES:(h + 1) * LANES] = dvb
            dckvn = dckvn + _dot(dkraw, wk_ref[h], NT) + _dot(dvb, wv_ref[h], NT)
        dpm_ref[:, 0:Q_LORA] = _rms_bwd(cq_h, cq_r, dcqn * qn_ref[...], Q_LORA).astype(BF)
        dpm_ref[:, Q_LORA:Q_LORA + KV_LORA] = _rms_bwd(ckv_h, ckv_r, dckvn * kvn_ref[...], KV_LORA).astype(BF)
        dpm_ref[:, Q_LORA + KV_LORA:512] = dkrope.astype(BF)

        @pl.when(pl.program_id(0) == 0)
        def _():
            dqn_ref[...] = jnp.zeros_like(dqn_ref)
            dkvn_ref[...] = jnp.zeros_like(dkvn_ref)
            dqg_ref[...] = jnp.zeros_like(dqg_ref)
            dkg_ref[...] = jnp.zeros_like(dkg_ref)

        dqn_ref[...] += _colsum(dcqn * cq_h)
        dkvn_ref[...] += _colsum(dckvn * ckv_h)
        dqg_ref[...] += dqg
        dkg_ref[...] += dkg

    tab = pl.BlockSpec((tq, LANES), lambda i: (i, 0))
    full = lambda shape: pl.BlockSpec(shape, lambda i: (0,) * len(shape))
    hblk = pl.BlockSpec((MLA_HEADS, tq, LANES), lambda i: (0, i, 0))
    wide = pl.BlockSpec((tq, MLA_HEADS * LANES), lambda i: (i, 0))
    return _call(
        body, name=name, grid=(t // tq,),
        in_specs=[pl.BlockSpec((tq, Q_LORA), lambda i: (i, 2)), pl.BlockSpec((tq, LANES), lambda i: (i, 6)),
                  pl.BlockSpec((tq, LANES), lambda i: (i, 7)), tab, tab, tab,
                  full((1, Q_LORA)), full((1, KV_LORA)), full((MLA_HEADS, Q_LORA, LANES)),
                  full((MLA_HEADS, KV_LORA, LANES)), full((MLA_HEADS, KV_LORA, LANES)), full((1, LANES)), full((1, LANES)),
                  pl.BlockSpec((MLA_HEADS, 1, LANES, tq), lambda i: (0, i, 0, 0)), hblk, hblk],
        out_specs=[pl.BlockSpec((tq, 512), lambda i: (i, 0)),
                   pl.BlockSpec((tq, Q_LORA), lambda i: (i, 0)), pl.BlockSpec((tq, KV_LORA), lambda i: (i, 0)),
                   wide, wide, wide, full((1, Q_LORA)), full((1, KV_LORA)), full((1, LANES)), full((1, LANES))],
        out_shape=[_sds((t, 512), BF), _sds((t, Q_LORA), BF), _sds((t, KV_LORA), BF),
                   _sds((t, MLA_HEADS * LANES), BF), _sds((t, MLA_HEADS * LANES), BF), _sds((t, MLA_HEADS * LANES), BF),
                   _sds((1, Q_LORA), F32), _sds((1, KV_LORA), F32), _sds((1, LANES), F32), _sds((1, LANES), F32)],
        sem=('arbitrary',),
    )(proj, proj, proj, *tabs, w['q_norm'], w['kv_norm'], w['wq'], w['wk'], w['wv'], w['q_gain'], w['k_gain'], dq, dk, dv)


ATT_BLK = 256
ATT_SCALE = 1.0 / math.sqrt(QK_DIM)


def _overlapped(grid, make_copies):
    ids = [pl.program_id(a) for a in range(len(grid))]
    first = functools.reduce(jnp.logical_and, [i == 0 for i in ids])
    last = functools.reduce(jnp.logical_and, [i == n - 1 for i, n in zip(ids, grid)])

    @pl.when(first)
    def _():
        for cs in make_copies():
            _start_copies(cs)

    @pl.when(last)
    def _():
        for cs in make_copies():
            _wait_copies(cs)


def flash_fwd(q, kt, v, *, name, gather=()):
    t = q.shape[1]
    blk = _tile(t, ATT_BLK)
    grid = (MLA_HEADS // 2, t // blk)

    def body(q_ref, kt_ref, v_ref, *rest):
        nc = len(gather)
        srcs, (o_ref, lse_ref), dsts, sems = rest[:nc], rest[nc:nc + 2], rest[nc + 2:2 * nc + 2], rest[2 * nc + 2:]
        if nc:
            _overlapped(grid, lambda: [_copies('gather', srcs[i], dsts[i], *sems[3 * i:3 * i + 3]) for i in range(nc)])
        qi = pl.program_id(1)
        row = lax.broadcasted_iota(jnp.int32, (blk, blk), 0)
        col = lax.broadcasted_iota(jnp.int32, (blk, blk), 1)

        def block(j, carry, masked):
            out = []
            for hh in range(2):
                m, l, acc = carry[hh]
                s = _dot(q_ref[hh], kt_ref[hh, j])
                if masked:
                    s = jnp.where(col <= row, s, -jnp.inf)
                m2 = jnp.maximum(m, jnp.max(s, axis=-1, keepdims=True))
                p = jnp.exp(s - m2)
                alpha = jnp.exp(m - m2)
                rows = pl.ds(pl.multiple_of(j * blk, blk), blk)
                out.append((m2, alpha * l + jnp.sum(p, axis=-1, keepdims=True), alpha * acc + _dot(p, v_ref[hh, rows, :])))
            return tuple(out)

        init = (jnp.full((blk, 1), -jnp.inf, F32), jnp.zeros((blk, 1), F32), jnp.zeros((blk, LANES), F32))
        carry = lax.fori_loop(0, qi, lambda j, c: block(j, c, False), (init, init))
        carry = block(qi, carry, True)
        o_acc = jnp.zeros((blk, LANES), F32)
        for hh in range(2):
            m, l, acc = carry[hh]
            o_acc = o_acc + acc / l
            lse_ref[hh, 0] = jnp.broadcast_to(m + jnp.log(l), (blk, LANES)).T[0:1, :]
        o_ref[...] = o_acc

    in_specs = [pl.BlockSpec((2, blk, LANES), lambda p, i: (p, i, 0)),
                pl.BlockSpec((2, t // blk, LANES, blk), lambda p, i: (p, 0, 0, 0)),
                pl.BlockSpec((2, t, LANES), lambda p, i: (p, 0, 0))]
    out_specs = [pl.BlockSpec((blk, LANES), lambda p, i: (i, p)), pl.BlockSpec((2, 1, 1, blk), lambda p, i: (p, i, 0, 0))]
    out_shape = [_sds((t, 512), F32), _sds((MLA_HEADS, t // blk, 1, blk), F32)]
    nc = len(gather)
    return _call(body, name=name, grid=grid, in_specs=in_specs + [_ANY] * nc, out_specs=out_specs + [_ANY] * nc,
                 out_shape=out_shape + [_sds((NDEV,) + g.shape, g.dtype) for g in gather], scratch=_COMM_SCRATCH * nc,
                 sem=('arbitrary', 'arbitrary') if nc else ('parallel', 'parallel'))(q, kt, v, *gather)


def mla_out_bwd(o, dyn, g, *, name):
    t = o.shape[0]
    blk = _tile(t, ATT_BLK)

    def body(o_ref, dh_ref, g_ref, do_ref, dot_ref, delta_ref, dg_ref):
        ov = o_ref[...]
        oh, r = _rms(ov, 512)
        dh = dh_ref[...]
        do = _rms_bwd(oh, r, dh * g_ref[...], 512)
        do_ref[...] = do.astype(BF)
        dd = do * ov
        for pb in range(MLA_HEADS // 2):
            cols = slice(pb * LANES, (pb + 1) * LANES)
            dot_ref[pb, 0] = do[:, cols].T.astype(BF)
            ddt = dd[:, cols].T
            delta_ref[2 * pb, 0] = jnp.sum(ddt[0:V_DIM, :], axis=0, keepdims=True)
            delta_ref[2 * pb + 1, 0] = jnp.sum(ddt[V_DIM:LANES, :], axis=0, keepdims=True)

        @pl.when(pl.program_id(0) == 0)
        def _():
            dg_ref[...] = jnp.zeros_like(dg_ref)

        dg_ref[...] += _colsum(dh * oh)

    return _call(
        body, name=name, grid=(t // blk,),
        in_specs=[pl.BlockSpec((blk, 512), lambda i: (i, 0)), pl.BlockSpec((blk, 512), lambda i: (i, 1)),
                  pl.BlockSpec((1, 512), lambda i: (0, 0))],
        out_specs=[pl.BlockSpec((blk, 512), lambda i: (i, 0)), pl.BlockSpec((MLA_HEADS // 2, 1, LANES, blk), lambda i: (0, i, 0, 0)),
                   pl.BlockSpec((MLA_HEADS, 1, 1, blk), lambda i: (0, i, 0, 0)), pl.BlockSpec((1, 512), lambda i: (0, 0))],
        out_shape=[_sds((t, 512), BF), _sds((MLA_HEADS // 2, t // blk, LANES, blk), BF),
                   _sds((MLA_HEADS, t // blk, 1, blk), F32), _sds((1, 512), F32)],
        sem=('arbitrary',),
    )(o, dyn, g)


def flash_bwd(q, qt, k, kt, v, do, dot, lse, delta, *, name, scatter=()):
    t = q.shape[1]
    blk = _tile(t, ATT_BLK)
    nb = t // blk
    grid = (MLA_HEADS, nb)

    def body(q_ref, qt_ref, k_ref, kt_ref, v_ref, do_ref, dot_ref, lse_ref, delta_ref, *rest):
        nc = len(scatter)
        srcs, (dqt_ref, dk_ref, dv_ref), dsts, sems = rest[:nc], rest[nc:nc + 3], rest[nc + 3:2 * nc + 3], rest[2 * nc + 3:]
        if nc:
            _overlapped(grid, lambda: [_copies('scatter', srcs[i], dsts[i], *sems[3 * i:3 * i + 3]) for i in range(nc)])
        h, j = pl.program_id(0), pl.program_id(1)
        row = lax.broadcasted_iota(jnp.int32, (blk, blk), 0)
        col = lax.broadcasted_iota(jnp.int32, (blk, blk), 1)
        lane = lax.broadcasted_iota(jnp.int32, (1, LANES), 1)
        mine = (lane // V_DIM) == (h % 2)

        @pl.when(j == 0)
        def _():
            dqt_ref[...] = jnp.zeros_like(dqt_ref)

        kv, ktv, vv = k_ref[...], kt_ref[...], v_ref[...]

        def block(i, carry, masked):
            dk, dv = carry
            rows = pl.ds(pl.multiple_of(i * blk, blk), blk)
            pt = jnp.exp(_dot(kv, qt_ref[i]) - lse_ref[i])
            if masked:
                pt = jnp.where(col >= row, pt, 0.0)
            dv = dv + _dot(pt, do_ref[rows, :])
            dst = (pt * (_dot(vv, dot_ref[i]) - delta_ref[i])).astype(BF)
            dk = dk + _dot(dst, q_ref[rows, :])
            dqt_ref[i] += _dot(ktv, dst)
            return dk, dv

        zero = jnp.zeros((blk, LANES), F32)
        carry = block(j, (zero, zero), True)
        npairs = (nb - 1 - j) // 2
        carry = lax.fori_loop(0, npairs, lambda p, c: block(j + 2 + 2 * p, block(j + 1 + 2 * p, c, False), False), carry)
        dk, dv = lax.fori_loop(j + 1 + 2 * npairs, nb, lambda i, c: block(i, c, False), carry)
        dk_ref[...] = dk
        dv_ref[...] = jnp.where(mine, dv, 0.0)

    whole = pl.BlockSpec((None, t, LANES), lambda h, j: (h, 0, 0))
    wholet = pl.BlockSpec((None, nb, LANES, blk), lambda h, j: (h, 0, 0, 0))
    kvb = pl.BlockSpec((None, blk, LANES), lambda h, j: (h, j, 0))
    rowv = pl.BlockSpec((None, nb, 1, blk), lambda h, j: (h, 0, 0, 0))
    in_specs = [whole, wholet, kvb, pl.BlockSpec((None, None, LANES, blk), lambda h, j: (h, j, 0, 0)), kvb,
                pl.BlockSpec((t, LANES), lambda h, j: (0, h // 2)),
                pl.BlockSpec((None, nb, LANES, blk), lambda h, j: (h // 2, 0, 0, 0)), rowv, rowv]
    out_specs = [wholet, kvb, kvb]
    out_shape = [_sds((MLA_HEADS, nb, LANES, blk), F32), _sds((MLA_HEADS, t, LANES), F32), _sds((MLA_HEADS, t, LANES), F32)]
    args = (q, qt, k, kt, v, do, dot, lse, delta)
    nc = len(scatter)
    return _call(body, name=name, grid=grid, in_specs=in_specs + [_ANY] * nc, out_specs=out_specs + [_ANY] * nc,
                 out_shape=out_shape + [_sds(s.shape, s.dtype) for s in scatter], scratch=_COMM_SCRATCH * nc,
                 sem=('arbitrary', 'arbitrary') if nc else ('parallel', 'arbitrary'), vmem=VMEM_BIG)(*args, *scatter)


def mix_out_fwd(x, y_ssm, o, g_ssm, g_mla, w_out, *, name, tq=512):
    t = x.shape[0]
    tq = _tile(t, tq)

    def body(x_ref, ys_ref, o_ref, gs_ref, gm_ref, w_ref, x1_ref, yn_ref):
        ns = (_rms(ys_ref[...], SSM_W)[0] * gs_ref[...]).astype(BF)
        nm = (_rms(o_ref[...], 512)[0] * gm_ref[...]).astype(BF)
        yn_ref[:, 0:SSM_W] = ns
        yn_ref[:, SSM_W:D] = nm
        x1_ref[...] = x_ref[...] + _dot(ns, w_ref[0:SSM_W, :]) + _dot(nm, w_ref[SSM_W:D, :])

    row = lambda w: pl.BlockSpec((tq, w), lambda i: (i, 0))
    vec = pl.BlockSpec((1, 512), lambda i: (0, 0))
    return _call(body, name=name, grid=(t // tq,),
                 in_specs=[row(D), row(512), row(512), vec, vec, pl.BlockSpec((D, D), lambda i: (0, 0))],
                 out_specs=[row(D), row(D)], out_shape=[_sds((t, D), F32), _sds((t, D), BF)], sem=('parallel',),
                 )(x, y_ssm, o, g_ssm, g_mla, w_out)


MEM_SCALE = 1.0 / math.sqrt(MEM_HD)


def memkv_fwd(mem, g, wk, wv, kg, *, name):
    def body(m_ref, g_ref, wk_ref, wv_ref, kg_ref, mh_ref, k_ref, v_ref):
        mh = (_rms(m_ref[...], D)[0] * g_ref[...]).astype(BF)
        mh_ref[...] = mh
        for h in range(MEM_HEADS):
            cols = slice(h * LANES, (h + 1) * LANES)
            k_ref[h] = (_rms(_dot(mh, wk_ref[:, cols]), MEM_HD)[0] * kg_ref[...]).astype(BF)
            v_ref[h] = _dot(mh, wv_ref[:, cols]).astype(BF)

    return _call(body, name=name,
                 out_shape=[_sds((N_MEM, D), BF), _sds((MEM_HEADS, N_MEM, LANES), BF), _sds((MEM_HEADS, N_MEM, LANES), BF)],
                 )(mem, g, wk, wv, kg)


def memkv_bwd(mem, g, wk, wv, kg, dk, dv, *, name):
    def body(m_ref, g_ref, wk_ref, wv_ref, kg_ref, dk_ref, dv_ref, dwk_ref, dwv_ref, dkg_ref, dg_ref):
        mhat, _ = _rms(m_ref[...], D)
        mh = (mhat * g_ref[...]).astype(BF)
        lane = lax.broadcasted_iota(jnp.int32, (1, LANES), 1)
        dkg = jnp.zeros((1, LANES), F32)
        dmh = jnp.zeros((N_MEM, D), F32)
        for h in range(MEM_HEADS):
            cols = slice(h * LANES, (h + 1) * LANES)
            kh, kr = _rms(_dot(mh, wk_ref[:, cols]), MEM_HD)
            dko = dk_ref[h]
            dkg = dkg + _colsum(dko * kh)
            dkraw = _rms_bwd(kh, kr, dko * kg_ref[...], MEM_HD).astype(BF)
            dvh = jnp.where((lane // MEM_HD) == (h % 2), dv_ref[h], 0.0).astype(BF)
            dwk_ref[:, cols] = _dot(mh, dkraw, TN)
            dwv_ref[:, cols] = _dot(mh, dvh, TN)
            dmh = dmh + _dot(dkraw, wk_ref[:, cols], NT) + _dot(dvh, wv_ref[:, cols], NT)
        dkg_ref[...] = dkg
        dg_ref[...] = _colsum(dmh * mhat)

    return _call(body, name=name,
                 out_shape=[_sds((D, 512), F32), _sds((D, 512), F32), _sds((1, LANES), F32), _sds((1, D), F32)],
                 )(mem, g, wk, wv, kg, dk, dv)


def memattn_fwd(x, g, wq, qg, kh, vh, wo, *, name, tq=512):
    t = x.shape[0]
    tq = _tile(t, tq)

    def body(x_ref, g_ref, wq_ref, qg_ref, k_ref, v_ref, wo_ref, x2_ref, hn_ref):
        xv = x_ref[...]
        hn = (_rms(xv, D)[0] * g_ref[...]).astype(BF)
        hn_ref[...] = hn
        out = xv
        for pb in range(MEM_HEADS // 2):
            o = jnp.zeros((tq, LANES), F32)
            for h in (2 * pb, 2 * pb + 1):
                q = _rms(_dot(hn, wq_ref[:, h * LANES:(h + 1) * LANES]), MEM_HD)[0] * qg_ref[...]
                s = _dot(q, k_ref[h], NT) * MEM_SCALE
                p = jnp.exp(s - jnp.max(s, axis=-1, keepdims=True))
                p = p / jnp.sum(p, axis=-1, keepdims=True)
                o = o + _dot(p, v_ref[h])
            out = out + _dot(o, wo_ref[pb * LANES:(pb + 1) * LANES, :])
        x2_ref[...] = out

    full = lambda shape: pl.BlockSpec(shape, lambda i: (0,) * len(shape))
    row = pl.BlockSpec((tq, D), lambda i: (i, 0))
    return _call(body, name=name, grid=(t // tq,),
                 in_specs=[row, full((1, D)), full((D, 512)), full((1, LANES)), full((MEM_HEADS, N_MEM, LANES)),
                           full((MEM_HEADS, N_MEM, LANES)), full((MEM_HEADS * MEM_HD, D))],
                 out_specs=[row, row], out_shape=[_sds((t, D), F32), _sds((t, D), BF)], sem=('parallel',),
                 )(x, g, wq, qg, kh, vh, wo)


def memattn_bwd(x, dx2, g, wq, qg, kh, vh, wo, *, name, tq=512):
    t = x.shape[0]
    tq = _tile(t, tq)

    def body(x_ref, dx2_ref, g_ref, wq_ref, qg_ref, k_ref, v_ref, wo_ref,
             dx_ref, dxb_ref, o_ref, dqr_ref, dk_ref, dv_ref, dqg_ref, dg_ref):
        @pl.when(pl.program_id(0) == 0)
        def _():
            dk_ref[...] = jnp.zeros_like(dk_ref)
            dv_ref[...] = jnp.zeros_like(dv_ref)
            dqg_ref[...] = jnp.zeros_like(dqg_ref)
            dg_ref[...] = jnp.zeros_like(dg_ref)

        xhat, xr = _rms(x_ref[...], D)
        hn = (xhat * g_ref[...]).astype(BF)
        dx2 = dx2_ref[...]
        dx2b = dx2.astype(BF)
        dh = jnp.zeros((tq, D), F32)
        dqg = jnp.zeros((1, LANES), F32)
        for pb in range(MEM_HEADS // 2):
            do = _dot(dx2b, wo_ref[pb * LANES:(pb + 1) * LANES, :], NT).astype(BF)
            o = jnp.zeros((tq, LANES), F32)
            for h in (2 * pb, 2 * pb + 1):
                cols = slice(h * LANES, (h + 1) * LANES)
                qh, qr = _rms(_dot(hn, wq_ref[:, cols]), MEM_HD)
                qb = (qh * qg_ref[...]).astype(BF)
                s = _dot(qb, k_ref[h], NT) * MEM_SCALE
                p = jnp.exp(s - jnp.max(s, axis=-1, keepdims=True))
                p = p / jnp.sum(p, axis=-1, keepdims=True)
                pb16 = p.astype(BF)
                o = o + _dot(pb16, v_ref[h])
                dv_ref[h] += _dot(pb16, do, TN)
                dp = _dot(do, v_ref[h], NT)
                ds = (p * (dp - jnp.sum(dp * p, axis=-1, keepdims=True)) * MEM_SCALE).astype(BF)
                dk_ref[h] += _dot(ds, qb, TN)
                dqo = _dot(ds, k_ref[h])
                dqg = dqg + _colsum(dqo * qh)
                dqraw = _rms_bwd(qh, qr, dqo * qg_ref[...], MEM_HD).astype(BF)
                dqr_ref[:, cols] = dqraw
                dh = dh + _dot(dqraw, wq_ref[:, cols], NT)
            o_ref[:, pb * LANES:(pb + 1) * LANES] = o.astype(BF)
        dx = dx2 + _rms_bwd(xhat, xr, dh * g_ref[...], D)
        dx_ref[...] = dx
        dxb_ref[...] = dx.astype(BF)
        dqg_ref[...] += dqg
        dg_ref[...] += _colsum(dh * xhat)

    full = lambda shape: pl.BlockSpec(shape, lambda i: (0,) * len(shape))
    row = lambda w: pl.BlockSpec((tq, w), lambda i: (i, 0))
    return _call(body, name=name, grid=(t // tq,),
                 in_specs=[row(D), row(D), full((1, D)), full((D, 512)), full((1, LANES)), full((MEM_HEADS, N_MEM, LANES)),
                           full((MEM_HEADS, N_MEM, LANES)), full((MEM_HEADS * MEM_HD, D))],
                 out_specs=[row(D), row(D), row(256), row(512), full((MEM_HEADS, N_MEM, LANES)), full((MEM_HEADS, N_MEM, LANES)),
                            full((1, LANES)), full((1, D))],
                 out_shape=[_sds((t, D), F32), _sds((t, D), BF), _sds((t, 256), BF), _sds((t, 512), BF),
                            _sds((MEM_HEADS, N_MEM, LANES), F32), _sds((MEM_HEADS, N_MEM, LANES), F32),
                            _sds((1, LANES), F32), _sds((1, D), F32)],
                 sem=('arbitrary',))(x, dx2, g, wq, qg, kh, vh, wo)


def mlp_fwd(x, h, w1, w2, *, name, tq=1024, tf=512):
    t = x.shape[0]
    tq = _tile(t, tq)

    def body(x_ref, h_ref, w1_ref, w2_ref, o_ref):
        @pl.when(pl.program_id(1) == 0)
        def _():
            o_ref[...] = x_ref[...]

        a = jnp.maximum(_dot(h_ref[...], w1_ref[...]), 0.0)
        o_ref[...] += _dot(a * a, w2_ref[...])

    row = pl.BlockSpec((tq, D), lambda i, f: (i, 0))
    return _call(body, name=name, grid=(t // tq, D_FF // tf),
                 in_specs=[row, row, pl.BlockSpec((None, D, tf), lambda i, f: (f, 0, 0)), pl.BlockSpec((tf, D), lambda i, f: (f, 0))],
                 out_specs=row, out_shape=_sds((t, D), F32), sem=('parallel', 'arbitrary'), vmem=VMEM_BIG)(x, h, w1, w2)


def mlp_bwd(h, dx, w1, w2, *, name, tq=1024, tf=512):
    t = h.shape[0]
    tq = _tile(t, tq)

    def body(h_ref, dx_ref, w1_ref, w2_ref, dh_ref, r_ref, da_ref):
        @pl.when(pl.program_id(1) == 0)
        def _():
            dh_ref[...] = jnp.zeros_like(dh_ref)

        a = jnp.maximum(_dot(h_ref[...], w1_ref[...]), 0.0)
        r_ref[...] = (a * a).astype(BF)
        da = (_dot(dx_ref[...], w2_ref[...], NT) * (2.0 * a)).astype(BF)
        da_ref[...] = da
        dh_ref[...] += _dot(da, w1_ref[...], NT)

    row = pl.BlockSpec((tq, D), lambda i, f: (i, 0))
    act = pl.BlockSpec((tq, tf), lambda i, f: (i, f))
    return _call(body, name=name, grid=(t // tq, D_FF // tf),
                 in_specs=[row, row, pl.BlockSpec((None, D, tf), lambda i, f: (f, 0, 0)), pl.BlockSpec((tf, D), lambda i, f: (f, 0))],
                 out_specs=[row, act, act], out_shape=[_sds((t, D), F32), _sds((t, D_FF), BF), _sds((t, D_FF), BF)],
                 sem=('parallel', 'arbitrary'), vmem=VMEM_BIG)(h, dx, w1, w2)


def loss_fwd_bwd(y, target, *, name, tq=512):
    t = y.shape[0]
    tq = _tile(t, tq)

    def body(y_ref, t_ref, dy_ref, dyb_ref, l_ref):
        @pl.when(pl.program_id(0) == 0)
        def _():
            l_ref[...] = jnp.zeros_like(l_ref)

        e = y_ref[...] - t_ref[...]
        dy = e * (1.0 / D)
        dy_ref[...] = dy
        dyb_ref[...] = dy.astype(BF)
        l_ref[...] += _colsum(e * e) * (0.5 / D)

    row = pl.BlockSpec((tq, D), lambda i: (i, 0))
    return _call(body, name=name, grid=(t // tq,), in_specs=[row, row],
                 out_specs=[row, row, pl.BlockSpec((1, D), lambda i: (0, 0))],
                 out_shape=[_sds((t, D), F32), _sds((t, D), BF), _sds((1, D), F32)], sem=('arbitrary',))(y, target)


def prep_early(w):
    w_in = w['w_in']
    z = lambda r, c: jnp.zeros((r, c), w_in.dtype)
    w_in_pad = jnp.concatenate([w_in[:, :896], z(D, 64), w_in[:, 896:928], z(D, 32)], axis=1)
    wq = w['mla_w_uq'].reshape(Q_LORA, MLA_HEADS, QK_DIM).transpose(1, 0, 2)
    wq = jnp.pad(wq, ((0, 0), (0, 0), (0, LANES - QK_DIM)))
    ukv = w['mla_w_ukv'].reshape(KV_LORA, MLA_HEADS, QK_NOPE + V_DIM).transpose(1, 0, 2)
    wk = jnp.pad(ukv[:, :, :QK_NOPE], ((0, 0), (0, 0), (0, LANES - QK_NOPE)))
    vpart = ukv[:, :, QK_NOPE:]
    zv = jnp.zeros_like(vpart)
    odd = (jnp.arange(MLA_HEADS) % 2)[:, None, None] == 1
    wv = jnp.where(odd, jnp.concatenate([zv, vpart], axis=2), jnp.concatenate([vpart, zv], axis=2))
    return dict(w_in=w_in_pad, w_glu=w['ssm_w_glu'], wq=wq, wk=wk, wv=wv)


def prep_late(w):
    mq = jnp.pad(w['mem_w_q'].reshape(D, MEM_HEADS, MEM_HD), ((0, 0), (0, 0), (0, LANES - MEM_HD))).reshape(D, 512)
    mkv = w['mem_w_kv'].reshape(D, MEM_HEADS, 2 * MEM_HD)
    mk = jnp.pad(mkv[:, :, :MEM_HD], ((0, 0), (0, 0), (0, LANES - MEM_HD))).reshape(D, 512)
    mvp = mkv[:, :, MEM_HD:]
    zm = jnp.zeros_like(mvp)
    modd = (jnp.arange(MEM_HEADS) % 2)[None, :, None] == 1
    mv = jnp.where(modd, jnp.concatenate([zm, mvp], axis=2), jnp.concatenate([mvp, zm], axis=2)).reshape(D, 512)
    return dict(w_out=w['w_out'], mq=mq, mk=mk, mv=mv, mo=w['mem_w_o'], w1=w['mlp_w1'], w2=w['mlp_w2'])


def prep_small(t, s):
    row = lambda a: a.reshape(1, -1)
    pad = lambda a: jnp.pad(a, (0, LANES - a.shape[0])).reshape(1, LANES)
    out = s5_prep(t, s['ssm_lambda_re'], s['ssm_lambda_im'], s['ssm_log_step'], s['ssm_b_re'], s['ssm_b_im'],
                  s['ssm_c_re'], s['ssm_c_im'])
    out.update(d=row(s['ssm_d']), norm_mix=row(s['norm_mix']), b_glu=row(s['ssm_b_glu']), q_norm=row(s['mla_q_norm']),
               kv_norm=row(s['mla_kv_norm']), q_gain=pad(s['mla_q_gain']), k_gain=pad(s['mla_k_gain']),
               g_ssm=row(s['out_norm_ssm']), g_mla=row(s['out_norm_mla']), norm_mem_q=row(s['norm_mem_q']),
               norm_mem_kv=row(s['norm_mem_kv']), mem_q_gain=pad(s['mem_q_gain']), mem_k_gain=pad(s['mem_k_gain']),
               norm_mlp=row(s['norm_mlp']))
    return out


def _perm(a):
    t, c = a.shape
    return a.reshape(SEGS, t // SEGS, c).transpose(1, 0, 2).reshape(t, c)


def _unperm(a):
    t, c = a.shape
    return a.reshape(t // SEGS, SEGS, c).transpose(1, 0, 2).reshape(t, c)


def layer_fwd(l, x, mem, tabs, plan, ws):
    n = lambda s: f'l{l}_{s}'
    wb = prep_early(plan.early(l))
    h1 = rmsnorm_fwd(x, ws['norm_mix'], name=n('norm_mix'))
    proj = mm(h1, wb['w_in'], 'nn', name=n('w_in'))
    u_p = _perm(proj[:, :SSM_W])
    ypre_p = s5_fwd(u_p, ws, name=n('s5'))
    ypre = _unperm(ypre_p)
    y_ssm = glu_fwd(ypre, wb['w_glu'], ws['b_glu'], name=n('glu'))
    mw = dict(q_norm=ws['q_norm'], kv_norm=ws['kv_norm'], wq=wb['wq'], wk=wb['wk'], wv=wb['wv'],
              q_gain=ws['q_gain'], k_gain=ws['k_gain'])
    q, qt, k, kt, v = mla_prep_fwd(proj, tabs, mw, name=n('mla_prep'))
    o, lse, *gathered = flash_fwd(q, kt, v, name=n('flash'), gather=plan.gather_src(l))
    plan.gathered(l, gathered)
    wb.update(prep_late(plan.late(l)))
    x1, yn = mix_out_fwd(x, y_ssm, o, ws['g_ssm'], ws['g_mla'], wb['w_out'], name=n('mix_out'))
    mh, kh, vh = memkv_fwd(mem, ws['norm_mem_kv'], wb['mk'], wb['mv'], ws['mem_k_gain'], name=n('memkv'))
    x2, h2 = memattn_fwd(x1, ws['norm_mem_q'], wb['mq'], ws['mem_q_gain'], kh, vh, wb['mo'], name=n('memattn'))
    h3 = rmsnorm_fwd(x2, ws['norm_mlp'], name=n('norm_mlp'))
    x3 = mlp_fwd(x2, h3, wb['w1'], wb['w2'], name=n('mlp'))
    saved = dict(x=x, h1=h1, proj=proj, u_p=u_p, ypre=ypre, y_ssm=y_ssm, q=q, qt=qt, k=k, kt=kt, v=v, o=o, lse=lse, x1=x1, yn=yn,
                 kh=kh, vh=vh, x2=x2, h2=h2, h3=h3, mw=mw)
    return x3, wb, saved


def layer_bwd(l, dx3, dx3b, mem, tabs, plan, wb, ws, sv):
    n = lambda s: f'l{l}_{s}_bwd'
    gb, gs = {}, {}
    structs = lambda names: {k: _sds(plan.shapes[k], F32) for k in names}
    dh3, r, da = mlp_bwd(sv['h3'], dx3b, wb['w1'], wb['w2'], name=n('mlp'))
    gb['w1'] = mm(sv['h3'], da, 'tn', name=n('w1'), slots=NDEV)
    gb['w2'] = mm(r, dx3b, 'tn', name=n('w2'))
    dx2, dx2b, gs['norm_mlp'] = rmsnorm_bwd(sv['x2'], ws['norm_mlp'], dh3, dx3, name=n('norm_mlp'))
    dx1, dx1b, o_mem, dqr_mem, dkh, dvh, gs['mem_q_gain'], gs['norm_mem_q'] = memattn_bwd(
        sv['x1'], dx2, ws['norm_mem_q'], wb['mq'], ws['mem_q_gain'], sv['kh'], sv['vh'], wb['mo'], name=n('memattn'))
    gb['mo'] = mm(o_mem, dx2b, 'tn', name=n('mo'))
    gb['mq'] = mm(sv['h2'], dqr_mem, 'tn', name=n('mq'))
    gb['mk'], gb['mv'], gs['mem_k_gain'], gs['norm_mem_kv'] = memkv_bwd(
        mem, ws['norm_mem_kv'], wb['mk'], wb['mv'], ws['mem_k_gain'], dkh, dvh, name=n('memkv'))
    dyn = mm(dx1b, wb['w_out'], 'nt', name=n('w_out_dx'))
    gb['w_out'] = mm(sv['yn'], dx1b, 'tn', name=n('w_out'))
    dy_ssm, _, gs['g_ssm'] = rmsnorm_bwd(sv['y_ssm'], ws['g_ssm'], dyn, None, name=n('out_norm_ssm'), col=0)
    do, dot, delta, gs['g_mla'] = mla_out_bwd(sv['o'], dyn, ws['g_mla'], name=n('out_norm_mla'))
    late = {k: gb.pop(k) for k in ('w_out', 'mq', 'mk', 'mv', 'mo', 'w1', 'w2')}
    plan.late_grads(l, jax.linear_transpose(prep_late, structs(BIG_LATE))(late)[0])
    dq, dk, dv, *received = flash_bwd(sv['q'], sv['qt'], sv['k'], sv['kt'], sv['v'], do, dot, sv['lse'], delta,
                                      name=n('flash'), scatter=plan.scatter_src(l))
    plan.scattered(l, received)
    (dproj_m, cqn, ckvn, dqr, dkr, dvb, gs['q_norm'], gs['kv_norm'], gs['q_gain'], gs['k_gain']) = mla_prep_bwd(
        sv['proj'], tabs, sv['mw'], dq, dk, dv, name=n('mla_prep'))
    by_head = lambda g: g.reshape(g.shape[0], MLA_HEADS, LANES).transpose(1, 0, 2)
    gb['wq'] = by_head(mm(cqn, dqr, 'tn', name=n('wq')))
    gb['wk'] = by_head(mm(ckvn, dkr, 'tn', name=n('wk')))
    gb['wv'] = by_head(mm(ckvn, dvb, 'tn', name=n('wv')))
    dypre, yg, dz, gs['b_glu'] = glu_bwd(sv['ypre'], dy_ssm, wb['w_glu'], ws['b_glu'], name=n('glu'))
    gb['w_glu'] = mm(yg, dz, 'tn', name=n('w_glu'))
    du_p, gs['ar'], gs['ai'], gs['bre'], gs['bim'], gs['cre'], gs['cim'], gs['d'] = s5_bwd(sv['u_p'], _perm(dypre), ws, name=n('s5'))
    dprojb = jnp.concatenate([_unperm(du_p), dproj_m], axis=1)
    dh1 = mm(dprojb, wb['w_in'], 'nt', name=n('w_in_dx'))
    gb['w_in'] = mm(sv['h1'], dprojb, 'tn', name=n('w_in'))
    dx0, dx0b, gs['norm_mix'] = rmsnorm_bwd(sv['x'], ws['norm_mix'], dh1, dx1, name=n('norm_mix'))
    plan.early_grads(l, jax.linear_transpose(prep_early, structs(BIG_EARLY))(gb)[0])
    return dx0, dx0b, gs


def local_step(x, mem, positions, target, small, plan):
    t = x.shape[0]
    tabs = rope_tables(positions)
    layers = []
    for l in range(DEPTH):
        ws, small_vjp = jax.vjp(functools.partial(prep_small, t), {k: small[k][l] for k in SMALL})
        x, wb, sv = layer_fwd(l, x, mem, tabs, plan, ws)
        layers.append((wb, ws, small_vjp, sv))
    dx, dxb, lcols = loss_fwd_bwd(x, target, name='loss')
    loss = jnp.sum(lcols)
    gsmall = [None] * DEPTH
    for l in reversed(range(DEPTH)):
        wb, ws, small_vjp, sv = layers[l]
        dx, dxb, gs = layer_bwd(l, dx, dxb, mem, tabs, plan, wb, ws, sv)
        gs['pr'], gs['pi'] = jnp.zeros_like(ws['pr']), jnp.zeros_like(ws['pi'])
        gsmall[l] = small_vjp(gs)[0]
    return loss, dx, gsmall


class ExchangePlan:
    def __init__(self, shard_shapes, mine, first_early):
        self.shapes = {k: (s[1] * (NDEV if BIG_AXIS[k] == 1 else 1), s[2] * (NDEV if BIG_AXIS[k] == 2 else 1))
                       for k, s in shard_shapes.items()}
        self.shard = {k: s[1:] for k, s in shard_shapes.items()}
        self.shapes['mlp_w1'] = (NDEV,) + self.shard['mlp_w1']
        self.mine = mine
        self.w_early = {0: first_early}
        self.w_late = {}
        self.g_late, self.g_early = {}, {}
        self.r_late, self.r_early = {}, {}

    def _unpack(self, g, names):
        out, r0 = {}, 0
        for k in names:
            nr = math.prod(self.shard[k]) // D
            s = g[:, r0:r0 + nr]
            out[k] = (s.reshape(self.shapes[k]) if k == 'mlp_w1'
                      else _from_slots(s.reshape(NDEV, -1), (1,) + self.shard[k], BIG_AXIS[k])[0])
            r0 += nr
        return out

    def _pack(self, g, names, rows):
        slots = jnp.concatenate([g[k].reshape(NDEV, -1) if k == 'mlp_w1' else _to_slots(g[k][None], BIG_AXIS[k])
                                 for k in names], axis=1)
        return jnp.pad(slots, ((0, 0), (0, rows * D - slots.shape[1]))).astype(BF).reshape(NDEV, rows, D)

    def early(self, l):
        return self._unpack(self.w_early.pop(l), BIG_EARLY)

    def late(self, l):
        return self._unpack(self.w_late.pop(l), BIG_LATE)

    def gather_src(self, l):
        src = [self.mine[l, :LATE_ROWS]]
        if l + 1 < DEPTH:
            src.append(self.mine[l + 1, LATE_ROWS:])
        return tuple(src)

    def gathered(self, l, res):
        self.w_late[l] = res[0]
        if l + 1 < DEPTH:
            self.w_early[l + 1] = res[1]

    def late_grads(self, l, g):
        self.g_late[l] = self._pack(g, BIG_LATE, LATE_ROWS)

    def early_grads(self, l, g):
        self.g_early[l] = self._pack(g, BIG_EARLY, LAYER_ROWS - LATE_ROWS)

    def scatter_src(self, l):
        src = [self.g_late.pop(l)]
        if l + 1 < DEPTH:
            src.append(self.g_early.pop(l + 1))
        return tuple(src)

    def scattered(self, l, res):
        self.r_late[l] = res[0]
        if l + 1 < DEPTH:
            self.r_early[l + 1] = res[1]


def _peer(k):
    x, y, c = lax.axis_index('x'), lax.axis_index('y'), lax.axis_index('c')
    px, py, pc = x ^ ((k >> 2) & 1), y ^ ((k >> 1) & 1), c ^ (k & 1)
    return (px, py, pc), 4 * px + 2 * py + pc


def _copies(kind, src_ref, dst_ref, send_sems, recv_sems, loc_sem):
    _, me = _peer(0)
    src = (lambda p: src_ref.at[p]) if kind == 'scatter' else (lambda p: src_ref)
    local = pltpu.make_async_copy(src(me), dst_ref.at[me], loc_sem)
    sends, recvs = [], []
    for k in range(1, NDEV):
        dev, p = _peer(k)
        for slot, lst in ((me, sends), (p, recvs)):
            lst.append(pltpu.make_async_remote_copy(src_ref=src(p), dst_ref=dst_ref.at[slot], send_sem=send_sems.at[k - 1],
                                                    recv_sem=recv_sems.at[k - 1], device_id=dev,
                                                    device_id_type=pl.DeviceIdType.MESH))
    return local, sends, recvs


def _start_copies(cs):
    local, sends, _ = cs
    local.start()
    for cp in sends:
        cp.start()


def _wait_copies(cs):
    local, sends, recvs = cs
    for cp in sends:
        cp.wait_send()
    for cp in recvs:
        cp.wait_recv()
    local.wait()


_COMM_SCRATCH = (pltpu.SemaphoreType.DMA((NDEV - 1,)), pltpu.SemaphoreType.DMA((NDEV - 1,)), pltpu.SemaphoreType.DMA(()))
_ANY = pl.BlockSpec(memory_space=pl.ANY)


def exchange(scatters, gathers, *, name):
    ins = list(scatters) + list(gathers)
    kinds = ['scatter'] * len(scatters) + ['gather'] * len(gathers)
    n_in = len(ins)
    outs = [_sds(a.shape, a.dtype) for a in scatters] + [_sds((NDEV,) + b.shape, b.dtype) for b in gathers]

    def body(*refs):
        in_refs, out_refs, sems = refs[:n_in], refs[n_in:2 * n_in], refs[2 * n_in:]
        sets = [_copies(kind, in_refs[i], out_refs[i], *sems[3 * i:3 * i + 3]) for i, kind in enumerate(kinds)]
        for cs in sets:
            _start_copies(cs)
        for cs in sets:
            _wait_copies(cs)

    return pl.pallas_call(body, name=name, in_specs=[_ANY] * n_in, out_specs=[_ANY] * n_in, out_shape=outs,
                          scratch_shapes=list(_COMM_SCRATCH * n_in))(*ins)


def adamw(w, m, v, g8, *, name, tr):
    r = w.shape[0]
    c1 = 1.0 / (1.0 - ADAM_B1 ** ADAM_STEP)
    c2 = 1.0 / (1.0 - ADAM_B2 ** ADAM_STEP)

    def body(w_ref, m_ref, v_ref, g_ref, go_ref, d_ref, mo_ref, vo_ref):
        g = g_ref[0].astype(F32)
        for i in range(1, NDEV):
            g = g + g_ref[i].astype(F32)
        m_new = ADAM_B1 * m_ref[...] + (1.0 - ADAM_B1) * g
        v_new = ADAM_B2 * v_ref[...] + (1.0 - ADAM_B2) * (g * g)
        go_ref[...] = g
        mo_ref[...] = m_new
        vo_ref[...] = v_new
        d_ref[...] = -ADAM_LR * ((m_new * c1) / (jnp.sqrt(v_new * c2) + ADAM_EPS) + ADAM_WD * w_ref[...])

    row = pl.BlockSpec((tr, D), lambda i: (i, 0))
    return _call(body, name=name, grid=(r // tr,),
                 in_specs=[row, row, row, pl.BlockSpec((NDEV, tr, D), lambda i: (0, i, 0))],
                 out_specs=[row] * 4, out_shape=[_sds((r, D), F32)] * 4, sem=('parallel',), vmem=VMEM_BIG)(w, m, v, g8)


def _flat_rows(parts, rows):
    flat = jnp.concatenate([p.reshape(-1) for p in parts])
    return jnp.pad(flat, (0, rows * D - flat.shape[0])).reshape(rows, D)


def _unflat(flat2d, shapes):
    flat = flat2d.reshape(-1)
    out, off = [], 0
    for s in shapes:
        n = math.prod(s)
        out.append(flat[off:off + n].reshape(s))
        off += n
    return out


def _to_slots(g, axis):
    l, r, c = g.shape
    if axis == 1:
        return g.reshape(l, NDEV, r // NDEV, c).transpose(1, 0, 2, 3).reshape(NDEV, -1)
    return g.reshape(l, r, NDEV, c // NDEV).transpose(2, 0, 1, 3).reshape(NDEV, -1)


def _from_slots(s, shard_shape, axis):
    l, r, c = shard_shape
    s = s.reshape(NDEV, l, r, c)
    if axis == 1:
        return s.transpose(1, 0, 2, 3).reshape(l, NDEV * r, c)
    return s.transpose(1, 2, 0, 3).reshape(l, r, NDEV * c)


LATE_ROWS = 1280
LAYER_ROWS = 1536
BIG_ROWS = DEPTH * LAYER_ROWS
SMALL_ROWS = 640


def kernel(x, mem, positions, norm_mix, w_in, ssm_lambda_re, ssm_lambda_im, ssm_log_step, ssm_b_re, ssm_b_im, ssm_c_re, ssm_c_im, ssm_d, ssm_w_glu, ssm_b_glu, mla_q_norm, mla_w_uq, mla_kv_norm, mla_w_ukv, mla_q_gain, mla_k_gain, out_norm_ssm, out_norm_mla, w_out, norm_mem_q, norm_mem_kv, mem_w_q, mem_w_kv, mem_q_gain, mem_k_gain, mem_w_o, norm_mlp, mlp_w1, mlp_w2, loss_target, m_norm_mix, m_w_in, m_ssm_lambda_re, m_ssm_lambda_im, m_ssm_log_step, m_ssm_b_re, m_ssm_b_im, m_ssm_c_re, m_ssm_c_im, m_ssm_d, m_ssm_w_glu, m_ssm_b_glu, m_mla_q_norm, m_mla_w_uq, m_mla_kv_norm, m_mla_w_ukv, m_mla_q_gain, m_mla_k_gain, m_out_norm_ssm, m_out_norm_mla, m_w_out, m_norm_mem_q, m_norm_mem_kv, m_mem_w_q, m_mem_w_kv, m_mem_q_gain, m_mem_k_gain, m_mem_w_o, m_norm_mlp, m_mlp_w1, m_mlp_w2, v_norm_mix, v_w_in, v_ssm_lambda_re, v_ssm_lambda_im, v_ssm_log_step, v_ssm_b_re, v_ssm_b_im, v_ssm_c_re, v_ssm_c_im, v_ssm_d, v_ssm_w_glu, v_ssm_b_glu, v_mla_q_norm, v_mla_w_uq, v_mla_kv_norm, v_mla_w_ukv, v_mla_q_gain, v_mla_k_gain, v_out_norm_ssm, v_out_norm_mla, v_w_out, v_norm_mem_q, v_norm_mem_kv, v_mem_w_q, v_mem_w_kv, v_mem_q_gain, v_mem_k_gain, v_mem_w_o, v_norm_mlp, v_mlp_w1, v_mlp_w2):
    wvals = (norm_mix, w_in, ssm_lambda_re, ssm_lambda_im, ssm_log_step, ssm_b_re, ssm_b_im, ssm_c_re, ssm_c_im, ssm_d, ssm_w_glu, ssm_b_glu, mla_q_norm, mla_w_uq, mla_kv_norm, mla_w_ukv, mla_q_gain, mla_k_gain, out_norm_ssm, out_norm_mla, w_out, norm_mem_q, norm_mem_kv, mem_w_q, mem_w_kv, mem_q_gain, mem_k_gain, mem_w_o, norm_mlp, mlp_w1, mlp_w2)
    mvals = (m_norm_mix, m_w_in, m_ssm_lambda_re, m_ssm_lambda_im, m_ssm_log_step, m_ssm_b_re, m_ssm_b_im, m_ssm_c_re, m_ssm_c_im, m_ssm_d, m_ssm_w_glu, m_ssm_b_glu, m_mla_q_norm, m_mla_w_uq, m_mla_kv_norm, m_mla_w_ukv, m_mla_q_gain, m_mla_k_gain, m_out_norm_ssm, m_out_norm_mla, m_w_out, m_norm_mem_q, m_norm_mem_kv, m_mem_w_q, m_mem_w_kv, m_mem_q_gain, m_mem_k_gain, m_mem_w_o, m_norm_mlp, m_mlp_w1, m_mlp_w2)
    vvals = (v_norm_mix, v_w_in, v_ssm_lambda_re, v_ssm_lambda_im, v_ssm_log_step, v_ssm_b_re, v_ssm_b_im, v_ssm_c_re, v_ssm_c_im, v_ssm_d, v_ssm_w_glu, v_ssm_b_glu, v_mla_q_norm, v_mla_w_uq, v_mla_kv_norm, v_mla_w_ukv, v_mla_q_gain, v_mla_k_gain, v_out_norm_ssm, v_out_norm_mla, v_w_out, v_norm_mem_q, v_norm_mem_kv, v_mem_w_q, v_mem_w_kv, v_mem_q_gain, v_mem_k_gain, v_mem_w_o, v_norm_mlp, v_mlp_w1, v_mlp_w2)
    w = dict(zip(WEIGHTS, wvals))
    m = dict(zip(WEIGHTS, mvals))
    v = dict(zip(WEIGHTS, vvals))

    shard_shapes = {k: w[k].shape for k in BIG}
    layer_shapes = [shard_shapes[k][1:] for k in BIG]

    def layer_flat(parts):
        flat = jnp.concatenate([p.reshape(DEPTH, -1) for p in parts], axis=1)
        return jnp.pad(flat, ((0, 0), (0, LAYER_ROWS * D - flat.shape[1]))).reshape(DEPTH, LAYER_ROWS, D)

    mine = layer_flat([w[k].astype(BF) for k in BIG])
    first, = exchange([], [mine[0, LATE_ROWS:]], name='gather_early0')
    plan = ExchangePlan(shard_shapes, mine, first)
    small = {k: w[k] for k in SMALL}
    loss, grad_x, gsmall = local_step(x[0], mem[0], positions[0], loss_target[0], small, plan)
    gs_full = [jnp.stack([gsmall[l][k] for l in range(DEPTH)]) for k in SMALL]
    small_flat = _flat_rows(gs_full, SMALL_ROWS).astype(BF)
    plan.r_early[0], g8_small, losses = exchange([plan.g_early.pop(0)], [small_flat, jnp.full((8, LANES), loss, F32)],
                                                 name='exchange_last')
    loss_all = jnp.sum(losses[:, 0, 0])
    g8_big = jnp.concatenate([r[l] for l in range(DEPTH) for r in (plan.r_late, plan.r_early)], axis=1)

    small_shapes = [w[k].shape for k in SMALL]
    flat_big = lambda d: layer_flat([d[k] for k in BIG]).reshape(BIG_ROWS, D)
    gb, db, mb, vb = adamw(flat_big(w), flat_big(m), flat_big(v), g8_big, name='adamw_big', tr=256)
    gs, ds, ms, vs = adamw(_flat_rows([w[k] for k in SMALL], SMALL_ROWS), _flat_rows([m[k] for k in SMALL], SMALL_ROWS),
                           _flat_rows([v[k] for k in SMALL], SMALL_ROWS), g8_small, name='adamw_small', tr=128)

    def unflat_big(fb):
        fb, out, r0 = fb.reshape(DEPTH, LAYER_ROWS, D), [], 0
        for k, shp in zip(BIG, layer_shapes):
            nr = math.prod(shp) // D
            out.append(fb[:, r0:r0 + nr].reshape(shard_shapes[k]))
            r0 += nr
        return out

    res = {}
    for tag, fb, fs in (('g', gb, gs), ('d', db, ds), ('m', mb, ms), ('v', vb, vs)):
        res[tag] = dict(zip(BIG, unflat_big(fb)))
        res[tag].update(zip(SMALL, _unflat(fs, small_shapes)))
    return (loss_all, grad_x[None], *[res['g'][k] for k in WEIGHTS], *[res['d'][k] for k in WEIGHTS],
            *[res['m'][k] for k in WEIGHTS], *[res['v'][k] for k in WEIGHTS])
```

```python
import functools
import math

import jax
import jax.numpy as jnp
from jax import lax
from jax.experimental import pallas as pl
from jax.experimental.pallas import tpu as pltpu

F32 = jnp.float32
BF = jnp.bfloat16

D = 1024
DEPTH = 4
N_MEM = 256
MEM_HEADS = 4
MEM_HD = 64
SSM_W = 512
SSM_G = 32
SSM_H = 16
SSM_P = 64
MLA_HEADS = 8
QK_NOPE = 64
QK_ROPE = 32
QK_DIM = 96
V_DIM = 64
Q_LORA = 256
KV_LORA = 128
ROPE_THETA = 10000.0
D_FF = 4096
IN_COLS = 928
EPS = 1e-6
NDEV = 8
LANES = 128
SEGS = 32
S5_LW = 256
S5_NHB = (SSM_G * SSM_P) // S5_LW
ADAM_LR = 0.001
ADAM_B1 = 0.9
ADAM_B2 = 0.999
ADAM_EPS = 1e-08
ADAM_WD = 0.01
ADAM_STEP = 10
VMEM_BIG = 56 * 1024 * 1024

NN = (((1,), (0,)), ((), ()))
NT = (((1,), (1,)), ((), ()))
TN = (((0,), (0,)), ((), ()))

BIG_LATE = ('w_out', 'mem_w_q', 'mem_w_kv', 'mem_w_o', 'mlp_w1', 'mlp_w2')
BIG_EARLY = ('w_in', 'ssm_w_glu', 'mla_w_uq', 'mla_w_ukv')
BIG = BIG_LATE + BIG_EARLY
BIG_AXIS = {'w_in': 1, 'ssm_w_glu': 1, 'mla_w_uq': 2, 'mla_w_ukv': 2, 'w_out': 1, 'mem_w_q': 1, 'mem_w_kv': 1,
            'mem_w_o': 2, 'mlp_w1': 2, 'mlp_w2': 1}
SMALL = ('norm_mix', 'ssm_lambda_re', 'ssm_lambda_im', 'ssm_log_step', 'ssm_b_re', 'ssm_b_im', 'ssm_c_re', 'ssm_c_im',
         'ssm_d', 'ssm_b_glu', 'mla_q_norm', 'mla_kv_norm', 'mla_q_gain', 'mla_k_gain', 'out_norm_ssm', 'out_norm_mla',
         'norm_mem_q', 'norm_mem_kv', 'mem_q_gain', 'mem_k_gain', 'norm_mlp')
WEIGHTS = ('norm_mix', 'w_in', 'ssm_lambda_re', 'ssm_lambda_im', 'ssm_log_step', 'ssm_b_re', 'ssm_b_im', 'ssm_c_re',
           'ssm_c_im', 'ssm_d', 'ssm_w_glu', 'ssm_b_glu', 'mla_q_norm', 'mla_w_uq', 'mla_kv_norm', 'mla_w_ukv',
           'mla_q_gain', 'mla_k_gain', 'out_norm_ssm', 'out_norm_mla', 'w_out', 'norm_mem_q', 'norm_mem_kv', 'mem_w_q',
           'mem_w_kv', 'mem_q_gain', 'mem_k_gain', 'mem_w_o', 'norm_mlp', 'mlp_w1', 'mlp_w2')


def _call(body, *, name, out_shape, grid=(), in_specs=None, out_specs=None, scratch=(), sem=None, vmem=None):
    params = {}
    if sem is not None:
        params['dimension_semantics'] = sem
    if vmem is not None:
        params['vmem_limit_bytes'] = vmem
    specs = {} if in_specs is None else dict(grid=grid, in_specs=in_specs, out_specs=out_specs)
    return pl.pallas_call(body, name=name, out_shape=out_shape, scratch_shapes=list(scratch),
                          compiler_params=pltpu.CompilerParams(**params), **specs)


def _sds(shape, dtype):
    return jax.ShapeDtypeStruct(shape, dtype)


def _dot(a, b, dims=NN):
    return lax.dot_general(a.astype(BF), b.astype(BF), dims, preferred_element_type=F32)


def _split(a):
    hi = a.astype(BF)
    return hi, (a - hi.astype(F32)).astype(BF)


def _dot3(a, b, dims=NN):
    ah, al = _split(a)
    bh, bl = _split(b)
    d = lambda p, q: lax.dot_general(p, q, dims, preferred_element_type=F32)
    return d(ah, bh) + (d(ah, bl) + d(al, bh))


_sdot = _dot


def _rms(x, n):
    r = lax.rsqrt(jnp.sum(x * x, axis=-1, keepdims=True) * (1.0 / n) + EPS)
    return x * r, r


def _rms_bwd(xhat, r, dxhat, n):
    return r * (dxhat - xhat * (jnp.sum(dxhat * xhat, axis=-1, keepdims=True) * (1.0 / n)))


def _colsum(a):
    return jnp.sum(a, axis=0, keepdims=True)


def _tile(t, want):
    return min(t, want)


def _bidx(nb):
    return (lambda b: b) if nb > 1 else (lambda b: 0)


def mm(a, b, mode, *, name, out_dtype=F32, tm=1024, tn=1024, slots=0):
    squeeze = a.ndim == 2 and b.ndim == 2
    a = a[None] if a.ndim == 2 else a
    b = b[None] if b.ndim == 2 else b
    nb = max(a.shape[0], b.shape[0])
    ab, bb = _bidx(a.shape[0]), _bidx(b.shape[0])
    if mode in ('nn', 'nt'):
        m, k = a.shape[1:]
        n = b.shape[2] if mode == 'nn' else b.shape[1]
        tm, tn = _tile(m, tm), _tile(n, tn)
        dims = NN if mode == 'nn' else NT

        def body(a_ref, b_ref, o_ref):
            o_ref[...] = _dot(a_ref[...], b_ref[...], dims).astype(o_ref.dtype)

        bspec = (pl.BlockSpec((None, k, tn), lambda bi, i, j: (bb(bi), 0, j)) if mode == 'nn'
                 else pl.BlockSpec((None, tn, k), lambda bi, i, j: (bb(bi), j, 0)))
        out = _call(body, name=name, grid=(nb, m // tm, n // tn),
                    in_specs=[pl.BlockSpec((None, tm, k), lambda bi, i, j: (ab(bi), i, 0)), bspec],
                    out_specs=pl.BlockSpec((None, tm, tn), lambda bi, i, j: (bi, i, j)),
                    out_shape=_sds((nb, m, n), out_dtype), sem=('parallel', 'parallel', 'parallel'), vmem=VMEM_BIG)(a, b)
    else:
        k, m = a.shape[1:]
        n = b.shape[2]
        tm, tn, tk = _tile(m, 1024), _tile(n, 1024), _tile(k, 512)
        per = 1
        if slots:
            ts = n // slots
            per = tn // ts
            out_spec, out_shape = pl.BlockSpec((per, tm, ts), lambda bi, i, j, kk: (j, i, 0)), _sds((slots, m, ts), F32)
        else:
            out_spec, out_shape = pl.BlockSpec((None, tm, tn), lambda bi, i, j, kk: (bi, i, j)), _sds((nb, m, n), F32)

        def body(a_ref, b_ref, o_ref):
            @pl.when(pl.program_id(3) == 0)
            def _():
                o_ref[...] = jnp.zeros_like(o_ref)

            res = _dot(a_ref[...], b_ref[...], TN)
            if slots:
                for s in range(per):
                    o_ref[s] += res[:, s * ts:(s + 1) * ts]
            else:
                o_ref[...] += res

        out = _call(body, name=name, grid=(nb, m // tm, n // tn, k // tk),
                    in_specs=[pl.BlockSpec((None, tk, tm), lambda bi, i, j, kk: (ab(bi), kk, i)),
                              pl.BlockSpec((None, tk, tn), lambda bi, i, j, kk: (bb(bi), kk, j))],
                    out_specs=out_spec, out_shape=out_shape,
                    sem=('parallel', 'parallel', 'parallel', 'arbitrary'), vmem=VMEM_BIG)(a, b)
    return out[0] if squeeze and not slots else out


def rmsnorm_fwd(x, g, *, name, tq=512):
    t, d = x.shape
    tq = _tile(t, tq)

    def body(x_ref, g_ref, o_ref):
        xh, _ = _rms(x_ref[...], d)
        o_ref[...] = (xh * g_ref[...]).astype(o_ref.dtype)

    return _call(body, name=name, grid=(t // tq,),
                 in_specs=[pl.BlockSpec((tq, d), lambda i: (i, 0)), pl.BlockSpec((1, d), lambda i: (0, 0))],
                 out_specs=pl.BlockSpec((tq, d), lambda i: (i, 0)), out_shape=_sds((t, d), BF), sem=('parallel',))(x, g)


def rmsnorm_bwd(x, g, dh, dres, *, name, col=0, tq=512):
    t, d = x.shape
    tq = _tile(t, tq)
    has_res = dres is not None

    def body(*refs):
        if has_res:
            x_ref, g_ref, dh_ref, dres_ref, dx_ref, dxb_ref, dg_ref = refs
        else:
            x_ref, g_ref, dh_ref, dx_ref, dxb_ref, dg_ref = refs
        xh, r = _rms(x_ref[...], d)
        dh_ = dh_ref[...].astype(F32)
        dx = _rms_bwd(xh, r, dh_ * g_ref[...], d)
        if has_res:
            dx = dx + dres_ref[...]
        dx_ref[...] = dx
        dxb_ref[...] = dx.astype(BF)

        @pl.when(pl.program_id(0) == 0)
        def _():
            dg_ref[...] = jnp.zeros_like(dg_ref)

        dg_ref[...] += _colsum(dh_ * xh)

    in_specs = [pl.BlockSpec((tq, d), lambda i: (i, 0)), pl.BlockSpec((1, d), lambda i: (0, 0)),
                pl.BlockSpec((tq, d), lambda i: (i, col))]
    args = [x, g, dh]
    if has_res:
        in_specs.append(pl.BlockSpec((tq, d), lambda i: (i, 0)))
        args.append(dres)
    row = pl.BlockSpec((tq, d), lambda i: (i, 0))
    return _call(body, name=name, grid=(t // tq,), in_specs=in_specs,
                 out_specs=[row, row, pl.BlockSpec((1, d), lambda i: (0, 0))],
                 out_shape=[_sds((t, d), F32), _sds((t, d), BF), _sds((1, d), F32)], sem=('arbitrary',))(*args)


def _cmul(ar, ai, xr, xi):
    return ar * xr - ai * xi, ar * xi + ai * xr


def _seg_carries(er, ei, pr, pi, reverse):
    lw = er.shape[1]
    zero = jnp.zeros((1, lw), F32)
    order = range(SEGS - 1, -1, -1) if reverse else range(SEGS)
    cin_r, cin_i = [None] * SEGS, [None] * SEGS
    tr, ti = zero, zero
    for j in order:
        cin_r[j], cin_i[j] = tr, ti
        mr, mi = _cmul(pr, pi, tr, ti)
        tr, ti = er[j:j + 1, :] + mr, ei[j:j + 1, :] + mi
    return jnp.concatenate(cin_r, axis=0), jnp.concatenate(cin_i, axis=0)


def _s5_chunk(t):
    return _tile(t, 512)


def s5_fwd(u_p, prm, *, name):
    t = u_p.shape[0]
    ch = _s5_chunk(t)
    nch, steps = t // ch, ch // SEGS
    lw = S5_LW

    def body(u_ref, ar_ref, ai_ref, pr_ref, pi_ref, bre_ref, bim_ref, cre_ref, cim_ref, d_ref, y_ref, bur, bui):
        hb = pl.program_id(0)
        ar = jnp.broadcast_to(ar_ref[0], (SEGS, lw))
        ai = jnp.broadcast_to(ai_ref[0], (SEGS, lw))

        def rows_of(c):
            return pl.ds(pl.multiple_of(c * ch, ch), ch)

        @pl.loop(0, nch)
        def _(c):
            u = u_ref[rows_of(c), :]
            bur[rows_of(c), :] = _sdot(u, bre_ref[0])
            bui[rows_of(c), :] = _sdot(u, bim_ref[0])

        def scan(carry, store):
            def step(i, s):
                r0 = pl.multiple_of(i * SEGS, SEGS)
                mr, mi = _cmul(ar, ai, s[0], s[1])
                nr, ni = mr + bur[pl.ds(r0, SEGS), :], mi + bui[pl.ds(r0, SEGS), :]
                if store:
                    bur[pl.ds(r0, SEGS), :] = nr
                    bui[pl.ds(r0, SEGS), :] = ni
                return nr, ni

            return lax.fori_loop(0, t // SEGS, step, carry, unroll=8)

        zero = jnp.zeros((SEGS, lw), F32)
        er, ei = scan((zero, zero), False)
        scan(_seg_carries(er, ei, pr_ref[0], pi_ref[0], False), True)

        @pl.loop(0, nch)
        def _(c):
            rows = rows_of(c)
            y = _sdot(bur[rows, :], cre_ref[0]) - _sdot(bui[rows, :], cim_ref[0])

            @pl.when(hb % 2 == 0)
            def _():
                y_ref[rows, :] = y + d_ref[...] * u_ref[rows, :]

            @pl.when(hb % 2 == 1)
            def _():
                y_ref[rows, :] += y

    vec = pl.BlockSpec((1, 1, lw), lambda h: (h, 0, 0))
    return _call(
        body, name=name, grid=(S5_NHB,),
        in_specs=[pl.BlockSpec((t, LANES), lambda h: (0, h // 2)), vec, vec, vec, vec,
                  pl.BlockSpec((1, LANES, lw), lambda h: (h, 0, 0)), pl.BlockSpec((1, LANES, lw), lambda h: (h, 0, 0)),
                  pl.BlockSpec((1, lw, LANES), lambda h: (h, 0, 0)), pl.BlockSpec((1, lw, LANES), lambda h: (h, 0, 0)),
                  pl.BlockSpec((1, LANES), lambda h: (0, h // 2))],
        out_specs=pl.BlockSpec((t, LANES), lambda h: (0, h // 2)), out_shape=_sds((t, SSM_W), F32),
        scratch=[pltpu.VMEM((t, lw), F32)] * 2, sem=('arbitrary',), vmem=VMEM_BIG,
    )(u_p, prm['ar'], prm['ai'], prm['pr'], prm['pi'], prm['bre'], prm['bim'], prm['cre'], prm['cim'], prm['d'])


def s5_bwd(u_p, dy_p, prm, *, name):
    t = u_p.shape[0]
    ch = _s5_chunk(t)
    nch, steps = t // ch, ch // SEGS
    lw = S5_LW

    def body(u_ref, dy_ref, ar_ref, ai_ref, pr_ref, pi_ref, bre_ref, bim_ref, cre_ref, cim_ref, d_ref,
             du_ref, dar_ref, dai_ref, dbre_ref, dbim_ref, dcre_ref, dcim_ref, dd_ref, bur, bui, sr, si, du_acc):
        hb = pl.program_id(0)
        ar = jnp.broadcast_to(ar_ref[0], (SEGS, lw))
        ai = jnp.broadcast_to(ai_ref[0], (SEGS, lw))
        zero = jnp.zeros((SEGS, lw), F32)

        def rows_of(c):
            return pl.ds(pl.multiple_of(c * ch, ch), ch)

        nsteps = t // SEGS

        @pl.loop(0, nch)
        def _(c):
            u = u_ref[rows_of(c), :]
            bur[rows_of(c), :] = _sdot(u, bre_ref[0])
            bui[rows_of(c), :] = _sdot(u, bim_ref[0])

        def fwd_scan(carry, store):
            def step(i, s):
                r0 = pl.multiple_of(i * SEGS, SEGS)
                mr, mi = _cmul(ar, ai, s[0], s[1])
                nr, ni = mr + bur[pl.ds(r0, SEGS), :], mi + bui[pl.ds(r0, SEGS), :]
                if store:
                    w0 = pl.multiple_of(i * SEGS + SEGS, SEGS)
                    sr[pl.ds(w0, SEGS), :] = nr
                    si[pl.ds(w0, SEGS), :] = ni
                return nr, ni

            return lax.fori_loop(0, nsteps, step, carry, unroll=8)

        er, ei = fwd_scan((zero, zero), False)
        cin_r, cin_i = _seg_carries(er, ei, pr_ref[0], pi_ref[0], False)
        sr[pl.ds(0, SEGS), :] = cin_r
        si[pl.ds(0, SEGS), :] = cin_i
        fwd_scan((cin_r, cin_i), True)

        @pl.loop(0, nch)
        def _(c):
            dy = dy_ref[rows_of(c), :]
            bur[rows_of(c), :] = _sdot(dy, cre_ref[0], NT)
            bui[rows_of(c), :] = -_sdot(dy, cim_ref[0], NT)

        def rev_local(ii, lam):
            r0 = pl.multiple_of((nsteps - 1 - ii) * SEGS, SEGS)
            mr, mi = _cmul(ar, -ai, lam[0], lam[1])
            return mr + bur[pl.ds(r0, SEGS), :], mi + bui[pl.ds(r0, SEGS), :]

        lr0, li0 = lax.fori_loop(0, nsteps, rev_local, (zero, zero), unroll=8)
        rin = _seg_carries(lr0, li0, pr_ref[0], -pi_ref[0], True)

        def rev_step(ii, st):
            lam_r, lam_i, acc_r, acc_i = st
            r0 = pl.multiple_of((nsteps - 1 - ii) * SEGS, SEGS)
            mr, mi = _cmul(ar, -ai, lam_r, lam_i)
            nr, ni = mr + bur[pl.ds(r0, SEGS), :], mi + bui[pl.ds(r0, SEGS), :]
            bur[pl.ds(r0, SEGS), :] = nr
            bui[pl.ds(r0, SEGS), :] = ni
            pr_, pi_ = sr[pl.ds(r0, SEGS), :], si[pl.ds(r0, SEGS), :]
            return nr, ni, acc_r + (nr * pr_ + ni * pi_), acc_i + (ni * pr_ - nr * pi_)

        _, _, acc_r, acc_i = lax.fori_loop(0, nsteps, rev_step, (rin[0], rin[1], zero, zero), unroll=8)
        dar_ref[0] = _colsum(acc_r)
        dai_ref[0] = _colsum(acc_i)

        dbre_ref[...] = jnp.zeros_like(dbre_ref)
        dbim_ref[...] = jnp.zeros_like(dbim_ref)
        dcre_ref[...] = jnp.zeros_like(dcre_ref)
        dcim_ref[...] = jnp.zeros_like(dcim_ref)

        @pl.loop(0, nch)
        def _(c):
            rows = rows_of(c)
            u = u_ref[rows, :]
            dy = dy_ref[rows, :]
            lam_r, lam_i = bur[rows, :], bui[rows, :]
            du = _sdot(lam_r, bre_ref[0], NT) + _sdot(lam_i, bim_ref[0], NT)

            @pl.when(hb % 2 == 0)
            def _():
                du_acc[rows, :] = du + d_ref[...] * dy

            @pl.when(hb % 2 == 1)
            def _():
                du_ref[rows, :] = (du_acc[rows, :] + du).astype(BF)

            dbre_ref[0] += _sdot(u, lam_r, TN)
            dbim_ref[0] += _sdot(u, lam_i, TN)
            srows = pl.ds(pl.multiple_of(c * ch + SEGS, SEGS), ch)
            dcre_ref[0] += _sdot(sr[srows, :], dy, TN)
            dcim_ref[0] -= _sdot(si[srows, :], dy, TN)

        @pl.when(hb % 2 == 0)
        def _():
            dd_ref[...] = _colsum(dy_ref[...] * u_ref[...])

    vec = pl.BlockSpec((1, 1, lw), lambda h: (h, 0, 0))
    bsp = pl.BlockSpec((1, LANES, lw), lambda h: (h, 0, 0))
    csp = pl.BlockSpec((1, lw, LANES), lambda h: (h, 0, 0))
    act = pl.BlockSpec((t, LANES), lambda h: (0, h // 2))
    dsp = pl.BlockSpec((1, LANES), lambda h: (0, h // 2))
    return _call(
        body, name=name, grid=(S5_NHB,),
        in_specs=[act, act, vec, vec, vec, vec, bsp, bsp, csp, csp, dsp],
        out_specs=[act, vec, vec, bsp, bsp, csp, csp, dsp],
        out_shape=[_sds((t, SSM_W), BF), _sds((S5_NHB, 1, lw), F32), _sds((S5_NHB, 1, lw), F32),
                   _sds((S5_NHB, LANES, lw), F32), _sds((S5_NHB, LANES, lw), F32),
                   _sds((S5_NHB, lw, LANES), F32), _sds((S5_NHB, lw, LANES), F32), _sds((1, SSM_W), F32)],
        scratch=[pltpu.VMEM((t, lw), F32), pltpu.VMEM((t, lw), F32),
                 pltpu.VMEM((t + SEGS, lw), F32), pltpu.VMEM((t + SEGS, lw), F32), pltpu.VMEM((t, LANES), F32)],
        sem=('arbitrary',), vmem=VMEM_BIG,
    )(u_p, dy_p, prm['ar'], prm['ai'], prm['pr'], prm['pi'], prm['bre'], prm['bim'], prm['cre'], prm['cim'], prm['d'])


def s5_prep(t, lam_re, lam_im, log_step, b_re, b_im, c_re, c_im):
    step = jnp.exp(log_step)[:, None]
    mag = jnp.exp(lam_re * step)
    ar, ai = mag * jnp.cos(lam_im * step), mag * jnp.sin(lam_im * step)
    den = lam_re * lam_re + lam_im * lam_im
    nr, ni = ar - 1.0, ai
    fr, fi = (nr * lam_re + ni * lam_im) / den, (ni * lam_re - nr * lam_im) / den
    bbr = fr[..., None] * b_re - fi[..., None] * b_im
    bbi = fr[..., None] * b_im + fi[..., None] * b_re
    gl = S5_LW // SSM_P
    eye = jnp.eye(gl, dtype=F32)
    half = (jnp.arange(S5_NHB) % 2)[:, None, None]

    def bmat(bb):
        x = bb.transpose(0, 2, 1).reshape(S5_NHB, gl, SSM_H, SSM_P)
        x = jnp.einsum('bghp,gk->bghkp', x, eye).reshape(S5_NHB, gl * SSM_H, S5_LW)
        z = jnp.zeros_like(x)
        return jnp.where(half == 0, jnp.concatenate([x, z], axis=1), jnp.concatenate([z, x], axis=1))

    def cmat(cc):
        x = cc.transpose(0, 2, 1).reshape(S5_NHB, gl, SSM_P, SSM_H)
        x = jnp.einsum('bgph,gk->bgpkh', x, eye).reshape(S5_NHB, S5_LW, gl * SSM_H)
        z = jnp.zeros_like(x)
        return jnp.where(half == 0, jnp.concatenate([x, z], axis=2), jnp.concatenate([z, x], axis=2))

    vec = lambda a: a.reshape(S5_NHB, 1, S5_LW)
    ni_steps = float(t // SEGS)
    pmag = jnp.exp(lam_re * step * ni_steps)
    pr, pi = pmag * jnp.cos(lam_im * step * ni_steps), pmag * jnp.sin(lam_im * step * ni_steps)
    return dict(ar=vec(ar), ai=vec(ai), bre=bmat(bbr), bim=bmat(bbi), cre=cmat(c_re), cim=cmat(c_im),
                pr=lax.stop_gradient(vec(pr)), pi=lax.stop_gradient(vec(pi)))


def _gelu(x):
    c = math.sqrt(2.0 / math.pi)
    return 0.5 * x * (1.0 + jnp.tanh(c * (x + 0.044715 * (x * x * x))))


def _gelu_grad(x):
    c = math.sqrt(2.0 / math.pi)
    th = jnp.tanh(c * (x + 0.044715 * (x * x * x)))
    return 0.5 * (1.0 + th) + 0.5 * x * (1.0 - th * th) * (c * (1.0 + 3.0 * 0.044715 * (x * x)))


def glu_fwd(ypre, w_glu, b_glu, *, name, tq=512):
    t = ypre.shape[0]
    tq = _tile(t, tq)

    def body(y_ref, w_ref, b_ref, o_ref):
        yg = _gelu(y_ref[...])
        z = _dot(yg, w_ref[...]) + b_ref[...]
        o_ref[...] = yg * jax.nn.sigmoid(z)

    return _call(body, name=name, grid=(t // tq,),
                 in_specs=[pl.BlockSpec((tq, SSM_W), lambda i: (i, 0)), pl.BlockSpec((SSM_W, SSM_W), lambda i: (0, 0)),
                           pl.BlockSpec((1, SSM_W), lambda i: (0, 0))],
                 out_specs=pl.BlockSpec((tq, SSM_W), lambda i: (i, 0)), out_shape=_sds((t, SSM_W), F32),
                 sem=('parallel',))(ypre, w_glu, b_glu)


def glu_bwd(ypre, dy, w_glu, b_glu, *, name, tq=512):
    t = ypre.shape[0]
    tq = _tile(t, tq)

    def body(y_ref, dy_ref, w_ref, b_ref, dyp_ref, yg_ref, dz_ref, db_ref):
        ypre_ = y_ref[...]
        yg = _gelu(ypre_)
        sig = jax.nn.sigmoid(_dot(yg, w_ref[...]) + b_ref[...])
        dy_ = dy_ref[...]
        dz = dy_ * yg * sig * (1.0 - sig)
        dyg = dy_ * sig + _dot(dz, w_ref[...], NT)
        dyp_ref[...] = dyg * _gelu_grad(ypre_)
        yg_ref[...] = yg.astype(BF)
        dz_ref[...] = dz.astype(BF)

        @pl.when(pl.program_id(0) == 0)
        def _():
            db_ref[...] = jnp.zeros_like(db_ref)

        db_ref[...] += _colsum(dz)

    row = pl.BlockSpec((tq, SSM_W), lambda i: (i, 0))
    vec = pl.BlockSpec((1, SSM_W), lambda i: (0, 0))
    return _call(body, name=name, grid=(t // tq,),
                 in_specs=[row, row, pl.BlockSpec((SSM_W, SSM_W), lambda i: (0, 0)), vec],
                 out_specs=[row, row, row, vec],
                 out_shape=[_sds((t, SSM_W), F32), _sds((t, SSM_W), BF), _sds((t, SSM_W), BF), _sds((1, SSM_W), F32)],
                 sem=('arbitrary',))(ypre, dy, w_glu, b_glu)


def _rope(x, cos, sa, sb):
    return x * cos + pltpu.roll(x, 16, 1) * sa + pltpu.roll(x, 112, 1) * sb


def _rope_t(d, cos, sa, sb):
    return d * cos + pltpu.roll(d * sa, 112, 1) + pltpu.roll(d * sb, 16, 1)


def rope_tables(positions):
    half = QK_ROPE // 2
    inv_freq = ROPE_THETA ** (-jnp.arange(half, dtype=F32) / half)
    ang = positions.astype(F32)[:, None] * inv_freq
    cos, sin = jnp.cos(ang), jnp.sin(ang)
    t = positions.shape[0]
    one, zero = jnp.ones((t, QK_NOPE), F32), jnp.zeros((t, QK_NOPE), F32)
    pad1, pad0 = jnp.ones((t, 32), F32), jnp.zeros((t, 32), F32)
    z16 = jnp.zeros((t, half), F32)
    return (jnp.concatenate([one, cos, cos, pad1], axis=1), jnp.concatenate([zero, z16, sin, pad0], axis=1),
            jnp.concatenate([zero, -sin, z16, pad0], axis=1))


def mla_prep_fwd(proj, tabs, w, *, name):
    t = proj.shape[0]
    tq = _tile(t, ATT_BLK)

    def body(cq_ref, ckv_ref, kr_ref, cos_ref, sa_ref, sb_ref, qn_ref, kvn_ref, wq_ref, wk_ref, wv_ref, qg_ref, kg_ref,
             q_ref, qt_ref, k_ref, kt_ref, v_ref):
        cqn = (_rms(cq_ref[...], Q_LORA)[0] * qn_ref[...]).astype(BF)
        ckvn = (_rms(ckv_ref[...], KV_LORA)[0] * kvn_ref[...]).astype(BF)
        cos, sa, sb = cos_ref[...], sa_ref[...], sb_ref[...]
        kr = kr_ref[...]
        for h in range(MLA_HEADS):
            q = _rms(_dot(cqn, wq_ref[h]), QK_DIM)[0] * qg_ref[...]
            q = _rope(q, cos, sa, sb) * ATT_SCALE
            q_ref[h] = q.astype(BF)
            qt_ref[h, 0] = q.T.astype(BF)
            k = _rms(_dot(ckvn, wk_ref[h]) + kr, QK_DIM)[0] * kg_ref[...]
            k = _rope(k, cos, sa, sb)
            k_ref[h] = k.astype(BF)
            kt_ref[h, 0] = k.T.astype(BF)
            v_ref[h] = _dot(ckvn, wv_ref[h]).astype(BF)

    tab = pl.BlockSpec((tq, LANES), lambda i: (i, 0))
    full = lambda shape: pl.BlockSpec(shape, lambda i: (0,) * len(shape))
    hout = pl.BlockSpec((MLA_HEADS, tq, LANES), lambda i: (0, i, 0))
    tout = pl.BlockSpec((MLA_HEADS, 1, LANES, tq), lambda i: (0, i, 0, 0))
    hshape = _sds((MLA_HEADS, t, LANES), BF)
    tshape = _sds((MLA_HEADS, t // tq, LANES, tq), BF)
    return _call(
        body, name=name, grid=(t // tq,),
        in_specs=[pl.BlockSpec((tq, Q_LORA), lambda i: (i, 2)), pl.BlockSpec((tq, LANES), lambda i: (i, 6)),
                  pl.BlockSpec((tq, LANES), lambda i: (i, 7)), tab, tab, tab,
                  full((1, Q_LORA)), full((1, KV_LORA)), full((MLA_HEADS, Q_LORA, LANES)),
                  full((MLA_HEADS, KV_LORA, LANES)), full((MLA_HEADS, KV_LORA, LANES)), full((1, LANES)), full((1, LANES))],
        out_specs=[hout, tout, hout, tout, hout], out_shape=[hshape, tshape, hshape, tshape, hshape], sem=('parallel',),
    )(proj, proj, proj, *tabs, w['q_norm'], w['kv_norm'], w['wq'], w['wk'], w['wv'], w['q_gain'], w['k_gain'])


def mla_prep_bwd(proj, tabs, w, dq, dk, dv, *, name):
    t = proj.shape[0]
    tq = _tile(t, ATT_BLK)

    def body(cq_ref, ckv_ref, kr_ref, cos_ref, sa_ref, sb_ref, qn_ref, kvn_ref, wq_ref, wk_ref, wv_ref, qg_ref, kg_ref,
             dq_ref, dk_ref, dv_ref,
             dpm_ref, cqn_ref, ckvn_ref, dqr_ref, dkraw_ref, dvb_ref, dqn_ref, dkvn_ref, dqg_ref, dkg_ref):
        cq_h, cq_r = _rms(cq_ref[...], Q_LORA)
        ckv_h, ckv_r = _rms(ckv_ref[...], KV_LORA)
        cqn = (cq_h * qn_ref[...]).astype(BF)
        ckvn = (ckv_h * kvn_ref[...]).astype(BF)
        cqn_ref[...] = cqn
        ckvn_ref[...] = ckvn
        cos, sa, sb = cos_ref[...], sa_ref[...], sb_ref[...]
        kr = kr_ref[...]
        dcqn = jnp.zeros((tq, Q_LORA), F32)
        dckvn = jnp.zeros((tq, KV_LORA), F32)
        dkrope = jnp.zeros((tq, LANES), F32)
        dqg = jnp.zeros((1, LANES), F32)
        dkg = jnp.zeros((1, LANES), F32)
        for h in range(MLA_HEADS):
            qh, qr = _rms(_dot(cqn, wq_ref[h]), QK_DIM)
            dqo = _rope_t(dq_ref[h, 0].T * ATT_SCALE, cos, sa, sb)
            dqg = dqg + _colsum(dqo * qh)
            dqraw = _rms_bwd(qh, qr, dqo * qg_ref[...], QK_DIM).astype(BF)
            dqr_ref[:, h * LANES:(h + 1) * LANES] = dqraw
            dcqn = dcqn + _dot(dqraw, wq_ref[h], NT)
            kh, krs = _rms(_dot(ckvn, wk_ref[h]) + kr, QK_DIM)
            dko = _rope_t(dk_ref[h], cos, sa, sb)
            dkg = dkg + _colsum(dko * kh)
            dkraw = _rms_bwd(kh, krs, dko * kg_ref[...], QK_DIM)
            dkrope = dkrope + dkraw
            dkraw = dkraw.astype(BF)
            dkraw_ref[:, h * LANES:(h + 1) * LANES] = dkraw
            dvb = dv_ref[h].astype(BF)
            dvb_ref[:, h * LANES:(h + 1) * LANES] = dvb
            dckvn = dckvn + _dot(dkraw, wk_ref[h], NT) + _dot(dvb, wv_ref[h], NT)
        dpm_ref[:, 0:Q_LORA] = _rms_bwd(cq_h, cq_r, dcqn * qn_ref[...], Q_LORA).astype(BF)
        dpm_ref[:, Q_LORA:Q_LORA + KV_LORA] = _rms_bwd(ckv_h, ckv_r, dckvn * kvn_ref[...], KV_LORA).astype(BF)
        dpm_ref[:, Q_LORA + KV_LORA:512] = dkrope.astype(BF)

        @pl.when(pl.program_id(0) == 0)
        def _():
            dqn_ref[...] = jnp.zeros_like(dqn_ref)
            dkvn_ref[...] = jnp.zeros_like(dkvn_ref)
            dqg_ref[...] = jnp.zeros_like(dqg_ref)
            dkg_ref[...] = jnp.zeros_like(dkg_ref)

        dqn_ref[...] += _colsum(dcqn * cq_h)
        dkvn_ref[...] += _colsum(dckvn * ckv_h)
        dqg_ref[...] += dqg
        dkg_ref[...] += dkg

    tab = pl.BlockSpec((tq, LANES), lambda i: (i, 0))
    full = lambda shape: pl.BlockSpec(shape, lambda i: (0,) * len(shape))
    hblk = pl.BlockSpec((MLA_HEADS, tq, LANES), lambda i: (0, i, 0))
    wide = pl.BlockSpec((tq, MLA_HEADS * LANES), lambda i: (i, 0))
    return _call(
        body, name=name, grid=(t // tq,),
        in_specs=[pl.BlockSpec((tq, Q_LORA), lambda i: (i, 2)), pl.BlockSpec((tq, LANES), lambda i: (i, 6)),
                  pl.BlockSpec((tq, LANES), lambda i: (i, 7)), tab, tab, tab,
                  full((1, Q_LORA)), full((1, KV_LORA)), full((MLA_HEADS, Q_LORA, LANES)),
                  full((MLA_HEADS, KV_LORA, LANES)), full((MLA_HEADS, KV_LORA, LANES)), full((1, LANES)), full((1, LANES)),
                  pl.BlockSpec((MLA_HEADS, 1, LANES, tq), lambda i: (0, i, 0, 0)), hblk, hblk],
        out_specs=[pl.BlockSpec((tq, 512), lambda i: (i, 0)),
                   pl.BlockSpec((tq, Q_LORA), lambda i: (i, 0)), pl.BlockSpec((tq, KV_LORA), lambda i: (i, 0)),
                   wide, wide, wide, full((1, Q_LORA)), full((1, KV_LORA)), full((1, LANES)), full((1, LANES))],
        out_shape=[_sds((t, 512), BF), _sds((t, Q_LORA), BF), _sds((t, KV_LORA), BF),
                   _sds((t, MLA_HEADS * LANES), BF), _sds((t, MLA_HEADS * LANES), BF), _sds((t, MLA_HEADS * LANES), BF),
                   _sds((1, Q_LORA), F32), _sds((1, KV_LORA), F32), _sds((1, LANES), F32), _sds((1, LANES), F32)],
        sem=('arbitrary',),
    )(proj, proj, proj, *tabs, w['q_norm'], w['kv_norm'], w['wq'], w['wk'], w['wv'], w['q_gain'], w['k_gain'], dq, dk, dv)


ATT_BLK = 256
ATT_SCALE = 1.0 / math.sqrt(QK_DIM)


def _overlapped(grid, make_copies):
    ids = [pl.program_id(a) for a in range(len(grid))]
    first = functools.reduce(jnp.logical_and, [i == 0 for i in ids])
    last = functools.reduce(jnp.logical_and, [i == n - 1 for i, n in zip(ids, grid)])

    @pl.when(first)
    def _():
        for cs in make_copies():
            _start_copies(cs)

    @pl.when(last)
    def _():
        for cs in make_copies():
            _wait_copies(cs)


def flash_fwd(q, kt, v, *, name, gather=()):
    t = q.shape[1]
    blk = _tile(t, ATT_BLK)
    grid = (MLA_HEADS // 2, t // blk)

    def body(q_ref, kt_ref, v_ref, *rest):
        nc = len(gather)
        srcs, (o_ref, lse_ref), dsts, sems = rest[:nc], rest[nc:nc + 2], rest[nc + 2:2 * nc + 2], rest[2 * nc + 2:]
        if nc:
            _overlapped(grid, lambda: [_copies('gather', srcs[i], dsts[i], *sems[3 * i:3 * i + 3]) for i in range(nc)])
        qi = pl.program_id(1)
        row = lax.broadcasted_iota(jnp.int32, (blk, blk), 0)
        col = lax.broadcasted_iota(jnp.int32, (blk, blk), 1)

        def block(j, carry, masked):
            out = []
            for hh in range(2):
                m, l, acc = carry[hh]
                s = _dot(q_ref[hh], kt_ref[hh, j])
                if masked:
                    s = jnp.where(col <= row, s, -jnp.inf)
                m2 = jnp.maximum(m, jnp.max(s, axis=-1, keepdims=True))
                p = jnp.exp(s - m2)
                alpha = jnp.exp(m - m2)
                rows = pl.ds(pl.multiple_of(j * blk, blk), blk)
                out.append((m2, alpha * l + jnp.sum(p, axis=-1, keepdims=True), alpha * acc + _dot(p, v_ref[hh, rows, :])))
            return tuple(out)

        init = (jnp.full((blk, 1), -jnp.inf, F32), jnp.zeros((blk, 1), F32), jnp.zeros((blk, LANES), F32))
        carry = lax.fori_loop(0, qi, lambda j, c: block(j, c, False), (init, init))
        carry = block(qi, carry, True)
        o_acc = jnp.zeros((blk, LANES), F32)
        for hh in range(2):
            m, l, acc = carry[hh]
            o_acc = o_acc + acc / l
            lse_ref[hh, 0] = jnp.broadcast_to(m + jnp.log(l), (blk, LANES)).T[0:1, :]
        o_ref[...] = o_acc

    in_specs = [pl.BlockSpec((2, blk, LANES), lambda p, i: (p, i, 0)),
                pl.BlockSpec((2, t // blk, LANES, blk), lambda p, i: (p, 0, 0, 0)),
                pl.BlockSpec((2, t, LANES), lambda p, i: (p, 0, 0))]
    out_specs = [pl.BlockSpec((blk, LANES), lambda p, i: (i, p)), pl.BlockSpec((2, 1, 1, blk), lambda p, i: (p, i, 0, 0))]
    out_shape = [_sds((t, 512), F32), _sds((MLA_HEADS, t // blk, 1, blk), F32)]
    nc = len(gather)
    return _call(body, name=name, grid=grid, in_specs=in_specs + [_ANY] * nc, out_specs=out_specs + [_ANY] * nc,
                 out_shape=out_shape + [_sds((NDEV,) + g.shape, g.dtype) for g in gather], scratch=_COMM_SCRATCH * nc,
                 sem=('arbitrary', 'arbitrary') if nc else ('parallel', 'parallel'))(q, kt, v, *gather)


def mla_out_bwd(o, dyn, g, *, name):
    t = o.shape[0]
    blk = _tile(t, ATT_BLK)

    def body(o_ref, dh_ref, g_ref, do_ref, dot_ref, delta_ref, dg_ref):
        ov = o_ref[...]
        oh, r = _rms(ov, 512)
        dh = dh_ref[...]
        do = _rms_bwd(oh, r, dh * g_ref[...], 512)
        do_ref[...] = do.astype(BF)
        dd = do * ov
        for pb in range(MLA_HEADS // 2):
            cols = slice(pb * LANES, (pb + 1) * LANES)
            dot_ref[pb, 0] = do[:, cols].T.astype(BF)
            ddt = dd[:, cols].T
            delta_ref[2 * pb, 0] = jnp.sum(ddt[0:V_DIM, :], axis=0, keepdims=True)
            delta_ref[2 * pb + 1, 0] = jnp.sum(ddt[V_DIM:LANES, :], axis=0, keepdims=True)

        @pl.when(pl.program_id(0) == 0)
        def _():
            dg_ref[...] = jnp.zeros_like(dg_ref)

        dg_ref[...] += _colsum(dh * oh)

    return _call(
        body, name=name, grid=(t // blk,),
        in_specs=[pl.BlockSpec((blk, 512), lambda i: (i, 0)), pl.BlockSpec((blk, 512), lambda i: (i, 1)),
                  pl.BlockSpec((1, 512), lambda i: (0, 0))],
        out_specs=[pl.BlockSpec((blk, 512), lambda i: (i, 0)), pl.BlockSpec((MLA_HEADS // 2, 1, LANES, blk), lambda i: (0, i, 0, 0)),
                   pl.BlockSpec((MLA_HEADS, 1, 1, blk), lambda i: (0, i, 0, 0)), pl.BlockSpec((1, 512), lambda i: (0, 0))],
        out_shape=[_sds((t, 512), BF), _sds((MLA_HEADS // 2, t // blk, LANES, blk), BF),
                   _sds((MLA_HEADS, t // blk, 1, blk), F32), _sds((1, 512), F32)],
        sem=('arbitrary',),
    )(o, dyn, g)


def flash_bwd(q, qt, k, kt, v, do, dot, lse, delta, *, name, scatter=()):
    t = q.shape[1]
    blk = _tile(t, ATT_BLK)
    nb = t // blk
    grid = (MLA_HEADS, nb)

    def body(q_ref, qt_ref, k_ref, kt_ref, v_ref, do_ref, dot_ref, lse_ref, delta_ref, *rest):
        nc = len(scatter)
        srcs, (dqt_ref, dk_ref, dv_ref), dsts, sems = rest[:nc], rest[nc:nc + 3], rest[nc + 3:2 * nc + 3], rest[2 * nc + 3:]
        if nc:
            _overlapped(grid, lambda: [_copies('scatter', srcs[i], dsts[i], *sems[3 * i:3 * i + 3]) for i in range(nc)])
        h, j = pl.program_id(0), pl.program_id(1)
        row = lax.broadcasted_iota(jnp.int32, (blk, blk), 0)
        col = lax.broadcasted_iota(jnp.int32, (blk, blk), 1)
        lane = lax.broadcasted_iota(jnp.int32, (1, LANES), 1)
        mine = (lane // V_DIM) == (h % 2)

        @pl.when(j == 0)
        def _():
            dqt_ref[...] = jnp.zeros_like(dqt_ref)

        kv, ktv, vv = k_ref[...], kt_ref[...], v_ref[...]

        def block(i, carry, masked):
            dk, dv = carry
            rows = pl.ds(pl.multiple_of(i * blk, blk), blk)
            pt = jnp.exp(_dot(kv, qt_ref[i]) - lse_ref[i])
            if masked:
                pt = jnp.where(col >= row, pt, 0.0)
            dv = dv + _dot(pt, do_ref[rows, :])
            dst = (pt * (_dot(vv, dot_ref[i]) - delta_ref[i])).astype(BF)
            dk = dk + _dot(dst, q_ref[rows, :])
            dqt_ref[i] += _dot(ktv, dst)
            return dk, dv

        zero = jnp.zeros((blk, LANES), F32)
        carry = block(j, (zero, zero), True)
        ngroups = (nb - 1 - j) // 3

        def group(p, c):
            i0 = j + 1 + 3 * p
            return block(i0 + 2, block(i0 + 1, block(i0, c, False), False), False)

        carry = lax.fori_loop(0, ngroups, group, carry)
        rest = j + 1 + 3 * ngroups
        npairs = (nb - rest) // 2
        carry = lax.fori_loop(0, npairs, lambda p, c: block(rest + 1, block(rest, c, False), False), carry)
        dk, dv = lax.fori_loop(rest + 2 * npairs, nb, lambda i, c: block(i, c, False), carry)
        dk_ref[...] = dk
        dv_ref[...] = jnp.where(mine, dv, 0.0)

    whole = pl.BlockSpec((None, t, LANES), lambda h, j: (h, 0, 0))
    wholet = pl.BlockSpec((None, nb, LANES, blk), lambda h, j: (h, 0, 0, 0))
    kvb = pl.BlockSpec((None, blk, LANES), lambda h, j: (h, j, 0))
    rowv = pl.BlockSpec((None, nb, 1, blk), lambda h, j: (h, 0, 0, 0))
    in_specs = [whole, wholet, kvb, pl.BlockSpec((None, None, LANES, blk), lambda h, j: (h, j, 0, 0)), kvb,
                pl.BlockSpec((t, LANES), lambda h, j: (0, h // 2)),
                pl.BlockSpec((None, nb, LANES, blk), lambda h, j: (h // 2, 0, 0, 0)), rowv, rowv]
    out_specs = [wholet, kvb, kvb]
    out_shape = [_sds((MLA_HEADS, nb, LANES, blk), F32), _sds((MLA_HEADS, t, LANES), F32), _sds((MLA_HEADS, t, LANES), F32)]
    args = (q, qt, k, kt, v, do, dot, lse, delta)
    nc = len(scatter)
    return _call(body, name=name, grid=grid, in_specs=in_specs + [_ANY] * nc, out_specs=out_specs + [_ANY] * nc,
                 out_shape=out_shape + [_sds(s.shape, s.dtype) for s in scatter], scratch=_COMM_SCRATCH * nc,
                 sem=('arbitrary', 'arbitrary') if nc else ('parallel', 'arbitrary'), vmem=VMEM_BIG)(*args, *scatter)


def mix_out_fwd(x, y_ssm, o, g_ssm, g_mla, w_out, *, name, tq=512):
    t = x.shape[0]
    tq = _tile(t, tq)

    def body(x_ref, ys_ref, o_ref, gs_ref, gm_ref, w_ref, x1_ref, yn_ref):
        ns = (_rms(ys_ref[...], SSM_W)[0] * gs_ref[...]).astype(BF)
        nm = (_rms(o_ref[...], 512)[0] * gm_ref[...]).astype(BF)
        yn_ref[:, 0:SSM_W] = ns
        yn_ref[:, SSM_W:D] = nm
        x1_ref[...] = x_ref[...] + _dot(ns, w_ref[0:SSM_W, :]) + _dot(nm, w_ref[SSM_W:D, :])

    row = lambda w: pl.BlockSpec((tq, w), lambda i: (i, 0))
    vec = pl.BlockSpec((1, 512), lambda i: (0, 0))
    return _call(body, name=name, grid=(t // tq,),
                 in_specs=[row(D), row(512), row(512), vec, vec, pl.BlockSpec((D, D), lambda i: (0, 0))],
                 out_specs=[row(D), row(D)], out_shape=[_sds((t, D), F32), _sds((t, D), BF)], sem=('parallel',),
                 )(x, y_ssm, o, g_ssm, g_mla, w_out)


MEM_SCALE = 1.0 / math.sqrt(MEM_HD)


def memkv_fwd(mem, g, wk, wv, kg, *, name):
    def body(m_ref, g_ref, wk_ref, wv_ref, kg_ref, mh_ref, k_ref, v_ref):
        mh = (_rms(m_ref[...], D)[0] * g_ref[...]).astype(BF)
        mh_ref[...] = mh
        for h in range(MEM_HEADS):
            cols = slice(h * LANES, (h + 1) * LANES)
            k_ref[h] = (_rms(_dot(mh, wk_ref[:, cols]), MEM_HD)[0] * kg_ref[...]).astype(BF)
            v_ref[h] = _dot(mh, wv_ref[:, cols]).astype(BF)

    return _call(body, name=name,
                 out_shape=[_sds((N_MEM, D), BF), _sds((MEM_HEADS, N_MEM, LANES), BF), _sds((MEM_HEADS, N_MEM, LANES), BF)],
                 )(mem, g, wk, wv, kg)


def memkv_bwd(mem, g, wk, wv, kg, dk, dv, *, name):
    def body(m_ref, g_ref, wk_ref, wv_ref, kg_ref, dk_ref, dv_ref, dwk_ref, dwv_ref, dkg_ref, dg_ref):
        mhat, _ = _rms(m_ref[...], D)
        mh = (mhat * g_ref[...]).astype(BF)
        lane = lax.broadcasted_iota(jnp.int32, (1, LANES), 1)
        dkg = jnp.zeros((1, LANES), F32)
        dmh = jnp.zeros((N_MEM, D), F32)
        for h in range(MEM_HEADS):
            cols = slice(h * LANES, (h + 1) * LANES)
            kh, kr = _rms(_dot(mh, wk_ref[:, cols]), MEM_HD)
            dko = dk_ref[h]
            dkg = dkg + _colsum(dko * kh)
            dkraw = _rms_bwd(kh, kr, dko * kg_ref[...], MEM_HD).astype(BF)
            dvh = jnp.where((lane // MEM_HD) == (h % 2), dv_ref[h], 0.0).astype(BF)
            dwk_ref[:, cols] = _dot(mh, dkraw, TN)
            dwv_ref[:, cols] = _dot(mh, dvh, TN)
            dmh = dmh + _dot(dkraw, wk_ref[:, cols], NT) + _dot(dvh, wv_ref[:, cols], NT)
        dkg_ref[...] = dkg
        dg_ref[...] = _colsum(dmh * mhat)

    return _call(body, name=name,
                 out_shape=[_sds((D, 512), F32), _sds((D, 512), F32), _sds((1, LANES), F32), _sds((1, D), F32)],
                 )(mem, g, wk, wv, kg, dk, dv)


def memattn_fwd(x, g, wq, qg, kh, vh, wo, g_next, *, name, tq=512):
    t = x.shape[0]
    tq = _tile(t, tq)

    def body(x_ref, g_ref, wq_ref, qg_ref, k_ref, v_ref, wo_ref, gn_ref, x2_ref, hn_ref, h3_ref):
        xv = x_ref[...]
        hn = (_rms(xv, D)[0] * g_ref[...]).astype(BF)
        hn_ref[...] = hn
        out = xv
        for pb in range(MEM_HEADS // 2):
            o = jnp.zeros((tq, LANES), F32)
            for h in (2 * pb, 2 * pb + 1):
                q = _rms(_dot(hn, wq_ref[:, h * LANES:(h + 1) * LANES]), MEM_HD)[0] * qg_ref[...]
                s = _dot(q, k_ref[h], NT) * MEM_SCALE
                p = jnp.exp(s - jnp.max(s, axis=-1, keepdims=True))
                p = p / jnp.sum(p, axis=-1, keepdims=True)
                o = o + _dot(p, v_ref[h])
            out = out + _dot(o, wo_ref[pb * LANES:(pb + 1) * LANES, :])
        x2_ref[...] = out
        h3_ref[...] = (_rms(out, D)[0] * gn_ref[...]).astype(BF)

    full = lambda shape: pl.BlockSpec(shape, lambda i: (0,) * len(shape))
    row = pl.BlockSpec((tq, D), lambda i: (i, 0))
    return _call(body, name=name, grid=(t // tq,),
                 in_specs=[row, full((1, D)), full((D, 512)), full((1, LANES)), full((MEM_HEADS, N_MEM, LANES)),
                           full((MEM_HEADS, N_MEM, LANES)), full((MEM_HEADS * MEM_HD, D)), full((1, D))],
                 out_specs=[row, row, row], out_shape=[_sds((t, D), F32), _sds((t, D), BF), _sds((t, D), BF)],
                 sem=('parallel',))(x, g, wq, qg, kh, vh, wo, g_next)


def memattn_bwd(x, dx2, g, wq, qg, kh, vh, wo, *, name, tq=512):
    t = x.shape[0]
    tq = _tile(t, tq)

    def body(x_ref, dx2_ref, g_ref, wq_ref, qg_ref, k_ref, v_ref, wo_ref,
             dx_ref, dxb_ref, o_ref, dqr_ref, dk_ref, dv_ref, dqg_ref, dg_ref):
        @pl.when(pl.program_id(0) == 0)
        def _():
            dk_ref[...] = jnp.zeros_like(dk_ref)
            dv_ref[...] = jnp.zeros_like(dv_ref)
            dqg_ref[...] = jnp.zeros_like(dqg_ref)
            dg_ref[...] = jnp.zeros_like(dg_ref)

        xhat, xr = _rms(x_ref[...], D)
        hn = (xhat * g_ref[...]).astype(BF)
        dx2 = dx2_ref[...]
        dx2b = dx2.astype(BF)
        dh = jnp.zeros((tq, D), F32)
        dqg = jnp.zeros((1, LANES), F32)
        for pb in range(MEM_HEADS // 2):
            do = _dot(dx2b, wo_ref[pb * LANES:(pb + 1) * LANES, :], NT).astype(BF)
            o = jnp.zeros((tq, LANES), F32)
            for h in (2 * pb, 2 * pb + 1):
                cols = slice(h * LANES, (h + 1) * LANES)
                qh, qr = _rms(_dot(hn, wq_ref[:, cols]), MEM_HD)
                qb = (qh * qg_ref[...]).astype(BF)
                s = _dot(qb, k_ref[h], NT) * MEM_SCALE
                p = jnp.exp(s - jnp.max(s, axis=-1, keepdims=True))
                p = p / jnp.sum(p, axis=-1, keepdims=True)
                pb16 = p.astype(BF)
                o = o + _dot(pb16, v_ref[h])
                dv_ref[h] += _dot(pb16, do, TN)
                dp = _dot(do, v_ref[h], NT)
                ds = (p * (dp - jnp.sum(dp * p, axis=-1, keepdims=True)) * MEM_SCALE).astype(BF)
                dk_ref[h] += _dot(ds, qb, TN)
                dqo = _dot(ds, k_ref[h])
                dqg = dqg + _colsum(dqo * qh)
                dqraw = _rms_bwd(qh, qr, dqo * qg_ref[...], MEM_HD).astype(BF)
                dqr_ref[:, cols] = dqraw
                dh = dh + _dot(dqraw, wq_ref[:, cols], NT)
            o_ref[:, pb * LANES:(pb + 1) * LANES] = o.astype(BF)
        dx = dx2 + _rms_bwd(xhat, xr, dh * g_ref[...], D)
        dx_ref[...] = dx
        dxb_ref[...] = dx.astype(BF)
        dqg_ref[...] += dqg
        dg_ref[...] += _colsum(dh * xhat)

    full = lambda shape: pl.BlockSpec(shape, lambda i: (0,) * len(shape))
    row = lambda w: pl.BlockSpec((tq, w), lambda i: (i, 0))
    return _call(body, name=name, grid=(t // tq,),
                 in_specs=[row(D), row(D), full((1, D)), full((D, 512)), full((1, LANES)), full((MEM_HEADS, N_MEM, LANES)),
                           full((MEM_HEADS, N_MEM, LANES)), full((MEM_HEADS * MEM_HD, D))],
                 out_specs=[row(D), row(D), row(256), row(512), full((MEM_HEADS, N_MEM, LANES)), full((MEM_HEADS, N_MEM, LANES)),
                            full((1, LANES)), full((1, D))],
                 out_shape=[_sds((t, D), F32), _sds((t, D), BF), _sds((t, 256), BF), _sds((t, 512), BF),
                            _sds((MEM_HEADS, N_MEM, LANES), F32), _sds((MEM_HEADS, N_MEM, LANES), F32),
                            _sds((1, LANES), F32), _sds((1, D), F32)],
                 sem=('arbitrary',))(x, dx2, g, wq, qg, kh, vh, wo)


def mlp_fwd(x, h, w1, w2, *, name, tq=1024, tf=512):
    t = x.shape[0]
    tq = _tile(t, tq)

    def body(x_ref, h_ref, w1_ref, w2_ref, o_ref):
        @pl.when(pl.program_id(1) == 0)
        def _():
            o_ref[...] = x_ref[...]

        a = jnp.maximum(_dot(h_ref[...], w1_ref[...]), 0.0)
        o_ref[...] += _dot(a * a, w2_ref[...])

    row = pl.BlockSpec((tq, D), lambda i, f: (i, 0))
    return _call(body, name=name, grid=(t // tq, D_FF // tf),
                 in_specs=[row, row, pl.BlockSpec((None, D, tf), lambda i, f: (f, 0, 0)), pl.BlockSpec((tf, D), lambda i, f: (f, 0))],
                 out_specs=row, out_shape=_sds((t, D), F32), sem=('parallel', 'arbitrary'), vmem=VMEM_BIG)(x, h, w1, w2)


def mlp_bwd(h, dx, w1, w2, *, name, tq=1024, tf=512):
    t = h.shape[0]
    tq = _tile(t, tq)

    def body(h_ref, dx_ref, w1_ref, w2_ref, dh_ref, r_ref, da_ref):
        @pl.when(pl.program_id(1) == 0)
        def _():
            dh_ref[...] = jnp.zeros_like(dh_ref)

        a = jnp.maximum(_dot(h_ref[...], w1_ref[...]), 0.0)
        r_ref[...] = (a * a).astype(BF)
        da = (_dot(dx_ref[...], w2_ref[...], NT) * (2.0 * a)).astype(BF)
        da_ref[...] = da
        dh_ref[...] += _dot(da, w1_ref[...], NT)

    row = pl.BlockSpec((tq, D), lambda i, f: (i, 0))
    act = pl.BlockSpec((tq, tf), lambda i, f: (i, f))
    return _call(body, name=name, grid=(t // tq, D_FF // tf),
                 in_specs=[row, row, pl.BlockSpec((None, D, tf), lambda i, f: (f, 0, 0)), pl.BlockSpec((tf, D), lambda i, f: (f, 0))],
                 out_specs=[row, act, act], out_shape=[_sds((t, D), F32), _sds((t, D_FF), BF), _sds((t, D_FF), BF)],
                 sem=('parallel', 'arbitrary'), vmem=VMEM_BIG)(h, dx, w1, w2)


def loss_fwd_bwd(y, target, *, name, tq=512):
    t = y.shape[0]
    tq = _tile(t, tq)

    def body(y_ref, t_ref, dy_ref, dyb_ref, l_ref):
        @pl.when(pl.program_id(0) == 0)
        def _():
            l_ref[...] = jnp.zeros_like(l_ref)

        e = y_ref[...] - t_ref[...]
        dy = e * (1.0 / D)
        dy_ref[...] = dy
        dyb_ref[...] = dy.astype(BF)
        l_ref[...] += _colsum(e * e) * (0.5 / D)

    row = pl.BlockSpec((tq, D), lambda i: (i, 0))
    return _call(body, name=name, grid=(t // tq,), in_specs=[row, row],
                 out_specs=[row, row, pl.BlockSpec((1, D), lambda i: (0, 0))],
                 out_shape=[_sds((t, D), F32), _sds((t, D), BF), _sds((1, D), F32)], sem=('arbitrary',))(y, target)


def prep_early(w):
    w_in = w['w_in']
    z = lambda r, c: jnp.zeros((r, c), w_in.dtype)
    w_in_pad = jnp.concatenate([w_in[:, :896], z(D, 64), w_in[:, 896:928], z(D, 32)], axis=1)
    wq = w['mla_w_uq'].reshape(Q_LORA, MLA_HEADS, QK_DIM).transpose(1, 0, 2)
    wq = jnp.pad(wq, ((0, 0), (0, 0), (0, LANES - QK_DIM)))
    ukv = w['mla_w_ukv'].reshape(KV_LORA, MLA_HEADS, QK_NOPE + V_DIM).transpose(1, 0, 2)
    wk = jnp.pad(ukv[:, :, :QK_NOPE], ((0, 0), (0, 0), (0, LANES - QK_NOPE)))
    vpart = ukv[:, :, QK_NOPE:]
    zv = jnp.zeros_like(vpart)
    odd = (jnp.arange(MLA_HEADS) % 2)[:, None, None] == 1
    wv = jnp.where(odd, jnp.concatenate([zv, vpart], axis=2), jnp.concatenate([vpart, zv], axis=2))
    return dict(w_in=w_in_pad, w_glu=w['ssm_w_glu'], wq=wq, wk=wk, wv=wv)


def prep_late(w):
    mq = jnp.pad(w['mem_w_q'].reshape(D, MEM_HEADS, MEM_HD), ((0, 0), (0, 0), (0, LANES - MEM_HD))).reshape(D, 512)
    mkv = w['mem_w_kv'].reshape(D, MEM_HEADS, 2 * MEM_HD)
    mk = jnp.pad(mkv[:, :, :MEM_HD], ((0, 0), (0, 0), (0, LANES - MEM_HD))).reshape(D, 512)
    mvp = mkv[:, :, MEM_HD:]
    zm = jnp.zeros_like(mvp)
    modd = (jnp.arange(MEM_HEADS) % 2)[None, :, None] == 1
    mv = jnp.where(modd, jnp.concatenate([zm, mvp], axis=2), jnp.concatenate([mvp, zm], axis=2)).reshape(D, 512)
    return dict(w_out=w['w_out'], mq=mq, mk=mk, mv=mv, mo=w['mem_w_o'], w1=w['mlp_w1'], w2=w['mlp_w2'])


def prep_small(t, s):
    row = lambda a: a.reshape(1, -1)
    pad = lambda a: jnp.pad(a, (0, LANES - a.shape[0])).reshape(1, LANES)
    out = s5_prep(t, s['ssm_lambda_re'], s['ssm_lambda_im'], s['ssm_log_step'], s['ssm_b_re'], s['ssm_b_im'],
                  s['ssm_c_re'], s['ssm_c_im'])
    out.update(d=row(s['ssm_d']), norm_mix=row(s['norm_mix']), b_glu=row(s['ssm_b_glu']), q_norm=row(s['mla_q_norm']),
               kv_norm=row(s['mla_kv_norm']), q_gain=pad(s['mla_q_gain']), k_gain=pad(s['mla_k_gain']),
               g_ssm=row(s['out_norm_ssm']), g_mla=row(s['out_norm_mla']), norm_mem_q=row(s['norm_mem_q']),
               norm_mem_kv=row(s['norm_mem_kv']), mem_q_gain=pad(s['mem_q_gain']), mem_k_gain=pad(s['mem_k_gain']),
               norm_mlp=row(s['norm_mlp']))
    return out


def _perm(a):
    t, c = a.shape
    return a.reshape(SEGS, t // SEGS, c).transpose(1, 0, 2).reshape(t, c)


def _unperm(a):
    t, c = a.shape
    return a.reshape(t // SEGS, SEGS, c).transpose(1, 0, 2).reshape(t, c)


def layer_fwd(l, x, mem, tabs, plan, ws):
    n = lambda s: f'l{l}_{s}'
    wb = prep_early(plan.early(l))
    h1 = rmsnorm_fwd(x, ws['norm_mix'], name=n('norm_mix'))
    proj = mm(h1, wb['w_in'], 'nn', name=n('w_in'))
    u_p = _perm(proj[:, :SSM_W])
    ypre_p = s5_fwd(u_p, ws, name=n('s5'))
    ypre = _unperm(ypre_p)
    y_ssm = glu_fwd(ypre, wb['w_glu'], ws['b_glu'], name=n('glu'))
    mw = dict(q_norm=ws['q_norm'], kv_norm=ws['kv_norm'], wq=wb['wq'], wk=wb['wk'], wv=wb['wv'],
              q_gain=ws['q_gain'], k_gain=ws['k_gain'])
    q, qt, k, kt, v = mla_prep_fwd(proj, tabs, mw, name=n('mla_prep'))
    o, lse, *gathered = flash_fwd(q, kt, v, name=n('flash'), gather=plan.gather_src(l))
    plan.gathered(l, gathered)
    wb.update(prep_late(plan.late(l)))
    x1, yn = mix_out_fwd(x, y_ssm, o, ws['g_ssm'], ws['g_mla'], wb['w_out'], name=n('mix_out'))
    mh, kh, vh = memkv_fwd(mem, ws['norm_mem_kv'], wb['mk'], wb['mv'], ws['mem_k_gain'], name=n('memkv'))
    x2, h2, h3 = memattn_fwd(x1, ws['norm_mem_q'], wb['mq'], ws['mem_q_gain'], kh, vh, wb['mo'], ws['norm_mlp'],
                             name=n('memattn'))
    x3 = mlp_fwd(x2, h3, wb['w1'], wb['w2'], name=n('mlp'))
    saved = dict(x=x, h1=h1, proj=proj, u_p=u_p, ypre=ypre, y_ssm=y_ssm, q=q, qt=qt, k=k, kt=kt, v=v, o=o, lse=lse, x1=x1, yn=yn,
                 kh=kh, vh=vh, x2=x2, h2=h2, h3=h3, mw=mw)
    return x3, wb, saved


def layer_bwd(l, dx3, dx3b, mem, tabs, plan, wb, ws, sv):
    n = lambda s: f'l{l}_{s}_bwd'
    gb, gs = {}, {}
    structs = lambda names: {k: _sds(plan.shapes[k], F32) for k in names}
    dh3, r, da = mlp_bwd(sv['h3'], dx3b, wb['w1'], wb['w2'], name=n('mlp'))
    gb['w1'] = mm(sv['h3'], da, 'tn', name=n('w1'), slots=NDEV)
    gb['w2'] = mm(r, dx3b, 'tn', name=n('w2'))
    dx2, dx2b, gs['norm_mlp'] = rmsnorm_bwd(sv['x2'], ws['norm_mlp'], dh3, dx3, name=n('norm_mlp'))
    dx1, dx1b, o_mem, dqr_mem, dkh, dvh, gs['mem_q_gain'], gs['norm_mem_q'] = memattn_bwd(
        sv['x1'], dx2, ws['norm_mem_q'], wb['mq'], ws['mem_q_gain'], sv['kh'], sv['vh'], wb['mo'], name=n('memattn'))
    gb['mo'] = mm(o_mem, dx2b, 'tn', name=n('mo'))
    gb['mq'] = mm(sv['h2'], dqr_mem, 'tn', name=n('mq'))
    gb['mk'], gb['mv'], gs['mem_k_gain'], gs['norm_mem_kv'] = memkv_bwd(
        mem, ws['norm_mem_kv'], wb['mk'], wb['mv'], ws['mem_k_gain'], dkh, dvh, name=n('memkv'))
    dyn = mm(dx1b, wb['w_out'], 'nt', name=n('w_out_dx'))
    gb['w_out'] = mm(sv['yn'], dx1b, 'tn', name=n('w_out'))
    dy_ssm, _, gs['g_ssm'] = rmsnorm_bwd(sv['y_ssm'], ws['g_ssm'], dyn, None, name=n('out_norm_ssm'), col=0)
    do, dot, delta, gs['g_mla'] = mla_out_bwd(sv['o'], dyn, ws['g_mla'], name=n('out_norm_mla'))
    late = {k: gb.pop(k) for k in ('w_out', 'mq', 'mk', 'mv', 'mo', 'w1', 'w2')}
    plan.late_grads(l, jax.linear_transpose(prep_late, structs(BIG_LATE))(late)[0])
    dq, dk, dv, *received = flash_bwd(sv['q'], sv['qt'], sv['k'], sv['kt'], sv['v'], do, dot, sv['lse'], delta,
                                      name=n('flash'), scatter=plan.scatter_src(l))
    plan.scattered(l, received)
    (dproj_m, cqn, ckvn, dqr, dkr, dvb, gs['q_norm'], gs['kv_norm'], gs['q_gain'], gs['k_gain']) = mla_prep_bwd(
        sv['proj'], tabs, sv['mw'], dq, dk, dv, name=n('mla_prep'))
    by_head = lambda g: g.reshape(g.shape[0], MLA_HEADS, LANES).transpose(1, 0, 2)
    gb['wq'] = by_head(mm(cqn, dqr, 'tn', name=n('wq')))
    gb['wk'] = by_head(mm(ckvn, dkr, 'tn', name=n('wk')))
    gb['wv'] = by_head(mm(ckvn, dvb, 'tn', name=n('wv')))
    dypre, yg, dz, gs['b_glu'] = glu_bwd(sv['ypre'], dy_ssm, wb['w_glu'], ws['b_glu'], name=n('glu'))
    gb['w_glu'] = mm(yg, dz, 'tn', name=n('w_glu'))
    du_p, gs['ar'], gs['ai'], gs['bre'], gs['bim'], gs['cre'], gs['cim'], gs['d'] = s5_bwd(sv['u_p'], _perm(dypre), ws, name=n('s5'))
    dprojb = jnp.concatenate([_unperm(du_p), dproj_m], axis=1)
    dh1 = mm(dprojb, wb['w_in'], 'nt', name=n('w_in_dx'))
    gb['w_in'] = mm(sv['h1'], dprojb, 'tn', name=n('w_in'))
    dx0, dx0b, gs['norm_mix'] = rmsnorm_bwd(sv['x'], ws['norm_mix'], dh1, dx1, name=n('norm_mix'))
    plan.early_grads(l, jax.linear_transpose(prep_early, structs(BIG_EARLY))(gb)[0])
    return dx0, dx0b, gs


def local_step(x, mem, positions, target, small, plan):
    t = x.shape[0]
    tabs = rope_tables(positions)
    layers = []
    for l in range(DEPTH):
        ws, small_vjp = jax.vjp(functools.partial(prep_small, t), {k: small[k][l] for k in SMALL})
        x, wb, sv = layer_fwd(l, x, mem, tabs, plan, ws)
        layers.append((wb, ws, small_vjp, sv))
    dx, dxb, lcols = loss_fwd_bwd(x, target, name='loss')
    loss = jnp.sum(lcols)
    gsmall = [None] * DEPTH
    for l in reversed(range(DEPTH)):
        wb, ws, small_vjp, sv = layers[l]
        dx, dxb, gs = layer_bwd(l, dx, dxb, mem, tabs, plan, wb, ws, sv)
        gs['pr'], gs['pi'] = jnp.zeros_like(ws['pr']), jnp.zeros_like(ws['pi'])
        gsmall[l] = small_vjp(gs)[0]
    return loss, dx, gsmall


class ExchangePlan:
    def __init__(self, shard_shapes, mine, first_early):
        self.shapes = {k: (s[1] * (NDEV if BIG_AXIS[k] == 1 else 1), s[2] * (NDEV if BIG_AXIS[k] == 2 else 1))
                       for k, s in shard_shapes.items()}
        self.shard = {k: s[1:] for k, s in shard_shapes.items()}
        self.shapes['mlp_w1'] = (NDEV,) + self.shard['mlp_w1']
        self.mine = mine
        self.w_early = {0: first_early}
        self.w_late = {}
        self.g_late, self.g_early = {}, {}
        self.r_late, self.r_early = {}, {}

    def _unpack(self, g, names):
        out, r0 = {}, 0
        for k in names:
            nr = math.prod(self.shard[k]) // D
            s = g[:, r0:r0 + nr]
            out[k] = (s.reshape(self.shapes[k]) if k == 'mlp_w1'
                      else _from_slots(s.reshape(NDEV, -1), (1,) + self.shard[k], BIG_AXIS[k])[0])
            r0 += nr
        return out

    def _pack(self, g, names, rows):
        slots = jnp.concatenate([g[k].reshape(NDEV, -1) if k == 'mlp_w1' else _to_slots(g[k][None], BIG_AXIS[k])
                                 for k in names], axis=1)
        return jnp.pad(slots, ((0, 0), (0, rows * D - slots.shape[1]))).astype(BF).reshape(NDEV, rows, D)

    def early(self, l):
        return self._unpack(self.w_early.pop(l), BIG_EARLY)

    def late(self, l):
        return self._unpack(self.w_late.pop(l), BIG_LATE)

    def gather_src(self, l):
        src = [self.mine[l, :LATE_ROWS]]
        if l + 1 < DEPTH:
            src.append(self.mine[l + 1, LATE_ROWS:])
        return tuple(src)

    def gathered(self, l, res):
        self.w_late[l] = res[0]
        if l + 1 < DEPTH:
            self.w_early[l + 1] = res[1]

    def late_grads(self, l, g):
        self.g_late[l] = self._pack(g, BIG_LATE, LATE_ROWS)

    def early_grads(self, l, g):
        self.g_early[l] = self._pack(g, BIG_EARLY, LAYER_ROWS - LATE_ROWS)

    def scatter_src(self, l):
        src = [self.g_late.pop(l)]
        if l + 1 < DEPTH:
            src.append(self.g_early.pop(l + 1))
        return tuple(src)

    def scattered(self, l, res):
        self.r_late[l] = res[0]
        if l + 1 < DEPTH:
            self.r_early[l + 1] = res[1]


def _peer(k):
    x, y, c = lax.axis_index('x'), lax.axis_index('y'), lax.axis_index('c')
    px, py, pc = x ^ ((k >> 2) & 1), y ^ ((k >> 1) & 1), c ^ (k & 1)
    return (px, py, pc), 4 * px + 2 * py + pc


def _copies(kind, src_ref, dst_ref, send_sems, recv_sems, loc_sem):
    _, me = _peer(0)
    src = (lambda p: src_ref.at[p]) if kind == 'scatter' else (lambda p: src_ref)
    local = pltpu.make_async_copy(src(me), dst_ref.at[me], loc_sem)
    sends, recvs = [], []
    for k in range(1, NDEV):
        dev, p = _peer(k)
        for slot, lst in ((me, sends), (p, recvs)):
            lst.append(pltpu.make_async_remote_copy(src_ref=src(p), dst_ref=dst_ref.at[slot], send_sem=send_sems.at[k - 1],
                                                    recv_sem=recv_sems.at[k - 1], device_id=dev,
                                                    device_id_type=pl.DeviceIdType.MESH))
    return local, sends, recvs


def _start_copies(cs):
    local, sends, _ = cs
    local.start()
    for cp in sends:
        cp.start()


def _wait_copies(cs):
    local, sends, recvs = cs
    for cp in sends:
        cp.wait_send()
    for cp in recvs:
        cp.wait_recv()
    local.wait()


_COMM_SCRATCH = (pltpu.SemaphoreType.DMA((NDEV - 1,)), pltpu.SemaphoreType.DMA((NDEV - 1,)), pltpu.SemaphoreType.DMA(()))
_ANY = pl.BlockSpec(memory_space=pl.ANY)


def exchange(scatters, gathers, *, name):
    ins = list(scatters) + list(gathers)
    kinds = ['scatter'] * len(scatters) + ['gather'] * len(gathers)
    n_in = len(ins)
    outs = [_sds(a.shape, a.dtype) for a in scatters] + [_sds((NDEV,) + b.shape, b.dtype) for b in gathers]

    def body(*refs):
        in_refs, out_refs, sems = refs[:n_in], refs[n_in:2 * n_in], refs[2 * n_in:]
        sets = [_copies(kind, in_refs[i], out_refs[i], *sems[3 * i:3 * i + 3]) for i, kind in enumerate(kinds)]
        for cs in sets:
            _start_copies(cs)
        for cs in sets:
            _wait_copies(cs)

    return pl.pallas_call(body, name=name, in_specs=[_ANY] * n_in, out_specs=[_ANY] * n_in, out_shape=outs,
                          scratch_shapes=list(_COMM_SCRATCH * n_in))(*ins)


def adamw(w, m, v, g8, *, name, tr):
    r = w.shape[0]
    c1 = 1.0 / (1.0 - ADAM_B1 ** ADAM_STEP)
    c2 = 1.0 / (1.0 - ADAM_B2 ** ADAM_STEP)

    def body(w_ref, m_ref, v_ref, g_ref, go_ref, d_ref, mo_ref, vo_ref):
        g = g_ref[0].astype(F32)
        for i in range(1, NDEV):
            g = g + g_ref[i].astype(F32)
        m_new = ADAM_B1 * m_ref[...] + (1.0 - ADAM_B1) * g
        v_new = ADAM_B2 * v_ref[...] + (1.0 - ADAM_B2) * (g * g)
        go_ref[...] = g
        mo_ref[...] = m_new
        vo_ref[...] = v_new
        d_ref[...] = -ADAM_LR * ((m_new * c1) / (jnp.sqrt(v_new * c2) + ADAM_EPS) + ADAM_WD * w_ref[...])

    row = pl.BlockSpec((tr, D), lambda i: (i, 0))
    return _call(body, name=name, grid=(r // tr,),
                 in_specs=[row, row, row, pl.BlockSpec((NDEV, tr, D), lambda i: (0, i, 0))],
                 out_specs=[row] * 4, out_shape=[_sds((r, D), F32)] * 4, sem=('parallel',), vmem=VMEM_BIG)(w, m, v, g8)


def _flat_rows(parts, rows):
    flat = jnp.concatenate([p.reshape(-1) for p in parts])
    return jnp.pad(flat, (0, rows * D - flat.shape[0])).reshape(rows, D)


def _unflat(flat2d, shapes):
    flat = flat2d.reshape(-1)
    out, off = [], 0
    for s in shapes:
        n = math.prod(s)
        out.append(flat[off:off + n].reshape(s))
        off += n
    return out


def _to_slots(g, axis):
    l, r, c = g.shape
    if axis == 1:
        return g.reshape(l, NDEV, r // NDEV, c).transpose(1, 0, 2, 3).reshape(NDEV, -1)
    return g.reshape(l, r, NDEV, c // NDEV).transpose(2, 0, 1, 3).reshape(NDEV, -1)


def _from_slots(s, shard_shape, axis):
    l, r, c = shard_shape
    s = s.reshape(NDEV, l, r, c)
    if axis == 1:
        return s.transpose(1, 0, 2, 3).reshape(l, NDEV * r, c)
    return s.transpose(1, 2, 0, 3).reshape(l, r, NDEV * c)


LATE_ROWS = 1280
LAYER_ROWS = 1536
BIG_ROWS = DEPTH * LAYER_ROWS
SMALL_ROWS = 640


def kernel(x, mem, positions, norm_mix, w_in, ssm_lambda_re, ssm_lambda_im, ssm_log_step, ssm_b_re, ssm_b_im, ssm_c_re, ssm_c_im, ssm_d, ssm_w_glu, ssm_b_glu, mla_q_norm, mla_w_uq, mla_kv_norm, mla_w_ukv, mla_q_gain, mla_k_gain, out_norm_ssm, out_norm_mla, w_out, norm_mem_q, norm_mem_kv, mem_w_q, mem_w_kv, mem_q_gain, mem_k_gain, mem_w_o, norm_mlp, mlp_w1, mlp_w2, loss_target, m_norm_mix, m_w_in, m_ssm_lambda_re, m_ssm_lambda_im, m_ssm_log_step, m_ssm_b_re, m_ssm_b_im, m_ssm_c_re, m_ssm_c_im, m_ssm_d, m_ssm_w_glu, m_ssm_b_glu, m_mla_q_norm, m_mla_w_uq, m_mla_kv_norm, m_mla_w_ukv, m_mla_q_gain, m_mla_k_gain, m_out_norm_ssm, m_out_norm_mla, m_w_out, m_norm_mem_q, m_norm_mem_kv, m_mem_w_q, m_mem_w_kv, m_mem_q_gain, m_mem_k_gain, m_mem_w_o, m_norm_mlp, m_mlp_w1, m_mlp_w2, v_norm_mix, v_w_in, v_ssm_lambda_re, v_ssm_lambda_im, v_ssm_log_step, v_ssm_b_re, v_ssm_b_im, v_ssm_c_re, v_ssm_c_im, v_ssm_d, v_ssm_w_glu, v_ssm_b_glu, v_mla_q_norm, v_mla_w_uq, v_mla_kv_norm, v_mla_w_ukv, v_mla_q_gain, v_mla_k_gain, v_out_norm_ssm, v_out_norm_mla, v_w_out, v_norm_mem_q, v_norm_mem_kv, v_mem_w_q, v_mem_w_kv, v_mem_q_gain, v_mem_k_gain, v_mem_w_o, v_norm_mlp, v_mlp_w1, v_mlp_w2):
    wvals = (norm_mix, w_in, ssm_lambda_re, ssm_lambda_im, ssm_log_step, ssm_b_re, ssm_b_im, ssm_c_re, ssm_c_im, ssm_d, ssm_w_glu, ssm_b_glu, mla_q_norm, mla_w_uq, mla_kv_norm, mla_w_ukv, mla_q_gain, mla_k_gain, out_norm_ssm, out_norm_mla, w_out, norm_mem_q, norm_mem_kv, mem_w_q, mem_w_kv, mem_q_gain, mem_k_gain, mem_w_o, norm_mlp, mlp_w1, mlp_w2)
    mvals = (m_norm_mix, m_w_in, m_ssm_lambda_re, m_ssm_lambda_im, m_ssm_log_step, m_ssm_b_re, m_ssm_b_im, m_ssm_c_re, m_ssm_c_im, m_ssm_d, m_ssm_w_glu, m_ssm_b_glu, m_mla_q_norm, m_mla_w_uq, m_mla_kv_norm, m_mla_w_ukv, m_mla_q_gain, m_mla_k_gain, m_out_norm_ssm, m_out_norm_mla, m_w_out, m_norm_mem_q, m_norm_mem_kv, m_mem_w_q, m_mem_w_kv, m_mem_q_gain, m_mem_k_gain, m_mem_w_o, m_norm_mlp, m_mlp_w1, m_mlp_w2)
    vvals = (v_norm_mix, v_w_in, v_ssm_lambda_re, v_ssm_lambda_im, v_ssm_log_step, v_ssm_b_re, v_ssm_b_im, v_ssm_c_re, v_ssm_c_im, v_ssm_d, v_ssm_w_glu, v_ssm_b_glu, v_mla_q_norm, v_mla_w_uq, v_mla_kv_norm, v_mla_w_ukv, v_mla_q_gain, v_mla_k_gain, v_out_norm_ssm, v_out_norm_mla, v_w_out, v_norm_mem_q, v_norm_mem_kv, v_mem_w_q, v_mem_w_kv, v_mem_q_gain, v_mem_k_gain, v_mem_w_o, v_norm_mlp, v_mlp_w1, v_mlp_w2)
    w = dict(zip(WEIGHTS, wvals))
    m = dict(zip(WEIGHTS, mvals))
    v = dict(zip(WEIGHTS, vvals))

    shard_shapes = {k: w[k].shape for k in BIG}
    layer_shapes = [shard_shapes[k][1:] for k in BIG]

    def layer_flat(parts):
        flat = jnp.concatenate([p.reshape(DEPTH, -1) for p in parts], axis=1)
        return jnp.pad(flat, ((0, 0), (0, LAYER_ROWS * D - flat.shape[1]))).reshape(DEPTH, LAYER_ROWS, D)

    mine = layer_flat([w[k].astype(BF) for k in BIG])
    first, = exchange([], [mine[0, LATE_ROWS:]], name='gather_early0')
    plan = ExchangePlan(shard_shapes, mine, first)
    small = {k: w[k] for k in SMALL}
    loss, grad_x, gsmall = local_step(x[0], mem[0], positions[0], loss_target[0], small, plan)
    gs_full = [jnp.stack([gsmall[l][k] for l in range(DEPTH)]) for k in SMALL]
    small_flat = _flat_rows(gs_full, SMALL_ROWS).astype(BF)
    plan.r_early[0], g8_small, losses = exchange([plan.g_early.pop(0)], [small_flat, jnp.full((8, LANES), loss, F32)],
                                                 name='exchange_last')
    loss_all = jnp.sum(losses[:, 0, 0])
    g8_big = jnp.concatenate([r[l] for l in range(DEPTH) for r in (plan.r_late, plan.r_early)], axis=1)

    small_shapes = [w[k].shape for k in SMALL]
    flat_big = lambda d: layer_flat([d[k] for k in BIG]).reshape(BIG_ROWS, D)
    gb, db, mb, vb = adamw(flat_big(w), flat_big(m), flat_big(v), g8_big, name='adamw_big', tr=256)
    gs, ds, ms, vs = adamw(_flat_rows([w[k] for k in SMALL], SMALL_ROWS), _flat_rows([m[k] for k in SMALL], SMALL_ROWS),
                           _flat_rows([v[k] for k in SMALL], SMALL_ROWS), g8_small, name='adamw_small', tr=128)

    def unflat_big(fb):
        fb, out, r0 = fb.reshape(DEPTH, LAYER_ROWS, D), [], 0
        for k, shp in zip(BIG, layer_shapes):
            nr = math.prod(shp) // D
            out.append(fb[:, r0:r0 + nr].reshape(shard_shapes[k]))
            r0 += nr
        return out

    res = {}
    for tag, fb, fs in (('g', gb, gs), ('d', db, ds), ('m', mb, ms), ('v', vb, vs)):
        res[tag] = dict(zip(BIG, unflat_big(fb)))
        res[tag].update(zip(SMALL, _unflat(fs, small_shapes)))
    return (loss_all, grad_x[None], *[res['g'][k] for k in WEIGHTS], *[res['d'][k] for k in WEIGHTS],
            *[res['m'][k] for k in WEIGHTS], *[res['v'][k] for k in WEIGHTS])
```

```python
import functools
import math

import jax
import jax.numpy as jnp
from jax import lax
from jax.experimental import pallas as pl
from jax.experimental.pallas import tpu as pltpu

F32 = jnp.float32
BF = jnp.bfloat16

D = 1024
DEPTH = 4
N_MEM = 256
MEM_HEADS = 4
MEM_HD = 64
SSM_W = 512
SSM_G = 32
SSM_H = 16
SSM_P = 64
MLA_HEADS = 8
QK_NOPE = 64
QK_ROPE = 32
QK_DIM = 96
V_DIM = 64
Q_LORA = 256
KV_LORA = 128
ROPE_THETA = 10000.0
D_FF = 4096
IN_COLS = 928
EPS = 1e-6
NDEV = 8
LANES = 128
SEGS = 32
S5_LW = 256
S5_NHB = (SSM_G * SSM_P) // S5_LW
ADAM_LR = 0.001
ADAM_B1 = 0.9
ADAM_B2 = 0.999
ADAM_EPS = 1e-08
ADAM_WD = 0.01
ADAM_STEP = 10
VMEM_BIG = 56 * 1024 * 1024

NN = (((1,), (0,)), ((), ()))
NT = (((1,), (1,)), ((), ()))
TN = (((0,), (0,)), ((), ()))

BIG_LATE = ('w_out', 'mem_w_q', 'mem_w_kv', 'mem_w_o', 'mlp_w1', 'mlp_w2')
BIG_EARLY = ('w_in', 'ssm_w_glu', 'mla_w_uq', 'mla_w_ukv')
BIG = BIG_LATE + BIG_EARLY
BIG_AXIS = {'w_in': 1, 'ssm_w_glu': 1, 'mla_w_uq': 2, 'mla_w_ukv': 2, 'w_out': 1, 'mem_w_q': 1, 'mem_w_kv': 1,
            'mem_w_o': 2, 'mlp_w1': 2, 'mlp_w2': 1}
SMALL = ('norm_mix', 'ssm_lambda_re', 'ssm_lambda_im', 'ssm_log_step', 'ssm_b_re', 'ssm_b_im', 'ssm_c_re', 'ssm_c_im',
         'ssm_d', 'ssm_b_glu', 'mla_q_norm', 'mla_kv_norm', 'mla_q_gain', 'mla_k_gain', 'out_norm_ssm', 'out_norm_mla',
         'norm_mem_q', 'norm_mem_kv', 'mem_q_gain', 'mem_k_gain', 'norm_mlp')
WEIGHTS = ('norm_mix', 'w_in', 'ssm_lambda_re', 'ssm_lambda_im', 'ssm_log_step', 'ssm_b_re', 'ssm_b_im', 'ssm_c_re',
           'ssm_c_im', 'ssm_d', 'ssm_w_glu', 'ssm_b_glu', 'mla_q_norm', 'mla_w_uq', 'mla_kv_norm', 'mla_w_ukv',
           'mla_q_gain', 'mla_k_gain', 'out_norm_ssm', 'out_norm_mla', 'w_out', 'norm_mem_q', 'norm_mem_kv', 'mem_w_q',
           'mem_w_kv', 'mem_q_gain', 'mem_k_gain', 'mem_w_o', 'norm_mlp', 'mlp_w1', 'mlp_w2')


def _call(body, *, name, out_shape, grid=(), in_specs=None, out_specs=None, scratch=(), sem=None, vmem=None):
    params = {}
    if sem is not None:
        params['dimension_semantics'] = sem
    if vmem is not None:
        params['vmem_limit_bytes'] = vmem
    specs = {} if in_specs is None else dict(grid=grid, in_specs=in_specs, out_specs=out_specs)
    return pl.pallas_call(body, name=name, out_shape=out_shape, scratch_shapes=list(scratch),
                          compiler_params=pltpu.CompilerParams(**params), **specs)


def _sds(shape, dtype):
    return jax.ShapeDtypeStruct(shape, dtype)


def _dot(a, b, dims=NN):
    return lax.dot_general(a.astype(BF), b.astype(BF), dims, preferred_element_type=F32)


def _split(a):
    hi = a.astype(BF)
    return hi, (a - hi.astype(F32)).astype(BF)


def _dot3(a, b, dims=NN):
    ah, al = _split(a)
    bh, bl = _split(b)
    d = lambda p, q: lax.dot_general(p, q, dims, preferred_element_type=F32)
    return d(ah, bh) + (d(ah, bl) + d(al, bh))


_sdot = _dot


def _rms(x, n):
    r = lax.rsqrt(jnp.sum(x * x, axis=-1, keepdims=True) * (1.0 / n) + EPS)
    return x * r, r


def _rms_bwd(xhat, r, dxhat, n):
    return r * (dxhat - xhat * (jnp.sum(dxhat * xhat, axis=-1, keepdims=True) * (1.0 / n)))


def _colsum(a):
    return jnp.sum(a, axis=0, keepdims=True)


def _tile(t, want):
    return min(t, want)


def _bidx(nb):
    return (lambda b: b) if nb > 1 else (lambda b: 0)


def mm(a, b, mode, *, name, out_dtype=F32, tm=1024, tn=1024, slots=0):
    squeeze = a.ndim == 2 and b.ndim == 2
    a = a[None] if a.ndim == 2 else a
    b = b[None] if b.ndim == 2 else b
    nb = max(a.shape[0], b.shape[0])
    ab, bb = _bidx(a.shape[0]), _bidx(b.shape[0])
    if mode in ('nn', 'nt'):
        m, k = a.shape[1:]
        n = b.shape[2] if mode == 'nn' else b.shape[1]
        tm, tn = _tile(m, tm), _tile(n, tn)
        dims = NN if mode == 'nn' else NT

        def body(a_ref, b_ref, o_ref):
            o_ref[...] = _dot(a_ref[...], b_ref[...], dims).astype(o_ref.dtype)

        bspec = (pl.BlockSpec((None, k, tn), lambda bi, i, j: (bb(bi), 0, j)) if mode == 'nn'
                 else pl.BlockSpec((None, tn, k), lambda bi, i, j: (bb(bi), j, 0)))
        out = _call(body, name=name, grid=(nb, m // tm, n // tn),
                    in_specs=[pl.BlockSpec((None, tm, k), lambda bi, i, j: (ab(bi), i, 0)), bspec],
                    out_specs=pl.BlockSpec((None, tm, tn), lambda bi, i, j: (bi, i, j)),
                    out_shape=_sds((nb, m, n), out_dtype), sem=('parallel', 'parallel', 'parallel'), vmem=VMEM_BIG)(a, b)
    else:
        k, m = a.shape[1:]
        n = b.shape[2]
        tm, tn, tk = _tile(m, 1024), _tile(n, 1024), _tile(k, 1024)
        per = 1
        if slots:
            ts = n // slots
            per = tn // ts
            out_spec, out_shape = pl.BlockSpec((per, tm, ts), lambda bi, i, j, kk: (j, i, 0)), _sds((slots, m, ts), F32)
        else:
            out_spec, out_shape = pl.BlockSpec((None, tm, tn), lambda bi, i, j, kk: (bi, i, j)), _sds((nb, m, n), F32)

        def body(a_ref, b_ref, o_ref):
            @pl.when(pl.program_id(3) == 0)
            def _():
                o_ref[...] = jnp.zeros_like(o_ref)

            res = _dot(a_ref[...], b_ref[...], TN)
            if slots:
                for s in range(per):
                    o_ref[s] += res[:, s * ts:(s + 1) * ts]
            else:
                o_ref[...] += res

        out = _call(body, name=name, grid=(nb, m // tm, n // tn, k // tk),
                    in_specs=[pl.BlockSpec((None, tk, tm), lambda bi, i, j, kk: (ab(bi), kk, i)),
                              pl.BlockSpec((None, tk, tn), lambda bi, i, j, kk: (bb(bi), kk, j))],
                    out_specs=out_spec, out_shape=out_shape,
                    sem=('parallel', 'parallel', 'parallel', 'arbitrary'), vmem=VMEM_BIG)(a, b)
    return out[0] if squeeze and not slots else out


def rmsnorm_fwd(x, g, *, name, tq=512):
    t, d = x.shape
    tq = _tile(t, tq)

    def body(x_ref, g_ref, o_ref):
        xh, _ = _rms(x_ref[...], d)
        o_ref[...] = (xh * g_ref[...]).astype(o_ref.dtype)

    return _call(body, name=name, grid=(t // tq,),
                 in_specs=[pl.BlockSpec((tq, d), lambda i: (i, 0)), pl.BlockSpec((1, d), lambda i: (0, 0))],
                 out_specs=pl.BlockSpec((tq, d), lambda i: (i, 0)), out_shape=_sds((t, d), BF), sem=('parallel',))(x, g)


def rmsnorm_bwd(x, g, dh, dres, *, name, col=0, tq=512):
    t, d = x.shape
    tq = _tile(t, tq)
    has_res = dres is not None

    def body(*refs):
        if has_res:
            x_ref, g_ref, dh_ref, dres_ref, dx_ref, dxb_ref, dg_ref = refs
        else:
            x_ref, g_ref, dh_ref, dx_ref, dxb_ref, dg_ref = refs
        xh, r = _rms(x_ref[...], d)
        dh_ = dh_ref[...].astype(F32)
        dx = _rms_bwd(xh, r, dh_ * g_ref[...], d)
        if has_res:
            dx = dx + dres_ref[...]
        dx_ref[...] = dx
        dxb_ref[...] = dx.astype(BF)

        @pl.when(pl.program_id(0) == 0)
        def _():
            dg_ref[...] = jnp.zeros_like(dg_ref)

        dg_ref[...] += _colsum(dh_ * xh)

    in_specs = [pl.BlockSpec((tq, d), lambda i: (i, 0)), pl.BlockSpec((1, d), lambda i: (0, 0)),
                pl.BlockSpec((tq, d), lambda i: (i, col))]
    args = [x, g, dh]
    if has_res:
        in_specs.append(pl.BlockSpec((tq, d), lambda i: (i, 0)))
        args.append(dres)
    row = pl.BlockSpec((tq, d), lambda i: (i, 0))
    return _call(body, name=name, grid=(t // tq,), in_specs=in_specs,
                 out_specs=[row, row, pl.BlockSpec((1, d), lambda i: (0, 0))],
                 out_shape=[_sds((t, d), F32), _sds((t, d), BF), _sds((1, d), F32)], sem=('arbitrary',))(*args)


def _cmul(ar, ai, xr, xi):
    return ar * xr - ai * xi, ar * xi + ai * xr


def _seg_carries(er, ei, pr, pi, reverse):
    lw = er.shape[1]
    zero = jnp.zeros((1, lw), F32)
    order = range(SEGS - 1, -1, -1) if reverse else range(SEGS)
    cin_r, cin_i = [None] * SEGS, [None] * SEGS
    tr, ti = zero, zero
    for j in order:
        cin_r[j], cin_i[j] = tr, ti
        mr, mi = _cmul(pr, pi, tr, ti)
        tr, ti = er[j:j + 1, :] + mr, ei[j:j + 1, :] + mi
    return jnp.concatenate(cin_r, axis=0), jnp.concatenate(cin_i, axis=0)


def _s5_chunk(t):
    return _tile(t, 512)


def s5_fwd(u_p, prm, *, name):
    t = u_p.shape[0]
    ch = _s5_chunk(t)
    nch, steps = t // ch, ch // SEGS
    lw = S5_LW

    def body(u_ref, ar_ref, ai_ref, pr_ref, pi_ref, bre_ref, bim_ref, cre_ref, cim_ref, d_ref, y_ref, bur, bui):
        hb = pl.program_id(0)
        ar = jnp.broadcast_to(ar_ref[0], (SEGS, lw))
        ai = jnp.broadcast_to(ai_ref[0], (SEGS, lw))

        def rows_of(c):
            return pl.ds(pl.multiple_of(c * ch, ch), ch)

        @pl.loop(0, nch)
        def _(c):
            u = u_ref[rows_of(c), :]
            bur[rows_of(c), :] = _sdot(u, bre_ref[0])
            bui[rows_of(c), :] = _sdot(u, bim_ref[0])

        def scan(carry, store):
            def step(i, s):
                r0 = pl.multiple_of(i * SEGS, SEGS)
                mr, mi = _cmul(ar, ai, s[0], s[1])
                nr, ni = mr + bur[pl.ds(r0, SEGS), :], mi + bui[pl.ds(r0, SEGS), :]
                if store:
                    bur[pl.ds(r0, SEGS), :] = nr
                    bui[pl.ds(r0, SEGS), :] = ni
                return nr, ni

            return lax.fori_loop(0, t // SEGS, step, carry, unroll=8)

        zero = jnp.zeros((SEGS, lw), F32)
        er, ei = scan((zero, zero), False)
        scan(_seg_carries(er, ei, pr_ref[0], pi_ref[0], False), True)

        @pl.loop(0, nch)
        def _(c):
            rows = rows_of(c)
            y = _sdot(bur[rows, :], cre_ref[0]) - _sdot(bui[rows, :], cim_ref[0])

            @pl.when(hb % 2 == 0)
            def _():
                y_ref[rows, :] = y + d_ref[...] * u_ref[rows, :]

            @pl.when(hb % 2 == 1)
            def _():
                y_ref[rows, :] += y

    vec = pl.BlockSpec((1, 1, lw), lambda h: (h, 0, 0))
    return _call(
        body, name=name, grid=(S5_NHB,),
        in_specs=[pl.BlockSpec((t, LANES), lambda h: (0, h // 2)), vec, vec, vec, vec,
                  pl.BlockSpec((1, LANES, lw), lambda h: (h, 0, 0)), pl.BlockSpec((1, LANES, lw), lambda h: (h, 0, 0)),
                  pl.BlockSpec((1, lw, LANES), lambda h: (h, 0, 0)), pl.BlockSpec((1, lw, LANES), lambda h: (h, 0, 0)),
                  pl.BlockSpec((1, LANES), lambda h: (0, h // 2))],
        out_specs=pl.BlockSpec((t, LANES), lambda h: (0, h // 2)), out_shape=_sds((t, SSM_W), F32),
        scratch=[pltpu.VMEM((t, lw), F32)] * 2, sem=('arbitrary',), vmem=VMEM_BIG,
    )(u_p, prm['ar'], prm['ai'], prm['pr'], prm['pi'], prm['bre'], prm['bim'], prm['cre'], prm['cim'], prm['d'])


def s5_bwd(u_p, dy_p, prm, *, name):
    t = u_p.shape[0]
    ch = _s5_chunk(t)
    nch, steps = t // ch, ch // SEGS
    lw = S5_LW

    def body(u_ref, dy_ref, ar_ref, ai_ref, pr_ref, pi_ref, bre_ref, bim_ref, cre_ref, cim_ref, d_ref,
             du_ref, dar_ref, dai_ref, dbre_ref, dbim_ref, dcre_ref, dcim_ref, dd_ref, bur, bui, sr, si, du_acc):
        hb = pl.program_id(0)
        ar = jnp.broadcast_to(ar_ref[0], (SEGS, lw))
        ai = jnp.broadcast_to(ai_ref[0], (SEGS, lw))
        zero = jnp.zeros((SEGS, lw), F32)

        def rows_of(c):
            return pl.ds(pl.multiple_of(c * ch, ch), ch)

        nsteps = t // SEGS

        @pl.loop(0, nch)
        def _(c):
            u = u_ref[rows_of(c), :]
            bur[rows_of(c), :] = _sdot(u, bre_ref[0])
            bui[rows_of(c), :] = _sdot(u, bim_ref[0])

        def fwd_scan(carry, store):
            def step(i, s):
                r0 = pl.multiple_of(i * SEGS, SEGS)
                mr, mi = _cmul(ar, ai, s[0], s[1])
                nr, ni = mr + bur[pl.ds(r0, SEGS), :], mi + bui[pl.ds(r0, SEGS), :]
                if store:
                    w0 = pl.multiple_of(i * SEGS + SEGS, SEGS)
                    sr[pl.ds(w0, SEGS), :] = nr
                    si[pl.ds(w0, SEGS), :] = ni
                return nr, ni

            return lax.fori_loop(0, nsteps, step, carry, unroll=8)

        er, ei = fwd_scan((zero, zero), False)
        cin_r, cin_i = _seg_carries(er, ei, pr_ref[0], pi_ref[0], False)
        sr[pl.ds(0, SEGS), :] = cin_r
        si[pl.ds(0, SEGS), :] = cin_i
        fwd_scan((cin_r, cin_i), True)

        @pl.loop(0, nch)
        def _(c):
            dy = dy_ref[rows_of(c), :]
            bur[rows_of(c), :] = _sdot(dy, cre_ref[0], NT)
            bui[rows_of(c), :] = -_sdot(dy, cim_ref[0], NT)

        def rev_local(ii, lam):
            r0 = pl.multiple_of((nsteps - 1 - ii) * SEGS, SEGS)
            mr, mi = _cmul(ar, -ai, lam[0], lam[1])
            return mr + bur[pl.ds(r0, SEGS), :], mi + bui[pl.ds(r0, SEGS), :]

        lr0, li0 = lax.fori_loop(0, nsteps, rev_local, (zero, zero), unroll=8)
        rin = _seg_carries(lr0, li0, pr_ref[0], -pi_ref[0], True)

        def rev_step(ii, st):
            lam_r, lam_i, acc_r, acc_i = st
            r0 = pl.multiple_of((nsteps - 1 - ii) * SEGS, SEGS)
            mr, mi = _cmul(ar, -ai, lam_r, lam_i)
            nr, ni = mr + bur[pl.ds(r0, SEGS), :], mi + bui[pl.ds(r0, SEGS), :]
            bur[pl.ds(r0, SEGS), :] = nr
            bui[pl.ds(r0, SEGS), :] = ni
            pr_, pi_ = sr[pl.ds(r0, SEGS), :], si[pl.ds(r0, SEGS), :]
            return nr, ni, acc_r + (nr * pr_ + ni * pi_), acc_i + (ni * pr_ - nr * pi_)

        _, _, acc_r, acc_i = lax.fori_loop(0, nsteps, rev_step, (rin[0], rin[1], zero, zero), unroll=8)
        dar_ref[0] = _colsum(acc_r)
        dai_ref[0] = _colsum(acc_i)

        dbre_ref[...] = jnp.zeros_like(dbre_ref)
        dbim_ref[...] = jnp.zeros_like(dbim_ref)
        dcre_ref[...] = jnp.zeros_like(dcre_ref)
        dcim_ref[...] = jnp.zeros_like(dcim_ref)

        @pl.loop(0, nch)
        def _(c):
            rows = rows_of(c)
            u = u_ref[rows, :]
            dy = dy_ref[rows, :]
            lam_r, lam_i = bur[rows, :], bui[rows, :]
            du = _sdot(lam_r, bre_ref[0], NT) + _sdot(lam_i, bim_ref[0], NT)

            @pl.when(hb % 2 == 0)
            def _():
                du_acc[rows, :] = du + d_ref[...] * dy

            @pl.when(hb % 2 == 1)
            def _():
                du_ref[rows, :] = (du_acc[rows, :] + du).astype(BF)

            dbre_ref[0] += _sdot(u, lam_r, TN)
            dbim_ref[0] += _sdot(u, lam_i, TN)
            srows = pl.ds(pl.multiple_of(c * ch + SEGS, SEGS), ch)
            dcre_ref[0] += _sdot(sr[srows, :], dy, TN)
            dcim_ref[0] -= _sdot(si[srows, :], dy, TN)

        @pl.when(hb % 2 == 0)
        def _():
            dd_ref[...] = _colsum(dy_ref[...] * u_ref[...])

    vec = pl.BlockSpec((1, 1, lw), lambda h: (h, 0, 0))
    bsp = pl.BlockSpec((1, LANES, lw), lambda h: (h, 0, 0))
    csp = pl.BlockSpec((1, lw, LANES), lambda h: (h, 0, 0))
    act = pl.BlockSpec((t, LANES), lambda h: (0, h // 2))
    dsp = pl.BlockSpec((1, LANES), lambda h: (0, h // 2))
    return _call(
        body, name=name, grid=(S5_NHB,),
        in_specs=[act, act, vec, vec, vec, vec, bsp, bsp, csp, csp, dsp],
        out_specs=[act, vec, vec, bsp, bsp, csp, csp, dsp],
        out_shape=[_sds((t, SSM_W), BF), _sds((S5_NHB, 1, lw), F32), _sds((S5_NHB, 1, lw), F32),
                   _sds((S5_NHB, LANES, lw), F32), _sds((S5_NHB, LANES, lw), F32),
                   _sds((S5_NHB, lw, LANES), F32), _sds((S5_NHB, lw, LANES), F32), _sds((1, SSM_W), F32)],
        scratch=[pltpu.VMEM((t, lw), F32), pltpu.VMEM((t, lw), F32),
                 pltpu.VMEM((t + SEGS, lw), F32), pltpu.VMEM((t + SEGS, lw), F32), pltpu.VMEM((t, LANES), F32)],
        sem=('arbitrary',), vmem=VMEM_BIG,
    )(u_p, dy_p, prm['ar'], prm['ai'], prm['pr'], prm['pi'], prm['bre'], prm['bim'], prm['cre'], prm['cim'], prm['d'])


def s5_prep(t, lam_re, lam_im, log_step, b_re, b_im, c_re, c_im):
    step = jnp.exp(log_step)[:, None]
    mag = jnp.exp(lam_re * step)
    ar, ai = mag * jnp.cos(lam_im * step), mag * jnp.sin(lam_im * step)
    den = lam_re * lam_re + lam_im * lam_im
    nr, ni = ar - 1.0, ai
    fr, fi = (nr * lam_re + ni * lam_im) / den, (ni * lam_re - nr * lam_im) / den
    bbr = fr[..., None] * b_re - fi[..., None] * b_im
    bbi = fr[..., None] * b_im + fi[..., None] * b_re
    gl = S5_LW // SSM_P
    eye = jnp.eye(gl, dtype=F32)
    half = (jnp.arange(S5_NHB) % 2)[:, None, None]

    def bmat(bb):
        x = bb.transpose(0, 2, 1).reshape(S5_NHB, gl, SSM_H, SSM_P)
        x = jnp.einsum('bghp,gk->bghkp', x, eye).reshape(S5_NHB, gl * SSM_H, S5_LW)
        z = jnp.zeros_like(x)
        return jnp.where(half == 0, jnp.concatenate([x, z], axis=1), jnp.concatenate([z, x], axis=1))

    def cmat(cc):
        x = cc.transpose(0, 2, 1).reshape(S5_NHB, gl, SSM_P, SSM_H)
        x = jnp.einsum('bgph,gk->bgpkh', x, eye).reshape(S5_NHB, S5_LW, gl * SSM_H)
        z = jnp.zeros_like(x)
        return jnp.where(half == 0, jnp.concatenate([x, z], axis=2), jnp.concatenate([z, x], axis=2))

    vec = lambda a: a.reshape(S5_NHB, 1, S5_LW)
    ni_steps = float(t // SEGS)
    pmag = jnp.exp(lam_re * step * ni_steps)
    pr, pi = pmag * jnp.cos(lam_im * step * ni_steps), pmag * jnp.sin(lam_im * step * ni_steps)
    return dict(ar=vec(ar), ai=vec(ai), bre=bmat(bbr), bim=bmat(bbi), cre=cmat(c_re), cim=cmat(c_im),
                pr=lax.stop_gradient(vec(pr)), pi=lax.stop_gradient(vec(pi)))


def _gelu(x):
    c = math.sqrt(2.0 / math.pi)
    return 0.5 * x * (1.0 + jnp.tanh(c * (x + 0.044715 * (x * x * x))))


def _gelu_grad(x):
    c = math.sqrt(2.0 / math.pi)
    th = jnp.tanh(c * (x + 0.044715 * (x * x * x)))
    return 0.5 * (1.0 + th) + 0.5 * x * (1.0 - th * th) * (c * (1.0 + 3.0 * 0.044715 * (x * x)))


def glu_fwd(ypre, w_glu, b_glu, *, name, tq=512):
    t = ypre.shape[0]
    tq = _tile(t, tq)

    def body(y_ref, w_ref, b_ref, o_ref):
        yg = _gelu(y_ref[...])
        z = _dot(yg, w_ref[...]) + b_ref[...]
        o_ref[...] = yg * jax.nn.sigmoid(z)

    return _call(body, name=name, grid=(t // tq,),
                 in_specs=[pl.BlockSpec((tq, SSM_W), lambda i: (i, 0)), pl.BlockSpec((SSM_W, SSM_W), lambda i: (0, 0)),
                           pl.BlockSpec((1, SSM_W), lambda i: (0, 0))],
                 out_specs=pl.BlockSpec((tq, SSM_W), lambda i: (i, 0)), out_shape=_sds((t, SSM_W), F32),
                 sem=('parallel',))(ypre, w_glu, b_glu)


def glu_bwd(ypre, dy, w_glu, b_glu, *, name, tq=512):
    t = ypre.shape[0]
    tq = _tile(t, tq)

    def body(y_ref, dy_ref, w_ref, b_ref, dyp_ref, yg_ref, dz_ref, db_ref):
        ypre_ = y_ref[...]
        yg = _gelu(ypre_)
        sig = jax.nn.sigmoid(_dot(yg, w_ref[...]) + b_ref[...])
        dy_ = dy_ref[...]
        dz = dy_ * yg * sig * (1.0 - sig)
        dyg = dy_ * sig + _dot(dz, w_ref[...], NT)
        dyp_ref[...] = dyg * _gelu_grad(ypre_)
        yg_ref[...] = yg.astype(BF)
        dz_ref[...] = dz.astype(BF)

        @pl.when(pl.program_id(0) == 0)
        def _():
            db_ref[...] = jnp.zeros_like(db_ref)

        db_ref[...] += _colsum(dz)

    row = pl.BlockSpec((tq, SSM_W), lambda i: (i, 0))
    vec = pl.BlockSpec((1, SSM_W), lambda i: (0, 0))
    return _call(body, name=name, grid=(t // tq,),
                 in_specs=[row, row, pl.BlockSpec((SSM_W, SSM_W), lambda i: (0, 0)), vec],
                 out_specs=[row, row, row, vec],
                 out_shape=[_sds((t, SSM_W), F32), _sds((t, SSM_W), BF), _sds((t, SSM_W), BF), _sds((1, SSM_W), F32)],
                 sem=('arbitrary',))(ypre, dy, w_glu, b_glu)


def _rope(x, cos, sa, sb):
    return x * cos + pltpu.roll(x, 16, 1) * sa + pltpu.roll(x, 112, 1) * sb


def _rope_t(d, cos, sa, sb):
    return d * cos + pltpu.roll(d * sa, 112, 1) + pltpu.roll(d * sb, 16, 1)


def rope_tables(positions):
    half = QK_ROPE // 2
    inv_freq = ROPE_THETA ** (-jnp.arange(half, dtype=F32) / half)
    ang = positions.astype(F32)[:, None] * inv_freq
    cos, sin = jnp.cos(ang), jnp.sin(ang)
    t = positions.shape[0]
    one, zero = jnp.ones((t, QK_NOPE), F32), jnp.zeros((t, QK_NOPE), F32)
    pad1, pad0 = jnp.ones((t, 32), F32), jnp.zeros((t, 32), F32)
    z16 = jnp.zeros((t, half), F32)
    return (jnp.concatenate([one, cos, cos, pad1], axis=1), jnp.concatenate([zero, z16, sin, pad0], axis=1),
            jnp.concatenate([zero, -sin, z16, pad0], axis=1))


def mla_prep_fwd(proj, tabs, w, *, name):
    t = proj.shape[0]
    tq = _tile(t, ATT_BLK)

    def body(cq_ref, ckv_ref, kr_ref, cos_ref, sa_ref, sb_ref, qn_ref, kvn_ref, wq_ref, wk_ref, wv_ref, qg_ref, kg_ref,
             q_ref, qt_ref, k_ref, kt_ref, v_ref):
        cqn = (_rms(cq_ref[...], Q_LORA)[0] * qn_ref[...]).astype(BF)
        ckvn = (_rms(ckv_ref[...], KV_LORA)[0] * kvn_ref[...]).astype(BF)
        cos, sa, sb = cos_ref[...], sa_ref[...], sb_ref[...]
        kr = kr_ref[...]
        for h in range(MLA_HEADS):
            q = _rms(_dot(cqn, wq_ref[h]), QK_DIM)[0] * qg_ref[...]
            q = _rope(q, cos, sa, sb) * ATT_SCALE
            q_ref[h] = q.astype(BF)
            qt_ref[h, 0] = q.T.astype(BF)
            k = _rms(_dot(ckvn, wk_ref[h]) + kr, QK_DIM)[0] * kg_ref[...]
            k = _rope(k, cos, sa, sb)
            k_ref[h] = k.astype(BF)
            kt_ref[h, 0] = k.T.astype(BF)
            v_ref[h] = _dot(ckvn, wv_ref[h]).astype(BF)

    tab = pl.BlockSpec((tq, LANES), lambda i: (i, 0))
    full = lambda shape: pl.BlockSpec(shape, lambda i: (0,) * len(shape))
    hout = pl.BlockSpec((MLA_HEADS, tq, LANES), lambda i: (0, i, 0))
    tout = pl.BlockSpec((MLA_HEADS, 1, LANES, tq), lambda i: (0, i, 0, 0))
    hshape = _sds((MLA_HEADS, t, LANES), BF)
    tshape = _sds((MLA_HEADS, t // tq, LANES, tq), BF)
    return _call(
        body, name=name, grid=(t // tq,),
        in_specs=[pl.BlockSpec((tq, Q_LORA), lambda i: (i, 2)), pl.BlockSpec((tq, LANES), lambda i: (i, 6)),
                  pl.BlockSpec((tq, LANES), lambda i: (i, 7)), tab, tab, tab,
                  full((1, Q_LORA)), full((1, KV_LORA)), full((MLA_HEADS, Q_LORA, LANES)),
                  full((MLA_HEADS, KV_LORA, LANES)), full((MLA_HEADS, KV_LORA, LANES)), full((1, LANES)), full((1, LANES))],
        out_specs=[hout, tout, hout, tout, hout], out_shape=[hshape, tshape, hshape, tshape, hshape], sem=('parallel',),
    )(proj, proj, proj, *tabs, w['q_norm'], w['kv_norm'], w['wq'], w['wk'], w['wv'], w['q_gain'], w['k_gain'])


def mla_prep_bwd(proj, tabs, w, dq, dk, dv, *, name):
    t = proj.shape[0]
    tq = _tile(t, ATT_BLK)

    def body(cq_ref, ckv_ref, kr_ref, cos_ref, sa_ref, sb_ref, qn_ref, kvn_ref, wq_ref, wk_ref, wv_ref, qg_ref, kg_ref,
             dq_ref, dk_ref, dv_ref,
             dpm_ref, cqn_ref, ckvn_ref, dqr_ref, dkraw_ref, dvb_ref, dqn_ref, dkvn_ref, dqg_ref, dkg_ref):
        cq_h, cq_r = _rms(cq_ref[...], Q_LORA)
        ckv_h, ckv_r = _rms(ckv_ref[...], KV_LORA)
        cqn = (cq_h * qn_ref[...]).astype(BF)
        ckvn = (ckv_h * kvn_ref[...]).astype(BF)
        cqn_ref[...] = cqn
        ckvn_ref[...] = ckvn
        cos, sa, sb = cos_ref[...], sa_ref[...], sb_ref[...]
        kr = kr_ref[...]
        dcqn = jnp.zeros((tq, Q_LORA), F32)
        dckvn = jnp.zeros((tq, KV_LORA), F32)
        dkrope = jnp.zeros((tq, LANES), F32)
        dqg = jnp.zeros((1, LANES), F32)
        dkg = jnp.zeros((1, LANES), F32)
        for h in range(MLA_HEADS):
            qh, qr = _rms(_dot(cqn, wq_ref[h]), QK_DIM)
            dqo = _rope_t(dq_ref[h, 0].T * ATT_SCALE, cos, sa, sb)
            dqg = dqg + _colsum(dqo * qh)
            dqraw = _rms_bwd(qh, qr, dqo * qg_ref[...], QK_DIM).astype(BF)
            dqr_ref[:, h * LANES:(h + 1) * LANES] = dqraw
            dcqn = dcqn + _dot(dqraw, wq_ref[h], NT)
            kh, krs = _rms(_dot(ckvn, wk_ref[h]) + kr, QK_DIM)
            dko = _rope_t(dk_ref[h], cos, sa, sb)
            dkg = dkg + _colsum(dko * kh)
            dkraw = _rms_bwd(kh, krs, dko * kg_ref[...], QK_DIM)
            dkrope = dkrope + dkraw
            dkraw = dkraw.astype(BF)
            dkraw_ref[:, h * LANES:(h + 1) * LANES] = dkraw
            dvb = dv_ref[h].astype(BF)
            dvb_ref[:, h * LANES:(h + 1) * LANES] = dvb
            dckvn = dckvn + _dot(dkraw, wk_ref[h], NT) + _dot(dvb, wv_ref[h], NT)
        dpm_ref[:, 0:Q_LORA] = _rms_bwd(cq_h, cq_r, dcqn * qn_ref[...], Q_LORA).astype(BF)
        dpm_ref[:, Q_LORA:Q_LORA + KV_LORA] = _rms_bwd(ckv_h, ckv_r, dckvn * kvn_ref[...], KV_LORA).astype(BF)
        dpm_ref[:, Q_LORA + KV_LORA:512] = dkrope.astype(BF)

        @pl.when(pl.program_id(0) == 0)
        def _():
            dqn_ref[...] = jnp.zeros_like(dqn_ref)
            dkvn_ref[...] = jnp.zeros_like(dkvn_ref)
            dqg_ref[...] = jnp.zeros_like(dqg_ref)
            dkg_ref[...] = jnp.zeros_like(dkg_ref)

        dqn_ref[...] += _colsum(dcqn * cq_h)
        dkvn_ref[...] += _colsum(dckvn * ckv_h)
        dqg_ref[...] += dqg
        dkg_ref[...] += dkg

    tab = pl.BlockSpec((tq, LANES), lambda i: (i, 0))
    full = lambda shape: pl.BlockSpec(shape, lambda i: (0,) * len(shape))
    hblk = pl.BlockSpec((MLA_HEADS, tq, LANES), lambda i: (0, i, 0))
    wide = pl.BlockSpec((tq, MLA_HEADS * LANES), lambda i: (i, 0))
    return _call(
        body, name=name, grid=(t // tq,),
        in_specs=[pl.BlockSpec((tq, Q_LORA), lambda i: (i, 2)), pl.BlockSpec((tq, LANES), lambda i: (i, 6)),
                  pl.BlockSpec((tq, LANES), lambda i: (i, 7)), tab, tab, tab,
                  full((1, Q_LORA)), full((1, KV_LORA)), full((MLA_HEADS, Q_LORA, LANES)),
                  full((MLA_HEADS, KV_LORA, LANES)), full((MLA_HEADS, KV_LORA, LANES)), full((1, LANES)), full((1, LANES)),
                  pl.BlockSpec((MLA_HEADS, 1, LANES, tq), lambda i: (0, i, 0, 0)), hblk, hblk],
        out_specs=[pl.BlockSpec((tq, 512), lambda i: (i, 0)),
                   pl.BlockSpec((tq, Q_LORA), lambda i: (i, 0)), pl.BlockSpec((tq, KV_LORA), lambda i: (i, 0)),
                   wide, wide, wide, full((1, Q_LORA)), full((1, KV_LORA)), full((1, LANES)), full((1, LANES))],
        out_shape=[_sds((t, 512), BF), _sds((t, Q_LORA), BF), _sds((t, KV_LORA), BF),
                   _sds((t, MLA_HEADS * LANES), BF), _sds((t, MLA_HEADS * LANES), BF), _sds((t, MLA_HEADS * LANES), BF),
                   _sds((1, Q_LORA), F32), _sds((1, KV_LORA), F32), _sds((1, LANES), F32), _sds((1, LANES), F32)],
        sem=('arbitrary',),
    )(proj, proj, proj, *tabs, w['q_norm'], w['kv_norm'], w['wq'], w['wk'], w['wv'], w['q_gain'], w['k_gain'], dq, dk, dv)


ATT_BLK = 256
ATT_SCALE = 1.0 / math.sqrt(QK_DIM)


def _overlapped(grid, make_copies):
    ids = [pl.program_id(a) for a in range(len(grid))]
    first = functools.reduce(jnp.logical_and, [i == 0 for i in ids])
    last = functools.reduce(jnp.logical_and, [i == n - 1 for i, n in zip(ids, grid)])

    @pl.when(first)
    def _():
        for cs in make_copies():
            _start_copies(cs)

    @pl.when(last)
    def _():
        for cs in make_copies():
            _wait_copies(cs)


def flash_fwd(q, kt, v, *, name, gather=()):
    t = q.shape[1]
    blk = _tile(t, ATT_BLK)
    grid = (MLA_HEADS // 2, t // blk)

    def body(q_ref, kt_ref, v_ref, *rest):
        nc = len(gather)
        srcs, (o_ref, lse_ref), dsts, sems = rest[:nc], rest[nc:nc + 2], rest[nc + 2:2 * nc + 2], rest[2 * nc + 2:]
        if nc:
            _overlapped(grid, lambda: [_copies('gather', srcs[i], dsts[i], *sems[3 * i:3 * i + 3]) for i in range(nc)])
        qi = pl.program_id(1)
        row = lax.broadcasted_iota(jnp.int32, (blk, blk), 0)
        col = lax.broadcasted_iota(jnp.int32, (blk, blk), 1)

        def block(j, carry, masked):
            out = []
            for hh in range(2):
                m, l, acc = carry[hh]
                s = _dot(q_ref[hh], kt_ref[hh, j])
                if masked:
                    s = jnp.where(col <= row, s, -jnp.inf)
                m2 = jnp.maximum(m, jnp.max(s, axis=-1, keepdims=True))
                p = jnp.exp(s - m2)
                alpha = jnp.exp(m - m2)
                rows = pl.ds(pl.multiple_of(j * blk, blk), blk)
                out.append((m2, alpha * l + jnp.sum(p, axis=-1, keepdims=True), alpha * acc + _dot(p, v_ref[hh, rows, :])))
            return tuple(out)

        init = (jnp.full((blk, 1), -jnp.inf, F32), jnp.zeros((blk, 1), F32), jnp.zeros((blk, LANES), F32))
        carry = lax.fori_loop(0, qi, lambda j, c: block(j, c, False), (init, init))
        carry = block(qi, carry, True)
        o_acc = jnp.zeros((blk, LANES), F32)
        for hh in range(2):
            m, l, acc = carry[hh]
            o_acc = o_acc + acc / l
            lse_ref[hh, 0] = jnp.broadcast_to(m + jnp.log(l), (blk, LANES)).T[0:1, :]
        o_ref[...] = o_acc

    in_specs = [pl.BlockSpec((2, blk, LANES), lambda p, i: (p, i, 0)),
                pl.BlockSpec((2, t // blk, LANES, blk), lambda p, i: (p, 0, 0, 0)),
                pl.BlockSpec((2, t, LANES), lambda p, i: (p, 0, 0))]
    out_specs = [pl.BlockSpec((blk, LANES), lambda p, i: (i, p)), pl.BlockSpec((2, 1, 1, blk), lambda p, i: (p, i, 0, 0))]
    out_shape = [_sds((t, 512), F32), _sds((MLA_HEADS, t // blk, 1, blk), F32)]
    nc = len(gather)
    return _call(body, name=name, grid=grid, in_specs=in_specs + [_ANY] * nc, out_specs=out_specs + [_ANY] * nc,
                 out_shape=out_shape + [_sds((NDEV,) + g.shape, g.dtype) for g in gather], scratch=_COMM_SCRATCH * nc,
                 sem=('arbitrary', 'arbitrary') if nc else ('parallel', 'parallel'))(q, kt, v, *gather)


def mla_out_bwd(o, dyn, g, *, name):
    t = o.shape[0]
    blk = _tile(t, ATT_BLK)

    def body(o_ref, dh_ref, g_ref, do_ref, dot_ref, delta_ref, dg_ref):
        ov = o_ref[...]
        oh, r = _rms(ov, 512)
        dh = dh_ref[...]
        do = _rms_bwd(oh, r, dh * g_ref[...], 512)
        do_ref[...] = do.astype(BF)
        dd = do * ov
        for pb in range(MLA_HEADS // 2):
            cols = slice(pb * LANES, (pb + 1) * LANES)
            dot_ref[pb, 0] = do[:, cols].T.astype(BF)
            ddt = dd[:, cols].T
            delta_ref[2 * pb, 0] = jnp.sum(ddt[0:V_DIM, :], axis=0, keepdims=True)
            delta_ref[2 * pb + 1, 0] = jnp.sum(ddt[V_DIM:LANES, :], axis=0, keepdims=True)

        @pl.when(pl.program_id(0) == 0)
        def _():
            dg_ref[...] = jnp.zeros_like(dg_ref)

        dg_ref[...] += _colsum(dh * oh)

    return _call(
        body, name=name, grid=(t // blk,),
        in_specs=[pl.BlockSpec((blk, 512), lambda i: (i, 0)), pl.BlockSpec((blk, 512), lambda i: (i, 1)),
                  pl.BlockSpec((1, 512), lambda i: (0, 0))],
        out_specs=[pl.BlockSpec((blk, 512), lambda i: (i, 0)), pl.BlockSpec((MLA_HEADS // 2, 1, LANES, blk), lambda i: (0, i, 0, 0)),
                   pl.BlockSpec((MLA_HEADS, 1, 1, blk), lambda i: (0, i, 0, 0)), pl.BlockSpec((1, 512), lambda i: (0, 0))],
        out_shape=[_sds((t, 512), BF), _sds((MLA_HEADS // 2, t // blk, LANES, blk), BF),
                   _sds((MLA_HEADS, t // blk, 1, blk), F32), _sds((1, 512), F32)],
        sem=('arbitrary',),
    )(o, dyn, g)


def flash_bwd(q, qt, k, kt, v, do, dot, lse, delta, *, name, scatter=()):
    t = q.shape[1]
    blk = _tile(t, ATT_BLK)
    nb = t // blk
    grid = (MLA_HEADS, nb)

    def body(q_ref, qt_ref, k_ref, kt_ref, v_ref, do_ref, dot_ref, lse_ref, delta_ref, *rest):
        nc = len(scatter)
        srcs, (dqt_ref, dk_ref, dv_ref), dsts, sems = rest[:nc], rest[nc:nc + 3], rest[nc + 3:2 * nc + 3], rest[2 * nc + 3:]
        if nc:
            _overlapped(grid, lambda: [_copies('scatter', srcs[i], dsts[i], *sems[3 * i:3 * i + 3]) for i in range(nc)])
        h, j = pl.program_id(0), pl.program_id(1)
        row = lax.broadcasted_iota(jnp.int32, (blk, blk), 0)
        col = lax.broadcasted_iota(jnp.int32, (blk, blk), 1)
        lane = lax.broadcasted_iota(jnp.int32, (1, LANES), 1)
        mine = (lane // V_DIM) == (h % 2)

        @pl.when(j == 0)
        def _():
            dqt_ref[...] = jnp.zeros_like(dqt_ref)

        kv, ktv, vv = k_ref[...], kt_ref[...], v_ref[...]

        def block(i, carry, masked):
            dk, dv = carry
            rows = pl.ds(pl.multiple_of(i * blk, blk), blk)
            pt = jnp.exp(_dot(kv, qt_ref[i]) - lse_ref[i])
            if masked:
                pt = jnp.where(col >= row, pt, 0.0)
            dv = dv + _dot(pt, do_ref[rows, :])
            dst = (pt * (_dot(vv, dot_ref[i]) - delta_ref[i])).astype(BF)
            dk = dk + _dot(dst, q_ref[rows, :])
            dqt_ref[i] += _dot(ktv, dst)
            return dk, dv

        zero = jnp.zeros((blk, LANES), F32)
        carry = block(j, (zero, zero), True)
        ngroups = (nb - 1 - j) // 3

        def group(p, c):
            i0 = j + 1 + 3 * p
            return block(i0 + 2, block(i0 + 1, block(i0, c, False), False), False)

        carry = lax.fori_loop(0, ngroups, group, carry)
        rest = j + 1 + 3 * ngroups
        npairs = (nb - rest) // 2
        carry = lax.fori_loop(0, npairs, lambda p, c: block(rest + 1, block(rest, c, False), False), carry)
        dk, dv = lax.fori_loop(rest + 2 * npairs, nb, lambda i, c: block(i, c, False), carry)
        dk_ref[...] = dk
        dv_ref[...] = jnp.where(mine, dv, 0.0)

    whole = pl.BlockSpec((None, t, LANES), lambda h, j: (h, 0, 0))
    wholet = pl.BlockSpec((None, nb, LANES, blk), lambda h, j: (h, 0, 0, 0))
    kvb = pl.BlockSpec((None, blk, LANES), lambda h, j: (h, j, 0))
    rowv = pl.BlockSpec((None, nb, 1, blk), lambda h, j: (h, 0, 0, 0))
    in_specs = [whole, wholet, kvb, pl.BlockSpec((None, None, LANES, blk), lambda h, j: (h, j, 0, 0)), kvb,
                pl.BlockSpec((t, LANES), lambda h, j: (0, h // 2)),
                pl.BlockSpec((None, nb, LANES, blk), lambda h, j: (h // 2, 0, 0, 0)), rowv, rowv]
    out_specs = [wholet, kvb, kvb]
    out_shape = [_sds((MLA_HEADS, nb, LANES, blk), F32), _sds((MLA_HEADS, t, LANES), F32), _sds((MLA_HEADS, t, LANES), F32)]
    args = (q, qt, k, kt, v, do, dot, lse, delta)
    nc = len(scatter)
    return _call(body, name=name, grid=grid, in_specs=in_specs + [_ANY] * nc, out_specs=out_specs + [_ANY] * nc,
                 out_shape=out_shape + [_sds(s.shape, s.dtype) for s in scatter], scratch=_COMM_SCRATCH * nc,
                 sem=('arbitrary', 'arbitrary') if nc else ('parallel', 'arbitrary'), vmem=VMEM_BIG)(*args, *scatter)


def mix_out_fwd(x, y_ssm, o, g_ssm, g_mla, w_out, *, name, tq=512):
    t = x.shape[0]
    tq = _tile(t, tq)

    def body(x_ref, ys_ref, o_ref, gs_ref, gm_ref, w_ref, x1_ref, yn_ref):
        ns = (_rms(ys_ref[...], SSM_W)[0] * gs_ref[...]).astype(BF)
        nm = (_rms(o_ref[...], 512)[0] * gm_ref[...]).astype(BF)
        yn_ref[:, 0:SSM_W] = ns
        yn_ref[:, SSM_W:D] = nm
        x1_ref[...] = x_ref[...] + _dot(ns, w_ref[0:SSM_W, :]) + _dot(nm, w_ref[SSM_W:D, :])

    row = lambda w: pl.BlockSpec((tq, w), lambda i: (i, 0))
    vec = pl.BlockSpec((1, 512), lambda i: (0, 0))
    return _call(body, name=name, grid=(t // tq,),
                 in_specs=[row(D), row(512), row(512), vec, vec, pl.BlockSpec((D, D), lambda i: (0, 0))],
                 out_specs=[row(D), row(D)], out_shape=[_sds((t, D), F32), _sds((t, D), BF)], sem=('parallel',),
                 )(x, y_ssm, o, g_ssm, g_mla, w_out)


MEM_SCALE = 1.0 / math.sqrt(MEM_HD)


def memkv_fwd(mem, g, wk, wv, kg, *, name):
    def body(m_ref, g_ref, wk_ref, wv_ref, kg_ref, mh_ref, k_ref, v_ref):
        mh = (_rms(m_ref[...], D)[0] * g_ref[...]).astype(BF)
        mh_ref[...] = mh
        for h in range(MEM_HEADS):
            cols = slice(h * LANES, (h + 1) * LANES)
            k_ref[h] = (_rms(_dot(mh, wk_ref[:, cols]), MEM_HD)[0] * kg_ref[...]).astype(BF)
            v_ref[h] = _dot(mh, wv_ref[:, cols]).astype(BF)

    return _call(body, name=name,
                 out_shape=[_sds((N_MEM, D), BF), _sds((MEM_HEADS, N_MEM, LANES), BF), _sds((MEM_HEADS, N_MEM, LANES), BF)],
                 )(mem, g, wk, wv, kg)


def memkv_bwd(mem, g, wk, wv, kg, dk, dv, *, name):
    def body(m_ref, g_ref, wk_ref, wv_ref, kg_ref, dk_ref, dv_ref, dwk_ref, dwv_ref, dkg_ref, dg_ref):
        mhat, _ = _rms(m_ref[...], D)
        mh = (mhat * g_ref[...]).astype(BF)
        lane = lax.broadcasted_iota(jnp.int32, (1, LANES), 1)
        dkg = jnp.zeros((1, LANES), F32)
        dmh = jnp.zeros((N_MEM, D), F32)
        for h in range(MEM_HEADS):
            cols = slice(h * LANES, (h + 1) * LANES)
            kh, kr = _rms(_dot(mh, wk_ref[:, cols]), MEM_HD)
            dko = dk_ref[h]
            dkg = dkg + _colsum(dko * kh)
            dkraw = _rms_bwd(kh, kr, dko * kg_ref[...], MEM_HD).astype(BF)
            dvh = jnp.where((lane // MEM_HD) == (h % 2), dv_ref[h], 0.0).astype(BF)
            dwk_ref[:, cols] = _dot(mh, dkraw, TN)
            dwv_ref[:, cols] = _dot(mh, dvh, TN)
            dmh = dmh + _dot(dkraw, wk_ref[:, cols], NT) + _dot(dvh, wv_ref[:, cols], NT)
        dkg_ref[...] = dkg
        dg_ref[...] = _colsum(dmh * mhat)

    return _call(body, name=name,
                 out_shape=[_sds((D, 512), F32), _sds((D, 512), F32), _sds((1, LANES), F32), _sds((1, D), F32)],
                 )(mem, g, wk, wv, kg, dk, dv)


def memattn_fwd(x, g, wq, qg, kh, vh, wo, g_next, *, name, tq=512):
    t = x.shape[0]
    tq = _tile(t, tq)

    def body(x_ref, g_ref, wq_ref, qg_ref, k_ref, v_ref, wo_ref, gn_ref, x2_ref, hn_ref, h3_ref):
        xv = x_ref[...]
        hn = (_rms(xv, D)[0] * g_ref[...]).astype(BF)
        hn_ref[...] = hn
        out = xv
        for pb in range(MEM_HEADS // 2):
            o = jnp.zeros((tq, LANES), F32)
            for h in (2 * pb, 2 * pb + 1):
                q = _rms(_dot(hn, wq_ref[:, h * LANES:(h + 1) * LANES]), MEM_HD)[0] * qg_ref[...]
                s = _dot(q, k_ref[h], NT) * MEM_SCALE
                p = jnp.exp(s - jnp.max(s, axis=-1, keepdims=True))
                p = p / jnp.sum(p, axis=-1, keepdims=True)
                o = o + _dot(p, v_ref[h])
            out = out + _dot(o, wo_ref[pb * LANES:(pb + 1) * LANES, :])
        x2_ref[...] = out
        h3_ref[...] = (_rms(out, D)[0] * gn_ref[...]).astype(BF)

    full = lambda shape: pl.BlockSpec(shape, lambda i: (0,) * len(shape))
    row = pl.BlockSpec((tq, D), lambda i: (i, 0))
    return _call(body, name=name, grid=(t // tq,),
                 in_specs=[row, full((1, D)), full((D, 512)), full((1, LANES)), full((MEM_HEADS, N_MEM, LANES)),
                           full((MEM_HEADS, N_MEM, LANES)), full((MEM_HEADS * MEM_HD, D)), full((1, D))],
                 out_specs=[row, row, row], out_shape=[_sds((t, D), F32), _sds((t, D), BF), _sds((t, D), BF)],
                 sem=('parallel',))(x, g, wq, qg, kh, vh, wo, g_next)


def memattn_bwd(x, dx2, g, wq, qg, kh, vh, wo, *, name, tq=512):
    t = x.shape[0]
    tq = _tile(t, tq)

    def body(x_ref, dx2_ref, g_ref, wq_ref, qg_ref, k_ref, v_ref, wo_ref,
             dx_ref, dxb_ref, o_ref, dqr_ref, dk_ref, dv_ref, dqg_ref, dg_ref):
        @pl.when(pl.program_id(0) == 0)
        def _():
            dk_ref[...] = jnp.zeros_like(dk_ref)
            dv_ref[...] = jnp.zeros_like(dv_ref)
            dqg_ref[...] = jnp.zeros_like(dqg_ref)
            dg_ref[...] = jnp.zeros_like(dg_ref)

        xhat, xr = _rms(x_ref[...], D)
        hn = (xhat * g_ref[...]).astype(BF)
        dx2 = dx2_ref[...]
        dx2b = dx2.astype(BF)
        dh = jnp.zeros((tq, D), F32)
        dqg = jnp.zeros((1, LANES), F32)
        for pb in range(MEM_HEADS // 2):
            do = _dot(dx2b, wo_ref[pb * LANES:(pb + 1) * LANES, :], NT).astype(BF)
            o = jnp.zeros((tq, LANES), F32)
            for h in (2 * pb, 2 * pb + 1):
                cols = slice(h * LANES, (h + 1) * LANES)
                qh, qr = _rms(_dot(hn, wq_ref[:, cols]), MEM_HD)
                qb = (qh * qg_ref[...]).astype(BF)
                s = _dot(qb, k_ref[h], NT) * MEM_SCALE
                p = jnp.exp(s - jnp.max(s, axis=-1, keepdims=True))
                p = p / jnp.sum(p, axis=-1, keepdims=True)
                pb16 = p.astype(BF)
                o = o + _dot(pb16, v_ref[h])
                dv_ref[h] += _dot(pb16, do, TN)
                dp = _dot(do, v_ref[h], NT)
                ds = (p * (dp - jnp.sum(dp * p, axis=-1, keepdims=True)) * MEM_SCALE).astype(BF)
                dk_ref[h] += _dot(ds, qb, TN)
                dqo = _dot(ds, k_ref[h])
                dqg = dqg + _colsum(dqo * qh)
                dqraw = _rms_bwd(qh, qr, dqo * qg_ref[...], MEM_HD).astype(BF)
                dqr_ref[:, cols] = dqraw
                dh = dh + _dot(dqraw, wq_ref[:, cols], NT)
            o_ref[:, pb * LANES:(pb + 1) * LANES] = o.astype(BF)
        dx = dx2 + _rms_bwd(xhat, xr, dh * g_ref[...], D)
        dx_ref[...] = dx
        dxb_ref[...] = dx.astype(BF)
        dqg_ref[...] += dqg
        dg_ref[...] += _colsum(dh * xhat)

    full = lambda shape: pl.BlockSpec(shape, lambda i: (0,) * len(shape))
    row = lambda w: pl.BlockSpec((tq, w), lambda i: (i, 0))
    return _call(body, name=name, grid=(t // tq,),
                 in_specs=[row(D), row(D), full((1, D)), full((D, 512)), full((1, LANES)), full((MEM_HEADS, N_MEM, LANES)),
                           full((MEM_HEADS, N_MEM, LANES)), full((MEM_HEADS * MEM_HD, D))],
                 out_specs=[row(D), row(D), row(256), row(512), full((MEM_HEADS, N_MEM, LANES)), full((MEM_HEADS, N_MEM, LANES)),
                            full((1, LANES)), full((1, D))],
                 out_shape=[_sds((t, D), F32), _sds((t, D), BF), _sds((t, 256), BF), _sds((t, 512), BF),
                            _sds((MEM_HEADS, N_MEM, LANES), F32), _sds((MEM_HEADS, N_MEM, LANES), F32),
                            _sds((1, LANES), F32), _sds((1, D), F32)],
                 sem=('arbitrary',))(x, dx2, g, wq, qg, kh, vh, wo)


def mlp_fwd(x, h, w1, w2, g_next, *, name, tq=1024, tf=512):
    t = x.shape[0]
    tq = _tile(t, tq)
    nf = D_FF // tf

    def body(x_ref, h_ref, w1_ref, w2_ref, *rest):
        o_ref = rest[-2] if g_next is not None else rest[-1]

        @pl.when(pl.program_id(1) == 0)
        def _():
            o_ref[...] = x_ref[...]

        a = jnp.maximum(_dot(h_ref[...], w1_ref[...]), 0.0)
        o_ref[...] += _dot(a * a, w2_ref[...])
        if g_next is not None:
            g_ref, hn_ref = rest[0], rest[-1]

            @pl.when(pl.program_id(1) == nf - 1)
            def _():
                hn_ref[...] = (_rms(o_ref[...], D)[0] * g_ref[...]).astype(BF)

    row = pl.BlockSpec((tq, D), lambda i, f: (i, 0))
    in_specs = [row, row, pl.BlockSpec((None, D, tf), lambda i, f: (f, 0, 0)), pl.BlockSpec((tf, D), lambda i, f: (f, 0))]
    if g_next is None:
        return _call(body, name=name, grid=(t // tq, nf), in_specs=in_specs, out_specs=row, out_shape=_sds((t, D), F32),
                     sem=('parallel', 'arbitrary'), vmem=VMEM_BIG)(x, h, w1, w2), None
    return _call(body, name=name, grid=(t // tq, nf), in_specs=in_specs + [pl.BlockSpec((1, D), lambda i, f: (0, 0))],
                 out_specs=[row, row], out_shape=[_sds((t, D), F32), _sds((t, D), BF)],
                 sem=('parallel', 'arbitrary'), vmem=VMEM_BIG)(x, h, w1, w2, g_next)


def mlp_bwd(h, dx, w1, w2, *, name, tq=1024, tf=512):
    t = h.shape[0]
    tq = _tile(t, tq)

    def body(h_ref, dx_ref, w1_ref, w2_ref, dh_ref, r_ref, da_ref):
        @pl.when(pl.program_id(1) == 0)
        def _():
            dh_ref[...] = jnp.zeros_like(dh_ref)

        a = jnp.maximum(_dot(h_ref[...], w1_ref[...]), 0.0)
        r_ref[...] = (a * a).astype(BF)
        da = (_dot(dx_ref[...], w2_ref[...], NT) * (2.0 * a)).astype(BF)
        da_ref[...] = da
        dh_ref[...] += _dot(da, w1_ref[...], NT)

    row = pl.BlockSpec((tq, D), lambda i, f: (i, 0))
    act = pl.BlockSpec((tq, tf), lambda i, f: (i, f))
    return _call(body, name=name, grid=(t // tq, D_FF // tf),
                 in_specs=[row, row, pl.BlockSpec((None, D, tf), lambda i, f: (f, 0, 0)), pl.BlockSpec((tf, D), lambda i, f: (f, 0))],
                 out_specs=[row, act, act], out_shape=[_sds((t, D), F32), _sds((t, D_FF), BF), _sds((t, D_FF), BF)],
                 sem=('parallel', 'arbitrary'), vmem=VMEM_BIG)(h, dx, w1, w2)


def loss_fwd_bwd(y, target, *, name, tq=512):
    t = y.shape[0]
    tq = _tile(t, tq)

    def body(y_ref, t_ref, dy_ref, dyb_ref, l_ref):
        @pl.when(pl.program_id(0) == 0)
        def _():
            l_ref[...] = jnp.zeros_like(l_ref)

        e = y_ref[...] - t_ref[...]
        dy = e * (1.0 / D)
        dy_ref[...] = dy
        dyb_ref[...] = dy.astype(BF)
        l_ref[...] += _colsum(e * e) * (0.5 / D)

    row = pl.BlockSpec((tq, D), lambda i: (i, 0))
    return _call(body, name=name, grid=(t // tq,), in_specs=[row, row],
                 out_specs=[row, row, pl.BlockSpec((1, D), lambda i: (0, 0))],
                 out_shape=[_sds((t, D), F32), _sds((t, D), BF), _sds((1, D), F32)], sem=('arbitrary',))(y, target)


def prep_early(w):
    w_in = w['w_in']
    z = lambda r, c: jnp.zeros((r, c), w_in.dtype)
    w_in_pad = jnp.concatenate([w_in[:, :896], z(D, 64), w_in[:, 896:928], z(D, 32)], axis=1)
    wq = w['mla_w_uq'].reshape(Q_LORA, MLA_HEADS, QK_DIM).transpose(1, 0, 2)
    wq = jnp.pad(wq, ((0, 0), (0, 0), (0, LANES - QK_DIM)))
    ukv = w['mla_w_ukv'].reshape(KV_LORA, MLA_HEADS, QK_NOPE + V_DIM).transpose(1, 0, 2)
    wk = jnp.pad(ukv[:, :, :QK_NOPE], ((0, 0), (0, 0), (0, LANES - QK_NOPE)))
    vpart = ukv[:, :, QK_NOPE:]
    zv = jnp.zeros_like(vpart)
    odd = (jnp.arange(MLA_HEADS) % 2)[:, None, None] == 1
    wv = jnp.where(odd, jnp.concatenate([zv, vpart], axis=2), jnp.concatenate([vpart, zv], axis=2))
    return dict(w_in=w_in_pad, w_glu=w['ssm_w_glu'], wq=wq, wk=wk, wv=wv)


def prep_late(w):
    mq = jnp.pad(w['mem_w_q'].reshape(D, MEM_HEADS, MEM_HD), ((0, 0), (0, 0), (0, LANES - MEM_HD))).reshape(D, 512)
    mkv = w['mem_w_kv'].reshape(D, MEM_HEADS, 2 * MEM_HD)
    mk = jnp.pad(mkv[:, :, :MEM_HD], ((0, 0), (0, 0), (0, LANES - MEM_HD))).reshape(D, 512)
    mvp = mkv[:, :, MEM_HD:]
    zm = jnp.zeros_like(mvp)
    modd = (jnp.arange(MEM_HEADS) % 2)[None, :, None] == 1
    mv = jnp.where(modd, jnp.concatenate([zm, mvp], axis=2), jnp.concatenate([mvp, zm], axis=2)).reshape(D, 512)
    return dict(w_out=w['w_out'], mq=mq, mk=mk, mv=mv, mo=w['mem_w_o'], w1=w['mlp_w1'], w2=w['mlp_w2'])


def prep_small(t, s):
    row = lambda a: a.reshape(1, -1)
    pad = lambda a: jnp.pad(a, (0, LANES - a.shape[0])).reshape(1, LANES)
    out = s5_prep(t, s['ssm_lambda_re'], s['ssm_lambda_im'], s['ssm_log_step'], s['ssm_b_re'], s['ssm_b_im'],
                  s['ssm_c_re'], s['ssm_c_im'])
    out.update(d=row(s['ssm_d']), norm_mix=row(s['norm_mix']), b_glu=row(s['ssm_b_glu']), q_norm=row(s['mla_q_norm']),
               kv_norm=row(s['mla_kv_norm']), q_gain=pad(s['mla_q_gain']), k_gain=pad(s['mla_k_gain']),
               g_ssm=row(s['out_norm_ssm']), g_mla=row(s['out_norm_mla']), norm_mem_q=row(s['norm_mem_q']),
               norm_mem_kv=row(s['norm_mem_kv']), mem_q_gain=pad(s['mem_q_gain']), mem_k_gain=pad(s['mem_k_gain']),
               norm_mlp=row(s['norm_mlp']))
    return out


def _perm(a):
    t, c = a.shape
    return a.reshape(SEGS, t // SEGS, c).transpose(1, 0, 2).reshape(t, c)


def _unperm(a):
    t, c = a.shape
    return a.reshape(t // SEGS, SEGS, c).transpose(1, 0, 2).reshape(t, c)


def layer_fwd(l, x, h1, mem, tabs, plan, ws, g_next):
    n = lambda s: f'l{l}_{s}'
    wb = prep_early(plan.early(l))
    if h1 is None:
        h1 = rmsnorm_fwd(x, ws['norm_mix'], name=n('norm_mix'))
    proj = mm(h1, wb['w_in'], 'nn', name=n('w_in'))
    u_p = _perm(proj[:, :SSM_W])
    ypre_p = s5_fwd(u_p, ws, name=n('s5'))
    ypre = _unperm(ypre_p)
    y_ssm = glu_fwd(ypre, wb['w_glu'], ws['b_glu'], name=n('glu'))
    mw = dict(q_norm=ws['q_norm'], kv_norm=ws['kv_norm'], wq=wb['wq'], wk=wb['wk'], wv=wb['wv'],
              q_gain=ws['q_gain'], k_gain=ws['k_gain'])
    q, qt, k, kt, v = mla_prep_fwd(proj, tabs, mw, name=n('mla_prep'))
    o, lse, *gathered = flash_fwd(q, kt, v, name=n('flash'), gather=plan.gather_src(l))
    plan.gathered(l, gathered)
    wb.update(prep_late(plan.late(l)))
    x1, yn = mix_out_fwd(x, y_ssm, o, ws['g_ssm'], ws['g_mla'], wb['w_out'], name=n('mix_out'))
    mh, kh, vh = memkv_fwd(mem, ws['norm_mem_kv'], wb['mk'], wb['mv'], ws['mem_k_gain'], name=n('memkv'))
    x2, h2, h3 = memattn_fwd(x1, ws['norm_mem_q'], wb['mq'], ws['mem_q_gain'], kh, vh, wb['mo'], ws['norm_mlp'],
                             name=n('memattn'))
    x3, h1_next = mlp_fwd(x2, h3, wb['w1'], wb['w2'], g_next, name=n('mlp'))
    saved = dict(x=x, h1=h1, proj=proj, u_p=u_p, ypre=ypre, y_ssm=y_ssm, q=q, qt=qt, k=k, kt=kt, v=v, o=o, lse=lse, x1=x1, yn=yn,
                 kh=kh, vh=vh, x2=x2, h2=h2, h3=h3, mw=mw)
    return x3, h1_next, wb, saved


def layer_bwd(l, dx3, dx3b, mem, tabs, plan, wb, ws, sv):
    n = lambda s: f'l{l}_{s}_bwd'
    gb, gs = {}, {}
    structs = lambda names: {k: _sds(plan.shapes[k], F32) for k in names}
    dh3, r, da = mlp_bwd(sv['h3'], dx3b, wb['w1'], wb['w2'], name=n('mlp'))
    gb['w1'] = mm(sv['h3'], da, 'tn', name=n('w1'), slots=NDEV)
    gb['w2'] = mm(r, dx3b, 'tn', name=n('w2'))
    dx2, dx2b, gs['norm_mlp'] = rmsnorm_bwd(sv['x2'], ws['norm_mlp'], dh3, dx3, name=n('norm_mlp'))
    dx1, dx1b, o_mem, dqr_mem, dkh, dvh, gs['mem_q_gain'], gs['norm_mem_q'] = memattn_bwd(
        sv['x1'], dx2, ws['norm_mem_q'], wb['mq'], ws['mem_q_gain'], sv['kh'], sv['vh'], wb['mo'], name=n('memattn'))
    gb['mo'] = mm(o_mem, dx2b, 'tn', name=n('mo'))
    gb['mq'] = mm(sv['h2'], dqr_mem, 'tn', name=n('mq'))
    gb['mk'], gb['mv'], gs['mem_k_gain'], gs['norm_mem_kv'] = memkv_bwd(
        mem, ws['norm_mem_kv'], wb['mk'], wb['mv'], ws['mem_k_gain'], dkh, dvh, name=n('memkv'))
    dyn = mm(dx1b, wb['w_out'], 'nt', name=n('w_out_dx'))
    gb['w_out'] = mm(sv['yn'], dx1b, 'tn', name=n('w_out'))
    dy_ssm, _, gs['g_ssm'] = rmsnorm_bwd(sv['y_ssm'], ws['g_ssm'], dyn, None, name=n('out_norm_ssm'), col=0)
    do, dot, delta, gs['g_mla'] = mla_out_bwd(sv['o'], dyn, ws['g_mla'], name=n('out_norm_mla'))
    late = {k: gb.pop(k) for k in ('w_out', 'mq', 'mk', 'mv', 'mo', 'w1', 'w2')}
    plan.late_grads(l, jax.linear_transpose(prep_late, structs(BIG_LATE))(late)[0])
    dq, dk, dv, *received = flash_bwd(sv['q'], sv['qt'], sv['k'], sv['kt'], sv['v'], do, dot, sv['lse'], delta,
                                      name=n('flash'), scatter=plan.scatter_src(l))
    plan.scattered(l, received)
    (dproj_m, cqn, ckvn, dqr, dkr, dvb, gs['q_norm'], gs['kv_norm'], gs['q_gain'], gs['k_gain']) = mla_prep_bwd(
        sv['proj'], tabs, sv['mw'], dq, dk, dv, name=n('mla_prep'))
    by_head = lambda g: g.reshape(g.shape[0], MLA_HEADS, LANES).transpose(1, 0, 2)
    gb['wq'] = by_head(mm(cqn, dqr, 'tn', name=n('wq')))
    gb['wk'] = by_head(mm(ckvn, dkr, 'tn', name=n('wk')))
    gb['wv'] = by_head(mm(ckvn, dvb, 'tn', name=n('wv')))
    dypre, yg, dz, gs['b_glu'] = glu_bwd(sv['ypre'], dy_ssm, wb['w_glu'], ws['b_glu'], name=n('glu'))
    gb['w_glu'] = mm(yg, dz, 'tn', name=n('w_glu'))
    du_p, gs['ar'], gs['ai'], gs['bre'], gs['bim'], gs['cre'], gs['cim'], gs['d'] = s5_bwd(sv['u_p'], _perm(dypre), ws, name=n('s5'))
    dprojb = jnp.concatenate([_unperm(du_p), dproj_m], axis=1)
    dh1 = mm(dprojb, wb['w_in'], 'nt', name=n('w_in_dx'))
    gb['w_in'] = mm(sv['h1'], dprojb, 'tn', name=n('w_in'))
    dx0, dx0b, gs['norm_mix'] = rmsnorm_bwd(sv['x'], ws['norm_mix'], dh1, dx1, name=n('norm_mix'))
    plan.early_grads(l, jax.linear_transpose(prep_early, structs(BIG_EARLY))(gb)[0])
    return dx0, dx0b, gs


def local_step(x, mem, positions, target, small, plan):
    t = x.shape[0]
    tabs = rope_tables(positions)
    preps = [jax.vjp(functools.partial(prep_small, t), {k: small[k][l] for k in SMALL}) for l in range(DEPTH)]
    layers, h1 = [], None
    for l in range(DEPTH):
        ws, small_vjp = preps[l]
        g_next = preps[l + 1][0]['norm_mix'] if l + 1 < DEPTH else None
        x, h1, wb, sv = layer_fwd(l, x, h1, mem, tabs, plan, ws, g_next)
        layers.append((wb, ws, small_vjp, sv))
    dx, dxb, lcols = loss_fwd_bwd(x, target, name='loss')
    loss = jnp.sum(lcols)
    gsmall = [None] * DEPTH
    for l in reversed(range(DEPTH)):
        wb, ws, small_vjp, sv = layers[l]
        dx, dxb, gs = layer_bwd(l, dx, dxb, mem, tabs, plan, wb, ws, sv)
        gs['pr'], gs['pi'] = jnp.zeros_like(ws['pr']), jnp.zeros_like(ws['pi'])
        gsmall[l] = small_vjp(gs)[0]
    return loss, dx, gsmall


class ExchangePlan:
    def __init__(self, shard_shapes, mine, first_early):
        self.shapes = {k: (s[1] * (NDEV if BIG_AXIS[k] == 1 else 1), s[2] * (NDEV if BIG_AXIS[k] == 2 else 1))
                       for k, s in shard_shapes.items()}
        self.shard = {k: s[1:] for k, s in shard_shapes.items()}
        self.shapes['mlp_w1'] = (NDEV,) + self.shard['mlp_w1']
        self.mine = mine
        self.w_early = {0: first_early}
        self.w_late = {}
        self.g_late, self.g_early = {}, {}
        self.r_late, self.r_early = {}, {}

    def _unpack(self, g, names):
        out, r0 = {}, 0
        for k in names:
            nr = math.prod(self.shard[k]) // D
            s = g[:, r0:r0 + nr]
            out[k] = (s.reshape(self.shapes[k]) if k == 'mlp_w1'
                      else _from_slots(s.reshape(NDEV, -1), (1,) + self.shard[k], BIG_AXIS[k])[0])
            r0 += nr
        return out

    def _pack(self, g, names, rows):
        slots = jnp.concatenate([g[k].reshape(NDEV, -1) if k == 'mlp_w1' else _to_slots(g[k][None], BIG_AXIS[k])
                                 for k in names], axis=1)
        return jnp.pad(slots, ((0, 0), (0, rows * D - slots.shape[1]))).astype(BF).reshape(NDEV, rows, D)

    def early(self, l):
        return self._unpack(self.w_early.pop(l), BIG_EARLY)

    def late(self, l):
        return self._unpack(self.w_late.pop(l), BIG_LATE)

    def gather_src(self, l):
        src = [self.mine[l, :LATE_ROWS]]
        if l + 1 < DEPTH:
            src.append(self.mine[l + 1, LATE_ROWS:])
        return tuple(src)

    def gathered(self, l, res):
        self.w_late[l] = res[0]
        if l + 1 < DEPTH:
            self.w_early[l + 1] = res[1]

    def late_grads(self, l, g):
        self.g_late[l] = self._pack(g, BIG_LATE, LATE_ROWS)

    def early_grads(self, l, g):
        self.g_early[l] = self._pack(g, BIG_EARLY, LAYER_ROWS - LATE_ROWS)

    def scatter_src(self, l):
        src = [self.g_late.pop(l)]
        if l + 1 < DEPTH:
            src.append(self.g_early.pop(l + 1))
        return tuple(src)

    def scattered(self, l, res):
        self.r_late[l] = res[0]
        if l + 1 < DEPTH:
            self.r_early[l + 1] = res[1]


def _peer(k):
    x, y, c = lax.axis_index('x'), lax.axis_index('y'), lax.axis_index('c')
    px, py, pc = x ^ ((k >> 2) & 1), y ^ ((k >> 1) & 1), c ^ (k & 1)
    return (px, py, pc), 4 * px + 2 * py + pc


def _copies(kind, src_ref, dst_ref, send_sems, recv_sems, loc_sem):
    _, me = _peer(0)
    src = (lambda p: src_ref.at[p]) if kind == 'scatter' else (lambda p: src_ref)
    local = pltpu.make_async_copy(src(me), dst_ref.at[me], loc_sem)
    sends, recvs = [], []
    for k in range(1, NDEV):
        dev, p = _peer(k)
        for slot, lst in ((me, sends), (p, recvs)):
            lst.append(pltpu.make_async_remote_copy(src_ref=src(p), dst_ref=dst_ref.at[slot], send_sem=send_sems.at[k - 1],
                                                    recv_sem=recv_sems.at[k - 1], device_id=dev,
                                                    device_id_type=pl.DeviceIdType.MESH))
    return local, sends, recvs


def _start_copies(cs):
    local, sends, _ = cs
    local.start()
    for cp in sends:
        cp.start()


def _wait_copies(cs):
    local, sends, recvs = cs
    for cp in sends:
        cp.wait_send()
    for cp in recvs:
        cp.wait_recv()
    local.wait()


_COMM_SCRATCH = (pltpu.SemaphoreType.DMA((NDEV - 1,)), pltpu.SemaphoreType.DMA((NDEV - 1,)), pltpu.SemaphoreType.DMA(()))
_ANY = pl.BlockSpec(memory_space=pl.ANY)


def exchange(scatters, gathers, *, name):
    ins = list(scatters) + list(gathers)
    kinds = ['scatter'] * len(scatters) + ['gather'] * len(gathers)
    n_in = len(ins)
    outs = [_sds(a.shape, a.dtype) for a in scatters] + [_sds((NDEV,) + b.shape, b.dtype) for b in gathers]

    def body(*refs):
        in_refs, out_refs, sems = refs[:n_in], refs[n_in:2 * n_in], refs[2 * n_in:]
        sets = [_copies(kind, in_refs[i], out_refs[i], *sems[3 * i:3 * i + 3]) for i, kind in enumerate(kinds)]
        for cs in sets:
            _start_copies(cs)
        for cs in sets:
            _wait_copies(cs)

    return pl.pallas_call(body, name=name, in_specs=[_ANY] * n_in, out_specs=[_ANY] * n_in, out_shape=outs,
                          scratch_shapes=list(_COMM_SCRATCH * n_in))(*ins)


def adamw(w, m, v, g8, *, name, tr):
    r = w.shape[0]
    c1 = 1.0 / (1.0 - ADAM_B1 ** ADAM_STEP)
    c2 = 1.0 / (1.0 - ADAM_B2 ** ADAM_STEP)

    def body(w_ref, m_ref, v_ref, g_ref, go_ref, d_ref, mo_ref, vo_ref):
        g = g_ref[0].astype(F32)
        for i in range(1, NDEV):
            g = g + g_ref[i].astype(F32)
        m_new = ADAM_B1 * m_ref[...] + (1.0 - ADAM_B1) * g
        v_new = ADAM_B2 * v_ref[...] + (1.0 - ADAM_B2) * (g * g)
        go_ref[...] = g
        mo_ref[...] = m_new
        vo_ref[...] = v_new
        d_ref[...] = -ADAM_LR * ((m_new * c1) / (jnp.sqrt(v_new * c2) + ADAM_EPS) + ADAM_WD * w_ref[...])

    row = pl.BlockSpec((tr, D), lambda i: (i, 0))
    return _call(body, name=name, grid=(r // tr,),
                 in_specs=[row, row, row, pl.BlockSpec((NDEV, tr, D), lambda i: (0, i, 0))],
                 out_specs=[row] * 4, out_shape=[_sds((r, D), F32)] * 4, sem=('parallel',), vmem=VMEM_BIG)(w, m, v, g8)


def _flat_rows(parts, rows):
    flat = jnp.concatenate([p.reshape(-1) for p in parts])
    return jnp.pad(flat, (0, rows * D - flat.shape[0])).reshape(rows, D)


def _unflat(flat2d, shapes):
    flat = flat2d.reshape(-1)
    out, off = [], 0
    for s in shapes:
        n = math.prod(s)
        out.append(flat[off:off + n].reshape(s))
        off += n
    return out


def _to_slots(g, axis):
    l, r, c = g.shape
    if axis == 1:
        return g.reshape(l, NDEV, r // NDEV, c).transpose(1, 0, 2, 3).reshape(NDEV, -1)
    return g.reshape(l, r, NDEV, c // NDEV).transpose(2, 0, 1, 3).reshape(NDEV, -1)


def _from_slots(s, shard_shape, axis):
    l, r, c = shard_shape
    s = s.reshape(NDEV, l, r, c)
    if axis == 1:
        return s.transpose(1, 0, 2, 3).reshape(l, NDEV * r, c)
    return s.transpose(1, 2, 0, 3).reshape(l, r, NDEV * c)


LATE_ROWS = 1280
LAYER_ROWS = 1536
BIG_ROWS = DEPTH * LAYER_ROWS
SMALL_ROWS = 640


def kernel(x, mem, positions, norm_mix, w_in, ssm_lambda_re, ssm_lambda_im, ssm_log_step, ssm_b_re, ssm_b_im, ssm_c_re, ssm_c_im, ssm_d, ssm_w_glu, ssm_b_glu, mla_q_norm, mla_w_uq, mla_kv_norm, mla_w_ukv, mla_q_gain, mla_k_gain, out_norm_ssm, out_norm_mla, w_out, norm_mem_q, norm_mem_kv, mem_w_q, mem_w_kv, mem_q_gain, mem_k_gain, mem_w_o, norm_mlp, mlp_w1, mlp_w2, loss_target, m_norm_mix, m_w_in, m_ssm_lambda_re, m_ssm_lambda_im, m_ssm_log_step, m_ssm_b_re, m_ssm_b_im, m_ssm_c_re, m_ssm_c_im, m_ssm_d, m_ssm_w_glu, m_ssm_b_glu, m_mla_q_norm, m_mla_w_uq, m_mla_kv_norm, m_mla_w_ukv, m_mla_q_gain, m_mla_k_gain, m_out_norm_ssm, m_out_norm_mla, m_w_out, m_norm_mem_q, m_norm_mem_kv, m_mem_w_q, m_mem_w_kv, m_mem_q_gain, m_mem_k_gain, m_mem_w_o, m_norm_mlp, m_mlp_w1, m_mlp_w2, v_norm_mix, v_w_in, v_ssm_lambda_re, v_ssm_lambda_im, v_ssm_log_step, v_ssm_b_re, v_ssm_b_im, v_ssm_c_re, v_ssm_c_im, v_ssm_d, v_ssm_w_glu, v_ssm_b_glu, v_mla_q_norm, v_mla_w_uq, v_mla_kv_norm, v_mla_w_ukv, v_mla_q_gain, v_mla_k_gain, v_out_norm_ssm, v_out_norm_mla, v_w_out, v_norm_mem_q, v_norm_mem_kv, v_mem_w_q, v_mem_w_kv, v_mem_q_gain, v_mem_k_gain, v_mem_w_o, v_norm_mlp, v_mlp_w1, v_mlp_w2):
    wvals = (norm_mix, w_in, ssm_lambda_re, ssm_lambda_im, ssm_log_step, ssm_b_re, ssm_b_im, ssm_c_re, ssm_c_im, ssm_d, ssm_w_glu, ssm_b_glu, mla_q_norm, mla_w_uq, mla_kv_norm, mla_w_ukv, mla_q_gain, mla_k_gain, out_norm_ssm, out_norm_mla, w_out, norm_mem_q, norm_mem_kv, mem_w_q, mem_w_kv, mem_q_gain, mem_k_gain, mem_w_o, norm_mlp, mlp_w1, mlp_w2)
    mvals = (m_norm_mix, m_w_in, m_ssm_lambda_re, m_ssm_lambda_im, m_ssm_log_step, m_ssm_b_re, m_ssm_b_im, m_ssm_c_re, m_ssm_c_im, m_ssm_d, m_ssm_w_glu, m_ssm_b_glu, m_mla_q_norm, m_mla_w_uq, m_mla_kv_norm, m_mla_w_ukv, m_mla_q_gain, m_mla_k_gain, m_out_norm_ssm, m_out_norm_mla, m_w_out, m_norm_mem_q, m_norm_mem_kv, m_mem_w_q, m_mem_w_kv, m_mem_q_gain, m_mem_k_gain, m_mem_w_o, m_norm_mlp, m_mlp_w1, m_mlp_w2)
    vvals = (v_norm_mix, v_w_in, v_ssm_lambda_re, v_ssm_lambda_im, v_ssm_log_step, v_ssm_b_re, v_ssm_b_im, v_ssm_c_re, v_ssm_c_im, v_ssm_d, v_ssm_w_glu, v_ssm_b_glu, v_mla_q_norm, v_mla_w_uq, v_mla_kv_norm, v_mla_w_ukv, v_mla_q_gain, v_mla_k_gain, v_out_norm_ssm, v_out_norm_mla, v_w_out, v_norm_mem_q, v_norm_mem_kv, v_mem_w_q, v_mem_w_kv, v_mem_q_gain, v_mem_k_gain, v_mem_w_o, v_norm_mlp, v_mlp_w1, v_mlp_w2)
    w = dict(zip(WEIGHTS, wvals))
    m = dict(zip(WEIGHTS, mvals))
    v = dict(zip(WEIGHTS, vvals))

    shard_shapes = {k: w[k].shape for k in BIG}
    layer_shapes = [shard_shapes[k][1:] for k in BIG]

    def layer_flat(parts):
        flat = jnp.concatenate([p.reshape(DEPTH, -1) for p in parts], axis=1)
        return jnp.pad(flat, ((0, 0), (0, LAYER_ROWS * D - flat.shape[1]))).reshape(DEPTH, LAYER_ROWS, D)

    mine = layer_flat([w[k].astype(BF) for k in BIG])
    first, = exchange([], [mine[0, LATE_ROWS:]], name='gather_early0')
    plan = ExchangePlan(shard_shapes, mine, first)
    small = {k: w[k] for k in SMALL}
    loss, grad_x, gsmall = local_step(x[0], mem[0], positions[0], loss_target[0], small, plan)
    gs_full = [jnp.stack([gsmall[l][k] for l in range(DEPTH)]) for k in SMALL]
    small_flat = _flat_rows(gs_full, SMALL_ROWS).astype(BF)
    plan.r_early[0], g8_small, losses = exchange([plan.g_early.pop(0)], [small_flat, jnp.full((8, LANES), loss, F32)],
                                                 name='exchange_last')
    loss_all = jnp.sum(losses[:, 0, 0])
    g8_big = jnp.concatenate([r[l] for l in range(DEPTH) for r in (plan.r_late, plan.r_early)], axis=1)

    small_shapes = [w[k].shape for k in SMALL]
    flat_big = lambda d: layer_flat([d[k] for k in BIG]).reshape(BIG_ROWS, D)
    gb, db, mb, vb = adamw(flat_big(w), flat_big(m), flat_big(v), g8_big, name='adamw_big', tr=256)
    gs, ds, ms, vs = adamw(_flat_rows([w[k] for k in SMALL], SMALL_ROWS), _flat_rows([m[k] for k in SMALL], SMALL_ROWS),
                           _flat_rows([v[k] for k in SMALL], SMALL_ROWS), g8_small, name='adamw_small', tr=128)

    def unflat_big(fb):
        fb, out, r0 = fb.reshape(DEPTH, LAYER_ROWS, D), [], 0
        for k, shp in zip(BIG, layer_shapes):
            nr = math.prod(shp) // D
            out.append(fb[:, r0:r0 + nr].reshape(shard_shapes[k]))
            r0 += nr
        return out

    res = {}
    for tag, fb, fs in (('g', gb, gs), ('d', db, ds), ('m', mb, ms), ('v', vb, vs)):
        res[tag] = dict(zip(BIG, unflat_big(fb)))
        res[tag].update(zip(SMALL, _unflat(fs, small_shapes)))
    return (loss_all, grad_x[None], *[res['g'][k] for k in WEIGHTS], *[res['d'][k] for k in WEIGHTS],
            *[res['m'][k] for k in WEIGHTS], *[res['v'][k] for k in WEIGHTS])
```

```python
import functools
import math

import jax
import jax.numpy as jnp
from jax import lax
from jax.experimental import pallas as pl
from jax.experimental.pallas import tpu as pltpu

F32 = jnp.float32
BF = jnp.bfloat16

D = 1024
DEPTH = 4
N_MEM = 256
MEM_HEADS = 4
MEM_HD = 64
SSM_W = 512
SSM_G = 32
SSM_H = 16
SSM_P = 64
MLA_HEADS = 8
QK_NOPE = 64
QK_ROPE = 32
QK_DIM = 96
V_DIM = 64
Q_LORA = 256
KV_LORA = 128
ROPE_THETA = 10000.0
D_FF = 4096
IN_COLS = 928
EPS = 1e-6
NDEV = 8
LANES = 128
SEGS = 32
S5_LW = 256
S5_NHB = (SSM_G * SSM_P) // S5_LW
ADAM_LR = 0.001
ADAM_B1 = 0.9
ADAM_B2 = 0.999
ADAM_EPS = 1e-08
ADAM_WD = 0.01
ADAM_STEP = 10
VMEM_BIG = 56 * 1024 * 1024

NN = (((1,), (0,)), ((), ()))
NT = (((1,), (1,)), ((), ()))
TN = (((0,), (0,)), ((), ()))

BIG_LATE = ('w_out', 'mem_w_q', 'mem_w_kv', 'mem_w_o', 'mlp_w1', 'mlp_w2')
BIG_EARLY = ('w_in', 'ssm_w_glu', 'mla_w_uq', 'mla_w_ukv')
BIG = BIG_LATE + BIG_EARLY
BIG_AXIS = {'w_in': 1, 'ssm_w_glu': 1, 'mla_w_uq': 2, 'mla_w_ukv': 2, 'w_out': 1, 'mem_w_q': 1, 'mem_w_kv': 1,
            'mem_w_o': 2, 'mlp_w1': 2, 'mlp_w2': 1}
SMALL = ('norm_mix', 'ssm_lambda_re', 'ssm_lambda_im', 'ssm_log_step', 'ssm_b_re', 'ssm_b_im', 'ssm_c_re', 'ssm_c_im',
         'ssm_d', 'ssm_b_glu', 'mla_q_norm', 'mla_kv_norm', 'mla_q_gain', 'mla_k_gain', 'out_norm_ssm', 'out_norm_mla',
         'norm_mem_q', 'norm_mem_kv', 'mem_q_gain', 'mem_k_gain', 'norm_mlp')
WEIGHTS = ('norm_mix', 'w_in', 'ssm_lambda_re', 'ssm_lambda_im', 'ssm_log_step', 'ssm_b_re', 'ssm_b_im', 'ssm_c_re',
           'ssm_c_im', 'ssm_d', 'ssm_w_glu', 'ssm_b_glu', 'mla_q_norm', 'mla_w_uq', 'mla_kv_norm', 'mla_w_ukv',
           'mla_q_gain', 'mla_k_gain', 'out_norm_ssm', 'out_norm_mla', 'w_out', 'norm_mem_q', 'norm_mem_kv', 'mem_w_q',
           'mem_w_kv', 'mem_q_gain', 'mem_k_gain', 'mem_w_o', 'norm_mlp', 'mlp_w1', 'mlp_w2')


def _call(body, *, name, out_shape, grid=(), in_specs=None, out_specs=None, scratch=(), sem=None, vmem=None):
    params = {}
    if sem is not None:
        params['dimension_semantics'] = sem
    if vmem is not None:
        params['vmem_limit_bytes'] = vmem
    specs = {} if in_specs is None else dict(grid=grid, in_specs=in_specs, out_specs=out_specs)
    return pl.pallas_call(body, name=name, out_shape=out_shape, scratch_shapes=list(scratch),
                          compiler_params=pltpu.CompilerParams(**params), **specs)


def _sds(shape, dtype):
    return jax.ShapeDtypeStruct(shape, dtype)


def _dot(a, b, dims=NN):
    return lax.dot_general(a.astype(BF), b.astype(BF), dims, preferred_element_type=F32)


def _split(a):
    hi = a.astype(BF)
    return hi, (a - hi.astype(F32)).astype(BF)


def _dot3(a, b, dims=NN):
    ah, al = _split(a)
    bh, bl = _split(b)
    d = lambda p, q: lax.dot_general(p, q, dims, preferred_element_type=F32)
    return d(ah, bh) + (d(ah, bl) + d(al, bh))


_sdot = _dot


def _rms(x, n):
    r = lax.rsqrt(jnp.sum(x * x, axis=-1, keepdims=True) * (1.0 / n) + EPS)
    return x * r, r


def _rms_bwd(xhat, r, dxhat, n):
    return r * (dxhat - xhat * (jnp.sum(dxhat * xhat, axis=-1, keepdims=True) * (1.0 / n)))


def _colsum(a):
    return jnp.sum(a, axis=0, keepdims=True)


def _tile(t, want):
    return min(t, want)


def _bidx(nb):
    return (lambda b: b) if nb > 1 else (lambda b: 0)


def mm(a, b, mode, *, name, out_dtype=F32, tm=1024, tn=1024, slots=0):
    squeeze = a.ndim == 2 and b.ndim == 2
    a = a[None] if a.ndim == 2 else a
    b = b[None] if b.ndim == 2 else b
    nb = max(a.shape[0], b.shape[0])
    ab, bb = _bidx(a.shape[0]), _bidx(b.shape[0])
    if mode in ('nn', 'nt'):
        m, k = a.shape[1:]
        n = b.shape[2] if mode == 'nn' else b.shape[1]
        tm, tn = _tile(m, tm), _tile(n, tn)
        dims = NN if mode == 'nn' else NT

        def body(a_ref, b_ref, o_ref):
            o_ref[...] = _dot(a_ref[...], b_ref[...], dims).astype(o_ref.dtype)

        bspec = (pl.BlockSpec((None, k, tn), lambda bi, i, j: (bb(bi), 0, j)) if mode == 'nn'
                 else pl.BlockSpec((None, tn, k), lambda bi, i, j: (bb(bi), j, 0)))
        out = _call(body, name=name, grid=(nb, m // tm, n // tn),
                    in_specs=[pl.BlockSpec((None, tm, k), lambda bi, i, j: (ab(bi), i, 0)), bspec],
                    out_specs=pl.BlockSpec((None, tm, tn), lambda bi, i, j: (bi, i, j)),
                    out_shape=_sds((nb, m, n), out_dtype), sem=('parallel', 'parallel', 'parallel'), vmem=VMEM_BIG)(a, b)
    else:
        k, m = a.shape[1:]
        n = b.shape[2]
        tm, tn, tk = _tile(m, 1024), _tile(n, 1024), _tile(k, 1024)
        per = 1
        if slots:
            ts = n // slots
            per = tn // ts
            out_spec, out_shape = pl.BlockSpec((per, tm, ts), lambda bi, i, j, kk: (j, i, 0)), _sds((slots, m, ts), F32)
        else:
            out_spec, out_shape = pl.BlockSpec((None, tm, tn), lambda bi, i, j, kk: (bi, i, j)), _sds((nb, m, n), F32)

        def body(a_ref, b_ref, o_ref):
            @pl.when(pl.program_id(3) == 0)
            def _():
                o_ref[...] = jnp.zeros_like(o_ref)

            res = _dot(a_ref[...], b_ref[...], TN)
            if slots:
                for s in range(per):
                    o_ref[s] += res[:, s * ts:(s + 1) * ts]
            else:
                o_ref[...] += res

        out = _call(body, name=name, grid=(nb, m // tm, n // tn, k // tk),
                    in_specs=[pl.BlockSpec((None, tk, tm), lambda bi, i, j, kk: (ab(bi), kk, i)),
                              pl.BlockSpec((None, tk, tn), lambda bi, i, j, kk: (bb(bi), kk, j))],
                    out_specs=out_spec, out_shape=out_shape,
                    sem=('parallel', 'parallel', 'parallel', 'arbitrary'), vmem=VMEM_BIG)(a, b)
    return out[0] if squeeze and not slots else out


def rmsnorm_fwd(x, g, *, name, tq=512):
    t, d = x.shape
    tq = _tile(t, tq)

    def body(x_ref, g_ref, o_ref):
        xh, _ = _rms(x_ref[...], d)
        o_ref[...] = (xh * g_ref[...]).astype(o_ref.dtype)

    return _call(body, name=name, grid=(t // tq,),
                 in_specs=[pl.BlockSpec((tq, d), lambda i: (i, 0)), pl.BlockSpec((1, d), lambda i: (0, 0))],
                 out_specs=pl.BlockSpec((tq, d), lambda i: (i, 0)), out_shape=_sds((t, d), BF), sem=('parallel',))(x, g)


def rmsnorm_bwd(x, g, dh, dres, *, name, col=0, tq=512):
    t, d = x.shape
    tq = _tile(t, tq)
    has_res = dres is not None

    def body(*refs):
        if has_res:
            x_ref, g_ref, dh_ref, dres_ref, dx_ref, dxb_ref, dg_ref = refs
        else:
            x_ref, g_ref, dh_ref, dx_ref, dxb_ref, dg_ref = refs
        xh, r = _rms(x_ref[...], d)
        dh_ = dh_ref[...].astype(F32)
        dx = _rms_bwd(xh, r, dh_ * g_ref[...], d)
        if has_res:
            dx = dx + dres_ref[...]
        dx_ref[...] = dx
        dxb_ref[...] = dx.astype(BF)

        @pl.when(pl.program_id(0) == 0)
        def _():
            dg_ref[...] = jnp.zeros_like(dg_ref)

        dg_ref[...] += _colsum(dh_ * xh)

    in_specs = [pl.BlockSpec((tq, d), lambda i: (i, 0)), pl.BlockSpec((1, d), lambda i: (0, 0)),
                pl.BlockSpec((tq, d), lambda i: (i, col))]
    args = [x, g, dh]
    if has_res:
        in_specs.append(pl.BlockSpec((tq, d), lambda i: (i, 0)))
        args.append(dres)
    row = pl.BlockSpec((tq, d), lambda i: (i, 0))
    return _call(body, name=name, grid=(t // tq,), in_specs=in_specs,
                 out_specs=[row, row, pl.BlockSpec((1, d), lambda i: (0, 0))],
                 out_shape=[_sds((t, d), F32), _sds((t, d), BF), _sds((1, d), F32)], sem=('arbitrary',))(*args)


def _cmul(ar, ai, xr, xi):
    return ar * xr - ai * xi, ar * xi + ai * xr


def _seg_carries(er, ei, pr, pi, reverse):
    lw = er.shape[1]
    zero = jnp.zeros((1, lw), F32)
    order = range(SEGS - 1, -1, -1) if reverse else range(SEGS)
    cin_r, cin_i = [None] * SEGS, [None] * SEGS
    tr, ti = zero, zero
    for j in order:
        cin_r[j], cin_i[j] = tr, ti
        mr, mi = _cmul(pr, pi, tr, ti)
        tr, ti = er[j:j + 1, :] + mr, ei[j:j + 1, :] + mi
    return jnp.concatenate(cin_r, axis=0), jnp.concatenate(cin_i, axis=0)


def _s5_chunk(t):
    return _tile(t, 1024)


def s5_fwd(u_p, prm, *, name):
    t = u_p.shape[0]
    ch = _s5_chunk(t)
    nch, steps = t // ch, ch // SEGS
    lw = S5_LW

    def body(u_ref, ar_ref, ai_ref, pr_ref, pi_ref, bre_ref, bim_ref, cre_ref, cim_ref, d_ref, y_ref, bur, bui):
        hb = pl.program_id(0)
        ar = jnp.broadcast_to(ar_ref[0], (SEGS, lw))
        ai = jnp.broadcast_to(ai_ref[0], (SEGS, lw))

        def rows_of(c):
            return pl.ds(pl.multiple_of(c * ch, ch), ch)

        @pl.loop(0, nch)
        def _(c):
            u = u_ref[rows_of(c), :]
            bur[rows_of(c), :] = _sdot(u, bre_ref[0])
            bui[rows_of(c), :] = _sdot(u, bim_ref[0])

        def scan(carry, store):
            def step(i, s):
                r0 = pl.multiple_of(i * SEGS, SEGS)
                mr, mi = _cmul(ar, ai, s[0], s[1])
                nr, ni = mr + bur[pl.ds(r0, SEGS), :], mi + bui[pl.ds(r0, SEGS), :]
                if store:
                    bur[pl.ds(r0, SEGS), :] = nr
                    bui[pl.ds(r0, SEGS), :] = ni
                return nr, ni

            return lax.fori_loop(0, t // SEGS, step, carry, unroll=8)

        zero = jnp.zeros((SEGS, lw), F32)
        er, ei = scan((zero, zero), False)
        scan(_seg_carries(er, ei, pr_ref[0], pi_ref[0], False), True)

        @pl.loop(0, nch)
        def _(c):
            rows = rows_of(c)
            y = _sdot(bur[rows, :], cre_ref[0]) - _sdot(bui[rows, :], cim_ref[0])

            @pl.when(hb % 2 == 0)
            def _():
                y_ref[rows, :] = y + d_ref[...] * u_ref[rows, :]

            @pl.when(hb % 2 == 1)
            def _():
                y_ref[rows, :] += y

    vec = pl.BlockSpec((1, 1, lw), lambda h: (h, 0, 0))
    return _call(
        body, name=name, grid=(S5_NHB,),
        in_specs=[pl.BlockSpec((t, LANES), lambda h: (0, h // 2)), vec, vec, vec, vec,
                  pl.BlockSpec((1, LANES, lw), lambda h: (h, 0, 0)), pl.BlockSpec((1, LANES, lw), lambda h: (h, 0, 0)),
                  pl.BlockSpec((1, lw, LANES), lambda h: (h, 0, 0)), pl.BlockSpec((1, lw, LANES), lambda h: (h, 0, 0)),
                  pl.BlockSpec((1, LANES), lambda h: (0, h // 2))],
        out_specs=pl.BlockSpec((t, LANES), lambda h: (0, h // 2)), out_shape=_sds((t, SSM_W), F32),
        scratch=[pltpu.VMEM((t, lw), F32)] * 2, sem=('arbitrary',), vmem=VMEM_BIG,
    )(u_p, prm['ar'], prm['ai'], prm['pr'], prm['pi'], prm['bre'], prm['bim'], prm['cre'], prm['cim'], prm['d'])


def s5_bwd(u_p, dy_p, prm, *, name):
    t = u_p.shape[0]
    ch = _s5_chunk(t)
    nch, steps = t // ch, ch // SEGS
    lw = S5_LW

    def body(u_ref, dy_ref, ar_ref, ai_ref, pr_ref, pi_ref, bre_ref, bim_ref, cre_ref, cim_ref, d_ref,
             du_ref, dar_ref, dai_ref, dbre_ref, dbim_ref, dcre_ref, dcim_ref, dd_ref, bur, bui, sr, si, du_acc):
        hb = pl.program_id(0)
        ar = jnp.broadcast_to(ar_ref[0], (SEGS, lw))
        ai = jnp.broadcast_to(ai_ref[0], (SEGS, lw))
        zero = jnp.zeros((SEGS, lw), F32)

        def rows_of(c):
            return pl.ds(pl.multiple_of(c * ch, ch), ch)

        nsteps = t // SEGS

        @pl.loop(0, nch)
        def _(c):
            u = u_ref[rows_of(c), :]
            bur[rows_of(c), :] = _sdot(u, bre_ref[0])
            bui[rows_of(c), :] = _sdot(u, bim_ref[0])

        def fwd_scan(carry, store):
            def step(i, s):
                r0 = pl.multiple_of(i * SEGS, SEGS)
                mr, mi = _cmul(ar, ai, s[0], s[1])
                nr, ni = mr + bur[pl.ds(r0, SEGS), :], mi + bui[pl.ds(r0, SEGS), :]
                if store:
                    w0 = pl.multiple_of(i * SEGS + SEGS, SEGS)
                    sr[pl.ds(w0, SEGS), :] = nr
                    si[pl.ds(w0, SEGS), :] = ni
                return nr, ni

            return lax.fori_loop(0, nsteps, step, carry, unroll=8)

        er, ei = fwd_scan((zero, zero), False)
        cin_r, cin_i = _seg_carries(er, ei, pr_ref[0], pi_ref[0], False)
        sr[pl.ds(0, SEGS), :] = cin_r
        si[pl.ds(0, SEGS), :] = cin_i
        fwd_scan((cin_r, cin_i), True)

        @pl.loop(0, nch)
        def _(c):
            dy = dy_ref[rows_of(c), :]
            bur[rows_of(c), :] = _sdot(dy, cre_ref[0], NT)
            bui[rows_of(c), :] = -_sdot(dy, cim_ref[0], NT)

        def rev_local(ii, lam):
            r0 = pl.multiple_of((nsteps - 1 - ii) * SEGS, SEGS)
            mr, mi = _cmul(ar, -ai, lam[0], lam[1])
            return mr + bur[pl.ds(r0, SEGS), :], mi + bui[pl.ds(r0, SEGS), :]

        lr0, li0 = lax.fori_loop(0, nsteps, rev_local, (zero, zero), unroll=8)
        rin = _seg_carries(lr0, li0, pr_ref[0], -pi_ref[0], True)

        def rev_step(ii, st):
            lam_r, lam_i, acc_r, acc_i = st
            r0 = pl.multiple_of((nsteps - 1 - ii) * SEGS, SEGS)
            mr, mi = _cmul(ar, -ai, lam_r, lam_i)
            nr, ni = mr + bur[pl.ds(r0, SEGS), :], mi + bui[pl.ds(r0, SEGS), :]
            bur[pl.ds(r0, SEGS), :] = nr
            bui[pl.ds(r0, SEGS), :] = ni
            pr_, pi_ = sr[pl.ds(r0, SEGS), :], si[pl.ds(r0, SEGS), :]
            return nr, ni, acc_r + (nr * pr_ + ni * pi_), acc_i + (ni * pr_ - nr * pi_)

        _, _, acc_r, acc_i = lax.fori_loop(0, nsteps, rev_step, (rin[0], rin[1], zero, zero), unroll=8)
        dar_ref[0] = _colsum(acc_r)
        dai_ref[0] = _colsum(acc_i)

        dbre_ref[...] = jnp.zeros_like(dbre_ref)
        dbim_ref[...] = jnp.zeros_like(dbim_ref)
        dcre_ref[...] = jnp.zeros_like(dcre_ref)
        dcim_ref[...] = jnp.zeros_like(dcim_ref)

        @pl.loop(0, nch)
        def _(c):
            rows = rows_of(c)
            u = u_ref[rows, :]
            dy = dy_ref[rows, :]
            lam_r, lam_i = bur[rows, :], bui[rows, :]
            du = _sdot(lam_r, bre_ref[0], NT) + _sdot(lam_i, bim_ref[0], NT)

            @pl.when(hb % 2 == 0)
            def _():
                du_acc[rows, :] = du + d_ref[...] * dy

            @pl.when(hb % 2 == 1)
            def _():
                du_ref[rows, :] = (du_acc[rows, :] + du).astype(BF)

            dbre_ref[0] += _sdot(u, lam_r, TN)
            dbim_ref[0] += _sdot(u, lam_i, TN)
            srows = pl.ds(pl.multiple_of(c * ch + SEGS, SEGS), ch)
            dcre_ref[0] += _sdot(sr[srows, :], dy, TN)
            dcim_ref[0] -= _sdot(si[srows, :], dy, TN)

        @pl.when(hb % 2 == 0)
        def _():
            dd_ref[...] = _colsum(dy_ref[...] * u_ref[...])

    vec = pl.BlockSpec((1, 1, lw), lambda h: (h, 0, 0))
    bsp = pl.BlockSpec((1, LANES, lw), lambda h: (h, 0, 0))
    csp = pl.BlockSpec((1, lw, LANES), lambda h: (h, 0, 0))
    act = pl.BlockSpec((t, LANES), lambda h: (0, h // 2))
    dsp = pl.BlockSpec((1, LANES), lambda h: (0, h // 2))
    return _call(
        body, name=name, grid=(S5_NHB,),
        in_specs=[act, act, vec, vec, vec, vec, bsp, bsp, csp, csp, dsp],
        out_specs=[act, vec, vec, bsp, bsp, csp, csp, dsp],
        out_shape=[_sds((t, SSM_W), BF), _sds((S5_NHB, 1, lw), F32), _sds((S5_NHB, 1, lw), F32),
                   _sds((S5_NHB, LANES, lw), F32), _sds((S5_NHB, LANES, lw), F32),
                   _sds((S5_NHB, lw, LANES), F32), _sds((S5_NHB, lw, LANES), F32), _sds((1, SSM_W), F32)],
        scratch=[pltpu.VMEM((t, lw), F32), pltpu.VMEM((t, lw), F32),
                 pltpu.VMEM((t + SEGS, lw), F32), pltpu.VMEM((t + SEGS, lw), F32), pltpu.VMEM((t, LANES), F32)],
        sem=('arbitrary',), vmem=VMEM_BIG,
    )(u_p, dy_p, prm['ar'], prm['ai'], prm['pr'], prm['pi'], prm['bre'], prm['bim'], prm['cre'], prm['cim'], prm['d'])


def s5_prep(t, lam_re, lam_im, log_step, b_re, b_im, c_re, c_im):
    step = jnp.exp(log_step)[:, None]
    mag = jnp.exp(lam_re * step)
    ar, ai = mag * jnp.cos(lam_im * step), mag * jnp.sin(lam_im * step)
    den = lam_re * lam_re + lam_im * lam_im
    nr, ni = ar - 1.0, ai
    fr, fi = (nr * lam_re + ni * lam_im) / den, (ni * lam_re - nr * lam_im) / den
    bbr = fr[..., None] * b_re - fi[..., None] * b_im
    bbi = fr[..., None] * b_im + fi[..., None] * b_re
    gl = S5_LW // SSM_P
    eye = jnp.eye(gl, dtype=F32)
    half = (jnp.arange(S5_NHB) % 2)[:, None, None]

    def bmat(bb):
        x = bb.transpose(0, 2, 1).reshape(S5_NHB, gl, SSM_H, SSM_P)
        x = jnp.einsum('bghp,gk->bghkp', x, eye).reshape(S5_NHB, gl * SSM_H, S5_LW)
        z = jnp.zeros_like(x)
        return jnp.where(half == 0, jnp.concatenate([x, z], axis=1), jnp.concatenate([z, x], axis=1))

    def cmat(cc):
        x = cc.transpose(0, 2, 1).reshape(S5_NHB, gl, SSM_P, SSM_H)
        x = jnp.einsum('bgph,gk->bgpkh', x, eye).reshape(S5_NHB, S5_LW, gl * SSM_H)
        z = jnp.zeros_like(x)
        return jnp.where(half == 0, jnp.concatenate([x, z], axis=2), jnp.concatenate([z, x], axis=2))

    vec = lambda a: a.reshape(S5_NHB, 1, S5_LW)
    ni_steps = float(t // SEGS)
    pmag = jnp.exp(lam_re * step * ni_steps)
    pr, pi = pmag * jnp.cos(lam_im * step * ni_steps), pmag * jnp.sin(lam_im * step * ni_steps)
    return dict(ar=vec(ar), ai=vec(ai), bre=bmat(bbr), bim=bmat(bbi), cre=cmat(c_re), cim=cmat(c_im),
                pr=lax.stop_gradient(vec(pr)), pi=lax.stop_gradient(vec(pi)))


def _gelu(x):
    c = math.sqrt(2.0 / math.pi)
    return 0.5 * x * (1.0 + jnp.tanh(c * (x + 0.044715 * (x * x * x))))


def _gelu_grad(x):
    c = math.sqrt(2.0 / math.pi)
    th = jnp.tanh(c * (x + 0.044715 * (x * x * x)))
    return 0.5 * (1.0 + th) + 0.5 * x * (1.0 - th * th) * (c * (1.0 + 3.0 * 0.044715 * (x * x)))


def glu_fwd(ypre, w_glu, b_glu, *, name, tq=1024):
    t = ypre.shape[0]
    tq = _tile(t, tq)

    def body(y_ref, w_ref, b_ref, o_ref):
        yg = _gelu(y_ref[...])
        z = _dot(yg, w_ref[...]) + b_ref[...]
        o_ref[...] = yg * jax.nn.sigmoid(z)

    return _call(body, name=name, grid=(t // tq,),
                 in_specs=[pl.BlockSpec((tq, SSM_W), lambda i: (i, 0)), pl.BlockSpec((SSM_W, SSM_W), lambda i: (0, 0)),
                           pl.BlockSpec((1, SSM_W), lambda i: (0, 0))],
                 out_specs=pl.BlockSpec((tq, SSM_W), lambda i: (i, 0)), out_shape=_sds((t, SSM_W), F32),
                 sem=('parallel',))(ypre, w_glu, b_glu)


def glu_bwd(ypre, dy, w_glu, b_glu, *, name, tq=1024):
    t = ypre.shape[0]
    tq = _tile(t, tq)

    def body(y_ref, dy_ref, w_ref, b_ref, dyp_ref, yg_ref, dz_ref, db_ref):
        ypre_ = y_ref[...]
        yg = _gelu(ypre_)
        sig = jax.nn.sigmoid(_dot(yg, w_ref[...]) + b_ref[...])
        dy_ = dy_ref[...]
        dz = dy_ * yg * sig * (1.0 - sig)
        dyg = dy_ * sig + _dot(dz, w_ref[...], NT)
        dyp_ref[...] = dyg * _gelu_grad(ypre_)
        yg_ref[...] = yg.astype(BF)
        dz_ref[...] = dz.astype(BF)

        @pl.when(pl.program_id(0) == 0)
        def _():
            db_ref[...] = jnp.zeros_like(db_ref)

        db_ref[...] += _colsum(dz)

    row = pl.BlockSpec((tq, SSM_W), lambda i: (i, 0))
    vec = pl.BlockSpec((1, SSM_W), lambda i: (0, 0))
    return _call(body, name=name, grid=(t // tq,),
                 in_specs=[row, row, pl.BlockSpec((SSM_W, SSM_W), lambda i: (0, 0)), vec],
                 out_specs=[row, row, row, vec],
                 out_shape=[_sds((t, SSM_W), F32), _sds((t, SSM_W), BF), _sds((t, SSM_W), BF), _sds((1, SSM_W), F32)],
                 sem=('arbitrary',))(ypre, dy, w_glu, b_glu)


def _rope(x, cos, sa, sb):
    return x * cos + pltpu.roll(x, 16, 1) * sa + pltpu.roll(x, 112, 1) * sb


def _rope_t(d, cos, sa, sb):
    return d * cos + pltpu.roll(d * sa, 112, 1) + pltpu.roll(d * sb, 16, 1)


def rope_tables(positions):
    half = QK_ROPE // 2
    inv_freq = ROPE_THETA ** (-jnp.arange(half, dtype=F32) / half)
    ang = positions.astype(F32)[:, None] * inv_freq
    cos, sin = jnp.cos(ang), jnp.sin(ang)
    t = positions.shape[0]
    one, zero = jnp.ones((t, QK_NOPE), F32), jnp.zeros((t, QK_NOPE), F32)
    pad1, pad0 = jnp.ones((t, 32), F32), jnp.zeros((t, 32), F32)
    z16 = jnp.zeros((t, half), F32)
    return (jnp.concatenate([one, cos, cos, pad1], axis=1), jnp.concatenate([zero, z16, sin, pad0], axis=1),
            jnp.concatenate([zero, -sin, z16, pad0], axis=1))


def mla_prep_fwd(proj, tabs, w, *, name):
    t = proj.shape[0]
    tq = _tile(t, ATT_BLK)

    def body(cq_ref, ckv_ref, kr_ref, cos_ref, sa_ref, sb_ref, qn_ref, kvn_ref, wq_ref, wk_ref, wv_ref, qg_ref, kg_ref,
             q_ref, qt_ref, k_ref, kt_ref, v_ref):
        cqn = (_rms(cq_ref[...], Q_LORA)[0] * qn_ref[...]).astype(BF)
        ckvn = (_rms(ckv_ref[...], KV_LORA)[0] * kvn_ref[...]).astype(BF)
        cos, sa, sb = cos_ref[...], sa_ref[...], sb_ref[...]
        kr = kr_ref[...]
        for h in range(MLA_HEADS):
            q = _rms(_dot(cqn, wq_ref[h]), QK_DIM)[0] * qg_ref[...]
            q = _rope(q, cos, sa, sb) * ATT_SCALE
            q_ref[h] = q.astype(BF)
            qt_ref[h, 0] = q.T.astype(BF)
            k = _rms(_dot(ckvn, wk_ref[h]) + kr, QK_DIM)[0] * kg_ref[...]
            k = _rope(k, cos, sa, sb)
            k_ref[h] = k.astype(BF)
            kt_ref[h, 0] = k.T.astype(BF)
            v_ref[h] = _dot(ckvn, wv_ref[h]).astype(BF)

    tab = pl.BlockSpec((tq, LANES), lambda i: (i, 0))
    full = lambda shape: pl.BlockSpec(shape, lambda i: (0,) * len(shape))
    hout = pl.BlockSpec((MLA_HEADS, tq, LANES), lambda i: (0, i, 0))
    tout = pl.BlockSpec((MLA_HEADS, 1, LANES, tq), lambda i: (0, i, 0, 0))
    hshape = _sds((MLA_HEADS, t, LANES), BF)
    tshape = _sds((MLA_HEADS, t // tq, LANES, tq), BF)
    return _call(
        body, name=name, grid=(t // tq,),
        in_specs=[pl.BlockSpec((tq, Q_LORA), lambda i: (i, 2)), pl.BlockSpec((tq, LANES), lambda i: (i, 6)),
                  pl.BlockSpec((tq, LANES), lambda i: (i, 7)), tab, tab, tab,
                  full((1, Q_LORA)), full((1, KV_LORA)), full((MLA_HEADS, Q_LORA, LANES)),
                  full((MLA_HEADS, KV_LORA, LANES)), full((MLA_HEADS, KV_LORA, LANES)), full((1, LANES)), full((1, LANES))],
        out_specs=[hout, tout, hout, tout, hout], out_shape=[hshape, tshape, hshape, tshape, hshape], sem=('parallel',),
    )(proj, proj, proj, *tabs, w['q_norm'], w['kv_norm'], w['wq'], w['wk'], w['wv'], w['q_gain'], w['k_gain'])


def mla_prep_bwd(proj, tabs, w, dq, dk, dv, *, name):
    t = proj.shape[0]
    tq = _tile(t, ATT_BLK)

    def body(cq_ref, ckv_ref, kr_ref, cos_ref, sa_ref, sb_ref, qn_ref, kvn_ref, wq_ref, wk_ref, wv_ref, qg_ref, kg_ref,
             dq_ref, dk_ref, dv_ref,
             dpm_ref, cqn_ref, ckvn_ref, dqr_ref, dkraw_ref, dvb_ref, dqn_ref, dkvn_ref, dqg_ref, dkg_ref):
        cq_h, cq_r = _rms(cq_ref[...], Q_LORA)
        ckv_h, ckv_r = _rms(ckv_ref[...], KV_LORA)
        cqn = (cq_h * qn_ref[...]).astype(BF)
        ckvn = (ckv_h * kvn_ref[...]).astype(BF)
        cqn_ref[...] = cqn
        ckvn_ref[...] = ckvn
        cos, sa, sb = cos_ref[...], sa_ref[...], sb_ref[...]
        kr = kr_ref[...]
        dcqn = jnp.zeros((tq, Q_LORA), F32)
        dckvn = jnp.zeros((tq, KV_LORA), F32)
        dkrope = jnp.zeros((tq, LANES), F32)
        dqg = jnp.zeros((1, LANES), F32)
        dkg = jnp.zeros((1, LANES), F32)
        for h in range(MLA_HEADS):
            qh, qr = _rms(_dot(cqn, wq_ref[h]), QK_DIM)
            dqo = _rope_t(dq_ref[h, 0].T * ATT_SCALE, cos, sa, sb)
            dqg = dqg + _colsum(dqo * qh)
            dqraw = _rms_bwd(qh, qr, dqo * qg_ref[...], QK_DIM).astype(BF)
            dqr_ref[:, h * LANES:(h + 1) * LANES] = dqraw
            dcqn = dcqn + _dot(dqraw, wq_ref[h], NT)
            kh, krs = _rms(_dot(ckvn, wk_ref[h]) + kr, QK_DIM)
            dko = _rope_t(dk_ref[h], cos, sa, sb)
            dkg = dkg + _colsum(dko * kh)
            dkraw = _rms_bwd(kh, krs, dko * kg_ref[...], QK_DIM)
            dkrope = dkrope + dkraw
            dkraw = dkraw.astype(BF)
            dkraw_ref[:, h * LANES:(h + 1) * LANES] = dkraw
            dvb = dv_ref[h].astype(BF)
            dvb_ref[:, h * LANES:(h + 1) * LANES] = dvb
            dckvn = dckvn + _dot(dkraw, wk_ref[h], NT) + _dot(dvb, wv_ref[h], NT)
        dpm_ref[:, 0:Q_LORA] = _rms_bwd(cq_h, cq_r, dcqn * qn_ref[...], Q_LORA).astype(BF)
        dpm_ref[:, Q_LORA:Q_LORA + KV_LORA] = _rms_bwd(ckv_h, ckv_r, dckvn * kvn_ref[...], KV_LORA).astype(BF)
        dpm_ref[:, Q_LORA + KV_LORA:512] = dkrope.astype(BF)

        @pl.when(pl.program_id(0) == 0)
        def _():
            dqn_ref[...] = jnp.zeros_like(dqn_ref)
            dkvn_ref[...] = jnp.zeros_like(dkvn_ref)
            dqg_ref[...] = jnp.zeros_like(dqg_ref)
            dkg_ref[...] = jnp.zeros_like(dkg_ref)

        dqn_ref[...] += _colsum(dcqn * cq_h)
        dkvn_ref[...] += _colsum(dckvn * ckv_h)
        dqg_ref[...] += dqg
        dkg_ref[...] += dkg

    tab = pl.BlockSpec((tq, LANES), lambda i: (i, 0))
    full = lambda shape: pl.BlockSpec(shape, lambda i: (0,) * len(shape))
    hblk = pl.BlockSpec((MLA_HEADS, tq, LANES), lambda i: (0, i, 0))
    wide = pl.BlockSpec((tq, MLA_HEADS * LANES), lambda i: (i, 0))
    return _call(
        body, name=name, grid=(t // tq,),
        in_specs=[pl.BlockSpec((tq, Q_LORA), lambda i: (i, 2)), pl.BlockSpec((tq, LANES), lambda i: (i, 6)),
                  pl.BlockSpec((tq, LANES), lambda i: (i, 7)), tab, tab, tab,
                  full((1, Q_LORA)), full((1, KV_LORA)), full((MLA_HEADS, Q_LORA, LANES)),
                  full((MLA_HEADS, KV_LORA, LANES)), full((MLA_HEADS, KV_LORA, LANES)), full((1, LANES)), full((1, LANES)),
                  pl.BlockSpec((MLA_HEADS, 1, LANES, tq), lambda i: (0, i, 0, 0)), hblk, hblk],
        out_specs=[pl.BlockSpec((tq, 512), lambda i: (i, 0)),
                   pl.BlockSpec((tq, Q_LORA), lambda i: (i, 0)), pl.BlockSpec((tq, KV_LORA), lambda i: (i, 0)),
                   wide, wide, wide, full((1, Q_LORA)), full((1, KV_LORA)), full((1, LANES)), full((1, LANES))],
        out_shape=[_sds((t, 512), BF), _sds((t, Q_LORA), BF), _sds((t, KV_LORA), BF),
                   _sds((t, MLA_HEADS * LANES), BF), _sds((t, MLA_HEADS * LANES), BF), _sds((t, MLA_HEADS * LANES), BF),
                   _sds((1, Q_LORA), F32), _sds((1, KV_LORA), F32), _sds((1, LANES), F32), _sds((1, LANES), F32)],
        sem=('arbitrary',),
    )(proj, proj, proj, *tabs, w['q_norm'], w['kv_norm'], w['wq'], w['wk'], w['wv'], w['q_gain'], w['k_gain'], dq, dk, dv)


ATT_BLK = 256
ATT_SCALE = 1.0 / math.sqrt(QK_DIM)


def _overlapped(grid, make_copies):
    ids = [pl.program_id(a) for a in range(len(grid))]
    first = functools.reduce(jnp.logical_and, [i == 0 for i in ids])
    last = functools.reduce(jnp.logical_and, [i == n - 1 for i, n in zip(ids, grid)])

    @pl.when(first)
    def _():
        for cs in make_copies():
            _start_copies(cs)

    @pl.when(last)
    def _():
        for cs in make_copies():
            _wait_copies(cs)


def flash_fwd(q, kt, v, *, name, gather=()):
    t = q.shape[1]
    blk = _tile(t, ATT_BLK)
    grid = (MLA_HEADS // 2, t // blk)

    def body(q_ref, kt_ref, v_ref, *rest):
        nc = len(gather)
        srcs, (o_ref, lse_ref), dsts, sems = rest[:nc], rest[nc:nc + 2], rest[nc + 2:2 * nc + 2], rest[2 * nc + 2:]
        if nc:
            _overlapped(grid, lambda: [_copies('gather', srcs[i], dsts[i], *sems[3 * i:3 * i + 3]) for i in range(nc)])
        qi = pl.program_id(1)
        row = lax.broadcasted_iota(jnp.int32, (blk, blk), 0)
        col = lax.broadcasted_iota(jnp.int32, (blk, blk), 1)

        def block(j, carry, masked):
            out = []
            for hh in range(2):
                m, l, acc = carry[hh]
                s = _dot(q_ref[hh], kt_ref[hh, j])
                if masked:
                    s = jnp.where(col <= row, s, -jnp.inf)
                m2 = jnp.maximum(m, jnp.max(s, axis=-1, keepdims=True))
                p = jnp.exp(s - m2)
                alpha = jnp.exp(m - m2)
                rows = pl.ds(pl.multiple_of(j * blk, blk), blk)
                out.append((m2, alpha * l + jnp.sum(p, axis=-1, keepdims=True), alpha * acc + _dot(p, v_ref[hh, rows, :])))
            return tuple(out)

        init = (jnp.full((blk, 1), -jnp.inf, F32), jnp.zeros((blk, 1), F32), jnp.zeros((blk, LANES), F32))
        carry = lax.fori_loop(0, qi, lambda j, c: block(j, c, False), (init, init))
        carry = block(qi, carry, True)
        o_acc = jnp.zeros((blk, LANES), F32)
        for hh in range(2):
            m, l, acc = carry[hh]
            o_acc = o_acc + acc / l
            lse_ref[hh, 0] = jnp.broadcast_to(m + jnp.log(l), (blk, LANES)).T[0:1, :]
        o_ref[...] = o_acc

    in_specs = [pl.BlockSpec((2, blk, LANES), lambda p, i: (p, i, 0)),
                pl.BlockSpec((2, t // blk, LANES, blk), lambda p, i: (p, 0, 0, 0)),
                pl.BlockSpec((2, t, LANES), lambda p, i: (p, 0, 0))]
    out_specs = [pl.BlockSpec((blk, LANES), lambda p, i: (i, p)), pl.BlockSpec((2, 1, 1, blk), lambda p, i: (p, i, 0, 0))]
    out_shape = [_sds((t, 512), F32), _sds((MLA_HEADS, t // blk, 1, blk), F32)]
    nc = len(gather)
    return _call(body, name=name, grid=grid, in_specs=in_specs + [_ANY] * nc, out_specs=out_specs + [_ANY] * nc,
                 out_shape=out_shape + [_sds((NDEV,) + g.shape, g.dtype) for g in gather], scratch=_COMM_SCRATCH * nc,
                 sem=('arbitrary', 'arbitrary') if nc else ('parallel', 'parallel'))(q, kt, v, *gather)


def mla_out_bwd(o, dyn, g, *, name):
    t = o.shape[0]
    blk = _tile(t, ATT_BLK)

    def body(o_ref, dh_ref, g_ref, do_ref, dot_ref, delta_ref, dg_ref):
        ov = o_ref[...]
        oh, r = _rms(ov, 512)
        dh = dh_ref[...]
        do = _rms_bwd(oh, r, dh * g_ref[...], 512)
        do_ref[...] = do.astype(BF)
        dd = do * ov
        for pb in range(MLA_HEADS // 2):
            cols = slice(pb * LANES, (pb + 1) * LANES)
            dot_ref[pb, 0] = do[:, cols].T.astype(BF)
            ddt = dd[:, cols].T
            delta_ref[2 * pb, 0] = jnp.sum(ddt[0:V_DIM, :], axis=0, keepdims=True)
            delta_ref[2 * pb + 1, 0] = jnp.sum(ddt[V_DIM:LANES, :], axis=0, keepdims=True)

        @pl.when(pl.program_id(0) == 0)
        def _():
            dg_ref[...] = jnp.zeros_like(dg_ref)

        dg_ref[...] += _colsum(dh * oh)

    return _call(
        body, name=name, grid=(t // blk,),
        in_specs=[pl.BlockSpec((blk, 512), lambda i: (i, 0)), pl.BlockSpec((blk, 512), lambda i: (i, 1)),
                  pl.BlockSpec((1, 512), lambda i: (0, 0))],
        out_specs=[pl.BlockSpec((blk, 512), lambda i: (i, 0)), pl.BlockSpec((MLA_HEADS // 2, 1, LANES, blk), lambda i: (0, i, 0, 0)),
                   pl.BlockSpec((MLA_HEADS, 1, 1, blk), lambda i: (0, i, 0, 0)), pl.BlockSpec((1, 512), lambda i: (0, 0))],
        out_shape=[_sds((t, 512), BF), _sds((MLA_HEADS // 2, t // blk, LANES, blk), BF),
                   _sds((MLA_HEADS, t // blk, 1, blk), F32), _sds((1, 512), F32)],
        sem=('arbitrary',),
    )(o, dyn, g)


def flash_bwd(q, qt, k, kt, v, do, dot, lse, delta, *, name, scatter=()):
    t = q.shape[1]
    blk = _tile(t, ATT_BLK)
    nb = t // blk
    grid = (MLA_HEADS, nb)

    def body(q_ref, qt_ref, k_ref, kt_ref, v_ref, do_ref, dot_ref, lse_ref, delta_ref, *rest):
        nc = len(scatter)
        srcs, (dqt_ref, dk_ref, dv_ref), dsts, sems = rest[:nc], rest[nc:nc + 3], rest[nc + 3:2 * nc + 3], rest[2 * nc + 3:]
        if nc:
            _overlapped(grid, lambda: [_copies('scatter', srcs[i], dsts[i], *sems[3 * i:3 * i + 3]) for i in range(nc)])
        h, j = pl.program_id(0), pl.program_id(1)
        row = lax.broadcasted_iota(jnp.int32, (blk, blk), 0)
        col = lax.broadcasted_iota(jnp.int32, (blk, blk), 1)
        lane = lax.broadcasted_iota(jnp.int32, (1, LANES), 1)
        mine = (lane // V_DIM) == (h % 2)

        @pl.when(j == 0)
        def _():
            dqt_ref[...] = jnp.zeros_like(dqt_ref)

        kv, ktv, vv = k_ref[...], kt_ref[...], v_ref[...]

        def block(i, carry, masked):
            dk, dv = carry
            rows = pl.ds(pl.multiple_of(i * blk, blk), blk)
            pt = jnp.exp(_dot(kv, qt_ref[i]) - lse_ref[i])
            if masked:
                pt = jnp.where(col >= row, pt, 0.0)
            dv = dv + _dot(pt, do_ref[rows, :])
            dst = (pt * (_dot(vv, dot_ref[i]) - delta_ref[i])).astype(BF)
            dk = dk + _dot(dst, q_ref[rows, :])
            dqt_ref[i] += _dot(ktv, dst)
            return dk, dv

        zero = jnp.zeros((blk, LANES), F32)
        carry = block(j, (zero, zero), True)
        ngroups = (nb - 1 - j) // 3

        def group(p, c):
            i0 = j + 1 + 3 * p
            return block(i0 + 2, block(i0 + 1, block(i0, c, False), False), False)

        carry = lax.fori_loop(0, ngroups, group, carry)
        rest = j + 1 + 3 * ngroups
        npairs = (nb - rest) // 2
        carry = lax.fori_loop(0, npairs, lambda p, c: block(rest + 1, block(rest, c, False), False), carry)
        dk, dv = lax.fori_loop(rest + 2 * npairs, nb, lambda i, c: block(i, c, False), carry)
        dk_ref[...] = dk
        dv_ref[...] = jnp.where(mine, dv, 0.0)

    whole = pl.BlockSpec((None, t, LANES), lambda h, j: (h, 0, 0))
    wholet = pl.BlockSpec((None, nb, LANES, blk), lambda h, j: (h, 0, 0, 0))
    kvb = pl.BlockSpec((None, blk, LANES), lambda h, j: (h, j, 0))
    rowv = pl.BlockSpec((None, nb, 1, blk), lambda h, j: (h, 0, 0, 0))
    in_specs = [whole, wholet, kvb, pl.BlockSpec((None, None, LANES, blk), lambda h, j: (h, j, 0, 0)), kvb,
                pl.BlockSpec((t, LANES), lambda h, j: (0, h // 2)),
                pl.BlockSpec((None, nb, LANES, blk), lambda h, j: (h // 2, 0, 0, 0)), rowv, rowv]
    out_specs = [wholet, kvb, kvb]
    out_shape = [_sds((MLA_HEADS, nb, LANES, blk), F32), _sds((MLA_HEADS, t, LANES), F32), _sds((MLA_HEADS, t, LANES), F32)]
    args = (q, qt, k, kt, v, do, dot, lse, delta)
    nc = len(scatter)
    return _call(body, name=name, grid=grid, in_specs=in_specs + [_ANY] * nc, out_specs=out_specs + [_ANY] * nc,
                 out_shape=out_shape + [_sds(s.shape, s.dtype) for s in scatter], scratch=_COMM_SCRATCH * nc,
                 sem=('arbitrary', 'arbitrary') if nc else ('parallel', 'arbitrary'), vmem=VMEM_BIG)(*args, *scatter)


def mix_out_fwd(x, y_ssm, o, g_ssm, g_mla, w_out, *, name, tq=512):
    t = x.shape[0]
    tq = _tile(t, tq)

    def body(x_ref, ys_ref, o_ref, gs_ref, gm_ref, w_ref, x1_ref, yn_ref):
        ns = (_rms(ys_ref[...], SSM_W)[0] * gs_ref[...]).astype(BF)
        nm = (_rms(o_ref[...], 512)[0] * gm_ref[...]).astype(BF)
        yn_ref[:, 0:SSM_W] = ns
        yn_ref[:, SSM_W:D] = nm
        x1_ref[...] = x_ref[...] + _dot(ns, w_ref[0:SSM_W, :]) + _dot(nm, w_ref[SSM_W:D, :])

    row = lambda w: pl.BlockSpec((tq, w), lambda i: (i, 0))
    vec = pl.BlockSpec((1, 512), lambda i: (0, 0))
    return _call(body, name=name, grid=(t // tq,),
                 in_specs=[row(D), row(512), row(512), vec, vec, pl.BlockSpec((D, D), lambda i: (0, 0))],
                 out_specs=[row(D), row(D)], out_shape=[_sds((t, D), F32), _sds((t, D), BF)], sem=('parallel',),
                 )(x, y_ssm, o, g_ssm, g_mla, w_out)


MEM_SCALE = 1.0 / math.sqrt(MEM_HD)


def memkv_fwd(mem, g, wk, wv, kg, *, name):
    def body(m_ref, g_ref, wk_ref, wv_ref, kg_ref, mh_ref, k_ref, v_ref):
        mh = (_rms(m_ref[...], D)[0] * g_ref[...]).astype(BF)
        mh_ref[...] = mh
        for h in range(MEM_HEADS):
            cols = slice(h * LANES, (h + 1) * LANES)
            k_ref[h] = (_rms(_dot(mh, wk_ref[:, cols]), MEM_HD)[0] * kg_ref[...]).astype(BF)
            v_ref[h] = _dot(mh, wv_ref[:, cols]).astype(BF)

    return _call(body, name=name,
                 out_shape=[_sds((N_MEM, D), BF), _sds((MEM_HEADS, N_MEM, LANES), BF), _sds((MEM_HEADS, N_MEM, LANES), BF)],
                 )(mem, g, wk, wv, kg)


def memkv_bwd(mem, g, wk, wv, kg, dk, dv, *, name):
    def body(m_ref, g_ref, wk_ref, wv_ref, kg_ref, dk_ref, dv_ref, dwk_ref, dwv_ref, dkg_ref, dg_ref):
        mhat, _ = _rms(m_ref[...], D)
        mh = (mhat * g_ref[...]).astype(BF)
        lane = lax.broadcasted_iota(jnp.int32, (1, LANES), 1)
        dkg = jnp.zeros((1, LANES), F32)
        dmh = jnp.zeros((N_MEM, D), F32)
        for h in range(MEM_HEADS):
            cols = slice(h * LANES, (h + 1) * LANES)
            kh, kr = _rms(_dot(mh, wk_ref[:, cols]), MEM_HD)
            dko = dk_ref[h]
            dkg = dkg + _colsum(dko * kh)
            dkraw = _rms_bwd(kh, kr, dko * kg_ref[...], MEM_HD).astype(BF)
            dvh = jnp.where((lane // MEM_HD) == (h % 2), dv_ref[h], 0.0).astype(BF)
            dwk_ref[:, cols] = _dot(mh, dkraw, TN)
            dwv_ref[:, cols] = _dot(mh, dvh, TN)
            dmh = dmh + _dot(dkraw, wk_ref[:, cols], NT) + _dot(dvh, wv_ref[:, cols], NT)
        dkg_ref[...] = dkg
        dg_ref[...] = _colsum(dmh * mhat)

    return _call(body, name=name,
                 out_shape=[_sds((D, 512), F32), _sds((D, 512), F32), _sds((1, LANES), F32), _sds((1, D), F32)],
                 )(mem, g, wk, wv, kg, dk, dv)


def memattn_fwd(x, g, wq, qg, kh, vh, wo, g_next, *, name, tq=512):
    t = x.shape[0]
    tq = _tile(t, tq)

    def body(x_ref, g_ref, wq_ref, qg_ref, k_ref, v_ref, wo_ref, gn_ref, x2_ref, hn_ref, h3_ref):
        xv = x_ref[...]
        hn = (_rms(xv, D)[0] * g_ref[...]).astype(BF)
        hn_ref[...] = hn
        out = xv
        for pb in range(MEM_HEADS // 2):
            o = jnp.zeros((tq, LANES), F32)
            for h in (2 * pb, 2 * pb + 1):
                q = _rms(_dot(hn, wq_ref[:, h * LANES:(h + 1) * LANES]), MEM_HD)[0] * qg_ref[...]
                s = _dot(q, k_ref[h], NT) * MEM_SCALE
                p = jnp.exp(s - jnp.max(s, axis=-1, keepdims=True))
                p = p / jnp.sum(p, axis=-1, keepdims=True)
                o = o + _dot(p, v_ref[h])
            out = out + _dot(o, wo_ref[pb * LANES:(pb + 1) * LANES, :])
        x2_ref[...] = out
        h3_ref[...] = (_rms(out, D)[0] * gn_ref[...]).astype(BF)

    full = lambda shape: pl.BlockSpec(shape, lambda i: (0,) * len(shape))
    row = pl.BlockSpec((tq, D), lambda i: (i, 0))
    return _call(body, name=name, grid=(t // tq,),
                 in_specs=[row, full((1, D)), full((D, 512)), full((1, LANES)), full((MEM_HEADS, N_MEM, LANES)),
                           full((MEM_HEADS, N_MEM, LANES)), full((MEM_HEADS * MEM_HD, D)), full((1, D))],
                 out_specs=[row, row, row], out_shape=[_sds((t, D), F32), _sds((t, D), BF), _sds((t, D), BF)],
                 sem=('parallel',))(x, g, wq, qg, kh, vh, wo, g_next)


def memattn_bwd(x, dx2, g, wq, qg, kh, vh, wo, *, name, tq=512):
    t = x.shape[0]
    tq = _tile(t, tq)

    def body(x_ref, dx2_ref, g_ref, wq_ref, qg_ref, k_ref, v_ref, wo_ref,
             dx_ref, dxb_ref, o_ref, dqr_ref, dk_ref, dv_ref, dqg_ref, dg_ref):
        @pl.when(pl.program_id(0) == 0)
        def _():
            dk_ref[...] = jnp.zeros_like(dk_ref)
            dv_ref[...] = jnp.zeros_like(dv_ref)
            dqg_ref[...] = jnp.zeros_like(dqg_ref)
            dg_ref[...] = jnp.zeros_like(dg_ref)

        xhat, xr = _rms(x_ref[...], D)
        hn = (xhat * g_ref[...]).astype(BF)
        dx2 = dx2_ref[...]
        dx2b = dx2.astype(BF)
        dh = jnp.zeros((tq, D), F32)
        dqg = jnp.zeros((1, LANES), F32)
        for pb in range(MEM_HEADS // 2):
            do = _dot(dx2b, wo_ref[pb * LANES:(pb + 1) * LANES, :], NT).astype(BF)
            o = jnp.zeros((tq, LANES), F32)
            for h in (2 * pb, 2 * pb + 1):
                cols = slice(h * LANES, (h + 1) * LANES)
                qh, qr = _rms(_dot(hn, wq_ref[:, cols]), MEM_HD)
                qb = (qh * qg_ref[...]).astype(BF)
                s = _dot(qb, k_ref[h], NT) * MEM_SCALE
                p = jnp.exp(s - jnp.max(s, axis=-1, keepdims=True))
                p = p / jnp.sum(p, axis=-1, keepdims=True)
                pb16 = p.astype(BF)
                o = o + _dot(pb16, v_ref[h])
                dv_ref[h] += _dot(pb16, do, TN)
                dp = _dot(do, v_ref[h], NT)
                ds = (p * (dp - jnp.sum(dp * p, axis=-1, keepdims=True)) * MEM_SCALE).astype(BF)
                dk_ref[h] += _dot(ds, qb, TN)
                dqo = _dot(ds, k_ref[h])
                dqg = dqg + _colsum(dqo * qh)
                dqraw = _rms_bwd(qh, qr, dqo * qg_ref[...], MEM_HD).astype(BF)
                dqr_ref[:, cols] = dqraw
                dh = dh + _dot(dqraw, wq_ref[:, cols], NT)
            o_ref[:, pb * LANES:(pb + 1) * LANES] = o.astype(BF)
        dx = dx2 + _rms_bwd(xhat, xr, dh * g_ref[...], D)
        dx_ref[...] = dx
        dxb_ref[...] = dx.astype(BF)
        dqg_ref[...] += dqg
        dg_ref[...] += _colsum(dh * xhat)

    full = lambda shape: pl.BlockSpec(shape, lambda i: (0,) * len(shape))
    row = lambda w: pl.BlockSpec((tq, w), lambda i: (i, 0))
    return _call(body, name=name, grid=(t // tq,),
                 in_specs=[row(D), row(D), full((1, D)), full((D, 512)), full((1, LANES)), full((MEM_HEADS, N_MEM, LANES)),
                           full((MEM_HEADS, N_MEM, LANES)), full((MEM_HEADS * MEM_HD, D))],
                 out_specs=[row(D), row(D), row(256), row(512), full((MEM_HEADS, N_MEM, LANES)), full((MEM_HEADS, N_MEM, LANES)),
                            full((1, LANES)), full((1, D))],
                 out_shape=[_sds((t, D), F32), _sds((t, D), BF), _sds((t, 256), BF), _sds((t, 512), BF),
                            _sds((MEM_HEADS, N_MEM, LANES), F32), _sds((MEM_HEADS, N_MEM, LANES), F32),
                            _sds((1, LANES), F32), _sds((1, D), F32)],
                 sem=('arbitrary',))(x, dx2, g, wq, qg, kh, vh, wo)


def mlp_fwd(x, h, w1, w2, g_next, *, name, tq=1024, tf=512):
    t = x.shape[0]
    tq = _tile(t, tq)
    nf = D_FF // tf

    def body(x_ref, h_ref, w1_ref, w2_ref, *rest):
        o_ref = rest[-2] if g_next is not None else rest[-1]

        @pl.when(pl.program_id(1) == 0)
        def _():
            o_ref[...] = x_ref[...]

        a = jnp.maximum(_dot(h_ref[...], w1_ref[...]), 0.0)
        o_ref[...] += _dot(a * a, w2_ref[...])
        if g_next is not None:
            g_ref, hn_ref = rest[0], rest[-1]

            @pl.when(pl.program_id(1) == nf - 1)
            def _():
                hn_ref[...] = (_rms(o_ref[...], D)[0] * g_ref[...]).astype(BF)

    row = pl.BlockSpec((tq, D), lambda i, f: (i, 0))
    in_specs = [row, row, pl.BlockSpec((None, D, tf), lambda i, f: (f, 0, 0)), pl.BlockSpec((tf, D), lambda i, f: (f, 0))]
    if g_next is None:
        return _call(body, name=name, grid=(t // tq, nf), in_specs=in_specs, out_specs=row, out_shape=_sds((t, D), F32),
                     sem=('parallel', 'arbitrary'), vmem=VMEM_BIG)(x, h, w1, w2), None
    return _call(body, name=name, grid=(t // tq, nf), in_specs=in_specs + [pl.BlockSpec((1, D), lambda i, f: (0, 0))],
                 out_specs=[row, row], out_shape=[_sds((t, D), F32), _sds((t, D), BF)],
                 sem=('parallel', 'arbitrary'), vmem=VMEM_BIG)(x, h, w1, w2, g_next)


def mlp_bwd(h, dx, w1, w2, *, name, tq=1024, tf=512):
    t = h.shape[0]
    tq = _tile(t, tq)

    def body(h_ref, dx_ref, w1_ref, w2_ref, dh_ref, r_ref, da_ref):
        @pl.when(pl.program_id(1) == 0)
        def _():
            dh_ref[...] = jnp.zeros_like(dh_ref)

        a = jnp.maximum(_dot(h_ref[...], w1_ref[...]), 0.0)
        r_ref[...] = (a * a).astype(BF)
        da = (_dot(dx_ref[...], w2_ref[...], NT) * (2.0 * a)).astype(BF)
        da_ref[...] = da
        dh_ref[...] += _dot(da, w1_ref[...], NT)

    row = pl.BlockSpec((tq, D), lambda i, f: (i, 0))
    act = pl.BlockSpec((tq, tf), lambda i, f: (i, f))
    return _call(body, name=name, grid=(t // tq, D_FF // tf),
                 in_specs=[row, row, pl.BlockSpec((None, D, tf), lambda i, f: (f, 0, 0)), pl.BlockSpec((tf, D), lambda i, f: (f, 0))],
                 out_specs=[row, act, act], out_shape=[_sds((t, D), F32), _sds((t, D_FF), BF), _sds((t, D_FF), BF)],
                 sem=('parallel', 'arbitrary'), vmem=VMEM_BIG)(h, dx, w1, w2)


def loss_fwd_bwd(y, target, *, name, tq=512):
    t = y.shape[0]
    tq = _tile(t, tq)

    def body(y_ref, t_ref, dy_ref, dyb_ref, l_ref):
        @pl.when(pl.program_id(0) == 0)
        def _():
            l_ref[...] = jnp.zeros_like(l_ref)

        e = y_ref[...] - t_ref[...]
        dy = e * (1.0 / D)
        dy_ref[...] = dy
        dyb_ref[...] = dy.astype(BF)
        l_ref[...] += _colsum(e * e) * (0.5 / D)

    row = pl.BlockSpec((tq, D), lambda i: (i, 0))
    return _call(body, name=name, grid=(t // tq,), in_specs=[row, row],
                 out_specs=[row, row, pl.BlockSpec((1, D), lambda i: (0, 0))],
                 out_shape=[_sds((t, D), F32), _sds((t, D), BF), _sds((1, D), F32)], sem=('arbitrary',))(y, target)


def prep_early(w):
    w_in = w['w_in']
    z = lambda r, c: jnp.zeros((r, c), w_in.dtype)
    w_in_pad = jnp.concatenate([w_in[:, :896], z(D, 64), w_in[:, 896:928], z(D, 32)], axis=1)
    wq = w['mla_w_uq'].reshape(Q_LORA, MLA_HEADS, QK_DIM).transpose(1, 0, 2)
    wq = jnp.pad(wq, ((0, 0), (0, 0), (0, LANES - QK_DIM)))
    ukv = w['mla_w_ukv'].reshape(KV_LORA, MLA_HEADS, QK_NOPE + V_DIM).transpose(1, 0, 2)
    wk = jnp.pad(ukv[:, :, :QK_NOPE], ((0, 0), (0, 0), (0, LANES - QK_NOPE)))
    vpart = ukv[:, :, QK_NOPE:]
    zv = jnp.zeros_like(vpart)
    odd = (jnp.arange(MLA_HEADS) % 2)[:, None, None] == 1
    wv = jnp.where(odd, jnp.concatenate([zv, vpart], axis=2), jnp.concatenate([vpart, zv], axis=2))
    return dict(w_in=w_in_pad, w_glu=w['ssm_w_glu'], wq=wq, wk=wk, wv=wv)


def prep_late(w):
    mq = jnp.pad(w['mem_w_q'].reshape(D, MEM_HEADS, MEM_HD), ((0, 0), (0, 0), (0, LANES - MEM_HD))).reshape(D, 512)
    mkv = w['mem_w_kv'].reshape(D, MEM_HEADS, 2 * MEM_HD)
    mk = jnp.pad(mkv[:, :, :MEM_HD], ((0, 0), (0, 0), (0, LANES - MEM_HD))).reshape(D, 512)
    mvp = mkv[:, :, MEM_HD:]
    zm = jnp.zeros_like(mvp)
    modd = (jnp.arange(MEM_HEADS) % 2)[None, :, None] == 1
    mv = jnp.where(modd, jnp.concatenate([zm, mvp], axis=2), jnp.concatenate([mvp, zm], axis=2)).reshape(D, 512)
    return dict(w_out=w['w_out'], mq=mq, mk=mk, mv=mv, mo=w['mem_w_o'], w1=w['mlp_w1'], w2=w['mlp_w2'])


def prep_small(t, s):
    row = lambda a: a.reshape(1, -1)
    pad = lambda a: jnp.pad(a, (0, LANES - a.shape[0])).reshape(1, LANES)
    out = s5_prep(t, s['ssm_lambda_re'], s['ssm_lambda_im'], s['ssm_log_step'], s['ssm_b_re'], s['ssm_b_im'],
                  s['ssm_c_re'], s['ssm_c_im'])
    out.update(d=row(s['ssm_d']), norm_mix=row(s['norm_mix']), b_glu=row(s['ssm_b_glu']), q_norm=row(s['mla_q_norm']),
               kv_norm=row(s['mla_kv_norm']), q_gain=pad(s['mla_q_gain']), k_gain=pad(s['mla_k_gain']),
               g_ssm=row(s['out_norm_ssm']), g_mla=row(s['out_norm_mla']), norm_mem_q=row(s['norm_mem_q']),
               norm_mem_kv=row(s['norm_mem_kv']), mem_q_gain=pad(s['mem_q_gain']), mem_k_gain=pad(s['mem_k_gain']),
               norm_mlp=row(s['norm_mlp']))
    return out


def _perm(a):
    t, c = a.shape
    return a.reshape(SEGS, t // SEGS, c).transpose(1, 0, 2).reshape(t, c)


def _unperm(a):
    t, c = a.shape
    return a.reshape(t // SEGS, SEGS, c).transpose(1, 0, 2).reshape(t, c)


def layer_fwd(l, x, h1, mem, tabs, plan, ws, g_next):
    n = lambda s: f'l{l}_{s}'
    wb = prep_early(plan.early(l))
    if h1 is None:
        h1 = rmsnorm_fwd(x, ws['norm_mix'], name=n('norm_mix'))
    proj = mm(h1, wb['w_in'], 'nn', name=n('w_in'))
    u_p = _perm(proj[:, :SSM_W])
    ypre_p = s5_fwd(u_p, ws, name=n('s5'))
    ypre = _unperm(ypre_p)
    y_ssm = glu_fwd(ypre, wb['w_glu'], ws['b_glu'], name=n('glu'))
    mw = dict(q_norm=ws['q_norm'], kv_norm=ws['kv_norm'], wq=wb['wq'], wk=wb['wk'], wv=wb['wv'],
              q_gain=ws['q_gain'], k_gain=ws['k_gain'])
    q, qt, k, kt, v = mla_prep_fwd(proj, tabs, mw, name=n('mla_prep'))
    o, lse, *gathered = flash_fwd(q, kt, v, name=n('flash'), gather=plan.gather_src(l))
    plan.gathered(l, gathered)
    wb.update(prep_late(plan.late(l)))
    x1, yn = mix_out_fwd(x, y_ssm, o, ws['g_ssm'], ws['g_mla'], wb['w_out'], name=n('mix_out'))
    mh, kh, vh = memkv_fwd(mem, ws['norm_mem_kv'], wb['mk'], wb['mv'], ws['mem_k_gain'], name=n('memkv'))
    x2, h2, h3 = memattn_fwd(x1, ws['norm_mem_q'], wb['mq'], ws['mem_q_gain'], kh, vh, wb['mo'], ws['norm_mlp'],
                             name=n('memattn'))
    x3, h1_next = mlp_fwd(x2, h3, wb['w1'], wb['w2'], g_next, name=n('mlp'))
    saved = dict(x=x, h1=h1, proj=proj, u_p=u_p, ypre=ypre, y_ssm=y_ssm, q=q, qt=qt, k=k, kt=kt, v=v, o=o, lse=lse, x1=x1, yn=yn,
                 kh=kh, vh=vh, x2=x2, h2=h2, h3=h3, mw=mw)
    return x3, h1_next, wb, saved


def layer_bwd(l, dx3, dx3b, mem, tabs, plan, wb, ws, sv):
    n = lambda s: f'l{l}_{s}_bwd'
    gb, gs = {}, {}
    structs = lambda names: {k: _sds(plan.shapes[k], F32) for k in names}
    dh3, r, da = mlp_bwd(sv['h3'], dx3b, wb['w1'], wb['w2'], name=n('mlp'))
    gb['w1'] = mm(sv['h3'], da, 'tn', name=n('w1'), slots=NDEV)
    gb['w2'] = mm(r, dx3b, 'tn', name=n('w2'))
    dx2, dx2b, gs['norm_mlp'] = rmsnorm_bwd(sv['x2'], ws['norm_mlp'], dh3, dx3, name=n('norm_mlp'))
    dx1, dx1b, o_mem, dqr_mem, dkh, dvh, gs['mem_q_gain'], gs['norm_mem_q'] = memattn_bwd(
        sv['x1'], dx2, ws['norm_mem_q'], wb['mq'], ws['mem_q_gain'], sv['kh'], sv['vh'], wb['mo'], name=n('memattn'))
    gb['mo'] = mm(o_mem, dx2b, 'tn', name=n('mo'))
    gb['mq'] = mm(sv['h2'], dqr_mem, 'tn', name=n('mq'))
    gb['mk'], gb['mv'], gs['mem_k_gain'], gs['norm_mem_kv'] = memkv_bwd(
        mem, ws['norm_mem_kv'], wb['mk'], wb['mv'], ws['mem_k_gain'], dkh, dvh, name=n('memkv'))
    dyn = mm(dx1b, wb['w_out'], 'nt', name=n('w_out_dx'))
    gb['w_out'] = mm(sv['yn'], dx1b, 'tn', name=n('w_out'))
    dy_ssm, _, gs['g_ssm'] = rmsnorm_bwd(sv['y_ssm'], ws['g_ssm'], dyn, None, name=n('out_norm_ssm'), col=0)
    do, dot, delta, gs['g_mla'] = mla_out_bwd(sv['o'], dyn, ws['g_mla'], name=n('out_norm_mla'))
    late = {k: gb.pop(k) for k in ('w_out', 'mq', 'mk', 'mv', 'mo', 'w1', 'w2')}
    plan.late_grads(l, jax.linear_transpose(prep_late, structs(BIG_LATE))(late)[0])
    dq, dk, dv, *received = flash_bwd(sv['q'], sv['qt'], sv['k'], sv['kt'], sv['v'], do, dot, sv['lse'], delta,
                                      name=n('flash'), scatter=plan.scatter_src(l))
    plan.scattered(l, received)
    (dproj_m, cqn, ckvn, dqr, dkr, dvb, gs['q_norm'], gs['kv_norm'], gs['q_gain'], gs['k_gain']) = mla_prep_bwd(
        sv['proj'], tabs, sv['mw'], dq, dk, dv, name=n('mla_prep'))
    by_head = lambda g: g.reshape(g.shape[0], MLA_HEADS, LANES).transpose(1, 0, 2)
    gb['wq'] = by_head(mm(cqn, dqr, 'tn', name=n('wq')))
    gb['wk'] = by_head(mm(ckvn, dkr, 'tn', name=n('wk')))
    gb['wv'] = by_head(mm(ckvn, dvb, 'tn', name=n('wv')))
    dypre, yg, dz, gs['b_glu'] = glu_bwd(sv['ypre'], dy_ssm, wb['w_glu'], ws['b_glu'], name=n('glu'))
    gb['w_glu'] = mm(yg, dz, 'tn', name=n('w_glu'))
    du_p, gs['ar'], gs['ai'], gs['bre'], gs['bim'], gs['cre'], gs['cim'], gs['d'] = s5_bwd(sv['u_p'], _perm(dypre), ws, name=n('s5'))
    dprojb = jnp.concatenate([_unperm(du_p), dproj_m], axis=1)
    dh1 = mm(dprojb, wb['w_in'], 'nt', name=n('w_in_dx'))
    gb['w_in'] = mm(sv['h1'], dprojb, 'tn', name=n('w_in'))
    dx0, dx0b, gs['norm_mix'] = rmsnorm_bwd(sv['x'], ws['norm_mix'], dh1, dx1, name=n('norm_mix'))
    plan.early_grads(l, jax.linear_transpose(prep_early, structs(BIG_EARLY))(gb)[0])
    return dx0, dx0b, gs


def local_step(x, mem, positions, target, small, plan):
    t = x.shape[0]
    tabs = rope_tables(positions)
    preps = [jax.vjp(functools.partial(prep_small, t), {k: small[k][l] for k in SMALL}) for l in range(DEPTH)]
    layers, h1 = [], None
    for l in range(DEPTH):
        ws, small_vjp = preps[l]
        g_next = preps[l + 1][0]['norm_mix'] if l + 1 < DEPTH else None
        x, h1, wb, sv = layer_fwd(l, x, h1, mem, tabs, plan, ws, g_next)
        layers.append((wb, ws, small_vjp, sv))
    dx, dxb, lcols = loss_fwd_bwd(x, target, name='loss')
    loss = jnp.sum(lcols)
    gsmall = [None] * DEPTH
    for l in reversed(range(DEPTH)):
        wb, ws, small_vjp, sv = layers[l]
        dx, dxb, gs = layer_bwd(l, dx, dxb, mem, tabs, plan, wb, ws, sv)
        gs['pr'], gs['pi'] = jnp.zeros_like(ws['pr']), jnp.zeros_like(ws['pi'])
        gsmall[l] = small_vjp(gs)[0]
    return loss, dx, gsmall


class ExchangePlan:
    def __init__(self, shard_shapes, mine, first_early):
        self.shapes = {k: (s[1] * (NDEV if BIG_AXIS[k] == 1 else 1), s[2] * (NDEV if BIG_AXIS[k] == 2 else 1))
                       for k, s in shard_shapes.items()}
        self.shard = {k: s[1:] for k, s in shard_shapes.items()}
        self.shapes['mlp_w1'] = (NDEV,) + self.shard['mlp_w1']
        self.mine = mine
        self.w_early = {0: first_early}
        self.w_late = {}
        self.g_late, self.g_early = {}, {}
        self.r_late, self.r_early = {}, {}

    def _unpack(self, g, names):
        out, r0 = {}, 0
        for k in names:
            nr = math.prod(self.shard[k]) // D
            s = g[:, r0:r0 + nr]
            out[k] = (s.reshape(self.shapes[k]) if k == 'mlp_w1'
                      else _from_slots(s.reshape(NDEV, -1), (1,) + self.shard[k], BIG_AXIS[k])[0])
            r0 += nr
        return out

    def _pack(self, g, names, rows):
        slots = jnp.concatenate([g[k].reshape(NDEV, -1) if k == 'mlp_w1' else _to_slots(g[k][None], BIG_AXIS[k])
                                 for k in names], axis=1)
        return jnp.pad(slots, ((0, 0), (0, rows * D - slots.shape[1]))).astype(BF).reshape(NDEV, rows, D)

    def early(self, l):
        return self._unpack(self.w_early.pop(l), BIG_EARLY)

    def late(self, l):
        return self._unpack(self.w_late.pop(l), BIG_LATE)

    def gather_src(self, l):
        src = [self.mine[l, :LATE_ROWS]]
        if l + 1 < DEPTH:
            src.append(self.mine[l + 1, LATE_ROWS:])
        return tuple(src)

    def gathered(self, l, res):
        self.w_late[l] = res[0]
        if l + 1 < DEPTH:
            self.w_early[l + 1] = res[1]

    def late_grads(self, l, g):
        self.g_late[l] = self._pack(g, BIG_LATE, LATE_ROWS)

    def early_grads(self, l, g):
        self.g_early[l] = self._pack(g, BIG_EARLY, LAYER_ROWS - LATE_ROWS)

    def scatter_src(self, l):
        src = [self.g_late.pop(l)]
        if l + 1 < DEPTH:
            src.append(self.g_early.pop(l + 1))
        return tuple(src)

    def scattered(self, l, res):
        self.r_late[l] = res[0]
        if l + 1 < DEPTH:
            self.r_early[l + 1] = res[1]


def _peer(k):
    x, y, c = lax.axis_index('x'), lax.axis_index('y'), lax.axis_index('c')
    px, py, pc = x ^ ((k >> 2) & 1), y ^ ((k >> 1) & 1), c ^ (k & 1)
    return (px, py, pc), 4 * px + 2 * py + pc


def _copies(kind, src_ref, dst_ref, send_sems, recv_sems, loc_sem):
    _, me = _peer(0)
    src = (lambda p: src_ref.at[p]) if kind == 'scatter' else (lambda p: src_ref)
    local = pltpu.make_async_copy(src(me), dst_ref.at[me], loc_sem)
    sends, recvs = [], []
    for k in range(1, NDEV):
        dev, p = _peer(k)
        for slot, lst in ((me, sends), (p, recvs)):
            lst.append(pltpu.make_async_remote_copy(src_ref=src(p), dst_ref=dst_ref.at[slot], send_sem=send_sems.at[k - 1],
                                                    recv_sem=recv_sems.at[k - 1], device_id=dev,
                                                    device_id_type=pl.DeviceIdType.MESH))
    return local, sends, recvs


def _start_copies(cs):
    local, sends, _ = cs
    local.start()
    for cp in sends:
        cp.start()


def _wait_copies(cs):
    local, sends, recvs = cs
    for cp in sends:
        cp.wait_send()
    for cp in recvs:
        cp.wait_recv()
    local.wait()


_COMM_SCRATCH = (pltpu.SemaphoreType.DMA((NDEV - 1,)), pltpu.SemaphoreType.DMA((NDEV - 1,)), pltpu.SemaphoreType.DMA(()))
_ANY = pl.BlockSpec(memory_space=pl.ANY)


def exchange(scatters, gathers, *, name):
    ins = list(scatters) + list(gathers)
    kinds = ['scatter'] * len(scatters) + ['gather'] * len(gathers)
    n_in = len(ins)
    outs = [_sds(a.shape, a.dtype) for a in scatters] + [_sds((NDEV,) + b.shape, b.dtype) for b in gathers]

    def body(*refs):
        in_refs, out_refs, sems = refs[:n_in], refs[n_in:2 * n_in], refs[2 * n_in:]
        sets = [_copies(kind, in_refs[i], out_refs[i], *sems[3 * i:3 * i + 3]) for i, kind in enumerate(kinds)]
        for cs in sets:
            _start_copies(cs)
        for cs in sets:
            _wait_copies(cs)

    return pl.pallas_call(body, name=name, in_specs=[_ANY] * n_in, out_specs=[_ANY] * n_in, out_shape=outs,
                          scratch_shapes=list(_COMM_SCRATCH * n_in))(*ins)


def adamw(w, m, v, g8, *, name, tr):
    r = w.shape[0]
    c1 = 1.0 / (1.0 - ADAM_B1 ** ADAM_STEP)
    c2 = 1.0 / (1.0 - ADAM_B2 ** ADAM_STEP)

    def body(w_ref, m_ref, v_ref, g_ref, go_ref, d_ref, mo_ref, vo_ref):
        g = g_ref[0].astype(F32)
        for i in range(1, NDEV):
            g = g + g_ref[i].astype(F32)
        m_new = ADAM_B1 * m_ref[...] + (1.0 - ADAM_B1) * g
        v_new = ADAM_B2 * v_ref[...] + (1.0 - ADAM_B2) * (g * g)
        go_ref[...] = g
        mo_ref[...] = m_new
        vo_ref[...] = v_new
        d_ref[...] = -ADAM_LR * ((m_new * c1) / (jnp.sqrt(v_new * c2) + ADAM_EPS) + ADAM_WD * w_ref[...])

    row = pl.BlockSpec((tr, D), lambda i: (i, 0))
    return _call(body, name=name, grid=(r // tr,),
                 in_specs=[row, row, row, pl.BlockSpec((NDEV, tr, D), lambda i: (0, i, 0))],
                 out_specs=[row] * 4, out_shape=[_sds((r, D), F32)] * 4, sem=('parallel',), vmem=VMEM_BIG)(w, m, v, g8)


def _flat_rows(parts, rows):
    flat = jnp.concatenate([p.reshape(-1) for p in parts])
    return jnp.pad(flat, (0, rows * D - flat.shape[0])).reshape(rows, D)


def _unflat(flat2d, shapes):
    flat = flat2d.reshape(-1)
    out, off = [], 0
    for s in shapes:
        n = math.prod(s)
        out.append(flat[off:off + n].reshape(s))
        off += n
    return out


def _to_slots(g, axis):
    l, r, c = g.shape
    if axis == 1:
        return g.reshape(l, NDEV, r // NDEV, c).transpose(1, 0, 2, 3).reshape(NDEV, -1)
    return g.reshape(l, r, NDEV, c // NDEV).transpose(2, 0, 1, 3).reshape(NDEV, -1)


def _from_slots(s, shard_shape, axis):
    l, r, c = shard_shape
    s = s.reshape(NDEV, l, r, c)
    if axis == 1:
        return s.transpose(1, 0, 2, 3).reshape(l, NDEV * r, c)
    return s.transpose(1, 2, 0, 3).reshape(l, r, NDEV * c)


LATE_ROWS = 1280
LAYER_ROWS = 1536
BIG_ROWS = DEPTH * LAYER_ROWS
SMALL_ROWS = 640


def kernel(x, mem, positions, norm_mix, w_in, ssm_lambda_re, ssm_lambda_im, ssm_log_step, ssm_b_re, ssm_b_im, ssm_c_re, ssm_c_im, ssm_d, ssm_w_glu, ssm_b_glu, mla_q_norm, mla_w_uq, mla_kv_norm, mla_w_ukv, mla_q_gain, mla_k_gain, out_norm_ssm, out_norm_mla, w_out, norm_mem_q, norm_mem_kv, mem_w_q, mem_w_kv, mem_q_gain, mem_k_gain, mem_w_o, norm_mlp, mlp_w1, mlp_w2, loss_target, m_norm_mix, m_w_in, m_ssm_lambda_re, m_ssm_lambda_im, m_ssm_log_step, m_ssm_b_re, m_ssm_b_im, m_ssm_c_re, m_ssm_c_im, m_ssm_d, m_ssm_w_glu, m_ssm_b_glu, m_mla_q_norm, m_mla_w_uq, m_mla_kv_norm, m_mla_w_ukv, m_mla_q_gain, m_mla_k_gain, m_out_norm_ssm, m_out_norm_mla, m_w_out, m_norm_mem_q, m_norm_mem_kv, m_mem_w_q, m_mem_w_kv, m_mem_q_gain, m_mem_k_gain, m_mem_w_o, m_norm_mlp, m_mlp_w1, m_mlp_w2, v_norm_mix, v_w_in, v_ssm_lambda_re, v_ssm_lambda_im, v_ssm_log_step, v_ssm_b_re, v_ssm_b_im, v_ssm_c_re, v_ssm_c_im, v_ssm_d, v_ssm_w_glu, v_ssm_b_glu, v_mla_q_norm, v_mla_w_uq, v_mla_kv_norm, v_mla_w_ukv, v_mla_q_gain, v_mla_k_gain, v_out_norm_ssm, v_out_norm_mla, v_w_out, v_norm_mem_q, v_norm_mem_kv, v_mem_w_q, v_mem_w_kv, v_mem_q_gain, v_mem_k_gain, v_mem_w_o, v_norm_mlp, v_mlp_w1, v_mlp_w2):
    wvals = (norm_mix, w_in, ssm_lambda_re, ssm_lambda_im, ssm_log_step, ssm_b_re, ssm_b_im, ssm_c_re, ssm_c_im, ssm_d, ssm_w_glu, ssm_b_glu, mla_q_norm, mla_w_uq, mla_kv_norm, mla_w_ukv, mla_q_gain, mla_k_gain, out_norm_ssm, out_norm_mla, w_out, norm_mem_q, norm_mem_kv, mem_w_q, mem_w_kv, mem_q_gain, mem_k_gain, mem_w_o, norm_mlp, mlp_w1, mlp_w2)
    mvals = (m_norm_mix, m_w_in, m_ssm_lambda_re, m_ssm_lambda_im, m_ssm_log_step, m_ssm_b_re, m_ssm_b_im, m_ssm_c_re, m_ssm_c_im, m_ssm_d, m_ssm_w_glu, m_ssm_b_glu, m_mla_q_norm, m_mla_w_uq, m_mla_kv_norm, m_mla_w_ukv, m_mla_q_gain, m_mla_k_gain, m_out_norm_ssm, m_out_norm_mla, m_w_out, m_norm_mem_q, m_norm_mem_kv, m_mem_w_q, m_mem_w_kv, m_mem_q_gain, m_mem_k_gain, m_mem_w_o, m_norm_mlp, m_mlp_w1, m_mlp_w2)
    vvals = (v_norm_mix, v_w_in, v_ssm_lambda_re, v_ssm_lambda_im, v_ssm_log_step, v_ssm_b_re, v_ssm_b_im, v_ssm_c_re, v_ssm_c_im, v_ssm_d, v_ssm_w_glu, v_ssm_b_glu, v_mla_q_norm, v_mla_w_uq, v_mla_kv_norm, v_mla_w_ukv, v_mla_q_gain, v_mla_k_gain, v_out_norm_ssm, v_out_norm_mla, v_w_out, v_norm_mem_q, v_norm_mem_kv, v_mem_w_q, v_mem_w_kv, v_mem_q_gain, v_mem_k_gain, v_mem_w_o, v_norm_mlp, v_mlp_w1, v_mlp_w2)
    w = dict(zip(WEIGHTS, wvals))
    m = dict(zip(WEIGHTS, mvals))
    v = dict(zip(WEIGHTS, vvals))

    shard_shapes = {k: w[k].shape for k in BIG}
    layer_shapes = [shard_shapes[k][1:] for k in BIG]

    def layer_flat(parts):
        flat = jnp.concatenate([p.reshape(DEPTH, -1) for p in parts], axis=1)
        return jnp.pad(flat, ((0, 0), (0, LAYER_ROWS * D - flat.shape[1]))).reshape(DEPTH, LAYER_ROWS, D)

    mine = layer_flat([w[k].astype(BF) for k in BIG])
    first, = exchange([], [mine[0, LATE_ROWS:]], name='gather_early0')
    plan = ExchangePlan(shard_shapes, mine, first)
    small = {k: w[k] for k in SMALL}
    loss, grad_x, gsmall = local_step(x[0], mem[0], positions[0], loss_target[0], small, plan)
    gs_full = [jnp.stack([gsmall[l][k] for l in range(DEPTH)]) for k in SMALL]
    small_flat = _flat_rows(gs_full, SMALL_ROWS).astype(BF)
    plan.r_early[0], g8_small, losses = exchange([plan.g_early.pop(0)], [small_flat, jnp.full((8, LANES), loss, F32)],
                                                 name='exchange_last')
    loss_all = jnp.sum(losses[:, 0, 0])
    g8_big = jnp.concatenate([r[l] for l in range(DEPTH) for r in (plan.r_late, plan.r_early)], axis=1)

    small_shapes = [w[k].shape for k in SMALL]
    flat_big = lambda d: layer_flat([d[k] for k in BIG]).reshape(BIG_ROWS, D)
    gb, db, mb, vb = adamw(flat_big(w), flat_big(m), flat_big(v), g8_big, name='adamw_big', tr=256)
    gs, ds, ms, vs = adamw(_flat_rows([w[k] for k in SMALL], SMALL_ROWS), _flat_rows([m[k] for k in SMALL], SMALL_ROWS),
                           _flat_rows([v[k] for k in SMALL], SMALL_ROWS), g8_small, name='adamw_small', tr=128)

    def unflat_big(fb):
        fb, out, r0 = fb.reshape(DEPTH, LAYER_ROWS, D), [], 0
        for k, shp in zip(BIG, layer_shapes):
            nr = math.prod(shp) // D
            out.append(fb[:, r0:r0 + nr].reshape(shard_shapes[k]))
            r0 += nr
        return out

    res = {}
    for tag, fb, fs in (('g', gb, gs), ('d', db, ds), ('m', mb, ms), ('v', vb, vs)):
        res[tag] = dict(zip(BIG, unflat_big(fb)))
        res[tag].update(zip(SMALL, _unflat(fs, small_shapes)))
    return (loss_all, grad_x[None], *[res['g'][k] for k in WEIGHTS], *[res['d'][k] for k in WEIGHTS],
            *[res['m'][k] for k in WEIGHTS], *[res['v'][k] for k in WEIGHTS])
```

```python
import functools
import math

import jax
import jax.numpy as jnp
from jax import lax
from jax.experimental import pallas as pl
from jax.experimental.pallas import tpu as pltpu

F32 = jnp.float32
BF = jnp.bfloat16

D = 1024
DEPTH = 4
N_MEM = 256
MEM_HEADS = 4
MEM_HD = 64
SSM_W = 512
SSM_G = 32
SSM_H = 16
SSM_P = 64
MLA_HEADS = 8
QK_NOPE = 64
QK_ROPE = 32
QK_DIM = 96
V_DIM = 64
Q_LORA = 256
KV_LORA = 128
ROPE_THETA = 10000.0
D_FF = 4096
IN_COLS = 928
EPS = 1e-6
NDEV = 8
LANES = 128
SEGS = 32
S5_LW = 256
S5_NHB = (SSM_G * SSM_P) // S5_LW
ADAM_LR = 0.001
ADAM_B1 = 0.9
ADAM_B2 = 0.999
ADAM_EPS = 1e-08
ADAM_WD = 0.01
ADAM_STEP = 10
VMEM_BIG = 56 * 1024 * 1024

NN = (((1,), (0,)), ((), ()))
NT = (((1,), (1,)), ((), ()))
TN = (((0,), (0,)), ((), ()))

BIG_LATE = ('w_out', 'mem_w_q', 'mem_w_kv', 'mem_w_o', 'mlp_w1', 'mlp_w2')
BIG_EARLY = ('w_in', 'ssm_w_glu', 'mla_w_uq', 'mla_w_ukv')
BIG = BIG_LATE + BIG_EARLY
BIG_AXIS = {'w_in': 1, 'ssm_w_glu': 1, 'mla_w_uq': 2, 'mla_w_ukv': 2, 'w_out': 1, 'mem_w_q': 1, 'mem_w_kv': 1,
            'mem_w_o': 2, 'mlp_w1': 2, 'mlp_w2': 1}
SMALL = ('norm_mix', 'ssm_lambda_re', 'ssm_lambda_im', 'ssm_log_step', 'ssm_b_re', 'ssm_b_im', 'ssm_c_re', 'ssm_c_im',
         'ssm_d', 'ssm_b_glu', 'mla_q_norm', 'mla_kv_norm', 'mla_q_gain', 'mla_k_gain', 'out_norm_ssm', 'out_norm_mla',
         'norm_mem_q', 'norm_mem_kv', 'mem_q_gain', 'mem_k_gain', 'norm_mlp')
WEIGHTS = ('norm_mix', 'w_in', 'ssm_lambda_re', 'ssm_lambda_im', 'ssm_log_step', 'ssm_b_re', 'ssm_b_im', 'ssm_c_re',
           'ssm_c_im', 'ssm_d', 'ssm_w_glu', 'ssm_b_glu', 'mla_q_norm', 'mla_w_uq', 'mla_kv_norm', 'mla_w_ukv',
           'mla_q_gain', 'mla_k_gain', 'out_norm_ssm', 'out_norm_mla', 'w_out', 'norm_mem_q', 'norm_mem_kv', 'mem_w_q',
           'mem_w_kv', 'mem_q_gain', 'mem_k_gain', 'mem_w_o', 'norm_mlp', 'mlp_w1', 'mlp_w2')


def _call(body, *, name, out_shape, grid=(), in_specs=None, out_specs=None, scratch=(), sem=None, vmem=None):
    params = {}
    if sem is not None:
        params['dimension_semantics'] = sem
    if vmem is not None:
        params['vmem_limit_bytes'] = vmem
    specs = {} if in_specs is None else dict(grid=grid, in_specs=in_specs, out_specs=out_specs)
    return pl.pallas_call(body, name=name, out_shape=out_shape, scratch_shapes=list(scratch),
                          compiler_params=pltpu.CompilerParams(**params), **specs)


def _sds(shape, dtype):
    return jax.ShapeDtypeStruct(shape, dtype)


def _dot(a, b, dims=NN):
    return lax.dot_general(a.astype(BF), b.astype(BF), dims, preferred_element_type=F32)


def _split(a):
    hi = a.astype(BF)
    return hi, (a - hi.astype(F32)).astype(BF)


def _dot3(a, b, dims=NN):
    ah, al = _split(a)
    bh, bl = _split(b)
    d = lambda p, q: lax.dot_general(p, q, dims, preferred_element_type=F32)
    return d(ah, bh) + (d(ah, bl) + d(al, bh))


_sdot = _dot


def _rms(x, n):
    r = lax.rsqrt(jnp.sum(x * x, axis=-1, keepdims=True) * (1.0 / n) + EPS)
    return x * r, r


def _rms_bwd(xhat, r, dxhat, n):
    return r * (dxhat - xhat * (jnp.sum(dxhat * xhat, axis=-1, keepdims=True) * (1.0 / n)))


def _colsum(a):
    return jnp.sum(a, axis=0, keepdims=True)


def _tile(t, want):
    return min(t, want)


def _bidx(nb):
    return (lambda b: b) if nb > 1 else (lambda b: 0)


def mm(a, b, mode, *, name, out_dtype=F32, tm=1024, tn=1024, slots=0):
    squeeze = a.ndim == 2 and b.ndim == 2
    a = a[None] if a.ndim == 2 else a
    b = b[None] if b.ndim == 2 else b
    nb = max(a.shape[0], b.shape[0])
    ab, bb = _bidx(a.shape[0]), _bidx(b.shape[0])
    if mode in ('nn', 'nt'):
        m, k = a.shape[1:]
        n = b.shape[2] if mode == 'nn' else b.shape[1]
        tm, tn = _tile(m, tm), _tile(n, tn)
        dims = NN if mode == 'nn' else NT

        def body(a_ref, b_ref, o_ref):
            o_ref[...] = _dot(a_ref[...], b_ref[...], dims).astype(o_ref.dtype)

        bspec = (pl.BlockSpec((None, k, tn), lambda bi, i, j: (bb(bi), 0, j)) if mode == 'nn'
                 else pl.BlockSpec((None, tn, k), lambda bi, i, j: (bb(bi), j, 0)))
        out = _call(body, name=name, grid=(nb, m // tm, n // tn),
                    in_specs=[pl.BlockSpec((None, tm, k), lambda bi, i, j: (ab(bi), i, 0)), bspec],
                    out_specs=pl.BlockSpec((None, tm, tn), lambda bi, i, j: (bi, i, j)),
                    out_shape=_sds((nb, m, n), out_dtype), sem=('parallel', 'parallel', 'parallel'), vmem=VMEM_BIG)(a, b)
    else:
        k, m = a.shape[1:]
        n = b.shape[2]
        tm, tn, tk = _tile(m, 1024), _tile(n, 1024), _tile(k, 1024)
        per = 1
        if slots:
            ts = n // slots
            per = tn // ts
            out_spec, out_shape = pl.BlockSpec((per, tm, ts), lambda bi, i, j, kk: (j, i, 0)), _sds((slots, m, ts), F32)
        else:
            out_spec, out_shape = pl.BlockSpec((None, tm, tn), lambda bi, i, j, kk: (bi, i, j)), _sds((nb, m, n), F32)

        def body(a_ref, b_ref, o_ref):
            @pl.when(pl.program_id(3) == 0)
            def _():
                o_ref[...] = jnp.zeros_like(o_ref)

            res = _dot(a_ref[...], b_ref[...], TN)
            if slots:
                for s in range(per):
                    o_ref[s] += res[:, s * ts:(s + 1) * ts]
            else:
                o_ref[...] += res

        out = _call(body, name=name, grid=(nb, m // tm, n // tn, k // tk),
                    in_specs=[pl.BlockSpec((None, tk, tm), lambda bi, i, j, kk: (ab(bi), kk, i)),
                              pl.BlockSpec((None, tk, tn), lambda bi, i, j, kk: (bb(bi), kk, j))],
                    out_specs=out_spec, out_shape=out_shape,
                    sem=('parallel', 'parallel', 'parallel', 'arbitrary'), vmem=VMEM_BIG)(a, b)
    return out[0] if squeeze and not slots else out


def rmsnorm_fwd(x, g, *, name, tq=512):
    t, d = x.shape
    tq = _tile(t, tq)

    def body(x_ref, g_ref, o_ref):
        xh, _ = _rms(x_ref[...], d)
        o_ref[...] = (xh * g_ref[...]).astype(o_ref.dtype)

    return _call(body, name=name, grid=(t // tq,),
                 in_specs=[pl.BlockSpec((tq, d), lambda i: (i, 0)), pl.BlockSpec((1, d), lambda i: (0, 0))],
                 out_specs=pl.BlockSpec((tq, d), lambda i: (i, 0)), out_shape=_sds((t, d), BF), sem=('parallel',))(x, g)


def rmsnorm_bwd(x, g, dh, dres, *, name, col=0, tq=512):
    t, d = x.shape
    tq = _tile(t, tq)
    has_res = dres is not None

    def body(*refs):
        if has_res:
            x_ref, g_ref, dh_ref, dres_ref, dx_ref, dxb_ref, dg_ref = refs
        else:
            x_ref, g_ref, dh_ref, dx_ref, dxb_ref, dg_ref = refs
        xh, r = _rms(x_ref[...], d)
        dh_ = dh_ref[...].astype(F32)
        dx = _rms_bwd(xh, r, dh_ * g_ref[...], d)
        if has_res:
            dx = dx + dres_ref[...]
        dx_ref[...] = dx
        dxb_ref[...] = dx.astype(BF)

        @pl.when(pl.program_id(0) == 0)
        def _():
            dg_ref[...] = jnp.zeros_like(dg_ref)

        dg_ref[...] += _colsum(dh_ * xh)

    in_specs = [pl.BlockSpec((tq, d), lambda i: (i, 0)), pl.BlockSpec((1, d), lambda i: (0, 0)),
                pl.BlockSpec((tq, d), lambda i: (i, col))]
    args = [x, g, dh]
    if has_res:
        in_specs.append(pl.BlockSpec((tq, d), lambda i: (i, 0)))
        args.append(dres)
    row = pl.BlockSpec((tq, d), lambda i: (i, 0))
    return _call(body, name=name, grid=(t // tq,), in_specs=in_specs,
                 out_specs=[row, row, pl.BlockSpec((1, d), lambda i: (0, 0))],
                 out_shape=[_sds((t, d), F32), _sds((t, d), BF), _sds((1, d), F32)], sem=('arbitrary',))(*args)


def _cmul(ar, ai, xr, xi):
    return ar * xr - ai * xi, ar * xi + ai * xr


def _seg_carries(er, ei, pr, pi, reverse):
    lw = er.shape[1]
    zero = jnp.zeros((1, lw), F32)
    order = range(SEGS - 1, -1, -1) if reverse else range(SEGS)
    cin_r, cin_i = [None] * SEGS, [None] * SEGS
    tr, ti = zero, zero
    for j in order:
        cin_r[j], cin_i[j] = tr, ti
        mr, mi = _cmul(pr, pi, tr, ti)
        tr, ti = er[j:j + 1, :] + mr, ei[j:j + 1, :] + mi
    return jnp.concatenate(cin_r, axis=0), jnp.concatenate(cin_i, axis=0)


def _s5_chunk(t):
    return _tile(t, 2048)


def s5_fwd(u_p, prm, *, name):
    t = u_p.shape[0]
    ch = _s5_chunk(t)
    nch, steps = t // ch, ch // SEGS
    lw = S5_LW

    def body(u_ref, ar_ref, ai_ref, pr_ref, pi_ref, bre_ref, bim_ref, cre_ref, cim_ref, d_ref, y_ref, bur, bui):
        hb = pl.program_id(0)
        ar = jnp.broadcast_to(ar_ref[0], (SEGS, lw))
        ai = jnp.broadcast_to(ai_ref[0], (SEGS, lw))

        def rows_of(c):
            return pl.ds(pl.multiple_of(c * ch, ch), ch)

        @pl.loop(0, nch)
        def _(c):
            u = u_ref[rows_of(c), :]
            bur[rows_of(c), :] = _sdot(u, bre_ref[0])
            bui[rows_of(c), :] = _sdot(u, bim_ref[0])

        def scan(carry, store):
            def step(i, s):
                r0 = pl.multiple_of(i * SEGS, SEGS)
                mr, mi = _cmul(ar, ai, s[0], s[1])
                nr, ni = mr + bur[pl.ds(r0, SEGS), :], mi + bui[pl.ds(r0, SEGS), :]
                if store:
                    bur[pl.ds(r0, SEGS), :] = nr
                    bui[pl.ds(r0, SEGS), :] = ni
                return nr, ni

            return lax.fori_loop(0, t // SEGS, step, carry, unroll=8)

        zero = jnp.zeros((SEGS, lw), F32)
        er, ei = scan((zero, zero), False)
        scan(_seg_carries(er, ei, pr_ref[0], pi_ref[0], False), True)

        @pl.loop(0, nch)
        def _(c):
            rows = rows_of(c)
            y = _sdot(bur[rows, :], cre_ref[0]) - _sdot(bui[rows, :], cim_ref[0])

            @pl.when(hb % 2 == 0)
            def _():
                y_ref[rows, :] = y + d_ref[...] * u_ref[rows, :]

            @pl.when(hb % 2 == 1)
            def _():
                y_ref[rows, :] += y

    vec = pl.BlockSpec((1, 1, lw), lambda h: (h, 0, 0))
    return _call(
        body, name=name, grid=(S5_NHB,),
        in_specs=[pl.BlockSpec((t, LANES), lambda h: (0, h // 2)), vec, vec, vec, vec,
                  pl.BlockSpec((1, LANES, lw), lambda h: (h, 0, 0)), pl.BlockSpec((1, LANES, lw), lambda h: (h, 0, 0)),
                  pl.BlockSpec((1, lw, LANES), lambda h: (h, 0, 0)), pl.BlockSpec((1, lw, LANES), lambda h: (h, 0, 0)),
                  pl.BlockSpec((1, LANES), lambda h: (0, h // 2))],
        out_specs=pl.BlockSpec((t, LANES), lambda h: (0, h // 2)), out_shape=_sds((t, SSM_W), F32),
        scratch=[pltpu.VMEM((t, lw), F32)] * 2, sem=('arbitrary',), vmem=VMEM_BIG,
    )(u_p, prm['ar'], prm['ai'], prm['pr'], prm['pi'], prm['bre'], prm['bim'], prm['cre'], prm['cim'], prm['d'])


def s5_bwd(u_p, dy_p, prm, *, name):
    t = u_p.shape[0]
    ch = _s5_chunk(t)
    nch, steps = t // ch, ch // SEGS
    lw = S5_LW

    def body(u_ref, dy_ref, ar_ref, ai_ref, pr_ref, pi_ref, bre_ref, bim_ref, cre_ref, cim_ref, d_ref,
             du_ref, dar_ref, dai_ref, dbre_ref, dbim_ref, dcre_ref, dcim_ref, dd_ref, bur, bui, sr, si, du_acc):
        hb = pl.program_id(0)
        ar = jnp.broadcast_to(ar_ref[0], (SEGS, lw))
        ai = jnp.broadcast_to(ai_ref[0], (SEGS, lw))
        zero = jnp.zeros((SEGS, lw), F32)

        def rows_of(c):
            return pl.ds(pl.multiple_of(c * ch, ch), ch)

        nsteps = t // SEGS

        @pl.loop(0, nch)
        def _(c):
            u = u_ref[rows_of(c), :]
            bur[rows_of(c), :] = _sdot(u, bre_ref[0])
            bui[rows_of(c), :] = _sdot(u, bim_ref[0])

        def fwd_scan(carry, store):
            def step(i, s):
                r0 = pl.multiple_of(i * SEGS, SEGS)
                mr, mi = _cmul(ar, ai, s[0], s[1])
                nr, ni = mr + bur[pl.ds(r0, SEGS), :], mi + bui[pl.ds(r0, SEGS), :]
                if store:
                    w0 = pl.multiple_of(i * SEGS + SEGS, SEGS)
                    sr[pl.ds(w0, SEGS), :] = nr
                    si[pl.ds(w0, SEGS), :] = ni
                return nr, ni

            return lax.fori_loop(0, nsteps, step, carry, unroll=8)

        er, ei = fwd_scan((zero, zero), False)
        cin_r, cin_i = _seg_carries(er, ei, pr_ref[0], pi_ref[0], False)
        sr[pl.ds(0, SEGS), :] = cin_r
        si[pl.ds(0, SEGS), :] = cin_i
        fwd_scan((cin_r, cin_i), True)

        @pl.loop(0, nch)
        def _(c):
            dy = dy_ref[rows_of(c), :]
            bur[rows_of(c), :] = _sdot(dy, cre_ref[0], NT)
            bui[rows_of(c), :] = -_sdot(dy, cim_ref[0], NT)

        def rev_local(ii, lam):
            r0 = pl.multiple_of((nsteps - 1 - ii) * SEGS, SEGS)
            mr, mi = _cmul(ar, -ai, lam[0], lam[1])
            return mr + bur[pl.ds(r0, SEGS), :], mi + bui[pl.ds(r0, SEGS), :]

        lr0, li0 = lax.fori_loop(0, nsteps, rev_local, (zero, zero), unroll=8)
        rin = _seg_carries(lr0, li0, pr_ref[0], -pi_ref[0], True)

        def rev_step(ii, st):
            lam_r, lam_i, acc_r, acc_i = st
            r0 = pl.multiple_of((nsteps - 1 - ii) * SEGS, SEGS)
            mr, mi = _cmul(ar, -ai, lam_r, lam_i)
            nr, ni = mr + bur[pl.ds(r0, SEGS), :], mi + bui[pl.ds(r0, SEGS), :]
            bur[pl.ds(r0, SEGS), :] = nr
            bui[pl.ds(r0, SEGS), :] = ni
            pr_, pi_ = sr[pl.ds(r0, SEGS), :], si[pl.ds(r0, SEGS), :]
            return nr, ni, acc_r + (nr * pr_ + ni * pi_), acc_i + (ni * pr_ - nr * pi_)

        _, _, acc_r, acc_i = lax.fori_loop(0, nsteps, rev_step, (rin[0], rin[1], zero, zero), unroll=8)
        dar_ref[0] = _colsum(acc_r)
        dai_ref[0] = _colsum(acc_i)

        dbre_ref[...] = jnp.zeros_like(dbre_ref)
        dbim_ref[...] = jnp.zeros_like(dbim_ref)
        dcre_ref[...] = jnp.zeros_like(dcre_ref)
        dcim_ref[...] = jnp.zeros_like(dcim_ref)

        @pl.loop(0, nch)
        def _(c):
            rows = rows_of(c)
            u = u_ref[rows, :]
            dy = dy_ref[rows, :]
            lam_r, lam_i = bur[rows, :], bui[rows, :]
            du = _sdot(lam_r, bre_ref[0], NT) + _sdot(lam_i, bim_ref[0], NT)

            @pl.when(hb % 2 == 0)
            def _():
                du_acc[rows, :] = du + d_ref[...] * dy

            @pl.when(hb % 2 == 1)
            def _():
                du_ref[rows, :] = (du_acc[rows, :] + du).astype(BF)

            dbre_ref[0] += _sdot(u, lam_r, TN)
            dbim_ref[0] += _sdot(u, lam_i, TN)
            srows = pl.ds(pl.multiple_of(c * ch + SEGS, SEGS), ch)
            dcre_ref[0] += _sdot(sr[srows, :], dy, TN)
            dcim_ref[0] -= _sdot(si[srows, :], dy, TN)

        @pl.when(hb % 2 == 0)
        def _():
            dd_ref[...] = _colsum(dy_ref[...] * u_ref[...])

    vec = pl.BlockSpec((1, 1, lw), lambda h: (h, 0, 0))
    bsp = pl.BlockSpec((1, LANES, lw), lambda h: (h, 0, 0))
    csp = pl.BlockSpec((1, lw, LANES), lambda h: (h, 0, 0))
    act = pl.BlockSpec((t, LANES), lambda h: (0, h // 2))
    dsp = pl.BlockSpec((1, LANES), lambda h: (0, h // 2))
    return _call(
        body, name=name, grid=(S5_NHB,),
        in_specs=[act, act, vec, vec, vec, vec, bsp, bsp, csp, csp, dsp],
        out_specs=[act, vec, vec, bsp, bsp, csp, csp, dsp],
        out_shape=[_sds((t, SSM_W), BF), _sds((S5_NHB, 1, lw), F32), _sds((S5_NHB, 1, lw), F32),
                   _sds((S5_NHB, LANES, lw), F32), _sds((S5_NHB, LANES, lw), F32),
                   _sds((S5_NHB, lw, LANES), F32), _sds((S5_NHB, lw, LANES), F32), _sds((1, SSM_W), F32)],
        scratch=[pltpu.VMEM((t, lw), F32), pltpu.VMEM((t, lw), F32),
                 pltpu.VMEM((t + SEGS, lw), F32), pltpu.VMEM((t + SEGS, lw), F32), pltpu.VMEM((t, LANES), F32)],
        sem=('arbitrary',), vmem=VMEM_BIG,
    )(u_p, dy_p, prm['ar'], prm['ai'], prm['pr'], prm['pi'], prm['bre'], prm['bim'], prm['cre'], prm['cim'], prm['d'])


def s5_prep(t, lam_re, lam_im, log_step, b_re, b_im, c_re, c_im):
    step = jnp.exp(log_step)[:, None]
    mag = jnp.exp(lam_re * step)
    ar, ai = mag * jnp.cos(lam_im * step), mag * jnp.sin(lam_im * step)
    den = lam_re * lam_re + lam_im * lam_im
    nr, ni = ar - 1.0, ai
    fr, fi = (nr * lam_re + ni * lam_im) / den, (ni * lam_re - nr * lam_im) / den
    bbr = fr[..., None] * b_re - fi[..., None] * b_im
    bbi = fr[..., None] * b_im + fi[..., None] * b_re
    gl = S5_LW // SSM_P
    eye = jnp.eye(gl, dtype=F32)
    half = (jnp.arange(S5_NHB) % 2)[:, None, None]

    def bmat(bb):
        x = bb.transpose(0, 2, 1).reshape(S5_NHB, gl, SSM_H, SSM_P)
        x = jnp.einsum('bghp,gk->bghkp', x, eye).reshape(S5_NHB, gl * SSM_H, S5_LW)
        z = jnp.zeros_like(x)
        return jnp.where(half == 0, jnp.concatenate([x, z], axis=1), jnp.concatenate([z, x], axis=1))

    def cmat(cc):
        x = cc.transpose(0, 2, 1).reshape(S5_NHB, gl, SSM_P, SSM_H)
        x = jnp.einsum('bgph,gk->bgpkh', x, eye).reshape(S5_NHB, S5_LW, gl * SSM_H)
        z = jnp.zeros_like(x)
        return jnp.where(half == 0, jnp.concatenate([x, z], axis=2), jnp.concatenate([z, x], axis=2))

    vec = lambda a: a.reshape(S5_NHB, 1, S5_LW)
    ni_steps = float(t // SEGS)
    pmag = jnp.exp(lam_re * step * ni_steps)
    pr, pi = pmag * jnp.cos(lam_im * step * ni_steps), pmag * jnp.sin(lam_im * step * ni_steps)
    return dict(ar=vec(ar), ai=vec(ai), bre=bmat(bbr), bim=bmat(bbi), cre=cmat(c_re), cim=cmat(c_im),
                pr=lax.stop_gradient(vec(pr)), pi=lax.stop_gradient(vec(pi)))


def _gelu(x):
    c = math.sqrt(2.0 / math.pi)
    return 0.5 * x * (1.0 + jnp.tanh(c * (x + 0.044715 * (x * x * x))))


def _gelu_grad(x):
    c = math.sqrt(2.0 / math.pi)
    th = jnp.tanh(c * (x + 0.044715 * (x * x * x)))
    return 0.5 * (1.0 + th) + 0.5 * x * (1.0 - th * th) * (c * (1.0 + 3.0 * 0.044715 * (x * x)))


def glu_fwd(ypre, w_glu, b_glu, *, name, tq=1024):
    t = ypre.shape[0]
    tq = _tile(t, tq)

    def body(y_ref, w_ref, b_ref, o_ref):
        yg = _gelu(y_ref[...])
        z = _dot(yg, w_ref[...]) + b_ref[...]
        o_ref[...] = yg * jax.nn.sigmoid(z)

    return _call(body, name=name, grid=(t // tq,),
                 in_specs=[pl.BlockSpec((tq, SSM_W), lambda i: (i, 0)), pl.BlockSpec((SSM_W, SSM_W), lambda i: (0, 0)),
                           pl.BlockSpec((1, SSM_W), lambda i: (0, 0))],
                 out_specs=pl.BlockSpec((tq, SSM_W), lambda i: (i, 0)), out_shape=_sds((t, SSM_W), F32),
                 sem=('parallel',))(ypre, w_glu, b_glu)


def glu_bwd(ypre, dy, w_glu, b_glu, *, name, tq=1024):
    t = ypre.shape[0]
    tq = _tile(t, tq)

    def body(y_ref, dy_ref, w_ref, b_ref, dyp_ref, yg_ref, dz_ref, db_ref):
        ypre_ = y_ref[...]
        yg = _gelu(ypre_)
        sig = jax.nn.sigmoid(_dot(yg, w_ref[...]) + b_ref[...])
        dy_ = dy_ref[...]
        dz = dy_ * yg * sig * (1.0 - sig)
        dyg = dy_ * sig + _dot(dz, w_ref[...], NT)
        dyp_ref[...] = dyg * _gelu_grad(ypre_)
        yg_ref[...] = yg.astype(BF)
        dz_ref[...] = dz.astype(BF)

        @pl.when(pl.program_id(0) == 0)
        def _():
            db_ref[...] = jnp.zeros_like(db_ref)

        db_ref[...] += _colsum(dz)

    row = pl.BlockSpec((tq, SSM_W), lambda i: (i, 0))
    vec = pl.BlockSpec((1, SSM_W), lambda i: (0, 0))
    return _call(body, name=name, grid=(t // tq,),
                 in_specs=[row, row, pl.BlockSpec((SSM_W, SSM_W), lambda i: (0, 0)), vec],
                 out_specs=[row, row, row, vec],
                 out_shape=[_sds((t, SSM_W), F32), _sds((t, SSM_W), BF), _sds((t, SSM_W), BF), _sds((1, SSM_W), F32)],
                 sem=('arbitrary',))(ypre, dy, w_glu, b_glu)


def _rope(x, cos, sa, sb):
    return x * cos + pltpu.roll(x, 16, 1) * sa + pltpu.roll(x, 112, 1) * sb


def _rope_t(d, cos, sa, sb):
    return d * cos + pltpu.roll(d * sa, 112, 1) + pltpu.roll(d * sb, 16, 1)


def rope_tables(positions):
    half = QK_ROPE // 2
    inv_freq = ROPE_THETA ** (-jnp.arange(half, dtype=F32) / half)
    ang = positions.astype(F32)[:, None] * inv_freq
    cos, sin = jnp.cos(ang), jnp.sin(ang)
    t = positions.shape[0]
    one, zero = jnp.ones((t, QK_NOPE), F32), jnp.zeros((t, QK_NOPE), F32)
    pad1, pad0 = jnp.ones((t, 32), F32), jnp.zeros((t, 32), F32)
    z16 = jnp.zeros((t, half), F32)
    return (jnp.concatenate([one, cos, cos, pad1], axis=1), jnp.concatenate([zero, z16, sin, pad0], axis=1),
            jnp.concatenate([zero, -sin, z16, pad0], axis=1))


def mla_prep_fwd(proj, tabs, w, *, name):
    t = proj.shape[0]
    tq = _tile(t, ATT_BLK)

    def body(cq_ref, ckv_ref, kr_ref, cos_ref, sa_ref, sb_ref, qn_ref, kvn_ref, wq_ref, wk_ref, wv_ref, qg_ref, kg_ref,
             q_ref, qt_ref, k_ref, kt_ref, v_ref):
        cqn = (_rms(cq_ref[...], Q_LORA)[0] * qn_ref[...]).astype(BF)
        ckvn = (_rms(ckv_ref[...], KV_LORA)[0] * kvn_ref[...]).astype(BF)
        cos, sa, sb = cos_ref[...], sa_ref[...], sb_ref[...]
        kr = kr_ref[...]
        for h in range(MLA_HEADS):
            q = _rms(_dot(cqn, wq_ref[h]), QK_DIM)[0] * qg_ref[...]
            q = _rope(q, cos, sa, sb) * ATT_SCALE
            q_ref[h] = q.astype(BF)
            qt_ref[h, 0] = q.T.astype(BF)
            k = _rms(_dot(ckvn, wk_ref[h]) + kr, QK_DIM)[0] * kg_ref[...]
            k = _rope(k, cos, sa, sb)
            k_ref[h] = k.astype(BF)
            kt_ref[h, 0] = k.T.astype(BF)
            v_ref[h] = _dot(ckvn, wv_ref[h]).astype(BF)

    tab = pl.BlockSpec((tq, LANES), lambda i: (i, 0))
    full = lambda shape: pl.BlockSpec(shape, lambda i: (0,) * len(shape))
    hout = pl.BlockSpec((MLA_HEADS, tq, LANES), lambda i: (0, i, 0))
    tout = pl.BlockSpec((MLA_HEADS, 1, LANES, tq), lambda i: (0, i, 0, 0))
    hshape = _sds((MLA_HEADS, t, LANES), BF)
    tshape = _sds((MLA_HEADS, t // tq, LANES, tq), BF)
    return _call(
        body, name=name, grid=(t // tq,),
        in_specs=[pl.BlockSpec((tq, Q_LORA), lambda i: (i, 2)), pl.BlockSpec((tq, LANES), lambda i: (i, 6)),
                  pl.BlockSpec((tq, LANES), lambda i: (i, 7)), tab, tab, tab,
                  full((1, Q_LORA)), full((1, KV_LORA)), full((MLA_HEADS, Q_LORA, LANES)),
                  full((MLA_HEADS, KV_LORA, LANES)), full((MLA_HEADS, KV_LORA, LANES)), full((1, LANES)), full((1, LANES))],
        out_specs=[hout, tout, hout, tout, hout], out_shape=[hshape, tshape, hshape, tshape, hshape], sem=('parallel',),
    )(proj, proj, proj, *tabs, w['q_norm'], w['kv_norm'], w['wq'], w['wk'], w['wv'], w['q_gain'], w['k_gain'])


def mla_prep_bwd(proj, tabs, w, dq, dk, dv, *, name):
    t = proj.shape[0]
    tq = _tile(t, ATT_BLK)

    def body(cq_ref, ckv_ref, kr_ref, cos_ref, sa_ref, sb_ref, qn_ref, kvn_ref, wq_ref, wk_ref, wv_ref, qg_ref, kg_ref,
             dq_ref, dk_ref, dv_ref,
             dpm_ref, cqn_ref, ckvn_ref, dqr_ref, dkraw_ref, dvb_ref, dqn_ref, dkvn_ref, dqg_ref, dkg_ref):
        cq_h, cq_r = _rms(cq_ref[...], Q_LORA)
        ckv_h, ckv_r = _rms(ckv_ref[...], KV_LORA)
        cqn = (cq_h * qn_ref[...]).astype(BF)
        ckvn = (ckv_h * kvn_ref[...]).astype(BF)
        cqn_ref[...] = cqn
        ckvn_ref[...] = ckvn
        cos, sa, sb = cos_ref[...], sa_ref[...], sb_ref[...]
        kr = kr_ref[...]
        dcqn = jnp.zeros((tq, Q_LORA), F32)
        dckvn = jnp.zeros((tq, KV_LORA), F32)
        dkrope = jnp.zeros((tq, LANES), F32)
        dqg = jnp.zeros((1, LANES), F32)
        dkg = jnp.zeros((1, LANES), F32)
        for h in range(MLA_HEADS):
            qh, qr = _rms(_dot(cqn, wq_ref[h]), QK_DIM)
            dqo = _rope_t(dq_ref[h, 0].T * ATT_SCALE, cos, sa, sb)
            dqg = dqg + _colsum(dqo * qh)
            dqraw = _rms_bwd(qh, qr, dqo * qg_ref[...], QK_DIM).astype(BF)
            dqr_ref[:, h * LANES:(h + 1) * LANES] = dqraw
            dcqn = dcqn + _dot(dqraw, wq_ref[h], NT)
            kh, krs = _rms(_dot(ckvn, wk_ref[h]) + kr, QK_DIM)
            dko = _rope_t(dk_ref[h], cos, sa, sb)
            dkg = dkg + _colsum(dko * kh)
            dkraw = _rms_bwd(kh, krs, dko * kg_ref[...], QK_DIM)
            dkrope = dkrope + dkraw
            dkraw = dkraw.astype(BF)
            dkraw_ref[:, h * LANES:(h + 1) * LANES] = dkraw
            dvb = dv_ref[h].astype(BF)
            dvb_ref[:, h * LANES:(h + 1) * LANES] = dvb
            dckvn = dckvn + _dot(dkraw, wk_ref[h], NT) + _dot(dvb, wv_ref[h], NT)
        dpm_ref[:, 0:Q_LORA] = _rms_bwd(cq_h, cq_r, dcqn * qn_ref[...], Q_LORA).astype(BF)
        dpm_ref[:, Q_LORA:Q_LORA + KV_LORA] = _rms_bwd(ckv_h, ckv_r, dckvn * kvn_ref[...], KV_LORA).astype(BF)
        dpm_ref[:, Q_LORA + KV_LORA:512] = dkrope.astype(BF)

        @pl.when(pl.program_id(0) == 0)
        def _():
            dqn_ref[...] = jnp.zeros_like(dqn_ref)
            dkvn_ref[...] = jnp.zeros_like(dkvn_ref)
            dqg_ref[...] = jnp.zeros_like(dqg_ref)
            dkg_ref[...] = jnp.zeros_like(dkg_ref)

        dqn_ref[...] += _colsum(dcqn * cq_h)
        dkvn_ref[...] += _colsum(dckvn * ckv_h)
        dqg_ref[...] += dqg
        dkg_ref[...] += dkg

    tab = pl.BlockSpec((tq, LANES), lambda i: (i, 0))
    full = lambda shape: pl.BlockSpec(shape, lambda i: (0,) * len(shape))
    hblk = pl.BlockSpec((MLA_HEADS, tq, LANES), lambda i: (0, i, 0))
    wide = pl.BlockSpec((tq, MLA_HEADS * LANES), lambda i: (i, 0))
    return _call(
        body, name=name, grid=(t // tq,),
        in_specs=[pl.BlockSpec((tq, Q_LORA), lambda i: (i, 2)), pl.BlockSpec((tq, LANES), lambda i: (i, 6)),
                  pl.BlockSpec((tq, LANES), lambda i: (i, 7)), tab, tab, tab,
                  full((1, Q_LORA)), full((1, KV_LORA)), full((MLA_HEADS, Q_LORA, LANES)),
                  full((MLA_HEADS, KV_LORA, LANES)), full((MLA_HEADS, KV_LORA, LANES)), full((1, LANES)), full((1, LANES)),
                  pl.BlockSpec((MLA_HEADS, 1, LANES, tq), lambda i: (0, i, 0, 0)), hblk, hblk],
        out_specs=[pl.BlockSpec((tq, 512), lambda i: (i, 0)),
                   pl.BlockSpec((tq, Q_LORA), lambda i: (i, 0)), pl.BlockSpec((tq, KV_LORA), lambda i: (i, 0)),
                   wide, wide, wide, full((1, Q_LORA)), full((1, KV_LORA)), full((1, LANES)), full((1, LANES))],
        out_shape=[_sds((t, 512), BF), _sds((t, Q_LORA), BF), _sds((t, KV_LORA), BF),
                   _sds((t, MLA_HEADS * LANES), BF), _sds((t, MLA_HEADS * LANES), BF), _sds((t, MLA_HEADS * LANES), BF),
                   _sds((1, Q_LORA), F32), _sds((1, KV_LORA), F32), _sds((1, LANES), F32), _sds((1, LANES), F32)],
        sem=('arbitrary',),
    )(proj, proj, proj, *tabs, w['q_norm'], w['kv_norm'], w['wq'], w['wk'], w['wv'], w['q_gain'], w['k_gain'], dq, dk, dv)


ATT_BLK = 256
ATT_SCALE = 1.0 / math.sqrt(QK_DIM)


def _overlapped(grid, make_copies):
    ids = [pl.program_id(a) for a in range(len(grid))]
    first = functools.reduce(jnp.logical_and, [i == 0 for i in ids])
    last = functools.reduce(jnp.logical_and, [i == n - 1 for i, n in zip(ids, grid)])

    @pl.when(first)
    def _():
        for cs in make_copies():
            _start_copies(cs)

    @pl.when(last)
    def _():
        for cs in make_copies():
            _wait_copies(cs)


def flash_fwd(q, kt, v, *, name, gather=()):
    t = q.shape[1]
    blk = _tile(t, ATT_BLK)
    grid = (MLA_HEADS // 2, t // blk)

    def body(q_ref, kt_ref, v_ref, *rest):
        nc = len(gather)
        srcs, (o_ref, lse_ref), dsts, sems = rest[:nc], rest[nc:nc + 2], rest[nc + 2:2 * nc + 2], rest[2 * nc + 2:]
        if nc:
            _overlapped(grid, lambda: [_copies('gather', srcs[i], dsts[i], *sems[3 * i:3 * i + 3]) for i in range(nc)])
        qi = pl.program_id(1)
        row = lax.broadcasted_iota(jnp.int32, (blk, blk), 0)
        col = lax.broadcasted_iota(jnp.int32, (blk, blk), 1)

        def block(j, carry, masked):
            out = []
            for hh in range(2):
                m, l, acc = carry[hh]
                s = _dot(q_ref[hh], kt_ref[hh, j])
                if masked:
                    s = jnp.where(col <= row, s, -jnp.inf)
                m2 = jnp.maximum(m, jnp.max(s, axis=-1, keepdims=True))
                p = jnp.exp(s - m2)
                alpha = jnp.exp(m - m2)
                rows = pl.ds(pl.multiple_of(j * blk, blk), blk)
                out.append((m2, alpha * l + jnp.sum(p, axis=-1, keepdims=True), alpha * acc + _dot(p, v_ref[hh, rows, :])))
            return tuple(out)

        init = (jnp.full((blk, 1), -jnp.inf, F32), jnp.zeros((blk, 1), F32), jnp.zeros((blk, LANES), F32))
        carry = lax.fori_loop(0, qi, lambda j, c: block(j, c, False), (init, init))
        carry = block(qi, carry, True)
        o_acc = jnp.zeros((blk, LANES), F32)
        for hh in range(2):
            m, l, acc = carry[hh]
            o_acc = o_acc + acc / l
            lse_ref[hh, 0] = jnp.broadcast_to(m + jnp.log(l), (blk, LANES)).T[0:1, :]
        o_ref[...] = o_acc

    in_specs = [pl.BlockSpec((2, blk, LANES), lambda p, i: (p, i, 0)),
                pl.BlockSpec((2, t // blk, LANES, blk), lambda p, i: (p, 0, 0, 0)),
                pl.BlockSpec((2, t, LANES), lambda p, i: (p, 0, 0))]
    out_specs = [pl.BlockSpec((blk, LANES), lambda p, i: (i, p)), pl.BlockSpec((2, 1, 1, blk), lambda p, i: (p, i, 0, 0))]
    out_shape = [_sds((t, 512), F32), _sds((MLA_HEADS, t // blk, 1, blk), F32)]
    nc = len(gather)
    return _call(body, name=name, grid=grid, in_specs=in_specs + [_ANY] * nc, out_specs=out_specs + [_ANY] * nc,
                 out_shape=out_shape + [_sds((NDEV,) + g.shape, g.dtype) for g in gather], scratch=_COMM_SCRATCH * nc,
                 sem=('arbitrary', 'arbitrary') if nc else ('parallel', 'parallel'))(q, kt, v, *gather)


def mla_out_bwd(o, dyn, g, *, name):
    t = o.shape[0]
    blk = _tile(t, ATT_BLK)

    def body(o_ref, dh_ref, g_ref, do_ref, dot_ref, delta_ref, dg_ref):
        ov = o_ref[...]
        oh, r = _rms(ov, 512)
        dh = dh_ref[...]
        do = _rms_bwd(oh, r, dh * g_ref[...], 512)
        do_ref[...] = do.astype(BF)
        dd = do * ov
        for pb in range(MLA_HEADS // 2):
            cols = slice(pb * LANES, (pb + 1) * LANES)
            dot_ref[pb, 0] = do[:, cols].T.astype(BF)
            ddt = dd[:, cols].T
            delta_ref[2 * pb, 0] = jnp.sum(ddt[0:V_DIM, :], axis=0, keepdims=True)
            delta_ref[2 * pb + 1, 0] = jnp.sum(ddt[V_DIM:LANES, :], axis=0, keepdims=True)

        @pl.when(pl.program_id(0) == 0)
        def _():
            dg_ref[...] = jnp.zeros_like(dg_ref)

        dg_ref[...] += _colsum(dh * oh)

    return _call(
        body, name=name, grid=(t // blk,),
        in_specs=[pl.BlockSpec((blk, 512), lambda i: (i, 0)), pl.BlockSpec((blk, 512), lambda i: (i, 1)),
                  pl.BlockSpec((1, 512), lambda i: (0, 0))],
        out_specs=[pl.BlockSpec((blk, 512), lambda i: (i, 0)), pl.BlockSpec((MLA_HEADS // 2, 1, LANES, blk), lambda i: (0, i, 0, 0)),
                   pl.BlockSpec((MLA_HEADS, 1, 1, blk), lambda i: (0, i, 0, 0)), pl.BlockSpec((1, 512), lambda i: (0, 0))],
        out_shape=[_sds((t, 512), BF), _sds((MLA_HEADS // 2, t // blk, LANES, blk), BF),
                   _sds((MLA_HEADS, t // blk, 1, blk), F32), _sds((1, 512), F32)],
        sem=('arbitrary',),
    )(o, dyn, g)


def flash_bwd(q, qt, k, kt, v, do, dot, lse, delta, *, name, scatter=()):
    t = q.shape[1]
    blk = _tile(t, ATT_BLK)
    nb = t // blk
    grid = (MLA_HEADS, nb)

    def body(q_ref, qt_ref, k_ref, kt_ref, v_ref, do_ref, dot_ref, lse_ref, delta_ref, *rest):
        nc = len(scatter)
        srcs, (dqt_ref, dk_ref, dv_ref), dsts, sems = rest[:nc], rest[nc:nc + 3], rest[nc + 3:2 * nc + 3], rest[2 * nc + 3:]
        if nc:
            _overlapped(grid, lambda: [_copies('scatter', srcs[i], dsts[i], *sems[3 * i:3 * i + 3]) for i in range(nc)])
        h, j = pl.program_id(0), pl.program_id(1)
        row = lax.broadcasted_iota(jnp.int32, (blk, blk), 0)
        col = lax.broadcasted_iota(jnp.int32, (blk, blk), 1)
        lane = lax.broadcasted_iota(jnp.int32, (1, LANES), 1)
        mine = (lane // V_DIM) == (h % 2)

        @pl.when(j == 0)
        def _():
            dqt_ref[...] = jnp.zeros_like(dqt_ref)

        kv, ktv, vv = k_ref[...], kt_ref[...], v_ref[...]

        def block(i, carry, masked):
            dk, dv = carry
            rows = pl.ds(pl.multiple_of(i * blk, blk), blk)
            pt = jnp.exp(_dot(kv, qt_ref[i]) - lse_ref[i])
            if masked:
                pt = jnp.where(col >= row, pt, 0.0)
            dv = dv + _dot(pt, do_ref[rows, :])
            dst = (pt * (_dot(vv, dot_ref[i]) - delta_ref[i])).astype(BF)
            dk = dk + _dot(dst, q_ref[rows, :])
            dqt_ref[i] += _dot(ktv, dst)
            return dk, dv

        zero = jnp.zeros((blk, LANES), F32)
        carry = block(j, (zero, zero), True)
        ngroups = (nb - 1 - j) // 3

        def group(p, c):
            i0 = j + 1 + 3 * p
            return block(i0 + 2, block(i0 + 1, block(i0, c, False), False), False)

        carry = lax.fori_loop(0, ngroups, group, carry)
        rest = j + 1 + 3 * ngroups
        npairs = (nb - rest) // 2
        carry = lax.fori_loop(0, npairs, lambda p, c: block(rest + 1, block(rest, c, False), False), carry)
        dk, dv = lax.fori_loop(rest + 2 * npairs, nb, lambda i, c: block(i, c, False), carry)
        dk_ref[...] = dk
        dv_ref[...] = jnp.where(mine, dv, 0.0)

    whole = pl.BlockSpec((None, t, LANES), lambda h, j: (h, 0, 0))
    wholet = pl.BlockSpec((None, nb, LANES, blk), lambda h, j: (h, 0, 0, 0))
    kvb = pl.BlockSpec((None, blk, LANES), lambda h, j: (h, j, 0))
    rowv = pl.BlockSpec((None, nb, 1, blk), lambda h, j: (h, 0, 0, 0))
    in_specs = [whole, wholet, kvb, pl.BlockSpec((None, None, LANES, blk), lambda h, j: (h, j, 0, 0)), kvb,
                pl.BlockSpec((t, LANES), lambda h, j: (0, h // 2)),
                pl.BlockSpec((None, nb, LANES, blk), lambda h, j: (h // 2, 0, 0, 0)), rowv, rowv]
    out_specs = [wholet, kvb, kvb]
    out_shape = [_sds((MLA_HEADS, nb, LANES, blk), F32), _sds((MLA_HEADS, t, LANES), F32), _sds((MLA_HEADS, t, LANES), F32)]
    args = (q, qt, k, kt, v, do, dot, lse, delta)
    nc = len(scatter)
    return _call(body, name=name, grid=grid, in_specs=in_specs + [_ANY] * nc, out_specs=out_specs + [_ANY] * nc,
                 out_shape=out_shape + [_sds(s.shape, s.dtype) for s in scatter], scratch=_COMM_SCRATCH * nc,
                 sem=('arbitrary', 'arbitrary') if nc else ('parallel', 'arbitrary'), vmem=VMEM_BIG)(*args, *scatter)


def mix_out_fwd(x, y_ssm, o, g_ssm, g_mla, w_out, *, name, tq=512):
    t = x.shape[0]
    tq = _tile(t, tq)

    def body(x_ref, ys_ref, o_ref, gs_ref, gm_ref, w_ref, x1_ref, yn_ref):
        ns = (_rms(ys_ref[...], SSM_W)[0] * gs_ref[...]).astype(BF)
        nm = (_rms(o_ref[...], 512)[0] * gm_ref[...]).astype(BF)
        yn_ref[:, 0:SSM_W] = ns
        yn_ref[:, SSM_W:D] = nm
        x1_ref[...] = x_ref[...] + _dot(ns, w_ref[0:SSM_W, :]) + _dot(nm, w_ref[SSM_W:D, :])

    row = lambda w: pl.BlockSpec((tq, w), lambda i: (i, 0))
    vec = pl.BlockSpec((1, 512), lambda i: (0, 0))
    return _call(body, name=name, grid=(t // tq,),
                 in_specs=[row(D), row(512), row(512), vec, vec, pl.BlockSpec((D, D), lambda i: (0, 0))],
                 out_specs=[row(D), row(D)], out_shape=[_sds((t, D), F32), _sds((t, D), BF)], sem=('parallel',),
                 )(x, y_ssm, o, g_ssm, g_mla, w_out)


MEM_SCALE = 1.0 / math.sqrt(MEM_HD)


def memkv_fwd(mem, g, wk, wv, kg, *, name):
    def body(m_ref, g_ref, wk_ref, wv_ref, kg_ref, mh_ref, k_ref, v_ref):
        mh = (_rms(m_ref[...], D)[0] * g_ref[...]).astype(BF)
        mh_ref[...] = mh
        for h in range(MEM_HEADS):
            cols = slice(h * LANES, (h + 1) * LANES)
            k_ref[h] = (_rms(_dot(mh, wk_ref[:, cols]), MEM_HD)[0] * kg_ref[...]).astype(BF)
            v_ref[h] = _dot(mh, wv_ref[:, cols]).astype(BF)

    return _call(body, name=name,
                 out_shape=[_sds((N_MEM, D), BF), _sds((MEM_HEADS, N_MEM, LANES), BF), _sds((MEM_HEADS, N_MEM, LANES), BF)],
                 )(mem, g, wk, wv, kg)


def memkv_bwd(mem, g, wk, wv, kg, dk, dv, *, name):
    def body(m_ref, g_ref, wk_ref, wv_ref, kg_ref, dk_ref, dv_ref, dwk_ref, dwv_ref, dkg_ref, dg_ref):
        mhat, _ = _rms(m_ref[...], D)
        mh = (mhat * g_ref[...]).astype(BF)
        lane = lax.broadcasted_iota(jnp.int32, (1, LANES), 1)
        dkg = jnp.zeros((1, LANES), F32)
        dmh = jnp.zeros((N_MEM, D), F32)
        for h in range(MEM_HEADS):
            cols = slice(h * LANES, (h + 1) * LANES)
            kh, kr = _rms(_dot(mh, wk_ref[:, cols]), MEM_HD)
            dko = dk_ref[h]
            dkg = dkg + _colsum(dko * kh)
            dkraw = _rms_bwd(kh, kr, dko * kg_ref[...], MEM_HD).astype(BF)
            dvh = jnp.where((lane // MEM_HD) == (h % 2), dv_ref[h], 0.0).astype(BF)
            dwk_ref[:, cols] = _dot(mh, dkraw, TN)
            dwv_ref[:, cols] = _dot(mh, dvh, TN)
            dmh = dmh + _dot(dkraw, wk_ref[:, cols], NT) + _dot(dvh, wv_ref[:, cols], NT)
        dkg_ref[...] = dkg
        dg_ref[...] = _colsum(dmh * mhat)

    return _call(body, name=name,
                 out_shape=[_sds((D, 512), F32), _sds((D, 512), F32), _sds((1, LANES), F32), _sds((1, D), F32)],
                 )(mem, g, wk, wv, kg, dk, dv)


def memattn_fwd(x, g, wq, qg, kh, vh, wo, g_next, *, name, tq=512):
    t = x.shape[0]
    tq = _tile(t, tq)

    def body(x_ref, g_ref, wq_ref, qg_ref, k_ref, v_ref, wo_ref, gn_ref, x2_ref, hn_ref, h3_ref):
        xv = x_ref[...]
        hn = (_rms(xv, D)[0] * g_ref[...]).astype(BF)
        hn_ref[...] = hn
        out = xv
        for pb in range(MEM_HEADS // 2):
            o = jnp.zeros((tq, LANES), F32)
            for h in (2 * pb, 2 * pb + 1):
                q = _rms(_dot(hn, wq_ref[:, h * LANES:(h + 1) * LANES]), MEM_HD)[0] * qg_ref[...]
                s = _dot(q, k_ref[h], NT) * MEM_SCALE
                p = jnp.exp(s - jnp.max(s, axis=-1, keepdims=True))
                p = p / jnp.sum(p, axis=-1, keepdims=True)
                o = o + _dot(p, v_ref[h])
            out = out + _dot(o, wo_ref[pb * LANES:(pb + 1) * LANES, :])
        x2_ref[...] = out
        h3_ref[...] = (_rms(out, D)[0] * gn_ref[...]).astype(BF)

    full = lambda shape: pl.BlockSpec(shape, lambda i: (0,) * len(shape))
    row = pl.BlockSpec((tq, D), lambda i: (i, 0))
    return _call(body, name=name, grid=(t // tq,),
                 in_specs=[row, full((1, D)), full((D, 512)), full((1, LANES)), full((MEM_HEADS, N_MEM, LANES)),
                           full((MEM_HEADS, N_MEM, LANES)), full((MEM_HEADS * MEM_HD, D)), full((1, D))],
                 out_specs=[row, row, row], out_shape=[_sds((t, D), F32), _sds((t, D), BF), _sds((t, D), BF)],
                 sem=('parallel',))(x, g, wq, qg, kh, vh, wo, g_next)


def memattn_bwd(x, dx2, g, wq, qg, kh, vh, wo, *, name, tq=512):
    t = x.shape[0]
    tq = _tile(t, tq)

    def body(x_ref, dx2_ref, g_ref, wq_ref, qg_ref, k_ref, v_ref, wo_ref,
             dx_ref, dxb_ref, o_ref, dqr_ref, dk_ref, dv_ref, dqg_ref, dg_ref):
        @pl.when(pl.program_id(0) == 0)
        def _():
            dk_ref[...] = jnp.zeros_like(dk_ref)
            dv_ref[...] = jnp.zeros_like(dv_ref)
            dqg_ref[...] = jnp.zeros_like(dqg_ref)
            dg_ref[...] = jnp.zeros_like(dg_ref)

        xhat, xr = _rms(x_ref[...], D)
        hn = (xhat * g_ref[...]).astype(BF)
        dx2 = dx2_ref[...]
        dx2b = dx2.astype(BF)
        dh = jnp.zeros((tq, D), F32)
        dqg = jnp.zeros((1, LANES), F32)
        for pb in range(MEM_HEADS // 2):
            do = _dot(dx2b, wo_ref[pb * LANES:(pb + 1) * LANES, :], NT).astype(BF)
            o = jnp.zeros((tq, LANES), F32)
            for h in (2 * pb, 2 * pb + 1):
                cols = slice(h * LANES, (h + 1) * LANES)
                qh, qr = _rms(_dot(hn, wq_ref[:, cols]), MEM_HD)
                qb = (qh * qg_ref[...]).astype(BF)
                s = _dot(qb, k_ref[h], NT) * MEM_SCALE
                p = jnp.exp(s - jnp.max(s, axis=-1, keepdims=True))
                p = p / jnp.sum(p, axis=-1, keepdims=True)
                pb16 = p.astype(BF)
                o = o + _dot(pb16, v_ref[h])
                dv_ref[h] += _dot(pb16, do, TN)
                dp = _dot(do, v_ref[h], NT)
                ds = (p * (dp - jnp.sum(dp * p, axis=-1, keepdims=True)) * MEM_SCALE).astype(BF)
                dk_ref[h] += _dot(ds, qb, TN)
                dqo = _dot(ds, k_ref[h])
                dqg = dqg + _colsum(dqo * qh)
                dqraw = _rms_bwd(qh, qr, dqo * qg_ref[...], MEM_HD).astype(BF)
                dqr_ref[:, cols] = dqraw
                dh = dh + _dot(dqraw, wq_ref[:, cols], NT)
            o_ref[:, pb * LANES:(pb + 1) * LANES] = o.astype(BF)
        dx = dx2 + _rms_bwd(xhat, xr, dh * g_ref[...], D)
        dx_ref[...] = dx
        dxb_ref[...] = dx.astype(BF)
        dqg_ref[...] += dqg
        dg_ref[...] += _colsum(dh * xhat)

    full = lambda shape: pl.BlockSpec(shape, lambda i: (0,) * len(shape))
    row = lambda w: pl.BlockSpec((tq, w), lambda i: (i, 0))
    return _call(body, name=name, grid=(t // tq,),
                 in_specs=[row(D), row(D), full((1, D)), full((D, 512)), full((1, LANES)), full((MEM_HEADS, N_MEM, LANES)),
                           full((MEM_HEADS, N_MEM, LANES)), full((MEM_HEADS * MEM_HD, D))],
                 out_specs=[row(D), row(D), row(256), row(512), full((MEM_HEADS, N_MEM, LANES)), full((MEM_HEADS, N_MEM, LANES)),
                            full((1, LANES)), full((1, D))],
                 out_shape=[_sds((t, D), F32), _sds((t, D), BF), _sds((t, 256), BF), _sds((t, 512), BF),
                            _sds((MEM_HEADS, N_MEM, LANES), F32), _sds((MEM_HEADS, N_MEM, LANES), F32),
                            _sds((1, LANES), F32), _sds((1, D), F32)],
                 sem=('arbitrary',))(x, dx2, g, wq, qg, kh, vh, wo)


def mlp_fwd(x, h, w1, w2, g_next, *, name, tq=1024, tf=512):
    t = x.shape[0]
    tq = _tile(t, tq)
    nf = D_FF // tf

    def body(x_ref, h_ref, w1_ref, w2_ref, *rest):
        o_ref = rest[-2] if g_next is not None else rest[-1]

        @pl.when(pl.program_id(1) == 0)
        def _():
            o_ref[...] = x_ref[...]

        a = jnp.maximum(_dot(h_ref[...], w1_ref[...]), 0.0)
        o_ref[...] += _dot(a * a, w2_ref[...])
        if g_next is not None:
            g_ref, hn_ref = rest[0], rest[-1]

            @pl.when(pl.program_id(1) == nf - 1)
            def _():
                hn_ref[...] = (_rms(o_ref[...], D)[0] * g_ref[...]).astype(BF)

    row = pl.BlockSpec((tq, D), lambda i, f: (i, 0))
    in_specs = [row, row, pl.BlockSpec((None, D, tf), lambda i, f: (f, 0, 0)), pl.BlockSpec((tf, D), lambda i, f: (f, 0))]
    if g_next is None:
        return _call(body, name=name, grid=(t // tq, nf), in_specs=in_specs, out_specs=row, out_shape=_sds((t, D), F32),
                     sem=('parallel', 'arbitrary'), vmem=VMEM_BIG)(x, h, w1, w2), None
    return _call(body, name=name, grid=(t // tq, nf), in_specs=in_specs + [pl.BlockSpec((1, D), lambda i, f: (0, 0))],
                 out_specs=[row, row], out_shape=[_sds((t, D), F32), _sds((t, D), BF)],
                 sem=('parallel', 'arbitrary'), vmem=VMEM_BIG)(x, h, w1, w2, g_next)


def mlp_bwd(h, dx, w1, w2, *, name, tq=1024, tf=512):
    t = h.shape[0]
    tq = _tile(t, tq)

    def body(h_ref, dx_ref, w1_ref, w2_ref, dh_ref, r_ref, da_ref):
        @pl.when(pl.program_id(1) == 0)
        def _():
            dh_ref[...] = jnp.zeros_like(dh_ref)

        a = jnp.maximum(_dot(h_ref[...], w1_ref[...]), 0.0)
        r_ref[...] = (a * a).astype(BF)
        da = (_dot(dx_ref[...], w2_ref[...], NT) * (2.0 * a)).astype(BF)
        da_ref[...] = da
        dh_ref[...] += _dot(da, w1_ref[...], NT)

    row = pl.BlockSpec((tq, D), lambda i, f: (i, 0))
    act = pl.BlockSpec((tq, tf), lambda i, f: (i, f))
    return _call(body, name=name, grid=(t // tq, D_FF // tf),
                 in_specs=[row, row, pl.BlockSpec((None, D, tf), lambda i, f: (f, 0, 0)), pl.BlockSpec((tf, D), lambda i, f: (f, 0))],
                 out_specs=[row, act, act], out_shape=[_sds((t, D), F32), _sds((t, D_FF), BF), _sds((t, D_FF), BF)],
                 sem=('parallel', 'arbitrary'), vmem=VMEM_BIG)(h, dx, w1, w2)


def loss_fwd_bwd(y, target, *, name, tq=512):
    t = y.shape[0]
    tq = _tile(t, tq)

    def body(y_ref, t_ref, dy_ref, dyb_ref, l_ref):
        @pl.when(pl.program_id(0) == 0)
        def _():
            l_ref[...] = jnp.zeros_like(l_ref)

        e = y_ref[...] - t_ref[...]
        dy = e * (1.0 / D)
        dy_ref[...] = dy
        dyb_ref[...] = dy.astype(BF)
        l_ref[...] += _colsum(e * e) * (0.5 / D)

    row = pl.BlockSpec((tq, D), lambda i: (i, 0))
    return _call(body, name=name, grid=(t // tq,), in_specs=[row, row],
                 out_specs=[row, row, pl.BlockSpec((1, D), lambda i: (0, 0))],
                 out_shape=[_sds((t, D), F32), _sds((t, D), BF), _sds((1, D), F32)], sem=('arbitrary',))(y, target)


def prep_early(w):
    w_in = w['w_in']
    z = lambda r, c: jnp.zeros((r, c), w_in.dtype)
    w_in_pad = jnp.concatenate([w_in[:, :896], z(D, 64), w_in[:, 896:928], z(D, 32)], axis=1)
    wq = w['mla_w_uq'].reshape(Q_LORA, MLA_HEADS, QK_DIM).transpose(1, 0, 2)
    wq = jnp.pad(wq, ((0, 0), (0, 0), (0, LANES - QK_DIM)))
    ukv = w['mla_w_ukv'].reshape(KV_LORA, MLA_HEADS, QK_NOPE + V_DIM).transpose(1, 0, 2)
    wk = jnp.pad(ukv[:, :, :QK_NOPE], ((0, 0), (0, 0), (0, LANES - QK_NOPE)))
    vpart = ukv[:, :, QK_NOPE:]
    zv = jnp.zeros_like(vpart)
    odd = (jnp.arange(MLA_HEADS) % 2)[:, None, None] == 1
    wv = jnp.where(odd, jnp.concatenate([zv, vpart], axis=2), jnp.concatenate([vpart, zv], axis=2))
    return dict(w_in=w_in_pad, w_glu=w['ssm_w_glu'], wq=wq, wk=wk, wv=wv)


def prep_late(w):
    mq = jnp.pad(w['mem_w_q'].reshape(D, MEM_HEADS, MEM_HD), ((0, 0), (0, 0), (0, LANES - MEM_HD))).reshape(D, 512)
    mkv = w['mem_w_kv'].reshape(D, MEM_HEADS, 2 * MEM_HD)
    mk = jnp.pad(mkv[:, :, :MEM_HD], ((0, 0), (0, 0), (0, LANES - MEM_HD))).reshape(D, 512)
    mvp = mkv[:, :, MEM_HD:]
    zm = jnp.zeros_like(mvp)
    modd = (jnp.arange(MEM_HEADS) % 2)[None, :, None] == 1
    mv = jnp.where(modd, jnp.concatenate([zm, mvp], axis=2), jnp.concatenate([mvp, zm], axis=2)).reshape(D, 512)
    return dict(w_out=w['w_out'], mq=mq, mk=mk, mv=mv, mo=w['mem_w_o'], w1=w['mlp_w1'], w2=w['mlp_w2'])


def prep_small(t, s):
    row = lambda a: a.reshape(1, -1)
    pad = lambda a: jnp.pad(a, (0, LANES - a.shape[0])).reshape(1, LANES)
    out = s5_prep(t, s['ssm_lambda_re'], s['ssm_lambda_im'], s['ssm_log_step'], s['ssm_b_re'], s['ssm_b_im'],
                  s['ssm_c_re'], s['ssm_c_im'])
    out.update(d=row(s['ssm_d']), norm_mix=row(s['norm_mix']), b_glu=row(s['ssm_b_glu']), q_norm=row(s['mla_q_norm']),
               kv_norm=row(s['mla_kv_norm']), q_gain=pad(s['mla_q_gain']), k_gain=pad(s['mla_k_gain']),
               g_ssm=row(s['out_norm_ssm']), g_mla=row(s['out_norm_mla']), norm_mem_q=row(s['norm_mem_q']),
               norm_mem_kv=row(s['norm_mem_kv']), mem_q_gain=pad(s['mem_q_gain']), mem_k_gain=pad(s['mem_k_gain']),
               norm_mlp=row(s['norm_mlp']))
    return out


def _perm(a):
    t, c = a.shape
    return a.reshape(SEGS, t // SEGS, c).transpose(1, 0, 2).reshape(t, c)


def _unperm(a):
    t, c = a.shape
    return a.reshape(t // SEGS, SEGS, c).transpose(1, 0, 2).reshape(t, c)


def layer_fwd(l, x, h1, mem, tabs, plan, ws, g_next):
    n = lambda s: f'l{l}_{s}'
    wb = prep_early(plan.early(l))
    if h1 is None:
        h1 = rmsnorm_fwd(x, ws['norm_mix'], name=n('norm_mix'))
    proj = mm(h1, wb['w_in'], 'nn', name=n('w_in'))
    u_p = _perm(proj[:, :SSM_W])
    ypre_p = s5_fwd(u_p, ws, name=n('s5'))
    ypre = _unperm(ypre_p)
    y_ssm = glu_fwd(ypre, wb['w_glu'], ws['b_glu'], name=n('glu'))
    mw = dict(q_norm=ws['q_norm'], kv_norm=ws['kv_norm'], wq=wb['wq'], wk=wb['wk'], wv=wb['wv'],
              q_gain=ws['q_gain'], k_gain=ws['k_gain'])
    q, qt, k, kt, v = mla_prep_fwd(proj, tabs, mw, name=n('mla_prep'))
    o, lse, *gathered = flash_fwd(q, kt, v, name=n('flash'), gather=plan.gather_src(l))
    plan.gathered(l, gathered)
    wb.update(prep_late(plan.late(l)))
    x1, yn = mix_out_fwd(x, y_ssm, o, ws['g_ssm'], ws['g_mla'], wb['w_out'], name=n('mix_out'))
    mh, kh, vh = memkv_fwd(mem, ws['norm_mem_kv'], wb['mk'], wb['mv'], ws['mem_k_gain'], name=n('memkv'))
    x2, h2, h3 = memattn_fwd(x1, ws['norm_mem_q'], wb['mq'], ws['mem_q_gain'], kh, vh, wb['mo'], ws['norm_mlp'],
                             name=n('memattn'))
    x3, h1_next = mlp_fwd(x2, h3, wb['w1'], wb['w2'], g_next, name=n('mlp'))
    saved = dict(x=x, h1=h1, proj=proj, u_p=u_p, ypre=ypre, y_ssm=y_ssm, q=q, qt=qt, k=k, kt=kt, v=v, o=o, lse=lse, x1=x1, yn=yn,
                 kh=kh, vh=vh, x2=x2, h2=h2, h3=h3, mw=mw)
    return x3, h1_next, wb, saved


def layer_bwd(l, dx3, dx3b, mem, tabs, plan, wb, ws, sv):
    n = lambda s: f'l{l}_{s}_bwd'
    gb, gs = {}, {}
    structs = lambda names: {k: _sds(plan.shapes[k], F32) for k in names}
    dh3, r, da = mlp_bwd(sv['h3'], dx3b, wb['w1'], wb['w2'], name=n('mlp'))
    gb['w1'] = mm(sv['h3'], da, 'tn', name=n('w1'), slots=NDEV)
    gb['w2'] = mm(r, dx3b, 'tn', name=n('w2'))
    dx2, dx2b, gs['norm_mlp'] = rmsnorm_bwd(sv['x2'], ws['norm_mlp'], dh3, dx3, name=n('norm_mlp'))
    dx1, dx1b, o_mem, dqr_mem, dkh, dvh, gs['mem_q_gain'], gs['norm_mem_q'] = memattn_bwd(
        sv['x1'], dx2, ws['norm_mem_q'], wb['mq'], ws['mem_q_gain'], sv['kh'], sv['vh'], wb['mo'], name=n('memattn'))
    gb['mo'] = mm(o_mem, dx2b, 'tn', name=n('mo'))
    gb['mq'] = mm(sv['h2'], dqr_mem, 'tn', name=n('mq'))
    gb['mk'], gb['mv'], gs['mem_k_gain'], gs['norm_mem_kv'] = memkv_bwd(
        mem, ws['norm_mem_kv'], wb['mk'], wb['mv'], ws['mem_k_gain'], dkh, dvh, name=n('memkv'))
    dyn = mm(dx1b, wb['w_out'], 'nt', name=n('w_out_dx'))
    gb['w_out'] = mm(sv['yn'], dx1b, 'tn', name=n('w_out'))
    dy_ssm, _, gs['g_ssm'] = rmsnorm_bwd(sv['y_ssm'], ws['g_ssm'], dyn, None, name=n('out_norm_ssm'), col=0)
    do, dot, delta, gs['g_mla'] = mla_out_bwd(sv['o'], dyn, ws['g_mla'], name=n('out_norm_mla'))
    late = {k: gb.pop(k) for k in ('w_out', 'mq', 'mk', 'mv', 'mo', 'w1', 'w2')}
    plan.late_grads(l, jax.linear_transpose(prep_late, structs(BIG_LATE))(late)[0])
    dq, dk, dv, *received = flash_bwd(sv['q'], sv['qt'], sv['k'], sv['kt'], sv['v'], do, dot, sv['lse'], delta,
                                      name=n('flash'), scatter=plan.scatter_src(l))
    plan.scattered(l, received)
    (dproj_m, cqn, ckvn, dqr, dkr, dvb, gs['q_norm'], gs['kv_norm'], gs['q_gain'], gs['k_gain']) = mla_prep_bwd(
        sv['proj'], tabs, sv['mw'], dq, dk, dv, name=n('mla_prep'))
    by_head = lambda g: g.reshape(g.shape[0], MLA_HEADS, LANES).transpose(1, 0, 2)
    gb['wq'] = by_head(mm(cqn, dqr, 'tn', name=n('wq')))
    gb['wk'] = by_head(mm(ckvn, dkr, 'tn', name=n('wk')))
    gb['wv'] = by_head(mm(ckvn, dvb, 'tn', name=n('wv')))
    dypre, yg, dz, gs['b_glu'] = glu_bwd(sv['ypre'], dy_ssm, wb['w_glu'], ws['b_glu'], name=n('glu'))
    gb['w_glu'] = mm(yg, dz, 'tn', name=n('w_glu'))
    du_p, gs['ar'], gs['ai'], gs['bre'], gs['bim'], gs['cre'], gs['cim'], gs['d'] = s5_bwd(sv['u_p'], _perm(dypre), ws, name=n('s5'))
    dprojb = jnp.concatenate([_unperm(du_p), dproj_m], axis=1)
    dh1 = mm(dprojb, wb['w_in'], 'nt', name=n('w_in_dx'))
    gb['w_in'] = mm(sv['h1'], dprojb, 'tn', name=n('w_in'))
    dx0, dx0b, gs['norm_mix'] = rmsnorm_bwd(sv['x'], ws['norm_mix'], dh1, dx1, name=n('norm_mix'))
    plan.early_grads(l, jax.linear_transpose(prep_early, structs(BIG_EARLY))(gb)[0])
    return dx0, dx0b, gs


def local_step(x, mem, positions, target, small, plan):
    t = x.shape[0]
    tabs = rope_tables(positions)
    preps = [jax.vjp(functools.partial(prep_small, t), {k: small[k][l] for k in SMALL}) for l in range(DEPTH)]
    layers, h1 = [], None
    for l in range(DEPTH):
        ws, small_vjp = preps[l]
        g_next = preps[l + 1][0]['norm_mix'] if l + 1 < DEPTH else None
        x, h1, wb, sv = layer_fwd(l, x, h1, mem, tabs, plan, ws, g_next)
        layers.append((wb, ws, small_vjp, sv))
    dx, dxb, lcols = loss_fwd_bwd(x, target, name='loss')
    loss = jnp.sum(lcols)
    gsmall = [None] * DEPTH
    for l in reversed(range(DEPTH)):
        wb, ws, small_vjp, sv = layers[l]
        dx, dxb, gs = layer_bwd(l, dx, dxb, mem, tabs, plan, wb, ws, sv)
        gs['pr'], gs['pi'] = jnp.zeros_like(ws['pr']), jnp.zeros_like(ws['pi'])
        gsmall[l] = small_vjp(gs)[0]
    return loss, dx, gsmall


class ExchangePlan:
    def __init__(self, shard_shapes, mine, first_early):
        self.shapes = {k: (s[1] * (NDEV if BIG_AXIS[k] == 1 else 1), s[2] * (NDEV if BIG_AXIS[k] == 2 else 1))
                       for k, s in shard_shapes.items()}
        self.shard = {k: s[1:] for k, s in shard_shapes.items()}
        self.shapes['mlp_w1'] = (NDEV,) + self.shard['mlp_w1']
        self.mine = mine
        self.w_early = {0: first_early}
        self.w_late = {}
        self.g_late, self.g_early = {}, {}
        self.r_late, self.r_early = {}, {}

    def _unpack(self, g, names):
        out, r0 = {}, 0
        for k in names:
            nr = math.prod(self.shard[k]) // D
            s = g[:, r0:r0 + nr]
            out[k] = (s.reshape(self.shapes[k]) if k == 'mlp_w1'
                      else _from_slots(s.reshape(NDEV, -1), (1,) + self.shard[k], BIG_AXIS[k])[0])
            r0 += nr
        return out

    def _pack(self, g, names, rows):
        slots = jnp.concatenate([g[k].reshape(NDEV, -1) if k == 'mlp_w1' else _to_slots(g[k][None], BIG_AXIS[k])
                                 for k in names], axis=1)
        return jnp.pad(slots, ((0, 0), (0, rows * D - slots.shape[1]))).astype(BF).reshape(NDEV, rows, D)

    def early(self, l):
        return self._unpack(self.w_early.pop(l), BIG_EARLY)

    def late(self, l):
        return self._unpack(self.w_late.pop(l), BIG_LATE)

    def gather_src(self, l):
        src = [self.mine[l, :LATE_ROWS]]
        if l + 1 < DEPTH:
            src.append(self.mine[l + 1, LATE_ROWS:])
        return tuple(src)

    def gathered(self, l, res):
        self.w_late[l] = res[0]
        if l + 1 < DEPTH:
            self.w_early[l + 1] = res[1]

    def late_grads(self, l, g):
        self.g_late[l] = self._pack(g, BIG_LATE, LATE_ROWS)

    def early_grads(self, l, g):
        self.g_early[l] = self._pack(g, BIG_EARLY, LAYER_ROWS - LATE_ROWS)

    def scatter_src(self, l):
        src = [self.g_late.pop(l)]
        if l + 1 < DEPTH:
            src.append(self.g_early.pop(l + 1))
        return tuple(src)

    def scattered(self, l, res):
        self.r_late[l] = res[0]
        if l + 1 < DEPTH:
            self.r_early[l + 1] = res[1]


def _peer(k):
    x, y, c = lax.axis_index('x'), lax.axis_index('y'), lax.axis_index('c')
    px, py, pc = x ^ ((k >> 2) & 1), y ^ ((k >> 1) & 1), c ^ (k & 1)
    return (px, py, pc), 4 * px + 2 * py + pc


def _copies(kind, src_ref, dst_ref, send_sems, recv_sems, loc_sem):
    _, me = _peer(0)
    src = (lambda p: src_ref.at[p]) if kind == 'scatter' else (lambda p: src_ref)
    local = pltpu.make_async_copy(src(me), dst_ref.at[me], loc_sem)
    sends, recvs = [], []
    for k in range(1, NDEV):
        dev, p = _peer(k)
        for slot, lst in ((me, sends), (p, recvs)):
            lst.append(pltpu.make_async_remote_copy(src_ref=src(p), dst_ref=dst_ref.at[slot], send_sem=send_sems.at[k - 1],
                                                    recv_sem=recv_sems.at[k - 1], device_id=dev,
                                                    device_id_type=pl.DeviceIdType.MESH))
    return local, sends, recvs


def _start_copies(cs):
    local, sends, _ = cs
    local.start()
    for cp in sends:
        cp.start()


def _wait_copies(cs):
    local, sends, recvs = cs
    for cp in sends:
        cp.wait_send()
    for cp in recvs:
        cp.wait_recv()
    local.wait()


_COMM_SCRATCH = (pltpu.SemaphoreType.DMA((NDEV - 1,)), pltpu.SemaphoreType.DMA((NDEV - 1,)), pltpu.SemaphoreType.DMA(()))
_ANY = pl.BlockSpec(memory_space=pl.ANY)


def exchange(scatters, gathers, *, name):
    ins = list(scatters) + list(gathers)
    kinds = ['scatter'] * len(scatters) + ['gather'] * len(gathers)
    n_in = len(ins)
    outs = [_sds(a.shape, a.dtype) for a in scatters] + [_sds((NDEV,) + b.shape, b.dtype) for b in gathers]

    def body(*refs):
        in_refs, out_refs, sems = refs[:n_in], refs[n_in:2 * n_in], refs[2 * n_in:]
        sets = [_copies(kind, in_refs[i], out_refs[i], *sems[3 * i:3 * i + 3]) for i, kind in enumerate(kinds)]
        for cs in sets:
            _start_copies(cs)
        for cs in sets:
            _wait_copies(cs)

    return pl.pallas_call(body, name=name, in_specs=[_ANY] * n_in, out_specs=[_ANY] * n_in, out_shape=outs,
                          scratch_shapes=list(_COMM_SCRATCH * n_in))(*ins)


def adamw(w, m, v, g8, *, name, tr):
    r = w.shape[0]
    c1 = 1.0 / (1.0 - ADAM_B1 ** ADAM_STEP)
    c2 = 1.0 / (1.0 - ADAM_B2 ** ADAM_STEP)

    def body(w_ref, m_ref, v_ref, g_ref, go_ref, d_ref, mo_ref, vo_ref):
        g = g_ref[0].astype(F32)
        for i in range(1, NDEV):
            g = g + g_ref[i].astype(F32)
        m_new = ADAM_B1 * m_ref[...] + (1.0 - ADAM_B1) * g
        v_new = ADAM_B2 * v_ref[...] + (1.0 - ADAM_B2) * (g * g)
        go_ref[...] = g
        mo_ref[...] = m_new
        vo_ref[...] = v_new
        d_ref[...] = -ADAM_LR * ((m_new * c1) / (jnp.sqrt(v_new * c2) + ADAM_EPS) + ADAM_WD * w_ref[...])

    row = pl.BlockSpec((tr, D), lambda i: (i, 0))
    return _call(body, name=name, grid=(r // tr,),
                 in_specs=[row, row, row, pl.BlockSpec((NDEV, tr, D), lambda i: (0, i, 0))],
                 out_specs=[row] * 4, out_shape=[_sds((r, D), F32)] * 4, sem=('parallel',), vmem=VMEM_BIG)(w, m, v, g8)


def _flat_rows(parts, rows):
    flat = jnp.concatenate([p.reshape(-1) for p in parts])
    return jnp.pad(flat, (0, rows * D - flat.shape[0])).reshape(rows, D)


def _unflat(flat2d, shapes):
    flat = flat2d.reshape(-1)
    out, off = [], 0
    for s in shapes:
        n = math.prod(s)
        out.append(flat[off:off + n].reshape(s))
        off += n
    return out


def _to_slots(g, axis):
    l, r, c = g.shape
    if axis == 1:
        return g.reshape(l, NDEV, r // NDEV, c).transpose(1, 0, 2, 3).reshape(NDEV, -1)
    return g.reshape(l, r, NDEV, c // NDEV).transpose(2, 0, 1, 3).reshape(NDEV, -1)


def _from_slots(s, shard_shape, axis):
    l, r, c = shard_shape
    s = s.reshape(NDEV, l, r, c)
    if axis == 1:
        return s.transpose(1, 0, 2, 3).reshape(l, NDEV * r, c)
    return s.transpose(1, 2, 0, 3).reshape(l, r, NDEV * c)


LATE_ROWS = 1280
LAYER_ROWS = 1536
BIG_ROWS = DEPTH * LAYER_ROWS
SMALL_ROWS = 640


def kernel(x, mem, positions, norm_mix, w_in, ssm_lambda_re, ssm_lambda_im, ssm_log_step, ssm_b_re, ssm_b_im, ssm_c_re, ssm_c_im, ssm_d, ssm_w_glu, ssm_b_glu, mla_q_norm, mla_w_uq, mla_kv_norm, mla_w_ukv, mla_q_gain, mla_k_gain, out_norm_ssm, out_norm_mla, w_out, norm_mem_q, norm_mem_kv, mem_w_q, mem_w_kv, mem_q_gain, mem_k_gain, mem_w_o, norm_mlp, mlp_w1, mlp_w2, loss_target, m_norm_mix, m_w_in, m_ssm_lambda_re, m_ssm_lambda_im, m_ssm_log_step, m_ssm_b_re, m_ssm_b_im, m_ssm_c_re, m_ssm_c_im, m_ssm_d, m_ssm_w_glu, m_ssm_b_glu, m_mla_q_norm, m_mla_w_uq, m_mla_kv_norm, m_mla_w_ukv, m_mla_q_gain, m_mla_k_gain, m_out_norm_ssm, m_out_norm_mla, m_w_out, m_norm_mem_q, m_norm_mem_kv, m_mem_w_q, m_mem_w_kv, m_mem_q_gain, m_mem_k_gain, m_mem_w_o, m_norm_mlp, m_mlp_w1, m_mlp_w2, v_norm_mix, v_w_in, v_ssm_lambda_re, v_ssm_lambda_im, v_ssm_log_step, v_ssm_b_re, v_ssm_b_im, v_ssm_c_re, v_ssm_c_im, v_ssm_d, v_ssm_w_glu, v_ssm_b_glu, v_mla_q_norm, v_mla_w_uq, v_mla_kv_norm, v_mla_w_ukv, v_mla_q_gain, v_mla_k_gain, v_out_norm_ssm, v_out_norm_mla, v_w_out, v_norm_mem_q, v_norm_mem_kv, v_mem_w_q, v_mem_w_kv, v_mem_q_gain, v_mem_k_gain, v_mem_w_o, v_norm_mlp, v_mlp_w1, v_mlp_w2):
    wvals = (norm_mix, w_in, ssm_lambda_re, ssm_lambda_im, ssm_log_step, ssm_b_re, ssm_b_im, ssm_c_re, ssm_c_im, ssm_d, ssm_w_glu, ssm_b_glu, mla_q_norm, mla_w_uq, mla_kv_norm, mla_w_ukv, mla_q_gain, mla_k_gain, out_norm_ssm, out_norm_mla, w_out, norm_mem_q, norm_mem_kv, mem_w_q, mem_w_kv, mem_q_gain, mem_k_gain, mem_w_o, norm_mlp, mlp_w1, mlp_w2)
    mvals = (m_norm_mix, m_w_in, m_ssm_lambda_re, m_ssm_lambda_im, m_ssm_log_step, m_ssm_b_re, m_ssm_b_im, m_ssm_c_re, m_ssm_c_im, m_ssm_d, m_ssm_w_glu, m_ssm_b_glu, m_mla_q_norm, m_mla_w_uq, m_mla_kv_norm, m_mla_w_ukv, m_mla_q_gain, m_mla_k_gain, m_out_norm_ssm, m_out_norm_mla, m_w_out, m_norm_mem_q, m_norm_mem_kv, m_mem_w_q, m_mem_w_kv, m_mem_q_gain, m_mem_k_gain, m_mem_w_o, m_norm_mlp, m_mlp_w1, m_mlp_w2)
    vvals = (v_norm_mix, v_w_in, v_ssm_lambda_re, v_ssm_lambda_im, v_ssm_log_step, v_ssm_b_re, v_ssm_b_im, v_ssm_c_re, v_ssm_c_im, v_ssm_d, v_ssm_w_glu, v_ssm_b_glu, v_mla_q_norm, v_mla_w_uq, v_mla_kv_norm, v_mla_w_ukv, v_mla_q_gain, v_mla_k_gain, v_out_norm_ssm, v_out_norm_mla, v_w_out, v_norm_mem_q, v_norm_mem_kv, v_mem_w_q, v_mem_w_kv, v_mem_q_gain, v_mem_k_gain, v_mem_w_o, v_norm_mlp, v_mlp_w1, v_mlp_w2)
    w = dict(zip(WEIGHTS, wvals))
    m = dict(zip(WEIGHTS, mvals))
    v = dict(zip(WEIGHTS, vvals))

    shard_shapes = {k: w[k].shape for k in BIG}
    layer_shapes = [shard_shapes[k][1:] for k in BIG]

    def layer_flat(parts):
        flat = jnp.concatenate([p.reshape(DEPTH, -1) for p in parts], axis=1)
        return jnp.pad(flat, ((0, 0), (0, LAYER_ROWS * D - flat.shape[1]))).reshape(DEPTH, LAYER_ROWS, D)

    mine = layer_flat([w[k].astype(BF) for k in BIG])
    first, = exchange([], [mine[0, LATE_ROWS:]], name='gather_early0')
    plan = ExchangePlan(shard_shapes, mine, first)
    small = {k: w[k] for k in SMALL}
    loss, grad_x, gsmall = local_step(x[0], mem[0], positions[0], loss_target[0], small, plan)
    gs_full = [jnp.stack([gsmall[l][k] for l in range(DEPTH)]) for k in SMALL]
    small_flat = _flat_rows(gs_full, SMALL_ROWS).astype(BF)
    plan.r_early[0], g8_small, losses = exchange([plan.g_early.pop(0)], [small_flat, jnp.full((8, LANES), loss, F32)],
                                                 name='exchange_last')
    loss_all = jnp.sum(losses[:, 0, 0])
    g8_big = jnp.concatenate([r[l] for l in range(DEPTH) for r in (plan.r_late, plan.r_early)], axis=1)

    small_shapes = [w[k].shape for k in SMALL]
    flat_big = lambda d: layer_flat([d[k] for k in BIG]).reshape(BIG_ROWS, D)
    gb, db, mb, vb = adamw(flat_big(w), flat_big(m), flat_big(v), g8_big, name='adamw_big', tr=256)
    gs, ds, ms, vs = adamw(_flat_rows([w[k] for k in SMALL], SMALL_ROWS), _flat_rows([m[k] for k in SMALL], SMALL_ROWS),
                           _flat_rows([v[k] for k in SMALL], SMALL_ROWS), g8_small, name='adamw_small', tr=128)

    def unflat_big(fb):
        fb, out, r0 = fb.reshape(DEPTH, LAYER_ROWS, D), [], 0
        for k, shp in zip(BIG, layer_shapes):
            nr = math.prod(shp) // D
            out.append(fb[:, r0:r0 + nr].reshape(shard_shapes[k]))
            r0 += nr
        return out

    res = {}
    for tag, fb, fs in (('g', gb, gs), ('d', db, ds), ('m', mb, ms), ('v', vb, vs)):
        res[tag] = dict(zip(BIG, unflat_big(fb)))
        res[tag].update(zip(SMALL, _unflat(fs, small_shapes)))
    return (loss_all, grad_x[None], *[res['g'][k] for k in WEIGHTS], *[res['d'][k] for k in WEIGHTS],
            *[res['m'][k] for k in WEIGHTS], *[res['v'][k] for k in WEIGHTS])
```
